```python
import math
import jax, jax.numpy as jnp
from jax import lax
import numpy as np

D_MODEL = 1024
BATCH = 32
SEQ = 2048
DEPTH = 2

D_MIX = D_MODEL
D_A = D_MIX // 2
D_B = D_MIX - D_A
N_HEADS_A = 8
HEAD_DIM_A = D_A // N_HEADS_A
N_GROUPS_B = 8
CHUNK = 128
CONV_WIDTH = 3
D_FF = 2816
N_MOD = 9
D_IN_PROJ = 2 * D_A + 3 * D_B
EPS = 1e-6

kernel_name = "hybrid_sgu_shortconv_macaron_adaln"


def rms_norm(x, g):
    xf = x.astype(jnp.float32)
    y = xf * lax.rsqrt(jnp.mean(xf * xf, axis=-1, keepdims=True) + EPS)
    return (y * g.astype(jnp.float32)).astype(x.dtype)


def layer_norm(x, g, b):
    xf = x.astype(jnp.float32)
    mu = jnp.mean(xf, axis=-1, keepdims=True)
    var = jnp.mean(jnp.square(xf - mu), axis=-1, keepdims=True)
    y = (xf - mu) * lax.rsqrt(var + EPS)
    return (y * g.astype(jnp.float32) + b.astype(jnp.float32)).astype(x.dtype)


def modulate(h, shift, scale):
    return h * (1 + scale[:, None, :]) + shift[:, None, :]


def swiglu_ffn(h, w_gu, w_down):
    gu = jnp.einsum('bsd,df->bsf', h, w_gu)
    g, u = jnp.split(gu, 2, axis=-1)
    return jnp.einsum('bsf,fd->bsd', jax.nn.silu(g) * u, w_down)


def chunked_sgu(u, v, ln_g, ln_b, w_s, b_s):
    bsz, s, _ = v.shape
    n_chunks = s // CHUNK
    v = layer_norm(v.reshape(bsz, s, N_HEADS_A, HEAD_DIM_A), ln_g, ln_b)
    v = v.reshape(bsz, n_chunks, CHUNK, N_HEADS_A, HEAD_DIM_A)
    causal = jnp.tril(jnp.ones((CHUNK, CHUNK), dtype=bool))
    w_masked = jnp.where(causal[None], w_s, jnp.zeros_like(w_s))
    mixed = jnp.einsum('hts,bcshd->bcthd', w_masked, v)
    mixed = mixed + jnp.transpose(b_s)[None, None, :, :, None]
    return u * mixed.reshape(bsz, s, D_A)


def short_gated_conv(b_gate, c_gate, xb, conv_w):
    s = xb.shape[1]
    z = c_gate * xb
    zp = jnp.pad(z, ((0, 0), (CONV_WIDTH - 1, 0), (0, 0)))
    conv = zp[:, 0:s] * conv_w[0] + zp[:, 1:s + 1] * conv_w[1] + zp[:, 2:s + 2] * conv_w[2]
    return b_gate * conv


def _fwd_setup_inputs(seed: int = 0) -> dict:
    key = jax.random.key(seed)
    ks = jax.random.split(key, 24)

    def nrm(k, shape, scale):
        return scale * jax.random.normal(k, shape, jnp.float32)

    def gain(k, shape):
        return 1.0 + 0.02 * jax.random.normal(k, shape, jnp.float32)

    return {
        "x": nrm(ks[0], (BATCH, SEQ, D_MODEL), 1.0),
        "c": nrm(ks[1], (BATCH, D_MODEL), 1.0),
        "ada_w": nrm(ks[2], (DEPTH, D_MODEL, N_MOD * D_MODEL), 0.5 * D_MODEL ** -0.5),
        "ada_b": nrm(ks[3], (DEPTH, N_MOD * D_MODEL), 0.01),
        "norm_ffn1_g": gain(ks[4], (DEPTH, D_MODEL)),
        "ffn1_w_gu": nrm(ks[5], (DEPTH, D_MODEL, 2 * D_FF), D_MODEL ** -0.5),
        "ffn1_w_down": nrm(ks[6], (DEPTH, D_FF, D_MODEL), D_FF ** -0.5),
        "norm_mix_g": gain(ks[7], (DEPTH, D_MODEL)),
        "mix_w_in": nrm(ks[8], (DEPTH, D_MODEL, D_IN_PROJ), D_MODEL ** -0.5),
        "sgu_ln_g": gain(ks[9], (DEPTH, HEAD_DIM_A)),
        "sgu_ln_b": nrm(ks[10], (DEPTH, HEAD_DIM_A), 0.02),
        "sgu_w_s": nrm(ks[11], (DEPTH, N_HEADS_A, CHUNK, CHUNK), CHUNK ** -0.5),
        "sgu_b": gain(ks[12], (DEPTH, N_HEADS_A, CHUNK)),
        "conv_w": nrm(ks[13], (DEPTH, CONV_WIDTH, D_B), CONV_WIDTH ** -0.5),
        "out_norm_g": gain(ks[14], (DEPTH, D_MIX)),
        "mix_w_out": nrm(ks[15], (DEPTH, D_MIX, D_MODEL), D_MIX ** -0.5),
        "norm_ffn2_g": gain(ks[16], (DEPTH, D_MODEL)),
        "ffn2_w_gu": nrm(ks[17], (DEPTH, D_MODEL, 2 * D_FF), D_MODEL ** -0.5),
        "ffn2_w_down": nrm(ks[18], (DEPTH, D_FF, D_MODEL), D_FF ** -0.5),
        "final_norm_g": gain(ks[19], (D_MODEL,)),
    }


def _fwd_reference(x, c, ada_w, ada_b, norm_ffn1_g, ffn1_w_gu, ffn1_w_down, norm_mix_g,
              mix_w_in, sgu_ln_g, sgu_ln_b, sgu_w_s, sgu_b, conv_w, out_norm_g,
              mix_w_out, norm_ffn2_g, ffn2_w_gu, ffn2_w_down, final_norm_g):
    c_act = jax.nn.silu(c)
    split_points = [D_A, 2 * D_A, 2 * D_A + D_B, 2 * D_A + 2 * D_B]
    for l in range(DEPTH):
        ada = jnp.einsum('bd,de->be', c_act, ada_w[l]) + ada_b[l]
        (sh1, sc1, g1, sh2, sc2, g2, sh3, sc3, g3) = jnp.split(ada, N_MOD, axis=-1)

        h = modulate(rms_norm(x, norm_ffn1_g[l]), sh1, sc1)
        x = x + 0.5 * g1[:, None, :] * swiglu_ffn(h, ffn1_w_gu[l], ffn1_w_down[l])

        h = modulate(rms_norm(x, norm_mix_g[l]), sh2, sc2)
        proj = jnp.einsum('bsd,de->bse', h, mix_w_in[l])
        u_a, v_a, b_gate, c_gate, xb = jnp.split(proj, split_points, axis=-1)
        y_a = chunked_sgu(jax.nn.gelu(u_a, approximate=False), jax.nn.gelu(v_a, approximate=False),
                          sgu_ln_g[l], sgu_ln_b[l], sgu_w_s[l], sgu_b[l])
        y_b = short_gated_conv(b_gate, c_gate, xb, conv_w[l])
        y = jnp.concatenate([rms_norm(y_a, out_norm_g[l, :D_A]),
                             rms_norm(y_b, out_norm_g[l, D_A:])], axis=-1)
        x = x + g2[:, None, :] * jnp.einsum('bse,ed->bsd', y, mix_w_out[l])

        h = modulate(rms_norm(x, norm_ffn2_g[l]), sh3, sc3)
        x = x + 0.5 * g3[:, None, :] * swiglu_ffn(h, ffn2_w_gu[l], ffn2_w_down[l])
    return rms_norm(x, final_norm_g)


import jax as _jax
import jax.numpy as _jnp

TWIN_FORMAT = 'train_step'
FWD_PARAMS = ['x', 'c', 'ada_w', 'ada_b', 'norm_ffn1_g', 'ffn1_w_gu', 'ffn1_w_down', 'norm_mix_g', 'mix_w_in', 'sgu_ln_g', 'sgu_ln_b', 'sgu_w_s', 'sgu_b', 'conv_w', 'out_norm_g', 'mix_w_out', 'norm_ffn2_g', 'ffn2_w_gu', 'ffn2_w_down', 'final_norm_g']
TWIN_WEIGHTS = ['ada_w', 'ada_b', 'norm_ffn1_g', 'ffn1_w_gu', 'ffn1_w_down', 'norm_mix_g', 'mix_w_in', 'sgu_ln_g', 'sgu_ln_b', 'sgu_w_s', 'sgu_b', 'conv_w', 'out_norm_g', 'mix_w_out', 'norm_ffn2_g', 'ffn2_w_gu', 'ffn2_w_down', 'final_norm_g']
TWIN_DIFF_INPUT = 'x'
TWIN_INPUTS = ['x', 'c', 'ada_w', 'ada_b', 'norm_ffn1_g', 'ffn1_w_gu', 'ffn1_w_down', 'norm_mix_g', 'mix_w_in', 'sgu_ln_g', 'sgu_ln_b', 'sgu_w_s', 'sgu_b', 'conv_w', 'out_norm_g', 'mix_w_out', 'norm_ffn2_g', 'ffn2_w_gu', 'ffn2_w_down', 'final_norm_g', 'loss_target', 'm_ada_w', 'm_ada_b', 'm_norm_ffn1_g', 'm_ffn1_w_gu', 'm_ffn1_w_down', 'm_norm_mix_g', 'm_mix_w_in', 'm_sgu_ln_g', 'm_sgu_ln_b', 'm_sgu_w_s', 'm_sgu_b', 'm_conv_w', 'm_out_norm_g', 'm_mix_w_out', 'm_norm_ffn2_g', 'm_ffn2_w_gu', 'm_ffn2_w_down', 'm_final_norm_g', 'v_ada_w', 'v_ada_b', 'v_norm_ffn1_g', 'v_ffn1_w_gu', 'v_ffn1_w_down', 'v_norm_mix_g', 'v_mix_w_in', 'v_sgu_ln_g', 'v_sgu_ln_b', 'v_sgu_w_s', 'v_sgu_b', 'v_conv_w', 'v_out_norm_g', 'v_mix_w_out', 'v_norm_ffn2_g', 'v_ffn2_w_gu', 'v_ffn2_w_down', 'v_final_norm_g']
TWIN_OUTPUTS = ['loss', 'grad_x', 'grad_ada_w', 'grad_ada_b', 'grad_norm_ffn1_g', 'grad_ffn1_w_gu', 'grad_ffn1_w_down', 'grad_norm_mix_g', 'grad_mix_w_in', 'grad_sgu_ln_g', 'grad_sgu_ln_b', 'grad_sgu_w_s', 'grad_sgu_b', 'grad_conv_w', 'grad_out_norm_g', 'grad_mix_w_out', 'grad_norm_ffn2_g', 'grad_ffn2_w_gu', 'grad_ffn2_w_down', 'grad_final_norm_g', 'delta_ada_w', 'delta_ada_b', 'delta_norm_ffn1_g', 'delta_ffn1_w_gu', 'delta_ffn1_w_down', 'delta_norm_mix_g', 'delta_mix_w_in', 'delta_sgu_ln_g', 'delta_sgu_ln_b', 'delta_sgu_w_s', 'delta_sgu_b', 'delta_conv_w', 'delta_out_norm_g', 'delta_mix_w_out', 'delta_norm_ffn2_g', 'delta_ffn2_w_gu', 'delta_ffn2_w_down', 'delta_final_norm_g', 'new_m_ada_w', 'new_m_ada_b', 'new_m_norm_ffn1_g', 'new_m_ffn1_w_gu', 'new_m_ffn1_w_down', 'new_m_norm_mix_g', 'new_m_mix_w_in', 'new_m_sgu_ln_g', 'new_m_sgu_ln_b', 'new_m_sgu_w_s', 'new_m_sgu_b', 'new_m_conv_w', 'new_m_out_norm_g', 'new_m_mix_w_out', 'new_m_norm_ffn2_g', 'new_m_ffn2_w_gu', 'new_m_ffn2_w_down', 'new_m_final_norm_g', 'new_v_ada_w', 'new_v_ada_b', 'new_v_norm_ffn1_g', 'new_v_ffn1_w_gu', 'new_v_ffn1_w_down', 'new_v_norm_mix_g', 'new_v_mix_w_in', 'new_v_sgu_ln_g', 'new_v_sgu_ln_b', 'new_v_sgu_w_s', 'new_v_sgu_b', 'new_v_conv_w', 'new_v_out_norm_g', 'new_v_mix_w_out', 'new_v_norm_ffn2_g', 'new_v_ffn2_w_gu', 'new_v_ffn2_w_down', 'new_v_final_norm_g']
TWIN_LEAF_KINDS = {'loss': 'loss', 'grad_x': 'grad_x', 'grad_ada_w': 'grad_w', 'grad_ada_b': 'grad_w', 'grad_norm_ffn1_g': 'grad_w', 'grad_ffn1_w_gu': 'grad_w', 'grad_ffn1_w_down': 'grad_w', 'grad_norm_mix_g': 'grad_w', 'grad_mix_w_in': 'grad_w', 'grad_sgu_ln_g': 'grad_w', 'grad_sgu_ln_b': 'grad_w', 'grad_sgu_w_s': 'grad_w', 'grad_sgu_b': 'grad_w', 'grad_conv_w': 'grad_w', 'grad_out_norm_g': 'grad_w', 'grad_mix_w_out': 'grad_w', 'grad_norm_ffn2_g': 'grad_w', 'grad_ffn2_w_gu': 'grad_w', 'grad_ffn2_w_down': 'grad_w', 'grad_final_norm_g': 'grad_w', 'delta_ada_w': 'delta_w', 'delta_ada_b': 'delta_w', 'delta_norm_ffn1_g': 'delta_w', 'delta_ffn1_w_gu': 'delta_w', 'delta_ffn1_w_down': 'delta_w', 'delta_norm_mix_g': 'delta_w', 'delta_mix_w_in': 'delta_w', 'delta_sgu_ln_g': 'delta_w', 'delta_sgu_ln_b': 'delta_w', 'delta_sgu_w_s': 'delta_w', 'delta_sgu_b': 'delta_w', 'delta_conv_w': 'delta_w', 'delta_out_norm_g': 'delta_w', 'delta_mix_w_out': 'delta_w', 'delta_norm_ffn2_g': 'delta_w', 'delta_ffn2_w_gu': 'delta_w', 'delta_ffn2_w_down': 'delta_w', 'delta_final_norm_g': 'delta_w', 'new_m_ada_w': 'new_m', 'new_m_ada_b': 'new_m', 'new_m_norm_ffn1_g': 'new_m', 'new_m_ffn1_w_gu': 'new_m', 'new_m_ffn1_w_down': 'new_m', 'new_m_norm_mix_g': 'new_m', 'new_m_mix_w_in': 'new_m', 'new_m_sgu_ln_g': 'new_m', 'new_m_sgu_ln_b': 'new_m', 'new_m_sgu_w_s': 'new_m', 'new_m_sgu_b': 'new_m', 'new_m_conv_w': 'new_m', 'new_m_out_norm_g': 'new_m', 'new_m_mix_w_out': 'new_m', 'new_m_norm_ffn2_g': 'new_m', 'new_m_ffn2_w_gu': 'new_m', 'new_m_ffn2_w_down': 'new_m', 'new_m_final_norm_g': 'new_m', 'new_v_ada_w': 'new_v', 'new_v_ada_b': 'new_v', 'new_v_norm_ffn1_g': 'new_v', 'new_v_ffn1_w_gu': 'new_v', 'new_v_ffn1_w_down': 'new_v', 'new_v_norm_mix_g': 'new_v', 'new_v_mix_w_in': 'new_v', 'new_v_sgu_ln_g': 'new_v', 'new_v_sgu_ln_b': 'new_v', 'new_v_sgu_w_s': 'new_v', 'new_v_sgu_b': 'new_v', 'new_v_conv_w': 'new_v', 'new_v_out_norm_g': 'new_v', 'new_v_mix_w_out': 'new_v', 'new_v_norm_ffn2_g': 'new_v', 'new_v_ffn2_w_gu': 'new_v', 'new_v_ffn2_w_down': 'new_v', 'new_v_final_norm_g': 'new_v'}


def _forward(args):
    return _fwd_reference(*[args[k] for k in FWD_PARAMS])


def _output_shape():
    out = _jax.eval_shape(lambda: _forward(_fwd_setup_inputs(0)))
    return out.shape, out.dtype

N_MICROBATCH = 1
ADAM_LR = 0.001
ADAM_B1 = 0.9
ADAM_B2 = 0.999
ADAM_EPS = 1e-08
ADAM_WD = 0.01
ADAM_STEP = 10
PER_EXAMPLE_BATCH_AXIS = {'x': 0, 'c': 0, 'loss_target': 0}
SHARED_INPUTS = []
_WEIGHT_DTYPES = {'ada_w': _jnp.float32, 'ada_b': _jnp.float32, 'norm_ffn1_g': _jnp.float32, 'ffn1_w_gu': _jnp.float32, 'ffn1_w_down': _jnp.float32, 'norm_mix_g': _jnp.float32, 'mix_w_in': _jnp.float32, 'sgu_ln_g': _jnp.float32, 'sgu_ln_b': _jnp.float32, 'sgu_w_s': _jnp.float32, 'sgu_b': _jnp.float32, 'conv_w': _jnp.float32, 'out_norm_g': _jnp.float32, 'mix_w_out': _jnp.float32, 'norm_ffn2_g': _jnp.float32, 'ffn2_w_gu': _jnp.float32, 'ffn2_w_down': _jnp.float32, 'final_norm_g': _jnp.float32}
MOMENT_SCALE = {'ada_w': 6.919208e-02, 'ada_b': 1.105591e-01, 'norm_ffn1_g': 4.080853e-02, 'ffn1_w_gu': 1.770929e-02, 'ffn1_w_down': 2.887092e-02, 'norm_mix_g': 1.059891e-01, 'mix_w_in': 7.039974e-02, 'sgu_ln_g': 1.128207e-01, 'sgu_ln_b': 1.169421e-01, 'sgu_w_s': 2.886494e-02, 'sgu_b': 4.166875e-02, 'conv_w': 7.437074e-02, 'out_norm_g': 7.905488e-02, 'mix_w_out': 7.444377e-02, 'norm_ffn2_g': 3.714771e-02, 'ffn2_w_gu': 1.626633e-02, 'ffn2_w_down': 2.646392e-02, 'final_norm_g': 6.412801e+01}


def _to_microbatches(a, axis):
    t = _jnp.moveaxis(a, axis, 0)
    t = t.reshape((N_MICROBATCH, t.shape[0] // N_MICROBATCH) + t.shape[1:])
    return _jnp.moveaxis(t, 1, axis + 1)


def setup_inputs(seed: int = 0) -> dict:
    inp = _fwd_setup_inputs(seed)
    key = _jax.random.fold_in(_jax.random.key(seed), 7919)
    shape, _ = _output_shape()
    out = dict(inp)
    out["loss_target"] = _jax.random.normal(_jax.random.fold_in(key, 0), shape, _jnp.float32)
    for i, name in enumerate(TWIN_WEIGHTS):
        w = inp[name].astype(_jnp.float32)
        if MOMENT_SCALE is None:
            s = _jnp.sqrt(_jnp.mean(_jnp.square(w)) + 1e-30)
        else:
            s = MOMENT_SCALE[name]
        km, kv = _jax.random.split(_jax.random.fold_in(key, i + 1))
        out[name] = w
        out["m_" + name] = s * _jax.random.normal(km, w.shape, _jnp.float32)
        out["v_" + name] = (s * s) * _jax.random.uniform(kv, w.shape, _jnp.float32, 0.5, 1.5)
    if N_MICROBATCH > 1:
        for name, axis in PER_EXAMPLE_BATCH_AXIS.items():
            out[name] = _to_microbatches(out[name], axis)
    return {'x': out['x'], 'c': out['c'], 'ada_w': out['ada_w'], 'ada_b': out['ada_b'], 'norm_ffn1_g': out['norm_ffn1_g'], 'ffn1_w_gu': out['ffn1_w_gu'], 'ffn1_w_down': out['ffn1_w_down'], 'norm_mix_g': out['norm_mix_g'], 'mix_w_in': out['mix_w_in'], 'sgu_ln_g': out['sgu_ln_g'], 'sgu_ln_b': out['sgu_ln_b'], 'sgu_w_s': out['sgu_w_s'], 'sgu_b': out['sgu_b'], 'conv_w': out['conv_w'], 'out_norm_g': out['out_norm_g'], 'mix_w_out': out['mix_w_out'], 'norm_ffn2_g': out['norm_ffn2_g'], 'ffn2_w_gu': out['ffn2_w_gu'], 'ffn2_w_down': out['ffn2_w_down'], 'final_norm_g': out['final_norm_g'], 'loss_target': out['loss_target'], 'm_ada_w': out['m_ada_w'], 'm_ada_b': out['m_ada_b'], 'm_norm_ffn1_g': out['m_norm_ffn1_g'], 'm_ffn1_w_gu': out['m_ffn1_w_gu'], 'm_ffn1_w_down': out['m_ffn1_w_down'], 'm_norm_mix_g': out['m_norm_mix_g'], 'm_mix_w_in': out['m_mix_w_in'], 'm_sgu_ln_g': out['m_sgu_ln_g'], 'm_sgu_ln_b': out['m_sgu_ln_b'], 'm_sgu_w_s': out['m_sgu_w_s'], 'm_sgu_b': out['m_sgu_b'], 'm_conv_w': out['m_conv_w'], 'm_out_norm_g': out['m_out_norm_g'], 'm_mix_w_out': out['m_mix_w_out'], 'm_norm_ffn2_g': out['m_norm_ffn2_g'], 'm_ffn2_w_gu': out['m_ffn2_w_gu'], 'm_ffn2_w_down': out['m_ffn2_w_down'], 'm_final_norm_g': out['m_final_norm_g'], 'v_ada_w': out['v_ada_w'], 'v_ada_b': out['v_ada_b'], 'v_norm_ffn1_g': out['v_norm_ffn1_g'], 'v_ffn1_w_gu': out['v_ffn1_w_gu'], 'v_ffn1_w_down': out['v_ffn1_w_down'], 'v_norm_mix_g': out['v_norm_mix_g'], 'v_mix_w_in': out['v_mix_w_in'], 'v_sgu_ln_g': out['v_sgu_ln_g'], 'v_sgu_ln_b': out['v_sgu_ln_b'], 'v_sgu_w_s': out['v_sgu_w_s'], 'v_sgu_b': out['v_sgu_b'], 'v_conv_w': out['v_conv_w'], 'v_out_norm_g': out['v_out_norm_g'], 'v_mix_w_out': out['v_mix_w_out'], 'v_norm_ffn2_g': out['v_norm_ffn2_g'], 'v_ffn2_w_gu': out['v_ffn2_w_gu'], 'v_ffn2_w_down': out['v_ffn2_w_down'], 'v_final_norm_g': out['v_final_norm_g']}


def _loss(weights, diff, rest, loss_target):
    with _jax.named_scope("forward"):
        args = {**rest, TWIN_DIFF_INPUT: diff, **{k: w.astype(_WEIGHT_DTYPES[k]) for k, w in weights.items()}}
        y = _forward(args)
    with _jax.named_scope("loss_head"):
        err = _jnp.square(y.astype(_jnp.float32) - loss_target)
        return 0.5 * _jnp.sum(_jnp.mean(err, axis=-1)) if err.ndim else 0.5 * err


def _adamw(w, g, m, v):
    m = ADAM_B1 * m + (1.0 - ADAM_B1) * g
    v = ADAM_B2 * v + (1.0 - ADAM_B2) * _jnp.square(g)
    m_hat = m / (1.0 - ADAM_B1 ** ADAM_STEP)
    v_hat = v / (1.0 - ADAM_B2 ** ADAM_STEP)
    delta = -ADAM_LR * (m_hat / (_jnp.sqrt(v_hat) + ADAM_EPS) + ADAM_WD * w)
    return delta, m, v


def reference(x, c, ada_w, ada_b, norm_ffn1_g, ffn1_w_gu, ffn1_w_down, norm_mix_g, mix_w_in, sgu_ln_g, sgu_ln_b, sgu_w_s, sgu_b, conv_w, out_norm_g, mix_w_out, norm_ffn2_g, ffn2_w_gu, ffn2_w_down, final_norm_g, loss_target, m_ada_w, m_ada_b, m_norm_ffn1_g, m_ffn1_w_gu, m_ffn1_w_down, m_norm_mix_g, m_mix_w_in, m_sgu_ln_g, m_sgu_ln_b, m_sgu_w_s, m_sgu_b, m_conv_w, m_out_norm_g, m_mix_w_out, m_norm_ffn2_g, m_ffn2_w_gu, m_ffn2_w_down, m_final_norm_g, v_ada_w, v_ada_b, v_norm_ffn1_g, v_ffn1_w_gu, v_ffn1_w_down, v_norm_mix_g, v_mix_w_in, v_sgu_ln_g, v_sgu_ln_b, v_sgu_w_s, v_sgu_b, v_conv_w, v_out_norm_g, v_mix_w_out, v_norm_ffn2_g, v_ffn2_w_gu, v_ffn2_w_down, v_final_norm_g):
    given = dict(x=x, c=c, ada_w=ada_w, ada_b=ada_b, norm_ffn1_g=norm_ffn1_g, ffn1_w_gu=ffn1_w_gu, ffn1_w_down=ffn1_w_down, norm_mix_g=norm_mix_g, mix_w_in=mix_w_in, sgu_ln_g=sgu_ln_g, sgu_ln_b=sgu_ln_b, sgu_w_s=sgu_w_s, sgu_b=sgu_b, conv_w=conv_w, out_norm_g=out_norm_g, mix_w_out=mix_w_out, norm_ffn2_g=norm_ffn2_g, ffn2_w_gu=ffn2_w_gu, ffn2_w_down=ffn2_w_down, final_norm_g=final_norm_g, loss_target=loss_target, m_ada_w=m_ada_w, m_ada_b=m_ada_b, m_norm_ffn1_g=m_norm_ffn1_g, m_ffn1_w_gu=m_ffn1_w_gu, m_ffn1_w_down=m_ffn1_w_down, m_norm_mix_g=m_norm_mix_g, m_mix_w_in=m_mix_w_in, m_sgu_ln_g=m_sgu_ln_g, m_sgu_ln_b=m_sgu_ln_b, m_sgu_w_s=m_sgu_w_s, m_sgu_b=m_sgu_b, m_conv_w=m_conv_w, m_out_norm_g=m_out_norm_g, m_mix_w_out=m_mix_w_out, m_norm_ffn2_g=m_norm_ffn2_g, m_ffn2_w_gu=m_ffn2_w_gu, m_ffn2_w_down=m_ffn2_w_down, m_final_norm_g=m_final_norm_g, v_ada_w=v_ada_w, v_ada_b=v_ada_b, v_norm_ffn1_g=v_norm_ffn1_g, v_ffn1_w_gu=v_ffn1_w_gu, v_ffn1_w_down=v_ffn1_w_down, v_norm_mix_g=v_norm_mix_g, v_mix_w_in=v_mix_w_in, v_sgu_ln_g=v_sgu_ln_g, v_sgu_ln_b=v_sgu_ln_b, v_sgu_w_s=v_sgu_w_s, v_sgu_b=v_sgu_b, v_conv_w=v_conv_w, v_out_norm_g=v_out_norm_g, v_mix_w_out=v_mix_w_out, v_norm_ffn2_g=v_norm_ffn2_g, v_ffn2_w_gu=v_ffn2_w_gu, v_ffn2_w_down=v_ffn2_w_down, v_final_norm_g=v_final_norm_g)
    weights = {n: given[n] for n in TWIN_WEIGHTS}
    shared = {n: given[n] for n in SHARED_INPUTS}
    per_example = {n: given[n] for n in ['x', 'c']}
    grad_fn = _jax.value_and_grad(_loss, argnums=(0, 1))

    def one_microbatch(ex, loss_target):
        ex = dict(ex)
        diff = ex.pop(TWIN_DIFF_INPUT)
        return grad_fn(weights, diff, {**shared, **ex}, loss_target)

    if N_MICROBATCH == 1:
        loss, (grad_w, grad_x) = one_microbatch(per_example, given["loss_target"])
    else:
        def body(carry, xs):
            loss_sum, grad_sum = carry
            l_k, (gw_k, gx_k) = one_microbatch(xs[0], xs[1])
            with _jax.named_scope("update"):
                return (loss_sum + l_k, _jax.tree.map(_jnp.add, grad_sum, gw_k)), gx_k

        init = (_jnp.zeros((), _jnp.float32), _jax.tree.map(_jnp.zeros_like, weights))
        (loss, grad_w), grad_x = _jax.lax.scan(body, init, (per_example, given["loss_target"]))
    with _jax.named_scope("update"):
        delta_w, new_m, new_v = {}, {}, {}
        for n in TWIN_WEIGHTS:
            delta_w[n], new_m[n], new_v[n] = _adamw(weights[n], grad_w[n], given["m_" + n], given["v_" + n])
    return (loss, grad_x, *[grad_w[n] for n in TWIN_WEIGHTS], *[delta_w[n] for n in TWIN_WEIGHTS],
            *[new_m[n] for n in TWIN_WEIGHTS], *[new_v[n] for n in TWIN_WEIGHTS])
```

```python
import functools
import math

import jax
import jax.numpy as jnp
from jax import lax
from jax.experimental import pallas as pl
from jax.experimental.pallas import tpu as pltpu

F32 = jnp.float32
BF16 = jnp.bfloat16
MESH = pl.DeviceIdType.MESH

N_HEADS = 8
CHUNK = 128
N_MOD = 9
EPS = 1e-6
N_DEV = 8
N_CHIP = 4

ADAM_LR = 0.001
ADAM_B1 = 0.9
ADAM_B2 = 0.999
ADAM_EPS = 1e-08
ADAM_WD = 0.01
ADAM_STEP = 10

TOKEN_TILE = 512
FF_TILE = 256
MIX_TILE = 256
WGRAD_TOKENS = 1024
VMEM_LIMIT = 52 * 1024 * 1024


def _tile(pref, n):
    t = min(pref, n)
    assert n % t == 0, (pref, n)
    return t


def _params(*sem):
    return pltpu.CompilerParams(dimension_semantics=sem, vmem_limit_bytes=VMEM_LIMIT)


def _dot(a, b):
    return jnp.dot(a, b, preferred_element_type=F32)


def _dot_nt(a, b):
    return lax.dot_general(a, b, (((1,), (1,)), ((), ())), preferred_element_type=F32)


def _dot_tn(a, b):
    return lax.dot_general(a, b, (((0,), (0,)), ((), ())), preferred_element_type=F32)


def _sigmoid(x):
    return 1.0 / (1.0 + jnp.exp(-x))


def _rms(x):
    r = lax.rsqrt(jnp.mean(x * x, axis=-1, keepdims=True) + EPS)
    return x * r, r


def _norm_mod_bwd(x, dh, gain, sc):
    xh, r = _rms(x)
    dsc = jnp.sum(dh * (xh * gain), axis=0, keepdims=True)
    dsh = jnp.sum(dh, axis=0, keepdims=True)
    dn = dh * (1.0 + sc)
    dgain = jnp.sum(dn * xh, axis=0, keepdims=True)
    dy = dn * gain
    dx = r * (dy - xh * jnp.mean(dy * xh, axis=-1, keepdims=True))
    return dx, dsc, dsh, dgain


def _acc(ref, first, val):
    @pl.when(first)
    def _():
        ref[...] = val

    @pl.when(jnp.logical_not(first))
    def _():
        ref[...] += val


def _ffn_fwd(x, gain, sh, sc, gate, wgu, wd, l):
    T, D = x.shape
    F = wd.shape[1]
    B = sh.shape[0]
    tm = _tile(TOKEN_TILE, T // B)
    tf = _tile(FF_TILE, F)
    tps = (T // B) // tm
    nf = F // tf

    def body(x_ref, gain_ref, sh_ref, sc_ref, gate_ref, wg_ref, wu_ref, wd_ref,
             xo_ref, gu_ref, h_ref, f_ref, hs, acc):
        k = pl.program_id(1)

        @pl.when(k == 0)
        def _():
            xh, _ = _rms(x_ref[...])
            h = xh * gain_ref[...] * (1.0 + sc_ref[0]) + sh_ref[0]
            hs[...] = h.astype(BF16)
            h_ref[...] = h.astype(BF16)
            acc[...] = jnp.zeros_like(acc)

        g = _dot(hs[...], wg_ref[...])
        u = _dot(hs[...], wu_ref[...])
        a = g * _sigmoid(g) * u
        gu_ref[0] = g.astype(BF16)
        gu_ref[1] = u.astype(BF16)
        acc[...] += _dot(a.astype(BF16), wd_ref[...])

        @pl.when(k == nf - 1)
        def _():
            f = acc[...]
            f_ref[...] = f.astype(BF16)
            xo_ref[...] = x_ref[...] + 0.5 * gate_ref[0] * f

    seq = lambda i, k: (i // tps, 0, 0)
    return pl.pallas_call(
        body,
        name="ffn_fwd",
        grid=(T // tm, nf),
        in_specs=[
            pl.BlockSpec((tm, D), lambda i, k: (i, 0)),
            pl.BlockSpec((1, D), lambda i, k: (0, 0)),
            pl.BlockSpec((1, 1, D), seq),
            pl.BlockSpec((1, 1, D), seq),
            pl.BlockSpec((1, 1, D), seq),
            pl.BlockSpec((None, D, tf), lambda i, k: (l, 0, k)),
            pl.BlockSpec((None, D, tf), lambda i, k: (l, 0, nf + k)),
            pl.BlockSpec((None, tf, D), lambda i, k: (l, k, 0)),
        ],
        out_specs=[
            pl.BlockSpec((tm, D), lambda i, k: (i, 0)),
            pl.BlockSpec((2, tm, tf), lambda i, k: (0, i, k)),
            pl.BlockSpec((tm, D), lambda i, k: (i, 0)),
            pl.BlockSpec((tm, D), lambda i, k: (i, 0)),
        ],
        out_shape=[
            jax.ShapeDtypeStruct((T, D), F32),
            jax.ShapeDtypeStruct((2, T, F), BF16),
            jax.ShapeDtypeStruct((T, D), BF16),
            jax.ShapeDtypeStruct((T, D), BF16),
        ],
        scratch_shapes=[pltpu.VMEM((tm, D), BF16), pltpu.VMEM((tm, D), F32)],
        compiler_params=_params("arbitrary", "arbitrary"),
    )(x, gain, sh, sc, gate, wgu, wgu, wd)


def _ffn_bwd(dxo, x, gu, f, gain, sc, gate, wgu, wd, l):
    T, D = x.shape
    F = wd.shape[1]
    B = sc.shape[0]
    tm = _tile(TOKEN_TILE, T // B)
    tf = _tile(FF_TILE, F)
    tps = (T // B) // tm
    nf = F // tf

    def body(dxo_ref, x_ref, gu_ref, f_ref, gain_ref, sc_ref, gate_ref, wg_ref, wu_ref, wd_ref,
             dx_ref, dgu_ref, a_ref, df_ref, dgate_ref, dsc_ref, dsh_ref, dgain_ref, dfs, acc):
        i = pl.program_id(0)
        k = pl.program_id(1)
        first_of_seq = (i % tps) == 0

        @pl.when(k == 0)
        def _():
            dxo = dxo_ref[...]
            df = (0.5 * gate_ref[0] * dxo).astype(BF16)
            dfs[...] = df
            df_ref[...] = df
            acc[...] = jnp.zeros_like(acc)
            _acc(dgate_ref.at[0], first_of_seq,
                 0.5 * jnp.sum(dxo * f_ref[...].astype(F32), axis=0, keepdims=True))

        da = _dot_nt(dfs[...], wd_ref[...])
        g = gu_ref[0].astype(F32)
        u = gu_ref[1].astype(F32)
        sg = _sigmoid(g)
        sl = g * sg
        a_ref[...] = (sl * u).astype(BF16)
        du = (da * sl).astype(BF16)
        dg = (da * u * (sg * (1.0 + g * (1.0 - sg)))).astype(BF16)
        dgu_ref[0] = dg
        dgu_ref[1] = du
        acc[...] += _dot_nt(dg, wg_ref[...]) + _dot_nt(du, wu_ref[...])

        @pl.when(k == nf - 1)
        def _():
            dx, dsc, dsh, dgain = _norm_mod_bwd(x_ref[...], acc[...], gain_ref[...], sc_ref[0])
            dx_ref[...] = dxo_ref[...] + dx
            _acc(dsc_ref.at[0], first_of_seq, dsc)
            _acc(dsh_ref.at[0], first_of_seq, dsh)
            _acc(dgain_ref, i == 0, dgain)

    seq = lambda i, k: (i // tps, 0, 0)
    row = lambda i, k: (i, 0)
    return pl.pallas_call(
        body,
        name="ffn_bwd",
        grid=(T // tm, nf),
        in_specs=[
            pl.BlockSpec((tm, D), row),
            pl.BlockSpec((tm, D), row),
            pl.BlockSpec((2, tm, tf), lambda i, k: (0, i, k)),
            pl.BlockSpec((tm, D), row),
            pl.BlockSpec((1, D), lambda i, k: (0, 0)),
            pl.BlockSpec((1, 1, D), seq),
            pl.BlockSpec((1, 1, D), seq),
            pl.BlockSpec((None, D, tf), lambda i, k: (l, 0, k)),
            pl.BlockSpec((None, D, tf), lambda i, k: (l, 0, nf + k)),
            pl.BlockSpec((None, tf, D), lambda i, k: (l, k, 0)),
        ],
        out_specs=[
            pl.BlockSpec((tm, D), row),
            pl.BlockSpec((2, tm, tf), lambda i, k: (0, i, k)),
            pl.BlockSpec((tm, tf), lambda i, k: (i, k)),
            pl.BlockSpec((tm, D), row),
            pl.BlockSpec((1, 1, D), seq),
            pl.BlockSpec((1, 1, D), seq),
            pl.BlockSpec((1, 1, D), seq),
            pl.BlockSpec((1, D), lambda i, k: (0, 0)),
        ],
        out_shape=[
            jax.ShapeDtypeStruct((T, D), F32),
            jax.ShapeDtypeStruct((2, T, F), BF16),
            jax.ShapeDtypeStruct((T, F), BF16),
            jax.ShapeDtypeStruct((T, D), BF16),
            jax.ShapeDtypeStruct((B, 1, D), F32),
            jax.ShapeDtypeStruct((B, 1, D), F32),
            jax.ShapeDtypeStruct((B, 1, D), F32),
            jax.ShapeDtypeStruct((1, D), F32),
        ],
        scratch_shapes=[pltpu.VMEM((tm, D), BF16), pltpu.VMEM((tm, D), F32)],
        compiler_params=_params("arbitrary", "arbitrary"),
    )(dxo, x, gu, f, gain, sc, gate, wgu, wgu, wd)


def _wgrad(a, b, l, n_layers, prev, tmm, tn, col_major, name):
    T, M = a.shape
    nb, _, Nb = b.shape
    N = nb * Nb
    tk = _tile(WGRAD_TOKENS, T)
    npb = Nb // tn
    assert M % tmm == 0 and Nb % tn == 0
    if col_major:
        shape = (n_layers, N // tn, M // tmm, tmm, tn)
        out_spec = pl.BlockSpec((None, None, None, tmm, tn), lambda i, j, t: (l, j, i, 0, 0))
    else:
        shape = (n_layers, M // tmm, tmm, N)
        out_spec = pl.BlockSpec((None, None, tmm, tn), lambda i, j, t: (l, i, 0, j))

    def body(a_ref, b_ref, *rest):
        o_ref = rest[-1]
        t = pl.program_id(2)
        _acc(o_ref, t == 0, _dot_tn(a_ref[...], b_ref[...]))

    in_specs = [
        pl.BlockSpec((tk, tmm), lambda i, j, t: (t, i)),
        pl.BlockSpec((None, tk, tn), lambda i, j, t: (j // npb, t, j % npb)),
    ]
    args = [a, b]
    aliases = {}
    if prev is not None:
        in_specs.append(pl.BlockSpec(memory_space=pl.ANY))
        args.append(prev)
        aliases = {2: 0}
    return pl.pallas_call(
        body,
        name=name,
        grid=(M // tmm, N // tn, T // tk),
        in_specs=in_specs,
        out_specs=out_spec,
        out_shape=jax.ShapeDtypeStruct(shape, F32),
        input_output_aliases=aliases,
        compiler_params=_params("arbitrary", "arbitrary", "arbitrary"),
    )(*args)


def _mixin_fwd(x, gain, sh, sc, win, l):
    T, D = x.shape
    P = win.shape[2]
    B = sh.shape[0]
    tm = _tile(TOKEN_TILE, T // B)
    tps = (T // B) // tm
    tn = P // 5

    def body(x_ref, gain_ref, sh_ref, sc_ref, w_ref, proj_ref, h_ref, hs):
        @pl.when(pl.program_id(1) == 0)
        def _():
            xh, _ = _rms(x_ref[...])
            h = (xh * gain_ref[...] * (1.0 + sc_ref[0]) + sh_ref[0]).astype(BF16)
            hs[...] = h
            h_ref[...] = h

        proj_ref[...] = _dot(hs[...], w_ref[...])

    seq = lambda i, j: (i // tps, 0, 0)
    return pl.pallas_call(
        body,
        name="mixin_fwd",
        grid=(T // tm, P // tn),
        in_specs=[
            pl.BlockSpec((tm, D), lambda i, j: (i, 0)),
            pl.BlockSpec((1, D), lambda i, j: (0, 0)),
            pl.BlockSpec((1, 1, D), seq),
            pl.BlockSpec((1, 1, D), seq),
            pl.BlockSpec((None, D, tn), lambda i, j: (l, 0, j)),
        ],
        out_specs=[
            pl.BlockSpec((tm, tn), lambda i, j: (i, j)),
            pl.BlockSpec((tm, D), lambda i, j: (i, 0)),
        ],
        out_shape=[jax.ShapeDtypeStruct((T, P), F32), jax.ShapeDtypeStruct((T, D), BF16)],
        scratch_shapes=[pltpu.VMEM((tm, D), BF16)],
        compiler_params=_params("arbitrary", "arbitrary"),
    )(x, gain, sh, sc, win)


def _mixin_bwd(dxo, x, dproj, gain, sc, win, l):
    T, D = x.shape
    P = win.shape[2]
    B = sc.shape[0]
    tm = _tile(TOKEN_TILE, T // B)
    tps = (T // B) // tm
    tk = P // 5
    nk = P // tk

    def body(dxo_ref, x_ref, dp_ref, gain_ref, sc_ref, w_ref, dx_ref, dsc_ref, dsh_ref, dgain_ref, acc):
        i = pl.program_id(0)
        k = pl.program_id(1)
        _acc(acc, k == 0, _dot_nt(dp_ref[...], w_ref[...]))

        @pl.when(k == nk - 1)
        def _():
            first_of_seq = (i % tps) == 0
            dx, dsc, dsh, dgain = _norm_mod_bwd(x_ref[...], acc[...], gain_ref[...], sc_ref[0])
            dx_ref[...] = dxo_ref[...] + dx
            _acc(dsc_ref.at[0], first_of_seq, dsc)
            _acc(dsh_ref.at[0], first_of_seq, dsh)
            _acc(dgain_ref, i == 0, dgain)

    seq = lambda i, k: (i // tps, 0, 0)
    row = lambda i, k: (i, 0)
    return pl.pallas_call(
        body,
        name="mixin_bwd",
        grid=(T // tm, nk),
        in_specs=[
            pl.BlockSpec((tm, D), row),
            pl.BlockSpec((tm, D), row),
            pl.BlockSpec((tm, tk), lambda i, k: (i, k)),
            pl.BlockSpec((1, D), lambda i, k: (0, 0)),
            pl.BlockSpec((1, 1, D), seq),
            pl.BlockSpec((None, D, tk), lambda i, k: (l, 0, k)),
        ],
        out_specs=[
            pl.BlockSpec((tm, D), row),
            pl.BlockSpec((1, 1, D), seq),
            pl.BlockSpec((1, 1, D), seq),
            pl.BlockSpec((1, D), lambda i, k: (0, 0)),
        ],
        out_shape=[
            jax.ShapeDtypeStruct((T, D), F32),
            jax.ShapeDtypeStruct((B, 1, D), F32),
            jax.ShapeDtypeStruct((B, 1, D), F32),
            jax.ShapeDtypeStruct((1, D), F32),
        ],
        scratch_shapes=[pltpu.VMEM((tm, D), F32)],
        compiler_params=_params("arbitrary", "arbitrary"),
    )(dxo, x, dproj, gain, sc, win)


def _head_mean(z, pmat):
    hi = z.astype(BF16)
    lo = (z - hi.astype(F32)).astype(BF16)
    return _dot(hi, pmat) + _dot(lo, pmat)


def _gelu_parts(x):
    cdf = 0.5 * (1.0 + lax.erf(x * (1.0 / math.sqrt(2.0))))
    return x * cdf, cdf


def _gelu_grad(x, cdf):
    return cdf + x * jnp.exp(-0.5 * x * x) * (1.0 / math.sqrt(2.0 * math.pi))


def _head_masks(da):
    hd = da // N_HEADS
    col = lax.broadcasted_iota(jnp.int32, (1, da), 1)
    return [(col >= h * hd) & (col < (h + 1) * hd) for h in range(N_HEADS)]


def _select_heads(res, masks):
    out = res[0:CHUNK]
    for h in range(1, N_HEADS):
        out = jnp.where(masks[h], res[h * CHUNK:(h + 1) * CHUNK], out)
    return out


def _causal_stack(w, transposed):
    r = lax.broadcasted_iota(jnp.int32, w.shape, 0) % CHUNK
    c = lax.broadcasted_iota(jnp.int32, w.shape, 1)
    keep = (c >= r) if transposed else (c <= r)
    return jnp.where(keep, w, 0.0)


def _mix_core_forward(proj, zprev, prm, da, db):
    n = proj.shape[0]
    ua = proj[:, 0:da]
    va = proj[:, da:2 * da]
    bg = proj[:, 2 * da:2 * da + db]
    cg = proj[:, 2 * da + db:2 * da + 2 * db]
    xb = proj[:, 2 * da + 2 * db:]
    ug, ucdf = _gelu_parts(ua)
    vg, vcdf = _gelu_parts(va)
    zc = vg - _head_mean(vg, prm["pmat"])
    rs = lax.rsqrt(_head_mean(zc * zc, prm["pmat"]) + EPS)
    vhat = zc * rs
    vln = (vhat * prm["lng"] + prm["lnb"]).astype(BF16)
    wst = _causal_stack(prm["wst"], False).astype(BF16)
    masks = _head_masks(da)
    mixed = []
    for j in range(n // CHUNK):
        res = _dot(wst, vln[j * CHUNK:(j + 1) * CHUNK])
        mixed.append(_select_heads(res, masks) + prm["bias"])
    mixed = mixed[0] if len(mixed) == 1 else jnp.concatenate(mixed, axis=0)
    ya = ug * mixed
    z = cg * xb
    row = lax.broadcasted_iota(jnp.int32, z.shape, 0)
    z1 = jnp.where(row == 0, zprev[7:8], pltpu.roll(z, 1, 0))
    z2 = jnp.where(row == 0, zprev[6:7], jnp.where(row == 1, zprev[7:8], pltpu.roll(z, 2, 0)))
    cw = prm["convw"]
    conv = z2 * cw[0:1] + z1 * cw[1:2] + z * cw[2:3]
    yb = bg * conv
    yah, ra = _rms(ya)
    ybh, rb = _rms(yb)
    return dict(ua=ua, va=va, bg=bg, cg=cg, xb=xb, ug=ug, ucdf=ucdf, vcdf=vcdf, rs=rs, vhat=vhat, vln=vln,
                mixed=mixed, z=z, z1=z1, z2=z2, conv=conv, yah=yah, ra=ra, ybh=ybh, rb=rb, masks=masks)


def _mix_params(lng_ref, lnb_ref, wst_ref, bias_ref, pmat_ref, convw_ref):
    return dict(lng=lng_ref[...], lnb=lnb_ref[...], wst=wst_ref[...], bias=bias_ref[...],
                pmat=pmat_ref[...], convw=convw_ref[...])


def _mix_core_fwd(proj, x, gate, wout, l, lng, lnb, wst, bias, pmat, convw, og):
    T, P = proj.shape
    D = x.shape[1]
    B = gate.shape[0]
    da = lng.shape[1]
    db = convw.shape[1]
    tm = _tile(MIX_TILE, T // B)
    tps = (T // B) // tm

    def body(proj_ref, x_ref, gate_ref, wout_ref, lng_ref, lnb_ref, wst_ref, bias_ref, pmat_ref, convw_ref,
             og_ref, xo_ref, yn_ref, halo):
        i = pl.program_id(0)

        @pl.when((i % tps) == 0)
        def _():
            halo[...] = jnp.zeros_like(halo)

        prm = _mix_params(lng_ref, lnb_ref, wst_ref, bias_ref, pmat_ref, convw_ref)
        r = _mix_core_forward(proj_ref[...], halo[...], prm, da, db)
        halo[...] = r["z"][tm - 8:tm]
        og = og_ref[...]
        yn_ref[:, 0:da] = (r["yah"] * og[:, 0:da]).astype(BF16)
        yn_ref[:, da:] = (r["ybh"] * og[:, da:]).astype(BF16)
        xo_ref[...] = x_ref[...] + gate_ref[0] * _dot(yn_ref[...], wout_ref[...])

    full = lambda a: pl.BlockSpec(a.shape, lambda i: (0,) * a.ndim)
    return pl.pallas_call(
        body,
        name="mix_core_fwd",
        grid=(T // tm,),
        in_specs=[
            pl.BlockSpec((tm, P), lambda i: (i, 0)),
            pl.BlockSpec((tm, D), lambda i: (i, 0)),
            pl.BlockSpec((1, 1, D), lambda i: (i // tps, 0, 0)),
            pl.BlockSpec((None, D, D), lambda i: (l, 0, 0)),
            full(lng), full(lnb), full(wst), full(bias), full(pmat), full(convw), full(og),
        ],
        out_specs=[
            pl.BlockSpec((tm, D), lambda i: (i, 0)),
            pl.BlockSpec((tm, D), lambda i: (i, 0)),
        ],
        out_shape=[jax.ShapeDtypeStruct((T, D), F32), jax.ShapeDtypeStruct((T, D), BF16)],
        scratch_shapes=[pltpu.VMEM((8, db), F32)],
        compiler_params=_params("arbitrary"),
    )(proj, x, gate, wout, lng, lnb, wst, bias, pmat, convw, og)


def _mix_core_bwd(proj, dxo, gate, wout, l, lng, lnb, wst, wstt, bias, pmat, convw, og):
    T, P = proj.shape
    D = dxo.shape[1]
    B = gate.shape[0]
    da = lng.shape[1]
    db = convw.shape[1]
    assert da == db and P == 2 * da + 3 * db
    tm = _tile(MIX_TILE, T // B)
    tps = (T // B) // tm
    nt = T // tm
    hd = da // N_HEADS

    def body(proj_ref, cgp_ref, xbp_ref, dxo_ref, gate_ref, wout_ref, lng_ref, lnb_ref, wst_ref, wstt_ref,
             bias_ref, pmat_ref, convw_ref, og_ref,
             dproj_ref, do_ref, dgate_ref, dog_ref, dwst_ref, dbias_ref, dlng_ref, dlnb_ref, dconvw_ref, carry):
        i = pl.program_id(0)
        ri = nt - 1 - i
        first = i == 0
        end_of_seq = (ri % tps) == tps - 1
        start_of_seq = (ri % tps) == 0

        @pl.when(end_of_seq)
        def _():
            carry[...] = jnp.zeros_like(carry)

        prm = _mix_params(lng_ref, lnb_ref, wst_ref, bias_ref, pmat_ref, convw_ref)
        zprev = jnp.where(start_of_seq, 0.0, cgp_ref[...] * xbp_ref[...])
        r = _mix_core_forward(proj_ref[...], zprev, prm, da, db)
        og = og_ref[...]
        pmat = prm["pmat"]

        yn = jnp.concatenate([(r["yah"] * og[:, 0:da]).astype(BF16), (r["ybh"] * og[:, da:]).astype(BF16)], axis=1)
        dxo = dxo_ref[...]
        o = _dot(yn, wout_ref[...])
        _acc(dgate_ref.at[0], end_of_seq, jnp.sum(dxo * o, axis=0, keepdims=True))
        d_o = (gate_ref[0] * dxo).astype(BF16)
        do_ref[...] = d_o
        dyn = _dot_nt(d_o, wout_ref[...])

        def rms_bwd(dyn_g, yh, rr, og_g):
            dog_g = jnp.sum(dyn_g * yh, axis=0, keepdims=True)
            dyh = dyn_g * og_g
            return rr * (dyh - yh * jnp.mean(dyh * yh, axis=-1, keepdims=True)), dog_g

        dya, dog_a = rms_bwd(dyn[:, 0:da], r["yah"], r["ra"], og[:, 0:da])
        dyb, dog_b = rms_bwd(dyn[:, da:], r["ybh"], r["rb"], og[:, da:])
        _acc(dog_ref, first, jnp.concatenate([dog_a, dog_b], axis=1))

        dug = dya * r["mixed"]
        dmixed = dya * r["ug"]
        wstt_b = _causal_stack(wstt_ref[...], True).astype(BF16)
        masks = r["masks"]
        dbias = jnp.zeros((CHUNK, da), F32)
        dwst = jnp.zeros((N_HEADS * CHUNK, CHUNK), F32)
        dvln = []
        for j in range(tm // CHUNK):
            dm = dmixed[j * CHUNK:(j + 1) * CHUNK]
            dbias = dbias + dm
            dmb = dm.astype(BF16)
            stack = jnp.concatenate([jnp.where(masks[h], dmb, jnp.zeros_like(dmb)) for h in range(N_HEADS)], axis=0)
            dwst = dwst + _dot_nt(stack, r["vln"][j * CHUNK:(j + 1) * CHUNK])
            dvln.append(_select_heads(_dot(wstt_b, dmb), masks))
        dvln = dvln[0] if len(dvln) == 1 else jnp.concatenate(dvln, axis=0)
        _acc(dbias_ref, first, dbias)
        _acc(dwst_ref, first, dwst)
        _acc(dlng_ref, first, jnp.sum(dvln * r["vhat"], axis=0, keepdims=True))
        _acc(dlnb_ref, first, jnp.sum(dvln, axis=0, keepdims=True))
        dvhat = dvln * prm["lng"]
        dvg = r["rs"] * (dvhat - _head_mean(dvhat, pmat) - r["vhat"] * _head_mean(dvhat * r["vhat"], pmat))
        dproj_ref[:, 0:da] = (dug * _gelu_grad(r["ua"], r["ucdf"])).astype(BF16)
        dproj_ref[:, da:2 * da] = (dvg * _gelu_grad(r["va"], r["vcdf"])).astype(BF16)

        dproj_ref[:, 2 * da:2 * da + db] = (dyb * r["conv"]).astype(BF16)
        dconv = dyb * r["bg"]
        dcw = jnp.concatenate([
            jnp.sum(dconv * r["z2"], axis=0, keepdims=True),
            jnp.sum(dconv * r["z1"], axis=0, keepdims=True),
            jnp.sum(dconv * r["z"], axis=0, keepdims=True),
            jnp.zeros((5, db), F32)], axis=0)
        _acc(dconvw_ref, first, dcw)
        nxt = carry[...]
        row = lax.broadcasted_iota(jnp.int32, dconv.shape, 0)
        dc1 = jnp.where(row == tm - 1, nxt[0:1], pltpu.roll(dconv, tm - 1, 0))
        dc2 = jnp.where(row == tm - 2, nxt[0:1], jnp.where(row == tm - 1, nxt[1:2], pltpu.roll(dconv, tm - 2, 0)))
        carry[...] = dconv[0:8]
        cw = prm["convw"]
        dz = dconv * cw[2:3] + dc1 * cw[1:2] + dc2 * cw[0:1]
        dproj_ref[:, 2 * da + db:2 * da + 2 * db] = (dz * r["xb"]).astype(BF16)
        dproj_ref[:, 2 * da + 2 * db:] = (dz * r["cg"]).astype(BF16)

        @pl.when(i == nt - 1)
        def _():
            dwst_ref[...] = _causal_stack(dwst_ref[...], False)
            dbias_ref[...] = _head_mean(dbias_ref[...], pmat) * float(hd)

    full = lambda a: pl.BlockSpec(a.shape, lambda i: (0,) * a.ndim)
    const = lambda i: (0, 0)
    rev = lambda i: (nt - 1 - i, 0)
    prev8 = lambda col: (lambda i: (jnp.maximum((nt - 1 - i) * (tm // 8) - 1, 0), col))
    return pl.pallas_call(
        body,
        name="mix_core_bwd",
        grid=(nt,),
        in_specs=[
            pl.BlockSpec((tm, P), rev),
            pl.BlockSpec((8, db), prev8((2 * da + db) // db)),
            pl.BlockSpec((8, db), prev8((2 * da + 2 * db) // db)),
            pl.BlockSpec((tm, D), rev),
            pl.BlockSpec((1, 1, D), lambda i: ((nt - 1 - i) // tps, 0, 0)),
            pl.BlockSpec((None, D, D), lambda i: (l, 0, 0)),
            full(lng), full(lnb), full(wst), full(wstt), full(bias), full(pmat), full(convw), full(og),
        ],
        out_specs=[
            pl.BlockSpec((tm, P), rev),
            pl.BlockSpec((tm, D), rev),
            pl.BlockSpec((1, 1, D), lambda i: ((nt - 1 - i) // tps, 0, 0)),
            pl.BlockSpec((1, D), const),
            pl.BlockSpec((N_HEADS * CHUNK, CHUNK), const),
            pl.BlockSpec((CHUNK, da), const),
            pl.BlockSpec((1, da), const),
            pl.BlockSpec((1, da), const),
            pl.BlockSpec((8, db), const),
        ],
        out_shape=[
            jax.ShapeDtypeStruct((T, P), BF16),
            jax.ShapeDtypeStruct((T, D), BF16),
            jax.ShapeDtypeStruct((B, 1, D), F32),
            jax.ShapeDtypeStruct((1, D), F32),
            jax.ShapeDtypeStruct((N_HEADS * CHUNK, CHUNK), F32),
            jax.ShapeDtypeStruct((CHUNK, da), F32),
            jax.ShapeDtypeStruct((1, da), F32),
            jax.ShapeDtypeStruct((1, da), F32),
            jax.ShapeDtypeStruct((8, db), F32),
        ],
        scratch_shapes=[pltpu.VMEM((8, db), F32)],
        compiler_params=_params("arbitrary"),
    )(proj, proj, proj, dxo, gate, wout, lng, lnb, wst, wstt, bias, pmat, convw, og)


def _loss_head(x, target, gain):
    T, D = x.shape
    tm = _tile(TOKEN_TILE, T)

    def body(x_ref, t_ref, gain_ref, dx_ref, loss_ref, dgain_ref):
        first = pl.program_id(0) == 0
        xh, r = _rms(x_ref[...])
        gain = gain_ref[...]
        err = xh * gain - t_ref[...]
        _acc(loss_ref, first, jnp.zeros((8, 128), F32) + 0.5 * jnp.sum(err * err) / D)
        dout = err * (1.0 / D)
        _acc(dgain_ref, first, jnp.sum(dout * xh, axis=0, keepdims=True))
        dy = dout * gain
        dx_ref[...] = r * (dy - xh * jnp.mean(dy * xh, axis=-1, keepdims=True))

    return pl.pallas_call(
        body,
        name="loss_head",
        grid=(T // tm,),
        in_specs=[
            pl.BlockSpec((tm, D), lambda i: (i, 0)),
            pl.BlockSpec((tm, D), lambda i: (i, 0)),
            pl.BlockSpec((1, D), lambda i: (0, 0)),
        ],
        out_specs=[
            pl.BlockSpec((tm, D), lambda i: (i, 0)),
            pl.BlockSpec((8, 128), lambda i: (0, 0)),
            pl.BlockSpec((1, D), lambda i: (0, 0)),
        ],
        out_shape=[
            jax.ShapeDtypeStruct((T, D), F32),
            jax.ShapeDtypeStruct((8, 128), F32),
            jax.ShapeDtypeStruct((1, D), F32),
        ],
        compiler_params=_params("arbitrary"),
    )(x, target, gain)


def _ada_fwd(c_all, ada_w, ada_b):
    n, D = c_all.shape
    L, _, sa = ada_w.shape
    tn = _tile(768, sa)

    def body(c_ref, w_ref, b_ref, act_ref, o_ref):
        c = c_ref[...]
        act = (c * _sigmoid(c)).astype(BF16)
        act_ref[...] = act
        o_ref[...] = _dot(act, w_ref[...].astype(BF16)) + b_ref[...]

    return pl.pallas_call(
        body,
        name="ada_fwd",
        grid=(L, sa // tn),
        in_specs=[
            pl.BlockSpec((n, D), lambda l, j: (0, 0)),
            pl.BlockSpec((None, D, tn), lambda l, j: (l, 0, j)),
            pl.BlockSpec((None, 1, tn), lambda l, j: (l, 0, j)),
        ],
        out_specs=[
            pl.BlockSpec((n, D), lambda l, j: (0, 0)),
            pl.BlockSpec((None, n, tn), lambda l, j: (l, 0, j)),
        ],
        out_shape=[jax.ShapeDtypeStruct((n, D), BF16), jax.ShapeDtypeStruct((L, n, sa), F32)],
        compiler_params=_params("arbitrary", "arbitrary"),
    )(c_all, ada_w, ada_b)


def _colsum(a):
    L, n, C = a.shape

    def body(a_ref, o_ref):
        o_ref[...] = jnp.sum(a_ref[...], axis=0, keepdims=True)

    return pl.pallas_call(
        body,
        name="colsum",
        grid=(L,),
        in_specs=[pl.BlockSpec((None, n, C), lambda l: (l, 0, 0))],
        out_specs=pl.BlockSpec((None, 1, C), lambda l: (l, 0, 0)),
        out_shape=jax.ShapeDtypeStruct((L, 1, C), F32),
        compiler_params=_params("arbitrary"),
    )(a)


def _row_tile(rows, cols, nbuf):
    budget = VMEM_LIMIT // 3 // (2 * nbuf * 4 * cols)
    t = rows
    while t > max(budget, 8) and t % 2 == 0 and (t // 2) % 8 == 0:
        t //= 2
    return t


def _pair_sum(g, recv, core):
    n, _, R, C = g.shape
    tr = _row_tile(R, C, 3)

    def body(core_ref, g_ref, r_ref, o_ref):
        o_ref[...] = (g_ref[...] + r_ref[...]).astype(BF16)

    return pl.pallas_call(
        body,
        name="pair_sum",
        grid_spec=pltpu.PrefetchScalarGridSpec(
            num_scalar_prefetch=1,
            grid=(n, R // tr),
            in_specs=[
                pl.BlockSpec((None, None, tr, C), lambda i, r, core_ref: (i, core_ref[0], r, 0)),
                pl.BlockSpec((None, tr, C), lambda i, r, core_ref: (i, r, 0)),
            ],
            out_specs=pl.BlockSpec((None, tr, C), lambda i, r, core_ref: (i, r, 0)),
        ),
        out_shape=jax.ShapeDtypeStruct((n, R, C), BF16),
        compiler_params=_params("arbitrary", "arbitrary"),
    )(core, g, recv)


def _chip_sum(q, core):
    L, nq, R, C = q.shape
    tr = _row_tile(R, C, 4)

    def body(core_ref, q_ref, o_ref):
        s = q_ref[0].astype(F32)
        for j in range(1, nq):
            s = s + q_ref[j].astype(F32)
        o_ref[...] = s

    return pl.pallas_call(
        body,
        name="chip_sum",
        grid_spec=pltpu.PrefetchScalarGridSpec(
            num_scalar_prefetch=1,
            grid=(L, R // tr),
            in_specs=[pl.BlockSpec((None, nq, tr, C), lambda l, r, core_ref: (l, 0, r, 0))],
            out_specs=pl.BlockSpec((None, None, tr, C), lambda l, r, core_ref: (l, core_ref[0], r, 0)),
        ),
        out_shape=jax.ShapeDtypeStruct((L, 2, R, C), F32),
        compiler_params=_params("arbitrary", "arbitrary"),
    )(core, q)


def _sum_blocks(a, n):
    M = a.shape[0] // n
    C = a.shape[1]

    def body(a_ref, o_ref):
        s = a_ref[0:M]
        for j in range(1, n):
            s = s + a_ref[j * M:(j + 1) * M]
        o_ref[...] = s

    return pl.pallas_call(
        body,
        name="sum_blocks",
        out_shape=jax.ShapeDtypeStruct((M, C), F32),
        compiler_params=pltpu.CompilerParams(vmem_limit_bytes=VMEM_LIMIT),
    )(a)


def _adamw(w, g, m, v):
    R, C = w.shape
    tr = _row_tile(R, C, 7) if R % 8 == 0 else R

    def body(w_ref, g_ref, m_ref, v_ref, d_ref, nm_ref, nv_ref):
        g = g_ref[...]
        m = ADAM_B1 * m_ref[...] + (1.0 - ADAM_B1) * g
        v = ADAM_B2 * v_ref[...] + (1.0 - ADAM_B2) * (g * g)
        m_hat = m / (1.0 - ADAM_B1 ** ADAM_STEP)
        v_hat = v / (1.0 - ADAM_B2 ** ADAM_STEP)
        d_ref[...] = -ADAM_LR * (m_hat / (jnp.sqrt(v_hat) + ADAM_EPS) + ADAM_WD * w_ref[...])
        nm_ref[...] = m
        nv_ref[...] = v

    spec = pl.BlockSpec((tr, C), lambda i: (i, 0))
    return pl.pallas_call(
        body,
        name="adamw",
        grid=(R // tr,),
        in_specs=[spec] * 4,
        out_specs=[spec] * 3,
        out_shape=[jax.ShapeDtypeStruct((R, C), F32)] * 3,
        compiler_params=_params("arbitrary"),
    )(w, g, m, v)


def _position():
    return lax.axis_index("x"), lax.axis_index("y"), lax.axis_index("c")


def _all_gather(block, name):
    m_per, n = block.shape

    def body(x_ref, out_ref, send_sems, recv_sems, local_sem):
        x, y, c = _position()
        me, sibling = (x, y, c), (x, y, 1 - c)
        chips = [(1 - x, y), (x, 1 - y), (1 - x, 1 - y)]

        def rows(px, py, pc):
            return out_ref.at[pl.ds((4 * px + 2 * py + pc) * m_per, m_per), :]

        def copy(k, blk, to, src=None):
            return pltpu.make_async_remote_copy(
                src_ref=rows(*blk) if src is None else src, dst_ref=rows(*blk),
                send_sem=send_sems.at[k], recv_sem=recv_sems.at[k], device_id=to, device_id_type=MESH)

        mine = pltpu.make_async_copy(x_ref, rows(*me), local_sem)
        mine.start()
        first = [copy(0, me, sibling, src=x_ref)]
        first += [copy(1 + j, me, (*chip, c), src=x_ref) for j, chip in enumerate(chips)]
        for cp in first:
            cp.start()
        passed = [copy(4 + j, (*chip, c), sibling) for j, chip in enumerate(chips)]
        for j, chip in enumerate(chips):
            copy(1 + j, (*chip, c), me).wait_recv()
            passed[j].start()
        copy(0, sibling, me).wait_recv()
        for j, chip in enumerate(chips):
            copy(4 + j, (*chip, 1 - c), me).wait_recv()
        for cp in first + passed:
            cp.wait_send()
        mine.wait()

    return pl.pallas_call(
        body,
        name=name,
        out_shape=jax.ShapeDtypeStruct((N_DEV * m_per, n), block.dtype),
        in_specs=[pl.BlockSpec(memory_space=pltpu.VMEM)],
        out_specs=pl.BlockSpec(memory_space=pltpu.VMEM),
        scratch_shapes=[pltpu.SemaphoreType.DMA((7,)), pltpu.SemaphoreType.DMA((7,)), pltpu.SemaphoreType.DMA],
        compiler_params=pltpu.CompilerParams(vmem_limit_bytes=VMEM_LIMIT),
    )(block)


def _gather_weights(shards, col_sharded):
    n = len(shards)
    half = [s.shape[1] // 2 for s in shards]

    def full_shape(i):
        L, R, C = shards[i].shape
        return (L, R, N_CHIP * C) if col_sharded[i] else (L, N_CHIP * R, C)

    def body(*refs):
        ins, outs = refs[:n], refs[n:2 * n]
        send_sems, recv_sems, local_sems = refs[2 * n:]
        x, y, c = _position()

        def region(i, chip, h):
            _, R, C = shards[i].shape
            if col_sharded[i]:
                return outs[i].at[:, pl.ds(h * half[i], half[i]), pl.ds(chip * C, C)]
            return outs[i].at[:, pl.ds(chip * R + h * half[i], half[i]), :]

        def run(kx, ky, kc):
            k_me = 2 * kx + ky
            sibling = (kx, ky, 1 - kc)
            chips = [(1 - kx, ky), (kx, 1 - ky), (1 - kx, 1 - ky)]
            local, first, passed, arrive_ici, arrive_d2d = [], [], [], [], []
            for i in range(n):
                for h in range(2):
                    local.append(pltpu.make_async_copy(
                        ins[i].at[:, pl.ds(h * half[i], half[i]), :], region(i, k_me, h), local_sems.at[2 * i + h]))
                for j, (px, py) in enumerate(chips):
                    s = 6 * i + j
                    first.append(pltpu.make_async_remote_copy(
                        src_ref=ins[i].at[:, pl.ds(kc * half[i], half[i]), :], dst_ref=region(i, k_me, kc),
                        send_sem=send_sems.at[s], recv_sem=recv_sems.at[s], device_id=(px, py, kc), device_id_type=MESH))
                    got = region(i, 2 * px + py, kc)
                    arrive_ici.append(pltpu.make_async_remote_copy(
                        src_ref=got, dst_ref=got, send_sem=send_sems.at[s], recv_sem=recv_sems.at[s],
                        device_id=(px, py, kc), device_id_type=MESH))
                    passed.append(pltpu.make_async_remote_copy(
                        src_ref=got, dst_ref=got, send_sem=send_sems.at[s + 3], recv_sem=recv_sems.at[s + 3],
                        device_id=sibling, device_id_type=MESH))
                    other = region(i, 2 * px + py, 1 - kc)
                    arrive_d2d.append(pltpu.make_async_remote_copy(
                        src_ref=other, dst_ref=other, send_sem=send_sems.at[s + 3], recv_sem=recv_sems.at[s + 3],
                        device_id=sibling, device_id_type=MESH))
            for cp in local + first:
                cp.start()
            for a, p in zip(arrive_ici, passed):
                a.wait_recv()
                p.start()
            for a in arrive_d2d:
                a.wait_recv()
            for cp in first + passed:
                cp.wait_send()
            for cp in local:
                cp.wait()

        for kx in range(2):
            for ky in range(2):
                for kc in range(2):
                    pl.when((x == kx) & (y == ky) & (c == kc))(functools.partial(run, kx, ky, kc))

    any_spec = pl.BlockSpec(memory_space=pl.ANY)
    return pl.pallas_call(
        body,
        name="gather_weights",
        out_shape=[jax.ShapeDtypeStruct(full_shape(i), shards[i].dtype) for i in range(n)],
        in_specs=[any_spec] * n,
        out_specs=[any_spec] * n,
        scratch_shapes=[pltpu.SemaphoreType.DMA((6 * n,)), pltpu.SemaphoreType.DMA((6 * n,)),
                        pltpu.SemaphoreType.DMA((2 * n,))],
    )(*shards)


def _send_sibling_half(gs):
    n = len(gs)

    def body(*refs):
        ins, outs = refs[:n], refs[n:2 * n]
        send_sems, recv_sems = refs[2 * n:]
        x, y, c = _position()
        copies = [pltpu.make_async_remote_copy(
            src_ref=ins[i].at[:, :, 1 - c], dst_ref=outs[i], send_sem=send_sems.at[i], recv_sem=recv_sems.at[i],
            device_id=(x, y, 1 - c), device_id_type=MESH) for i in range(n)]
        for cp in copies:
            cp.start()
        for cp in copies:
            cp.wait()

    any_spec = pl.BlockSpec(memory_space=pl.ANY)
    return pl.pallas_call(
        body,
        name="send_sibling_half",
        out_shape=[jax.ShapeDtypeStruct(g.shape[:2] + g.shape[3:], g.dtype) for g in gs],
        in_specs=[any_spec] * n,
        out_specs=[any_spec] * n,
        scratch_shapes=[pltpu.SemaphoreType.DMA((n,)), pltpu.SemaphoreType.DMA((n,))],
    )(*gs)


def _scatter_to_chips(ps):
    n = len(ps)

    def body(*refs):
        ins, outs = refs[:n], refs[n:2 * n]
        send_sems, recv_sems, local_sems = refs[2 * n:]
        x, y, c = _position()
        k_me = 2 * x + y
        chips = [(1 - x, y), (x, 1 - y), (1 - x, 1 - y)]
        local = [pltpu.make_async_copy(ins[i].at[:, k_me], outs[i].at[:, k_me], local_sems.at[i]) for i in range(n)]
        copies = [pltpu.make_async_remote_copy(
            src_ref=ins[i].at[:, 2 * px + py], dst_ref=outs[i].at[:, k_me],
            send_sem=send_sems.at[3 * i + j], recv_sem=recv_sems.at[3 * i + j],
            device_id=(px, py, c), device_id_type=MESH) for i in range(n) for j, (px, py) in enumerate(chips)]
        for cp in local + copies:
            cp.start()
        for cp in copies:
            cp.wait()
        for cp in local:
            cp.wait()

    any_spec = pl.BlockSpec(memory_space=pl.ANY)
    return pl.pallas_call(
        body,
        name="scatter_to_chips",
        out_shape=[jax.ShapeDtypeStruct(p.shape, p.dtype) for p in ps],
        in_specs=[any_spec] * n,
        out_specs=[any_spec] * n,
        scratch_shapes=[pltpu.SemaphoreType.DMA((3 * n,)), pltpu.SemaphoreType.DMA((3 * n,)),
                        pltpu.SemaphoreType.DMA((n,))],
    )(*ps)


def _share_with_sibling(rs):
    n = len(rs)

    def body(*refs):
        outs = refs[n:2 * n]
        send_sems, recv_sems = refs[2 * n:]
        x, y, c = _position()
        sends = [pltpu.make_async_remote_copy(
            src_ref=outs[i].at[:, c], dst_ref=outs[i].at[:, c], send_sem=send_sems.at[i], recv_sem=recv_sems.at[i],
            device_id=(x, y, 1 - c), device_id_type=MESH) for i in range(n)]
        arrivals = [pltpu.make_async_remote_copy(
            src_ref=outs[i].at[:, 1 - c], dst_ref=outs[i].at[:, 1 - c], send_sem=send_sems.at[i],
            recv_sem=recv_sems.at[i], device_id=(x, y, 1 - c), device_id_type=MESH) for i in range(n)]
        for cp in sends:
            cp.start()
        for cp in arrivals:
            cp.wait_recv()
        for cp in sends:
            cp.wait_send()

    any_spec = pl.BlockSpec(memory_space=pl.ANY)
    return pl.pallas_call(
        body,
        name="share_with_sibling",
        out_shape=[jax.ShapeDtypeStruct(r.shape, r.dtype) for r in rs],
        in_specs=[any_spec] * n,
        out_specs=[any_spec] * n,
        input_output_aliases={i: i for i in range(n)},
        scratch_shapes=[pltpu.SemaphoreType.DMA((n,)), pltpu.SemaphoreType.DMA((n,))],
    )(*rs)


def kernel(x, c, ada_w, ada_b, norm_ffn1_g, ffn1_w_gu, ffn1_w_down, norm_mix_g, mix_w_in, sgu_ln_g, sgu_ln_b, sgu_w_s, sgu_b, conv_w, out_norm_g, mix_w_out, norm_ffn2_g, ffn2_w_gu, ffn2_w_down, final_norm_g, loss_target, m_ada_w, m_ada_b, m_norm_ffn1_g, m_ffn1_w_gu, m_ffn1_w_down, m_norm_mix_g, m_mix_w_in, m_sgu_ln_g, m_sgu_ln_b, m_sgu_w_s, m_sgu_b, m_conv_w, m_out_norm_g, m_mix_w_out, m_norm_ffn2_g, m_ffn2_w_gu, m_ffn2_w_down, m_final_norm_g, v_ada_w, v_ada_b, v_norm_ffn1_g, v_ffn1_w_gu, v_ffn1_w_down, v_norm_mix_g, v_mix_w_in, v_sgu_ln_g, v_sgu_ln_b, v_sgu_w_s, v_sgu_b, v_conv_w, v_out_norm_g, v_mix_w_out, v_norm_ffn2_g, v_ffn2_w_gu, v_ffn2_w_down, v_final_norm_g):
    weights = dict(ada_w=ada_w, ada_b=ada_b, norm_ffn1_g=norm_ffn1_g, ffn1_w_gu=ffn1_w_gu, ffn1_w_down=ffn1_w_down,
                   norm_mix_g=norm_mix_g, mix_w_in=mix_w_in, sgu_ln_g=sgu_ln_g, sgu_ln_b=sgu_ln_b, sgu_w_s=sgu_w_s,
                   sgu_b=sgu_b, conv_w=conv_w, out_norm_g=out_norm_g, mix_w_out=mix_w_out, norm_ffn2_g=norm_ffn2_g,
                   ffn2_w_gu=ffn2_w_gu, ffn2_w_down=ffn2_w_down, final_norm_g=final_norm_g)
    m_in = dict(ada_w=m_ada_w, ada_b=m_ada_b, norm_ffn1_g=m_norm_ffn1_g, ffn1_w_gu=m_ffn1_w_gu,
                ffn1_w_down=m_ffn1_w_down, norm_mix_g=m_norm_mix_g, mix_w_in=m_mix_w_in, sgu_ln_g=m_sgu_ln_g,
                sgu_ln_b=m_sgu_ln_b, sgu_w_s=m_sgu_w_s, sgu_b=m_sgu_b, conv_w=m_conv_w, out_norm_g=m_out_norm_g,
                mix_w_out=m_mix_w_out, norm_ffn2_g=m_norm_ffn2_g, ffn2_w_gu=m_ffn2_w_gu, ffn2_w_down=m_ffn2_w_down,
                final_norm_g=m_final_norm_g)
    v_in = dict(ada_w=v_ada_w, ada_b=v_ada_b, norm_ffn1_g=v_norm_ffn1_g, ffn1_w_gu=v_ffn1_w_gu,
                ffn1_w_down=v_ffn1_w_down, norm_mix_g=v_norm_mix_g, mix_w_in=v_mix_w_in, sgu_ln_g=v_sgu_ln_g,
                sgu_ln_b=v_sgu_ln_b, sgu_w_s=v_sgu_w_s, sgu_b=v_sgu_b, conv_w=v_conv_w, out_norm_g=v_out_norm_g,
                mix_w_out=v_mix_w_out, norm_ffn2_g=v_norm_ffn2_g, ffn2_w_gu=v_ffn2_w_gu, ffn2_w_down=v_ffn2_w_down,
                final_norm_g=v_final_norm_g)

    B, S, D = x.shape
    T = B * S
    L = ada_w.shape[0]
    F = ffn1_w_down.shape[1] * N_CHIP
    P = mix_w_in.shape[2] * N_CHIP
    DA = D // 2
    DB = D - DA
    HD = DA // N_HEADS
    SA = ada_w.shape[2]
    n_all = B * N_DEV
    mx, my, mc = _position()
    chip = 2 * mx + my
    dev = 2 * chip + mc
    core = jnp.reshape(mc, (1,)).astype(jnp.int32)

    c_all = _all_gather(c.reshape(8, B * D // 8), "gather_c").reshape(n_all, D)
    ada_b_mine = lax.dynamic_slice_in_dim(ada_b, chip * SA, SA, axis=1).reshape(L, 1, SA)
    c_act, ada_part = _ada_fwd(c_all, ada_w, ada_b_mine)
    ada_all = _all_gather(ada_part.reshape(L * n_all, SA), "gather_ada").reshape(N_CHIP, 2, L, n_all, SA)[:, 0]
    ada_all = jnp.transpose(ada_all, (1, 2, 0, 3)).reshape(L, n_all, N_CHIP * SA)
    ada = lax.dynamic_slice_in_dim(ada_all, dev * B, B, axis=1).reshape(L, B, N_MOD, 1, D)
    mods = [[ada[l, :, j] for j in range(N_MOD)] for l in range(L)]

    cw_block = jnp.pad(conv_w.reshape(L * conv_w.shape[1], conv_w.shape[2]), ((0, 8 - L * conv_w.shape[1]), (0, 0)))
    cw_all = _all_gather(cw_block, "gather_conv_w").reshape(N_CHIP, 2, 8, conv_w.shape[2])[:, 0, :L * conv_w.shape[1]]
    conv_full = jnp.transpose(cw_all.reshape(N_CHIP, L, conv_w.shape[1], conv_w.shape[2]), (1, 2, 0, 3))
    conv_full = conv_full.reshape(L, conv_w.shape[1], DB)
    big = ["ffn1_w_gu", "ffn1_w_down", "mix_w_in", "mix_w_out", "ffn2_w_gu", "ffn2_w_down"]
    col_sharded = [True, False, True, False, True, False]
    wgu1, wd1, win, wout, wgu2, wd2 = _gather_weights([weights[k].astype(BF16) for k in big], col_sharded)

    x0 = x.reshape(T, D)
    gains = lambda name, l: weights[name][l].reshape(1, D)
    hmask = jnp.repeat(jnp.eye(N_HEADS, dtype=F32), HD, axis=0)
    pmat = (jnp.repeat(hmask, HD, axis=1) / HD).astype(BF16)

    def mix_consts(l):
        lng = jnp.tile(sgu_ln_g[l], N_HEADS).reshape(1, DA)
        lnb = jnp.tile(sgu_ln_b[l], N_HEADS).reshape(1, DA)
        wst = sgu_w_s[l].reshape(N_HEADS * CHUNK, CHUNK)
        wstt = jnp.swapaxes(sgu_w_s[l], 1, 2).reshape(N_HEADS * CHUNK, CHUNK)
        bias = jnp.repeat(jnp.transpose(sgu_b[l]), HD, axis=1)
        return lng, lnb, wst, wstt, bias

    saved = []
    xc = x0
    for l in range(L):
        sh1, sc1, g1, sh2, sc2, g2, sh3, sc3, g3 = mods[l]
        lng, lnb, wst, wstt, bias = mix_consts(l)
        xa, gu1, h1, f1 = _ffn_fwd(xc, gains("norm_ffn1_g", l), sh1, sc1, g1, wgu1, wd1, l)
        proj, h2 = _mixin_fwd(xa, gains("norm_mix_g", l), sh2, sc2, win, l)
        xb, yn = _mix_core_fwd(proj, xa, g2, wout, l, lng, lnb, wst, bias, pmat, conv_full[l],
                               gains("out_norm_g", l))
        xd, gu2, h3, f2 = _ffn_fwd(xb, gains("norm_ffn2_g", l), sh3, sc3, g3, wgu2, wd2, l)
        saved.append(dict(x0=xc, xa=xa, xb=xb, gu1=gu1, h1=h1, f1=f1, proj=proj, h2=h2, yn=yn, gu2=gu2, h3=h3, f2=f2))
        xc = xd

    dx, loss_block, d_final = _loss_head(xc, loss_target.reshape(T, D), final_norm_g.reshape(1, D))
    loss = lax.psum(loss_block[0, 0], ("x", "y", "c"))

    gbig = dict.fromkeys(big)
    small = [None] * L
    d_ada = [None] * L
    for l in reversed(range(L)):
        sh1, sc1, g1, sh2, sc2, g2, sh3, sc3, g3 = mods[l]
        lng, lnb, wst, wstt, bias = mix_consts(l)
        s = saved[l]
        dx, dgu, a, df, dg3, dsc3, dsh3, dgain3 = _ffn_bwd(
            dx, s["xb"], s["gu2"], s["f2"], gains("norm_ffn2_g", l), sc3, g3, wgu2, wd2, l)
        gbig["ffn2_w_gu"] = _wgrad(s["h3"], dgu, l, L, gbig["ffn2_w_gu"], D // 2, 2 * F // N_CHIP, True, "wgrad_gu")
        gbig["ffn2_w_down"] = _wgrad(a, df[None], l, L, gbig["ffn2_w_down"], F // 2, _tile(512, D), False, "wgrad_down")
        dproj, d_o, dg2, dog, dwst, dbias, dlng, dlnb, dconvw = _mix_core_bwd(
            s["proj"], dx, g2, wout, l, lng, lnb, wst, wstt, bias, pmat, conv_full[l], gains("out_norm_g", l))
        gbig["mix_w_out"] = _wgrad(s["yn"], d_o[None], l, L, gbig["mix_w_out"], D, _tile(512, D), False, "wgrad_out")
        dx, dsc2, dsh2, dgain2 = _mixin_bwd(dx, s["xa"], dproj, gains("norm_mix_g", l), sc2, win, l)
        gbig["mix_w_in"] = _wgrad(s["h2"], dproj[None], l, L, gbig["mix_w_in"], D // 2, P // N_CHIP, True, "wgrad_in")
        dx, dgu, a, df, dg1, dsc1, dsh1, dgain1 = _ffn_bwd(
            dx, s["x0"], s["gu1"], s["f1"], gains("norm_ffn1_g", l), sc1, g1, wgu1, wd1, l)
        gbig["ffn1_w_gu"] = _wgrad(s["h1"], dgu, l, L, gbig["ffn1_w_gu"], D // 2, 2 * F // N_CHIP, True, "wgrad_gu")
        gbig["ffn1_w_down"] = _wgrad(a, df[None], l, L, gbig["ffn1_w_down"], F // 2, _tile(512, D), False, "wgrad_down")
        d_ada[l] = jnp.concatenate([dsh1, dsc1, dg1, dsh2, dsc2, dg2, dsh3, dsc3, dg3], axis=1).reshape(B, N_MOD * D)
        small[l] = [dgain1, dgain2, dgain3, dog, dlng, dlnb, dwst, dbias, dconvw]
    grad_x = dx.reshape(B, S, D)

    d_ada_all = _all_gather(jnp.stack(d_ada).reshape(L * B, N_MOD * D), "gather_d_ada")
    d_ada_all = jnp.transpose(d_ada_all.reshape(N_DEV, L, B, N_MOD * D), (1, 0, 2, 3)).reshape(L, n_all, N_MOD * D)
    g_ada_b = _colsum(d_ada_all).reshape(L, N_MOD * D)
    d_ada_mine = lax.dynamic_slice_in_dim(d_ada_all, chip * SA, SA, axis=2).astype(BF16)
    g_ada_w = None
    for l in range(L):
        g_ada_w = _wgrad(c_act, d_ada_mine[l][None], l, L, g_ada_w, D, _tile(768, SA), False, "wgrad_ada")
    g_ada_w = g_ada_w.reshape(L, D, SA)

    def halves(name, g):
        if g.ndim == 5:
            return g
        return g.reshape(L, N_CHIP, 2, weights[name].shape[1] // 2, g.shape[-1])

    gs = [halves(k, gbig[k]) for k in big]
    recv = _send_sibling_half(gs)
    ps = [_pair_sum(g.reshape((L * N_CHIP,) + g.shape[2:]), r.reshape((L * N_CHIP,) + r.shape[2:]), core)
          .reshape(r.shape) for g, r in zip(gs, recv)]
    qs = _scatter_to_chips(ps)
    rs = _share_with_sibling([_chip_sum(q, core) for q in qs])
    grads = {k: r.reshape(weights[k].shape) for k, r in zip(big, rs)}
    grads["ada_w"] = g_ada_w
    grads["ada_b"] = g_ada_b

    flat = [a.reshape(-1, 128) for l in range(L) for a in small[l]] + [d_final.reshape(-1, 128)]
    n_rows = sum(a.shape[0] for a in flat)
    pad = (-n_rows) % 8
    packed = jnp.concatenate(flat + [jnp.zeros((pad, 128), F32)], axis=0)
    total = _sum_blocks(_all_gather(packed, "gather_small"), N_DEV)
    pieces, at = [], 0
    for a in flat:
        pieces.append(total[at:at + a.shape[0]])
        at += a.shape[0]
    per_layer = len(small[0])
    stack = lambda j, shape: jnp.stack([pieces[l * per_layer + j].reshape(shape) for l in range(L)])
    grads["norm_ffn1_g"] = stack(0, (D,))
    grads["norm_mix_g"] = stack(1, (D,))
    grads["norm_ffn2_g"] = stack(2, (D,))
    grads["out_norm_g"] = stack(3, (D,))
    grads["sgu_ln_g"] = stack(4, (N_HEADS, HD)).sum(axis=1)
    grads["sgu_ln_b"] = stack(5, (N_HEADS, HD)).sum(axis=1)
    grads["sgu_w_s"] = stack(6, (N_HEADS, CHUNK, CHUNK))
    grads["sgu_b"] = jnp.swapaxes(stack(7, (CHUNK, DA))[:, :, ::HD], 1, 2)
    g_conv = stack(8, (8, DB))[:, :conv_w.shape[1]]
    grads["conv_w"] = lax.dynamic_slice_in_dim(g_conv, chip * conv_w.shape[2], conv_w.shape[2], axis=2)
    grads["final_norm_g"] = pieces[-1].reshape(D)

    names = list(weights)
    delta, new_m, new_v = {}, {}, {}
    for k in names:
        w = weights[k]
        view = (1, w.shape[0]) if w.ndim == 1 else (-1, w.shape[-1])
        d, nm, nv = _adamw(w.reshape(view), grads[k].reshape(view), m_in[k].reshape(view), v_in[k].reshape(view))
        delta[k], new_m[k], new_v[k] = d.reshape(w.shape), nm.reshape(w.shape), nv.reshape(w.shape)

    return (loss, grad_x, *[grads[k] for k in names], *[delta[k] for k in names],
            *[new_m[k] for k in names], *[new_v[k] for k in names])
```

```python
import functools
import math

import jax
import jax.numpy as jnp
from jax import lax
from jax.experimental import pallas as pl
from jax.experimental.pallas import tpu as pltpu

F32 = jnp.float32
BF16 = jnp.bfloat16
MESH = pl.DeviceIdType.MESH

N_HEADS = 8
CHUNK = 128
N_MOD = 9
EPS = 1e-6
N_DEV = 8
N_CHIP = 4

ADAM_LR = 0.001
ADAM_B1 = 0.9
ADAM_B2 = 0.999
ADAM_EPS = 1e-08
ADAM_WD = 0.01
ADAM_STEP = 10

TOKEN_TILE = 512
FF_TILE = 1408
MIX_TILE = 256
WGRAD_TOKENS = 1024
VMEM_LIMIT = 52 * 1024 * 1024


def _tile(pref, n):
    t = min(pref, n)
    assert n % t == 0, (pref, n)
    return t


def _params(*sem):
    return pltpu.CompilerParams(dimension_semantics=sem, vmem_limit_bytes=VMEM_LIMIT)


def _dot(a, b):
    return jnp.dot(a, b, preferred_element_type=F32)


def _dot_nt(a, b):
    return lax.dot_general(a, b, (((1,), (1,)), ((), ())), preferred_element_type=F32)


def _dot_tn(a, b):
    return lax.dot_general(a, b, (((0,), (0,)), ((), ())), preferred_element_type=F32)


def _sigmoid(x):
    return 1.0 / (1.0 + jnp.exp(-x))


def _rms(x):
    r = lax.rsqrt(jnp.mean(x * x, axis=-1, keepdims=True) + EPS)
    return x * r, r


def _norm_mod_bwd(x, dh, gain, sc):
    xh, r = _rms(x)
    dsc = jnp.sum(dh * (xh * gain), axis=0, keepdims=True)
    dsh = jnp.sum(dh, axis=0, keepdims=True)
    dn = dh * (1.0 + sc)
    dgain = jnp.sum(dn * xh, axis=0, keepdims=True)
    dy = dn * gain
    dx = r * (dy - xh * jnp.mean(dy * xh, axis=-1, keepdims=True))
    return dx, dsc, dsh, dgain


def _acc(ref, first, val):
    @pl.when(first)
    def _():
        ref[...] = val

    @pl.when(jnp.logical_not(first))
    def _():
        ref[...] += val


def _ffn_up(x, gain, sh, sc, wgu, l):
    T, D = x.shape
    F = wgu.shape[2] // 2
    B = sh.shape[0]
    tm = _tile(TOKEN_TILE, T // B)
    tf = _tile(FF_TILE, F)
    tps = (T // B) // tm
    nf = F // tf

    def body(x_ref, gain_ref, sh_ref, sc_ref, wg_ref, wu_ref, gu_ref, a_ref, h_ref, hs):
        @pl.when(pl.program_id(1) == 0)
        def _():
            xh, _ = _rms(x_ref[...])
            h = (xh * gain_ref[...] * (1.0 + sc_ref[0]) + sh_ref[0]).astype(BF16)
            hs[...] = h
            h_ref[...] = h

        g = _dot(hs[...], wg_ref[...])
        u = _dot(hs[...], wu_ref[...])
        gu_ref[0] = g.astype(BF16)
        gu_ref[1] = u.astype(BF16)
        a_ref[...] = (g * _sigmoid(g) * u).astype(BF16)

    seq = lambda i, k: (i // tps, 0, 0)
    return pl.pallas_call(
        body,
        name="ffn_up",
        grid=(T // tm, nf),
        in_specs=[
            pl.BlockSpec((tm, D), lambda i, k: (i, 0)),
            pl.BlockSpec((1, D), lambda i, k: (0, 0)),
            pl.BlockSpec((1, 1, D), seq),
            pl.BlockSpec((1, 1, D), seq),
            pl.BlockSpec((None, D, tf), lambda i, k: (l, 0, k)),
            pl.BlockSpec((None, D, tf), lambda i, k: (l, 0, nf + k)),
        ],
        out_specs=[
            pl.BlockSpec((2, tm, tf), lambda i, k: (0, i, k)),
            pl.BlockSpec((tm, tf), lambda i, k: (i, k)),
            pl.BlockSpec((tm, D), lambda i, k: (i, 0)),
        ],
        out_shape=[
            jax.ShapeDtypeStruct((2, T, F), BF16),
            jax.ShapeDtypeStruct((T, F), BF16),
            jax.ShapeDtypeStruct((T, D), BF16),
        ],
        scratch_shapes=[pltpu.VMEM((tm, D), BF16)],
        compiler_params=_params("arbitrary", "arbitrary"),
    )(x, gain, sh, sc, wgu, wgu)


def _ffn_down(a, x, gate, wd, l):
    T, F = a.shape
    D = x.shape[1]
    B = gate.shape[0]
    tm = _tile(TOKEN_TILE, T // B)
    tps = (T // B) // tm

    def body(a_ref, x_ref, gate_ref, wd_ref, xo_ref, f_ref):
        f = _dot(a_ref[...], wd_ref[...])
        f_ref[...] = f.astype(BF16)
        xo_ref[...] = x_ref[...] + 0.5 * gate_ref[0] * f

    return pl.pallas_call(
        body,
        name="ffn_down",
        grid=(T // tm,),
        in_specs=[
            pl.BlockSpec((tm, F), lambda i: (i, 0)),
            pl.BlockSpec((tm, D), lambda i: (i, 0)),
            pl.BlockSpec((1, 1, D), lambda i: (i // tps, 0, 0)),
            pl.BlockSpec((None, F, D), lambda i: (l, 0, 0)),
        ],
        out_specs=[pl.BlockSpec((tm, D), lambda i: (i, 0)), pl.BlockSpec((tm, D), lambda i: (i, 0))],
        out_shape=[jax.ShapeDtypeStruct((T, D), F32), jax.ShapeDtypeStruct((T, D), BF16)],
        compiler_params=_params("arbitrary"),
    )(a, x, gate, wd)


def _ffn_bwd_down(dxo, f, gu, gate, wd, l):
    T, D = dxo.shape
    F = wd.shape[1]
    B = gate.shape[0]
    tm = _tile(TOKEN_TILE, T // B)
    tf = _tile(FF_TILE, F)
    tps = (T // B) // tm
    nf = F // tf

    def body(dxo_ref, f_ref, gu_ref, gate_ref, wd_ref, dgu_ref, df_ref, dgate_ref, dfs):
        @pl.when(pl.program_id(1) == 0)
        def _():
            dxo = dxo_ref[...]
            df = (0.5 * gate_ref[0] * dxo).astype(BF16)
            dfs[...] = df
            df_ref[...] = df
            _acc(dgate_ref.at[0], (pl.program_id(0) % tps) == 0,
                 0.5 * jnp.sum(dxo * f_ref[...].astype(F32), axis=0, keepdims=True))

        da = _dot_nt(dfs[...], wd_ref[...])
        g = gu_ref[0].astype(F32)
        sg = _sigmoid(g)
        dgu_ref[1] = (da * (g * sg)).astype(BF16)
        dgu_ref[0] = (da * gu_ref[1].astype(F32) * (sg * (1.0 + g * (1.0 - sg)))).astype(BF16)

    seq = lambda i, k: (i // tps, 0, 0)
    row = lambda i, k: (i, 0)
    return pl.pallas_call(
        body,
        name="ffn_bwd_down",
        grid=(T // tm, nf),
        in_specs=[
            pl.BlockSpec((tm, D), row),
            pl.BlockSpec((tm, D), row),
            pl.BlockSpec((2, tm, tf), lambda i, k: (0, i, k)),
            pl.BlockSpec((1, 1, D), seq),
            pl.BlockSpec((None, tf, D), lambda i, k: (l, k, 0)),
        ],
        out_specs=[
            pl.BlockSpec((2, tm, tf), lambda i, k: (0, i, k)),
            pl.BlockSpec((tm, D), row),
            pl.BlockSpec((1, 1, D), seq),
        ],
        out_shape=[
            jax.ShapeDtypeStruct((2, T, F), BF16),
            jax.ShapeDtypeStruct((T, D), BF16),
            jax.ShapeDtypeStruct((B, 1, D), F32),
        ],
        scratch_shapes=[pltpu.VMEM((tm, D), BF16)],
        compiler_params=_params("arbitrary", "arbitrary"),
    )(dxo, f, gu, gate, wd)


def _ffn_bwd_up(dxo, x, dgu, gain, sc, wgu, l):
    T, D = x.shape
    F = wgu.shape[2] // 2
    B = sc.shape[0]
    tm = _tile(TOKEN_TILE, T // B)
    tps = (T // B) // tm

    def body(dxo_ref, x_ref, dgu_ref, gain_ref, sc_ref, w_ref, dx_ref, dsc_ref, dsh_ref, dgain_ref, acc):
        i = pl.program_id(0)
        k = pl.program_id(1)
        _acc(acc, k == 0, _dot_nt(dgu_ref[...], w_ref[...]))

        @pl.when(k == 1)
        def _():
            first_of_seq = (i % tps) == 0
            dx, dsc, dsh, dgain = _norm_mod_bwd(x_ref[...], acc[...], gain_ref[...], sc_ref[0])
            dx_ref[...] = dxo_ref[...] + dx
            _acc(dsc_ref.at[0], first_of_seq, dsc)
            _acc(dsh_ref.at[0], first_of_seq, dsh)
            _acc(dgain_ref, i == 0, dgain)

    seq = lambda i, k: (i // tps, 0, 0)
    row = lambda i, k: (i, 0)
    return pl.pallas_call(
        body,
        name="ffn_bwd_up",
        grid=(T // tm, 2),
        in_specs=[
            pl.BlockSpec((tm, D), row),
            pl.BlockSpec((tm, D), row),
            pl.BlockSpec((None, tm, F), lambda i, k: (k, i, 0)),
            pl.BlockSpec((1, D), lambda i, k: (0, 0)),
            pl.BlockSpec((1, 1, D), seq),
            pl.BlockSpec((None, D, F), lambda i, k: (l, 0, k)),
        ],
        out_specs=[
            pl.BlockSpec((tm, D), row),
            pl.BlockSpec((1, 1, D), seq),
            pl.BlockSpec((1, 1, D), seq),
            pl.BlockSpec((1, D), lambda i, k: (0, 0)),
        ],
        out_shape=[
            jax.ShapeDtypeStruct((T, D), F32),
            jax.ShapeDtypeStruct((B, 1, D), F32),
            jax.ShapeDtypeStruct((B, 1, D), F32),
            jax.ShapeDtypeStruct((1, D), F32),
        ],
        scratch_shapes=[pltpu.VMEM((tm, D), F32)],
        compiler_params=_params("arbitrary", "arbitrary"),
    )(dxo, x, dgu, gain, sc, wgu)


def _wgrad(a, b, l, n_layers, prev, tmm, tn, col_major, name, tokens=WGRAD_TOKENS):
    T, M = a.shape
    nb, _, Nb = b.shape
    N = nb * Nb
    tk = _tile(tokens, T)
    npb = Nb // tn
    assert M % tmm == 0 and Nb % tn == 0
    if col_major:
        assert tmm == M
        shape = (n_layers, N // tn, 2, M // 2, tn)
        out_spec = pl.BlockSpec((None, None, 2, M // 2, tn), lambda i, j, t: (l, j, 0, 0, 0))
    else:
        shape = (n_layers, M // tmm, tmm, N)
        out_spec = pl.BlockSpec((None, None, tmm, tn), lambda i, j, t: (l, i, 0, j))

    def body(a_ref, b_ref, *rest):
        o_ref = rest[-1]
        t = pl.program_id(2)
        res = _dot_tn(a_ref[...], b_ref[...])
        if col_major:
            _acc(o_ref.at[0], t == 0, res[:M // 2])
            _acc(o_ref.at[1], t == 0, res[M // 2:])
        else:
            _acc(o_ref, t == 0, res)

    in_specs = [
        pl.BlockSpec((tk, tmm), lambda i, j, t: (t, i)),
        pl.BlockSpec((None, tk, tn), lambda i, j, t: (j // npb, t, j % npb)),
    ]
    args = [a, b]
    aliases = {}
    if prev is not None:
        in_specs.append(pl.BlockSpec(memory_space=pl.ANY))
        args.append(prev)
        aliases = {2: 0}
    return pl.pallas_call(
        body,
        name=name,
        grid=(M // tmm, N // tn, T // tk),
        in_specs=in_specs,
        out_specs=out_spec,
        out_shape=jax.ShapeDtypeStruct(shape, F32),
        input_output_aliases=aliases,
        compiler_params=_params("arbitrary", "arbitrary", "arbitrary"),
    )(*args)


def _mixin_fwd(x, gain, sh, sc, win, l):
    T, D = x.shape
    P = win.shape[2]
    B = sh.shape[0]
    tm = _tile(TOKEN_TILE, T // B)
    tps = (T // B) // tm
    tn = P

    def body(x_ref, gain_ref, sh_ref, sc_ref, w_ref, proj_ref, h_ref, hs):
        @pl.when(pl.program_id(1) == 0)
        def _():
            xh, _ = _rms(x_ref[...])
            h = (xh * gain_ref[...] * (1.0 + sc_ref[0]) + sh_ref[0]).astype(BF16)
            hs[...] = h
            h_ref[...] = h

        proj_ref[...] = _dot(hs[...], w_ref[...])

    seq = lambda i, j: (i // tps, 0, 0)
    return pl.pallas_call(
        body,
        name="mixin_fwd",
        grid=(T // tm, P // tn),
        in_specs=[
            pl.BlockSpec((tm, D), lambda i, j: (i, 0)),
            pl.BlockSpec((1, D), lambda i, j: (0, 0)),
            pl.BlockSpec((1, 1, D), seq),
            pl.BlockSpec((1, 1, D), seq),
            pl.BlockSpec((None, D, tn), lambda i, j: (l, 0, j)),
        ],
        out_specs=[
            pl.BlockSpec((tm, tn), lambda i, j: (i, j)),
            pl.BlockSpec((tm, D), lambda i, j: (i, 0)),
        ],
        out_shape=[jax.ShapeDtypeStruct((T, P), F32), jax.ShapeDtypeStruct((T, D), BF16)],
        scratch_shapes=[pltpu.VMEM((tm, D), BF16)],
        compiler_params=_params("arbitrary", "arbitrary"),
    )(x, gain, sh, sc, win)


def _mixin_bwd(dxo, x, dproj, gain, sc, win, l):
    T, D = x.shape
    P = win.shape[2]
    B = sc.shape[0]
    tm = _tile(TOKEN_TILE, T // B)
    tps = (T // B) // tm
    tk = P
    nk = P // tk

    def body(dxo_ref, x_ref, dp_ref, gain_ref, sc_ref, w_ref, dx_ref, dsc_ref, dsh_ref, dgain_ref, acc):
        i = pl.program_id(0)
        k = pl.program_id(1)
        _acc(acc, k == 0, _dot_nt(dp_ref[...], w_ref[...]))

        @pl.when(k == nk - 1)
        def _():
            first_of_seq = (i % tps) == 0
            dx, dsc, dsh, dgain = _norm_mod_bwd(x_ref[...], acc[...], gain_ref[...], sc_ref[0])
            dx_ref[...] = dxo_ref[...] + dx
            _acc(dsc_ref.at[0], first_of_seq, dsc)
            _acc(dsh_ref.at[0], first_of_seq, dsh)
            _acc(dgain_ref, i == 0, dgain)

    seq = lambda i, k: (i // tps, 0, 0)
    row = lambda i, k: (i, 0)
    return pl.pallas_call(
        body,
        name="mixin_bwd",
        grid=(T // tm, nk),
        in_specs=[
            pl.BlockSpec((tm, D), row),
            pl.BlockSpec((tm, D), row),
            pl.BlockSpec((tm, tk), lambda i, k: (i, k)),
            pl.BlockSpec((1, D), lambda i, k: (0, 0)),
            pl.BlockSpec((1, 1, D), seq),
            pl.BlockSpec((None, D, tk), lambda i, k: (l, 0, k)),
        ],
        out_specs=[
            pl.BlockSpec((tm, D), row),
            pl.BlockSpec((1, 1, D), seq),
            pl.BlockSpec((1, 1, D), seq),
            pl.BlockSpec((1, D), lambda i, k: (0, 0)),
        ],
        out_shape=[
            jax.ShapeDtypeStruct((T, D), F32),
            jax.ShapeDtypeStruct((B, 1, D), F32),
            jax.ShapeDtypeStruct((B, 1, D), F32),
            jax.ShapeDtypeStruct((1, D), F32),
        ],
        scratch_shapes=[pltpu.VMEM((tm, D), F32)],
        compiler_params=_params("arbitrary", "arbitrary"),
    )(dxo, x, dproj, gain, sc, win)


def _head_mean(z, pmat):
    hi = z.astype(BF16)
    lo = (z - hi.astype(F32)).astype(BF16)
    return _dot(hi, pmat) + _dot(lo, pmat)


def _gelu_parts(x):
    cdf = 0.5 * (1.0 + lax.erf(x * (1.0 / math.sqrt(2.0))))
    return x * cdf, cdf


def _gelu_grad(x, cdf):
    return cdf + x * jnp.exp(-0.5 * x * x) * (1.0 / math.sqrt(2.0 * math.pi))


def _head_masks(da):
    hd = da // N_HEADS
    col = lax.broadcasted_iota(jnp.int32, (1, da), 1)
    return [(col >= h * hd) & (col < (h + 1) * hd) for h in range(N_HEADS)]


def _select_heads(res, masks):
    out = res[0:CHUNK]
    for h in range(1, N_HEADS):
        out = jnp.where(masks[h], res[h * CHUNK:(h + 1) * CHUNK], out)
    return out


def _causal_stack(w, transposed):
    r = lax.broadcasted_iota(jnp.int32, w.shape, 0) % CHUNK
    c = lax.broadcasted_iota(jnp.int32, w.shape, 1)
    keep = (c >= r) if transposed else (c <= r)
    return jnp.where(keep, w, 0.0)


def _mix_core_forward(proj, zprev, prm, da, db):
    n = proj.shape[0]
    ua = proj[:, 0:da]
    va = proj[:, da:2 * da]
    bg = proj[:, 2 * da:2 * da + db]
    cg = proj[:, 2 * da + db:2 * da + 2 * db]
    xb = proj[:, 2 * da + 2 * db:]
    ug, ucdf = _gelu_parts(ua)
    vg, vcdf = _gelu_parts(va)
    zc = vg - _head_mean(vg, prm["pmat"])
    rs = lax.rsqrt(_head_mean(zc * zc, prm["pmat"]) + EPS)
    vhat = zc * rs
    vln = (vhat * prm["lng"] + prm["lnb"]).astype(BF16)
    wst = _causal_stack(prm["wst"], False).astype(BF16)
    masks = _head_masks(da)
    mixed = []
    for j in range(n // CHUNK):
        res = _dot(wst, vln[j * CHUNK:(j + 1) * CHUNK])
        mixed.append(_select_heads(res, masks) + prm["bias"])
    mixed = mixed[0] if len(mixed) == 1 else jnp.concatenate(mixed, axis=0)
    ya = ug * mixed
    z = cg * xb
    row = lax.broadcasted_iota(jnp.int32, z.shape, 0)
    z1 = jnp.where(row == 0, zprev[7:8], pltpu.roll(z, 1, 0))
    z2 = jnp.where(row == 0, zprev[6:7], jnp.where(row == 1, zprev[7:8], pltpu.roll(z, 2, 0)))
    cw = prm["convw"]
    conv = z2 * cw[0:1] + z1 * cw[1:2] + z * cw[2:3]
    yb = bg * conv
    yah, ra = _rms(ya)
    ybh, rb = _rms(yb)
    return dict(ua=ua, va=va, bg=bg, cg=cg, xb=xb, ug=ug, ucdf=ucdf, vcdf=vcdf, rs=rs, vhat=vhat, vln=vln,
                mixed=mixed, z=z, z1=z1, z2=z2, conv=conv, yah=yah, ra=ra, ybh=ybh, rb=rb, masks=masks)


def _mix_params(lng_ref, lnb_ref, wst_ref, bias_ref, pmat_ref, convw_ref):
    return dict(lng=lng_ref[...], lnb=lnb_ref[...], wst=wst_ref[...], bias=bias_ref[...],
                pmat=pmat_ref[...], convw=convw_ref[...])


def _mix_core_fwd(proj, x, gate, wout, l, lng, lnb, wst, bias, pmat, convw, og):
    T, P = proj.shape
    D = x.shape[1]
    B = gate.shape[0]
    da = lng.shape[1]
    db = convw.shape[1]
    tm = _tile(MIX_TILE, T // B)
    tps = (T // B) // tm

    def body(proj_ref, x_ref, gate_ref, wout_ref, lng_ref, lnb_ref, wst_ref, bias_ref, pmat_ref, convw_ref,
             og_ref, xo_ref, yn_ref, halo):
        i = pl.program_id(0)

        @pl.when((i % tps) == 0)
        def _():
            halo[...] = jnp.zeros_like(halo)

        prm = _mix_params(lng_ref, lnb_ref, wst_ref, bias_ref, pmat_ref, convw_ref)
        r = _mix_core_forward(proj_ref[...], halo[...], prm, da, db)
        halo[...] = r["z"][tm - 8:tm]
        og = og_ref[...]
        yn_ref[:, 0:da] = (r["yah"] * og[:, 0:da]).astype(BF16)
        yn_ref[:, da:] = (r["ybh"] * og[:, da:]).astype(BF16)
        xo_ref[...] = x_ref[...] + gate_ref[0] * _dot(yn_ref[...], wout_ref[...])

    full = lambda a: pl.BlockSpec(a.shape, lambda i: (0,) * a.ndim)
    return pl.pallas_call(
        body,
        name="mix_core_fwd",
        grid=(T // tm,),
        in_specs=[
            pl.BlockSpec((tm, P), lambda i: (i, 0)),
            pl.BlockSpec((tm, D), lambda i: (i, 0)),
            pl.BlockSpec((1, 1, D), lambda i: (i // tps, 0, 0)),
            pl.BlockSpec((None, D, D), lambda i: (l, 0, 0)),
            full(lng), full(lnb), full(wst), full(bias), full(pmat), full(convw), full(og),
        ],
        out_specs=[
            pl.BlockSpec((tm, D), lambda i: (i, 0)),
            pl.BlockSpec((tm, D), lambda i: (i, 0)),
        ],
        out_shape=[jax.ShapeDtypeStruct((T, D), F32), jax.ShapeDtypeStruct((T, D), BF16)],
        scratch_shapes=[pltpu.VMEM((8, db), F32)],
        compiler_params=_params("arbitrary"),
    )(proj, x, gate, wout, lng, lnb, wst, bias, pmat, convw, og)


def _mix_core_bwd(proj, dxo, gate, wout, l, lng, lnb, wst, wstt, bias, pmat, convw, og):
    T, P = proj.shape
    D = dxo.shape[1]
    B = gate.shape[0]
    da = lng.shape[1]
    db = convw.shape[1]
    assert da == db and P == 2 * da + 3 * db
    tm = _tile(MIX_TILE, T // B)
    tps = (T // B) // tm
    nt = T // tm
    hd = da // N_HEADS

    def body(proj_ref, cgp_ref, xbp_ref, dxo_ref, gate_ref, wout_ref, lng_ref, lnb_ref, wst_ref, wstt_ref,
             bias_ref, pmat_ref, convw_ref, og_ref,
             dproj_ref, do_ref, dgate_ref, dog_ref, dwst_ref, dbias_ref, dlng_ref, dlnb_ref, dconvw_ref, carry):
        i = pl.program_id(0)
        ri = nt - 1 - i
        first = i == 0
        end_of_seq = (ri % tps) == tps - 1
        start_of_seq = (ri % tps) == 0

        @pl.when(end_of_seq)
        def _():
            carry[...] = jnp.zeros_like(carry)

        prm = _mix_params(lng_ref, lnb_ref, wst_ref, bias_ref, pmat_ref, convw_ref)
        zprev = jnp.where(start_of_seq, 0.0, cgp_ref[...] * xbp_ref[...])
        r = _mix_core_forward(proj_ref[...], zprev, prm, da, db)
        og = og_ref[...]
        pmat = prm["pmat"]

        yn = jnp.concatenate([(r["yah"] * og[:, 0:da]).astype(BF16), (r["ybh"] * og[:, da:]).astype(BF16)], axis=1)
        dxo = dxo_ref[...]
        o = _dot(yn, wout_ref[...])
        _acc(dgate_ref.at[0], end_of_seq, jnp.sum(dxo * o, axis=0, keepdims=True))
        d_o = (gate_ref[0] * dxo).astype(BF16)
        do_ref[...] = d_o
        dyn = _dot_nt(d_o, wout_ref[...])

        def rms_bwd(dyn_g, yh, rr, og_g):
            dog_g = jnp.sum(dyn_g * yh, axis=0, keepdims=True)
            dyh = dyn_g * og_g
            return rr * (dyh - yh * jnp.mean(dyh * yh, axis=-1, keepdims=True)), dog_g

        dya, dog_a = rms_bwd(dyn[:, 0:da], r["yah"], r["ra"], og[:, 0:da])
        dyb, dog_b = rms_bwd(dyn[:, da:], r["ybh"], r["rb"], og[:, da:])
        _acc(dog_ref, first, jnp.concatenate([dog_a, dog_b], axis=1))

        dug = dya * r["mixed"]
        dmixed = dya * r["ug"]
        wstt_b = _causal_stack(wstt_ref[...], True).astype(BF16)
        masks = r["masks"]
        dbias = jnp.zeros((CHUNK, da), F32)
        dwst = jnp.zeros((N_HEADS * CHUNK, CHUNK), F32)
        dvln = []
        for j in range(tm // CHUNK):
            dm = dmixed[j * CHUNK:(j + 1) * CHUNK]
            dbias = dbias + dm
            dmb = dm.astype(BF16)
            stack = jnp.concatenate([jnp.where(masks[h], dmb, jnp.zeros_like(dmb)) for h in range(N_HEADS)], axis=0)
            dwst = dwst + _dot_nt(stack, r["vln"][j * CHUNK:(j + 1) * CHUNK])
            dvln.append(_select_heads(_dot(wstt_b, dmb), masks))
        dvln = dvln[0] if len(dvln) == 1 else jnp.concatenate(dvln, axis=0)
        _acc(dbias_ref, first, dbias)
        _acc(dwst_ref, first, dwst)
        _acc(dlng_ref, first, jnp.sum(dvln * r["vhat"], axis=0, keepdims=True))
        _acc(dlnb_ref, first, jnp.sum(dvln, axis=0, keepdims=True))
        dvhat = dvln * prm["lng"]
        dvg = r["rs"] * (dvhat - _head_mean(dvhat, pmat) - r["vhat"] * _head_mean(dvhat * r["vhat"], pmat))
        dproj_ref[:, 0:da] = (dug * _gelu_grad(r["ua"], r["ucdf"])).astype(BF16)
        dproj_ref[:, da:2 * da] = (dvg * _gelu_grad(r["va"], r["vcdf"])).astype(BF16)

        dproj_ref[:, 2 * da:2 * da + db] = (dyb * r["conv"]).astype(BF16)
        dconv = dyb * r["bg"]
        dcw = jnp.concatenate([
            jnp.sum(dconv * r["z2"], axis=0, keepdims=True),
            jnp.sum(dconv * r["z1"], axis=0, keepdims=True),
            jnp.sum(dconv * r["z"], axis=0, keepdims=True),
            jnp.zeros((5, db), F32)], axis=0)
        _acc(dconvw_ref, first, dcw)
        nxt = carry[...]
        row = lax.broadcasted_iota(jnp.int32, dconv.shape, 0)
        dc1 = jnp.where(row == tm - 1, nxt[0:1], pltpu.roll(dconv, tm - 1, 0))
        dc2 = jnp.where(row == tm - 2, nxt[0:1], jnp.where(row == tm - 1, nxt[1:2], pltpu.roll(dconv, tm - 2, 0)))
        carry[...] = dconv[0:8]
        cw = prm["convw"]
        dz = dconv * cw[2:3] + dc1 * cw[1:2] + dc2 * cw[0:1]
        dproj_ref[:, 2 * da + db:2 * da + 2 * db] = (dz * r["xb"]).astype(BF16)
        dproj_ref[:, 2 * da + 2 * db:] = (dz * r["cg"]).astype(BF16)

        @pl.when(i == nt - 1)
        def _():
            dwst_ref[...] = _causal_stack(dwst_ref[...], False)
            dbias_ref[...] = _head_mean(dbias_ref[...], pmat) * float(hd)

    full = lambda a: pl.BlockSpec(a.shape, lambda i: (0,) * a.ndim)
    const = lambda i: (0, 0)
    rev = lambda i: (nt - 1 - i, 0)
    prev8 = lambda col: (lambda i: (jnp.maximum((nt - 1 - i) * (tm // 8) - 1, 0), col))
    return pl.pallas_call(
        body,
        name="mix_core_bwd",
        grid=(nt,),
        in_specs=[
            pl.BlockSpec((tm, P), rev),
            pl.BlockSpec((8, db), prev8((2 * da + db) // db)),
            pl.BlockSpec((8, db), prev8((2 * da + 2 * db) // db)),
            pl.BlockSpec((tm, D), rev),
            pl.BlockSpec((1, 1, D), lambda i: ((nt - 1 - i) // tps, 0, 0)),
            pl.BlockSpec((None, D, D), lambda i: (l, 0, 0)),
            full(lng), full(lnb), full(wst), full(wstt), full(bias), full(pmat), full(convw), full(og),
        ],
        out_specs=[
            pl.BlockSpec((tm, P), rev),
            pl.BlockSpec((tm, D), rev),
            pl.BlockSpec((1, 1, D), lambda i: ((nt - 1 - i) // tps, 0, 0)),
            pl.BlockSpec((1, D), const),
            pl.BlockSpec((N_HEADS * CHUNK, CHUNK), const),
            pl.BlockSpec((CHUNK, da), const),
            pl.BlockSpec((1, da), const),
            pl.BlockSpec((1, da), const),
            pl.BlockSpec((8, db), const),
        ],
        out_shape=[
            jax.ShapeDtypeStruct((T, P), BF16),
            jax.ShapeDtypeStruct((T, D), BF16),
            jax.ShapeDtypeStruct((B, 1, D), F32),
            jax.ShapeDtypeStruct((1, D), F32),
            jax.ShapeDtypeStruct((N_HEADS * CHUNK, CHUNK), F32),
            jax.ShapeDtypeStruct((CHUNK, da), F32),
            jax.ShapeDtypeStruct((1, da), F32),
            jax.ShapeDtypeStruct((1, da), F32),
            jax.ShapeDtypeStruct((8, db), F32),
        ],
        scratch_shapes=[pltpu.VMEM((8, db), F32)],
        compiler_params=_params("arbitrary"),
    )(proj, proj, proj, dxo, gate, wout, lng, lnb, wst, wstt, bias, pmat, convw, og)


def _loss_head(x, target, gain):
    T, D = x.shape
    tm = _tile(TOKEN_TILE, T)

    def body(x_ref, t_ref, gain_ref, dx_ref, loss_ref, dgain_ref):
        first = pl.program_id(0) == 0
        xh, r = _rms(x_ref[...])
        gain = gain_ref[...]
        err = xh * gain - t_ref[...]
        _acc(loss_ref, first, jnp.zeros((8, 128), F32) + 0.5 * jnp.sum(err * err) / D)
        dout = err * (1.0 / D)
        _acc(dgain_ref, first, jnp.sum(dout * xh, axis=0, keepdims=True))
        dy = dout * gain
        dx_ref[...] = r * (dy - xh * jnp.mean(dy * xh, axis=-1, keepdims=True))

    return pl.pallas_call(
        body,
        name="loss_head",
        grid=(T // tm,),
        in_specs=[
            pl.BlockSpec((tm, D), lambda i: (i, 0)),
            pl.BlockSpec((tm, D), lambda i: (i, 0)),
            pl.BlockSpec((1, D), lambda i: (0, 0)),
        ],
        out_specs=[
            pl.BlockSpec((tm, D), lambda i: (i, 0)),
            pl.BlockSpec((8, 128), lambda i: (0, 0)),
            pl.BlockSpec((1, D), lambda i: (0, 0)),
        ],
        out_shape=[
            jax.ShapeDtypeStruct((T, D), F32),
            jax.ShapeDtypeStruct((8, 128), F32),
            jax.ShapeDtypeStruct((1, D), F32),
        ],
        compiler_params=_params("arbitrary"),
    )(x, target, gain)


def _ada_fwd(c_all, ada_w, ada_b):
    n, D = c_all.shape
    L, _, sa = ada_w.shape
    tn = _tile(768, sa)

    def body(c_ref, w_ref, b_ref, act_ref, o_ref):
        c = c_ref[...]
        act = (c * _sigmoid(c)).astype(BF16)
        act_ref[...] = act
        o_ref[...] = _dot(act, w_ref[...].astype(BF16)) + b_ref[...]

    return pl.pallas_call(
        body,
        name="ada_fwd",
        grid=(L, sa // tn),
        in_specs=[
            pl.BlockSpec((n, D), lambda l, j: (0, 0)),
            pl.BlockSpec((None, D, tn), lambda l, j: (l, 0, j)),
            pl.BlockSpec((None, 1, tn), lambda l, j: (l, 0, j)),
        ],
        out_specs=[
            pl.BlockSpec((n, D), lambda l, j: (0, 0)),
            pl.BlockSpec((None, n, tn), lambda l, j: (l, 0, j)),
        ],
        out_shape=[jax.ShapeDtypeStruct((n, D), BF16), jax.ShapeDtypeStruct((L, n, sa), F32)],
        compiler_params=_params("arbitrary", "arbitrary"),
    )(c_all, ada_w, ada_b)


def _colsum(a):
    L, n, C = a.shape

    def body(a_ref, o_ref):
        o_ref[...] = jnp.sum(a_ref[...], axis=0, keepdims=True)

    return pl.pallas_call(
        body,
        name="colsum",
        grid=(L,),
        in_specs=[pl.BlockSpec((None, n, C), lambda l: (l, 0, 0))],
        out_specs=pl.BlockSpec((None, 1, C), lambda l: (l, 0, 0)),
        out_shape=jax.ShapeDtypeStruct((L, 1, C), F32),
        compiler_params=_params("arbitrary"),
    )(a)


def _row_tile(rows, cols, nbuf):
    budget = VMEM_LIMIT // 3 // (2 * nbuf * 4 * cols)
    t = rows
    while t > max(budget, 8) and t % 2 == 0 and (t // 2) % 8 == 0:
        t //= 2
    return t


def _pair_sum(g, recv, core):
    n, _, R, C = g.shape
    tr = _row_tile(R, C, 3)

    def body(core_ref, g_ref, r_ref, o_ref):
        o_ref[...] = (g_ref[...] + r_ref[...]).astype(BF16)

    return pl.pallas_call(
        body,
        name="pair_sum",
        grid_spec=pltpu.PrefetchScalarGridSpec(
            num_scalar_prefetch=1,
            grid=(n, R // tr),
            in_specs=[
                pl.BlockSpec((None, None, tr, C), lambda i, r, core_ref: (i, core_ref[0], r, 0)),
                pl.BlockSpec((None, tr, C), lambda i, r, core_ref: (i, r, 0)),
            ],
            out_specs=pl.BlockSpec((None, tr, C), lambda i, r, core_ref: (i, r, 0)),
        ),
        out_shape=jax.ShapeDtypeStruct((n, R, C), BF16),
        compiler_params=_params("arbitrary", "arbitrary"),
    )(core, g, recv)


def _chip_sum(q, core):
    L, nq, R, C = q.shape
    tr = _row_tile(R, C, 4)

    def body(core_ref, q_ref, o_ref):
        s = q_ref[0].astype(F32)
        for j in range(1, nq):
            s = s + q_ref[j].astype(F32)
        o_ref[...] = s

    return pl.pallas_call(
        body,
        name="chip_sum",
        grid_spec=pltpu.PrefetchScalarGridSpec(
            num_scalar_prefetch=1,
            grid=(L, R // tr),
            in_specs=[pl.BlockSpec((None, nq, tr, C), lambda l, r, core_ref: (l, 0, r, 0))],
            out_specs=pl.BlockSpec((None, None, tr, C), lambda l, r, core_ref: (l, core_ref[0], r, 0)),
        ),
        out_shape=jax.ShapeDtypeStruct((L, 2, R, C), F32),
        compiler_params=_params("arbitrary", "arbitrary"),
    )(core, q)


def _sum_blocks(a, n):
    M = a.shape[0] // n
    C = a.shape[1]

    def body(a_ref, o_ref):
        s = a_ref[0:M]
        for j in range(1, n):
            s = s + a_ref[j * M:(j + 1) * M]
        o_ref[...] = s

    return pl.pallas_call(
        body,
        name="sum_blocks",
        out_shape=jax.ShapeDtypeStruct((M, C), F32),
        compiler_params=pltpu.CompilerParams(vmem_limit_bytes=VMEM_LIMIT),
    )(a)


def _adamw(w, g, m, v):
    R, C = w.shape
    tr = _row_tile(R, C, 7) if R % 8 == 0 else R

    def body(w_ref, g_ref, m_ref, v_ref, d_ref, nm_ref, nv_ref):
        g = g_ref[...]
        m = ADAM_B1 * m_ref[...] + (1.0 - ADAM_B1) * g
        v = ADAM_B2 * v_ref[...] + (1.0 - ADAM_B2) * (g * g)
        m_hat = m / (1.0 - ADAM_B1 ** ADAM_STEP)
        v_hat = v / (1.0 - ADAM_B2 ** ADAM_STEP)
        d_ref[...] = -ADAM_LR * (m_hat / (jnp.sqrt(v_hat) + ADAM_EPS) + ADAM_WD * w_ref[...])
        nm_ref[...] = m
        nv_ref[...] = v

    spec = pl.BlockSpec((tr, C), lambda i: (i, 0))
    return pl.pallas_call(
        body,
        name="adamw",
        grid=(R // tr,),
        in_specs=[spec] * 4,
        out_specs=[spec] * 3,
        out_shape=[jax.ShapeDtypeStruct((R, C), F32)] * 3,
        compiler_params=_params("arbitrary"),
    )(w, g, m, v)


def _position():
    return lax.axis_index("x"), lax.axis_index("y"), lax.axis_index("c")


def _all_gather(block, name):
    m_per, n = block.shape

    def body(x_ref, out_ref, send_sems, recv_sems, local_sem):
        x, y, c = _position()
        me, sibling = (x, y, c), (x, y, 1 - c)
        chips = [(1 - x, y), (x, 1 - y), (1 - x, 1 - y)]

        def rows(px, py, pc):
            return out_ref.at[pl.ds((4 * px + 2 * py + pc) * m_per, m_per), :]

        def copy(k, blk, to, src=None):
            return pltpu.make_async_remote_copy(
                src_ref=rows(*blk) if src is None else src, dst_ref=rows(*blk),
                send_sem=send_sems.at[k], recv_sem=recv_sems.at[k], device_id=to, device_id_type=MESH)

        mine = pltpu.make_async_copy(x_ref, rows(*me), local_sem)
        mine.start()
        first = [copy(0, me, sibling, src=x_ref)]
        first += [copy(1 + j, me, (*chip, c), src=x_ref) for j, chip in enumerate(chips)]
        for cp in first:
            cp.start()
        passed = [copy(4 + j, (*chip, c), sibling) for j, chip in enumerate(chips)]
        for j, chip in enumerate(chips):
            copy(1 + j, (*chip, c), me).wait_recv()
            passed[j].start()
        copy(0, sibling, me).wait_recv()
        for j, chip in enumerate(chips):
            copy(4 + j, (*chip, 1 - c), me).wait_recv()
        for cp in first + passed:
            cp.wait_send()
        mine.wait()

    return pl.pallas_call(
        body,
        name=name,
        out_shape=jax.ShapeDtypeStruct((N_DEV * m_per, n), block.dtype),
        in_specs=[pl.BlockSpec(memory_space=pltpu.VMEM)],
        out_specs=pl.BlockSpec(memory_space=pltpu.VMEM),
        scratch_shapes=[pltpu.SemaphoreType.DMA((7,)), pltpu.SemaphoreType.DMA((7,)), pltpu.SemaphoreType.DMA],
        compiler_params=pltpu.CompilerParams(vmem_limit_bytes=VMEM_LIMIT),
    )(block)


def _gather_weights(shards, col_sharded):
    n = len(shards)
    half = [s.shape[1] // 2 for s in shards]

    def full_shape(i):
        L, R, C = shards[i].shape
        return (L, R, N_CHIP * C) if col_sharded[i] else (L, N_CHIP * R, C)

    def body(*refs):
        ins, outs = refs[:n], refs[n:2 * n]
        send_sems, recv_sems, local_sems = refs[2 * n:]
        x, y, c = _position()

        def region(i, chip, h):
            _, R, C = shards[i].shape
            if col_sharded[i]:
                return outs[i].at[:, pl.ds(h * half[i], half[i]), pl.ds(chip * C, C)]
            return outs[i].at[:, pl.ds(chip * R + h * half[i], half[i]), :]

        def run(kx, ky, kc):
            k_me = 2 * kx + ky
            sibling = (kx, ky, 1 - kc)
            chips = [(1 - kx, ky), (kx, 1 - ky), (1 - kx, 1 - ky)]
            local, first, passed, arrive_ici, arrive_d2d = [], [], [], [], []
            for i in range(n):
                for h in range(2):
                    local.append(pltpu.make_async_copy(
                        ins[i].at[:, pl.ds(h * half[i], half[i]), :], region(i, k_me, h), local_sems.at[2 * i + h]))
                for j, (px, py) in enumerate(chips):
                    s = 6 * i + j
                    first.append(pltpu.make_async_remote_copy(
                        src_ref=ins[i].at[:, pl.ds(kc * half[i], half[i]), :], dst_ref=region(i, k_me, kc),
                        send_sem=send_sems.at[s], recv_sem=recv_sems.at[s], device_id=(px, py, kc), device_id_type=MESH))
                    got = region(i, 2 * px + py, kc)
                    arrive_ici.append(pltpu.make_async_remote_copy(
                        src_ref=got, dst_ref=got, send_sem=send_sems.at[s], recv_sem=recv_sems.at[s],
                        device_id=(px, py, kc), device_id_type=MESH))
                    passed.append(pltpu.make_async_remote_copy(
                        src_ref=got, dst_ref=got, send_sem=send_sems.at[s + 3], recv_sem=recv_sems.at[s + 3],
                        device_id=sibling, device_id_type=MESH))
                    other = region(i, 2 * px + py, 1 - kc)
                    arrive_d2d.append(pltpu.make_async_remote_copy(
                        src_ref=other, dst_ref=other, send_sem=send_sems.at[s + 3], recv_sem=recv_sems.at[s + 3],
                        device_id=sibling, device_id_type=MESH))
            for cp in local + first:
                cp.start()
            for a, p in zip(arrive_ici, passed):
                a.wait_recv()
                p.start()
            for a in arrive_d2d:
                a.wait_recv()
            for cp in first + passed:
                cp.wait_send()
            for cp in local:
                cp.wait()

        for kx in range(2):
            for ky in range(2):
                for kc in range(2):
                    pl.when((x == kx) & (y == ky) & (c == kc))(functools.partial(run, kx, ky, kc))

    any_spec = pl.BlockSpec(memory_space=pl.ANY)
    return pl.pallas_call(
        body,
        name="gather_weights",
        out_shape=[jax.ShapeDtypeStruct(full_shape(i), shards[i].dtype) for i in range(n)],
        in_specs=[any_spec] * n,
        out_specs=[any_spec] * n,
        scratch_shapes=[pltpu.SemaphoreType.DMA((6 * n,)), pltpu.SemaphoreType.DMA((6 * n,)),
                        pltpu.SemaphoreType.DMA((2 * n,))],
    )(*shards)


def _send_sibling_half(gs):
    n = len(gs)

    def body(*refs):
        ins, outs = refs[:n], refs[n:2 * n]
        send_sems, recv_sems = refs[2 * n:]
        x, y, c = _position()
        copies = [pltpu.make_async_remote_copy(
            src_ref=ins[i].at[:, :, 1 - c], dst_ref=outs[i], send_sem=send_sems.at[i], recv_sem=recv_sems.at[i],
            device_id=(x, y, 1 - c), device_id_type=MESH) for i in range(n)]
        for cp in copies:
            cp.start()
        for cp in copies:
            cp.wait()

    any_spec = pl.BlockSpec(memory_space=pl.ANY)
    return pl.pallas_call(
        body,
        name="send_sibling_half",
        out_shape=[jax.ShapeDtypeStruct(g.shape[:2] + g.shape[3:], g.dtype) for g in gs],
        in_specs=[any_spec] * n,
        out_specs=[any_spec] * n,
        scratch_shapes=[pltpu.SemaphoreType.DMA((n,)), pltpu.SemaphoreType.DMA((n,))],
    )(*gs)


def _scatter_to_chips(ps):
    n = len(ps)

    def body(*refs):
        ins, outs = refs[:n], refs[n:2 * n]
        send_sems, recv_sems, local_sems = refs[2 * n:]
        x, y, c = _position()
        k_me = 2 * x + y
        chips = [(1 - x, y), (x, 1 - y), (1 - x, 1 - y)]
        local = [pltpu.make_async_copy(ins[i].at[:, k_me], outs[i].at[:, k_me], local_sems.at[i]) for i in range(n)]
        copies = [pltpu.make_async_remote_copy(
            src_ref=ins[i].at[:, 2 * px + py], dst_ref=outs[i].at[:, k_me],
            send_sem=send_sems.at[3 * i + j], recv_sem=recv_sems.at[3 * i + j],
            device_id=(px, py, c), device_id_type=MESH) for i in range(n) for j, (px, py) in enumerate(chips)]
        for cp in local + copies:
            cp.start()
        for cp in copies:
            cp.wait()
        for cp in local:
            cp.wait()

    any_spec = pl.BlockSpec(memory_space=pl.ANY)
    return pl.pallas_call(
        body,
        name="scatter_to_chips",
        out_shape=[jax.ShapeDtypeStruct(p.shape, p.dtype) for p in ps],
        in_specs=[any_spec] * n,
        out_specs=[any_spec] * n,
        scratch_shapes=[pltpu.SemaphoreType.DMA((3 * n,)), pltpu.SemaphoreType.DMA((3 * n,)),
                        pltpu.SemaphoreType.DMA((n,))],
    )(*ps)


def _share_with_sibling(rs):
    n = len(rs)

    def body(*refs):
        outs = refs[n:2 * n]
        send_sems, recv_sems = refs[2 * n:]
        x, y, c = _position()
        sends = [pltpu.make_async_remote_copy(
            src_ref=outs[i].at[:, c], dst_ref=outs[i].at[:, c], send_sem=send_sems.at[i], recv_sem=recv_sems.at[i],
            device_id=(x, y, 1 - c), device_id_type=MESH) for i in range(n)]
        arrivals = [pltpu.make_async_remote_copy(
            src_ref=outs[i].at[:, 1 - c], dst_ref=outs[i].at[:, 1 - c], send_sem=send_sems.at[i],
            recv_sem=recv_sems.at[i], device_id=(x, y, 1 - c), device_id_type=MESH) for i in range(n)]
        for cp in sends:
            cp.start()
        for cp in arrivals:
            cp.wait_recv()
        for cp in sends:
            cp.wait_send()

    any_spec = pl.BlockSpec(memory_space=pl.ANY)
    return pl.pallas_call(
        body,
        name="share_with_sibling",
        out_shape=[jax.ShapeDtypeStruct(r.shape, r.dtype) for r in rs],
        in_specs=[any_spec] * n,
        out_specs=[any_spec] * n,
        input_output_aliases={i: i for i in range(n)},
        scratch_shapes=[pltpu.SemaphoreType.DMA((n,)), pltpu.SemaphoreType.DMA((n,))],
    )(*rs)


def kernel(x, c, ada_w, ada_b, norm_ffn1_g, ffn1_w_gu, ffn1_w_down, norm_mix_g, mix_w_in, sgu_ln_g, sgu_ln_b, sgu_w_s, sgu_b, conv_w, out_norm_g, mix_w_out, norm_ffn2_g, ffn2_w_gu, ffn2_w_down, final_norm_g, loss_target, m_ada_w, m_ada_b, m_norm_ffn1_g, m_ffn1_w_gu, m_ffn1_w_down, m_norm_mix_g, m_mix_w_in, m_sgu_ln_g, m_sgu_ln_b, m_sgu_w_s, m_sgu_b, m_conv_w, m_out_norm_g, m_mix_w_out, m_norm_ffn2_g, m_ffn2_w_gu, m_ffn2_w_down, m_final_norm_g, v_ada_w, v_ada_b, v_norm_ffn1_g, v_ffn1_w_gu, v_ffn1_w_down, v_norm_mix_g, v_mix_w_in, v_sgu_ln_g, v_sgu_ln_b, v_sgu_w_s, v_sgu_b, v_conv_w, v_out_norm_g, v_mix_w_out, v_norm_ffn2_g, v_ffn2_w_gu, v_ffn2_w_down, v_final_norm_g):
    weights = dict(ada_w=ada_w, ada_b=ada_b, norm_ffn1_g=norm_ffn1_g, ffn1_w_gu=ffn1_w_gu, ffn1_w_down=ffn1_w_down,
                   norm_mix_g=norm_mix_g, mix_w_in=mix_w_in, sgu_ln_g=sgu_ln_g, sgu_ln_b=sgu_ln_b, sgu_w_s=sgu_w_s,
                   sgu_b=sgu_b, conv_w=conv_w, out_norm_g=out_norm_g, mix_w_out=mix_w_out, norm_ffn2_g=norm_ffn2_g,
                   ffn2_w_gu=ffn2_w_gu, ffn2_w_down=ffn2_w_down, final_norm_g=final_norm_g)
    m_in = dict(ada_w=m_ada_w, ada_b=m_ada_b, norm_ffn1_g=m_norm_ffn1_g, ffn1_w_gu=m_ffn1_w_gu,
                ffn1_w_down=m_ffn1_w_down, norm_mix_g=m_norm_mix_g, mix_w_in=m_mix_w_in, sgu_ln_g=m_sgu_ln_g,
                sgu_ln_b=m_sgu_ln_b, sgu_w_s=m_sgu_w_s, sgu_b=m_sgu_b, conv_w=m_conv_w, out_norm_g=m_out_norm_g,
                mix_w_out=m_mix_w_out, norm_ffn2_g=m_norm_ffn2_g, ffn2_w_gu=m_ffn2_w_gu, ffn2_w_down=m_ffn2_w_down,
                final_norm_g=m_final_norm_g)
    v_in = dict(ada_w=v_ada_w, ada_b=v_ada_b, norm_ffn1_g=v_norm_ffn1_g, ffn1_w_gu=v_ffn1_w_gu,
                ffn1_w_down=v_ffn1_w_down, norm_mix_g=v_norm_mix_g, mix_w_in=v_mix_w_in, sgu_ln_g=v_sgu_ln_g,
                sgu_ln_b=v_sgu_ln_b, sgu_w_s=v_sgu_w_s, sgu_b=v_sgu_b, conv_w=v_conv_w, out_norm_g=v_out_norm_g,
                mix_w_out=v_mix_w_out, norm_ffn2_g=v_norm_ffn2_g, ffn2_w_gu=v_ffn2_w_gu, ffn2_w_down=v_ffn2_w_down,
                final_norm_g=v_final_norm_g)

    B, S, D = x.shape
    T = B * S
    L = ada_w.shape[0]
    F = ffn1_w_down.shape[1] * N_CHIP
    P = mix_w_in.shape[2] * N_CHIP
    DA = D // 2
    DB = D - DA
    HD = DA // N_HEADS
    SA = ada_w.shape[2]
    n_all = B * N_DEV
    mx, my, mc = _position()
    chip = 2 * mx + my
    dev = 2 * chip + mc
    core = jnp.reshape(mc, (1,)).astype(jnp.int32)

    c_all = _all_gather(c.reshape(8, B * D // 8), "gather_c").reshape(n_all, D)
    ada_b_mine = lax.dynamic_slice_in_dim(ada_b, chip * SA, SA, axis=1).reshape(L, 1, SA)
    c_act, ada_part = _ada_fwd(c_all, ada_w, ada_b_mine)
    ada_all = _all_gather(ada_part.reshape(L * n_all, SA), "gather_ada").reshape(N_CHIP, 2, L, n_all, SA)[:, 0]
    ada_all = jnp.transpose(ada_all, (1, 2, 0, 3)).reshape(L, n_all, N_CHIP * SA)
    ada = lax.dynamic_slice_in_dim(ada_all, dev * B, B, axis=1).reshape(L, B, N_MOD, 1, D)
    mods = [[ada[l, :, j] for j in range(N_MOD)] for l in range(L)]

    cw_block = jnp.pad(conv_w.reshape(L * conv_w.shape[1], conv_w.shape[2]), ((0, 8 - L * conv_w.shape[1]), (0, 0)))
    cw_all = _all_gather(cw_block, "gather_conv_w").reshape(N_CHIP, 2, 8, conv_w.shape[2])[:, 0, :L * conv_w.shape[1]]
    conv_full = jnp.transpose(cw_all.reshape(N_CHIP, L, conv_w.shape[1], conv_w.shape[2]), (1, 2, 0, 3))
    conv_full = conv_full.reshape(L, conv_w.shape[1], DB)
    big = ["ffn1_w_gu", "ffn1_w_down", "mix_w_in", "mix_w_out", "ffn2_w_gu", "ffn2_w_down"]
    col_sharded = [True, False, True, False, True, False]
    wgu1, wd1, win, wout, wgu2, wd2 = _gather_weights([weights[k].astype(BF16) for k in big], col_sharded)

    x0 = x.reshape(T, D)
    gains = lambda name, l: weights[name][l].reshape(1, D)
    hmask = jnp.repeat(jnp.eye(N_HEADS, dtype=F32), HD, axis=0)
    pmat = (jnp.repeat(hmask, HD, axis=1) / HD).astype(BF16)

    def mix_consts(l):
        lng = jnp.tile(sgu_ln_g[l], N_HEADS).reshape(1, DA)
        lnb = jnp.tile(sgu_ln_b[l], N_HEADS).reshape(1, DA)
        wst = sgu_w_s[l].reshape(N_HEADS * CHUNK, CHUNK)
        wstt = jnp.swapaxes(sgu_w_s[l], 1, 2).reshape(N_HEADS * CHUNK, CHUNK)
        bias = jnp.repeat(jnp.transpose(sgu_b[l]), HD, axis=1)
        return lng, lnb, wst, wstt, bias

    saved = []
    xc = x0
    for l in range(L):
        sh1, sc1, g1, sh2, sc2, g2, sh3, sc3, g3 = mods[l]
        lng, lnb, wst, wstt, bias = mix_consts(l)
        gu1, a1, h1 = _ffn_up(xc, gains("norm_ffn1_g", l), sh1, sc1, wgu1, l)
        xa, f1 = _ffn_down(a1, xc, g1, wd1, l)
        proj, h2 = _mixin_fwd(xa, gains("norm_mix_g", l), sh2, sc2, win, l)
        xb, yn = _mix_core_fwd(proj, xa, g2, wout, l, lng, lnb, wst, bias, pmat, conv_full[l],
                               gains("out_norm_g", l))
        gu2, a2, h3 = _ffn_up(xb, gains("norm_ffn2_g", l), sh3, sc3, wgu2, l)
        xd, f2 = _ffn_down(a2, xb, g3, wd2, l)
        saved.append(dict(x0=xc, xa=xa, xb=xb, gu1=gu1, a1=a1, h1=h1, f1=f1, proj=proj, h2=h2, yn=yn,
                          gu2=gu2, a2=a2, h3=h3, f2=f2))
        xc = xd

    dx, loss_block, d_final = _loss_head(xc, loss_target.reshape(T, D), final_norm_g.reshape(1, D))
    loss = lax.psum(loss_block[0, 0], ("x", "y", "c"))

    gbig = dict.fromkeys(big)
    small = [None] * L
    d_ada = [None] * L
    for l in reversed(range(L)):
        sh1, sc1, g1, sh2, sc2, g2, sh3, sc3, g3 = mods[l]
        lng, lnb, wst, wstt, bias = mix_consts(l)
        s = saved[l]
        dgu, df, dg3 = _ffn_bwd_down(dx, s["f2"], s["gu2"], g3, wd2, l)
        dx, dsc3, dsh3, dgain3 = _ffn_bwd_up(dx, s["xb"], dgu, gains("norm_ffn2_g", l), sc3, wgu2, l)
        gbig["ffn2_w_gu"] = _wgrad(s["h3"], dgu, l, L, gbig["ffn2_w_gu"], D, 2 * F // N_CHIP, True, "wgrad_gu")
        gbig["ffn2_w_down"] = _wgrad(s["a2"], df[None], l, L, gbig["ffn2_w_down"], F // 2, D, False, "wgrad_down")
        dproj, d_o, dg2, dog, dwst, dbias, dlng, dlnb, dconvw = _mix_core_bwd(
            s["proj"], dx, g2, wout, l, lng, lnb, wst, wstt, bias, pmat, conv_full[l], gains("out_norm_g", l))
        gbig["mix_w_out"] = _wgrad(s["yn"], d_o[None], l, L, gbig["mix_w_out"], D, D, False, "wgrad_out",
                                   tokens=2 * WGRAD_TOKENS)
        dx, dsc2, dsh2, dgain2 = _mixin_bwd(dx, s["xa"], dproj, gains("norm_mix_g", l), sc2, win, l)
        gbig["mix_w_in"] = _wgrad(s["h2"], dproj[None], l, L, gbig["mix_w_in"], D, P // N_CHIP, True, "wgrad_in",
                                  tokens=2 * WGRAD_TOKENS)
        dgu, df, dg1 = _ffn_bwd_down(dx, s["f1"], s["gu1"], g1, wd1, l)
        dx, dsc1, dsh1, dgain1 = _ffn_bwd_up(dx, s["x0"], dgu, gains("norm_ffn1_g", l), sc1, wgu1, l)
        gbig["ffn1_w_gu"] = _wgrad(s["h1"], dgu, l, L, gbig["ffn1_w_gu"], D, 2 * F // N_CHIP, True, "wgrad_gu")
        gbig["ffn1_w_down"] = _wgrad(s["a1"], df[None], l, L, gbig["ffn1_w_down"], F // 2, D, False, "wgrad_down")
        d_ada[l] = jnp.concatenate([dsh1, dsc1, dg1, dsh2, dsc2, dg2, dsh3, dsc3, dg3], axis=1).reshape(B, N_MOD * D)
        small[l] = [dgain1, dgain2, dgain3, dog, dlng, dlnb, dwst, dbias, dconvw]
    grad_x = dx.reshape(B, S, D)

    d_ada_all = _all_gather(jnp.stack(d_ada).reshape(L * B, N_MOD * D), "gather_d_ada")
    d_ada_all = jnp.transpose(d_ada_all.reshape(N_DEV, L, B, N_MOD * D), (1, 0, 2, 3)).reshape(L, n_all, N_MOD * D)
    g_ada_b = _colsum(d_ada_all).reshape(L, N_MOD * D)
    d_ada_mine = lax.dynamic_slice_in_dim(d_ada_all, chip * SA, SA, axis=2).astype(BF16)
    g_ada_w = None
    for l in range(L):
        g_ada_w = _wgrad(c_act, d_ada_mine[l][None], l, L, g_ada_w, D, _tile(768, SA), False, "wgrad_ada")
    g_ada_w = g_ada_w.reshape(L, D, SA)

    def halves(name, g):
        if g.ndim == 5:
            return g
        return g.reshape(L, N_CHIP, 2, weights[name].shape[1] // 2, g.shape[-1])

    gs = [halves(k, gbig[k]) for k in big]
    recv = _send_sibling_half(gs)
    ps = [_pair_sum(g.reshape((L * N_CHIP,) + g.shape[2:]), r.reshape((L * N_CHIP,) + r.shape[2:]), core)
          .reshape(r.shape) for g, r in zip(gs, recv)]
    qs = _scatter_to_chips(ps)
    rs = _share_with_sibling([_chip_sum(q, core) for q in qs])
    grads = {k: r.reshape(weights[k].shape) for k, r in zip(big, rs)}
    grads["ada_w"] = g_ada_w
    grads["ada_b"] = g_ada_b

    flat = [a.reshape(-1, 128) for l in range(L) for a in small[l]] + [d_final.reshape(-1, 128)]
    n_rows = sum(a.shape[0] for a in flat)
    pad = (-n_rows) % 8
    packed = jnp.concatenate(flat + [jnp.zeros((pad, 128), F32)], axis=0)
    total = _sum_blocks(_all_gather(packed, "gather_small"), N_DEV)
    pieces, at = [], 0
    for a in flat:
        pieces.append(total[at:at + a.shape[0]])
        at += a.shape[0]
    per_layer = len(small[0])
    stack = lambda j, shape: jnp.stack([pieces[l * per_layer + j].reshape(shape) for l in range(L)])
    grads["norm_ffn1_g"] = stack(0, (D,))
    grads["norm_mix_g"] = stack(1, (D,))
    grads["norm_ffn2_g"] = stack(2, (D,))
    grads["out_norm_g"] = stack(3, (D,))
    grads["sgu_ln_g"] = stack(4, (N_HEADS, HD)).sum(axis=1)
    grads["sgu_ln_b"] = stack(5, (N_HEADS, HD)).sum(axis=1)
    grads["sgu_w_s"] = stack(6, (N_HEADS, CHUNK, CHUNK))
    grads["sgu_b"] = jnp.swapaxes(stack(7, (CHUNK, DA))[:, :, ::HD], 1, 2)
    g_conv = stack(8, (8, DB))[:, :conv_w.shape[1]]
    grads["conv_w"] = lax.dynamic_slice_in_dim(g_conv, chip * conv_w.shape[2], conv_w.shape[2], axis=2)
    grads["final_norm_g"] = pieces[-1].reshape(D)

    names = list(weights)
    delta, new_m, new_v = {}, {}, {}
    for k in names:
        w = weights[k]
        view = (1, w.shape[0]) if w.ndim == 1 else (-1, w.shape[-1])
        d, nm, nv = _adamw(w.reshape(view), grads[k].reshape(view), m_in[k].reshape(view), v_in[k].reshape(view))
        delta[k], new_m[k], new_v[k] = d.reshape(w.shape), nm.reshape(w.shape), nv.reshape(w.shape)

    return (loss, grad_x, *[grads[k] for k in names], *[delta[k] for k in names],
            *[new_m[k] for k in names], *[new_v[k] for k in names])
```

```python
import functools
import math

import jax
import jax.numpy as jnp
from jax import lax
from jax.experimental import pallas as pl
from jax.experimental.pallas import tpu as pltpu

F32 = jnp.float32
BF16 = jnp.bfloat16
MESH = pl.DeviceIdType.MESH

N_HEADS = 8
CHUNK = 128
N_MOD = 9
EPS = 1e-6
N_DEV = 8
N_CHIP = 4

ADAM_LR = 0.001
ADAM_B1 = 0.9
ADAM_B2 = 0.999
ADAM_EPS = 1e-08
ADAM_WD = 0.01
ADAM_STEP = 10

TOKEN_TILE = 512
FF_TILE = 1408
MIX_TILE = 256
WGRAD_TOKENS = 1024
VMEM_LIMIT = 52 * 1024 * 1024

ANY = pl.BlockSpec(memory_space=pl.ANY)


def _tile(pref, n):
    t = min(pref, n)
    assert n % t == 0, (pref, n)
    return t


def _dot(a, b):
    return jnp.dot(a, b, preferred_element_type=F32)


def _dot_nt(a, b):
    return lax.dot_general(a, b, (((1,), (1,)), ((), ())), preferred_element_type=F32)


def _dot_tn(a, b):
    return lax.dot_general(a, b, (((0,), (0,)), ((), ())), preferred_element_type=F32)


def _sigmoid(x):
    return 1.0 / (1.0 + jnp.exp(-x))


def _rms(x):
    r = lax.rsqrt(jnp.mean(x * x, axis=-1, keepdims=True) + EPS)
    return x * r, r


def _norm_mod_bwd(x, dh, gain, sc):
    xh, r = _rms(x)
    dsc = jnp.sum(dh * (xh * gain), axis=0, keepdims=True)
    dsh = jnp.sum(dh, axis=0, keepdims=True)
    dn = dh * (1.0 + sc)
    dgain = jnp.sum(dn * xh, axis=0, keepdims=True)
    dy = dn * gain
    dx = r * (dy - xh * jnp.mean(dy * xh, axis=-1, keepdims=True))
    return dx, dsc, dsh, dgain


def _acc(ref, first, val):
    @pl.when(first)
    def _():
        ref[...] = val

    @pl.when(jnp.logical_not(first))
    def _():
        ref[...] += val


class _Comm:
    def __init__(self, args, out_shape, scratch, phases, aliases=None):
        self.args, self.out_shape, self.scratch = list(args), list(out_shape), list(scratch)
        self.phases, self.aliases = phases, dict(aliases or {})


def _merge(*comms):
    comms = [c for c in comms if c is not None]
    if len(comms) <= 1:
        return comms[0] if comms else None
    args = [a for c in comms for a in c.args]
    out_shape = [o for c in comms for o in c.out_shape]
    scratch = [s for c in comms for s in c.scratch]
    aliases, ai, oi = {}, 0, 0
    for c in comms:
        aliases.update({ai + i: oi + o for i, o in c.aliases.items()})
        ai += len(c.args)
        oi += len(c.out_shape)

    def phases(ins, outs, sems):
        parts, ai, oi, si = [], 0, 0, 0
        for c in comms:
            parts.append(c.phases(ins[ai:ai + len(c.args)], outs[oi:oi + len(c.out_shape)], sems[si:si + len(c.scratch)]))
            ai, oi, si = ai + len(c.args), oi + len(c.out_shape), si + len(c.scratch)

        def run(k):
            def go():
                for p in parts:
                    if p[k] is not None:
                        p[k]()
            return go
        return run(0), run(1), run(2)

    return _Comm(args, out_shape, scratch, phases, aliases)


def _call(body, name, grid, in_specs, out_specs, out_shape, scratch, args, comm=None):
    n_in, n_out, n_scr = len(in_specs), len(out_specs), len(scratch)
    sem = ("arbitrary",) * len(grid)
    params = pltpu.CompilerParams(dimension_semantics=sem, vmem_limit_bytes=VMEM_LIMIT)
    if comm is None:
        res = pl.pallas_call(body, name=name, grid=grid, in_specs=in_specs, out_specs=out_specs, out_shape=out_shape,
                             scratch_shapes=scratch, compiler_params=params)(*args)
        return list(res), []
    m_in, m_out = len(comm.args), len(comm.out_shape)

    def full(*refs):
        c_in, c_min = refs[:n_in], refs[n_in:n_in + m_in]
        o = n_in + m_in
        c_out, c_mout = refs[o:o + n_out], refs[o + n_out:o + n_out + m_out]
        o += n_out + m_out
        c_scr, c_sem = refs[o:o + n_scr], refs[o + n_scr:]
        start, mid, finish = comm.phases(c_min, c_mout, c_sem)
        ids = [pl.program_id(a) for a in range(len(grid))]
        first = functools.reduce(jnp.logical_and, [i == 0 for i in ids])
        last = functools.reduce(jnp.logical_and, [i == g - 1 for i, g in zip(ids, grid)])
        pl.when(first)(start)
        if mid is not None:
            pl.when(last)(mid)
        body(*c_in, *c_out, *c_scr)
        pl.when(last)(finish)

    res = pl.pallas_call(
        full, name=name, grid=grid,
        in_specs=list(in_specs) + [ANY] * m_in,
        out_specs=list(out_specs) + [ANY] * m_out,
        out_shape=list(out_shape) + comm.out_shape,
        scratch_shapes=list(scratch) + comm.scratch,
        input_output_aliases={n_in + i: n_out + o for i, o in comm.aliases.items()},
        compiler_params=params,
    )(*args, *comm.args)
    return list(res[:n_out]), list(res[n_out:])


def _comm_call(comm, name):
    m_in, m_out = len(comm.args), len(comm.out_shape)

    def body(*refs):
        start, mid, finish = comm.phases(refs[:m_in], refs[m_in:m_in + m_out], refs[m_in + m_out:])
        start()
        if mid is not None:
            mid()
        finish()

    res = pl.pallas_call(
        body, name=name, in_specs=[ANY] * m_in, out_specs=[ANY] * m_out, out_shape=comm.out_shape,
        scratch_shapes=comm.scratch, input_output_aliases=comm.aliases,
    )(*comm.args)
    return list(res)


def _position():
    return lax.axis_index("x"), lax.axis_index("y"), lax.axis_index("c")


def _gather_comm(items):
    n = len(items)
    half = [s.shape[1] // 2 for s, _, _ in items]

    def full_shape(i):
        s, _, col = items[i]
        _, R, C = s.shape
        return jax.ShapeDtypeStruct((R, N_CHIP * C) if col else (N_CHIP * R, C), s.dtype)

    def phases(ins, outs, sems):
        send_sems, recv_sems, local_sems = sems
        x, y, c = _position()

        def region(i, chip, h):
            s, _, col = items[i]
            _, R, C = s.shape
            if col:
                return outs[i].at[pl.ds(h * half[i], half[i]), pl.ds(chip * C, C)]
            return outs[i].at[pl.ds(chip * R + h * half[i], half[i]), :]

        def mine(i, h):
            return ins[i].at[items[i][1], pl.ds(h * half[i], half[i]), :]

        def copies(kx, ky, kc):
            k_me = 2 * kx + ky
            sibling = (kx, ky, 1 - kc)
            chips = [(1 - kx, ky), (kx, 1 - ky), (1 - kx, 1 - ky)]
            local, first, passed, arrive_ici, arrive_d2d = [], [], [], [], []

            def remote(src, dst, s, to):
                return pltpu.make_async_remote_copy(src_ref=src, dst_ref=dst, send_sem=send_sems.at[s],
                                                    recv_sem=recv_sems.at[s], device_id=to, device_id_type=MESH)

            for i in range(n):
                for h in range(2):
                    local.append(pltpu.make_async_copy(mine(i, h), region(i, k_me, h), local_sems.at[2 * i + h]))
                for j, (px, py) in enumerate(chips):
                    s = 6 * i + j
                    first.append(remote(mine(i, kc), region(i, k_me, kc), s, (px, py, kc)))
                    got = region(i, 2 * px + py, kc)
                    arrive_ici.append(remote(got, got, s, (px, py, kc)))
                    passed.append(remote(got, got, s + 3, sibling))
                    other = region(i, 2 * px + py, 1 - kc)
                    arrive_d2d.append(remote(other, other, s + 3, sibling))
            return local, first, passed, arrive_ici, arrive_d2d

        def on_each_device(fn):
            def go():
                for kx in range(2):
                    for ky in range(2):
                        for kc in range(2):
                            pl.when((x == kx) & (y == ky) & (c == kc))(functools.partial(fn, *copies(kx, ky, kc)))
            return go

        def start(local, first, passed, arrive_ici, arrive_d2d):
            for cp in local + first:
                cp.start()

        def mid(local, first, passed, arrive_ici, arrive_d2d):
            for a, p in zip(arrive_ici, passed):
                a.wait_recv()
                p.start()

        def finish(local, first, passed, arrive_ici, arrive_d2d):
            for a in arrive_d2d:
                a.wait_recv()
            for cp in first + passed:
                cp.wait_send()
            for cp in local:
                cp.wait()

        return on_each_device(start), on_each_device(mid), on_each_device(finish)

    scratch = [pltpu.SemaphoreType.DMA((6 * n,)), pltpu.SemaphoreType.DMA((6 * n,)), pltpu.SemaphoreType.DMA((2 * n,))]
    return _Comm([s for s, _, _ in items], [full_shape(i) for i in range(n)], scratch, phases)


def _sibling_half_comm(gs):
    n = len(gs)

    def phases(ins, outs, sems):
        send_sems, recv_sems = sems
        x, y, c = _position()

        def copies():
            return [pltpu.make_async_remote_copy(
                src_ref=ins[i].at[:, 1 - c], dst_ref=outs[i], send_sem=send_sems.at[i], recv_sem=recv_sems.at[i],
                device_id=(x, y, 1 - c), device_id_type=MESH) for i in range(n)]

        def start():
            for cp in copies():
                cp.start()

        def finish():
            for cp in copies():
                cp.wait()

        return start, None, finish

    out_shape = [jax.ShapeDtypeStruct(g.shape[:1] + g.shape[2:], g.dtype) for g in gs]
    return _Comm(gs, out_shape, [pltpu.SemaphoreType.DMA((n,)), pltpu.SemaphoreType.DMA((n,))], phases)


def _scatter_comm(ps):
    n = len(ps)

    def phases(ins, outs, sems):
        send_sems, recv_sems, local_sems = sems
        x, y, c = _position()
        k_me = 2 * x + y
        chips = [(1 - x, y), (x, 1 - y), (1 - x, 1 - y)]

        def copies():
            local = [pltpu.make_async_copy(ins[i].at[k_me], outs[i].at[k_me], local_sems.at[i]) for i in range(n)]
            remote = [pltpu.make_async_remote_copy(
                src_ref=ins[i].at[2 * px + py], dst_ref=outs[i].at[k_me],
                send_sem=send_sems.at[3 * i + j], recv_sem=recv_sems.at[3 * i + j],
                device_id=(px, py, c), device_id_type=MESH) for i in range(n) for j, (px, py) in enumerate(chips)]
            return local, remote

        def start():
            local, remote = copies()
            for cp in local + remote:
                cp.start()

        def finish():
            local, remote = copies()
            for cp in remote + local:
                cp.wait()

        return start, None, finish

    scratch = [pltpu.SemaphoreType.DMA((3 * n,)), pltpu.SemaphoreType.DMA((3 * n,)), pltpu.SemaphoreType.DMA((n,))]
    return _Comm(ps, [jax.ShapeDtypeStruct(p.shape, p.dtype) for p in ps], scratch, phases)


def _share_comm(rs, l):
    n = len(rs)

    def phases(ins, outs, sems):
        send_sems, recv_sems = sems
        x, y, c = _position()

        def copy(i, h):
            return pltpu.make_async_remote_copy(
                src_ref=outs[i].at[l, h], dst_ref=outs[i].at[l, h], send_sem=send_sems.at[i], recv_sem=recv_sems.at[i],
                device_id=(x, y, 1 - c), device_id_type=MESH)

        def start():
            for i in range(n):
                copy(i, c).start()

        def finish():
            for i in range(n):
                copy(i, 1 - c).wait_recv()
            for i in range(n):
                copy(i, c).wait_send()

        return start, None, finish

    return _Comm(rs, [jax.ShapeDtypeStruct(r.shape, r.dtype) for r in rs],
                 [pltpu.SemaphoreType.DMA((n,)), pltpu.SemaphoreType.DMA((n,))], phases,
                 aliases={i: i for i in range(n)})


def _all_gather(block, name):
    m_per, n = block.shape

    def body(x_ref, out_ref, send_sems, recv_sems, local_sem):
        x, y, c = _position()
        me, sibling = (x, y, c), (x, y, 1 - c)
        chips = [(1 - x, y), (x, 1 - y), (1 - x, 1 - y)]

        def rows(px, py, pc):
            return out_ref.at[pl.ds((4 * px + 2 * py + pc) * m_per, m_per), :]

        def copy(k, blk, to, src=None):
            return pltpu.make_async_remote_copy(
                src_ref=rows(*blk) if src is None else src, dst_ref=rows(*blk),
                send_sem=send_sems.at[k], recv_sem=recv_sems.at[k], device_id=to, device_id_type=MESH)

        mine = pltpu.make_async_copy(x_ref, rows(*me), local_sem)
        mine.start()
        first = [copy(0, me, sibling, src=x_ref)]
        first += [copy(1 + j, me, (*chip, c), src=x_ref) for j, chip in enumerate(chips)]
        for cp in first:
            cp.start()
        passed = [copy(4 + j, (*chip, c), sibling) for j, chip in enumerate(chips)]
        for j, chip in enumerate(chips):
            copy(1 + j, (*chip, c), me).wait_recv()
            passed[j].start()
        copy(0, sibling, me).wait_recv()
        for j, chip in enumerate(chips):
            copy(4 + j, (*chip, 1 - c), me).wait_recv()
        for cp in first + passed:
            cp.wait_send()
        mine.wait()

    return pl.pallas_call(
        body,
        name=name,
        out_shape=jax.ShapeDtypeStruct((N_DEV * m_per, n), block.dtype),
        in_specs=[pl.BlockSpec(memory_space=pltpu.VMEM)],
        out_specs=pl.BlockSpec(memory_space=pltpu.VMEM),
        scratch_shapes=[pltpu.SemaphoreType.DMA((7,)), pltpu.SemaphoreType.DMA((7,)), pltpu.SemaphoreType.DMA],
        compiler_params=pltpu.CompilerParams(vmem_limit_bytes=VMEM_LIMIT),
    )(block)


def _ffn_up(x, gain, sh, sc, wgu, comm=None):
    T, D = x.shape
    F = wgu.shape[1] // 2
    B = sh.shape[0]
    tm = _tile(TOKEN_TILE, T // B)
    tf = _tile(FF_TILE, F)
    tps = (T // B) // tm
    nf = F // tf

    def body(x_ref, gain_ref, sh_ref, sc_ref, wg_ref, wu_ref, gu_ref, a_ref, h_ref, hs):
        @pl.when(pl.program_id(1) == 0)
        def _():
            xh, _ = _rms(x_ref[...])
            h = (xh * gain_ref[...] * (1.0 + sc_ref[0]) + sh_ref[0]).astype(BF16)
            hs[...] = h
            h_ref[...] = h

        g = _dot(hs[...], wg_ref[...])
        u = _dot(hs[...], wu_ref[...])
        gu_ref[0] = g.astype(BF16)
        gu_ref[1] = u.astype(BF16)
        a_ref[...] = (g * _sigmoid(g) * u).astype(BF16)

    seq = lambda i, k: (i // tps, 0, 0)
    return _call(
        body, "ffn_up", (T // tm, nf),
        [
            pl.BlockSpec((tm, D), lambda i, k: (i, 0)),
            pl.BlockSpec((1, D), lambda i, k: (0, 0)),
            pl.BlockSpec((1, 1, D), seq),
            pl.BlockSpec((1, 1, D), seq),
            pl.BlockSpec((D, tf), lambda i, k: (0, k)),
            pl.BlockSpec((D, tf), lambda i, k: (0, nf + k)),
        ],
        [
            pl.BlockSpec((2, tm, tf), lambda i, k: (0, i, k)),
            pl.BlockSpec((tm, tf), lambda i, k: (i, k)),
            pl.BlockSpec((tm, D), lambda i, k: (i, 0)),
        ],
        [
            jax.ShapeDtypeStruct((2, T, F), BF16),
            jax.ShapeDtypeStruct((T, F), BF16),
            jax.ShapeDtypeStruct((T, D), BF16),
        ],
        [pltpu.VMEM((tm, D), BF16)],
        (x, gain, sh, sc, wgu, wgu), comm)


def _ffn_down(a, x, gate, wd, comm=None):
    T, F = a.shape
    D = x.shape[1]
    B = gate.shape[0]
    tm = _tile(TOKEN_TILE, T // B)
    tps = (T // B) // tm

    def body(a_ref, x_ref, gate_ref, wd_ref, xo_ref, f_ref):
        f = _dot(a_ref[...], wd_ref[...])
        f_ref[...] = f.astype(BF16)
        xo_ref[...] = x_ref[...] + 0.5 * gate_ref[0] * f

    return _call(
        body, "ffn_down", (T // tm,),
        [
            pl.BlockSpec((tm, F), lambda i: (i, 0)),
            pl.BlockSpec((tm, D), lambda i: (i, 0)),
            pl.BlockSpec((1, 1, D), lambda i: (i // tps, 0, 0)),
            pl.BlockSpec((F, D), lambda i: (0, 0)),
        ],
        [pl.BlockSpec((tm, D), lambda i: (i, 0)), pl.BlockSpec((tm, D), lambda i: (i, 0))],
        [jax.ShapeDtypeStruct((T, D), F32), jax.ShapeDtypeStruct((T, D), BF16)],
        [],
        (a, x, gate, wd), comm)


def _ffn_bwd_down(dxo, f, gu, gate, wd, comm=None):
    T, D = dxo.shape
    F = wd.shape[0]
    B = gate.shape[0]
    tm = _tile(TOKEN_TILE, T // B)
    tf = _tile(FF_TILE, F)
    tps = (T // B) // tm
    nf = F // tf

    def body(dxo_ref, f_ref, gu_ref, gate_ref, wd_ref, dgu_ref, df_ref, dgate_ref, dfs):
        @pl.when(pl.program_id(1) == 0)
        def _():
            dxo = dxo_ref[...]
            df = (0.5 * gate_ref[0] * dxo).astype(BF16)
            dfs[...] = df
            df_ref[...] = df
            _acc(dgate_ref.at[0], (pl.program_id(0) % tps) == 0,
                 0.5 * jnp.sum(dxo * f_ref[...].astype(F32), axis=0, keepdims=True))

        da = _dot_nt(dfs[...], wd_ref[...])
        g = gu_ref[0].astype(F32)
        sg = _sigmoid(g)
        dgu_ref[1] = (da * (g * sg)).astype(BF16)
        dgu_ref[0] = (da * gu_ref[1].astype(F32) * (sg * (1.0 + g * (1.0 - sg)))).astype(BF16)

    seq = lambda i, k: (i // tps, 0, 0)
    row = lambda i, k: (i, 0)
    return _call(
        body, "ffn_bwd_down", (T // tm, nf),
        [
            pl.BlockSpec((tm, D), row),
            pl.BlockSpec((tm, D), row),
            pl.BlockSpec((2, tm, tf), lambda i, k: (0, i, k)),
            pl.BlockSpec((1, 1, D), seq),
            pl.BlockSpec((tf, D), lambda i, k: (k, 0)),
        ],
        [
            pl.BlockSpec((2, tm, tf), lambda i, k: (0, i, k)),
            pl.BlockSpec((tm, D), row),
            pl.BlockSpec((1, 1, D), seq),
        ],
        [
            jax.ShapeDtypeStruct((2, T, F), BF16),
            jax.ShapeDtypeStruct((T, D), BF16),
            jax.ShapeDtypeStruct((B, 1, D), F32),
        ],
        [pltpu.VMEM((tm, D), BF16)],
        (dxo, f, gu, gate, wd), comm)


def _ffn_bwd_up(dxo, x, dgu, gain, sc, wgu, comm=None):
    T, D = x.shape
    F = wgu.shape[1] // 2
    B = sc.shape[0]
    tm = _tile(TOKEN_TILE, T // B)
    tps = (T // B) // tm

    def body(dxo_ref, x_ref, dgu_ref, gain_ref, sc_ref, w_ref, dx_ref, dsc_ref, dsh_ref, dgain_ref, acc):
        i = pl.program_id(0)
        k = pl.program_id(1)
        _acc(acc, k == 0, _dot_nt(dgu_ref[...], w_ref[...]))

        @pl.when(k == 1)
        def _():
            first_of_seq = (i % tps) == 0
            dx, dsc, dsh, dgain = _norm_mod_bwd(x_ref[...], acc[...], gain_ref[...], sc_ref[0])
            dx_ref[...] = dxo_ref[...] + dx
            _acc(dsc_ref.at[0], first_of_seq, dsc)
            _acc(dsh_ref.at[0], first_of_seq, dsh)
            _acc(dgain_ref, i == 0, dgain)

    seq = lambda i, k: (i // tps, 0, 0)
    row = lambda i, k: (i, 0)
    return _call(
        body, "ffn_bwd_up", (T // tm, 2),
        [
            pl.BlockSpec((tm, D), row),
            pl.BlockSpec((tm, D), row),
            pl.BlockSpec((None, tm, F), lambda i, k: (k, i, 0)),
            pl.BlockSpec((1, D), lambda i, k: (0, 0)),
            pl.BlockSpec((1, 1, D), seq),
            pl.BlockSpec((D, F), lambda i, k: (0, k)),
        ],
        [
            pl.BlockSpec((tm, D), row),
            pl.BlockSpec((1, 1, D), seq),
            pl.BlockSpec((1, 1, D), seq),
            pl.BlockSpec((1, D), lambda i, k: (0, 0)),
        ],
        [
            jax.ShapeDtypeStruct((T, D), F32),
            jax.ShapeDtypeStruct((B, 1, D), F32),
            jax.ShapeDtypeStruct((B, 1, D), F32),
            jax.ShapeDtypeStruct((1, D), F32),
        ],
        [pltpu.VMEM((tm, D), F32)],
        (dxo, x, dgu, gain, sc, wgu), comm)


def _wgrad(a, b, tmm, tn, col_major, name, tokens=WGRAD_TOKENS, comm=None):
    T, M = a.shape
    nb, _, Nb = b.shape
    N = nb * Nb
    tk = _tile(tokens, T)
    npb = Nb // tn
    assert M % tmm == 0 and Nb % tn == 0
    if col_major:
        assert tmm == M
        shape = (N // tn, 2, M // 2, tn)
        out_spec = pl.BlockSpec((None, 2, M // 2, tn), lambda i, j, t: (j, 0, 0, 0))
    else:
        shape = (M // tmm, tmm, N)
        out_spec = pl.BlockSpec((None, tmm, tn), lambda i, j, t: (i, 0, j))

    def body(a_ref, b_ref, o_ref):
        t = pl.program_id(2)
        res = _dot_tn(a_ref[...], b_ref[...])
        if col_major:
            _acc(o_ref.at[0], t == 0, res[:M // 2])
            _acc(o_ref.at[1], t == 0, res[M // 2:])
        else:
            _acc(o_ref, t == 0, res)

    return _call(
        body, name, (M // tmm, N // tn, T // tk),
        [
            pl.BlockSpec((tk, tmm), lambda i, j, t: (t, i)),
            pl.BlockSpec((None, tk, tn), lambda i, j, t: (j // npb, t, j % npb)),
        ],
        [out_spec], [jax.ShapeDtypeStruct(shape, F32)], [],
        (a, b), comm)


def _mixin_fwd(x, gain, sh, sc, win, comm=None):
    T, D = x.shape
    P = win.shape[1]
    B = sh.shape[0]
    tm = _tile(TOKEN_TILE, T // B)
    tps = (T // B) // tm

    def body(x_ref, gain_ref, sh_ref, sc_ref, w_ref, proj_ref, h_ref):
        xh, _ = _rms(x_ref[...])
        h = (xh * gain_ref[...] * (1.0 + sc_ref[0]) + sh_ref[0]).astype(BF16)
        h_ref[...] = h
        proj_ref[...] = _dot(h, w_ref[...])

    seq = lambda i: (i // tps, 0, 0)
    return _call(
        body, "mixin_fwd", (T // tm,),
        [
            pl.BlockSpec((tm, D), lambda i: (i, 0)),
            pl.BlockSpec((1, D), lambda i: (0, 0)),
            pl.BlockSpec((1, 1, D), seq),
            pl.BlockSpec((1, 1, D), seq),
            pl.BlockSpec((D, P), lambda i: (0, 0)),
        ],
        [pl.BlockSpec((tm, P), lambda i: (i, 0)), pl.BlockSpec((tm, D), lambda i: (i, 0))],
        [jax.ShapeDtypeStruct((T, P), F32), jax.ShapeDtypeStruct((T, D), BF16)],
        [],
        (x, gain, sh, sc, win), comm)


def _mixin_bwd(dxo, x, dproj, gain, sc, win, comm=None):
    T, D = x.shape
    P = win.shape[1]
    B = sc.shape[0]
    tm = _tile(TOKEN_TILE, T // B)
    tps = (T // B) // tm

    def body(dxo_ref, x_ref, dp_ref, gain_ref, sc_ref, w_ref, dx_ref, dsc_ref, dsh_ref, dgain_ref):
        i = pl.program_id(0)
        first_of_seq = (i % tps) == 0
        dh = _dot_nt(dp_ref[...], w_ref[...])
        dx, dsc, dsh, dgain = _norm_mod_bwd(x_ref[...], dh, gain_ref[...], sc_ref[0])
        dx_ref[...] = dxo_ref[...] + dx
        _acc(dsc_ref.at[0], first_of_seq, dsc)
        _acc(dsh_ref.at[0], first_of_seq, dsh)
        _acc(dgain_ref, i == 0, dgain)

    seq = lambda i: (i // tps, 0, 0)
    row = lambda i: (i, 0)
    return _call(
        body, "mixin_bwd", (T // tm,),
        [
            pl.BlockSpec((tm, D), row),
            pl.BlockSpec((tm, D), row),
            pl.BlockSpec((tm, P), row),
            pl.BlockSpec((1, D), lambda i: (0, 0)),
            pl.BlockSpec((1, 1, D), seq),
            pl.BlockSpec((D, P), lambda i: (0, 0)),
        ],
        [
            pl.BlockSpec((tm, D), row),
            pl.BlockSpec((1, 1, D), seq),
            pl.BlockSpec((1, 1, D), seq),
            pl.BlockSpec((1, D), lambda i: (0, 0)),
        ],
        [
            jax.ShapeDtypeStruct((T, D), F32),
            jax.ShapeDtypeStruct((B, 1, D), F32),
            jax.ShapeDtypeStruct((B, 1, D), F32),
            jax.ShapeDtypeStruct((1, D), F32),
        ],
        [],
        (dxo, x, dproj, gain, sc, win), comm)


def _head_mean(z, pmat):
    hi = z.astype(BF16)
    lo = (z - hi.astype(F32)).astype(BF16)
    return _dot(hi, pmat) + _dot(lo, pmat)


def _gelu_parts(x):
    cdf = 0.5 * (1.0 + lax.erf(x * (1.0 / math.sqrt(2.0))))
    return x * cdf, cdf


def _gelu_grad(x, cdf):
    return cdf + x * jnp.exp(-0.5 * x * x) * (1.0 / math.sqrt(2.0 * math.pi))


def _head_masks(da):
    hd = da // N_HEADS
    col = lax.broadcasted_iota(jnp.int32, (1, da), 1)
    return [(col >= h * hd) & (col < (h + 1) * hd) for h in range(N_HEADS)]


def _select_heads(res, masks):
    out = res[0:CHUNK]
    for h in range(1, N_HEADS):
        out = jnp.where(masks[h], res[h * CHUNK:(h + 1) * CHUNK], out)
    return out


def _causal_stack(w, transposed):
    r = lax.broadcasted_iota(jnp.int32, w.shape, 0) % CHUNK
    c = lax.broadcasted_iota(jnp.int32, w.shape, 1)
    keep = (c >= r) if transposed else (c <= r)
    return jnp.where(keep, w, 0.0)


def _mix_core_forward(proj, zprev, prm, da, db):
    n = proj.shape[0]
    ua = proj[:, 0:da]
    va = proj[:, da:2 * da]
    bg = proj[:, 2 * da:2 * da + db]
    cg = proj[:, 2 * da + db:2 * da + 2 * db]
    xb = proj[:, 2 * da + 2 * db:]
    ug, ucdf = _gelu_parts(ua)
    vg, vcdf = _gelu_parts(va)
    zc = vg - _head_mean(vg, prm["pmat"])
    rs = lax.rsqrt(_head_mean(zc * zc, prm["pmat"]) + EPS)
    vhat = zc * rs
    vln = (vhat * prm["lng"] + prm["lnb"]).astype(BF16)
    wst = _causal_stack(prm["wst"], False).astype(BF16)
    masks = _head_masks(da)
    mixed = []
    for j in range(n // CHUNK):
        res = _dot(wst, vln[j * CHUNK:(j + 1) * CHUNK])
        mixed.append(_select_heads(res, masks) + prm["bias"])
    mixed = mixed[0] if len(mixed) == 1 else jnp.concatenate(mixed, axis=0)
    ya = ug * mixed
    z = cg * xb
    row = lax.broadcasted_iota(jnp.int32, z.shape, 0)
    z1 = jnp.where(row == 0, zprev[7:8], pltpu.roll(z, 1, 0))
    z2 = jnp.where(row == 0, zprev[6:7], jnp.where(row == 1, zprev[7:8], pltpu.roll(z, 2, 0)))
    cw = prm["convw"]
    conv = z2 * cw[0:1] + z1 * cw[1:2] + z * cw[2:3]
    yb = bg * conv
    yah, ra = _rms(ya)
    ybh, rb = _rms(yb)
    return dict(ua=ua, va=va, bg=bg, cg=cg, xb=xb, ug=ug, ucdf=ucdf, vcdf=vcdf, rs=rs, vhat=vhat, vln=vln,
                mixed=mixed, z=z, z1=z1, z2=z2, conv=conv, yah=yah, ra=ra, ybh=ybh, rb=rb, masks=masks)


def _mix_params(lng_ref, lnb_ref, wst_ref, bias_ref, pmat_ref, convw_ref):
    return dict(lng=lng_ref[...], lnb=lnb_ref[...], wst=wst_ref[...], bias=bias_ref[...],
                pmat=pmat_ref[...], convw=convw_ref[...])


def _mix_core_fwd(proj, x, gate, wout, lng, lnb, wst, bias, pmat, convw, og, comm=None):
    T, P = proj.shape
    D = x.shape[1]
    B = gate.shape[0]
    da = lng.shape[1]
    db = convw.shape[1]
    tm = _tile(MIX_TILE, T // B)
    tps = (T // B) // tm

    def body(proj_ref, x_ref, gate_ref, wout_ref, lng_ref, lnb_ref, wst_ref, bias_ref, pmat_ref, convw_ref,
             og_ref, xo_ref, yn_ref, halo):
        i = pl.program_id(0)

        @pl.when((i % tps) == 0)
        def _():
            halo[...] = jnp.zeros_like(halo)

        prm = _mix_params(lng_ref, lnb_ref, wst_ref, bias_ref, pmat_ref, convw_ref)
        r = _mix_core_forward(proj_ref[...], halo[...], prm, da, db)
        halo[...] = r["z"][tm - 8:tm]
        og = og_ref[...]
        yn_ref[:, 0:da] = (r["yah"] * og[:, 0:da]).astype(BF16)
        yn_ref[:, da:] = (r["ybh"] * og[:, da:]).astype(BF16)
        xo_ref[...] = x_ref[...] + gate_ref[0] * _dot(yn_ref[...], wout_ref[...])

    full = lambda a: pl.BlockSpec(a.shape, lambda i: (0,) * a.ndim)
    return _call(
        body, "mix_core_fwd", (T // tm,),
        [
            pl.BlockSpec((tm, P), lambda i: (i, 0)),
            pl.BlockSpec((tm, D), lambda i: (i, 0)),
            pl.BlockSpec((1, 1, D), lambda i: (i // tps, 0, 0)),
            full(wout), full(lng), full(lnb), full(wst), full(bias), full(pmat), full(convw), full(og),
        ],
        [pl.BlockSpec((tm, D), lambda i: (i, 0)), pl.BlockSpec((tm, D), lambda i: (i, 0))],
        [jax.ShapeDtypeStruct((T, D), F32), jax.ShapeDtypeStruct((T, D), BF16)],
        [pltpu.VMEM((8, db), F32)],
        (proj, x, gate, wout, lng, lnb, wst, bias, pmat, convw, og), comm)


def _mix_core_bwd(proj, dxo, gate, wout, lng, lnb, wst, wstt, bias, pmat, convw, og, comm=None):
    T, P = proj.shape
    D = dxo.shape[1]
    B = gate.shape[0]
    da = lng.shape[1]
    db = convw.shape[1]
    assert da == db and P == 2 * da + 3 * db
    tm = _tile(MIX_TILE, T // B)
    tps = (T // B) // tm
    nt = T // tm
    hd = da // N_HEADS

    def body(proj_ref, cgp_ref, xbp_ref, dxo_ref, gate_ref, wout_ref, lng_ref, lnb_ref, wst_ref, wstt_ref,
             bias_ref, pmat_ref, convw_ref, og_ref,
             dproj_ref, do_ref, dgate_ref, dog_ref, dwst_ref, dbias_ref, dlng_ref, dlnb_ref, dconvw_ref, carry):
        i = pl.program_id(0)
        ri = nt - 1 - i
        first = i == 0
        end_of_seq = (ri % tps) == tps - 1
        start_of_seq = (ri % tps) == 0

        @pl.when(end_of_seq)
        def _():
            carry[...] = jnp.zeros_like(carry)

        prm = _mix_params(lng_ref, lnb_ref, wst_ref, bias_ref, pmat_ref, convw_ref)
        zprev = jnp.where(start_of_seq, 0.0, cgp_ref[...] * xbp_ref[...])
        r = _mix_core_forward(proj_ref[...], zprev, prm, da, db)
        og = og_ref[...]
        pmat = prm["pmat"]

        yn = jnp.concatenate([(r["yah"] * og[:, 0:da]).astype(BF16), (r["ybh"] * og[:, da:]).astype(BF16)], axis=1)
        dxo = dxo_ref[...]
        o = _dot(yn, wout_ref[...])
        _acc(dgate_ref.at[0], end_of_seq, jnp.sum(dxo * o, axis=0, keepdims=True))
        d_o = (gate_ref[0] * dxo).astype(BF16)
        do_ref[...] = d_o
        dyn = _dot_nt(d_o, wout_ref[...])

        def rms_bwd(dyn_g, yh, rr, og_g):
            dog_g = jnp.sum(dyn_g * yh, axis=0, keepdims=True)
            dyh = dyn_g * og_g
            return rr * (dyh - yh * jnp.mean(dyh * yh, axis=-1, keepdims=True)), dog_g

        dya, dog_a = rms_bwd(dyn[:, 0:da], r["yah"], r["ra"], og[:, 0:da])
        dyb, dog_b = rms_bwd(dyn[:, da:], r["ybh"], r["rb"], og[:, da:])
        _acc(dog_ref, first, jnp.concatenate([dog_a, dog_b], axis=1))

        dug = dya * r["mixed"]
        dmixed = dya * r["ug"]
        wstt_b = _causal_stack(wstt_ref[...], True).astype(BF16)
        masks = r["masks"]
        dbias = jnp.zeros((CHUNK, da), F32)
        dwst = jnp.zeros((N_HEADS * CHUNK, CHUNK), F32)
        dvln = []
        for j in range(tm // CHUNK):
            dm = dmixed[j * CHUNK:(j + 1) * CHUNK]
            dbias = dbias + dm
            dmb = dm.astype(BF16)
            stack = jnp.concatenate([jnp.where(masks[h], dmb, jnp.zeros_like(dmb)) for h in range(N_HEADS)], axis=0)
            dwst = dwst + _dot_nt(stack, r["vln"][j * CHUNK:(j + 1) * CHUNK])
            dvln.append(_select_heads(_dot(wstt_b, dmb), masks))
        dvln = dvln[0] if len(dvln) == 1 else jnp.concatenate(dvln, axis=0)
        _acc(dbias_ref, first, dbias)
        _acc(dwst_ref, first, dwst)
        _acc(dlng_ref, first, jnp.sum(dvln * r["vhat"], axis=0, keepdims=True))
        _acc(dlnb_ref, first, jnp.sum(dvln, axis=0, keepdims=True))
        dvhat = dvln * prm["lng"]
        dvg = r["rs"] * (dvhat - _head_mean(dvhat, pmat) - r["vhat"] * _head_mean(dvhat * r["vhat"], pmat))
        dproj_ref[:, 0:da] = (dug * _gelu_grad(r["ua"], r["ucdf"])).astype(BF16)
        dproj_ref[:, da:2 * da] = (dvg * _gelu_grad(r["va"], r["vcdf"])).astype(BF16)

        dproj_ref[:, 2 * da:2 * da + db] = (dyb * r["conv"]).astype(BF16)
        dconv = dyb * r["bg"]
        dcw = jnp.concatenate([
            jnp.sum(dconv * r["z2"], axis=0, keepdims=True),
            jnp.sum(dconv * r["z1"], axis=0, keepdims=True),
            jnp.sum(dconv * r["z"], axis=0, keepdims=True),
            jnp.zeros((5, db), F32)], axis=0)
        _acc(dconvw_ref, first, dcw)
        nxt = carry[...]
        row = lax.broadcasted_iota(jnp.int32, dconv.shape, 0)
        dc1 = jnp.where(row == tm - 1, nxt[0:1], pltpu.roll(dconv, tm - 1, 0))
        dc2 = jnp.where(row == tm - 2, nxt[0:1], jnp.where(row == tm - 1, nxt[1:2], pltpu.roll(dconv, tm - 2, 0)))
        carry[...] = dconv[0:8]
        cw = prm["convw"]
        dz = dconv * cw[2:3] + dc1 * cw[1:2] + dc2 * cw[0:1]
        dproj_ref[:, 2 * da + db:2 * da + 2 * db] = (dz * r["xb"]).astype(BF16)
        dproj_ref[:, 2 * da + 2 * db:] = (dz * r["cg"]).astype(BF16)

        @pl.when(i == nt - 1)
        def _():
            dwst_ref[...] = _causal_stack(dwst_ref[...], False)
            dbias_ref[...] = _head_mean(dbias_ref[...], pmat) * float(hd)

    full = lambda a: pl.BlockSpec(a.shape, lambda i: (0,) * a.ndim)
    const = lambda i: (0, 0)
    rev = lambda i: (nt - 1 - i, 0)
    prev8 = lambda col: (lambda i: (jnp.maximum((nt - 1 - i) * (tm // 8) - 1, 0), col))
    return _call(
        body, "mix_core_bwd", (nt,),
        [
            pl.BlockSpec((tm, P), rev),
            pl.BlockSpec((8, db), prev8((2 * da + db) // db)),
            pl.BlockSpec((8, db), prev8((2 * da + 2 * db) // db)),
            pl.BlockSpec((tm, D), rev),
            pl.BlockSpec((1, 1, D), lambda i: ((nt - 1 - i) // tps, 0, 0)),
            full(wout), full(lng), full(lnb), full(wst), full(wstt), full(bias), full(pmat), full(convw), full(og),
        ],
        [
            pl.BlockSpec((tm, P), rev),
            pl.BlockSpec((tm, D), rev),
            pl.BlockSpec((1, 1, D), lambda i: ((nt - 1 - i) // tps, 0, 0)),
            pl.BlockSpec((1, D), const),
            pl.BlockSpec((N_HEADS * CHUNK, CHUNK), const),
            pl.BlockSpec((CHUNK, da), const),
            pl.BlockSpec((1, da), const),
            pl.BlockSpec((1, da), const),
            pl.BlockSpec((8, db), const),
        ],
        [
            jax.ShapeDtypeStruct((T, P), BF16),
            jax.ShapeDtypeStruct((T, D), BF16),
            jax.ShapeDtypeStruct((B, 1, D), F32),
            jax.ShapeDtypeStruct((1, D), F32),
            jax.ShapeDtypeStruct((N_HEADS * CHUNK, CHUNK), F32),
            jax.ShapeDtypeStruct((CHUNK, da), F32),
            jax.ShapeDtypeStruct((1, da), F32),
            jax.ShapeDtypeStruct((1, da), F32),
            jax.ShapeDtypeStruct((8, db), F32),
        ],
        [pltpu.VMEM((8, db), F32)],
        (proj, proj, proj, dxo, gate, wout, lng, lnb, wst, wstt, bias, pmat, convw, og), comm)


def _loss_head(x, target, gain):
    T, D = x.shape
    tm = _tile(TOKEN_TILE, T)

    def body(x_ref, t_ref, gain_ref, dx_ref, loss_ref, dgain_ref):
        first = pl.program_id(0) == 0
        xh, r = _rms(x_ref[...])
        gain = gain_ref[...]
        err = xh * gain - t_ref[...]
        _acc(loss_ref, first, jnp.zeros((8, 128), F32) + 0.5 * jnp.sum(err * err) / D)
        dout = err * (1.0 / D)
        _acc(dgain_ref, first, jnp.sum(dout * xh, axis=0, keepdims=True))
        dy = dout * gain
        dx_ref[...] = r * (dy - xh * jnp.mean(dy * xh, axis=-1, keepdims=True))

    return _call(
        body, "loss_head", (T // tm,),
        [
            pl.BlockSpec((tm, D), lambda i: (i, 0)),
            pl.BlockSpec((tm, D), lambda i: (i, 0)),
            pl.BlockSpec((1, D), lambda i: (0, 0)),
        ],
        [
            pl.BlockSpec((tm, D), lambda i: (i, 0)),
            pl.BlockSpec((8, 128), lambda i: (0, 0)),
            pl.BlockSpec((1, D), lambda i: (0, 0)),
        ],
        [
            jax.ShapeDtypeStruct((T, D), F32),
            jax.ShapeDtypeStruct((8, 128), F32),
            jax.ShapeDtypeStruct((1, D), F32),
        ],
        [],
        (x, target, gain))[0]


def _ada_fwd(c_all, ada_w, ada_b):
    n, D = c_all.shape
    L, _, sa = ada_w.shape
    tn = _tile(768, sa)

    def body(c_ref, w_ref, b_ref, act_ref, o_ref):
        c = c_ref[...]
        act = (c * _sigmoid(c)).astype(BF16)
        act_ref[...] = act
        o_ref[...] = _dot(act, w_ref[...].astype(BF16)) + b_ref[...]

    return _call(
        body, "ada_fwd", (L, sa // tn),
        [
            pl.BlockSpec((n, D), lambda l, j: (0, 0)),
            pl.BlockSpec((None, D, tn), lambda l, j: (l, 0, j)),
            pl.BlockSpec((None, 1, tn), lambda l, j: (l, 0, j)),
        ],
        [
            pl.BlockSpec((n, D), lambda l, j: (0, 0)),
            pl.BlockSpec((None, n, tn), lambda l, j: (l, 0, j)),
        ],
        [jax.ShapeDtypeStruct((n, D), BF16), jax.ShapeDtypeStruct((L, n, sa), F32)],
        [],
        (c_all, ada_w, ada_b))[0]


def _colsum(a):
    L, n, C = a.shape

    def body(a_ref, o_ref):
        o_ref[...] = jnp.sum(a_ref[...], axis=0, keepdims=True)

    return _call(
        body, "colsum", (L,),
        [pl.BlockSpec((None, n, C), lambda l: (l, 0, 0))],
        [pl.BlockSpec((None, 1, C), lambda l: (l, 0, 0))],
        [jax.ShapeDtypeStruct((L, 1, C), F32)],
        [],
        (a,))[0][0]


def _row_tile(rows, cols, nbuf):
    budget = VMEM_LIMIT // 3 // (2 * nbuf * 4 * cols)
    t = rows
    while t > max(budget, 8) and t % 2 == 0 and (t // 2) % 8 == 0:
        t //= 2
    return t


def _pair_sum(g, recv, core):
    n, _, R, C = g.shape
    tr = _row_tile(R, C, 3)

    def body(core_ref, g_ref, r_ref, o_ref):
        o_ref[...] = (g_ref[...] + r_ref[...]).astype(BF16)

    return pl.pallas_call(
        body,
        name="pair_sum",
        grid_spec=pltpu.PrefetchScalarGridSpec(
            num_scalar_prefetch=1,
            grid=(n, R // tr),
            in_specs=[
                pl.BlockSpec((None, None, tr, C), lambda i, r, core_ref: (i, core_ref[0], r, 0)),
                pl.BlockSpec((None, tr, C), lambda i, r, core_ref: (i, r, 0)),
            ],
            out_specs=pl.BlockSpec((None, tr, C), lambda i, r, core_ref: (i, r, 0)),
        ),
        out_shape=jax.ShapeDtypeStruct((n, R, C), BF16),
        compiler_params=pltpu.CompilerParams(dimension_semantics=("arbitrary", "arbitrary"),
                                             vmem_limit_bytes=VMEM_LIMIT),
    )(core, g, recv)


def _chip_sum(q, core, l, n_layers, prev):
    nq, R, C = q.shape
    tr = _row_tile(R, C, 4)

    def body(core_ref, q_ref, *rest):
        o_ref = rest[-1]
        s = q_ref[0].astype(F32)
        for j in range(1, nq):
            s = s + q_ref[j].astype(F32)
        o_ref[...] = s

    in_specs = [pl.BlockSpec((nq, tr, C), lambda r, core_ref: (0, r, 0))]
    args = [core, q]
    aliases = {}
    if prev is not None:
        in_specs.append(ANY)
        args.append(prev)
        aliases = {2: 0}
    return pl.pallas_call(
        body,
        name="chip_sum",
        grid_spec=pltpu.PrefetchScalarGridSpec(
            num_scalar_prefetch=1,
            grid=(R // tr,),
            in_specs=in_specs,
            out_specs=pl.BlockSpec((None, None, tr, C), lambda r, core_ref: (l, core_ref[0], r, 0)),
        ),
        out_shape=jax.ShapeDtypeStruct((n_layers, 2, R, C), F32),
        input_output_aliases=aliases,
        compiler_params=pltpu.CompilerParams(dimension_semantics=("arbitrary",), vmem_limit_bytes=VMEM_LIMIT),
    )(*args)


def _sum_blocks(a, n):
    M = a.shape[0] // n
    C = a.shape[1]

    def body(a_ref, o_ref):
        s = a_ref[0:M]
        for j in range(1, n):
            s = s + a_ref[j * M:(j + 1) * M]
        o_ref[...] = s

    return pl.pallas_call(
        body,
        name="sum_blocks",
        out_shape=jax.ShapeDtypeStruct((M, C), F32),
        compiler_params=pltpu.CompilerParams(vmem_limit_bytes=VMEM_LIMIT),
    )(a)


def _adamw(w, g, m, v):
    R, C = w.shape
    tr = _row_tile(R, C, 7) if R % 8 == 0 else R

    def body(w_ref, g_ref, m_ref, v_ref, d_ref, nm_ref, nv_ref):
        g = g_ref[...]
        m = ADAM_B1 * m_ref[...] + (1.0 - ADAM_B1) * g
        v = ADAM_B2 * v_ref[...] + (1.0 - ADAM_B2) * (g * g)
        m_hat = m / (1.0 - ADAM_B1 ** ADAM_STEP)
        v_hat = v / (1.0 - ADAM_B2 ** ADAM_STEP)
        d_ref[...] = -ADAM_LR * (m_hat / (jnp.sqrt(v_hat) + ADAM_EPS) + ADAM_WD * w_ref[...])
        nm_ref[...] = m
        nv_ref[...] = v

    spec = pl.BlockSpec((tr, C), lambda i: (i, 0))
    return _call(body, "adamw", (R // tr,), [spec] * 4, [spec] * 3, [jax.ShapeDtypeStruct((R, C), F32)] * 3, [],
                 (w, g, m, v))[0]


def kernel(x, c, ada_w, ada_b, norm_ffn1_g, ffn1_w_gu, ffn1_w_down, norm_mix_g, mix_w_in, sgu_ln_g, sgu_ln_b, sgu_w_s, sgu_b, conv_w, out_norm_g, mix_w_out, norm_ffn2_g, ffn2_w_gu, ffn2_w_down, final_norm_g, loss_target, m_ada_w, m_ada_b, m_norm_ffn1_g, m_ffn1_w_gu, m_ffn1_w_down, m_norm_mix_g, m_mix_w_in, m_sgu_ln_g, m_sgu_ln_b, m_sgu_w_s, m_sgu_b, m_conv_w, m_out_norm_g, m_mix_w_out, m_norm_ffn2_g, m_ffn2_w_gu, m_ffn2_w_down, m_final_norm_g, v_ada_w, v_ada_b, v_norm_ffn1_g, v_ffn1_w_gu, v_ffn1_w_down, v_norm_mix_g, v_mix_w_in, v_sgu_ln_g, v_sgu_ln_b, v_sgu_w_s, v_sgu_b, v_conv_w, v_out_norm_g, v_mix_w_out, v_norm_ffn2_g, v_ffn2_w_gu, v_ffn2_w_down, v_final_norm_g):
    weights = dict(ada_w=ada_w, ada_b=ada_b, norm_ffn1_g=norm_ffn1_g, ffn1_w_gu=ffn1_w_gu, ffn1_w_down=ffn1_w_down,
                   norm_mix_g=norm_mix_g, mix_w_in=mix_w_in, sgu_ln_g=sgu_ln_g, sgu_ln_b=sgu_ln_b, sgu_w_s=sgu_w_s,
                   sgu_b=sgu_b, conv_w=conv_w, out_norm_g=out_norm_g, mix_w_out=mix_w_out, norm_ffn2_g=norm_ffn2_g,
                   ffn2_w_gu=ffn2_w_gu, ffn2_w_down=ffn2_w_down, final_norm_g=final_norm_g)
    m_in = dict(ada_w=m_ada_w, ada_b=m_ada_b, norm_ffn1_g=m_norm_ffn1_g, ffn1_w_gu=m_ffn1_w_gu,
                ffn1_w_down=m_ffn1_w_down, norm_mix_g=m_norm_mix_g, mix_w_in=m_mix_w_in, sgu_ln_g=m_sgu_ln_g,
                sgu_ln_b=m_sgu_ln_b, sgu_w_s=m_sgu_w_s, sgu_b=m_sgu_b, conv_w=m_conv_w, out_norm_g=m_out_norm_g,
                mix_w_out=m_mix_w_out, norm_ffn2_g=m_norm_ffn2_g, ffn2_w_gu=m_ffn2_w_gu, ffn2_w_down=m_ffn2_w_down,
                final_norm_g=m_final_norm_g)
    v_in = dict(ada_w=v_ada_w, ada_b=v_ada_b, norm_ffn1_g=v_norm_ffn1_g, ffn1_w_gu=v_ffn1_w_gu,
                ffn1_w_down=v_ffn1_w_down, norm_mix_g=v_norm_mix_g, mix_w_in=v_mix_w_in, sgu_ln_g=v_sgu_ln_g,
                sgu_ln_b=v_sgu_ln_b, sgu_w_s=v_sgu_w_s, sgu_b=v_sgu_b, conv_w=v_conv_w, out_norm_g=v_out_norm_g,
                mix_w_out=v_mix_w_out, norm_ffn2_g=v_norm_ffn2_g, ffn2_w_gu=v_ffn2_w_gu, ffn2_w_down=v_ffn2_w_down,
                final_norm_g=v_final_norm_g)

    B, S, D = x.shape
    T = B * S
    L = ada_w.shape[0]
    F = ffn1_w_down.shape[1] * N_CHIP
    P = mix_w_in.shape[2] * N_CHIP
    DA = D // 2
    DB = D - DA
    HD = DA // N_HEADS
    SA = ada_w.shape[2]
    n_all = B * N_DEV
    mx, my, mc = _position()
    chip = 2 * mx + my
    dev = 2 * chip + mc
    core = jnp.reshape(mc, (1,)).astype(jnp.int32)

    c_all = _all_gather(c.reshape(8, B * D // 8), "gather_c").reshape(n_all, D)
    ada_b_mine = lax.dynamic_slice_in_dim(ada_b, chip * SA, SA, axis=1).reshape(L, 1, SA)
    c_act, ada_part = _ada_fwd(c_all, ada_w, ada_b_mine)
    ada_all = _all_gather(ada_part.reshape(L * n_all, SA), "gather_ada").reshape(N_CHIP, 2, L, n_all, SA)[:, 0]
    ada_all = jnp.transpose(ada_all, (1, 2, 0, 3)).reshape(L, n_all, N_CHIP * SA)
    ada = lax.dynamic_slice_in_dim(ada_all, dev * B, B, axis=1).reshape(L, B, N_MOD, 1, D)
    mods = [[ada[l, :, j] for j in range(N_MOD)] for l in range(L)]

    cw_block = jnp.pad(conv_w.reshape(L * conv_w.shape[1], conv_w.shape[2]), ((0, 8 - L * conv_w.shape[1]), (0, 0)))
    cw_all = _all_gather(cw_block, "gather_conv_w").reshape(N_CHIP, 2, 8, conv_w.shape[2])[:, 0, :L * conv_w.shape[1]]
    conv_full = jnp.transpose(cw_all.reshape(N_CHIP, L, conv_w.shape[1], conv_w.shape[2]), (1, 2, 0, 3))
    conv_full = conv_full.reshape(L, conv_w.shape[1], DB)

    big = ["ffn1_w_gu", "ffn1_w_down", "mix_w_in", "mix_w_out", "ffn2_w_gu", "ffn2_w_down"]
    col_sharded = dict(ffn1_w_gu=True, ffn1_w_down=False, mix_w_in=True, mix_w_out=False,
                       ffn2_w_gu=True, ffn2_w_down=False)
    shards = {k: weights[k].astype(BF16) for k in big}
    gather = lambda l, *names: _gather_comm([(shards[k], l, col_sharded[k]) for k in names])
    full = [dict() for _ in range(L)]

    def arrived(l, names, res):
        full[l].update(zip(names, res))

    x0 = x.reshape(T, D)
    gains = lambda name, l: weights[name][l].reshape(1, D)
    hmask = jnp.repeat(jnp.eye(N_HEADS, dtype=F32), HD, axis=0)
    pmat = (jnp.repeat(hmask, HD, axis=1) / HD).astype(BF16)

    def mix_consts(l):
        lng = jnp.tile(sgu_ln_g[l], N_HEADS).reshape(1, DA)
        lnb = jnp.tile(sgu_ln_b[l], N_HEADS).reshape(1, DA)
        wst = sgu_w_s[l].reshape(N_HEADS * CHUNK, CHUNK)
        wstt = jnp.swapaxes(sgu_w_s[l], 1, 2).reshape(N_HEADS * CHUNK, CHUNK)
        bias = jnp.repeat(jnp.transpose(sgu_b[l]), HD, axis=1)
        return lng, lnb, wst, wstt, bias

    arrived(0, big[:2], _comm_call(gather(0, *big[:2]), "gather_first"))
    saved = []
    xc = x0
    for l in range(L):
        sh1, sc1, g1, sh2, sc2, g2, sh3, sc3, g3 = mods[l]
        lng, lnb, wst, wstt, bias = mix_consts(l)
        w = full[l]
        nxt = l + 1 < L
        if l == 0:
            (gu1, a1, h1), got = _ffn_up(xc, gains("norm_ffn1_g", l), sh1, sc1, w["ffn1_w_gu"],
                                         gather(l, "mix_w_in", "mix_w_out", "ffn2_w_gu"))
            arrived(l, ["mix_w_in", "mix_w_out", "ffn2_w_gu"], got)
            (xa, f1), got = _ffn_down(a1, xc, g1, w["ffn1_w_down"], gather(l, "ffn2_w_down"))
            arrived(l, ["ffn2_w_down"], got)
        else:
            (gu1, a1, h1), got = _ffn_up(xc, gains("norm_ffn1_g", l), sh1, sc1, w["ffn1_w_gu"], gather(l, "ffn2_w_gu"))
            arrived(l, ["ffn2_w_gu"], got)
            (xa, f1), _ = _ffn_down(a1, xc, g1, w["ffn1_w_down"])
        (proj, h2), got = _mixin_fwd(xa, gains("norm_mix_g", l), sh2, sc2, w["mix_w_in"],
                                     gather(l + 1, "mix_w_in") if nxt else None)
        if nxt:
            arrived(l + 1, ["mix_w_in"], got)
        (xb, yn), got = _mix_core_fwd(proj, xa, g2, w["mix_w_out"], lng, lnb, wst, bias, pmat, conv_full[l],
                                      gains("out_norm_g", l),
                                      gather(l + 1, "ffn1_w_down", "mix_w_out") if nxt else None)
        if nxt:
            arrived(l + 1, ["ffn1_w_down", "mix_w_out"], got)
        (gu2, a2, h3), got = _ffn_up(xb, gains("norm_ffn2_g", l), sh3, sc3, w["ffn2_w_gu"],
                                     gather(l + 1, "ffn1_w_gu") if nxt else None)
        if nxt:
            arrived(l + 1, ["ffn1_w_gu"], got)
        (xd, f2), got = _ffn_down(a2, xb, g3, w["ffn2_w_down"], gather(l + 1, "ffn2_w_down") if nxt else None)
        if nxt:
            arrived(l + 1, ["ffn2_w_down"], got)
        saved.append(dict(x0=xc, xa=xa, xb=xb, gu1=gu1, a1=a1, h1=h1, f1=f1, proj=proj, h2=h2, yn=yn,
                          gu2=gu2, a2=a2, h3=h3, f2=f2))
        xc = xd

    dx, loss_block, d_final = _loss_head(xc, loss_target.reshape(T, D), final_norm_g.reshape(1, D))
    loss = lax.psum(loss_block[0, 0], ("x", "y", "c"))

    reduced = dict.fromkeys(big)

    def halves(name, g):
        if g.ndim == 4:
            return g
        return g.reshape(N_CHIP, 2, weights[name].shape[1] // 2, g.shape[-1])

    class Reduction:
        def __init__(self, l, names, grads):
            self.l, self.names, self.gs = l, names, [halves(k, g) for k, g in zip(names, grads)]
            self.stage = 0

        def step(self):
            self.stage += 1
            if self.stage == 1:
                return _sibling_half_comm(self.gs)
            if self.stage == 2:
                ps = [_pair_sum(g, r, core) for g, r in zip(self.gs, self.got)]
                return _scatter_comm(ps)
            if self.stage == 3:
                for k, q in zip(self.names, self.got):
                    reduced[k] = _chip_sum(q, core, self.l, L, reduced[k])
                return _share_comm([reduced[k] for k in self.names], self.l)
            for k, r in zip(self.names, self.got):
                reduced[k] = r
            return None

        def done(self, got):
            self.got = got

    def ride(red, fn, *args):
        comm = red.step() if red is not None else None
        res, got = fn(*args, comm=comm)
        if comm is not None:
            red.done(got)
        return res

    def ride2(red_a, red_b, fn, *args):
        ca = red_a.step() if red_a is not None else None
        cb = red_b.step() if red_b is not None else None
        res, got = fn(*args, comm=_merge(ca, cb))
        na = len(ca.out_shape) if ca is not None else 0
        if ca is not None:
            red_a.done(got[:na])
        if cb is not None:
            red_b.done(got[na:])
        return res

    small = [None] * L
    d_ada = [None] * L
    pending = None
    for l in reversed(range(L)):
        sh1, sc1, g1, sh2, sc2, g2, sh3, sc3, g3 = mods[l]
        lng, lnb, wst, wstt, bias = mix_consts(l)
        s = saved[l]
        w = full[l]
        dgu, df, dg3 = ride(pending, _ffn_bwd_down, dx, s["f2"], s["gu2"], g3, w["ffn2_w_down"])
        dx, dsc3, dsh3, dgain3 = ride(pending, _ffn_bwd_up, dx, s["xb"], dgu, gains("norm_ffn2_g", l), sc3, w["ffn2_w_gu"])
        g_gu2, = ride(pending, _wgrad, s["h3"], dgu, D, 2 * F // N_CHIP, True, "wgrad_gu")
        if pending is not None:
            pending.step()
        g_d2, = _wgrad(s["a2"], df[None], F // 2, D, False, "wgrad_down")[0]
        red_ffn2 = Reduction(l, ["ffn2_w_gu", "ffn2_w_down"], [g_gu2, g_d2])
        dproj, d_o, dg2, dog, dwst, dbias, dlng, dlnb, dconvw = ride(
            red_ffn2, _mix_core_bwd, s["proj"], dx, g2, w["mix_w_out"], lng, lnb, wst, wstt, bias, pmat, conv_full[l],
            gains("out_norm_g", l))
        g_out, = _wgrad(s["yn"], d_o[None], D, D, False, "wgrad_out", 2 * WGRAD_TOKENS)[0]
        dx, dsc2, dsh2, dgain2 = _mixin_bwd(dx, s["xa"], dproj, gains("norm_mix_g", l), sc2, w["mix_w_in"])[0]
        g_in, = _wgrad(s["h2"], dproj[None], D, P // N_CHIP, True, "wgrad_in", 2 * WGRAD_TOKENS)[0]
        red_mix = Reduction(l, ["mix_w_in", "mix_w_out"], [g_in, g_out])
        dgu, df, dg1 = ride2(red_ffn2, red_mix, _ffn_bwd_down, dx, s["f1"], s["gu1"], g1, w["ffn1_w_down"])
        dx, dsc1, dsh1, dgain1 = ride2(red_ffn2, red_mix, _ffn_bwd_up, dx, s["x0"], dgu, gains("norm_ffn1_g", l), sc1,
                                       w["ffn1_w_gu"])
        red_ffn2.step()
        g_gu1, = ride(red_mix, _wgrad, s["h1"], dgu, D, 2 * F // N_CHIP, True, "wgrad_gu")
        red_mix.step()
        g_d1, = _wgrad(s["a1"], df[None], F // 2, D, False, "wgrad_down")[0]
        pending = Reduction(l, ["ffn1_w_gu", "ffn1_w_down"], [g_gu1, g_d1])
        d_ada[l] = jnp.concatenate([dsh1, dsc1, dg1, dsh2, dsc2, dg2, dsh3, dsc3, dg3], axis=1).reshape(B, N_MOD * D)
        small[l] = [dgain1, dgain2, dgain3, dog, dlng, dlnb, dwst, dbias[:, ::HD], dconvw]
    for stage in ("reduce_pair", "reduce_chips", "reduce_share"):
        pending.done(_comm_call(pending.step(), stage))
    pending.step()
    grad_x = dx.reshape(B, S, D)
    grads = {k: reduced[k].reshape(weights[k].shape) for k in big}

    d_ada_all = _all_gather(jnp.stack(d_ada).reshape(L * B, N_MOD * D), "gather_d_ada")
    d_ada_all = jnp.transpose(d_ada_all.reshape(N_DEV, L, B, N_MOD * D), (1, 0, 2, 3)).reshape(L, n_all, N_MOD * D)
    grads["ada_b"] = _colsum(d_ada_all).reshape(L, N_MOD * D)
    d_ada_mine = lax.dynamic_slice_in_dim(d_ada_all, chip * SA, SA, axis=2).astype(BF16)
    grads["ada_w"] = jnp.stack([_wgrad(c_act, d_ada_mine[l][None], D, _tile(768, SA), False, "wgrad_ada")[0][0]
                                for l in range(L)]).reshape(L, D, SA)

    flat = [a.reshape(-1, 128) for l in range(L) for a in small[l]] + [d_final.reshape(-1, 128)]
    n_rows = sum(a.shape[0] for a in flat)
    pad = (-n_rows) % 8
    packed = jnp.concatenate(flat + [jnp.zeros((pad, 128), F32)], axis=0)
    total = _sum_blocks(_all_gather(packed, "gather_small"), N_DEV)
    pieces, at = [], 0
    for a in flat:
        pieces.append(total[at:at + a.shape[0]])
        at += a.shape[0]
    per_layer = len(small[0])
    stack = lambda j, shape: jnp.stack([pieces[l * per_layer + j].reshape(shape) for l in range(L)])
    grads["norm_ffn1_g"] = stack(0, (D,))
    grads["norm_mix_g"] = stack(1, (D,))
    grads["norm_ffn2_g"] = stack(2, (D,))
    grads["out_norm_g"] = stack(3, (D,))
    grads["sgu_ln_g"] = stack(4, (N_HEADS, HD)).sum(axis=1)
    grads["sgu_ln_b"] = stack(5, (N_HEADS, HD)).sum(axis=1)
    grads["sgu_w_s"] = stack(6, (N_HEADS, CHUNK, CHUNK))
    grads["sgu_b"] = jnp.swapaxes(stack(7, (CHUNK, N_HEADS)), 1, 2)
    g_conv = stack(8, (8, DB))[:, :conv_w.shape[1]]
    grads["conv_w"] = lax.dynamic_slice_in_dim(g_conv, chip * conv_w.shape[2], conv_w.shape[2], axis=2)
    grads["final_norm_g"] = pieces[-1].reshape(D)

    names = list(weights)
    delta, new_m, new_v = {}, {}, {}
    for k in names:
        wk = weights[k]
        view = (1, wk.shape[0]) if wk.ndim == 1 else (-1, wk.shape[-1])
        d, nm, nv = _adamw(wk.reshape(view), grads[k].reshape(view), m_in[k].reshape(view), v_in[k].reshape(view))
        delta[k], new_m[k], new_v[k] = d.reshape(wk.shape), nm.reshape(wk.shape), nv.reshape(wk.shape)

    return (loss, grad_x, *[grads[k] for k in names], *[delta[k] for k in names],
            *[new_m[k] for k in names], *[new_v[k] for k in names])
```

```python
import functools
import math

import jax
import jax.numpy as jnp
from jax import lax
from jax.experimental import pallas as pl
from jax.experimental.pallas import tpu as pltpu

F32 = jnp.float32
BF16 = jnp.bfloat16
MESH = pl.DeviceIdType.MESH

N_HEADS = 8
CHUNK = 128
N_MOD = 9
EPS = 1e-6
N_DEV = 8
N_CHIP = 4

ADAM_LR = 0.001
ADAM_B1 = 0.9
ADAM_B2 = 0.999
ADAM_EPS = 1e-08
ADAM_WD = 0.01
ADAM_STEP = 10

TOKEN_TILE = 512
FF_TILE = 1408
MIX_TILE = 256
WGRAD_TOKENS = 2048
ELEMENTWISE_COLS = 512
VMEM_LIMIT = 56 * 1024 * 1024

ANY = pl.BlockSpec(memory_space=pl.ANY)


def _tile(pref, n):
    t = min(pref, n)
    assert n % t == 0, (pref, n)
    return t


def _dot(a, b):
    return jnp.dot(a, b, preferred_element_type=F32)


def _dot_nt(a, b):
    return lax.dot_general(a, b, (((1,), (1,)), ((), ())), preferred_element_type=F32)


def _dot_tn(a, b):
    return lax.dot_general(a, b, (((0,), (0,)), ((), ())), preferred_element_type=F32)


def _sigmoid(x):
    return 1.0 / (1.0 + jnp.exp(-x))


def _sigmoid_fast(x):
    return pl.reciprocal(1.0 + jnp.exp(-x), approx=True)


def _rms(x):
    r = lax.rsqrt(jnp.mean(x * x, axis=-1, keepdims=True) + EPS)
    return x * r, r


def _norm_mod_bwd(x, dh, gain, sc):
    xh, r = _rms(x)
    dsc = jnp.sum(dh * (xh * gain), axis=0, keepdims=True)
    dsh = jnp.sum(dh, axis=0, keepdims=True)
    dn = dh * (1.0 + sc)
    dgain = jnp.sum(dn * xh, axis=0, keepdims=True)
    dy = dn * gain
    dx = r * (dy - xh * jnp.mean(dy * xh, axis=-1, keepdims=True))
    return dx, dsc, dsh, dgain


def _acc(ref, first, val):
    @pl.when(first)
    def _():
        ref[...] = val

    @pl.when(jnp.logical_not(first))
    def _():
        ref[...] += val


class _Comm:
    def __init__(self, args, out_shape, scratch, phases, aliases=None):
        self.args, self.out_shape, self.scratch = list(args), list(out_shape), list(scratch)
        self.phases, self.aliases = phases, dict(aliases or {})


def _merge(*comms):
    comms = [c for c in comms if c is not None]
    if len(comms) <= 1:
        return comms[0] if comms else None
    args = [a for c in comms for a in c.args]
    out_shape = [o for c in comms for o in c.out_shape]
    scratch = [s for c in comms for s in c.scratch]
    aliases, ai, oi = {}, 0, 0
    for c in comms:
        aliases.update({ai + i: oi + o for i, o in c.aliases.items()})
        ai += len(c.args)
        oi += len(c.out_shape)

    def phases(ins, outs, sems):
        parts, ai, oi, si = [], 0, 0, 0
        for c in comms:
            parts.append(c.phases(ins[ai:ai + len(c.args)], outs[oi:oi + len(c.out_shape)], sems[si:si + len(c.scratch)]))
            ai, oi, si = ai + len(c.args), oi + len(c.out_shape), si + len(c.scratch)

        def run(k):
            def go():
                for p in parts:
                    if p[k] is not None:
                        p[k]()
            return go
        return run(0), run(1), run(2)

    return _Comm(args, out_shape, scratch, phases, aliases)


def _call(body, name, grid, in_specs, out_specs, out_shape, scratch, args, comm=None):
    n_in, n_out, n_scr = len(in_specs), len(out_specs), len(scratch)
    sem = ("arbitrary",) * len(grid)
    params = pltpu.CompilerParams(dimension_semantics=sem, vmem_limit_bytes=VMEM_LIMIT)
    if comm is None:
        res = pl.pallas_call(body, name=name, grid=grid, in_specs=in_specs, out_specs=out_specs, out_shape=out_shape,
                             scratch_shapes=scratch, compiler_params=params)(*args)
        return list(res), []
    m_in, m_out = len(comm.args), len(comm.out_shape)

    def full(*refs):
        c_in, c_min = refs[:n_in], refs[n_in:n_in + m_in]
        o = n_in + m_in
        c_out, c_mout = refs[o:o + n_out], refs[o + n_out:o + n_out + m_out]
        o += n_out + m_out
        c_scr, c_sem = refs[o:o + n_scr], refs[o + n_scr:]
        start, mid, finish = comm.phases(c_min, c_mout, c_sem)
        ids = [pl.program_id(a) for a in range(len(grid))]
        first = functools.reduce(jnp.logical_and, [i == 0 for i in ids])
        last = functools.reduce(jnp.logical_and, [i == g - 1 for i, g in zip(ids, grid)])
        pl.when(first)(start)
        if mid is not None:
            pl.when(last)(mid)
        body(*c_in, *c_out, *c_scr)
        pl.when(last)(finish)

    res = pl.pallas_call(
        full, name=name, grid=grid,
        in_specs=list(in_specs) + [ANY] * m_in,
        out_specs=list(out_specs) + [ANY] * m_out,
        out_shape=list(out_shape) + comm.out_shape,
        scratch_shapes=list(scratch) + comm.scratch,
        input_output_aliases={n_in + i: n_out + o for i, o in comm.aliases.items()},
        compiler_params=params,
    )(*args, *comm.args)
    return list(res[:n_out]), list(res[n_out:])


def _comm_call(comm, name):
    m_in, m_out = len(comm.args), len(comm.out_shape)

    def body(*refs):
        start, mid, finish = comm.phases(refs[:m_in], refs[m_in:m_in + m_out], refs[m_in + m_out:])
        start()
        if mid is not None:
            mid()
        finish()

    res = pl.pallas_call(
        body, name=name, in_specs=[ANY] * m_in, out_specs=[ANY] * m_out, out_shape=comm.out_shape,
        scratch_shapes=comm.scratch, input_output_aliases=comm.aliases,
    )(*comm.args)
    return list(res)


def _position():
    return lax.axis_index("x"), lax.axis_index("y"), lax.axis_index("c")


def _gather_comm(items):
    n = len(items)
    half = [s.shape[1] // 2 for s, _, _ in items]

    def full_shape(i):
        s, _, col = items[i]
        _, R, C = s.shape
        return jax.ShapeDtypeStruct((R, N_CHIP * C) if col else (N_CHIP * R, C), s.dtype)

    def phases(ins, outs, sems):
        send_sems, recv_sems, local_sems = sems
        x, y, c = _position()

        def region(i, chip, h):
            s, _, col = items[i]
            _, R, C = s.shape
            if col:
                return outs[i].at[pl.ds(h * half[i], half[i]), pl.ds(chip * C, C)]
            return outs[i].at[pl.ds(chip * R + h * half[i], half[i]), :]

        def mine(i, h):
            return ins[i].at[items[i][1], pl.ds(h * half[i], half[i]), :]

        def copies(kx, ky, kc):
            k_me = 2 * kx + ky
            sibling = (kx, ky, 1 - kc)
            chips = [(1 - kx, ky), (kx, 1 - ky), (1 - kx, 1 - ky)]
            local, first, passed, arrive_ici, arrive_d2d = [], [], [], [], []

            def remote(src, dst, s, to):
                return pltpu.make_async_remote_copy(src_ref=src, dst_ref=dst, send_sem=send_sems.at[s],
                                                    recv_sem=recv_sems.at[s], device_id=to, device_id_type=MESH)

            for i in range(n):
                for h in range(2):
                    local.append(pltpu.make_async_copy(mine(i, h), region(i, k_me, h), local_sems.at[2 * i + h]))
                for j, (px, py) in enumerate(chips):
                    s = 6 * i + j
                    first.append(remote(mine(i, kc), region(i, k_me, kc), s, (px, py, kc)))
                    got = region(i, 2 * px + py, kc)
                    arrive_ici.append(remote(got, got, s, (px, py, kc)))
                    passed.append(remote(got, got, s + 3, sibling))
                    other = region(i, 2 * px + py, 1 - kc)
                    arrive_d2d.append(remote(other, other, s + 3, sibling))
            return local, first, passed, arrive_ici, arrive_d2d

        def on_each_device(fn):
            def go():
                for kx in range(2):
                    for ky in range(2):
                        for kc in range(2):
                            pl.when((x == kx) & (y == ky) & (c == kc))(functools.partial(fn, *copies(kx, ky, kc)))
            return go

        def start(local, first, passed, arrive_ici, arrive_d2d):
            for cp in local + first:
                cp.start()

        def mid(local, first, passed, arrive_ici, arrive_d2d):
            for a, p in zip(arrive_ici, passed):
                a.wait_recv()
                p.start()

        def finish(local, first, passed, arrive_ici, arrive_d2d):
            for a in arrive_d2d:
                a.wait_recv()
            for cp in first + passed:
                cp.wait_send()
            for cp in local:
                cp.wait()

        return on_each_device(start), on_each_device(mid), on_each_device(finish)

    scratch = [pltpu.SemaphoreType.DMA((6 * n,)), pltpu.SemaphoreType.DMA((6 * n,)), pltpu.SemaphoreType.DMA((2 * n,))]
    return _Comm([s for s, _, _ in items], [full_shape(i) for i in range(n)], scratch, phases)


def _sibling_half_comm(gs):
    n = len(gs)

    def phases(ins, outs, sems):
        send_sems, recv_sems = sems
        x, y, c = _position()

        def copies():
            return [pltpu.make_async_remote_copy(
                src_ref=ins[i].at[:, 1 - c], dst_ref=outs[i], send_sem=send_sems.at[i], recv_sem=recv_sems.at[i],
                device_id=(x, y, 1 - c), device_id_type=MESH) for i in range(n)]

        def start():
            for cp in copies():
                cp.start()

        def finish():
            for cp in copies():
                cp.wait()

        return start, None, finish

    out_shape = [jax.ShapeDtypeStruct(g.shape[:1] + g.shape[2:], g.dtype) for g in gs]
    return _Comm(gs, out_shape, [pltpu.SemaphoreType.DMA((n,)), pltpu.SemaphoreType.DMA((n,))], phases)


def _scatter_comm(ps):
    n = len(ps)

    def phases(ins, outs, sems):
        send_sems, recv_sems, local_sems = sems
        x, y, c = _position()
        k_me = 2 * x + y
        chips = [(1 - x, y), (x, 1 - y), (1 - x, 1 - y)]

        def copies():
            local = [pltpu.make_async_copy(ins[i].at[k_me], outs[i].at[k_me], local_sems.at[i]) for i in range(n)]
            remote = [pltpu.make_async_remote_copy(
                src_ref=ins[i].at[2 * px + py], dst_ref=outs[i].at[k_me],
                send_sem=send_sems.at[3 * i + j], recv_sem=recv_sems.at[3 * i + j],
                device_id=(px, py, c), device_id_type=MESH) for i in range(n) for j, (px, py) in enumerate(chips)]
            return local, remote

        def start():
            local, remote = copies()
            for cp in local + remote:
                cp.start()

        def finish():
            local, remote = copies()
            for cp in remote + local:
                cp.wait()

        return start, None, finish

    scratch = [pltpu.SemaphoreType.DMA((3 * n,)), pltpu.SemaphoreType.DMA((3 * n,)), pltpu.SemaphoreType.DMA((n,))]
    return _Comm(ps, [jax.ShapeDtypeStruct(p.shape, p.dtype) for p in ps], scratch, phases)


def _share_comm(rs, l):
    n = len(rs)

    def phases(ins, outs, sems):
        send_sems, recv_sems = sems
        x, y, c = _position()

        def copy(i, h):
            return pltpu.make_async_remote_copy(
                src_ref=outs[i].at[l, h], dst_ref=outs[i].at[l, h], send_sem=send_sems.at[i], recv_sem=recv_sems.at[i],
                device_id=(x, y, 1 - c), device_id_type=MESH)

        def start():
            for i in range(n):
                copy(i, c).start()

        def finish():
            for i in range(n):
                copy(i, 1 - c).wait_recv()
            for i in range(n):
                copy(i, c).wait_send()

        return start, None, finish

    return _Comm(rs, [jax.ShapeDtypeStruct(r.shape, r.dtype) for r in rs],
                 [pltpu.SemaphoreType.DMA((n,)), pltpu.SemaphoreType.DMA((n,))], phases,
                 aliases={i: i for i in range(n)})


def _all_gather(block, name):
    m_per, n = block.shape

    def body(x_ref, out_ref, send_sems, recv_sems, local_sem):
        x, y, c = _position()
        me, sibling = (x, y, c), (x, y, 1 - c)
        chips = [(1 - x, y), (x, 1 - y), (1 - x, 1 - y)]

        def rows(px, py, pc):
            return out_ref.at[pl.ds((4 * px + 2 * py + pc) * m_per, m_per), :]

        def copy(k, blk, to, src=None):
            return pltpu.make_async_remote_copy(
                src_ref=rows(*blk) if src is None else src, dst_ref=rows(*blk),
                send_sem=send_sems.at[k], recv_sem=recv_sems.at[k], device_id=to, device_id_type=MESH)

        mine = pltpu.make_async_copy(x_ref, rows(*me), local_sem)
        mine.start()
        first = [copy(0, me, sibling, src=x_ref)]
        first += [copy(1 + j, me, (*chip, c), src=x_ref) for j, chip in enumerate(chips)]
        for cp in first:
            cp.start()
        passed = [copy(4 + j, (*chip, c), sibling) for j, chip in enumerate(chips)]
        for j, chip in enumerate(chips):
            copy(1 + j, (*chip, c), me).wait_recv()
            passed[j].start()
        copy(0, sibling, me).wait_recv()
        for j, chip in enumerate(chips):
            copy(4 + j, (*chip, 1 - c), me).wait_recv()
        for cp in first + passed:
            cp.wait_send()
        mine.wait()

    return pl.pallas_call(
        body,
        name=name,
        out_shape=jax.ShapeDtypeStruct((N_DEV * m_per, n), block.dtype),
        in_specs=[pl.BlockSpec(memory_space=pltpu.VMEM)],
        out_specs=pl.BlockSpec(memory_space=pltpu.VMEM),
        scratch_shapes=[pltpu.SemaphoreType.DMA((7,)), pltpu.SemaphoreType.DMA((7,)), pltpu.SemaphoreType.DMA],
        compiler_params=pltpu.CompilerParams(vmem_limit_bytes=VMEM_LIMIT),
    )(block)


def _ffn_up(x, gain, sh, sc, wgu, comm=None):
    T, D = x.shape
    F = wgu.shape[1] // 2
    B = sh.shape[0]
    tm = _tile(TOKEN_TILE, T // B)
    tf = _tile(FF_TILE, F)
    tps = (T // B) // tm
    nf = F // tf

    def body(x_ref, gain_ref, sh_ref, sc_ref, wg_ref, wu_ref, gu_ref, a_ref):
        xh, _ = _rms(x_ref[...])
        h = (xh * gain_ref[...] * (1.0 + sc_ref[0]) + sh_ref[0]).astype(BF16)
        g = _dot(h, wg_ref[...])
        u = _dot(h, wu_ref[...])
        gu_ref[0] = g.astype(BF16)
        gu_ref[1] = u.astype(BF16)
        a_ref[...] = (g * _sigmoid(g) * u).astype(BF16)

    seq = lambda k, i: (i // tps, 0, 0)
    return _call(
        body, "ffn_up", (nf, T // tm),
        [
            pl.BlockSpec((tm, D), lambda k, i: (i, 0)),
            pl.BlockSpec((1, D), lambda k, i: (0, 0)),
            pl.BlockSpec((1, 1, D), seq),
            pl.BlockSpec((1, 1, D), seq),
            pl.BlockSpec((D, tf), lambda k, i: (0, k)),
            pl.BlockSpec((D, tf), lambda k, i: (0, nf + k)),
        ],
        [
            pl.BlockSpec((2, tm, tf), lambda k, i: (0, i, k)),
            pl.BlockSpec((tm, tf), lambda k, i: (i, k)),
        ],
        [
            jax.ShapeDtypeStruct((2, T, F), BF16),
            jax.ShapeDtypeStruct((T, F), BF16),
        ],
        [],
        (x, gain, sh, sc, wgu, wgu), comm)


def _ffn_down(a, x, gate, wd, comm=None):
    T, F = a.shape
    D = x.shape[1]
    B = gate.shape[0]
    tm = _tile(TOKEN_TILE, T // B)
    tps = (T // B) // tm

    def body(a_ref, x_ref, gate_ref, wd_ref, xo_ref, f_ref):
        f = _dot(a_ref[...], wd_ref[...])
        f_ref[...] = f.astype(BF16)
        xo_ref[...] = x_ref[...] + 0.5 * gate_ref[0] * f

    return _call(
        body, "ffn_down", (T // tm,),
        [
            pl.BlockSpec((tm, F), lambda i: (i, 0)),
            pl.BlockSpec((tm, D), lambda i: (i, 0)),
            pl.BlockSpec((1, 1, D), lambda i: (i // tps, 0, 0)),
            pl.BlockSpec((F, D), lambda i: (0, 0)),
        ],
        [pl.BlockSpec((tm, D), lambda i: (i, 0)), pl.BlockSpec((tm, D), lambda i: (i, 0))],
        [jax.ShapeDtypeStruct((T, D), F32), jax.ShapeDtypeStruct((T, D), BF16)],
        [],
        (a, x, gate, wd), comm)


def _ffn_bwd_down(dxo, gu, gate, wd, comm=None):
    T, D = dxo.shape
    F = wd.shape[0]
    B = gate.shape[0]
    tm = _tile(TOKEN_TILE, T // B)
    tf = _tile(FF_TILE, F)
    tps = (T // B) // tm
    nf = F // tf
    cw = min(ELEMENTWISE_COLS, tf)

    def body(dxo_ref, gu_ref, gate_ref, wd_ref, dgu_ref):
        df = (0.5 * gate_ref[0] * dxo_ref[...]).astype(BF16)
        for c0 in range(0, tf, cw):
            w = min(cw, tf - c0)
            da = _dot_nt(df, wd_ref[c0:c0 + w, :])
            g = gu_ref[0, :, c0:c0 + w].astype(F32)
            sg = _sigmoid_fast(g)
            dgu_ref[1, :, c0:c0 + w] = (da * (g * sg)).astype(BF16)
            dgu_ref[0, :, c0:c0 + w] = (da * gu_ref[1, :, c0:c0 + w].astype(F32)
                                        * (sg * (1.0 + g * (1.0 - sg)))).astype(BF16)

    return _call(
        body, "ffn_bwd_down", (nf, T // tm),
        [
            pl.BlockSpec((tm, D), lambda k, i: (i, 0)),
            pl.BlockSpec((2, tm, tf), lambda k, i: (0, i, k)),
            pl.BlockSpec((1, 1, D), lambda k, i: (i // tps, 0, 0)),
            pl.BlockSpec((tf, D), lambda k, i: (k, 0)),
        ],
        [pl.BlockSpec((2, tm, tf), lambda k, i: (0, i, k))],
        [jax.ShapeDtypeStruct((2, T, F), BF16)],
        [],
        (dxo, gu, gate, wd), comm)


def _ffn_bwd_up(dxo, x, dgu, f, gain, sh, sc, gate, wgu, comm=None):
    T, D = x.shape
    F = wgu.shape[1] // 2
    B = sc.shape[0]
    tm = _tile(TOKEN_TILE, T // B)
    tps = (T // B) // tm

    def body(dxo_ref, x_ref, dgu_ref, f_ref, gain_ref, sh_ref, sc_ref, gate_ref, w_ref,
             dx_ref, h_ref, df_ref, dsc_ref, dsh_ref, dgain_ref, dgate_ref):
        i = pl.program_id(0)
        first_of_seq = (i % tps) == 0
        dh = _dot_nt(dgu_ref[0], w_ref[:, 0:F]) + _dot_nt(dgu_ref[1], w_ref[:, F:])
        dxo = dxo_ref[...]
        x = x_ref[...]
        gain = gain_ref[...]
        sc = sc_ref[0]
        dx, dsc, dsh, dgain = _norm_mod_bwd(x, dh, gain, sc)
        dx_ref[...] = dxo + dx
        h_ref[...] = (_rms(x)[0] * gain * (1.0 + sc) + sh_ref[0]).astype(BF16)
        df_ref[...] = (0.5 * gate_ref[0] * dxo).astype(BF16)
        _acc(dsc_ref.at[0], first_of_seq, dsc)
        _acc(dsh_ref.at[0], first_of_seq, dsh)
        _acc(dgain_ref, i == 0, dgain)
        _acc(dgate_ref.at[0], first_of_seq, 0.5 * jnp.sum(dxo * f_ref[...].astype(F32), axis=0, keepdims=True))

    seq = lambda i: (i // tps, 0, 0)
    row = lambda i: (i, 0)
    return _call(
        body, "ffn_bwd_up", (T // tm,),
        [
            pl.BlockSpec((tm, D), row),
            pl.BlockSpec((tm, D), row),
            pl.BlockSpec((2, tm, F), lambda i: (0, i, 0)),
            pl.BlockSpec((tm, D), row),
            pl.BlockSpec((1, D), lambda i: (0, 0)),
            pl.BlockSpec((1, 1, D), seq),
            pl.BlockSpec((1, 1, D), seq),
            pl.BlockSpec((1, 1, D), seq),
            pl.BlockSpec((D, 2 * F), lambda i: (0, 0), pipeline_mode=pl.Buffered(1)),
        ],
        [
            pl.BlockSpec((tm, D), row),
            pl.BlockSpec((tm, D), row),
            pl.BlockSpec((tm, D), row),
            pl.BlockSpec((1, 1, D), seq),
            pl.BlockSpec((1, 1, D), seq),
            pl.BlockSpec((1, D), lambda i: (0, 0)),
            pl.BlockSpec((1, 1, D), seq),
        ],
        [
            jax.ShapeDtypeStruct((T, D), F32),
            jax.ShapeDtypeStruct((T, D), BF16),
            jax.ShapeDtypeStruct((T, D), BF16),
            jax.ShapeDtypeStruct((B, 1, D), F32),
            jax.ShapeDtypeStruct((B, 1, D), F32),
            jax.ShapeDtypeStruct((1, D), F32),
            jax.ShapeDtypeStruct((B, 1, D), F32),
        ],
        [],
        (dxo, x, dgu, f, gain, sh, sc, gate, wgu), comm)


def _wgrad(a, b, tmm, tn, col_major, name, tokens=WGRAD_TOKENS, comm=None):
    T, M = a.shape
    nb, _, Nb = b.shape
    N = nb * Nb
    tk = _tile(tokens, T)
    npb = Nb // tn
    assert M % tmm == 0 and Nb % tn == 0
    if col_major:
        assert tmm == M
        shape = (N // tn, 2, M // 2, tn)
        out_spec = pl.BlockSpec((None, 2, M // 2, tn), lambda i, j, t: (j, 0, 0, 0))
    else:
        shape = (M // tmm, tmm, N)
        out_spec = pl.BlockSpec((None, tmm, tn), lambda i, j, t: (i, 0, j))

    def body(a_ref, b_ref, o_ref):
        t = pl.program_id(2)
        res = _dot_tn(a_ref[...], b_ref[...])
        if col_major:
            _acc(o_ref.at[0], t == 0, res[:M // 2])
            _acc(o_ref.at[1], t == 0, res[M // 2:])
        else:
            _acc(o_ref, t == 0, res)

    return _call(
        body, name, (M // tmm, N // tn, T // tk),
        [
            pl.BlockSpec((tk, tmm), lambda i, j, t: (t, i)),
            pl.BlockSpec((None, tk, tn), lambda i, j, t: (j // npb, t, j % npb)),
        ],
        [out_spec], [jax.ShapeDtypeStruct(shape, F32)], [],
        (a, b), comm)


def _mixin_fwd(x, gain, sh, sc, win, comm=None):
    T, D = x.shape
    P = win.shape[1]
    B = sh.shape[0]
    tm = _tile(TOKEN_TILE, T // B)
    tps = (T // B) // tm

    def body(x_ref, gain_ref, sh_ref, sc_ref, w_ref, proj_ref, h_ref):
        xh, _ = _rms(x_ref[...])
        h = (xh * gain_ref[...] * (1.0 + sc_ref[0]) + sh_ref[0]).astype(BF16)
        h_ref[...] = h
        proj_ref[...] = _dot(h, w_ref[...])

    seq = lambda i: (i // tps, 0, 0)
    return _call(
        body, "mixin_fwd", (T // tm,),
        [
            pl.BlockSpec((tm, D), lambda i: (i, 0)),
            pl.BlockSpec((1, D), lambda i: (0, 0)),
            pl.BlockSpec((1, 1, D), seq),
            pl.BlockSpec((1, 1, D), seq),
            pl.BlockSpec((D, P), lambda i: (0, 0)),
        ],
        [pl.BlockSpec((tm, P), lambda i: (i, 0)), pl.BlockSpec((tm, D), lambda i: (i, 0))],
        [jax.ShapeDtypeStruct((T, P), F32), jax.ShapeDtypeStruct((T, D), BF16)],
        [],
        (x, gain, sh, sc, win), comm)


def _mixin_bwd(dxo, x, dproj, gain, sc, win, comm=None):
    T, D = x.shape
    P = win.shape[1]
    B = sc.shape[0]
    tm = _tile(TOKEN_TILE, T // B)
    tps = (T // B) // tm

    def body(dxo_ref, x_ref, dp_ref, gain_ref, sc_ref, w_ref, dx_ref, dsc_ref, dsh_ref, dgain_ref):
        i = pl.program_id(0)
        first_of_seq = (i % tps) == 0
        dh = _dot_nt(dp_ref[...], w_ref[...])
        dx, dsc, dsh, dgain = _norm_mod_bwd(x_ref[...], dh, gain_ref[...], sc_ref[0])
        dx_ref[...] = dxo_ref[...] + dx
        _acc(dsc_ref.at[0], first_of_seq, dsc)
        _acc(dsh_ref.at[0], first_of_seq, dsh)
        _acc(dgain_ref, i == 0, dgain)

    seq = lambda i: (i // tps, 0, 0)
    row = lambda i: (i, 0)
    return _call(
        body, "mixin_bwd", (T // tm,),
        [
            pl.BlockSpec((tm, D), row),
            pl.BlockSpec((tm, D), row),
            pl.BlockSpec((tm, P), row),
            pl.BlockSpec((1, D), lambda i: (0, 0)),
            pl.BlockSpec((1, 1, D), seq),
            pl.BlockSpec((D, P), lambda i: (0, 0)),
        ],
        [
            pl.BlockSpec((tm, D), row),
            pl.BlockSpec((1, 1, D), seq),
            pl.BlockSpec((1, 1, D), seq),
            pl.BlockSpec((1, D), lambda i: (0, 0)),
        ],
        [
            jax.ShapeDtypeStruct((T, D), F32),
            jax.ShapeDtypeStruct((B, 1, D), F32),
            jax.ShapeDtypeStruct((B, 1, D), F32),
            jax.ShapeDtypeStruct((1, D), F32),
        ],
        [],
        (dxo, x, dproj, gain, sc, win), comm)


def _head_mean(z, pmat):
    hi = z.astype(BF16)
    lo = (z - hi.astype(F32)).astype(BF16)
    return _dot(hi, pmat) + _dot(lo, pmat)


def _gelu_parts(x):
    cdf = 0.5 * (1.0 + lax.erf(x * (1.0 / math.sqrt(2.0))))
    return x * cdf, cdf


def _gelu_grad(x, cdf):
    return cdf + x * jnp.exp(-0.5 * x * x) * (1.0 / math.sqrt(2.0 * math.pi))


def _head_masks(da):
    hd = da // N_HEADS
    col = lax.broadcasted_iota(jnp.int32, (1, da), 1)
    return [(col >= h * hd) & (col < (h + 1) * hd) for h in range(N_HEADS)]


def _select_heads(res, masks):
    out = res[0:CHUNK]
    for h in range(1, N_HEADS):
        out = jnp.where(masks[h], res[h * CHUNK:(h + 1) * CHUNK], out)
    return out


def _causal_stack(w, transposed):
    r = lax.broadcasted_iota(jnp.int32, w.shape, 0) % CHUNK
    c = lax.broadcasted_iota(jnp.int32, w.shape, 1)
    keep = (c >= r) if transposed else (c <= r)
    return jnp.where(keep, w, 0.0)


def _mix_core_forward(proj, zprev, prm, da, db):
    n = proj.shape[0]
    ua = proj[:, 0:da]
    va = proj[:, da:2 * da]
    bg = proj[:, 2 * da:2 * da + db]
    cg = proj[:, 2 * da + db:2 * da + 2 * db]
    xb = proj[:, 2 * da + 2 * db:]
    ug, ucdf = _gelu_parts(ua)
    vg, vcdf = _gelu_parts(va)
    zc = vg - _head_mean(vg, prm["pmat"])
    rs = lax.rsqrt(_head_mean(zc * zc, prm["pmat"]) + EPS)
    vhat = zc * rs
    vln = (vhat * prm["lng"] + prm["lnb"]).astype(BF16)
    wst = _causal_stack(prm["wst"], False).astype(BF16)
    masks = _head_masks(da)
    mixed = []
    for j in range(n // CHUNK):
        res = _dot(wst, vln[j * CHUNK:(j + 1) * CHUNK])
        mixed.append(_select_heads(res, masks) + prm["bias"])
    mixed = mixed[0] if len(mixed) == 1 else jnp.concatenate(mixed, axis=0)
    ya = ug * mixed
    z = cg * xb
    row = lax.broadcasted_iota(jnp.int32, z.shape, 0)
    z1 = jnp.where(row == 0, zprev[7:8], pltpu.roll(z, 1, 0))
    z2 = jnp.where(row == 0, zprev[6:7], jnp.where(row == 1, zprev[7:8], pltpu.roll(z, 2, 0)))
    cw = prm["convw"]
    conv = z2 * cw[0:1] + z1 * cw[1:2] + z * cw[2:3]
    yb = bg * conv
    yah, ra = _rms(ya)
    ybh, rb = _rms(yb)
    return dict(ua=ua, va=va, bg=bg, cg=cg, xb=xb, ug=ug, ucdf=ucdf, vcdf=vcdf, rs=rs, vhat=vhat, vln=vln,
                mixed=mixed, z=z, z1=z1, z2=z2, conv=conv, yah=yah, ra=ra, ybh=ybh, rb=rb, masks=masks)


def _mix_params(lng_ref, lnb_ref, wst_ref, bias_ref, pmat_ref, convw_ref):
    return dict(lng=lng_ref[...], lnb=lnb_ref[...], wst=wst_ref[...], bias=bias_ref[...],
                pmat=pmat_ref[...], convw=convw_ref[...])


def _mix_core_fwd(proj, x, gate, wout, lng, lnb, wst, bias, pmat, convw, og, comm=None):
    T, P = proj.shape
    D = x.shape[1]
    B = gate.shape[0]
    da = lng.shape[1]
    db = convw.shape[1]
    tm = _tile(MIX_TILE, T // B)
    tps = (T // B) // tm

    def body(proj_ref, x_ref, gate_ref, wout_ref, lng_ref, lnb_ref, wst_ref, bias_ref, pmat_ref, convw_ref,
             og_ref, xo_ref, yn_ref, halo):
        i = pl.program_id(0)

        @pl.when((i % tps) == 0)
        def _():
            halo[...] = jnp.zeros_like(halo)

        prm = _mix_params(lng_ref, lnb_ref, wst_ref, bias_ref, pmat_ref, convw_ref)
        r = _mix_core_forward(proj_ref[...], halo[...], prm, da, db)
        halo[...] = r["z"][tm - 8:tm]
        og = og_ref[...]
        yn_ref[:, 0:da] = (r["yah"] * og[:, 0:da]).astype(BF16)
        yn_ref[:, da:] = (r["ybh"] * og[:, da:]).astype(BF16)
        xo_ref[...] = x_ref[...] + gate_ref[0] * _dot(yn_ref[...], wout_ref[...])

    full = lambda a: pl.BlockSpec(a.shape, lambda i: (0,) * a.ndim)
    return _call(
        body, "mix_core_fwd", (T // tm,),
        [
            pl.BlockSpec((tm, P), lambda i: (i, 0)),
            pl.BlockSpec((tm, D), lambda i: (i, 0)),
            pl.BlockSpec((1, 1, D), lambda i: (i // tps, 0, 0)),
            full(wout), full(lng), full(lnb), full(wst), full(bias), full(pmat), full(convw), full(og),
        ],
        [pl.BlockSpec((tm, D), lambda i: (i, 0)), pl.BlockSpec((tm, D), lambda i: (i, 0))],
        [jax.ShapeDtypeStruct((T, D), F32), jax.ShapeDtypeStruct((T, D), BF16)],
        [pltpu.VMEM((8, db), F32)],
        (proj, x, gate, wout, lng, lnb, wst, bias, pmat, convw, og), comm)


def _mix_core_bwd(proj, dxo, gate, wout, lng, lnb, wst, wstt, bias, pmat, convw, og, comm=None):
    T, P = proj.shape
    D = dxo.shape[1]
    B = gate.shape[0]
    da = lng.shape[1]
    db = convw.shape[1]
    assert da == db and P == 2 * da + 3 * db
    tm = _tile(MIX_TILE, T // B)
    tps = (T // B) // tm
    nt = T // tm
    hd = da // N_HEADS

    def body(proj_ref, cgp_ref, xbp_ref, dxo_ref, gate_ref, wout_ref, lng_ref, lnb_ref, wst_ref, wstt_ref,
             bias_ref, pmat_ref, convw_ref, og_ref,
             dproj_ref, do_ref, dgate_ref, dog_ref, dwst_ref, dbias_ref, dlng_ref, dlnb_ref, dconvw_ref, carry):
        i = pl.program_id(0)
        ri = nt - 1 - i
        first = i == 0
        end_of_seq = (ri % tps) == tps - 1
        start_of_seq = (ri % tps) == 0

        @pl.when(end_of_seq)
        def _():
            carry[...] = jnp.zeros_like(carry)

        prm = _mix_params(lng_ref, lnb_ref, wst_ref, bias_ref, pmat_ref, convw_ref)
        zprev = jnp.where(start_of_seq, 0.0, cgp_ref[...] * xbp_ref[...])
        r = _mix_core_forward(proj_ref[...], zprev, prm, da, db)
        og = og_ref[...]
        pmat = prm["pmat"]

        yn = jnp.concatenate([(r["yah"] * og[:, 0:da]).astype(BF16), (r["ybh"] * og[:, da:]).astype(BF16)], axis=1)
        dxo = dxo_ref[...]
        o = _dot(yn, wout_ref[...])
        _acc(dgate_ref.at[0], end_of_seq, jnp.sum(dxo * o, axis=0, keepdims=True))
        d_o = (gate_ref[0] * dxo).astype(BF16)
        do_ref[...] = d_o
        dyn = _dot_nt(d_o, wout_ref[...])

        def rms_bwd(dyn_g, yh, rr, og_g):
            dog_g = jnp.sum(dyn_g * yh, axis=0, keepdims=True)
            dyh = dyn_g * og_g
            return rr * (dyh - yh * jnp.mean(dyh * yh, axis=-1, keepdims=True)), dog_g

        dya, dog_a = rms_bwd(dyn[:, 0:da], r["yah"], r["ra"], og[:, 0:da])
        dyb, dog_b = rms_bwd(dyn[:, da:], r["ybh"], r["rb"], og[:, da:])
        _acc(dog_ref, first, jnp.concatenate([dog_a, dog_b], axis=1))

        dug = dya * r["mixed"]
        dmixed = dya * r["ug"]
        wstt_b = _causal_stack(wstt_ref[...], True).astype(BF16)
        masks = r["masks"]
        dbias = jnp.zeros((CHUNK, da), F32)
        dwst = jnp.zeros((N_HEADS * CHUNK, CHUNK), F32)
        dvln = []
        for j in range(tm // CHUNK):
            dm = dmixed[j * CHUNK:(j + 1) * CHUNK]
            dbias = dbias + dm
            dmb = dm.astype(BF16)
            stack = jnp.concatenate([jnp.where(masks[h], dmb, jnp.zeros_like(dmb)) for h in range(N_HEADS)], axis=0)
            dwst = dwst + _dot_nt(stack, r["vln"][j * CHUNK:(j + 1) * CHUNK])
            dvln.append(_select_heads(_dot(wstt_b, dmb), masks))
        dvln = dvln[0] if len(dvln) == 1 else jnp.concatenate(dvln, axis=0)
        _acc(dbias_ref, first, dbias)
        _acc(dwst_ref, first, dwst)
        _acc(dlng_ref, first, jnp.sum(dvln * r["vhat"], axis=0, keepdims=True))
        _acc(dlnb_ref, first, jnp.sum(dvln, axis=0, keepdims=True))
        dvhat = dvln * prm["lng"]
        dvg = r["rs"] * (dvhat - _head_mean(dvhat, pmat) - r["vhat"] * _head_mean(dvhat * r["vhat"], pmat))
        dproj_ref[:, 0:da] = (dug * _gelu_grad(r["ua"], r["ucdf"])).astype(BF16)
        dproj_ref[:, da:2 * da] = (dvg * _gelu_grad(r["va"], r["vcdf"])).astype(BF16)

        dproj_ref[:, 2 * da:2 * da + db] = (dyb * r["conv"]).astype(BF16)
        dconv = dyb * r["bg"]
        dcw = jnp.concatenate([
            jnp.sum(dconv * r["z2"], axis=0, keepdims=True),
            jnp.sum(dconv * r["z1"], axis=0, keepdims=True),
            jnp.sum(dconv * r["z"], axis=0, keepdims=True),
            jnp.zeros((5, db), F32)], axis=0)
        _acc(dconvw_ref, first, dcw)
        nxt = carry[...]
        row = lax.broadcasted_iota(jnp.int32, dconv.shape, 0)
        dc1 = jnp.where(row == tm - 1, nxt[0:1], pltpu.roll(dconv, tm - 1, 0))
        dc2 = jnp.where(row == tm - 2, nxt[0:1], jnp.where(row == tm - 1, nxt[1:2], pltpu.roll(dconv, tm - 2, 0)))
        carry[...] = dconv[0:8]
        cw = prm["convw"]
        dz = dconv * cw[2:3] + dc1 * cw[1:2] + dc2 * cw[0:1]
        dproj_ref[:, 2 * da + db:2 * da + 2 * db] = (dz * r["xb"]).astype(BF16)
        dproj_ref[:, 2 * da + 2 * db:] = (dz * r["cg"]).astype(BF16)

        @pl.when(i == nt - 1)
        def _():
            dwst_ref[...] = _causal_stack(dwst_ref[...], False)
            dbias_ref[...] = _head_mean(dbias_ref[...], pmat) * float(hd)

    full = lambda a: pl.BlockSpec(a.shape, lambda i: (0,) * a.ndim)
    const = lambda i: (0, 0)
    rev = lambda i: (nt - 1 - i, 0)
    prev8 = lambda col: (lambda i: (jnp.maximum((nt - 1 - i) * (tm // 8) - 1, 0), col))
    return _call(
        body, "mix_core_bwd", (nt,),
        [
            pl.BlockSpec((tm, P), rev),
            pl.BlockSpec((8, db), prev8((2 * da + db) // db)),
            pl.BlockSpec((8, db), prev8((2 * da + 2 * db) // db)),
            pl.BlockSpec((tm, D), rev),
            pl.BlockSpec((1, 1, D), lambda i: ((nt - 1 - i) // tps, 0, 0)),
            full(wout), full(lng), full(lnb), full(wst), full(wstt), full(bias), full(pmat), full(convw), full(og),
        ],
        [
            pl.BlockSpec((tm, P), rev),
            pl.BlockSpec((tm, D), rev),
            pl.BlockSpec((1, 1, D), lambda i: ((nt - 1 - i) // tps, 0, 0)),
            pl.BlockSpec((1, D), const),
            pl.BlockSpec((N_HEADS * CHUNK, CHUNK), const),
            pl.BlockSpec((CHUNK, da), const),
            pl.BlockSpec((1, da), const),
            pl.BlockSpec((1, da), const),
            pl.BlockSpec((8, db), const),
        ],
        [
            jax.ShapeDtypeStruct((T, P), BF16),
            jax.ShapeDtypeStruct((T, D), BF16),
            jax.ShapeDtypeStruct((B, 1, D), F32),
            jax.ShapeDtypeStruct((1, D), F32),
            jax.ShapeDtypeStruct((N_HEADS * CHUNK, CHUNK), F32),
            jax.ShapeDtypeStruct((CHUNK, da), F32),
            jax.ShapeDtypeStruct((1, da), F32),
            jax.ShapeDtypeStruct((1, da), F32),
            jax.ShapeDtypeStruct((8, db), F32),
        ],
        [pltpu.VMEM((8, db), F32)],
        (proj, proj, proj, dxo, gate, wout, lng, lnb, wst, wstt, bias, pmat, convw, og), comm)


def _loss_head(x, target, gain):
    T, D = x.shape
    tm = _tile(TOKEN_TILE, T)

    def body(x_ref, t_ref, gain_ref, dx_ref, loss_ref, dgain_ref):
        first = pl.program_id(0) == 0
        xh, r = _rms(x_ref[...])
        gain = gain_ref[...]
        err = xh * gain - t_ref[...]
        _acc(loss_ref, first, jnp.zeros((8, 128), F32) + 0.5 * jnp.sum(err * err) / D)
        dout = err * (1.0 / D)
        _acc(dgain_ref, first, jnp.sum(dout * xh, axis=0, keepdims=True))
        dy = dout * gain
        dx_ref[...] = r * (dy - xh * jnp.mean(dy * xh, axis=-1, keepdims=True))

    return _call(
        body, "loss_head", (T // tm,),
        [
            pl.BlockSpec((tm, D), lambda i: (i, 0)),
            pl.BlockSpec((tm, D), lambda i: (i, 0)),
            pl.BlockSpec((1, D), lambda i: (0, 0)),
        ],
        [
            pl.BlockSpec((tm, D), lambda i: (i, 0)),
            pl.BlockSpec((8, 128), lambda i: (0, 0)),
            pl.BlockSpec((1, D), lambda i: (0, 0)),
        ],
        [
            jax.ShapeDtypeStruct((T, D), F32),
            jax.ShapeDtypeStruct((8, 128), F32),
            jax.ShapeDtypeStruct((1, D), F32),
        ],
        [],
        (x, target, gain))[0]


def _ada_fwd(c_all, ada_w, ada_b):
    n, D = c_all.shape
    L, _, sa = ada_w.shape
    tn = _tile(768, sa)

    def body(c_ref, w_ref, b_ref, act_ref, o_ref):
        c = c_ref[...]
        act = (c * _sigmoid(c)).astype(BF16)
        act_ref[...] = act
        o_ref[...] = _dot(act, w_ref[...].astype(BF16)) + b_ref[...]

    return _call(
        body, "ada_fwd", (L, sa // tn),
        [
            pl.BlockSpec((n, D), lambda l, j: (0, 0)),
            pl.BlockSpec((None, D, tn), lambda l, j: (l, 0, j)),
            pl.BlockSpec((None, 1, tn), lambda l, j: (l, 0, j)),
        ],
        [
            pl.BlockSpec((n, D), lambda l, j: (0, 0)),
            pl.BlockSpec((None, n, tn), lambda l, j: (l, 0, j)),
        ],
        [jax.ShapeDtypeStruct((n, D), BF16), jax.ShapeDtypeStruct((L, n, sa), F32)],
        [],
        (c_all, ada_w, ada_b))[0]


def _colsum(a):
    L, n, C = a.shape

    def body(a_ref, o_ref):
        o_ref[...] = jnp.sum(a_ref[...], axis=0, keepdims=True)

    return _call(
        body, "colsum", (L,),
        [pl.BlockSpec((None, n, C), lambda l: (l, 0, 0))],
        [pl.BlockSpec((None, 1, C), lambda l: (l, 0, 0))],
        [jax.ShapeDtypeStruct((L, 1, C), F32)],
        [],
        (a,))[0][0]


def _row_tile(rows, cols, nbuf):
    budget = VMEM_LIMIT // 3 // (2 * nbuf * 4 * cols)
    t = rows
    while t > max(budget, 8) and t % 2 == 0 and (t // 2) % 8 == 0:
        t //= 2
    return t


def _pair_sum(g, recv, core):
    n, _, R, C = g.shape
    tr = _row_tile(R, C, 3)

    def body(core_ref, g_ref, r_ref, o_ref):
        o_ref[...] = (g_ref[...] + r_ref[...]).astype(BF16)

    return pl.pallas_call(
        body,
        name="pair_sum",
        grid_spec=pltpu.PrefetchScalarGridSpec(
            num_scalar_prefetch=1,
            grid=(n, R // tr),
            in_specs=[
                pl.BlockSpec((None, None, tr, C), lambda i, r, core_ref: (i, core_ref[0], r, 0)),
                pl.BlockSpec((None, tr, C), lambda i, r, core_ref: (i, r, 0)),
            ],
            out_specs=pl.BlockSpec((None, tr, C), lambda i, r, core_ref: (i, r, 0)),
        ),
        out_shape=jax.ShapeDtypeStruct((n, R, C), BF16),
        compiler_params=pltpu.CompilerParams(dimension_semantics=("arbitrary", "arbitrary"),
                                             vmem_limit_bytes=VMEM_LIMIT),
    )(core, g, recv)


def _chip_sum(q, core, l, n_layers, prev):
    nq, R, C = q.shape
    tr = _row_tile(R, C, 4)

    def body(core_ref, q_ref, *rest):
        o_ref = rest[-1]
        s = q_ref[0].astype(F32)
        for j in range(1, nq):
            s = s + q_ref[j].astype(F32)
        o_ref[...] = s

    in_specs = [pl.BlockSpec((nq, tr, C), lambda r, core_ref: (0, r, 0))]
    args = [core, q]
    aliases = {}
    if prev is not None:
        in_specs.append(ANY)
        args.append(prev)
        aliases = {2: 0}
    return pl.pallas_call(
        body,
        name="chip_sum",
        grid_spec=pltpu.PrefetchScalarGridSpec(
            num_scalar_prefetch=1,
            grid=(R // tr,),
            in_specs=in_specs,
            out_specs=pl.BlockSpec((None, None, tr, C), lambda r, core_ref: (l, core_ref[0], r, 0)),
        ),
        out_shape=jax.ShapeDtypeStruct((n_layers, 2, R, C), F32),
        input_output_aliases=aliases,
        compiler_params=pltpu.CompilerParams(dimension_semantics=("arbitrary",), vmem_limit_bytes=VMEM_LIMIT),
    )(*args)


def _sum_blocks(a, n):
    M = a.shape[0] // n
    C = a.shape[1]

    def body(a_ref, o_ref):
        s = a_ref[0:M]
        for j in range(1, n):
            s = s + a_ref[j * M:(j + 1) * M]
        o_ref[...] = s

    return pl.pallas_call(
        body,
        name="sum_blocks",
        out_shape=jax.ShapeDtypeStruct((M, C), F32),
        compiler_params=pltpu.CompilerParams(vmem_limit_bytes=VMEM_LIMIT),
    )(a)


def _adamw(w, g, m, v):
    R, C = w.shape
    tr = _row_tile(R, C, 7) if R % 8 == 0 else R

    def body(w_ref, g_ref, m_ref, v_ref, d_ref, nm_ref, nv_ref):
        g = g_ref[...]
        m = ADAM_B1 * m_ref[...] + (1.0 - ADAM_B1) * g
        v = ADAM_B2 * v_ref[...] + (1.0 - ADAM_B2) * (g * g)
        m_hat = m / (1.0 - ADAM_B1 ** ADAM_STEP)
        v_hat = v / (1.0 - ADAM_B2 ** ADAM_STEP)
        d_ref[...] = -ADAM_LR * (m_hat / (jnp.sqrt(v_hat) + ADAM_EPS) + ADAM_WD * w_ref[...])
        nm_ref[...] = m
        nv_ref[...] = v

    spec = pl.BlockSpec((tr, C), lambda i: (i, 0))
    return _call(body, "adamw", (R // tr,), [spec] * 4, [spec] * 3, [jax.ShapeDtypeStruct((R, C), F32)] * 3, [],
                 (w, g, m, v))[0]


def kernel(x, c, ada_w, ada_b, norm_ffn1_g, ffn1_w_gu, ffn1_w_down, norm_mix_g, mix_w_in, sgu_ln_g, sgu_ln_b, sgu_w_s, sgu_b, conv_w, out_norm_g, mix_w_out, norm_ffn2_g, ffn2_w_gu, ffn2_w_down, final_norm_g, loss_target, m_ada_w, m_ada_b, m_norm_ffn1_g, m_ffn1_w_gu, m_ffn1_w_down, m_norm_mix_g, m_mix_w_in, m_sgu_ln_g, m_sgu_ln_b, m_sgu_w_s, m_sgu_b, m_conv_w, m_out_norm_g, m_mix_w_out, m_norm_ffn2_g, m_ffn2_w_gu, m_ffn2_w_down, m_final_norm_g, v_ada_w, v_ada_b, v_norm_ffn1_g, v_ffn1_w_gu, v_ffn1_w_down, v_norm_mix_g, v_mix_w_in, v_sgu_ln_g, v_sgu_ln_b, v_sgu_w_s, v_sgu_b, v_conv_w, v_out_norm_g, v_mix_w_out, v_norm_ffn2_g, v_ffn2_w_gu, v_ffn2_w_down, v_final_norm_g):
    weights = dict(ada_w=ada_w, ada_b=ada_b, norm_ffn1_g=norm_ffn1_g, ffn1_w_gu=ffn1_w_gu, ffn1_w_down=ffn1_w_down,
                   norm_mix_g=norm_mix_g, mix_w_in=mix_w_in, sgu_ln_g=sgu_ln_g, sgu_ln_b=sgu_ln_b, sgu_w_s=sgu_w_s,
                   sgu_b=sgu_b, conv_w=conv_w, out_norm_g=out_norm_g, mix_w_out=mix_w_out, norm_ffn2_g=norm_ffn2_g,
                   ffn2_w_gu=ffn2_w_gu, ffn2_w_down=ffn2_w_down, final_norm_g=final_norm_g)
    m_in = dict(ada_w=m_ada_w, ada_b=m_ada_b, norm_ffn1_g=m_norm_ffn1_g, ffn1_w_gu=m_ffn1_w_gu,
                ffn1_w_down=m_ffn1_w_down, norm_mix_g=m_norm_mix_g, mix_w_in=m_mix_w_in, sgu_ln_g=m_sgu_ln_g,
                sgu_ln_b=m_sgu_ln_b, sgu_w_s=m_sgu_w_s, sgu_b=m_sgu_b, conv_w=m_conv_w, out_norm_g=m_out_norm_g,
                mix_w_out=m_mix_w_out, norm_ffn2_g=m_norm_ffn2_g, ffn2_w_gu=m_ffn2_w_gu, ffn2_w_down=m_ffn2_w_down,
                final_norm_g=m_final_norm_g)
    v_in = dict(ada_w=v_ada_w, ada_b=v_ada_b, norm_ffn1_g=v_norm_ffn1_g, ffn1_w_gu=v_ffn1_w_gu,
                ffn1_w_down=v_ffn1_w_down, norm_mix_g=v_norm_mix_g, mix_w_in=v_mix_w_in, sgu_ln_g=v_sgu_ln_g,
                sgu_ln_b=v_sgu_ln_b, sgu_w_s=v_sgu_w_s, sgu_b=v_sgu_b, conv_w=v_conv_w, out_norm_g=v_out_norm_g,
                mix_w_out=v_mix_w_out, norm_ffn2_g=v_norm_ffn2_g, ffn2_w_gu=v_ffn2_w_gu, ffn2_w_down=v_ffn2_w_down,
                final_norm_g=v_final_norm_g)

    B, S, D = x.shape
    T = B * S
    L = ada_w.shape[0]
    F = ffn1_w_down.shape[1] * N_CHIP
    P = mix_w_in.shape[2] * N_CHIP
    DA = D // 2
    DB = D - DA
    HD = DA // N_HEADS
    SA = ada_w.shape[2]
    n_all = B * N_DEV
    mx, my, mc = _position()
    chip = 2 * mx + my
    dev = 2 * chip + mc
    core = jnp.reshape(mc, (1,)).astype(jnp.int32)

    c_all = _all_gather(c.reshape(8, B * D // 8), "gather_c").reshape(n_all, D)
    ada_b_mine = lax.dynamic_slice_in_dim(ada_b, chip * SA, SA, axis=1).reshape(L, 1, SA)
    c_act, ada_part = _ada_fwd(c_all, ada_w, ada_b_mine)
    ada_all = _all_gather(ada_part.reshape(L * n_all, SA), "gather_ada").reshape(N_CHIP, 2, L, n_all, SA)[:, 0]
    ada_all = jnp.transpose(ada_all, (1, 2, 0, 3)).reshape(L, n_all, N_CHIP * SA)
    ada = lax.dynamic_slice_in_dim(ada_all, dev * B, B, axis=1).reshape(L, B, N_MOD, 1, D)
    mods = [[ada[l, :, j] for j in range(N_MOD)] for l in range(L)]

    cw_block = jnp.pad(conv_w.reshape(L * conv_w.shape[1], conv_w.shape[2]), ((0, 8 - L * conv_w.shape[1]), (0, 0)))
    cw_all = _all_gather(cw_block, "gather_conv_w").reshape(N_CHIP, 2, 8, conv_w.shape[2])[:, 0, :L * conv_w.shape[1]]
    conv_full = jnp.transpose(cw_all.reshape(N_CHIP, L, conv_w.shape[1], conv_w.shape[2]), (1, 2, 0, 3))
    conv_full = conv_full.reshape(L, conv_w.shape[1], DB)

    big = ["ffn1_w_gu", "ffn1_w_down", "mix_w_in", "mix_w_out", "ffn2_w_gu", "ffn2_w_down"]
    col_sharded = dict(ffn1_w_gu=True, ffn1_w_down=False, mix_w_in=True, mix_w_out=False,
                       ffn2_w_gu=True, ffn2_w_down=False)
    shards = {k: weights[k].astype(BF16) for k in big}
    gather = lambda l, *names: _gather_comm([(shards[k], l, col_sharded[k]) for k in names])
    full = [dict() for _ in range(L)]

    def arrived(l, names, res):
        full[l].update(zip(names, res))

    x0 = x.reshape(T, D)
    gains = lambda name, l: weights[name][l].reshape(1, D)
    hmask = jnp.repeat(jnp.eye(N_HEADS, dtype=F32), HD, axis=0)
    pmat = (jnp.repeat(hmask, HD, axis=1) / HD).astype(BF16)

    def mix_consts(l):
        lng = jnp.tile(sgu_ln_g[l], N_HEADS).reshape(1, DA)
        lnb = jnp.tile(sgu_ln_b[l], N_HEADS).reshape(1, DA)
        wst = sgu_w_s[l].reshape(N_HEADS * CHUNK, CHUNK)
        wstt = jnp.swapaxes(sgu_w_s[l], 1, 2).reshape(N_HEADS * CHUNK, CHUNK)
        bias = jnp.repeat(jnp.transpose(sgu_b[l]), HD, axis=1)
        return lng, lnb, wst, wstt, bias

    arrived(0, big[:2], _comm_call(gather(0, *big[:2]), "gather_first"))
    saved = []
    xc = x0
    for l in range(L):
        sh1, sc1, g1, sh2, sc2, g2, sh3, sc3, g3 = mods[l]
        lng, lnb, wst, wstt, bias = mix_consts(l)
        w = full[l]
        nxt = l + 1 < L
        if l == 0:
            (gu1, a1), got = _ffn_up(xc, gains("norm_ffn1_g", l), sh1, sc1, w["ffn1_w_gu"],
                                     gather(l, "mix_w_in", "mix_w_out", "ffn2_w_gu"))
            arrived(l, ["mix_w_in", "mix_w_out", "ffn2_w_gu"], got)
            (xa, f1), got = _ffn_down(a1, xc, g1, w["ffn1_w_down"], gather(l, "ffn2_w_down"))
            arrived(l, ["ffn2_w_down"], got)
        else:
            (gu1, a1), got = _ffn_up(xc, gains("norm_ffn1_g", l), sh1, sc1, w["ffn1_w_gu"], gather(l, "ffn2_w_gu"))
            arrived(l, ["ffn2_w_gu"], got)
            (xa, f1), _ = _ffn_down(a1, xc, g1, w["ffn1_w_down"])
        (proj, h2), got = _mixin_fwd(xa, gains("norm_mix_g", l), sh2, sc2, w["mix_w_in"],
                                     gather(l + 1, "mix_w_in") if nxt else None)
        if nxt:
            arrived(l + 1, ["mix_w_in"], got)
        (xb, yn), got = _mix_core_fwd(proj, xa, g2, w["mix_w_out"], lng, lnb, wst, bias, pmat, conv_full[l],
                                      gains("out_norm_g", l),
                                      gather(l + 1, "ffn1_w_down", "mix_w_out") if nxt else None)
        if nxt:
            arrived(l + 1, ["ffn1_w_down", "mix_w_out"], got)
        (gu2, a2), got = _ffn_up(xb, gains("norm_ffn2_g", l), sh3, sc3, w["ffn2_w_gu"],
                                 gather(l + 1, "ffn1_w_gu") if nxt else None)
        if nxt:
            arrived(l + 1, ["ffn1_w_gu"], got)
        (xd, f2), got = _ffn_down(a2, xb, g3, w["ffn2_w_down"], gather(l + 1, "ffn2_w_down") if nxt else None)
        if nxt:
            arrived(l + 1, ["ffn2_w_down"], got)
        saved.append(dict(x0=xc, xa=xa, xb=xb, gu1=gu1, a1=a1, f1=f1, proj=proj, h2=h2, yn=yn,
                          gu2=gu2, a2=a2, f2=f2))
        xc = xd

    dx, loss_block, d_final = _loss_head(xc, loss_target.reshape(T, D), final_norm_g.reshape(1, D))
    loss = lax.psum(loss_block[0, 0], ("x", "y", "c"))

    reduced = dict.fromkeys(big)

    def halves(name, g):
        if g.ndim == 4:
            return g
        return g.reshape(N_CHIP, 2, weights[name].shape[1] // 2, g.shape[-1])

    class Reduction:
        def __init__(self, l, names, grads):
            self.l, self.names, self.gs = l, names, [halves(k, g) for k, g in zip(names, grads)]
            self.stage = 0

        def step(self):
            self.stage += 1
            if self.stage == 1:
                return _sibling_half_comm(self.gs)
            if self.stage == 2:
                ps = [_pair_sum(g, r, core) for g, r in zip(self.gs, self.got)]
                return _scatter_comm(ps)
            if self.stage == 3:
                for k, q in zip(self.names, self.got):
                    reduced[k] = _chip_sum(q, core, self.l, L, reduced[k])
                return _share_comm([reduced[k] for k in self.names], self.l)
            for k, r in zip(self.names, self.got):
                reduced[k] = r
            return None

        def done(self, got):
            self.got = got

    def ride(red, fn, *args):
        comm = red.step() if red is not None else None
        res, got = fn(*args, comm=comm)
        if comm is not None:
            red.done(got)
        return res

    def ride2(red_a, red_b, fn, *args):
        ca = red_a.step() if red_a is not None else None
        cb = red_b.step() if red_b is not None else None
        res, got = fn(*args, comm=_merge(ca, cb))
        na = len(ca.out_shape) if ca is not None else 0
        if ca is not None:
            red_a.done(got[:na])
        if cb is not None:
            red_b.done(got[na:])
        return res

    small = [None] * L
    d_ada = [None] * L
    pending = None
    for l in reversed(range(L)):
        sh1, sc1, g1, sh2, sc2, g2, sh3, sc3, g3 = mods[l]
        lng, lnb, wst, wstt, bias = mix_consts(l)
        s = saved[l]
        w = full[l]
        dgu, = ride(pending, _ffn_bwd_down, dx, s["gu2"], g3, w["ffn2_w_down"])
        dx, h3, df, dsc3, dsh3, dgain3, dg3 = ride(pending, _ffn_bwd_up, dx, s["xb"], dgu, s["f2"],
                                                   gains("norm_ffn2_g", l), sh3, sc3, g3, w["ffn2_w_gu"])
        g_gu2, = ride(pending, _wgrad, h3, dgu, D, 2 * F // N_CHIP, True, "wgrad_gu")
        if pending is not None:
            pending.step()
        g_d2, = _wgrad(s["a2"], df[None], F // 2, D, False, "wgrad_down")[0]
        red_ffn2 = Reduction(l, ["ffn2_w_gu", "ffn2_w_down"], [g_gu2, g_d2])
        dproj, d_o, dg2, dog, dwst, dbias, dlng, dlnb, dconvw = ride(
            red_ffn2, _mix_core_bwd, s["proj"], dx, g2, w["mix_w_out"], lng, lnb, wst, wstt, bias, pmat, conv_full[l],
            gains("out_norm_g", l))
        g_out, = _wgrad(s["yn"], d_o[None], D, D, False, "wgrad_out")[0]
        dx, dsc2, dsh2, dgain2 = _mixin_bwd(dx, s["xa"], dproj, gains("norm_mix_g", l), sc2, w["mix_w_in"])[0]
        g_in, = _wgrad(s["h2"], dproj[None], D, P // N_CHIP, True, "wgrad_in")[0]
        red_mix = Reduction(l, ["mix_w_in", "mix_w_out"], [g_in, g_out])
        dgu, = ride2(red_ffn2, red_mix, _ffn_bwd_down, dx, s["gu1"], g1, w["ffn1_w_down"])
        dx, h1, df, dsc1, dsh1, dgain1, dg1 = ride2(red_ffn2, red_mix, _ffn_bwd_up, dx, s["x0"], dgu, s["f1"],
                                                    gains("norm_ffn1_g", l), sh1, sc1, g1, w["ffn1_w_gu"])
        red_ffn2.step()
        g_gu1, = ride(red_mix, _wgrad, h1, dgu, D, 2 * F // N_CHIP, True, "wgrad_gu")
        red_mix.step()
        g_d1, = _wgrad(s["a1"], df[None], F // 2, D, False, "wgrad_down")[0]
        pending = Reduction(l, ["ffn1_w_gu", "ffn1_w_down"], [g_gu1, g_d1])
        d_ada[l] = jnp.concatenate([dsh1, dsc1, dg1, dsh2, dsc2, dg2, dsh3, dsc3, dg3], axis=1).reshape(B, N_MOD * D)
        small[l] = [dgain1, dgain2, dgain3, dog, dlng, dlnb, dwst, dbias[:, ::HD], dconvw]
    for stage in ("reduce_pair", "reduce_chips", "reduce_share"):
        pending.done(_comm_call(pending.step(), stage))
    pending.step()
    grad_x = dx.reshape(B, S, D)
    grads = {k: reduced[k].reshape(weights[k].shape) for k in big}

    d_ada_all = _all_gather(jnp.stack(d_ada).reshape(L * B, N_MOD * D), "gather_d_ada")
    d_ada_all = jnp.transpose(d_ada_all.reshape(N_DEV, L, B, N_MOD * D), (1, 0, 2, 3)).reshape(L, n_all, N_MOD * D)
    grads["ada_b"] = _colsum(d_ada_all).reshape(L, N_MOD * D)
    d_ada_mine = lax.dynamic_slice_in_dim(d_ada_all, chip * SA, SA, axis=2).astype(BF16)
    grads["ada_w"] = jnp.stack([_wgrad(c_act, d_ada_mine[l][None], D, _tile(768, SA), False, "wgrad_ada")[0][0]
                                for l in range(L)]).reshape(L, D, SA)

    flat = [a.reshape(-1, 128) for l in range(L) for a in small[l]] + [d_final.reshape(-1, 128)]
    n_rows = sum(a.shape[0] for a in flat)
    pad = (-n_rows) % 8
    packed = jnp.concatenate(flat + [jnp.zeros((pad, 128), F32)], axis=0)
    total = _sum_blocks(_all_gather(packed, "gather_small"), N_DEV)
    pieces, at = [], 0
    for a in flat:
        pieces.append(total[at:at + a.shape[0]])
        at += a.shape[0]
    per_layer = len(small[0])
    stack = lambda j, shape: jnp.stack([pieces[l * per_layer + j].reshape(shape) for l in range(L)])
    grads["norm_ffn1_g"] = stack(0, (D,))
    grads["norm_mix_g"] = stack(1, (D,))
    grads["norm_ffn2_g"] = stack(2, (D,))
    grads["out_norm_g"] = stack(3, (D,))
    grads["sgu_ln_g"] = stack(4, (N_HEADS, HD)).sum(axis=1)
    grads["sgu_ln_b"] = stack(5, (N_HEADS, HD)).sum(axis=1)
    grads["sgu_w_s"] = stack(6, (N_HEADS, CHUNK, CHUNK))
    grads["sgu_b"] = jnp.swapaxes(stack(7, (CHUNK, N_HEADS)), 1, 2)
    g_conv = stack(8, (8, DB))[:, :conv_w.shape[1]]
    grads["conv_w"] = lax.dynamic_slice_in_dim(g_conv, chip * conv_w.shape[2], conv_w.shape[2], axis=2)
    grads["final_norm_g"] = pieces[-1].reshape(D)

    names = list(weights)
    delta, new_m, new_v = {}, {}, {}
    for k in names:
        wk = weights[k]
        view = (1, wk.shape[0]) if wk.ndim == 1 else (-1, wk.shape[-1])
        d, nm, nv = _adamw(wk.reshape(view), grads[k].reshape(view), m_in[k].reshape(view), v_in[k].reshape(view))
        delta[k], new_m[k], new_v[k] = d.reshape(wk.shape), nm.reshape(wk.shape), nv.reshape(wk.shape)

    return (loss, grad_x, *[grads[k] for k in names], *[delta[k] for k in names],
            *[new_m[k] for k in names], *[new_v[k] for k in names])
```

```python
import functools
import math

import jax
import jax.numpy as jnp
from jax import lax
from jax.experimental import pallas as pl
from jax.experimental.pallas import tpu as pltpu

F32 = jnp.float32
BF16 = jnp.bfloat16
MESH = pl.DeviceIdType.MESH

N_HEADS = 8
CHUNK = 128
N_MOD = 9
EPS = 1e-6
N_DEV = 8
N_CHIP = 4

ADAM_LR = 0.001
ADAM_B1 = 0.9
ADAM_B2 = 0.999
ADAM_EPS = 1e-08
ADAM_WD = 0.01
ADAM_STEP = 10

TOKEN_TILE = 512
FF_TILE = 1408
MIX_TILE = 256
WGRAD_TOKENS = 2048
ELEMENTWISE_COLS = 512
VMEM_LIMIT = 56 * 1024 * 1024

SCATTER_BYTES_PER_US = 68_000
ADAMW_US_PER_ELEMENT = 1.0e-5
SCATTER_OVERSHOOT = 1.25

ANY = pl.BlockSpec(memory_space=pl.ANY)


def _tile(pref, n):
    t = min(pref, n)
    assert n % t == 0, (pref, n)
    return t


def _dot(a, b):
    return jnp.dot(a, b, preferred_element_type=F32)


def _dot_nt(a, b):
    return lax.dot_general(a, b, (((1,), (1,)), ((), ())), preferred_element_type=F32)


def _dot_tn(a, b):
    return lax.dot_general(a, b, (((0,), (0,)), ((), ())), preferred_element_type=F32)


def _sigmoid(x):
    return 1.0 / (1.0 + jnp.exp(-x))


def _sigmoid_fast(x):
    return pl.reciprocal(1.0 + jnp.exp(-x), approx=True)


def _rms(x):
    r = lax.rsqrt(jnp.mean(x * x, axis=-1, keepdims=True) + EPS)
    return x * r, r


def _norm_mod_bwd(x, dh, gain, sc):
    xh, r = _rms(x)
    dsc = jnp.sum(dh * (xh * gain), axis=0, keepdims=True)
    dsh = jnp.sum(dh, axis=0, keepdims=True)
    dn = dh * (1.0 + sc)
    dgain = jnp.sum(dn * xh, axis=0, keepdims=True)
    dy = dn * gain
    dx = r * (dy - xh * jnp.mean(dy * xh, axis=-1, keepdims=True))
    return dx, dsc, dsh, dgain


def _acc(ref, first, val):
    @pl.when(first)
    def _():
        ref[...] = val

    @pl.when(jnp.logical_not(first))
    def _():
        ref[...] += val


class _Comm:
    def __init__(self, args, out_shape, scratch, phases, aliases=None):
        self.args, self.out_shape, self.scratch = list(args), list(out_shape), list(scratch)
        self.phases, self.aliases = phases, dict(aliases or {})


def _merge(*comms):
    comms = [c for c in comms if c is not None]
    if len(comms) <= 1:
        return comms[0] if comms else None
    args = [a for c in comms for a in c.args]
    out_shape = [o for c in comms for o in c.out_shape]
    scratch = [s for c in comms for s in c.scratch]
    aliases, ai, oi = {}, 0, 0
    for c in comms:
        aliases.update({ai + i: oi + o for i, o in c.aliases.items()})
        ai += len(c.args)
        oi += len(c.out_shape)

    def phases(ins, outs, sems):
        parts, ai, oi, si = [], 0, 0, 0
        for c in comms:
            parts.append(c.phases(ins[ai:ai + len(c.args)], outs[oi:oi + len(c.out_shape)], sems[si:si + len(c.scratch)]))
            ai, oi, si = ai + len(c.args), oi + len(c.out_shape), si + len(c.scratch)

        def run(k):
            def go():
                for p in parts:
                    if p[k] is not None:
                        p[k]()
            return go
        return run(0), run(1), run(2)

    return _Comm(args, out_shape, scratch, phases, aliases)


def _call(body, name, grid, in_specs, out_specs, out_shape, scratch, args, comm=None):
    n_in, n_out, n_scr = len(in_specs), len(out_specs), len(scratch)
    sem = ("arbitrary",) * len(grid)
    params = pltpu.CompilerParams(dimension_semantics=sem, vmem_limit_bytes=VMEM_LIMIT)
    if comm is None:
        res = pl.pallas_call(body, name=name, grid=grid, in_specs=in_specs, out_specs=out_specs, out_shape=out_shape,
                             scratch_shapes=scratch, compiler_params=params)(*args)
        return list(res), []
    m_in, m_out = len(comm.args), len(comm.out_shape)

    def full(*refs):
        c_in, c_min = refs[:n_in], refs[n_in:n_in + m_in]
        o = n_in + m_in
        c_out, c_mout = refs[o:o + n_out], refs[o + n_out:o + n_out + m_out]
        o += n_out + m_out
        c_scr, c_sem = refs[o:o + n_scr], refs[o + n_scr:]
        start, mid, finish = comm.phases(c_min, c_mout, c_sem)
        ids = [pl.program_id(a) for a in range(len(grid))]
        first = functools.reduce(jnp.logical_and, [i == 0 for i in ids])
        last = functools.reduce(jnp.logical_and, [i == g - 1 for i, g in zip(ids, grid)])
        pl.when(first)(start)
        if mid is not None:
            pl.when(last)(mid)
        body(*c_in, *c_out, *c_scr)
        pl.when(last)(finish)

    res = pl.pallas_call(
        full, name=name, grid=grid,
        in_specs=list(in_specs) + [ANY] * m_in,
        out_specs=list(out_specs) + [ANY] * m_out,
        out_shape=list(out_shape) + comm.out_shape,
        scratch_shapes=list(scratch) + comm.scratch,
        input_output_aliases={n_in + i: n_out + o for i, o in comm.aliases.items()},
        compiler_params=params,
    )(*args, *comm.args)
    return list(res[:n_out]), list(res[n_out:])


def _comm_call(comm, name):
    m_in, m_out = len(comm.args), len(comm.out_shape)

    def body(*refs):
        start, mid, finish = comm.phases(refs[:m_in], refs[m_in:m_in + m_out], refs[m_in + m_out:])
        start()
        if mid is not None:
            mid()
        finish()

    res = pl.pallas_call(
        body, name=name, in_specs=[ANY] * m_in, out_specs=[ANY] * m_out, out_shape=comm.out_shape,
        scratch_shapes=comm.scratch, input_output_aliases=comm.aliases,
    )(*comm.args)
    return list(res)


def _position():
    return lax.axis_index("x"), lax.axis_index("y"), lax.axis_index("c")


def _gather_comm(items):
    n = len(items)
    half = [s.shape[1] // 2 for s, _, _ in items]

    def full_shape(i):
        s, _, col = items[i]
        _, R, C = s.shape
        return jax.ShapeDtypeStruct((R, N_CHIP * C) if col else (N_CHIP * R, C), s.dtype)

    def phases(ins, outs, sems):
        send_sems, recv_sems, local_sems = sems
        x, y, c = _position()

        def region(i, chip, h):
            s, _, col = items[i]
            _, R, C = s.shape
            if col:
                return outs[i].at[pl.ds(h * half[i], half[i]), pl.ds(chip * C, C)]
            return outs[i].at[pl.ds(chip * R + h * half[i], half[i]), :]

        def mine(i, h):
            return ins[i].at[items[i][1], pl.ds(h * half[i], half[i]), :]

        def copies(kx, ky, kc):
            k_me = 2 * kx + ky
            sibling = (kx, ky, 1 - kc)
            chips = [(1 - kx, ky), (kx, 1 - ky), (1 - kx, 1 - ky)]
            local, first, passed, arrive_ici, arrive_d2d = [], [], [], [], []

            def remote(src, dst, s, to):
                return pltpu.make_async_remote_copy(src_ref=src, dst_ref=dst, send_sem=send_sems.at[s],
                                                    recv_sem=recv_sems.at[s], device_id=to, device_id_type=MESH)

            for i in range(n):
                for h in range(2):
                    local.append(pltpu.make_async_copy(mine(i, h), region(i, k_me, h), local_sems.at[2 * i + h]))
                for j, (px, py) in enumerate(chips):
                    s = 6 * i + j
                    first.append(remote(mine(i, kc), region(i, k_me, kc), s, (px, py, kc)))
                    got = region(i, 2 * px + py, kc)
                    arrive_ici.append(remote(got, got, s, (px, py, kc)))
                    passed.append(remote(got, got, s + 3, sibling))
                    other = region(i, 2 * px + py, 1 - kc)
                    arrive_d2d.append(remote(other, other, s + 3, sibling))
            return local, first, passed, arrive_ici, arrive_d2d

        def on_each_device(fn):
            def go():
                for kx in range(2):
                    for ky in range(2):
                        for kc in range(2):
                            pl.when((x == kx) & (y == ky) & (c == kc))(functools.partial(fn, *copies(kx, ky, kc)))
            return go

        def start(local, first, passed, arrive_ici, arrive_d2d):
            for cp in local + first:
                cp.start()

        def mid(local, first, passed, arrive_ici, arrive_d2d):
            for a, p in zip(arrive_ici, passed):
                a.wait_recv()
                p.start()

        def finish(local, first, passed, arrive_ici, arrive_d2d):
            for a in arrive_d2d:
                a.wait_recv()
            for cp in first + passed:
                cp.wait_send()
            for cp in local:
                cp.wait()

        return on_each_device(start), on_each_device(mid), on_each_device(finish)

    scratch = [pltpu.SemaphoreType.DMA((6 * n,)), pltpu.SemaphoreType.DMA((6 * n,)), pltpu.SemaphoreType.DMA((2 * n,))]
    return _Comm([s for s, _, _ in items], [full_shape(i) for i in range(n)], scratch, phases)


def _sibling_half_comm(gs):
    n = len(gs)

    def phases(ins, outs, sems):
        send_sems, recv_sems = sems
        x, y, c = _position()

        def copies():
            return [pltpu.make_async_remote_copy(
                src_ref=ins[i].at[:, 1 - c], dst_ref=outs[i], send_sem=send_sems.at[i], recv_sem=recv_sems.at[i],
                device_id=(x, y, 1 - c), device_id_type=MESH) for i in range(n)]

        def start():
            for cp in copies():
                cp.start()

        def finish():
            for cp in copies():
                cp.wait()

        return start, None, finish

    out_shape = [jax.ShapeDtypeStruct(g.shape[:1] + g.shape[2:], g.dtype) for g in gs]
    return _Comm(gs, out_shape, [pltpu.SemaphoreType.DMA((n,)), pltpu.SemaphoreType.DMA((n,))], phases)


def _scatter_comm(ps):
    n = len(ps)

    def phases(ins, outs, sems):
        send_sems, recv_sems, local_sems = sems
        x, y, c = _position()
        k_me = 2 * x + y
        chips = [(1 - x, y), (x, 1 - y), (1 - x, 1 - y)]

        def copies():
            local = [pltpu.make_async_copy(ins[i].at[k_me], outs[i].at[k_me], local_sems.at[i]) for i in range(n)]
            remote = [pltpu.make_async_remote_copy(
                src_ref=ins[i].at[2 * px + py], dst_ref=outs[i].at[k_me],
                send_sem=send_sems.at[3 * i + j], recv_sem=recv_sems.at[3 * i + j],
                device_id=(px, py, c), device_id_type=MESH) for i in range(n) for j, (px, py) in enumerate(chips)]
            return local, remote

        def start():
            local, remote = copies()
            for cp in local + remote:
                cp.start()

        def finish():
            local, remote = copies()
            for cp in remote + local:
                cp.wait()

        return start, None, finish

    scratch = [pltpu.SemaphoreType.DMA((3 * n,)), pltpu.SemaphoreType.DMA((3 * n,)), pltpu.SemaphoreType.DMA((n,))]
    return _Comm(ps, [jax.ShapeDtypeStruct(p.shape, p.dtype) for p in ps], scratch, phases)


def _share_comm(rs, l):
    n = len(rs)

    def phases(ins, outs, sems):
        send_sems, recv_sems = sems
        x, y, c = _position()

        def copy(i, h):
            return pltpu.make_async_remote_copy(
                src_ref=outs[i].at[l, h], dst_ref=outs[i].at[l, h], send_sem=send_sems.at[i], recv_sem=recv_sems.at[i],
                device_id=(x, y, 1 - c), device_id_type=MESH)

        def start():
            for i in range(n):
                copy(i, c).start()

        def finish():
            for i in range(n):
                copy(i, 1 - c).wait_recv()
            for i in range(n):
                copy(i, c).wait_send()

        return start, None, finish

    return _Comm(rs, [jax.ShapeDtypeStruct(r.shape, r.dtype) for r in rs],
                 [pltpu.SemaphoreType.DMA((n,)), pltpu.SemaphoreType.DMA((n,))], phases,
                 aliases={i: i for i in range(n)})


def _all_gather_comm(block):
    def phases(ins, outs, sems):
        send_sems, recv_sems, local_sem = sems
        (src,), (out,) = ins, outs
        x, y, c = _position()
        sibling = (x, y, 1 - c)
        chips = [(1 - x, y), (x, 1 - y), (1 - x, 1 - y)]

        def slot(px, py, pc):
            return out.at[4 * px + 2 * py + pc]

        def copy(k, blk, to, own=False):
            return pltpu.make_async_remote_copy(
                src_ref=src if own else slot(*blk), dst_ref=slot(*blk),
                send_sem=send_sems.at[k], recv_sem=recv_sems.at[k], device_id=to, device_id_type=MESH)

        mine = lambda: pltpu.make_async_copy(src, slot(x, y, c), local_sem.at[0])
        first = lambda: [copy(0, (x, y, c), sibling, True)] + [
            copy(1 + j, (x, y, c), (*chip, c), True) for j, chip in enumerate(chips)]
        passed = lambda: [copy(4 + j, (*chip, c), sibling) for j, chip in enumerate(chips)]

        def start():
            mine().start()
            for cp in first():
                cp.start()

        def mid():
            for j, (chip, p) in enumerate(zip(chips, passed())):
                copy(1 + j, (*chip, c), (x, y, c)).wait_recv()
                p.start()

        def finish():
            copy(0, sibling, (x, y, c)).wait_recv()
            for j, chip in enumerate(chips):
                copy(4 + j, (*chip, 1 - c), (x, y, c)).wait_recv()
            for cp in first() + passed():
                cp.wait_send()
            mine().wait()

        return start, mid, finish

    scratch = [pltpu.SemaphoreType.DMA((7,)), pltpu.SemaphoreType.DMA((7,)), pltpu.SemaphoreType.DMA((1,))]
    return _Comm([block], [jax.ShapeDtypeStruct((N_DEV,) + block.shape, block.dtype)], scratch, phases)


def _ffn_up(x, gain, sh, sc, wgu, comm=None):
    T, D = x.shape
    F = wgu.shape[1] // 2
    B = sh.shape[0]
    tm = _tile(TOKEN_TILE, T // B)
    tf = _tile(FF_TILE, F)
    tps = (T // B) // tm
    nf = F // tf

    def body(x_ref, gain_ref, sh_ref, sc_ref, wg_ref, wu_ref, gu_ref, a_ref):
        xh, _ = _rms(x_ref[...])
        h = (xh * gain_ref[...] * (1.0 + sc_ref[0]) + sh_ref[0]).astype(BF16)
        g = _dot(h, wg_ref[...])
        u = _dot(h, wu_ref[...])
        gu_ref[0] = g.astype(BF16)
        gu_ref[1] = u.astype(BF16)
        a_ref[...] = (g * _sigmoid(g) * u).astype(BF16)

    seq = lambda k, i: (i // tps, 0, 0)
    return _call(
        body, "ffn_up", (nf, T // tm),
        [
            pl.BlockSpec((tm, D), lambda k, i: (i, 0)),
            pl.BlockSpec((1, D), lambda k, i: (0, 0)),
            pl.BlockSpec((1, 1, D), seq),
            pl.BlockSpec((1, 1, D), seq),
            pl.BlockSpec((D, tf), lambda k, i: (0, k)),
            pl.BlockSpec((D, tf), lambda k, i: (0, nf + k)),
        ],
        [
            pl.BlockSpec((2, tm, tf), lambda k, i: (0, i, k)),
            pl.BlockSpec((tm, tf), lambda k, i: (i, k)),
        ],
        [
            jax.ShapeDtypeStruct((2, T, F), BF16),
            jax.ShapeDtypeStruct((T, F), BF16),
        ],
        [],
        (x, gain, sh, sc, wgu, wgu), comm)


def _ffn_down(a, x, gate, wd, comm=None):
    T, F = a.shape
    D = x.shape[1]
    B = gate.shape[0]
    tm = _tile(TOKEN_TILE, T // B)
    tps = (T // B) // tm

    def body(a_ref, x_ref, gate_ref, wd_ref, xo_ref, f_ref):
        f = _dot(a_ref[...], wd_ref[...])
        f_ref[...] = f.astype(BF16)
        xo_ref[...] = x_ref[...] + 0.5 * gate_ref[0] * f

    return _call(
        body, "ffn_down", (T // tm,),
        [
            pl.BlockSpec((tm, F), lambda i: (i, 0)),
            pl.BlockSpec((tm, D), lambda i: (i, 0)),
            pl.BlockSpec((1, 1, D), lambda i: (i // tps, 0, 0)),
            pl.BlockSpec((F, D), lambda i: (0, 0)),
        ],
        [pl.BlockSpec((tm, D), lambda i: (i, 0)), pl.BlockSpec((tm, D), lambda i: (i, 0))],
        [jax.ShapeDtypeStruct((T, D), F32), jax.ShapeDtypeStruct((T, D), BF16)],
        [],
        (a, x, gate, wd), comm)


def _ffn_bwd_down(dxo, gu, gate, wd, comm=None):
    T, D = dxo.shape
    F = wd.shape[0]
    B = gate.shape[0]
    tm = _tile(TOKEN_TILE, T // B)
    tf = _tile(FF_TILE, F)
    tps = (T // B) // tm
    nf = F // tf
    cw = min(ELEMENTWISE_COLS, tf)

    def body(dxo_ref, gu_ref, gate_ref, wd_ref, dgu_ref):
        df = (0.5 * gate_ref[0] * dxo_ref[...]).astype(BF16)
        for c0 in range(0, tf, cw):
            w = min(cw, tf - c0)
            da = _dot_nt(df, wd_ref[c0:c0 + w, :])
            g = gu_ref[0, :, c0:c0 + w].astype(F32)
            sg = _sigmoid_fast(g)
            dgu_ref[1, :, c0:c0 + w] = (da * (g * sg)).astype(BF16)
            dgu_ref[0, :, c0:c0 + w] = (da * gu_ref[1, :, c0:c0 + w].astype(F32)
                                        * (sg * (1.0 + g * (1.0 - sg)))).astype(BF16)

    return _call(
        body, "ffn_bwd_down", (nf, T // tm),
        [
            pl.BlockSpec((tm, D), lambda k, i: (i, 0)),
            pl.BlockSpec((2, tm, tf), lambda k, i: (0, i, k)),
            pl.BlockSpec((1, 1, D), lambda k, i: (i // tps, 0, 0)),
            pl.BlockSpec((tf, D), lambda k, i: (k, 0)),
        ],
        [pl.BlockSpec((2, tm, tf), lambda k, i: (0, i, k))],
        [jax.ShapeDtypeStruct((2, T, F), BF16)],
        [],
        (dxo, gu, gate, wd), comm)


def _ffn_bwd_up(dxo, x, dgu, f, gain, sh, sc, gate, wgu, comm=None):
    T, D = x.shape
    F = wgu.shape[1] // 2
    B = sc.shape[0]
    tm = _tile(TOKEN_TILE, T // B)
    tps = (T // B) // tm

    def body(dxo_ref, x_ref, dgu_ref, f_ref, gain_ref, sh_ref, sc_ref, gate_ref, w_ref,
             dx_ref, h_ref, df_ref, dsc_ref, dsh_ref, dgain_ref, dgate_ref):
        i = pl.program_id(0)
        first_of_seq = (i % tps) == 0
        dh = _dot_nt(dgu_ref[0], w_ref[:, 0:F]) + _dot_nt(dgu_ref[1], w_ref[:, F:])
        dxo = dxo_ref[...]
        x = x_ref[...]
        gain = gain_ref[...]
        sc = sc_ref[0]
        dx, dsc, dsh, dgain = _norm_mod_bwd(x, dh, gain, sc)
        dx_ref[...] = dxo + dx
        h_ref[...] = (_rms(x)[0] * gain * (1.0 + sc) + sh_ref[0]).astype(BF16)
        df_ref[...] = (0.5 * gate_ref[0] * dxo).astype(BF16)
        _acc(dsc_ref.at[0], first_of_seq, dsc)
        _acc(dsh_ref.at[0], first_of_seq, dsh)
        _acc(dgain_ref, i == 0, dgain)
        _acc(dgate_ref.at[0], first_of_seq, 0.5 * jnp.sum(dxo * f_ref[...].astype(F32), axis=0, keepdims=True))

    seq = lambda i: (i // tps, 0, 0)
    row = lambda i: (i, 0)
    return _call(
        body, "ffn_bwd_up", (T // tm,),
        [
            pl.BlockSpec((tm, D), row),
            pl.BlockSpec((tm, D), row),
            pl.BlockSpec((2, tm, F), lambda i: (0, i, 0)),
            pl.BlockSpec((tm, D), row),
            pl.BlockSpec((1, D), lambda i: (0, 0)),
            pl.BlockSpec((1, 1, D), seq),
            pl.BlockSpec((1, 1, D), seq),
            pl.BlockSpec((1, 1, D), seq),
            pl.BlockSpec((D, 2 * F), lambda i: (0, 0), pipeline_mode=pl.Buffered(1)),
        ],
        [
            pl.BlockSpec((tm, D), row),
            pl.BlockSpec((tm, D), row),
            pl.BlockSpec((tm, D), row),
            pl.BlockSpec((1, 1, D), seq),
            pl.BlockSpec((1, 1, D), seq),
            pl.BlockSpec((1, D), lambda i: (0, 0)),
            pl.BlockSpec((1, 1, D), seq),
        ],
        [
            jax.ShapeDtypeStruct((T, D), F32),
            jax.ShapeDtypeStruct((T, D), BF16),
            jax.ShapeDtypeStruct((T, D), BF16),
            jax.ShapeDtypeStruct((B, 1, D), F32),
            jax.ShapeDtypeStruct((B, 1, D), F32),
            jax.ShapeDtypeStruct((1, D), F32),
            jax.ShapeDtypeStruct((B, 1, D), F32),
        ],
        [],
        (dxo, x, dgu, f, gain, sh, sc, gate, wgu), comm)


def _wgrad(a, b, tmm, tn, col_major, name, tokens=WGRAD_TOKENS, comm=None):
    T, M = a.shape
    nb, _, Nb = b.shape
    N = nb * Nb
    tk = _tile(tokens, T)
    npb = Nb // tn
    assert M % tmm == 0 and Nb % tn == 0
    if col_major:
        assert tmm == M
        shape = (N // tn, 2, M // 2, tn)
        out_spec = pl.BlockSpec((None, 2, M // 2, tn), lambda i, j, t: (j, 0, 0, 0))
    else:
        shape = (M // tmm, tmm, N)
        out_spec = pl.BlockSpec((None, tmm, tn), lambda i, j, t: (i, 0, j))

    def body(a_ref, b_ref, o_ref):
        t = pl.program_id(2)
        res = _dot_tn(a_ref[...], b_ref[...])
        if col_major:
            _acc(o_ref.at[0], t == 0, res[:M // 2])
            _acc(o_ref.at[1], t == 0, res[M // 2:])
        else:
            _acc(o_ref, t == 0, res)

    return _call(
        body, name, (M // tmm, N // tn, T // tk),
        [
            pl.BlockSpec((tk, tmm), lambda i, j, t: (t, i)),
            pl.BlockSpec((None, tk, tn), lambda i, j, t: (j // npb, t, j % npb)),
        ],
        [out_spec], [jax.ShapeDtypeStruct(shape, F32)], [],
        (a, b), comm)


def _mixin_fwd(x, gain, sh, sc, win, comm=None):
    T, D = x.shape
    P = win.shape[1]
    B = sh.shape[0]
    tm = _tile(TOKEN_TILE, T // B)
    tps = (T // B) // tm

    def body(x_ref, gain_ref, sh_ref, sc_ref, w_ref, proj_ref, h_ref):
        xh, _ = _rms(x_ref[...])
        h = (xh * gain_ref[...] * (1.0 + sc_ref[0]) + sh_ref[0]).astype(BF16)
        h_ref[...] = h
        proj_ref[...] = _dot(h, w_ref[...])

    seq = lambda i: (i // tps, 0, 0)
    return _call(
        body, "mixin_fwd", (T // tm,),
        [
            pl.BlockSpec((tm, D), lambda i: (i, 0)),
            pl.BlockSpec((1, D), lambda i: (0, 0)),
            pl.BlockSpec((1, 1, D), seq),
            pl.BlockSpec((1, 1, D), seq),
            pl.BlockSpec((D, P), lambda i: (0, 0)),
        ],
        [pl.BlockSpec((tm, P), lambda i: (i, 0)), pl.BlockSpec((tm, D), lambda i: (i, 0))],
        [jax.ShapeDtypeStruct((T, P), F32), jax.ShapeDtypeStruct((T, D), BF16)],
        [],
        (x, gain, sh, sc, win), comm)


def _mixin_bwd(dxo, x, dproj, gain, sc, win, comm=None):
    T, D = x.shape
    P = win.shape[1]
    B = sc.shape[0]
    tm = _tile(TOKEN_TILE, T // B)
    tps = (T // B) // tm

    def body(dxo_ref, x_ref, dp_ref, gain_ref, sc_ref, w_ref, dx_ref, dsc_ref, dsh_ref, dgain_ref):
        i = pl.program_id(0)
        first_of_seq = (i % tps) == 0
        dh = _dot_nt(dp_ref[...], w_ref[...])
        dx, dsc, dsh, dgain = _norm_mod_bwd(x_ref[...], dh, gain_ref[...], sc_ref[0])
        dx_ref[...] = dxo_ref[...] + dx
        _acc(dsc_ref.at[0], first_of_seq, dsc)
        _acc(dsh_ref.at[0], first_of_seq, dsh)
        _acc(dgain_ref, i == 0, dgain)

    seq = lambda i: (i // tps, 0, 0)
    row = lambda i: (i, 0)
    return _call(
        body, "mixin_bwd", (T // tm,),
        [
            pl.BlockSpec((tm, D), row),
            pl.BlockSpec((tm, D), row),
            pl.BlockSpec((tm, P), row),
            pl.BlockSpec((1, D), lambda i: (0, 0)),
            pl.BlockSpec((1, 1, D), seq),
            pl.BlockSpec((D, P), lambda i: (0, 0)),
        ],
        [
            pl.BlockSpec((tm, D), row),
            pl.BlockSpec((1, 1, D), seq),
            pl.BlockSpec((1, 1, D), seq),
            pl.BlockSpec((1, D), lambda i: (0, 0)),
        ],
        [
            jax.ShapeDtypeStruct((T, D), F32),
            jax.ShapeDtypeStruct((B, 1, D), F32),
            jax.ShapeDtypeStruct((B, 1, D), F32),
            jax.ShapeDtypeStruct((1, D), F32),
        ],
        [],
        (dxo, x, dproj, gain, sc, win), comm)


def _head_mean(z, pmat):
    hi = z.astype(BF16)
    lo = (z - hi.astype(F32)).astype(BF16)
    return _dot(hi, pmat) + _dot(lo, pmat)


def _gelu_parts(x):
    cdf = 0.5 * (1.0 + lax.erf(x * (1.0 / math.sqrt(2.0))))
    return x * cdf, cdf


def _gelu_grad(x, cdf):
    return cdf + x * jnp.exp(-0.5 * x * x) * (1.0 / math.sqrt(2.0 * math.pi))


def _head_masks(da):
    hd = da // N_HEADS
    col = lax.broadcasted_iota(jnp.int32, (1, da), 1)
    return [(col >= h * hd) & (col < (h + 1) * hd) for h in range(N_HEADS)]


def _select_heads(res, masks):
    out = res[0:CHUNK]
    for h in range(1, N_HEADS):
        out = jnp.where(masks[h], res[h * CHUNK:(h + 1) * CHUNK], out)
    return out


def _causal_stack(w, transposed):
    r = lax.broadcasted_iota(jnp.int32, w.shape, 0) % CHUNK
    c = lax.broadcasted_iota(jnp.int32, w.shape, 1)
    keep = (c >= r) if transposed else (c <= r)
    return jnp.where(keep, w, 0.0)


def _mix_core_forward(proj, zprev, prm, da, db):
    n = proj.shape[0]
    ua = proj[:, 0:da]
    va = proj[:, da:2 * da]
    bg = proj[:, 2 * da:2 * da + db]
    cg = proj[:, 2 * da + db:2 * da + 2 * db]
    xb = proj[:, 2 * da + 2 * db:]
    ug, ucdf = _gelu_parts(ua)
    vg, vcdf = _gelu_parts(va)
    zc = vg - _head_mean(vg, prm["pmat"])
    rs = lax.rsqrt(_head_mean(zc * zc, prm["pmat"]) + EPS)
    vhat = zc * rs
    vln = (vhat * prm["lng"] + prm["lnb"]).astype(BF16)
    wst = _causal_stack(prm["wst"], False).astype(BF16)
    masks = _head_masks(da)
    mixed = []
    for j in range(n // CHUNK):
        res = _dot(wst, vln[j * CHUNK:(j + 1) * CHUNK])
        mixed.append(_select_heads(res, masks) + prm["bias"])
    mixed = mixed[0] if len(mixed) == 1 else jnp.concatenate(mixed, axis=0)
    ya = ug * mixed
    z = cg * xb
    row = lax.broadcasted_iota(jnp.int32, z.shape, 0)
    z1 = jnp.where(row == 0, zprev[7:8], pltpu.roll(z, 1, 0))
    z2 = jnp.where(row == 0, zprev[6:7], jnp.where(row == 1, zprev[7:8], pltpu.roll(z, 2, 0)))
    cw = prm["convw"]
    conv = z2 * cw[0:1] + z1 * cw[1:2] + z * cw[2:3]
    yb = bg * conv
    yah, ra = _rms(ya)
    ybh, rb = _rms(yb)
    return dict(ua=ua, va=va, bg=bg, cg=cg, xb=xb, ug=ug, ucdf=ucdf, vcdf=vcdf, rs=rs, vhat=vhat, vln=vln,
                mixed=mixed, z=z, z1=z1, z2=z2, conv=conv, yah=yah, ra=ra, ybh=ybh, rb=rb, masks=masks)


def _mix_params(lng_ref, lnb_ref, wst_ref, bias_ref, pmat_ref, convw_ref):
    return dict(lng=lng_ref[...], lnb=lnb_ref[...], wst=wst_ref[...], bias=bias_ref[...],
                pmat=pmat_ref[...], convw=convw_ref[...])


def _mix_core_fwd(proj, x, gate, wout, lng, lnb, wst, bias, pmat, convw, og, comm=None):
    T, P = proj.shape
    D = x.shape[1]
    B = gate.shape[0]
    da = lng.shape[1]
    db = convw.shape[1]
    tm = _tile(MIX_TILE, T // B)
    tps = (T // B) // tm

    def body(proj_ref, x_ref, gate_ref, wout_ref, lng_ref, lnb_ref, wst_ref, bias_ref, pmat_ref, convw_ref,
             og_ref, xo_ref, yn_ref, halo):
        i = pl.program_id(0)

        @pl.when((i % tps) == 0)
        def _():
            halo[...] = jnp.zeros_like(halo)

        prm = _mix_params(lng_ref, lnb_ref, wst_ref, bias_ref, pmat_ref, convw_ref)
        r = _mix_core_forward(proj_ref[...], halo[...], prm, da, db)
        halo[...] = r["z"][tm - 8:tm]
        og = og_ref[...]
        yn_ref[:, 0:da] = (r["yah"] * og[:, 0:da]).astype(BF16)
        yn_ref[:, da:] = (r["ybh"] * og[:, da:]).astype(BF16)
        xo_ref[...] = x_ref[...] + gate_ref[0] * _dot(yn_ref[...], wout_ref[...])

    full = lambda a: pl.BlockSpec(a.shape, lambda i: (0,) * a.ndim)
    return _call(
        body, "mix_core_fwd", (T // tm,),
        [
            pl.BlockSpec((tm, P), lambda i: (i, 0)),
            pl.BlockSpec((tm, D), lambda i: (i, 0)),
            pl.BlockSpec((1, 1, D), lambda i: (i // tps, 0, 0)),
            full(wout), full(lng), full(lnb), full(wst), full(bias), full(pmat), full(convw), full(og),
        ],
        [pl.BlockSpec((tm, D), lambda i: (i, 0)), pl.BlockSpec((tm, D), lambda i: (i, 0))],
        [jax.ShapeDtypeStruct((T, D), F32), jax.ShapeDtypeStruct((T, D), BF16)],
        [pltpu.VMEM((8, db), F32)],
        (proj, x, gate, wout, lng, lnb, wst, bias, pmat, convw, og), comm)


def _mix_core_bwd(proj, dxo, gate, wout, lng, lnb, wst, wstt, bias, pmat, convw, og, comm=None):
    T, P = proj.shape
    D = dxo.shape[1]
    B = gate.shape[0]
    da = lng.shape[1]
    db = convw.shape[1]
    assert da == db and P == 2 * da + 3 * db
    tm = _tile(MIX_TILE, T // B)
    tps = (T // B) // tm
    nt = T // tm
    hd = da // N_HEADS

    def body(proj_ref, cgp_ref, xbp_ref, dxo_ref, gate_ref, wout_ref, lng_ref, lnb_ref, wst_ref, wstt_ref,
             bias_ref, pmat_ref, convw_ref, og_ref,
             dproj_ref, do_ref, dgate_ref, dog_ref, dwst_ref, dbias_ref, dlng_ref, dlnb_ref, dconvw_ref, carry):
        i = pl.program_id(0)
        ri = nt - 1 - i
        first = i == 0
        end_of_seq = (ri % tps) == tps - 1
        start_of_seq = (ri % tps) == 0

        @pl.when(end_of_seq)
        def _():
            carry[...] = jnp.zeros_like(carry)

        prm = _mix_params(lng_ref, lnb_ref, wst_ref, bias_ref, pmat_ref, convw_ref)
        zprev = jnp.where(start_of_seq, 0.0, cgp_ref[...] * xbp_ref[...])
        r = _mix_core_forward(proj_ref[...], zprev, prm, da, db)
        og = og_ref[...]
        pmat = prm["pmat"]

        yn = jnp.concatenate([(r["yah"] * og[:, 0:da]).astype(BF16), (r["ybh"] * og[:, da:]).astype(BF16)], axis=1)
        dxo = dxo_ref[...]
        o = _dot(yn, wout_ref[...])
        _acc(dgate_ref.at[0], end_of_seq, jnp.sum(dxo * o, axis=0, keepdims=True))
        d_o = (gate_ref[0] * dxo).astype(BF16)
        do_ref[...] = d_o
        dyn = _dot_nt(d_o, wout_ref[...])

        def rms_bwd(dyn_g, yh, rr, og_g):
            dog_g = jnp.sum(dyn_g * yh, axis=0, keepdims=True)
            dyh = dyn_g * og_g
            return rr * (dyh - yh * jnp.mean(dyh * yh, axis=-1, keepdims=True)), dog_g

        dya, dog_a = rms_bwd(dyn[:, 0:da], r["yah"], r["ra"], og[:, 0:da])
        dyb, dog_b = rms_bwd(dyn[:, da:], r["ybh"], r["rb"], og[:, da:])
        _acc(dog_ref, first, jnp.concatenate([dog_a, dog_b], axis=1))

        dug = dya * r["mixed"]
        dmixed = dya * r["ug"]
        wstt_b = _causal_stack(wstt_ref[...], True).astype(BF16)
        masks = r["masks"]
        dbias = jnp.zeros((CHUNK, da), F32)
        dwst = jnp.zeros((N_HEADS * CHUNK, CHUNK), F32)
        dvln = []
        for j in range(tm // CHUNK):
            dm = dmixed[j * CHUNK:(j + 1) * CHUNK]
            dbias = dbias + dm
            dmb = dm.astype(BF16)
            stack = jnp.concatenate([jnp.where(masks[h], dmb, jnp.zeros_like(dmb)) for h in range(N_HEADS)], axis=0)
            dwst = dwst + _dot_nt(stack, r["vln"][j * CHUNK:(j + 1) * CHUNK])
            dvln.append(_select_heads(_dot(wstt_b, dmb), masks))
        dvln = dvln[0] if len(dvln) == 1 else jnp.concatenate(dvln, axis=0)
        _acc(dbias_ref, first, dbias)
        _acc(dwst_ref, first, dwst)
        _acc(dlng_ref, first, jnp.sum(dvln * r["vhat"], axis=0, keepdims=True))
        _acc(dlnb_ref, first, jnp.sum(dvln, axis=0, keepdims=True))
        dvhat = dvln * prm["lng"]
        dvg = r["rs"] * (dvhat - _head_mean(dvhat, pmat) - r["vhat"] * _head_mean(dvhat * r["vhat"], pmat))
        dproj_ref[:, 0:da] = (dug * _gelu_grad(r["ua"], r["ucdf"])).astype(BF16)
        dproj_ref[:, da:2 * da] = (dvg * _gelu_grad(r["va"], r["vcdf"])).astype(BF16)

        dproj_ref[:, 2 * da:2 * da + db] = (dyb * r["conv"]).astype(BF16)
        dconv = dyb * r["bg"]
        dcw = jnp.concatenate([
            jnp.sum(dconv * r["z2"], axis=0, keepdims=True),
            jnp.sum(dconv * r["z1"], axis=0, keepdims=True),
            jnp.sum(dconv * r["z"], axis=0, keepdims=True),
            jnp.zeros((5, db), F32)], axis=0)
        _acc(dconvw_ref, first, dcw)
        nxt = carry[...]
        row = lax.broadcasted_iota(jnp.int32, dconv.shape, 0)
        dc1 = jnp.where(row == tm - 1, nxt[0:1], pltpu.roll(dconv, tm - 1, 0))
        dc2 = jnp.where(row == tm - 2, nxt[0:1], jnp.where(row == tm - 1, nxt[1:2], pltpu.roll(dconv, tm - 2, 0)))
        carry[...] = dconv[0:8]
        cw = prm["convw"]
        dz = dconv * cw[2:3] + dc1 * cw[1:2] + dc2 * cw[0:1]
        dproj_ref[:, 2 * da + db:2 * da + 2 * db] = (dz * r["xb"]).astype(BF16)
        dproj_ref[:, 2 * da + 2 * db:] = (dz * r["cg"]).astype(BF16)

        @pl.when(i == nt - 1)
        def _():
            dwst_ref[...] = _causal_stack(dwst_ref[...], False)
            dbias_ref[...] = _head_mean(dbias_ref[...], pmat) * float(hd)

    full = lambda a: pl.BlockSpec(a.shape, lambda i: (0,) * a.ndim)
    const = lambda i: (0, 0)
    rev = lambda i: (nt - 1 - i, 0)
    prev8 = lambda col: (lambda i: (jnp.maximum((nt - 1 - i) * (tm // 8) - 1, 0), col))
    return _call(
        body, "mix_core_bwd", (nt,),
        [
            pl.BlockSpec((tm, P), rev),
            pl.BlockSpec((8, db), prev8((2 * da + db) // db)),
            pl.BlockSpec((8, db), prev8((2 * da + 2 * db) // db)),
            pl.BlockSpec((tm, D), rev),
            pl.BlockSpec((1, 1, D), lambda i: ((nt - 1 - i) // tps, 0, 0)),
            full(wout), full(lng), full(lnb), full(wst), full(wstt), full(bias), full(pmat), full(convw), full(og),
        ],
        [
            pl.BlockSpec((tm, P), rev),
            pl.BlockSpec((tm, D), rev),
            pl.BlockSpec((1, 1, D), lambda i: ((nt - 1 - i) // tps, 0, 0)),
            pl.BlockSpec((1, D), const),
            pl.BlockSpec((N_HEADS * CHUNK, CHUNK), const),
            pl.BlockSpec((CHUNK, da), const),
            pl.BlockSpec((1, da), const),
            pl.BlockSpec((1, da), const),
            pl.BlockSpec((8, db), const),
        ],
        [
            jax.ShapeDtypeStruct((T, P), BF16),
            jax.ShapeDtypeStruct((T, D), BF16),
            jax.ShapeDtypeStruct((B, 1, D), F32),
            jax.ShapeDtypeStruct((1, D), F32),
            jax.ShapeDtypeStruct((N_HEADS * CHUNK, CHUNK), F32),
            jax.ShapeDtypeStruct((CHUNK, da), F32),
            jax.ShapeDtypeStruct((1, da), F32),
            jax.ShapeDtypeStruct((1, da), F32),
            jax.ShapeDtypeStruct((8, db), F32),
        ],
        [pltpu.VMEM((8, db), F32)],
        (proj, proj, proj, dxo, gate, wout, lng, lnb, wst, wstt, bias, pmat, convw, og), comm)


def _loss_head(x, target, gain):
    T, D = x.shape
    tm = _tile(TOKEN_TILE, T)

    def body(x_ref, t_ref, gain_ref, dx_ref, loss_ref, dgain_ref):
        first = pl.program_id(0) == 0
        xh, r = _rms(x_ref[...])
        gain = gain_ref[...]
        err = xh * gain - t_ref[...]
        _acc(loss_ref, first, jnp.zeros((8, 128), F32) + 0.5 * jnp.sum(err * err) / D)
        dout = err * (1.0 / D)
        _acc(dgain_ref, first, jnp.sum(dout * xh, axis=0, keepdims=True))
        dy = dout * gain
        dx_ref[...] = r * (dy - xh * jnp.mean(dy * xh, axis=-1, keepdims=True))

    return _call(
        body, "loss_head", (T // tm,),
        [
            pl.BlockSpec((tm, D), lambda i: (i, 0)),
            pl.BlockSpec((tm, D), lambda i: (i, 0)),
            pl.BlockSpec((1, D), lambda i: (0, 0)),
        ],
        [
            pl.BlockSpec((tm, D), lambda i: (i, 0)),
            pl.BlockSpec((8, 128), lambda i: (0, 0)),
            pl.BlockSpec((1, D), lambda i: (0, 0)),
        ],
        [
            jax.ShapeDtypeStruct((T, D), F32),
            jax.ShapeDtypeStruct((8, 128), F32),
            jax.ShapeDtypeStruct((1, D), F32),
        ],
        [],
        (x, target, gain))[0]


def _ada_fwd(c_all, ada_w, ada_b):
    n, D = c_all.shape
    L, _, sa = ada_w.shape
    tn = _tile(768, sa)

    def body(c_ref, w_ref, b_ref, act_ref, o_ref):
        c = c_ref[...]
        act = (c * _sigmoid(c)).astype(BF16)
        act_ref[...] = act
        o_ref[...] = _dot(act, w_ref[...].astype(BF16)) + b_ref[...]

    return _call(
        body, "ada_fwd", (L, sa // tn),
        [
            pl.BlockSpec((n, D), lambda l, j: (0, 0)),
            pl.BlockSpec((None, D, tn), lambda l, j: (l, 0, j)),
            pl.BlockSpec((None, 1, tn), lambda l, j: (l, 0, j)),
        ],
        [
            pl.BlockSpec((n, D), lambda l, j: (0, 0)),
            pl.BlockSpec((None, n, tn), lambda l, j: (l, 0, j)),
        ],
        [jax.ShapeDtypeStruct((n, D), BF16), jax.ShapeDtypeStruct((L, n, sa), F32)],
        [],
        (c_all, ada_w, ada_b))[0]


def _colsum(a):
    L, n, C = a.shape

    def body(a_ref, o_ref):
        o_ref[...] = jnp.sum(a_ref[...], axis=0, keepdims=True)

    return _call(
        body, "colsum", (L,),
        [pl.BlockSpec((None, n, C), lambda l: (l, 0, 0))],
        [pl.BlockSpec((None, 1, C), lambda l: (l, 0, 0))],
        [jax.ShapeDtypeStruct((L, 1, C), F32)],
        [],
        (a,))[0][0]


def _row_tile(rows, cols, nbuf):
    budget = VMEM_LIMIT // 3 // (2 * nbuf * 4 * cols)
    t = rows
    while t > max(budget, 8) and t % 2 == 0 and (t // 2) % 8 == 0:
        t //= 2
    return t


def _pair_sum(g, recv, core):
    n, _, R, C = g.shape
    tr = _row_tile(R, C, 3)

    def body(core_ref, g_ref, r_ref, o_ref):
        o_ref[...] = (g_ref[...] + r_ref[...]).astype(BF16)

    return pl.pallas_call(
        body,
        name="pair_sum",
        grid_spec=pltpu.PrefetchScalarGridSpec(
            num_scalar_prefetch=1,
            grid=(n, R // tr),
            in_specs=[
                pl.BlockSpec((None, None, tr, C), lambda i, r, core_ref: (i, core_ref[0], r, 0)),
                pl.BlockSpec((None, tr, C), lambda i, r, core_ref: (i, r, 0)),
            ],
            out_specs=pl.BlockSpec((None, tr, C), lambda i, r, core_ref: (i, r, 0)),
        ),
        out_shape=jax.ShapeDtypeStruct((n, R, C), BF16),
        compiler_params=pltpu.CompilerParams(dimension_semantics=("arbitrary", "arbitrary"),
                                             vmem_limit_bytes=VMEM_LIMIT),
    )(core, g, recv)


def _chip_sum(q, core, l, n_layers, prev):
    nq, R, C = q.shape
    tr = _row_tile(R, C, 4)

    def body(core_ref, q_ref, *rest):
        o_ref = rest[-1]
        s = q_ref[0].astype(F32)
        for j in range(1, nq):
            s = s + q_ref[j].astype(F32)
        o_ref[...] = s

    in_specs = [pl.BlockSpec((nq, tr, C), lambda r, core_ref: (0, r, 0))]
    args = [core, q]
    aliases = {}
    if prev is not None:
        in_specs.append(ANY)
        args.append(prev)
        aliases = {2: 0}
    return pl.pallas_call(
        body,
        name="chip_sum",
        grid_spec=pltpu.PrefetchScalarGridSpec(
            num_scalar_prefetch=1,
            grid=(R // tr,),
            in_specs=in_specs,
            out_specs=pl.BlockSpec((None, None, tr, C), lambda r, core_ref: (l, core_ref[0], r, 0)),
        ),
        out_shape=jax.ShapeDtypeStruct((n_layers, 2, R, C), F32),
        input_output_aliases=aliases,
        compiler_params=pltpu.CompilerParams(dimension_semantics=("arbitrary",), vmem_limit_bytes=VMEM_LIMIT),
    )(*args)


def _sum_blocks(a, n):
    M = a.shape[0] // n
    C = a.shape[1]

    def body(a_ref, o_ref):
        s = a_ref[0:M]
        for j in range(1, n):
            s = s + a_ref[j * M:(j + 1) * M]
        o_ref[...] = s

    return pl.pallas_call(
        body,
        name="sum_blocks",
        out_shape=jax.ShapeDtypeStruct((M, C), F32),
        compiler_params=pltpu.CompilerParams(vmem_limit_bytes=VMEM_LIMIT),
    )(a)


def _adamw(w, g, m, v, comm=None):
    R, C = w.shape
    tr = _row_tile(R, C, 7) if R % 8 == 0 else R

    def body(w_ref, g_ref, m_ref, v_ref, d_ref, nm_ref, nv_ref):
        g = g_ref[...]
        m = ADAM_B1 * m_ref[...] + (1.0 - ADAM_B1) * g
        v = ADAM_B2 * v_ref[...] + (1.0 - ADAM_B2) * (g * g)
        m_hat = m / (1.0 - ADAM_B1 ** ADAM_STEP)
        v_hat = v / (1.0 - ADAM_B2 ** ADAM_STEP)
        d_ref[...] = -ADAM_LR * (m_hat / (jnp.sqrt(v_hat) + ADAM_EPS) + ADAM_WD * w_ref[...])
        nm_ref[...] = m
        nv_ref[...] = v

    spec = pl.BlockSpec((tr, C), lambda i: (i, 0))
    return _call(body, "adamw", (R // tr,), [spec] * 4, [spec] * 3, [jax.ShapeDtypeStruct((R, C), F32)] * 3, [],
                 (w, g, m, v), comm)


def kernel(x, c, ada_w, ada_b, norm_ffn1_g, ffn1_w_gu, ffn1_w_down, norm_mix_g, mix_w_in, sgu_ln_g, sgu_ln_b, sgu_w_s, sgu_b, conv_w, out_norm_g, mix_w_out, norm_ffn2_g, ffn2_w_gu, ffn2_w_down, final_norm_g, loss_target, m_ada_w, m_ada_b, m_norm_ffn1_g, m_ffn1_w_gu, m_ffn1_w_down, m_norm_mix_g, m_mix_w_in, m_sgu_ln_g, m_sgu_ln_b, m_sgu_w_s, m_sgu_b, m_conv_w, m_out_norm_g, m_mix_w_out, m_norm_ffn2_g, m_ffn2_w_gu, m_ffn2_w_down, m_final_norm_g, v_ada_w, v_ada_b, v_norm_ffn1_g, v_ffn1_w_gu, v_ffn1_w_down, v_norm_mix_g, v_mix_w_in, v_sgu_ln_g, v_sgu_ln_b, v_sgu_w_s, v_sgu_b, v_conv_w, v_out_norm_g, v_mix_w_out, v_norm_ffn2_g, v_ffn2_w_gu, v_ffn2_w_down, v_final_norm_g):
    weights = dict(ada_w=ada_w, ada_b=ada_b, norm_ffn1_g=norm_ffn1_g, ffn1_w_gu=ffn1_w_gu, ffn1_w_down=ffn1_w_down,
                   norm_mix_g=norm_mix_g, mix_w_in=mix_w_in, sgu_ln_g=sgu_ln_g, sgu_ln_b=sgu_ln_b, sgu_w_s=sgu_w_s,
                   sgu_b=sgu_b, conv_w=conv_w, out_norm_g=out_norm_g, mix_w_out=mix_w_out, norm_ffn2_g=norm_ffn2_g,
                   ffn2_w_gu=ffn2_w_gu, ffn2_w_down=ffn2_w_down, final_norm_g=final_norm_g)
    m_in = dict(ada_w=m_ada_w, ada_b=m_ada_b, norm_ffn1_g=m_norm_ffn1_g, ffn1_w_gu=m_ffn1_w_gu,
                ffn1_w_down=m_ffn1_w_down, norm_mix_g=m_norm_mix_g, mix_w_in=m_mix_w_in, sgu_ln_g=m_sgu_ln_g,
                sgu_ln_b=m_sgu_ln_b, sgu_w_s=m_sgu_w_s, sgu_b=m_sgu_b, conv_w=m_conv_w, out_norm_g=m_out_norm_g,
                mix_w_out=m_mix_w_out, norm_ffn2_g=m_norm_ffn2_g, ffn2_w_gu=m_ffn2_w_gu, ffn2_w_down=m_ffn2_w_down,
                final_norm_g=m_final_norm_g)
    v_in = dict(ada_w=v_ada_w, ada_b=v_ada_b, norm_ffn1_g=v_norm_ffn1_g, ffn1_w_gu=v_ffn1_w_gu,
                ffn1_w_down=v_ffn1_w_down, norm_mix_g=v_norm_mix_g, mix_w_in=v_mix_w_in, sgu_ln_g=v_sgu_ln_g,
                sgu_ln_b=v_sgu_ln_b, sgu_w_s=v_sgu_w_s, sgu_b=v_sgu_b, conv_w=v_conv_w, out_norm_g=v_out_norm_g,
                mix_w_out=v_mix_w_out, norm_ffn2_g=v_norm_ffn2_g, ffn2_w_gu=v_ffn2_w_gu, ffn2_w_down=v_ffn2_w_down,
                final_norm_g=v_final_norm_g)

    B, S, D = x.shape
    T = B * S
    L = ada_w.shape[0]
    F = ffn1_w_down.shape[1] * N_CHIP
    P = mix_w_in.shape[2] * N_CHIP
    DA = D // 2
    DB = D - DA
    HD = DA // N_HEADS
    SA = ada_w.shape[2]
    n_all = B * N_DEV
    mx, my, mc = _position()
    chip = 2 * mx + my
    dev = 2 * chip + mc
    core = jnp.reshape(mc, (1,)).astype(jnp.int32)

    big = ["ffn1_w_gu", "ffn1_w_down", "mix_w_in", "mix_w_out", "ffn2_w_gu", "ffn2_w_down"]
    col_sharded = dict(ffn1_w_gu=True, ffn1_w_down=False, mix_w_in=True, mix_w_out=False,
                       ffn2_w_gu=True, ffn2_w_down=False)
    shards = {k: weights[k].astype(BF16) for k in big}
    gather = lambda l, *names: _gather_comm([(shards[k], l, col_sharded[k]) for k in names])
    full = [dict() for _ in range(L)]

    def arrived(l, names, res):
        full[l].update(zip(names, res))

    n_cw = L * conv_w.shape[1]
    cw_block = jnp.pad(conv_w.reshape(n_cw, conv_w.shape[2]), ((0, 8 - n_cw), (0, 0)))
    c_all, cw_all = _comm_call(_merge(_all_gather_comm(c.reshape(8, B * D // 8)), _all_gather_comm(cw_block)),
                               "gather_c")
    c_all = c_all.reshape(n_all, D)
    cw_all = cw_all.reshape(N_CHIP, 2, 8, conv_w.shape[2])[:, 0, :n_cw]
    conv_full = jnp.transpose(cw_all.reshape(N_CHIP, L, conv_w.shape[1], conv_w.shape[2]), (1, 2, 0, 3))
    conv_full = conv_full.reshape(L, conv_w.shape[1], DB)
    ada_b_mine = lax.dynamic_slice_in_dim(ada_b, chip * SA, SA, axis=1).reshape(L, 1, SA)
    c_act, ada_part = _ada_fwd(c_all, ada_w, ada_b_mine)
    ada_all, first_w = _comm_call(_merge(_all_gather_comm(ada_part.reshape(L * n_all, SA)), gather(0, big[0])),
                                  "gather_first")
    arrived(0, big[:1], [first_w])
    ada_all = ada_all.reshape(N_CHIP, 2, L, n_all, SA)[:, 0]
    ada_all = jnp.transpose(ada_all, (1, 2, 0, 3)).reshape(L, n_all, N_CHIP * SA)
    ada = lax.dynamic_slice_in_dim(ada_all, dev * B, B, axis=1).reshape(L, B, N_MOD, 1, D)
    mods = [[ada[l, :, j] for j in range(N_MOD)] for l in range(L)]

    x0 = x.reshape(T, D)
    gains = lambda name, l: weights[name][l].reshape(1, D)
    hmask = jnp.repeat(jnp.eye(N_HEADS, dtype=F32), HD, axis=0)
    pmat = (jnp.repeat(hmask, HD, axis=1) / HD).astype(BF16)

    def mix_consts(l):
        lng = jnp.tile(sgu_ln_g[l], N_HEADS).reshape(1, DA)
        lnb = jnp.tile(sgu_ln_b[l], N_HEADS).reshape(1, DA)
        wst = sgu_w_s[l].reshape(N_HEADS * CHUNK, CHUNK)
        wstt = jnp.swapaxes(sgu_w_s[l], 1, 2).reshape(N_HEADS * CHUNK, CHUNK)
        bias = jnp.repeat(jnp.transpose(sgu_b[l]), HD, axis=1)
        return lng, lnb, wst, wstt, bias

    def fetch(fn, *args, bring=()):
        bring = [(l, k) for l, k in bring if l < L]
        comm = _gather_comm([(shards[k], l, col_sharded[k]) for l, k in bring]) if bring else None
        res, got = fn(*args, comm)
        for (l, k), a in zip(bring, got):
            full[l][k] = a
        return res

    saved = []
    xc = x0
    for l in range(L):
        sh1, sc1, g1, sh2, sc2, g2, sh3, sc3, g3 = mods[l]
        lng, lnb, wst, wstt, bias = mix_consts(l)
        w = full[l]
        own = l == 0
        gu1, a1 = fetch(_ffn_up, xc, gains("norm_ffn1_g", l), sh1, sc1, w["ffn1_w_gu"],
                        bring=[(l, "ffn1_w_down"), (l, "mix_w_in"), (l, "mix_w_out")] if own else [(l, "ffn2_w_gu")])
        xa, f1 = fetch(_ffn_down, a1, xc, g1, w["ffn1_w_down"], bring=[(l, "ffn2_w_down")])
        proj, h2 = fetch(_mixin_fwd, xa, gains("norm_mix_g", l), sh2, sc2, w["mix_w_in"], bring=[(l + 1, "mix_w_in")])
        xb, yn = fetch(_mix_core_fwd, proj, xa, g2, w["mix_w_out"], lng, lnb, wst, bias, pmat, conv_full[l],
                       gains("out_norm_g", l), bring=[(l, "ffn2_w_gu")] if own else [])
        gu2, a2 = fetch(_ffn_up, xb, gains("norm_ffn2_g", l), sh3, sc3, w["ffn2_w_gu"],
                        bring=[(l + 1, "ffn1_w_gu"), (l + 1, "mix_w_out")])
        xd, f2 = fetch(_ffn_down, a2, xb, g3, w["ffn2_w_down"], bring=[(l + 1, "ffn1_w_down")])
        saved.append(dict(x0=xc, xa=xa, xb=xb, gu1=gu1, a1=a1, f1=f1, proj=proj, h2=h2, yn=yn,
                          gu2=gu2, a2=a2, f2=f2))
        xc = xd

    dx, loss_block, d_final = _loss_head(xc, loss_target.reshape(T, D), final_norm_g.reshape(1, D))
    loss = lax.psum(loss_block[0, 0], ("x", "y", "c"))

    reduced = dict.fromkeys(big)

    def halves(name, g):
        if g.ndim == 4:
            return g
        return g.reshape(N_CHIP, 2, weights[name].shape[1] // 2, g.shape[-1])

    class Reduction:
        def __init__(self, l, name, g):
            self.l, self.name, self.g, self.stage = l, name, halves(name, g), 0
            self.ici_bytes = 3 * (g.size // 8) * 2

        def step(self):
            self.stage += 1
            if self.stage == 1:
                return _sibling_half_comm([self.g])
            if self.stage == 2:
                return _scatter_comm([_pair_sum(self.g, self.got[0], core)])
            if self.stage == 3:
                reduced[self.name] = _chip_sum(self.got[0], core, self.l, L, reduced[self.name])
                return _share_comm([reduced[self.name]], self.l)
            reduced[self.name] = self.got[0]
            return None

    active, extra, gathered = [], [], {}

    def carry(fn, *args, us=None):
        left = None if us is None else us * SCATTER_BYTES_PER_US
        riders = []
        for r in active:
            if r.stage == 1 and left is not None:
                if r.ici_bytes > left * SCATTER_OVERSHOOT:
                    continue
                left -= r.ici_bytes
            riders.append(r)
        comms = [r.step() for r in riders] + [cm for cm, _ in extra]
        takers = [functools.partial(setattr, r, "got") for r in riders] + [cb for _, cb in extra]
        extra.clear()
        if fn is None:
            res, got = None, (_comm_call(_merge(*comms), "reduce_alone") if comms else [])
        else:
            res, got = fn(*args, comm=_merge(*comms))
        at = 0
        for cm, take in zip(comms, takers):
            take(got[at:at + len(cm.out_shape)])
            at += len(cm.out_shape)
        for r in riders:
            if r.stage == 3:
                r.step()
                active.remove(r)
        return res

    def reduce_later(l, name, g):
        active.append(Reduction(l, name, g))

    small = [None] * L
    d_ada = [None] * L
    for l in reversed(range(L)):
        sh1, sc1, g1, sh2, sc2, g2, sh3, sc3, g3 = mods[l]
        lng, lnb, wst, wstt, bias = mix_consts(l)
        s = saved[l]
        w = full[l]
        dgu, = carry(_ffn_bwd_down, dx, s["gu2"], g3, w["ffn2_w_down"], us=100)
        dx, h3, df, dsc3, dsh3, dgain3, dg3 = carry(_ffn_bwd_up, dx, s["xb"], dgu, s["f2"], gains("norm_ffn2_g", l),
                                                    sh3, sc3, g3, w["ffn2_w_gu"], us=120)
        reduce_later(l, "ffn2_w_gu", carry(_wgrad, h3, dgu, D, 2 * F // N_CHIP, True, "wgrad_gu", us=110)[0])
        reduce_later(l, "ffn2_w_down", carry(_wgrad, s["a2"], df[None], F // 2, D, False, "wgrad_down", us=50)[0])
        dproj, d_o, dg2, dog, dwst, dbias, dlng, dlnb, dconvw = carry(
            _mix_core_bwd, s["proj"], dx, g2, w["mix_w_out"], lng, lnb, wst, wstt, bias, pmat, conv_full[l],
            gains("out_norm_g", l), us=200)
        mix_grads = [
            lambda: reduce_later(l, "mix_w_out", carry(_wgrad, s["yn"], d_o[None], D, D, False, "wgrad_out", us=20)[0]),
            lambda: reduce_later(l, "mix_w_in", carry(_wgrad, s["h2"], dproj[None], D, P // N_CHIP, True, "wgrad_in",
                                                      us=65)[0])]
        last = l == 0
        if not last:
            mix_grads[0]()
        dx, dsc2, dsh2, dgain2 = carry(_mixin_bwd, dx, s["xa"], dproj, gains("norm_mix_g", l), sc2, w["mix_w_in"], us=60)
        if not last:
            mix_grads[1]()
        dgu, = carry(_ffn_bwd_down, dx, s["gu1"], g1, w["ffn1_w_down"], us=100)
        dx, h1, df, dsc1, dsh1, dgain1, dg1 = carry(_ffn_bwd_up, dx, s["x0"], dgu, s["f1"], gains("norm_ffn1_g", l),
                                                    sh1, sc1, g1, w["ffn1_w_gu"], us=120)
        d_ada[l] = jnp.concatenate([dsh1, dsc1, dg1, dsh2, dsc2, dg2, dsh3, dsc3, dg3], axis=1).reshape(B, N_MOD * D)
        small[l] = [dgain1, dgain2, dgain3, dog, dlng, dlnb, dwst, dbias[:, ::HD], dconvw]
        if last:
            flat = [a.reshape(-1, 128) for ll in range(L) for a in small[ll]] + [d_final.reshape(-1, 128)]
            pad = (-sum(a.shape[0] for a in flat)) % 8
            packed = jnp.concatenate(flat + [jnp.zeros((pad, 128), F32)], axis=0)
            extra.append((_all_gather_comm(jnp.stack(d_ada).reshape(L * B, N_MOD * D)),
                          lambda got: gathered.update(d_ada=got[0])))
            extra.append((_all_gather_comm(packed), lambda got: gathered.update(small=got[0])))
        reduce_later(l, "ffn1_w_gu", carry(_wgrad, h1, dgu, D, 2 * F // N_CHIP, True, "wgrad_gu", us=110)[0])
        reduce_later(l, "ffn1_w_down", carry(_wgrad, s["a1"], df[None], F // 2, D, False, "wgrad_down", us=50)[0])
        if last:
            mix_grads[1]()
            mix_grads[0]()
    grad_x = dx.reshape(B, S, D)

    def finished(name):
        while any(r.name == name for r in active):
            carry(None)
        return reduced[name].reshape(weights[name].shape)

    grads = {}
    d_ada_all = jnp.transpose(gathered["d_ada"].reshape(N_DEV, L, B, N_MOD * D), (1, 0, 2, 3))
    d_ada_all = d_ada_all.reshape(L, n_all, N_MOD * D)
    grads["ada_b"] = _colsum(d_ada_all).reshape(L, N_MOD * D)
    d_ada_mine = lax.dynamic_slice_in_dim(d_ada_all, chip * SA, SA, axis=2).astype(BF16)
    grads["ada_w"] = jnp.stack([carry(_wgrad, c_act, d_ada_mine[l][None], D, _tile(768, SA), False, "wgrad_ada", us=4)[0]
                                for l in range(L)]).reshape(L, D, SA)

    total = _sum_blocks(gathered["small"].reshape(-1, 128), N_DEV)
    pieces, at = [], 0
    for a in flat:
        pieces.append(total[at:at + a.shape[0]])
        at += a.shape[0]
    per_layer = len(small[0])
    stack = lambda j, shape: jnp.stack([pieces[l * per_layer + j].reshape(shape) for l in range(L)])
    grads["norm_ffn1_g"] = stack(0, (D,))
    grads["norm_mix_g"] = stack(1, (D,))
    grads["norm_ffn2_g"] = stack(2, (D,))
    grads["out_norm_g"] = stack(3, (D,))
    grads["sgu_ln_g"] = stack(4, (N_HEADS, HD)).sum(axis=1)
    grads["sgu_ln_b"] = stack(5, (N_HEADS, HD)).sum(axis=1)
    grads["sgu_w_s"] = stack(6, (N_HEADS, CHUNK, CHUNK))
    grads["sgu_b"] = jnp.swapaxes(stack(7, (CHUNK, N_HEADS)), 1, 2)
    g_conv = stack(8, (8, DB))[:, :conv_w.shape[1]]
    grads["conv_w"] = lax.dynamic_slice_in_dim(g_conv, chip * conv_w.shape[2], conv_w.shape[2], axis=2)
    grads["final_norm_g"] = pieces[-1].reshape(D)

    names = list(weights)
    delta, new_m, new_v = {}, {}, {}
    order = ["ada_w", "ffn2_w_gu", "ffn2_w_down"] + [k for k in names if k not in big and k != "ada_w"] + big[:4]
    for k in order:
        wk = weights[k]
        view = (1, wk.shape[0]) if wk.ndim == 1 else (-1, wk.shape[-1])
        if k in big:
            grads[k] = finished(k)
        d, nm, nv = carry(_adamw, wk.reshape(view), grads[k].reshape(view), m_in[k].reshape(view),
                          v_in[k].reshape(view), us=wk.size * ADAMW_US_PER_ELEMENT)
        delta[k], new_m[k], new_v[k] = d.reshape(wk.shape), nm.reshape(wk.shape), nv.reshape(wk.shape)

    return (loss, grad_x, *[grads[k] for k in names], *[delta[k] for k in names],
            *[new_m[k] for k in names], *[new_v[k] for k in names])
```

```python
import functools
import math

import jax
import jax.numpy as jnp
from jax import lax
from jax.experimental import pallas as pl
from jax.experimental.pallas import tpu as pltpu

F32 = jnp.float32
BF16 = jnp.bfloat16
MESH = pl.DeviceIdType.MESH

N_HEADS = 8
CHUNK = 128
N_MOD = 9
EPS = 1e-6
N_DEV = 8
N_CHIP = 4

ADAM_LR = 0.001
ADAM_B1 = 0.9
ADAM_B2 = 0.999
ADAM_EPS = 1e-08
ADAM_WD = 0.01
ADAM_STEP = 10

TOKEN_TILE = 512
FF_TILE = 1408
MIX_TILE = 256
WGRAD_TOKENS = 2048
ELEMENTWISE_COLS = 512
VMEM_LIMIT = 56 * 1024 * 1024

SCATTER_BYTES_PER_US = 68_000
SCATTER_OVERSHOOT = 1.25

ANY = pl.BlockSpec(memory_space=pl.ANY)


def _tile(pref, n):
    t = min(pref, n)
    assert n % t == 0, (pref, n)
    return t


def _slabs(n, width=None):
    width = ELEMENTWISE_COLS if width is None else width
    return [slice(c0, min(c0 + width, n)) for c0 in range(0, n, width)]


def _dot(a, b):
    return jnp.dot(a, b, preferred_element_type=F32)


def _dot_nt(a, b):
    return lax.dot_general(a, b, (((1,), (1,)), ((), ())), preferred_element_type=F32)


def _dot_tn(a, b):
    return lax.dot_general(a, b, (((0,), (0,)), ((), ())), preferred_element_type=F32)


def _sigmoid(x):
    return 1.0 / (1.0 + jnp.exp(-x))


def _sigmoid_fast(x):
    return pl.reciprocal(1.0 + jnp.exp(-x), approx=True)


def _rms(x):
    r = lax.rsqrt(jnp.mean(x * x, axis=-1, keepdims=True) + EPS)
    return x * r, r


def _norm_mod_bwd(x, dh, gain, sc):
    xh, r = _rms(x)
    dsc = jnp.sum(dh * (xh * gain), axis=0, keepdims=True)
    dsh = jnp.sum(dh, axis=0, keepdims=True)
    dn = dh * (1.0 + sc)
    dgain = jnp.sum(dn * xh, axis=0, keepdims=True)
    dy = dn * gain
    dx = r * (dy - xh * jnp.mean(dy * xh, axis=-1, keepdims=True))
    return dx, dsc, dsh, dgain


def _acc(ref, first, val):
    @pl.when(first)
    def _():
        ref[...] = val

    @pl.when(jnp.logical_not(first))
    def _():
        ref[...] += val


class _Comm:
    def __init__(self, args, out_shape, scratch, phases, aliases=None):
        self.args, self.out_shape, self.scratch = list(args), list(out_shape), list(scratch)
        self.phases, self.aliases = phases, dict(aliases or {})


def _merge(*comms):
    comms = [c for c in comms if c is not None]
    if len(comms) <= 1:
        return comms[0] if comms else None
    args = [a for c in comms for a in c.args]
    out_shape = [o for c in comms for o in c.out_shape]
    scratch = [s for c in comms for s in c.scratch]
    aliases, ai, oi = {}, 0, 0
    for c in comms:
        aliases.update({ai + i: oi + o for i, o in c.aliases.items()})
        ai += len(c.args)
        oi += len(c.out_shape)

    def phases(ins, outs, sems):
        parts, ai, oi, si = [], 0, 0, 0
        for c in comms:
            parts.append(c.phases(ins[ai:ai + len(c.args)], outs[oi:oi + len(c.out_shape)], sems[si:si + len(c.scratch)]))
            ai, oi, si = ai + len(c.args), oi + len(c.out_shape), si + len(c.scratch)

        def run(k):
            def go():
                for p in parts:
                    if p[k] is not None:
                        p[k]()
            return go
        return run(0), run(1), run(2)

    return _Comm(args, out_shape, scratch, phases, aliases)


def _call(body, name, grid, in_specs, out_specs, out_shape, scratch, args, comm=None):
    n_in, n_out, n_scr = len(in_specs), len(out_specs), len(scratch)
    sem = ("arbitrary",) * len(grid)
    params = pltpu.CompilerParams(dimension_semantics=sem, vmem_limit_bytes=VMEM_LIMIT)
    if comm is None:
        res = pl.pallas_call(body, name=name, grid=grid, in_specs=in_specs, out_specs=out_specs, out_shape=out_shape,
                             scratch_shapes=scratch, compiler_params=params)(*args)
        return list(res), []
    m_in, m_out = len(comm.args), len(comm.out_shape)

    def full(*refs):
        c_in, c_min = refs[:n_in], refs[n_in:n_in + m_in]
        o = n_in + m_in
        c_out, c_mout = refs[o:o + n_out], refs[o + n_out:o + n_out + m_out]
        o += n_out + m_out
        c_scr, c_sem = refs[o:o + n_scr], refs[o + n_scr:]
        start, mid, finish = comm.phases(c_min, c_mout, c_sem)
        ids = [pl.program_id(a) for a in range(len(grid))]
        first = functools.reduce(jnp.logical_and, [i == 0 for i in ids])
        last = functools.reduce(jnp.logical_and, [i == g - 1 for i, g in zip(ids, grid)])
        pl.when(first)(start)
        if mid is not None:
            pl.when(last)(mid)
        body(*c_in, *c_out, *c_scr)
        pl.when(last)(finish)

    res = pl.pallas_call(
        full, name=name, grid=grid,
        in_specs=list(in_specs) + [ANY] * m_in,
        out_specs=list(out_specs) + [ANY] * m_out,
        out_shape=list(out_shape) + comm.out_shape,
        scratch_shapes=list(scratch) + comm.scratch,
        input_output_aliases={n_in + i: n_out + o for i, o in comm.aliases.items()},
        compiler_params=params,
    )(*args, *comm.args)
    return list(res[:n_out]), list(res[n_out:])


def _comm_call(comm, name):
    m_in, m_out = len(comm.args), len(comm.out_shape)

    def body(*refs):
        start, mid, finish = comm.phases(refs[:m_in], refs[m_in:m_in + m_out], refs[m_in + m_out:])
        start()
        if mid is not None:
            mid()
        finish()

    res = pl.pallas_call(
        body, name=name, in_specs=[ANY] * m_in, out_specs=[ANY] * m_out, out_shape=comm.out_shape,
        scratch_shapes=comm.scratch, input_output_aliases=comm.aliases,
    )(*comm.args)
    return list(res)


def _position():
    return lax.axis_index("x"), lax.axis_index("y"), lax.axis_index("c")


def _gather_comm(items):
    n = len(items)
    half = [s.shape[1] // 2 for s, _, _ in items]

    def full_shape(i):
        s, _, col = items[i]
        _, R, C = s.shape
        return jax.ShapeDtypeStruct((R, N_CHIP * C) if col else (N_CHIP * R, C), s.dtype)

    def phases(ins, outs, sems):
        send_sems, recv_sems, local_sems = sems
        x, y, c = _position()

        def region(i, chip, h):
            s, _, col = items[i]
            _, R, C = s.shape
            if col:
                return outs[i].at[pl.ds(h * half[i], half[i]), pl.ds(chip * C, C)]
            return outs[i].at[pl.ds(chip * R + h * half[i], half[i]), :]

        def mine(i, h):
            return ins[i].at[items[i][1], pl.ds(h * half[i], half[i]), :]

        def copies(kx, ky, kc):
            k_me = 2 * kx + ky
            sibling = (kx, ky, 1 - kc)
            chips = [(1 - kx, ky), (kx, 1 - ky), (1 - kx, 1 - ky)]
            local, first, passed, arrive_ici, arrive_d2d = [], [], [], [], []

            def remote(src, dst, s, to):
                return pltpu.make_async_remote_copy(src_ref=src, dst_ref=dst, send_sem=send_sems.at[s],
                                                    recv_sem=recv_sems.at[s], device_id=to, device_id_type=MESH)

            for i in range(n):
                for h in range(2):
                    local.append(pltpu.make_async_copy(mine(i, h), region(i, k_me, h), local_sems.at[2 * i + h]))
                for j, (px, py) in enumerate(chips):
                    s = 6 * i + j
                    first.append(remote(mine(i, kc), region(i, k_me, kc), s, (px, py, kc)))
                    got = region(i, 2 * px + py, kc)
                    arrive_ici.append(remote(got, got, s, (px, py, kc)))
                    passed.append(remote(got, got, s + 3, sibling))
                    other = region(i, 2 * px + py, 1 - kc)
                    arrive_d2d.append(remote(other, other, s + 3, sibling))
            return local, first, passed, arrive_ici, arrive_d2d

        def on_each_device(fn):
            def go():
                for kx in range(2):
                    for ky in range(2):
                        for kc in range(2):
                            pl.when((x == kx) & (y == ky) & (c == kc))(functools.partial(fn, *copies(kx, ky, kc)))
            return go

        def start(local, first, passed, arrive_ici, arrive_d2d):
            for cp in local + first:
                cp.start()

        def mid(local, first, passed, arrive_ici, arrive_d2d):
            for a, p in zip(arrive_ici, passed):
                a.wait_recv()
                p.start()

        def finish(local, first, passed, arrive_ici, arrive_d2d):
            for a in arrive_d2d:
                a.wait_recv()
            for cp in first + passed:
                cp.wait_send()
            for cp in local:
                cp.wait()

        return on_each_device(start), on_each_device(mid), on_each_device(finish)

    scratch = [pltpu.SemaphoreType.DMA((6 * n,)), pltpu.SemaphoreType.DMA((6 * n,)), pltpu.SemaphoreType.DMA((2 * n,))]
    return _Comm([s for s, _, _ in items], [full_shape(i) for i in range(n)], scratch, phases)


def _sibling_half_comm(gs):
    n = len(gs)

    def phases(ins, outs, sems):
        send_sems, recv_sems = sems
        x, y, c = _position()

        def copies():
            return [pltpu.make_async_remote_copy(
                src_ref=ins[i].at[:, 1 - c], dst_ref=outs[i], send_sem=send_sems.at[i], recv_sem=recv_sems.at[i],
                device_id=(x, y, 1 - c), device_id_type=MESH) for i in range(n)]

        def start():
            for cp in copies():
                cp.start()

        def finish():
            for cp in copies():
                cp.wait()

        return start, None, finish

    out_shape = [jax.ShapeDtypeStruct(g.shape[:1] + g.shape[2:], g.dtype) for g in gs]
    return _Comm(gs, out_shape, [pltpu.SemaphoreType.DMA((n,)), pltpu.SemaphoreType.DMA((n,))], phases)


def _scatter_comm(ps):
    n = len(ps)

    def phases(ins, outs, sems):
        send_sems, recv_sems, local_sems = sems
        x, y, c = _position()
        k_me = 2 * x + y
        chips = [(1 - x, y), (x, 1 - y), (1 - x, 1 - y)]

        def copies():
            local = [pltpu.make_async_copy(ins[i].at[k_me], outs[i].at[k_me], local_sems.at[i]) for i in range(n)]
            remote = [pltpu.make_async_remote_copy(
                src_ref=ins[i].at[2 * px + py], dst_ref=outs[i].at[k_me],
                send_sem=send_sems.at[3 * i + j], recv_sem=recv_sems.at[3 * i + j],
                device_id=(px, py, c), device_id_type=MESH) for i in range(n) for j, (px, py) in enumerate(chips)]
            return local, remote

        def start():
            local, remote = copies()
            for cp in local + remote:
                cp.start()

        def finish():
            local, remote = copies()
            for cp in remote + local:
                cp.wait()

        return start, None, finish

    scratch = [pltpu.SemaphoreType.DMA((3 * n,)), pltpu.SemaphoreType.DMA((3 * n,)), pltpu.SemaphoreType.DMA((n,))]
    return _Comm(ps, [jax.ShapeDtypeStruct(p.shape, p.dtype) for p in ps], scratch, phases)


def _share_comm(rs, l):
    n = len(rs)

    def phases(ins, outs, sems):
        send_sems, recv_sems = sems
        x, y, c = _position()

        def copy(i, h):
            return pltpu.make_async_remote_copy(
                src_ref=outs[i].at[l, h], dst_ref=outs[i].at[l, h], send_sem=send_sems.at[i], recv_sem=recv_sems.at[i],
                device_id=(x, y, 1 - c), device_id_type=MESH)

        def start():
            for i in range(n):
                copy(i, c).start()

        def finish():
            for i in range(n):
                copy(i, 1 - c).wait_recv()
            for i in range(n):
                copy(i, c).wait_send()

        return start, None, finish

    return _Comm(rs, [jax.ShapeDtypeStruct(r.shape, r.dtype) for r in rs],
                 [pltpu.SemaphoreType.DMA((n,)), pltpu.SemaphoreType.DMA((n,))], phases,
                 aliases={i: i for i in range(n)})


def _all_gather_comm(block):
    def phases(ins, outs, sems):
        send_sems, recv_sems, local_sem = sems
        (src,), (out,) = ins, outs
        x, y, c = _position()
        sibling = (x, y, 1 - c)
        chips = [(1 - x, y), (x, 1 - y), (1 - x, 1 - y)]

        def slot(px, py, pc):
            return out.at[4 * px + 2 * py + pc]

        def copy(k, blk, to, own=False):
            return pltpu.make_async_remote_copy(
                src_ref=src if own else slot(*blk), dst_ref=slot(*blk),
                send_sem=send_sems.at[k], recv_sem=recv_sems.at[k], device_id=to, device_id_type=MESH)

        mine = lambda: pltpu.make_async_copy(src, slot(x, y, c), local_sem.at[0])
        first = lambda: [copy(0, (x, y, c), sibling, True)] + [
            copy(1 + j, (x, y, c), (*chip, c), True) for j, chip in enumerate(chips)]
        passed = lambda: [copy(4 + j, (*chip, c), sibling) for j, chip in enumerate(chips)]

        def start():
            mine().start()
            for cp in first():
                cp.start()

        def mid():
            for j, (chip, p) in enumerate(zip(chips, passed())):
                copy(1 + j, (*chip, c), (x, y, c)).wait_recv()
                p.start()

        def finish():
            copy(0, sibling, (x, y, c)).wait_recv()
            for j, chip in enumerate(chips):
                copy(4 + j, (*chip, 1 - c), (x, y, c)).wait_recv()
            for cp in first() + passed():
                cp.wait_send()
            mine().wait()

        return start, mid, finish

    scratch = [pltpu.SemaphoreType.DMA((7,)), pltpu.SemaphoreType.DMA((7,)), pltpu.SemaphoreType.DMA((1,))]
    return _Comm([block], [jax.ShapeDtypeStruct((N_DEV,) + block.shape, block.dtype)], scratch, phases)


def _ffn_up(x, gain, sh, sc, wgu, comm=None):
    T, D = x.shape
    F = wgu.shape[1] // 2
    B = sh.shape[0]
    tm = _tile(TOKEN_TILE, T // B)
    tf = _tile(FF_TILE, F)
    tps = (T // B) // tm
    nf = F // tf

    slabs = _slabs(tf)

    def body(x_ref, gain_ref, sh_ref, sc_ref, wg_ref, wu_ref, gu_ref, a_ref):
        xh, _ = _rms(x_ref[...])
        h = (xh * gain_ref[...] * (1.0 + sc_ref[0]) + sh_ref[0]).astype(BF16)

        def dots(s):
            return _dot(h, wg_ref[:, s]), _dot(h, wu_ref[:, s])

        nxt = dots(slabs[0])
        for j, s in enumerate(slabs):
            g, u = nxt
            if j + 1 < len(slabs):
                nxt = dots(slabs[j + 1])
            gu_ref[0, :, s] = g.astype(BF16)
            gu_ref[1, :, s] = u.astype(BF16)
            a_ref[:, s] = (g * _sigmoid(g) * u).astype(BF16)

    seq = lambda k, i: (i // tps, 0, 0)
    return _call(
        body, "ffn_up", (nf, T // tm),
        [
            pl.BlockSpec((tm, D), lambda k, i: (i, 0)),
            pl.BlockSpec((1, D), lambda k, i: (0, 0)),
            pl.BlockSpec((1, 1, D), seq),
            pl.BlockSpec((1, 1, D), seq),
            pl.BlockSpec((D, tf), lambda k, i: (0, k)),
            pl.BlockSpec((D, tf), lambda k, i: (0, nf + k)),
        ],
        [
            pl.BlockSpec((2, tm, tf), lambda k, i: (0, i, k)),
            pl.BlockSpec((tm, tf), lambda k, i: (i, k)),
        ],
        [
            jax.ShapeDtypeStruct((2, T, F), BF16),
            jax.ShapeDtypeStruct((T, F), BF16),
        ],
        [],
        (x, gain, sh, sc, wgu, wgu), comm)


def _ffn_down(a, x, gate, wd, comm=None):
    T, F = a.shape
    D = x.shape[1]
    B = gate.shape[0]
    tm = _tile(TOKEN_TILE, T // B)
    tps = (T // B) // tm

    def body(a_ref, x_ref, gate_ref, wd_ref, xo_ref, f_ref):
        f = _dot(a_ref[...], wd_ref[...])
        f_ref[...] = f.astype(BF16)
        xo_ref[...] = x_ref[...] + 0.5 * gate_ref[0] * f

    return _call(
        body, "ffn_down", (T // tm,),
        [
            pl.BlockSpec((tm, F), lambda i: (i, 0)),
            pl.BlockSpec((tm, D), lambda i: (i, 0)),
            pl.BlockSpec((1, 1, D), lambda i: (i // tps, 0, 0)),
            pl.BlockSpec((F, D), lambda i: (0, 0)),
        ],
        [pl.BlockSpec((tm, D), lambda i: (i, 0)), pl.BlockSpec((tm, D), lambda i: (i, 0))],
        [jax.ShapeDtypeStruct((T, D), F32), jax.ShapeDtypeStruct((T, D), BF16)],
        [],
        (a, x, gate, wd), comm)


def _ffn_bwd_down(dxo, gu, gate, wd, comm=None):
    T, D = dxo.shape
    F = wd.shape[0]
    B = gate.shape[0]
    tm = _tile(TOKEN_TILE, T // B)
    tf = _tile(FF_TILE, F)
    tps = (T // B) // tm
    nf = F // tf
    slabs = _slabs(tf)

    def body(dxo_ref, gu_ref, gate_ref, wd_ref, dgu_ref):
        df = (0.5 * gate_ref[0] * dxo_ref[...]).astype(BF16)
        nxt = _dot_nt(df, wd_ref[slabs[0], :])
        for j, s in enumerate(slabs):
            da = nxt
            if j + 1 < len(slabs):
                nxt = _dot_nt(df, wd_ref[slabs[j + 1], :])
            g = gu_ref[0, :, s].astype(F32)
            sg = _sigmoid_fast(g)
            t = g * sg
            dgu_ref[1, :, s] = (da * t).astype(BF16)
            dgu_ref[0, :, s] = (da * gu_ref[1, :, s].astype(F32) * (sg + t - t * sg)).astype(BF16)

    return _call(
        body, "ffn_bwd_down", (nf, T // tm),
        [
            pl.BlockSpec((tm, D), lambda k, i: (i, 0)),
            pl.BlockSpec((2, tm, tf), lambda k, i: (0, i, k)),
            pl.BlockSpec((1, 1, D), lambda k, i: (i // tps, 0, 0)),
            pl.BlockSpec((tf, D), lambda k, i: (k, 0)),
        ],
        [pl.BlockSpec((2, tm, tf), lambda k, i: (0, i, k))],
        [jax.ShapeDtypeStruct((2, T, F), BF16)],
        [],
        (dxo, gu, gate, wd), comm)


def _ffn_bwd_up(dxo, x, dgu, f, gain, sh, sc, gate, wgu, comm=None):
    T, D = x.shape
    F = wgu.shape[1] // 2
    B = sc.shape[0]
    tm = _tile(TOKEN_TILE, T // B)
    tps = (T // B) // tm

    def body(dxo_ref, x_ref, dgu_ref, f_ref, gain_ref, sh_ref, sc_ref, gate_ref, w_ref,
             dx_ref, h_ref, df_ref, dsc_ref, dsh_ref, dgain_ref, dgate_ref):
        i = pl.program_id(0)
        first_of_seq = (i % tps) == 0
        gain = gain_ref[...]
        sc = sc_ref[0]
        halves = _slabs(tm, tm // 2)

        def dots(r):
            return _dot_nt(dgu_ref[0, r, :], w_ref[:, 0:F]) + _dot_nt(dgu_ref[1, r, :], w_ref[:, F:])

        nxt = dots(halves[0])
        sums = None
        for j, r in enumerate(halves):
            dh = nxt
            if j + 1 < len(halves):
                nxt = dots(halves[j + 1])
            dxo = dxo_ref[r, :]
            x = x_ref[r, :]
            dx, dsc, dsh, dgain = _norm_mod_bwd(x, dh, gain, sc)
            dx_ref[r, :] = dxo + dx
            h_ref[r, :] = (_rms(x)[0] * gain * (1.0 + sc) + sh_ref[0]).astype(BF16)
            df_ref[r, :] = (0.5 * gate_ref[0] * dxo).astype(BF16)
            part = (dsc, dsh, dgain, 0.5 * jnp.sum(dxo * f_ref[r, :].astype(F32), axis=0, keepdims=True))
            sums = part if sums is None else tuple(a + b for a, b in zip(sums, part))
        _acc(dsc_ref.at[0], first_of_seq, sums[0])
        _acc(dsh_ref.at[0], first_of_seq, sums[1])
        _acc(dgain_ref, i == 0, sums[2])
        _acc(dgate_ref.at[0], first_of_seq, sums[3])

    seq = lambda i: (i // tps, 0, 0)
    row = lambda i: (i, 0)
    return _call(
        body, "ffn_bwd_up", (T // tm,),
        [
            pl.BlockSpec((tm, D), row),
            pl.BlockSpec((tm, D), row),
            pl.BlockSpec((2, tm, F), lambda i: (0, i, 0)),
            pl.BlockSpec((tm, D), row),
            pl.BlockSpec((1, D), lambda i: (0, 0)),
            pl.BlockSpec((1, 1, D), seq),
            pl.BlockSpec((1, 1, D), seq),
            pl.BlockSpec((1, 1, D), seq),
            pl.BlockSpec((D, 2 * F), lambda i: (0, 0), pipeline_mode=pl.Buffered(1)),
        ],
        [
            pl.BlockSpec((tm, D), row),
            pl.BlockSpec((tm, D), row),
            pl.BlockSpec((tm, D), row),
            pl.BlockSpec((1, 1, D), seq),
            pl.BlockSpec((1, 1, D), seq),
            pl.BlockSpec((1, D), lambda i: (0, 0)),
            pl.BlockSpec((1, 1, D), seq),
        ],
        [
            jax.ShapeDtypeStruct((T, D), F32),
            jax.ShapeDtypeStruct((T, D), BF16),
            jax.ShapeDtypeStruct((T, D), BF16),
            jax.ShapeDtypeStruct((B, 1, D), F32),
            jax.ShapeDtypeStruct((B, 1, D), F32),
            jax.ShapeDtypeStruct((1, D), F32),
            jax.ShapeDtypeStruct((B, 1, D), F32),
        ],
        [],
        (dxo, x, dgu, f, gain, sh, sc, gate, wgu), comm)


def _wgrad(a, b, tmm, tn, col_major, name, tokens=WGRAD_TOKENS, comm=None):
    T, M = a.shape
    nb, _, Nb = b.shape
    N = nb * Nb
    tk = _tile(tokens, T)
    npb = Nb // tn
    assert M % tmm == 0 and Nb % tn == 0
    if col_major:
        assert tmm == M
        shape = (N // tn, 2, M // 2, tn)
        out_spec = pl.BlockSpec((None, 2, M // 2, tn), lambda i, j, t: (j, 0, 0, 0))
    else:
        shape = (M // tmm, tmm, N)
        out_spec = pl.BlockSpec((None, tmm, tn), lambda i, j, t: (i, 0, j))

    def body(a_ref, b_ref, o_ref):
        t = pl.program_id(2)
        res = _dot_tn(a_ref[...], b_ref[...])
        if col_major:
            _acc(o_ref.at[0], t == 0, res[:M // 2])
            _acc(o_ref.at[1], t == 0, res[M // 2:])
        else:
            _acc(o_ref, t == 0, res)

    return _call(
        body, name, (M // tmm, N // tn, T // tk),
        [
            pl.BlockSpec((tk, tmm), lambda i, j, t: (t, i)),
            pl.BlockSpec((None, tk, tn), lambda i, j, t: (j // npb, t, j % npb)),
        ],
        [out_spec], [jax.ShapeDtypeStruct(shape, F32)], [],
        (a, b), comm)


def _mixin_fwd(x, gain, sh, sc, win, comm=None):
    T, D = x.shape
    P = win.shape[1]
    B = sh.shape[0]
    tm = _tile(TOKEN_TILE, T // B)
    tps = (T // B) // tm

    def body(x_ref, gain_ref, sh_ref, sc_ref, w_ref, proj_ref, h_ref):
        xh, _ = _rms(x_ref[...])
        h = (xh * gain_ref[...] * (1.0 + sc_ref[0]) + sh_ref[0]).astype(BF16)
        h_ref[...] = h
        proj_ref[...] = _dot(h, w_ref[...])

    seq = lambda i: (i // tps, 0, 0)
    return _call(
        body, "mixin_fwd", (T // tm,),
        [
            pl.BlockSpec((tm, D), lambda i: (i, 0)),
            pl.BlockSpec((1, D), lambda i: (0, 0)),
            pl.BlockSpec((1, 1, D), seq),
            pl.BlockSpec((1, 1, D), seq),
            pl.BlockSpec((D, P), lambda i: (0, 0)),
        ],
        [pl.BlockSpec((tm, P), lambda i: (i, 0)), pl.BlockSpec((tm, D), lambda i: (i, 0))],
        [jax.ShapeDtypeStruct((T, P), F32), jax.ShapeDtypeStruct((T, D), BF16)],
        [],
        (x, gain, sh, sc, win), comm)


def _mixin_bwd(dxo, x, dproj, gain, sc, win, comm=None):
    T, D = x.shape
    P = win.shape[1]
    B = sc.shape[0]
    tm = _tile(TOKEN_TILE, T // B)
    tps = (T // B) // tm

    def body(dxo_ref, x_ref, dp_ref, gain_ref, sc_ref, w_ref, dx_ref, dsc_ref, dsh_ref, dgain_ref):
        i = pl.program_id(0)
        first_of_seq = (i % tps) == 0
        halves = _slabs(tm, tm // 2)
        nxt = _dot_nt(dp_ref[halves[0], :], w_ref[...])
        sums = None
        for j, r in enumerate(halves):
            dh = nxt
            if j + 1 < len(halves):
                nxt = _dot_nt(dp_ref[halves[j + 1], :], w_ref[...])
            part = _norm_mod_bwd(x_ref[r, :], dh, gain_ref[...], sc_ref[0])
            dx_ref[r, :] = dxo_ref[r, :] + part[0]
            sums = part[1:] if sums is None else tuple(a + b for a, b in zip(sums, part[1:]))
        _acc(dsc_ref.at[0], first_of_seq, sums[0])
        _acc(dsh_ref.at[0], first_of_seq, sums[1])
        _acc(dgain_ref, i == 0, sums[2])

    seq = lambda i: (i // tps, 0, 0)
    row = lambda i: (i, 0)
    return _call(
        body, "mixin_bwd", (T // tm,),
        [
            pl.BlockSpec((tm, D), row),
            pl.BlockSpec((tm, D), row),
            pl.BlockSpec((tm, P), row),
            pl.BlockSpec((1, D), lambda i: (0, 0)),
            pl.BlockSpec((1, 1, D), seq),
            pl.BlockSpec((D, P), lambda i: (0, 0)),
        ],
        [
            pl.BlockSpec((tm, D), row),
            pl.BlockSpec((1, 1, D), seq),
            pl.BlockSpec((1, 1, D), seq),
            pl.BlockSpec((1, D), lambda i: (0, 0)),
        ],
        [
            jax.ShapeDtypeStruct((T, D), F32),
            jax.ShapeDtypeStruct((B, 1, D), F32),
            jax.ShapeDtypeStruct((B, 1, D), F32),
            jax.ShapeDtypeStruct((1, D), F32),
        ],
        [],
        (dxo, x, dproj, gain, sc, win), comm)


def _head_mean(z, pmat):
    hi = z.astype(BF16)
    lo = (z - hi.astype(F32)).astype(BF16)
    return _dot(hi, pmat) + _dot(lo, pmat)


def _gelu_parts(x):
    cdf = 0.5 * (1.0 + lax.erf(x * (1.0 / math.sqrt(2.0))))
    return x * cdf, cdf


def _gelu_grad(x, cdf):
    return cdf + x * jnp.exp(-0.5 * x * x) * (1.0 / math.sqrt(2.0 * math.pi))


def _head_masks(da):
    hd = da // N_HEADS
    col = lax.broadcasted_iota(jnp.int32, (1, da), 1)
    return [(col >= h * hd) & (col < (h + 1) * hd) for h in range(N_HEADS)]


def _select_heads(res, masks):
    out = res[0:CHUNK]
    for h in range(1, N_HEADS):
        out = jnp.where(masks[h], res[h * CHUNK:(h + 1) * CHUNK], out)
    return out


def _causal_stack(w, transposed):
    r = lax.broadcasted_iota(jnp.int32, w.shape, 0) % CHUNK
    c = lax.broadcasted_iota(jnp.int32, w.shape, 1)
    keep = (c >= r) if transposed else (c <= r)
    return jnp.where(keep, w, 0.0)


def _mix_core_forward(proj, zprev, prm, da, db):
    n = proj.shape[0]
    ua = proj[:, 0:da]
    va = proj[:, da:2 * da]
    bg = proj[:, 2 * da:2 * da + db]
    cg = proj[:, 2 * da + db:2 * da + 2 * db]
    xb = proj[:, 2 * da + 2 * db:]
    ug, ucdf = _gelu_parts(ua)
    vg, vcdf = _gelu_parts(va)
    zc = vg - _head_mean(vg, prm["pmat"])
    rs = lax.rsqrt(_head_mean(zc * zc, prm["pmat"]) + EPS)
    vhat = zc * rs
    vln = (vhat * prm["lng"] + prm["lnb"]).astype(BF16)
    wst = _causal_stack(prm["wst"], False).astype(BF16)
    masks = _head_masks(da)
    mixed = []
    for j in range(n // CHUNK):
        res = _dot(wst, vln[j * CHUNK:(j + 1) * CHUNK])
        mixed.append(_select_heads(res, masks) + prm["bias"])
    mixed = mixed[0] if len(mixed) == 1 else jnp.concatenate(mixed, axis=0)
    ya = ug * mixed
    z = cg * xb
    row = lax.broadcasted_iota(jnp.int32, z.shape, 0)
    z1 = jnp.where(row == 0, zprev[7:8], pltpu.roll(z, 1, 0))
    z2 = jnp.where(row == 0, zprev[6:7], jnp.where(row == 1, zprev[7:8], pltpu.roll(z, 2, 0)))
    cw = prm["convw"]
    conv = z2 * cw[0:1] + z1 * cw[1:2] + z * cw[2:3]
    yb = bg * conv
    yah, ra = _rms(ya)
    ybh, rb = _rms(yb)
    return dict(ua=ua, va=va, bg=bg, cg=cg, xb=xb, ug=ug, ucdf=ucdf, vcdf=vcdf, rs=rs, vhat=vhat, vln=vln,
                mixed=mixed, z=z, z1=z1, z2=z2, conv=conv, yah=yah, ra=ra, ybh=ybh, rb=rb, masks=masks)


def _mix_params(lng_ref, lnb_ref, wst_ref, bias_ref, pmat_ref, convw_ref):
    return dict(lng=lng_ref[...], lnb=lnb_ref[...], wst=wst_ref[...], bias=bias_ref[...],
                pmat=pmat_ref[...], convw=convw_ref[...])


def _mix_core_fwd(proj, x, gate, wout, lng, lnb, wst, bias, pmat, convw, og, comm=None):
    T, P = proj.shape
    D = x.shape[1]
    B = gate.shape[0]
    da = lng.shape[1]
    db = convw.shape[1]
    tm = _tile(MIX_TILE, T // B)
    tps = (T // B) // tm

    def body(proj_ref, x_ref, gate_ref, wout_ref, lng_ref, lnb_ref, wst_ref, bias_ref, pmat_ref, convw_ref,
             og_ref, xo_ref, yn_ref, halo):
        i = pl.program_id(0)

        @pl.when((i % tps) == 0)
        def _():
            halo[...] = jnp.zeros_like(halo)

        prm = _mix_params(lng_ref, lnb_ref, wst_ref, bias_ref, pmat_ref, convw_ref)
        r = _mix_core_forward(proj_ref[...], halo[...], prm, da, db)
        halo[...] = r["z"][tm - 8:tm]
        og = og_ref[...]
        yn_ref[:, 0:da] = (r["yah"] * og[:, 0:da]).astype(BF16)
        yn_ref[:, da:] = (r["ybh"] * og[:, da:]).astype(BF16)
        xo_ref[...] = x_ref[...] + gate_ref[0] * _dot(yn_ref[...], wout_ref[...])

    full = lambda a: pl.BlockSpec(a.shape, lambda i: (0,) * a.ndim)
    return _call(
        body, "mix_core_fwd", (T // tm,),
        [
            pl.BlockSpec((tm, P), lambda i: (i, 0)),
            pl.BlockSpec((tm, D), lambda i: (i, 0)),
            pl.BlockSpec((1, 1, D), lambda i: (i // tps, 0, 0)),
            full(wout), full(lng), full(lnb), full(wst), full(bias), full(pmat), full(convw), full(og),
        ],
        [pl.BlockSpec((tm, D), lambda i: (i, 0)), pl.BlockSpec((tm, D), lambda i: (i, 0))],
        [jax.ShapeDtypeStruct((T, D), F32), jax.ShapeDtypeStruct((T, D), BF16)],
        [pltpu.VMEM((8, db), F32)],
        (proj, x, gate, wout, lng, lnb, wst, bias, pmat, convw, og), comm)


def _mix_core_bwd(proj, dxo, gate, wout, lng, lnb, wst, wstt, bias, pmat, convw, og, comm=None):
    T, P = proj.shape
    D = dxo.shape[1]
    B = gate.shape[0]
    da = lng.shape[1]
    db = convw.shape[1]
    assert da == db and P == 2 * da + 3 * db
    tm = _tile(MIX_TILE, T // B)
    tps = (T // B) // tm
    nt = T // tm
    hd = da // N_HEADS

    def body(proj_ref, cgp_ref, xbp_ref, dxo_ref, gate_ref, wout_ref, lng_ref, lnb_ref, wst_ref, wstt_ref,
             bias_ref, pmat_ref, convw_ref, og_ref,
             dproj_ref, do_ref, dgate_ref, dog_ref, dwst_ref, dbias_ref, dlng_ref, dlnb_ref, dconvw_ref, carry):
        i = pl.program_id(0)
        ri = nt - 1 - i
        first = i == 0
        end_of_seq = (ri % tps) == tps - 1
        start_of_seq = (ri % tps) == 0

        @pl.when(end_of_seq)
        def _():
            carry[...] = jnp.zeros_like(carry)

        prm = _mix_params(lng_ref, lnb_ref, wst_ref, bias_ref, pmat_ref, convw_ref)
        zprev = jnp.where(start_of_seq, 0.0, cgp_ref[...] * xbp_ref[...])
        r = _mix_core_forward(proj_ref[...], zprev, prm, da, db)
        og = og_ref[...]
        pmat = prm["pmat"]

        yn = jnp.concatenate([(r["yah"] * og[:, 0:da]).astype(BF16), (r["ybh"] * og[:, da:]).astype(BF16)], axis=1)
        dxo = dxo_ref[...]
        o = _dot(yn, wout_ref[...])
        _acc(dgate_ref.at[0], end_of_seq, jnp.sum(dxo * o, axis=0, keepdims=True))
        d_o = (gate_ref[0] * dxo).astype(BF16)
        do_ref[...] = d_o
        dyn = _dot_nt(d_o, wout_ref[...])

        def rms_bwd(dyn_g, yh, rr, og_g):
            dog_g = jnp.sum(dyn_g * yh, axis=0, keepdims=True)
            dyh = dyn_g * og_g
            return rr * (dyh - yh * jnp.mean(dyh * yh, axis=-1, keepdims=True)), dog_g

        dya, dog_a = rms_bwd(dyn[:, 0:da], r["yah"], r["ra"], og[:, 0:da])
        dyb, dog_b = rms_bwd(dyn[:, da:], r["ybh"], r["rb"], og[:, da:])
        _acc(dog_ref, first, jnp.concatenate([dog_a, dog_b], axis=1))

        dug = dya * r["mixed"]
        dmixed = dya * r["ug"]
        wstt_b = _causal_stack(wstt_ref[...], True).astype(BF16)
        masks = r["masks"]
        dbias = jnp.zeros((CHUNK, da), F32)
        dwst = jnp.zeros((N_HEADS * CHUNK, CHUNK), F32)
        dvln = []
        for j in range(tm // CHUNK):
            dm = dmixed[j * CHUNK:(j + 1) * CHUNK]
            dbias = dbias + dm
            dmb = dm.astype(BF16)
            stack = jnp.concatenate([jnp.where(masks[h], dmb, jnp.zeros_like(dmb)) for h in range(N_HEADS)], axis=0)
            dwst = dwst + _dot_nt(stack, r["vln"][j * CHUNK:(j + 1) * CHUNK])
            dvln.append(_select_heads(_dot(wstt_b, dmb), masks))
        dvln = dvln[0] if len(dvln) == 1 else jnp.concatenate(dvln, axis=0)
        _acc(dbias_ref, first, dbias)
        _acc(dwst_ref, first, dwst)
        _acc(dlng_ref, first, jnp.sum(dvln * r["vhat"], axis=0, keepdims=True))
        _acc(dlnb_ref, first, jnp.sum(dvln, axis=0, keepdims=True))
        dvhat = dvln * prm["lng"]
        dvg = r["rs"] * (dvhat - _head_mean(dvhat, pmat) - r["vhat"] * _head_mean(dvhat * r["vhat"], pmat))
        dproj_ref[:, 0:da] = (dug * _gelu_grad(r["ua"], r["ucdf"])).astype(BF16)
        dproj_ref[:, da:2 * da] = (dvg * _gelu_grad(r["va"], r["vcdf"])).astype(BF16)

        dproj_ref[:, 2 * da:2 * da + db] = (dyb * r["conv"]).astype(BF16)
        dconv = dyb * r["bg"]
        dcw = jnp.concatenate([
            jnp.sum(dconv * r["z2"], axis=0, keepdims=True),
            jnp.sum(dconv * r["z1"], axis=0, keepdims=True),
            jnp.sum(dconv * r["z"], axis=0, keepdims=True),
            jnp.zeros((5, db), F32)], axis=0)
        _acc(dconvw_ref, first, dcw)
        nxt = carry[...]
        row = lax.broadcasted_iota(jnp.int32, dconv.shape, 0)
        dc1 = jnp.where(row == tm - 1, nxt[0:1], pltpu.roll(dconv, tm - 1, 0))
        dc2 = jnp.where(row == tm - 2, nxt[0:1], jnp.where(row == tm - 1, nxt[1:2], pltpu.roll(dconv, tm - 2, 0)))
        carry[...] = dconv[0:8]
        cw = prm["convw"]
        dz = dconv * cw[2:3] + dc1 * cw[1:2] + dc2 * cw[0:1]
        dproj_ref[:, 2 * da + db:2 * da + 2 * db] = (dz * r["xb"]).astype(BF16)
        dproj_ref[:, 2 * da + 2 * db:] = (dz * r["cg"]).astype(BF16)

        @pl.when(i == nt - 1)
        def _():
            dwst_ref[...] = _causal_stack(dwst_ref[...], False)
            dbias_ref[...] = _head_mean(dbias_ref[...], pmat) * float(hd)

    full = lambda a: pl.BlockSpec(a.shape, lambda i: (0,) * a.ndim)
    const = lambda i: (0, 0)
    rev = lambda i: (nt - 1 - i, 0)
    prev8 = lambda col: (lambda i: (jnp.maximum((nt - 1 - i) * (tm // 8) - 1, 0), col))
    return _call(
        body, "mix_core_bwd", (nt,),
        [
            pl.BlockSpec((tm, P), rev),
            pl.BlockSpec((8, db), prev8((2 * da + db) // db)),
            pl.BlockSpec((8, db), prev8((2 * da + 2 * db) // db)),
            pl.BlockSpec((tm, D), rev),
            pl.BlockSpec((1, 1, D), lambda i: ((nt - 1 - i) // tps, 0, 0)),
            full(wout), full(lng), full(lnb), full(wst), full(wstt), full(bias), full(pmat), full(convw), full(og),
        ],
        [
            pl.BlockSpec((tm, P), rev),
            pl.BlockSpec((tm, D), rev),
            pl.BlockSpec((1, 1, D), lambda i: ((nt - 1 - i) // tps, 0, 0)),
            pl.BlockSpec((1, D), const),
            pl.BlockSpec((N_HEADS * CHUNK, CHUNK), const),
            pl.BlockSpec((CHUNK, da), const),
            pl.BlockSpec((1, da), const),
            pl.BlockSpec((1, da), const),
            pl.BlockSpec((8, db), const),
        ],
        [
            jax.ShapeDtypeStruct((T, P), BF16),
            jax.ShapeDtypeStruct((T, D), BF16),
            jax.ShapeDtypeStruct((B, 1, D), F32),
            jax.ShapeDtypeStruct((1, D), F32),
            jax.ShapeDtypeStruct((N_HEADS * CHUNK, CHUNK), F32),
            jax.ShapeDtypeStruct((CHUNK, da), F32),
            jax.ShapeDtypeStruct((1, da), F32),
            jax.ShapeDtypeStruct((1, da), F32),
            jax.ShapeDtypeStruct((8, db), F32),
        ],
        [pltpu.VMEM((8, db), F32)],
        (proj, proj, proj, dxo, gate, wout, lng, lnb, wst, wstt, bias, pmat, convw, og), comm)


def _loss_head(x, target, gain):
    T, D = x.shape
    tm = _tile(TOKEN_TILE, T)

    def body(x_ref, t_ref, gain_ref, dx_ref, loss_ref, dgain_ref):
        first = pl.program_id(0) == 0
        xh, r = _rms(x_ref[...])
        gain = gain_ref[...]
        err = xh * gain - t_ref[...]
        _acc(loss_ref, first, jnp.zeros((8, 128), F32) + 0.5 * jnp.sum(err * err) / D)
        dout = err * (1.0 / D)
        _acc(dgain_ref, first, jnp.sum(dout * xh, axis=0, keepdims=True))
        dy = dout * gain
        dx_ref[...] = r * (dy - xh * jnp.mean(dy * xh, axis=-1, keepdims=True))

    return _call(
        body, "loss_head", (T // tm,),
        [
            pl.BlockSpec((tm, D), lambda i: (i, 0)),
            pl.BlockSpec((tm, D), lambda i: (i, 0)),
            pl.BlockSpec((1, D), lambda i: (0, 0)),
        ],
        [
            pl.BlockSpec((tm, D), lambda i: (i, 0)),
            pl.BlockSpec((8, 128), lambda i: (0, 0)),
            pl.BlockSpec((1, D), lambda i: (0, 0)),
        ],
        [
            jax.ShapeDtypeStruct((T, D), F32),
            jax.ShapeDtypeStruct((8, 128), F32),
            jax.ShapeDtypeStruct((1, D), F32),
        ],
        [],
        (x, target, gain))[0]


def _ada_fwd(c_all, ada_w, ada_b):
    n, D = c_all.shape
    L, _, sa = ada_w.shape
    tn = _tile(768, sa)

    def body(c_ref, w_ref, b_ref, act_ref, o_ref):
        c = c_ref[...]
        act = (c * _sigmoid(c)).astype(BF16)
        act_ref[...] = act
        o_ref[...] = _dot(act, w_ref[...].astype(BF16)) + b_ref[...]

    return _call(
        body, "ada_fwd", (L, sa // tn),
        [
            pl.BlockSpec((n, D), lambda l, j: (0, 0)),
            pl.BlockSpec((None, D, tn), lambda l, j: (l, 0, j)),
            pl.BlockSpec((None, 1, tn), lambda l, j: (l, 0, j)),
        ],
        [
            pl.BlockSpec((n, D), lambda l, j: (0, 0)),
            pl.BlockSpec((None, n, tn), lambda l, j: (l, 0, j)),
        ],
        [jax.ShapeDtypeStruct((n, D), BF16), jax.ShapeDtypeStruct((L, n, sa), F32)],
        [],
        (c_all, ada_w, ada_b))[0]


def _ada_bwd(c_act, d_ada, comm=None):
    n, D = c_act.shape
    L, _, sa = d_ada.shape
    tn = _tile(768, sa)

    def body(c_ref, d_ref, o_ref):
        o_ref[...] = _dot_tn(c_ref[...], d_ref[...])

    return _call(
        body, "ada_bwd", (L, sa // tn),
        [pl.BlockSpec((n, D), lambda l, j: (0, 0)), pl.BlockSpec((None, n, tn), lambda l, j: (l, 0, j))],
        [pl.BlockSpec((None, D, tn), lambda l, j: (l, 0, j))],
        [jax.ShapeDtypeStruct((L, D, sa), F32)],
        [],
        (c_act, d_ada), comm)


def _colsum(a):
    L, n, C = a.shape

    def body(a_ref, o_ref):
        o_ref[...] = jnp.sum(a_ref[...], axis=0, keepdims=True)

    return _call(
        body, "colsum", (L,),
        [pl.BlockSpec((None, n, C), lambda l: (l, 0, 0))],
        [pl.BlockSpec((None, 1, C), lambda l: (l, 0, 0))],
        [jax.ShapeDtypeStruct((L, 1, C), F32)],
        [],
        (a,))[0][0]


def _row_tile(rows, cols, nbuf):
    budget = VMEM_LIMIT // 3 // (2 * nbuf * 4 * cols)
    t = rows
    while t > max(budget, 8) and t % 2 == 0 and (t // 2) % 8 == 0:
        t //= 2
    return t


def _pair_sum(g, recv, core):
    n, _, R, C = g.shape
    tr = _row_tile(R, C, 3)

    def body(core_ref, g_ref, r_ref, o_ref):
        o_ref[...] = (g_ref[...] + r_ref[...]).astype(BF16)

    return pl.pallas_call(
        body,
        name="pair_sum",
        grid_spec=pltpu.PrefetchScalarGridSpec(
            num_scalar_prefetch=1,
            grid=(n, R // tr),
            in_specs=[
                pl.BlockSpec((None, None, tr, C), lambda i, r, core_ref: (i, core_ref[0], r, 0)),
                pl.BlockSpec((None, tr, C), lambda i, r, core_ref: (i, r, 0)),
            ],
            out_specs=pl.BlockSpec((None, tr, C), lambda i, r, core_ref: (i, r, 0)),
        ),
        out_shape=jax.ShapeDtypeStruct((n, R, C), BF16),
        compiler_params=pltpu.CompilerParams(dimension_semantics=("arbitrary", "arbitrary"),
                                             vmem_limit_bytes=VMEM_LIMIT),
    )(core, g, recv)


def _chip_sum(q, core, l, n_layers, prev):
    nq, R, C = q.shape
    tr = _row_tile(R, C, 4)

    def body(core_ref, q_ref, *rest):
        o_ref = rest[-1]
        s = q_ref[0].astype(F32)
        for j in range(1, nq):
            s = s + q_ref[j].astype(F32)
        o_ref[...] = s

    in_specs = [pl.BlockSpec((nq, tr, C), lambda r, core_ref: (0, r, 0))]
    args = [core, q]
    aliases = {}
    if prev is not None:
        in_specs.append(ANY)
        args.append(prev)
        aliases = {2: 0}
    return pl.pallas_call(
        body,
        name="chip_sum",
        grid_spec=pltpu.PrefetchScalarGridSpec(
            num_scalar_prefetch=1,
            grid=(R // tr,),
            in_specs=in_specs,
            out_specs=pl.BlockSpec((None, None, tr, C), lambda r, core_ref: (l, core_ref[0], r, 0)),
        ),
        out_shape=jax.ShapeDtypeStruct((n_layers, 2, R, C), F32),
        input_output_aliases=aliases,
        compiler_params=pltpu.CompilerParams(dimension_semantics=("arbitrary",), vmem_limit_bytes=VMEM_LIMIT),
    )(*args)


def _sum_blocks(a, n):
    M = a.shape[0] // n
    C = a.shape[1]

    def body(a_ref, o_ref):
        s = a_ref[0:M]
        for j in range(1, n):
            s = s + a_ref[j * M:(j + 1) * M]
        o_ref[...] = s

    return pl.pallas_call(
        body,
        name="sum_blocks",
        out_shape=jax.ShapeDtypeStruct((M, C), F32),
        compiler_params=pltpu.CompilerParams(vmem_limit_bytes=VMEM_LIMIT),
    )(a)


def _adamw(w, g, m, v, emit_grad=False):
    R, C = w.shape
    n_out = 4 if emit_grad else 3
    tr = _row_tile(R, C, 4 + n_out) if R % 8 == 0 else R

    def body(w_ref, g_ref, m_ref, v_ref, d_ref, nm_ref, nv_ref, *g_out):
        g = g_ref[...]
        m = ADAM_B1 * m_ref[...] + (1.0 - ADAM_B1) * g
        v = ADAM_B2 * v_ref[...] + (1.0 - ADAM_B2) * (g * g)
        m_hat = m / (1.0 - ADAM_B1 ** ADAM_STEP)
        v_hat = v / (1.0 - ADAM_B2 ** ADAM_STEP)
        d_ref[...] = -ADAM_LR * (m_hat / (jnp.sqrt(v_hat) + ADAM_EPS) + ADAM_WD * w_ref[...])
        nm_ref[...] = m
        nv_ref[...] = v
        if emit_grad:
            g_out[0][...] = g

    spec = pl.BlockSpec((tr, C), lambda i: (i, 0))
    return _call(body, "adamw", (R // tr,), [spec] * 4, [spec] * n_out, [jax.ShapeDtypeStruct((R, C), F32)] * n_out,
                 [], (w, g, m, v))[0]


def kernel(x, c, ada_w, ada_b, norm_ffn1_g, ffn1_w_gu, ffn1_w_down, norm_mix_g, mix_w_in, sgu_ln_g, sgu_ln_b, sgu_w_s, sgu_b, conv_w, out_norm_g, mix_w_out, norm_ffn2_g, ffn2_w_gu, ffn2_w_down, final_norm_g, loss_target, m_ada_w, m_ada_b, m_norm_ffn1_g, m_ffn1_w_gu, m_ffn1_w_down, m_norm_mix_g, m_mix_w_in, m_sgu_ln_g, m_sgu_ln_b, m_sgu_w_s, m_sgu_b, m_conv_w, m_out_norm_g, m_mix_w_out, m_norm_ffn2_g, m_ffn2_w_gu, m_ffn2_w_down, m_final_norm_g, v_ada_w, v_ada_b, v_norm_ffn1_g, v_ffn1_w_gu, v_ffn1_w_down, v_norm_mix_g, v_mix_w_in, v_sgu_ln_g, v_sgu_ln_b, v_sgu_w_s, v_sgu_b, v_conv_w, v_out_norm_g, v_mix_w_out, v_norm_ffn2_g, v_ffn2_w_gu, v_ffn2_w_down, v_final_norm_g):
    weights = dict(ada_w=ada_w, ada_b=ada_b, norm_ffn1_g=norm_ffn1_g, ffn1_w_gu=ffn1_w_gu, ffn1_w_down=ffn1_w_down,
                   norm_mix_g=norm_mix_g, mix_w_in=mix_w_in, sgu_ln_g=sgu_ln_g, sgu_ln_b=sgu_ln_b, sgu_w_s=sgu_w_s,
                   sgu_b=sgu_b, conv_w=conv_w, out_norm_g=out_norm_g, mix_w_out=mix_w_out, norm_ffn2_g=norm_ffn2_g,
                   ffn2_w_gu=ffn2_w_gu, ffn2_w_down=ffn2_w_down, final_norm_g=final_norm_g)
    m_in = dict(ada_w=m_ada_w, ada_b=m_ada_b, norm_ffn1_g=m_norm_ffn1_g, ffn1_w_gu=m_ffn1_w_gu,
                ffn1_w_down=m_ffn1_w_down, norm_mix_g=m_norm_mix_g, mix_w_in=m_mix_w_in, sgu_ln_g=m_sgu_ln_g,
                sgu_ln_b=m_sgu_ln_b, sgu_w_s=m_sgu_w_s, sgu_b=m_sgu_b, conv_w=m_conv_w, out_norm_g=m_out_norm_g,
                mix_w_out=m_mix_w_out, norm_ffn2_g=m_norm_ffn2_g, ffn2_w_gu=m_ffn2_w_gu, ffn2_w_down=m_ffn2_w_down,
                final_norm_g=m_final_norm_g)
    v_in = dict(ada_w=v_ada_w, ada_b=v_ada_b, norm_ffn1_g=v_norm_ffn1_g, ffn1_w_gu=v_ffn1_w_gu,
                ffn1_w_down=v_ffn1_w_down, norm_mix_g=v_norm_mix_g, mix_w_in=v_mix_w_in, sgu_ln_g=v_sgu_ln_g,
                sgu_ln_b=v_sgu_ln_b, sgu_w_s=v_sgu_w_s, sgu_b=v_sgu_b, conv_w=v_conv_w, out_norm_g=v_out_norm_g,
                mix_w_out=v_mix_w_out, norm_ffn2_g=v_norm_ffn2_g, ffn2_w_gu=v_ffn2_w_gu, ffn2_w_down=v_ffn2_w_down,
                final_norm_g=v_final_norm_g)

    B, S, D = x.shape
    T = B * S
    L = ada_w.shape[0]
    F = ffn1_w_down.shape[1] * N_CHIP
    P = mix_w_in.shape[2] * N_CHIP
    DA = D // 2
    DB = D - DA
    HD = DA // N_HEADS
    SA = ada_w.shape[2]
    n_all = B * N_DEV
    mx, my, mc = _position()
    chip = 2 * mx + my
    dev = 2 * chip + mc
    core = jnp.reshape(mc, (1,)).astype(jnp.int32)

    big = ["ffn1_w_gu", "ffn1_w_down", "mix_w_in", "mix_w_out", "ffn2_w_gu", "ffn2_w_down"]
    col_sharded = dict(ffn1_w_gu=True, ffn1_w_down=False, mix_w_in=True, mix_w_out=False,
                       ffn2_w_gu=True, ffn2_w_down=False)
    shards = {k: weights[k].astype(BF16) for k in big}
    gather = lambda l, *names: _gather_comm([(shards[k], l, col_sharded[k]) for k in names])
    full = [dict() for _ in range(L)]

    def arrived(l, names, res):
        full[l].update(zip(names, res))

    n_cw = L * conv_w.shape[1]
    cw_block = jnp.pad(conv_w.reshape(n_cw, conv_w.shape[2]), ((0, 8 - n_cw), (0, 0)))
    c_all, cw_all = _comm_call(_merge(_all_gather_comm(c.reshape(8, B * D // 8)), _all_gather_comm(cw_block)),
                               "gather_c")
    c_all = c_all.reshape(n_all, D)
    cw_all = cw_all.reshape(N_CHIP, 2, 8, conv_w.shape[2])[:, 0, :n_cw]
    conv_full = jnp.transpose(cw_all.reshape(N_CHIP, L, conv_w.shape[1], conv_w.shape[2]), (1, 2, 0, 3))
    conv_full = conv_full.reshape(L, conv_w.shape[1], DB)
    ada_b_mine = lax.dynamic_slice_in_dim(ada_b, chip * SA, SA, axis=1).reshape(L, 1, SA)
    c_act, ada_part = _ada_fwd(c_all, ada_w, ada_b_mine)
    ada_all, first_w = _comm_call(_merge(_all_gather_comm(ada_part.reshape(L * n_all, SA)), gather(0, big[0])),
                                  "gather_first")
    arrived(0, big[:1], [first_w])
    ada_all = ada_all.reshape(N_CHIP, 2, L, n_all, SA)[:, 0]
    ada_all = jnp.transpose(ada_all, (1, 2, 0, 3)).reshape(L, n_all, N_CHIP * SA)
    ada = lax.dynamic_slice_in_dim(ada_all, dev * B, B, axis=1).reshape(L, B, N_MOD, 1, D)
    mods = [[ada[l, :, j] for j in range(N_MOD)] for l in range(L)]

    x0 = x.reshape(T, D)
    gains = lambda name, l: weights[name][l].reshape(1, D)
    hmask = jnp.repeat(jnp.eye(N_HEADS, dtype=F32), HD, axis=0)
    pmat = (jnp.repeat(hmask, HD, axis=1) / HD).astype(BF16)

    def mix_consts(l):
        lng = jnp.tile(sgu_ln_g[l], N_HEADS).reshape(1, DA)
        lnb = jnp.tile(sgu_ln_b[l], N_HEADS).reshape(1, DA)
        wst = sgu_w_s[l].reshape(N_HEADS * CHUNK, CHUNK)
        wstt = jnp.swapaxes(sgu_w_s[l], 1, 2).reshape(N_HEADS * CHUNK, CHUNK)
        bias = jnp.repeat(jnp.transpose(sgu_b[l]), HD, axis=1)
        return lng, lnb, wst, wstt, bias

    def fetch(fn, *args, bring=()):
        bring = [(l, k) for l, k in bring if l < L]
        comm = _gather_comm([(shards[k], l, col_sharded[k]) for l, k in bring]) if bring else None
        res, got = fn(*args, comm)
        for (l, k), a in zip(bring, got):
            full[l][k] = a
        return res

    saved = []
    xc = x0
    for l in range(L):
        sh1, sc1, g1, sh2, sc2, g2, sh3, sc3, g3 = mods[l]
        lng, lnb, wst, wstt, bias = mix_consts(l)
        w = full[l]
        own = l == 0
        gu1, a1 = fetch(_ffn_up, xc, gains("norm_ffn1_g", l), sh1, sc1, w["ffn1_w_gu"],
                        bring=[(l, "ffn1_w_down"), (l, "mix_w_in"), (l, "mix_w_out")] if own else [(l, "ffn2_w_gu")])
        xa, f1 = fetch(_ffn_down, a1, xc, g1, w["ffn1_w_down"], bring=[(l, "ffn2_w_down")])
        proj, h2 = fetch(_mixin_fwd, xa, gains("norm_mix_g", l), sh2, sc2, w["mix_w_in"], bring=[(l + 1, "mix_w_in")])
        xb, yn = fetch(_mix_core_fwd, proj, xa, g2, w["mix_w_out"], lng, lnb, wst, bias, pmat, conv_full[l],
                       gains("out_norm_g", l), bring=[(l, "ffn2_w_gu")] if own else [])
        gu2, a2 = fetch(_ffn_up, xb, gains("norm_ffn2_g", l), sh3, sc3, w["ffn2_w_gu"],
                        bring=[(l + 1, "ffn1_w_gu"), (l + 1, "mix_w_out")])
        xd, f2 = fetch(_ffn_down, a2, xb, g3, w["ffn2_w_down"], bring=[(l + 1, "ffn1_w_down")])
        saved.append(dict(x0=xc, xa=xa, xb=xb, gu1=gu1, a1=a1, f1=f1, proj=proj, h2=h2, yn=yn,
                          gu2=gu2, a2=a2, f2=f2))
        xc = xd

    dx, loss_block, d_final = _loss_head(xc, loss_target.reshape(T, D), final_norm_g.reshape(1, D))
    loss = lax.psum(loss_block[0, 0], ("x", "y", "c"))

    reduced = dict.fromkeys(big)

    def halves(name, g):
        if g.ndim == 4:
            return g
        return g.reshape(N_CHIP, 2, weights[name].shape[1] // 2, g.shape[-1])

    class Reduction:
        def __init__(self, l, name, g):
            self.l, self.name, self.g, self.stage = l, name, halves(name, g), 0
            self.ici_bytes = 3 * (g.size // 8) * 2

        def step(self):
            self.stage += 1
            if self.stage == 1:
                return _sibling_half_comm([self.g])
            if self.stage == 2:
                return _scatter_comm([_pair_sum(self.g, self.got[0], core)])
            if self.stage == 3:
                reduced[self.name] = _chip_sum(self.got[0], core, self.l, L, reduced[self.name])
                return _share_comm([reduced[self.name]], self.l)
            reduced[self.name] = self.got[0]
            return None

    active, extra, gathered = [], [], {}

    def carry(fn, *args, us=None):
        left = None if us is None else us * SCATTER_BYTES_PER_US
        riders = []
        for r in active:
            if r.stage == 1 and left is not None:
                if r.ici_bytes > left * SCATTER_OVERSHOOT:
                    continue
                left -= r.ici_bytes
            riders.append(r)
        comms = [r.step() for r in riders] + [cm for cm, _ in extra]
        takers = [functools.partial(setattr, r, "got") for r in riders] + [cb for _, cb in extra]
        extra.clear()
        if fn is None:
            res, got = None, (_comm_call(_merge(*comms), "reduce_alone") if comms else [])
        else:
            res, got = fn(*args, comm=_merge(*comms))
        at = 0
        for cm, take in zip(comms, takers):
            take(got[at:at + len(cm.out_shape)])
            at += len(cm.out_shape)
        for r in riders:
            if r.stage == 3:
                r.step()
                active.remove(r)
        return res

    def reduce_later(l, name, g):
        active.append(Reduction(l, name, g))

    small = [None] * L
    d_ada = [None] * L
    for l in reversed(range(L)):
        sh1, sc1, g1, sh2, sc2, g2, sh3, sc3, g3 = mods[l]
        lng, lnb, wst, wstt, bias = mix_consts(l)
        s = saved[l]
        w = full[l]
        dgu, = carry(_ffn_bwd_down, dx, s["gu2"], g3, w["ffn2_w_down"], us=100)
        dx, h3, df, dsc3, dsh3, dgain3, dg3 = carry(_ffn_bwd_up, dx, s["xb"], dgu, s["f2"], gains("norm_ffn2_g", l),
                                                    sh3, sc3, g3, w["ffn2_w_gu"], us=120)
        reduce_later(l, "ffn2_w_gu", carry(_wgrad, h3, dgu, D, 2 * F // N_CHIP, True, "wgrad_gu", us=110)[0])
        reduce_later(l, "ffn2_w_down", carry(_wgrad, s["a2"], df[None], F // 2, D, False, "wgrad_down", us=50)[0])
        dproj, d_o, dg2, dog, dwst, dbias, dlng, dlnb, dconvw = carry(
            _mix_core_bwd, s["proj"], dx, g2, w["mix_w_out"], lng, lnb, wst, wstt, bias, pmat, conv_full[l],
            gains("out_norm_g", l), us=200)
        mix_grads = [
            lambda: reduce_later(l, "mix_w_out", carry(_wgrad, s["yn"], d_o[None], D, D, False, "wgrad_out", us=20)[0]),
            lambda: reduce_later(l, "mix_w_in", carry(_wgrad, s["h2"], dproj[None], D, P // N_CHIP, True, "wgrad_in",
                                                      us=65)[0])]
        last = l == 0
        if not last:
            mix_grads[0]()
        dx, dsc2, dsh2, dgain2 = carry(_mixin_bwd, dx, s["xa"], dproj, gains("norm_mix_g", l), sc2, w["mix_w_in"], us=60)
        if not last:
            mix_grads[1]()
        dgu, = carry(_ffn_bwd_down, dx, s["gu1"], g1, w["ffn1_w_down"], us=100)
        dx, h1, df, dsc1, dsh1, dgain1, dg1 = carry(_ffn_bwd_up, dx, s["x0"], dgu, s["f1"], gains("norm_ffn1_g", l),
                                                    sh1, sc1, g1, w["ffn1_w_gu"], us=120)
        d_ada[l] = jnp.concatenate([dsh1, dsc1, dg1, dsh2, dsc2, dg2, dsh3, dsc3, dg3], axis=1).reshape(B, N_MOD * D)
        small[l] = [dgain1, dgain2, dgain3, dog, dlng, dlnb, dwst, dbias[:, ::HD], dconvw]
        if last:
            flat = [a.reshape(-1, 128) for ll in range(L) for a in small[ll]] + [d_final.reshape(-1, 128)]
            pad = (-sum(a.shape[0] for a in flat)) % 8
            packed = jnp.concatenate(flat + [jnp.zeros((pad, 128), F32)], axis=0)
            extra.append((_all_gather_comm(jnp.stack(d_ada).reshape(L * B, N_MOD * D)),
                          lambda got: gathered.update(d_ada=got[0])))
            extra.append((_all_gather_comm(packed), lambda got: gathered.update(small=got[0])))
        reduce_later(l, "ffn1_w_gu", carry(_wgrad, h1, dgu, D, 2 * F // N_CHIP, True, "wgrad_gu", us=110)[0])
        reduce_later(l, "ffn1_w_down", carry(_wgrad, s["a1"], df[None], F // 2, D, False, "wgrad_down", us=50)[0])
        if last:
            mix_grads[1]()
            mix_grads[0]()
    grad_x = dx.reshape(B, S, D)

    def finished(name):
        while any(r.name == name for r in active):
            carry(None)
        return reduced[name].reshape(weights[name].shape)

    grads = {}
    d_ada_all = jnp.transpose(gathered["d_ada"].reshape(N_DEV, L, B, N_MOD * D), (1, 0, 2, 3))
    d_ada_all = d_ada_all.reshape(L, n_all, N_MOD * D)
    grads["ada_b"] = _colsum(d_ada_all).reshape(L, N_MOD * D)
    d_ada_mine = lax.dynamic_slice_in_dim(d_ada_all, chip * SA, SA, axis=2).astype(BF16)
    grads["ada_w"] = carry(_ada_bwd, c_act, d_ada_mine, us=8)[0]

    total = _sum_blocks(gathered["small"].reshape(-1, 128), N_DEV)
    pieces, at = [], 0
    for a in flat:
        pieces.append(total[at:at + a.shape[0]])
        at += a.shape[0]
    per_layer = len(small[0])
    stack = lambda j, shape: jnp.stack([pieces[l * per_layer + j].reshape(shape) for l in range(L)])
    grads["norm_ffn1_g"] = stack(0, (D,))
    grads["norm_mix_g"] = stack(1, (D,))
    grads["norm_ffn2_g"] = stack(2, (D,))
    grads["out_norm_g"] = stack(3, (D,))
    grads["sgu_ln_g"] = stack(4, (N_HEADS, HD)).sum(axis=1)
    grads["sgu_ln_b"] = stack(5, (N_HEADS, HD)).sum(axis=1)
    grads["sgu_w_s"] = stack(6, (N_HEADS, CHUNK, CHUNK))
    grads["sgu_b"] = jnp.swapaxes(stack(7, (CHUNK, N_HEADS)), 1, 2)
    g_conv = stack(8, (8, DB))[:, :conv_w.shape[1]]
    grads["conv_w"] = lax.dynamic_slice_in_dim(g_conv, chip * conv_w.shape[2], conv_w.shape[2], axis=2)
    grads["final_norm_g"] = pieces[-1].reshape(D)

    names = list(weights)
    delta, new_m, new_v = {}, {}, {}
    for k in big:
        grads[k] = finished(k)
    for k in names:
        wk = weights[k]
        view = (1, wk.shape[0]) if wk.ndim == 1 else (-1, wk.shape[-1])
        d, nm, nv, *g_again = _adamw(wk.reshape(view), grads[k].reshape(view), m_in[k].reshape(view),
                                     v_in[k].reshape(view), emit_grad=k in big)
        delta[k], new_m[k], new_v[k] = d.reshape(wk.shape), nm.reshape(wk.shape), nv.reshape(wk.shape)
        if g_again:
            grads[k] = g_again[0].reshape(wk.shape)

    return (loss, grad_x, *[grads[k] for k in names], *[delta[k] for k in names],
            *[new_m[k] for k in names], *[new_v[k] for k in names])
```

```python
import functools
import math

import jax
import jax.numpy as jnp
from jax import lax
from jax.experimental import pallas as pl
from jax.experimental.pallas import tpu as pltpu

F32 = jnp.float32
BF16 = jnp.bfloat16
MESH = pl.DeviceIdType.MESH

N_HEADS = 8
CHUNK = 128
N_MOD = 9
EPS = 1e-6
N_DEV = 8
N_CHIP = 4

ADAM_LR = 0.001
ADAM_B1 = 0.9
ADAM_B2 = 0.999
ADAM_EPS = 1e-08
ADAM_WD = 0.01
ADAM_STEP = 10

TOKEN_TILE = 512
FF_SLAB = 768
MIX_TILE = 256
WGRAD_TOKENS = 2048
VMEM_LIMIT = 56 * 1024 * 1024

SCATTER_BYTES_PER_US = 68_000
SCATTER_OVERSHOOT = 1.25

ANY = pl.BlockSpec(memory_space=pl.ANY)


def _tile(pref, n):
    t = min(pref, n)
    assert n % t == 0, (pref, n)
    return t


def _slabs(n, width):
    return [slice(c0, min(c0 + width, n)) for c0 in range(0, n, width)]


def _dot(a, b):
    return jnp.dot(a, b, preferred_element_type=F32)


def _dot_nt(a, b):
    return lax.dot_general(a, b, (((1,), (1,)), ((), ())), preferred_element_type=F32)


def _dot_tn(a, b):
    return lax.dot_general(a, b, (((0,), (0,)), ((), ())), preferred_element_type=F32)


def _sigmoid(x):
    return 1.0 / (1.0 + jnp.exp(-x))


def _sigmoid_fast(x):
    return pl.reciprocal(1.0 + jnp.exp(-x), approx=True)


def _rms(x):
    r = lax.rsqrt(jnp.mean(x * x, axis=-1, keepdims=True) + EPS)
    return x * r, r


def _norm_mod_bwd(x, dh, gain, sc):
    xh, r = _rms(x)
    dsc = jnp.sum(dh * (xh * gain), axis=0, keepdims=True)
    dsh = jnp.sum(dh, axis=0, keepdims=True)
    dn = dh * (1.0 + sc)
    dgain = jnp.sum(dn * xh, axis=0, keepdims=True)
    dy = dn * gain
    dx = r * (dy - xh * jnp.mean(dy * xh, axis=-1, keepdims=True))
    return dx, dsc, dsh, dgain


def _acc(ref, first, val):
    @pl.when(first)
    def _():
        ref[...] = val

    @pl.when(jnp.logical_not(first))
    def _():
        ref[...] += val


class _Comm:
    def __init__(self, args, out_shape, scratch, phases, aliases=None):
        self.args, self.out_shape, self.scratch = list(args), list(out_shape), list(scratch)
        self.phases, self.aliases = phases, dict(aliases or {})


def _merge(*comms):
    comms = [c for c in comms if c is not None]
    if len(comms) <= 1:
        return comms[0] if comms else None
    args = [a for c in comms for a in c.args]
    out_shape = [o for c in comms for o in c.out_shape]
    scratch = [s for c in comms for s in c.scratch]
    aliases, ai, oi = {}, 0, 0
    for c in comms:
        aliases.update({ai + i: oi + o for i, o in c.aliases.items()})
        ai += len(c.args)
        oi += len(c.out_shape)

    def phases(ins, outs, sems):
        parts, ai, oi, si = [], 0, 0, 0
        for c in comms:
            parts.append(c.phases(ins[ai:ai + len(c.args)], outs[oi:oi + len(c.out_shape)], sems[si:si + len(c.scratch)]))
            ai, oi, si = ai + len(c.args), oi + len(c.out_shape), si + len(c.scratch)

        def run(k):
            def go():
                for p in parts:
                    if p[k] is not None:
                        p[k]()
            return go
        return run(0), run(1), run(2)

    return _Comm(args, out_shape, scratch, phases, aliases)


def _call(body, name, grid, in_specs, out_specs, out_shape, scratch, args, comm=None):
    n_in, n_out, n_scr = len(in_specs), len(out_specs), len(scratch)
    sem = ("arbitrary",) * len(grid)
    params = pltpu.CompilerParams(dimension_semantics=sem, vmem_limit_bytes=VMEM_LIMIT)
    if comm is None:
        res = pl.pallas_call(body, name=name, grid=grid, in_specs=in_specs, out_specs=out_specs, out_shape=out_shape,
                             scratch_shapes=scratch, compiler_params=params)(*args)
        return list(res), []
    m_in, m_out = len(comm.args), len(comm.out_shape)

    def full(*refs):
        c_in, c_min = refs[:n_in], refs[n_in:n_in + m_in]
        o = n_in + m_in
        c_out, c_mout = refs[o:o + n_out], refs[o + n_out:o + n_out + m_out]
        o += n_out + m_out
        c_scr, c_sem = refs[o:o + n_scr], refs[o + n_scr:]
        start, mid, finish = comm.phases(c_min, c_mout, c_sem)
        ids = [pl.program_id(a) for a in range(len(grid))]
        first = functools.reduce(jnp.logical_and, [i == 0 for i in ids])
        last = functools.reduce(jnp.logical_and, [i == g - 1 for i, g in zip(ids, grid)])
        pl.when(first)(start)
        if mid is not None:
            pl.when(last)(mid)
        body(*c_in, *c_out, *c_scr)
        pl.when(last)(finish)

    res = pl.pallas_call(
        full, name=name, grid=grid,
        in_specs=list(in_specs) + [ANY] * m_in,
        out_specs=list(out_specs) + [ANY] * m_out,
        out_shape=list(out_shape) + comm.out_shape,
        scratch_shapes=list(scratch) + comm.scratch,
        input_output_aliases={n_in + i: n_out + o for i, o in comm.aliases.items()},
        compiler_params=params,
    )(*args, *comm.args)
    return list(res[:n_out]), list(res[n_out:])


def _comm_call(comm, name):
    m_in, m_out = len(comm.args), len(comm.out_shape)

    def body(*refs):
        start, mid, finish = comm.phases(refs[:m_in], refs[m_in:m_in + m_out], refs[m_in + m_out:])
        start()
        if mid is not None:
            mid()
        finish()

    res = pl.pallas_call(
        body, name=name, in_specs=[ANY] * m_in, out_specs=[ANY] * m_out, out_shape=comm.out_shape,
        scratch_shapes=comm.scratch, input_output_aliases=comm.aliases,
    )(*comm.args)
    return list(res)


def _position():
    return lax.axis_index("x"), lax.axis_index("y"), lax.axis_index("c")


def _gather_comm(items):
    n = len(items)
    half = [s.shape[1] // 2 for s, _, _ in items]

    def full_shape(i):
        s, _, col = items[i]
        _, R, C = s.shape
        return jax.ShapeDtypeStruct((R, N_CHIP * C) if col else (N_CHIP * R, C), s.dtype)

    def phases(ins, outs, sems):
        send_sems, recv_sems, local_sems = sems
        x, y, c = _position()

        def region(i, chip, h):
            s, _, col = items[i]
            _, R, C = s.shape
            if col:
                return outs[i].at[pl.ds(h * half[i], half[i]), pl.ds(chip * C, C)]
            return outs[i].at[pl.ds(chip * R + h * half[i], half[i]), :]

        def mine(i, h):
            return ins[i].at[items[i][1], pl.ds(h * half[i], half[i]), :]

        def copies(kx, ky, kc):
            k_me = 2 * kx + ky
            sibling = (kx, ky, 1 - kc)
            chips = [(1 - kx, ky), (kx, 1 - ky), (1 - kx, 1 - ky)]
            local, first, passed, arrive_ici, arrive_d2d = [], [], [], [], []

            def remote(src, dst, s, to):
                return pltpu.make_async_remote_copy(src_ref=src, dst_ref=dst, send_sem=send_sems.at[s],
                                                    recv_sem=recv_sems.at[s], device_id=to, device_id_type=MESH)

            for i in range(n):
                for h in range(2):
                    local.append(pltpu.make_async_copy(mine(i, h), region(i, k_me, h), local_sems.at[2 * i + h]))
                for j, (px, py) in enumerate(chips):
                    s = 6 * i + j
                    first.append(remote(mine(i, kc), region(i, k_me, kc), s, (px, py, kc)))
                    got = region(i, 2 * px + py, kc)
                    arrive_ici.append(remote(got, got, s, (px, py, kc)))
                    passed.append(remote(got, got, s + 3, sibling))
                    other = region(i, 2 * px + py, 1 - kc)
                    arrive_d2d.append(remote(other, other, s + 3, sibling))
            return local, first, passed, arrive_ici, arrive_d2d

        def on_each_device(fn):
            def go():
                for kx in range(2):
                    for ky in range(2):
                        for kc in range(2):
                            pl.when((x == kx) & (y == ky) & (c == kc))(functools.partial(fn, *copies(kx, ky, kc)))
            return go

        def start(local, first, passed, arrive_ici, arrive_d2d):
            for cp in local + first:
                cp.start()

        def mid(local, first, passed, arrive_ici, arrive_d2d):
            for a, p in zip(arrive_ici, passed):
                a.wait_recv()
                p.start()

        def finish(local, first, passed, arrive_ici, arrive_d2d):
            for a in arrive_d2d:
                a.wait_recv()
            for cp in first + passed:
                cp.wait_send()
            for cp in local:
                cp.wait()

        return on_each_device(start), on_each_device(mid), on_each_device(finish)

    scratch = [pltpu.SemaphoreType.DMA((6 * n,)), pltpu.SemaphoreType.DMA((6 * n,)), pltpu.SemaphoreType.DMA((2 * n,))]
    return _Comm([s for s, _, _ in items], [full_shape(i) for i in range(n)], scratch, phases)


def _sibling_half_comm(gs):
    n = len(gs)

    def phases(ins, outs, sems):
        send_sems, recv_sems = sems
        x, y, c = _position()

        def copies():
            return [pltpu.make_async_remote_copy(
                src_ref=ins[i].at[:, 1 - c], dst_ref=outs[i], send_sem=send_sems.at[i], recv_sem=recv_sems.at[i],
                device_id=(x, y, 1 - c), device_id_type=MESH) for i in range(n)]

        def start():
            for cp in copies():
                cp.start()

        def finish():
            for cp in copies():
                cp.wait()

        return start, None, finish

    out_shape = [jax.ShapeDtypeStruct(g.shape[:1] + g.shape[2:], g.dtype) for g in gs]
    return _Comm(gs, out_shape, [pltpu.SemaphoreType.DMA((n,)), pltpu.SemaphoreType.DMA((n,))], phases)


def _scatter_comm(ps):
    n = len(ps)

    def phases(ins, outs, sems):
        send_sems, recv_sems, local_sems = sems
        x, y, c = _position()
        k_me = 2 * x + y
        chips = [(1 - x, y), (x, 1 - y), (1 - x, 1 - y)]

        def copies():
            local = [pltpu.make_async_copy(ins[i].at[k_me], outs[i].at[k_me], local_sems.at[i]) for i in range(n)]
            remote = [pltpu.make_async_remote_copy(
                src_ref=ins[i].at[2 * px + py], dst_ref=outs[i].at[k_me],
                send_sem=send_sems.at[3 * i + j], recv_sem=recv_sems.at[3 * i + j],
                device_id=(px, py, c), device_id_type=MESH) for i in range(n) for j, (px, py) in enumerate(chips)]
            return local, remote

        def start():
            local, remote = copies()
            for cp in local + remote:
                cp.start()

        def finish():
            local, remote = copies()
            for cp in remote + local:
                cp.wait()

        return start, None, finish

    scratch = [pltpu.SemaphoreType.DMA((3 * n,)), pltpu.SemaphoreType.DMA((3 * n,)), pltpu.SemaphoreType.DMA((n,))]
    return _Comm(ps, [jax.ShapeDtypeStruct(p.shape, p.dtype) for p in ps], scratch, phases)


def _share_comm(rs, l):
    n = len(rs)

    def phases(ins, outs, sems):
        send_sems, recv_sems = sems
        x, y, c = _position()

        def copy(i, h):
            return pltpu.make_async_remote_copy(
                src_ref=outs[i].at[l, h], dst_ref=outs[i].at[l, h], send_sem=send_sems.at[i], recv_sem=recv_sems.at[i],
                device_id=(x, y, 1 - c), device_id_type=MESH)

        def start():
            for i in range(n):
                copy(i, c).start()

        def finish():
            for i in range(n):
                copy(i, 1 - c).wait_recv()
            for i in range(n):
                copy(i, c).wait_send()

        return start, None, finish

    return _Comm(rs, [jax.ShapeDtypeStruct(r.shape, r.dtype) for r in rs],
                 [pltpu.SemaphoreType.DMA((n,)), pltpu.SemaphoreType.DMA((n,))], phases,
                 aliases={i: i for i in range(n)})


def _all_gather_comm(block):
    def phases(ins, outs, sems):
        send_sems, recv_sems, local_sem = sems
        (src,), (out,) = ins, outs
        x, y, c = _position()
        sibling = (x, y, 1 - c)
        chips = [(1 - x, y), (x, 1 - y), (1 - x, 1 - y)]

        def slot(px, py, pc):
            return out.at[4 * px + 2 * py + pc]

        def copy(k, blk, to, own=False):
            return pltpu.make_async_remote_copy(
                src_ref=src if own else slot(*blk), dst_ref=slot(*blk),
                send_sem=send_sems.at[k], recv_sem=recv_sems.at[k], device_id=to, device_id_type=MESH)

        mine = lambda: pltpu.make_async_copy(src, slot(x, y, c), local_sem.at[0])
        first = lambda: [copy(0, (x, y, c), sibling, True)] + [
            copy(1 + j, (x, y, c), (*chip, c), True) for j, chip in enumerate(chips)]
        passed = lambda: [copy(4 + j, (*chip, c), sibling) for j, chip in enumerate(chips)]

        def start():
            mine().start()
            for cp in first():
                cp.start()

        def mid():
            for j, (chip, p) in enumerate(zip(chips, passed())):
                copy(1 + j, (*chip, c), (x, y, c)).wait_recv()
                p.start()

        def finish():
            copy(0, sibling, (x, y, c)).wait_recv()
            for j, chip in enumerate(chips):
                copy(4 + j, (*chip, 1 - c), (x, y, c)).wait_recv()
            for cp in first() + passed():
                cp.wait_send()
            mine().wait()

        return start, mid, finish

    scratch = [pltpu.SemaphoreType.DMA((7,)), pltpu.SemaphoreType.DMA((7,)), pltpu.SemaphoreType.DMA((1,))]
    return _Comm([block], [jax.ShapeDtypeStruct((N_DEV,) + block.shape, block.dtype)], scratch, phases)


def _ffn_up(x, gain, sh, sc, wgu, comm=None):
    T, D = x.shape
    F = wgu.shape[1] // 2
    B = sh.shape[0]
    tm = _tile(TOKEN_TILE, T // B)
    tps = (T // B) // tm
    slabs = _slabs(F, FF_SLAB)

    def body(x_ref, gain_ref, sh_ref, sc_ref, w_ref, gu_ref, a_ref):
        xh, _ = _rms(x_ref[...])
        h = (xh * gain_ref[...] * (1.0 + sc_ref[0]) + sh_ref[0]).astype(BF16)

        def dots(s):
            return _dot(h, w_ref[:, s]), _dot(h, w_ref[:, slice(F + s.start, F + s.stop)])

        nxt = dots(slabs[0])
        for j, s in enumerate(slabs):
            g, u = nxt
            if j + 1 < len(slabs):
                nxt = dots(slabs[j + 1])
            gu_ref[0, :, s] = g.astype(BF16)
            gu_ref[1, :, s] = u.astype(BF16)
            a_ref[:, s] = (g * _sigmoid(g) * u).astype(BF16)

    seq = lambda i: (i // tps, 0, 0)
    return _call(
        body, "ffn_up", (T // tm,),
        [
            pl.BlockSpec((tm, D), lambda i: (i, 0)),
            pl.BlockSpec((1, D), lambda i: (0, 0)),
            pl.BlockSpec((1, 1, D), seq),
            pl.BlockSpec((1, 1, D), seq),
            pl.BlockSpec((D, 2 * F), lambda i: (0, 0), pipeline_mode=pl.Buffered(1)),
        ],
        [
            pl.BlockSpec((2, tm, F), lambda i: (0, i, 0)),
            pl.BlockSpec((tm, F), lambda i: (i, 0)),
        ],
        [
            jax.ShapeDtypeStruct((2, T, F), BF16),
            jax.ShapeDtypeStruct((T, F), BF16),
        ],
        [],
        (x, gain, sh, sc, wgu), comm)


def _ffn_down(a, x, gate, wd, comm=None):
    T, F = a.shape
    D = x.shape[1]
    B = gate.shape[0]
    tm = _tile(TOKEN_TILE, T // B)
    tps = (T // B) // tm

    def body(a_ref, x_ref, gate_ref, wd_ref, xo_ref, f_ref):
        f = _dot(a_ref[...], wd_ref[...])
        f_ref[...] = f.astype(BF16)
        xo_ref[...] = x_ref[...] + 0.5 * gate_ref[0] * f

    return _call(
        body, "ffn_down", (T // tm,),
        [
            pl.BlockSpec((tm, F), lambda i: (i, 0)),
            pl.BlockSpec((tm, D), lambda i: (i, 0)),
            pl.BlockSpec((1, 1, D), lambda i: (i // tps, 0, 0)),
            pl.BlockSpec((F, D), lambda i: (0, 0)),
        ],
        [pl.BlockSpec((tm, D), lambda i: (i, 0)), pl.BlockSpec((tm, D), lambda i: (i, 0))],
        [jax.ShapeDtypeStruct((T, D), F32), jax.ShapeDtypeStruct((T, D), BF16)],
        [],
        (a, x, gate, wd), comm)


def _ffn_bwd_down(dxo, gu, gate, wd, comm=None):
    T, D = dxo.shape
    F = wd.shape[0]
    B = gate.shape[0]
    tm = _tile(TOKEN_TILE, T // B)
    tps = (T // B) // tm
    slabs = _slabs(F, FF_SLAB)

    def body(dxo_ref, gu_ref, gate_ref, wd_ref, dgu_ref):
        df = (0.5 * gate_ref[0] * dxo_ref[...]).astype(BF16)
        nxt = _dot_nt(df, wd_ref[slabs[0], :])
        for j, s in enumerate(slabs):
            da = nxt
            if j + 1 < len(slabs):
                nxt = _dot_nt(df, wd_ref[slabs[j + 1], :])
            g = gu_ref[0, :, s].astype(F32)
            sg = _sigmoid_fast(g)
            t = g * sg
            dgu_ref[1, :, s] = (da * t).astype(BF16)
            dgu_ref[0, :, s] = (da * gu_ref[1, :, s].astype(F32) * (sg + t - t * sg)).astype(BF16)

    return _call(
        body, "ffn_bwd_down", (T // tm,),
        [
            pl.BlockSpec((tm, D), lambda i: (i, 0)),
            pl.BlockSpec((2, tm, F), lambda i: (0, i, 0)),
            pl.BlockSpec((1, 1, D), lambda i: (i // tps, 0, 0)),
            pl.BlockSpec((F, D), lambda i: (0, 0), pipeline_mode=pl.Buffered(1)),
        ],
        [pl.BlockSpec((2, tm, F), lambda i: (0, i, 0))],
        [jax.ShapeDtypeStruct((2, T, F), BF16)],
        [],
        (dxo, gu, gate, wd), comm)


def _ffn_bwd_up(dxo, x, dgu, f, gain, sh, sc, gate, wgu, comm=None):
    T, D = x.shape
    F = wgu.shape[1] // 2
    B = sc.shape[0]
    tm = _tile(TOKEN_TILE, T // B)
    tps = (T // B) // tm

    def body(dxo_ref, x_ref, dgu_ref, f_ref, gain_ref, sh_ref, sc_ref, gate_ref, w_ref,
             dx_ref, h_ref, df_ref, dsc_ref, dsh_ref, dgain_ref, dgate_ref):
        i = pl.program_id(0)
        first_of_seq = (i % tps) == 0
        gain = gain_ref[...]
        sc = sc_ref[0]
        halves = _slabs(tm, tm // 2)

        def dots(r):
            return _dot_nt(dgu_ref[0, r, :], w_ref[:, 0:F]) + _dot_nt(dgu_ref[1, r, :], w_ref[:, F:])

        nxt = dots(halves[0])
        sums = None
        for j, r in enumerate(halves):
            dh = nxt
            if j + 1 < len(halves):
                nxt = dots(halves[j + 1])
            dxo = dxo_ref[r, :]
            x = x_ref[r, :]
            dx, dsc, dsh, dgain = _norm_mod_bwd(x, dh, gain, sc)
            dx_ref[r, :] = dxo + dx
            h_ref[r, :] = (_rms(x)[0] * gain * (1.0 + sc) + sh_ref[0]).astype(BF16)
            df_ref[r, :] = (0.5 * gate_ref[0] * dxo).astype(BF16)
            part = (dsc, dsh, dgain, 0.5 * jnp.sum(dxo * f_ref[r, :].astype(F32), axis=0, keepdims=True))
            sums = part if sums is None else tuple(a + b for a, b in zip(sums, part))
        _acc(dsc_ref.at[0], first_of_seq, sums[0])
        _acc(dsh_ref.at[0], first_of_seq, sums[1])
        _acc(dgain_ref, i == 0, sums[2])
        _acc(dgate_ref.at[0], first_of_seq, sums[3])

    seq = lambda i: (i // tps, 0, 0)
    row = lambda i: (i, 0)
    return _call(
        body, "ffn_bwd_up", (T // tm,),
        [
            pl.BlockSpec((tm, D), row),
            pl.BlockSpec((tm, D), row),
            pl.BlockSpec((2, tm, F), lambda i: (0, i, 0)),
            pl.BlockSpec((tm, D), row),
            pl.BlockSpec((1, D), lambda i: (0, 0)),
            pl.BlockSpec((1, 1, D), seq),
            pl.BlockSpec((1, 1, D), seq),
            pl.BlockSpec((1, 1, D), seq),
            pl.BlockSpec((D, 2 * F), lambda i: (0, 0), pipeline_mode=pl.Buffered(1)),
        ],
        [
            pl.BlockSpec((tm, D), row),
            pl.BlockSpec((tm, D), row),
            pl.BlockSpec((tm, D), row),
            pl.BlockSpec((1, 1, D), seq),
            pl.BlockSpec((1, 1, D), seq),
            pl.BlockSpec((1, D), lambda i: (0, 0)),
            pl.BlockSpec((1, 1, D), seq),
        ],
        [
            jax.ShapeDtypeStruct((T, D), F32),
            jax.ShapeDtypeStruct((T, D), BF16),
            jax.ShapeDtypeStruct((T, D), BF16),
            jax.ShapeDtypeStruct((B, 1, D), F32),
            jax.ShapeDtypeStruct((B, 1, D), F32),
            jax.ShapeDtypeStruct((1, D), F32),
            jax.ShapeDtypeStruct((B, 1, D), F32),
        ],
        [],
        (dxo, x, dgu, f, gain, sh, sc, gate, wgu), comm)


def _wgrad(a, b, tmm, tn, col_major, name, tokens=WGRAD_TOKENS, comm=None):
    T, M = a.shape
    nb, _, Nb = b.shape
    N = nb * Nb
    tk = _tile(tokens, T)
    npb = Nb // tn
    assert M % tmm == 0 and Nb % tn == 0
    if col_major:
        assert tmm == M
        shape = (N // tn, 2, M // 2, tn)
        out_spec = pl.BlockSpec((None, 2, M // 2, tn), lambda i, j, t: (j, 0, 0, 0))
    else:
        shape = (M // tmm, tmm, N)
        out_spec = pl.BlockSpec((None, tmm, tn), lambda i, j, t: (i, 0, j))

    def body(a_ref, b_ref, o_ref):
        t = pl.program_id(2)
        res = _dot_tn(a_ref[...], b_ref[...])
        if col_major:
            _acc(o_ref.at[0], t == 0, res[:M // 2])
            _acc(o_ref.at[1], t == 0, res[M // 2:])
        else:
            _acc(o_ref, t == 0, res)

    return _call(
        body, name, (M // tmm, N // tn, T // tk),
        [
            pl.BlockSpec((tk, tmm), lambda i, j, t: (t, i)),
            pl.BlockSpec((None, tk, tn), lambda i, j, t: (j // npb, t, j % npb)),
        ],
        [out_spec], [jax.ShapeDtypeStruct(shape, F32)], [],
        (a, b), comm)


def _mixin_fwd(x, gain, sh, sc, win, comm=None):
    T, D = x.shape
    P = win.shape[1]
    B = sh.shape[0]
    tm = _tile(TOKEN_TILE, T // B)
    tps = (T // B) // tm

    def body(x_ref, gain_ref, sh_ref, sc_ref, w_ref, proj_ref, h_ref):
        xh, _ = _rms(x_ref[...])
        h = (xh * gain_ref[...] * (1.0 + sc_ref[0]) + sh_ref[0]).astype(BF16)
        h_ref[...] = h
        proj_ref[...] = _dot(h, w_ref[...])

    seq = lambda i: (i // tps, 0, 0)
    return _call(
        body, "mixin_fwd", (T // tm,),
        [
            pl.BlockSpec((tm, D), lambda i: (i, 0)),
            pl.BlockSpec((1, D), lambda i: (0, 0)),
            pl.BlockSpec((1, 1, D), seq),
            pl.BlockSpec((1, 1, D), seq),
            pl.BlockSpec((D, P), lambda i: (0, 0)),
        ],
        [pl.BlockSpec((tm, P), lambda i: (i, 0)), pl.BlockSpec((tm, D), lambda i: (i, 0))],
        [jax.ShapeDtypeStruct((T, P), F32), jax.ShapeDtypeStruct((T, D), BF16)],
        [],
        (x, gain, sh, sc, win), comm)


def _mixin_bwd(dxo, x, dproj, gain, sc, win, comm=None):
    T, D = x.shape
    P = win.shape[1]
    B = sc.shape[0]
    tm = _tile(TOKEN_TILE, T // B)
    tps = (T // B) // tm

    def body(dxo_ref, x_ref, dp_ref, gain_ref, sc_ref, w_ref, dx_ref, dsc_ref, dsh_ref, dgain_ref):
        i = pl.program_id(0)
        first_of_seq = (i % tps) == 0
        halves = _slabs(tm, tm // 2)
        nxt = _dot_nt(dp_ref[halves[0], :], w_ref[...])
        sums = None
        for j, r in enumerate(halves):
            dh = nxt
            if j + 1 < len(halves):
                nxt = _dot_nt(dp_ref[halves[j + 1], :], w_ref[...])
            part = _norm_mod_bwd(x_ref[r, :], dh, gain_ref[...], sc_ref[0])
            dx_ref[r, :] = dxo_ref[r, :] + part[0]
            sums = part[1:] if sums is None else tuple(a + b for a, b in zip(sums, part[1:]))
        _acc(dsc_ref.at[0], first_of_seq, sums[0])
        _acc(dsh_ref.at[0], first_of_seq, sums[1])
        _acc(dgain_ref, i == 0, sums[2])

    seq = lambda i: (i // tps, 0, 0)
    row = lambda i: (i, 0)
    return _call(
        body, "mixin_bwd", (T // tm,),
        [
            pl.BlockSpec((tm, D), row),
            pl.BlockSpec((tm, D), row),
            pl.BlockSpec((tm, P), row),
            pl.BlockSpec((1, D), lambda i: (0, 0)),
            pl.BlockSpec((1, 1, D), seq),
            pl.BlockSpec((D, P), lambda i: (0, 0)),
        ],
        [
            pl.BlockSpec((tm, D), row),
            pl.BlockSpec((1, 1, D), seq),
            pl.BlockSpec((1, 1, D), seq),
            pl.BlockSpec((1, D), lambda i: (0, 0)),
        ],
        [
            jax.ShapeDtypeStruct((T, D), F32),
            jax.ShapeDtypeStruct((B, 1, D), F32),
            jax.ShapeDtypeStruct((B, 1, D), F32),
            jax.ShapeDtypeStruct((1, D), F32),
        ],
        [],
        (dxo, x, dproj, gain, sc, win), comm)


def _head_mean(z, pmat):
    hi = z.astype(BF16)
    lo = (z - hi.astype(F32)).astype(BF16)
    return _dot(hi, pmat) + _dot(lo, pmat)


def _gelu_parts(x):
    cdf = 0.5 * (1.0 + lax.erf(x * (1.0 / math.sqrt(2.0))))
    return x * cdf, cdf


def _gelu_grad(x, cdf):
    return cdf + x * jnp.exp(-0.5 * x * x) * (1.0 / math.sqrt(2.0 * math.pi))


def _head_masks(da):
    hd = da // N_HEADS
    col = lax.broadcasted_iota(jnp.int32, (1, da), 1)
    return [(col >= h * hd) & (col < (h + 1) * hd) for h in range(N_HEADS)]


def _select_heads(res, masks):
    out = res[0:CHUNK]
    for h in range(1, N_HEADS):
        out = jnp.where(masks[h], res[h * CHUNK:(h + 1) * CHUNK], out)
    return out


def _causal_stack(w, transposed):
    r = lax.broadcasted_iota(jnp.int32, w.shape, 0) % CHUNK
    c = lax.broadcasted_iota(jnp.int32, w.shape, 1)
    keep = (c >= r) if transposed else (c <= r)
    return jnp.where(keep, w, 0.0)


def _mix_core_forward(proj, zprev, prm, da, db):
    n = proj.shape[0]
    ua = proj[:, 0:da]
    va = proj[:, da:2 * da]
    bg = proj[:, 2 * da:2 * da + db]
    cg = proj[:, 2 * da + db:2 * da + 2 * db]
    xb = proj[:, 2 * da + 2 * db:]
    ug, ucdf = _gelu_parts(ua)
    vg, vcdf = _gelu_parts(va)
    zc = vg - _head_mean(vg, prm["pmat"])
    rs = lax.rsqrt(_head_mean(zc * zc, prm["pmat"]) + EPS)
    vhat = zc * rs
    vln = (vhat * prm["lng"] + prm["lnb"]).astype(BF16)
    wst = _causal_stack(prm["wst"], False).astype(BF16)
    masks = _head_masks(da)
    mixed = []
    for j in range(n // CHUNK):
        res = _dot(wst, vln[j * CHUNK:(j + 1) * CHUNK])
        mixed.append(_select_heads(res, masks) + prm["bias"])
    mixed = mixed[0] if len(mixed) == 1 else jnp.concatenate(mixed, axis=0)
    ya = ug * mixed
    z = cg * xb
    row = lax.broadcasted_iota(jnp.int32, z.shape, 0)
    z1 = jnp.where(row == 0, zprev[7:8], pltpu.roll(z, 1, 0))
    z2 = jnp.where(row == 0, zprev[6:7], jnp.where(row == 1, zprev[7:8], pltpu.roll(z, 2, 0)))
    cw = prm["convw"]
    conv = z2 * cw[0:1] + z1 * cw[1:2] + z * cw[2:3]
    yb = bg * conv
    yah, ra = _rms(ya)
    ybh, rb = _rms(yb)
    return dict(ua=ua, va=va, bg=bg, cg=cg, xb=xb, ug=ug, ucdf=ucdf, vcdf=vcdf, rs=rs, vhat=vhat, vln=vln,
                mixed=mixed, z=z, z1=z1, z2=z2, conv=conv, yah=yah, ra=ra, ybh=ybh, rb=rb, masks=masks)


def _mix_params(lng_ref, lnb_ref, wst_ref, bias_ref, pmat_ref, convw_ref):
    return dict(lng=lng_ref[...], lnb=lnb_ref[...], wst=wst_ref[...], bias=bias_ref[...],
                pmat=pmat_ref[...], convw=convw_ref[...])


def _mix_core_fwd(proj, x, gate, wout, lng, lnb, wst, bias, pmat, convw, og, comm=None):
    T, P = proj.shape
    D = x.shape[1]
    B = gate.shape[0]
    da = lng.shape[1]
    db = convw.shape[1]
    tm = _tile(MIX_TILE, T // B)
    tps = (T // B) // tm

    def body(proj_ref, x_ref, gate_ref, wout_ref, lng_ref, lnb_ref, wst_ref, bias_ref, pmat_ref, convw_ref,
             og_ref, xo_ref, yn_ref, halo):
        i = pl.program_id(0)

        @pl.when((i % tps) == 0)
        def _():
            halo[...] = jnp.zeros_like(halo)

        prm = _mix_params(lng_ref, lnb_ref, wst_ref, bias_ref, pmat_ref, convw_ref)
        r = _mix_core_forward(proj_ref[...], halo[...], prm, da, db)
        halo[...] = r["z"][tm - 8:tm]
        og = og_ref[...]
        yn_ref[:, 0:da] = (r["yah"] * og[:, 0:da]).astype(BF16)
        yn_ref[:, da:] = (r["ybh"] * og[:, da:]).astype(BF16)
        xo_ref[...] = x_ref[...] + gate_ref[0] * _dot(yn_ref[...], wout_ref[...])

    full = lambda a: pl.BlockSpec(a.shape, lambda i: (0,) * a.ndim)
    return _call(
        body, "mix_core_fwd", (T // tm,),
        [
            pl.BlockSpec((tm, P), lambda i: (i, 0)),
            pl.BlockSpec((tm, D), lambda i: (i, 0)),
            pl.BlockSpec((1, 1, D), lambda i: (i // tps, 0, 0)),
            full(wout), full(lng), full(lnb), full(wst), full(bias), full(pmat), full(convw), full(og),
        ],
        [pl.BlockSpec((tm, D), lambda i: (i, 0)), pl.BlockSpec((tm, D), lambda i: (i, 0))],
        [jax.ShapeDtypeStruct((T, D), F32), jax.ShapeDtypeStruct((T, D), BF16)],
        [pltpu.VMEM((8, db), F32)],
        (proj, x, gate, wout, lng, lnb, wst, bias, pmat, convw, og), comm)


def _mix_core_bwd(proj, dxo, gate, wout, lng, lnb, wst, wstt, bias, pmat, convw, og, comm=None):
    T, P = proj.shape
    D = dxo.shape[1]
    B = gate.shape[0]
    da = lng.shape[1]
    db = convw.shape[1]
    assert da == db and P == 2 * da + 3 * db
    tm = _tile(MIX_TILE, T // B)
    tps = (T // B) // tm
    nt = T // tm
    hd = da // N_HEADS

    def body(proj_ref, cgp_ref, xbp_ref, dxo_ref, gate_ref, wout_ref, lng_ref, lnb_ref, wst_ref, wstt_ref,
             bias_ref, pmat_ref, convw_ref, og_ref,
             dproj_ref, do_ref, dgate_ref, dog_ref, dwst_ref, dbias_ref, dlng_ref, dlnb_ref, dconvw_ref, carry):
        i = pl.program_id(0)
        ri = nt - 1 - i
        first = i == 0
        end_of_seq = (ri % tps) == tps - 1
        start_of_seq = (ri % tps) == 0

        @pl.when(end_of_seq)
        def _():
            carry[...] = jnp.zeros_like(carry)

        prm = _mix_params(lng_ref, lnb_ref, wst_ref, bias_ref, pmat_ref, convw_ref)
        zprev = jnp.where(start_of_seq, 0.0, cgp_ref[...] * xbp_ref[...])
        r = _mix_core_forward(proj_ref[...], zprev, prm, da, db)
        og = og_ref[...]
        pmat = prm["pmat"]

        yn = jnp.concatenate([(r["yah"] * og[:, 0:da]).astype(BF16), (r["ybh"] * og[:, da:]).astype(BF16)], axis=1)
        dxo = dxo_ref[...]
        o = _dot(yn, wout_ref[...])
        _acc(dgate_ref.at[0], end_of_seq, jnp.sum(dxo * o, axis=0, keepdims=True))
        d_o = (gate_ref[0] * dxo).astype(BF16)
        do_ref[...] = d_o
        dyn = _dot_nt(d_o, wout_ref[...])

        def rms_bwd(dyn_g, yh, rr, og_g):
            dog_g = jnp.sum(dyn_g * yh, axis=0, keepdims=True)
            dyh = dyn_g * og_g
            return rr * (dyh - yh * jnp.mean(dyh * yh, axis=-1, keepdims=True)), dog_g

        dya, dog_a = rms_bwd(dyn[:, 0:da], r["yah"], r["ra"], og[:, 0:da])
        dyb, dog_b = rms_bwd(dyn[:, da:], r["ybh"], r["rb"], og[:, da:])
        _acc(dog_ref, first, jnp.concatenate([dog_a, dog_b], axis=1))

        dug = dya * r["mixed"]
        dmixed = dya * r["ug"]
        wstt_b = _causal_stack(wstt_ref[...], True).astype(BF16)
        masks = r["masks"]
        dbias = jnp.zeros((CHUNK, da), F32)
        dwst = jnp.zeros((N_HEADS * CHUNK, CHUNK), F32)
        dvln = []
        for j in range(tm // CHUNK):
            dm = dmixed[j * CHUNK:(j + 1) * CHUNK]
            dbias = dbias + dm
            dmb = dm.astype(BF16)
            stack = jnp.concatenate([jnp.where(masks[h], dmb, jnp.zeros_like(dmb)) for h in range(N_HEADS)], axis=0)
            dwst = dwst + _dot_nt(stack, r["vln"][j * CHUNK:(j + 1) * CHUNK])
            dvln.append(_select_heads(_dot(wstt_b, dmb), masks))
        dvln = dvln[0] if len(dvln) == 1 else jnp.concatenate(dvln, axis=0)
        _acc(dbias_ref, first, dbias)
        _acc(dwst_ref, first, dwst)
        _acc(dlng_ref, first, jnp.sum(dvln * r["vhat"], axis=0, keepdims=True))
        _acc(dlnb_ref, first, jnp.sum(dvln, axis=0, keepdims=True))
        dvhat = dvln * prm["lng"]
        dvg = r["rs"] * (dvhat - _head_mean(dvhat, pmat) - r["vhat"] * _head_mean(dvhat * r["vhat"], pmat))
        dproj_ref[:, 0:da] = (dug * _gelu_grad(r["ua"], r["ucdf"])).astype(BF16)
        dproj_ref[:, da:2 * da] = (dvg * _gelu_grad(r["va"], r["vcdf"])).astype(BF16)

        dproj_ref[:, 2 * da:2 * da + db] = (dyb * r["conv"]).astype(BF16)
        dconv = dyb * r["bg"]
        dcw = jnp.concatenate([
            jnp.sum(dconv * r["z2"], axis=0, keepdims=True),
            jnp.sum(dconv * r["z1"], axis=0, keepdims=True),
            jnp.sum(dconv * r["z"], axis=0, keepdims=True),
            jnp.zeros((5, db), F32)], axis=0)
        _acc(dconvw_ref, first, dcw)
        nxt = carry[...]
        row = lax.broadcasted_iota(jnp.int32, dconv.shape, 0)
        dc1 = jnp.where(row == tm - 1, nxt[0:1], pltpu.roll(dconv, tm - 1, 0))
        dc2 = jnp.where(row == tm - 2, nxt[0:1], jnp.where(row == tm - 1, nxt[1:2], pltpu.roll(dconv, tm - 2, 0)))
        carry[...] = dconv[0:8]
        cw = prm["convw"]
        dz = dconv * cw[2:3] + dc1 * cw[1:2] + dc2 * cw[0:1]
        dproj_ref[:, 2 * da + db:2 * da + 2 * db] = (dz * r["xb"]).astype(BF16)
        dproj_ref[:, 2 * da + 2 * db:] = (dz * r["cg"]).astype(BF16)

        @pl.when(i == nt - 1)
        def _():
            dwst_ref[...] = _causal_stack(dwst_ref[...], False)
            dbias_ref[...] = _head_mean(dbias_ref[...], pmat) * float(hd)

    full = lambda a: pl.BlockSpec(a.shape, lambda i: (0,) * a.ndim)
    const = lambda i: (0, 0)
    rev = lambda i: (nt - 1 - i, 0)
    prev8 = lambda col: (lambda i: (jnp.maximum((nt - 1 - i) * (tm // 8) - 1, 0), col))
    return _call(
        body, "mix_core_bwd", (nt,),
        [
            pl.BlockSpec((tm, P), rev),
            pl.BlockSpec((8, db), prev8((2 * da + db) // db)),
            pl.BlockSpec((8, db), prev8((2 * da + 2 * db) // db)),
            pl.BlockSpec((tm, D), rev),
            pl.BlockSpec((1, 1, D), lambda i: ((nt - 1 - i) // tps, 0, 0)),
            full(wout), full(lng), full(lnb), full(wst), full(wstt), full(bias), full(pmat), full(convw), full(og),
        ],
        [
            pl.BlockSpec((tm, P), rev),
            pl.BlockSpec((tm, D), rev),
            pl.BlockSpec((1, 1, D), lambda i: ((nt - 1 - i) // tps, 0, 0)),
            pl.BlockSpec((1, D), const),
            pl.BlockSpec((N_HEADS * CHUNK, CHUNK), const),
            pl.BlockSpec((CHUNK, da), const),
            pl.BlockSpec((1, da), const),
            pl.BlockSpec((1, da), const),
            pl.BlockSpec((8, db), const),
        ],
        [
            jax.ShapeDtypeStruct((T, P), BF16),
            jax.ShapeDtypeStruct((T, D), BF16),
            jax.ShapeDtypeStruct((B, 1, D), F32),
            jax.ShapeDtypeStruct((1, D), F32),
            jax.ShapeDtypeStruct((N_HEADS * CHUNK, CHUNK), F32),
            jax.ShapeDtypeStruct((CHUNK, da), F32),
            jax.ShapeDtypeStruct((1, da), F32),
            jax.ShapeDtypeStruct((1, da), F32),
            jax.ShapeDtypeStruct((8, db), F32),
        ],
        [pltpu.VMEM((8, db), F32)],
        (proj, proj, proj, dxo, gate, wout, lng, lnb, wst, wstt, bias, pmat, convw, og), comm)


def _loss_head(x, target, gain):
    T, D = x.shape
    tm = _tile(TOKEN_TILE, T)

    def body(x_ref, t_ref, gain_ref, dx_ref, loss_ref, dgain_ref):
        first = pl.program_id(0) == 0
        xh, r = _rms(x_ref[...])
        gain = gain_ref[...]
        err = xh * gain - t_ref[...]
        _acc(loss_ref, first, jnp.zeros((8, 128), F32) + 0.5 * jnp.sum(err * err) / D)
        dout = err * (1.0 / D)
        _acc(dgain_ref, first, jnp.sum(dout * xh, axis=0, keepdims=True))
        dy = dout * gain
        dx_ref[...] = r * (dy - xh * jnp.mean(dy * xh, axis=-1, keepdims=True))

    return _call(
        body, "loss_head", (T // tm,),
        [
            pl.BlockSpec((tm, D), lambda i: (i, 0)),
            pl.BlockSpec((tm, D), lambda i: (i, 0)),
            pl.BlockSpec((1, D), lambda i: (0, 0)),
        ],
        [
            pl.BlockSpec((tm, D), lambda i: (i, 0)),
            pl.BlockSpec((8, 128), lambda i: (0, 0)),
            pl.BlockSpec((1, D), lambda i: (0, 0)),
        ],
        [
            jax.ShapeDtypeStruct((T, D), F32),
            jax.ShapeDtypeStruct((8, 128), F32),
            jax.ShapeDtypeStruct((1, D), F32),
        ],
        [],
        (x, target, gain))[0]


def _ada_fwd(c_all, ada_w, ada_b):
    n, D = c_all.shape
    L, _, sa = ada_w.shape
    tn = _tile(768, sa)

    def body(c_ref, w_ref, b_ref, act_ref, o_ref):
        c = c_ref[...]
        act = (c * _sigmoid(c)).astype(BF16)
        act_ref[...] = act
        o_ref[...] = _dot(act, w_ref[...].astype(BF16)) + b_ref[...]

    return _call(
        body, "ada_fwd", (L, sa // tn),
        [
            pl.BlockSpec((n, D), lambda l, j: (0, 0)),
            pl.BlockSpec((None, D, tn), lambda l, j: (l, 0, j)),
            pl.BlockSpec((None, 1, tn), lambda l, j: (l, 0, j)),
        ],
        [
            pl.BlockSpec((n, D), lambda l, j: (0, 0)),
            pl.BlockSpec((None, n, tn), lambda l, j: (l, 0, j)),
        ],
        [jax.ShapeDtypeStruct((n, D), BF16), jax.ShapeDtypeStruct((L, n, sa), F32)],
        [],
        (c_all, ada_w, ada_b))[0]


def _ada_bwd(c_act, d_ada, comm=None):
    n, D = c_act.shape
    L, _, sa = d_ada.shape
    tn = _tile(768, sa)

    def body(c_ref, d_ref, o_ref):
        o_ref[...] = _dot_tn(c_ref[...], d_ref[...])

    return _call(
        body, "ada_bwd", (L, sa // tn),
        [pl.BlockSpec((n, D), lambda l, j: (0, 0)), pl.BlockSpec((None, n, tn), lambda l, j: (l, 0, j))],
        [pl.BlockSpec((None, D, tn), lambda l, j: (l, 0, j))],
        [jax.ShapeDtypeStruct((L, D, sa), F32)],
        [],
        (c_act, d_ada), comm)


def _colsum(a):
    L, n, C = a.shape

    def body(a_ref, o_ref):
        o_ref[...] = jnp.sum(a_ref[...], axis=0, keepdims=True)

    return _call(
        body, "colsum", (L,),
        [pl.BlockSpec((None, n, C), lambda l: (l, 0, 0))],
        [pl.BlockSpec((None, 1, C), lambda l: (l, 0, 0))],
        [jax.ShapeDtypeStruct((L, 1, C), F32)],
        [],
        (a,))[0][0]


def _row_tile(rows, cols, nbuf):
    budget = VMEM_LIMIT // 3 // (2 * nbuf * 4 * cols)
    t = rows
    while t > max(budget, 8) and t % 2 == 0 and (t // 2) % 8 == 0:
        t //= 2
    return t


def _pair_sum(g, recv, core):
    n, _, R, C = g.shape
    tr = _row_tile(R, C, 3)

    def body(core_ref, g_ref, r_ref, o_ref):
        o_ref[...] = (g_ref[...] + r_ref[...]).astype(BF16)

    return pl.pallas_call(
        body,
        name="pair_sum",
        grid_spec=pltpu.PrefetchScalarGridSpec(
            num_scalar_prefetch=1,
            grid=(n, R // tr),
            in_specs=[
                pl.BlockSpec((None, None, tr, C), lambda i, r, core_ref: (i, core_ref[0], r, 0)),
                pl.BlockSpec((None, tr, C), lambda i, r, core_ref: (i, r, 0)),
            ],
            out_specs=pl.BlockSpec((None, tr, C), lambda i, r, core_ref: (i, r, 0)),
        ),
        out_shape=jax.ShapeDtypeStruct((n, R, C), BF16),
        compiler_params=pltpu.CompilerParams(dimension_semantics=("arbitrary", "arbitrary"),
                                             vmem_limit_bytes=VMEM_LIMIT),
    )(core, g, recv)


def _chip_sum(q, core, l, n_layers, prev):
    nq, R, C = q.shape
    tr = _row_tile(R, C, 4)

    def body(core_ref, q_ref, *rest):
        o_ref = rest[-1]
        s = q_ref[0].astype(F32)
        for j in range(1, nq):
            s = s + q_ref[j].astype(F32)
        o_ref[...] = s

    in_specs = [pl.BlockSpec((nq, tr, C), lambda r, core_ref: (0, r, 0))]
    args = [core, q]
    aliases = {}
    if prev is not None:
        in_specs.append(ANY)
        args.append(prev)
        aliases = {2: 0}
    return pl.pallas_call(
        body,
        name="chip_sum",
        grid_spec=pltpu.PrefetchScalarGridSpec(
            num_scalar_prefetch=1,
            grid=(R // tr,),
            in_specs=in_specs,
            out_specs=pl.BlockSpec((None, None, tr, C), lambda r, core_ref: (l, core_ref[0], r, 0)),
        ),
        out_shape=jax.ShapeDtypeStruct((n_layers, 2, R, C), F32),
        input_output_aliases=aliases,
        compiler_params=pltpu.CompilerParams(dimension_semantics=("arbitrary",), vmem_limit_bytes=VMEM_LIMIT),
    )(*args)


def _sum_blocks(a, n):
    M = a.shape[0] // n
    C = a.shape[1]

    def body(a_ref, o_ref):
        s = a_ref[0:M]
        for j in range(1, n):
            s = s + a_ref[j * M:(j + 1) * M]
        o_ref[...] = s

    return pl.pallas_call(
        body,
        name="sum_blocks",
        out_shape=jax.ShapeDtypeStruct((M, C), F32),
        compiler_params=pltpu.CompilerParams(vmem_limit_bytes=VMEM_LIMIT),
    )(a)


def _adamw(w, g, m, v, emit_grad=False):
    R, C = w.shape
    n_out = 4 if emit_grad else 3
    tr = _row_tile(R, C, 4 + n_out) if R % 8 == 0 else R

    def body(w_ref, g_ref, m_ref, v_ref, d_ref, nm_ref, nv_ref, *g_out):
        g = g_ref[...]
        m = ADAM_B1 * m_ref[...] + (1.0 - ADAM_B1) * g
        v = ADAM_B2 * v_ref[...] + (1.0 - ADAM_B2) * (g * g)
        m_hat = m / (1.0 - ADAM_B1 ** ADAM_STEP)
        v_hat = v / (1.0 - ADAM_B2 ** ADAM_STEP)
        d_ref[...] = -ADAM_LR * (m_hat / (jnp.sqrt(v_hat) + ADAM_EPS) + ADAM_WD * w_ref[...])
        nm_ref[...] = m
        nv_ref[...] = v
        if emit_grad:
            g_out[0][...] = g

    spec = pl.BlockSpec((tr, C), lambda i: (i, 0))
    return _call(body, "adamw", (R // tr,), [spec] * 4, [spec] * n_out, [jax.ShapeDtypeStruct((R, C), F32)] * n_out,
                 [], (w, g, m, v))[0]


def kernel(x, c, ada_w, ada_b, norm_ffn1_g, ffn1_w_gu, ffn1_w_down, norm_mix_g, mix_w_in, sgu_ln_g, sgu_ln_b, sgu_w_s, sgu_b, conv_w, out_norm_g, mix_w_out, norm_ffn2_g, ffn2_w_gu, ffn2_w_down, final_norm_g, loss_target, m_ada_w, m_ada_b, m_norm_ffn1_g, m_ffn1_w_gu, m_ffn1_w_down, m_norm_mix_g, m_mix_w_in, m_sgu_ln_g, m_sgu_ln_b, m_sgu_w_s, m_sgu_b, m_conv_w, m_out_norm_g, m_mix_w_out, m_norm_ffn2_g, m_ffn2_w_gu, m_ffn2_w_down, m_final_norm_g, v_ada_w, v_ada_b, v_norm_ffn1_g, v_ffn1_w_gu, v_ffn1_w_down, v_norm_mix_g, v_mix_w_in, v_sgu_ln_g, v_sgu_ln_b, v_sgu_w_s, v_sgu_b, v_conv_w, v_out_norm_g, v_mix_w_out, v_norm_ffn2_g, v_ffn2_w_gu, v_ffn2_w_down, v_final_norm_g):
    weights = dict(ada_w=ada_w, ada_b=ada_b, norm_ffn1_g=norm_ffn1_g, ffn1_w_gu=ffn1_w_gu, ffn1_w_down=ffn1_w_down,
                   norm_mix_g=norm_mix_g, mix_w_in=mix_w_in, sgu_ln_g=sgu_ln_g, sgu_ln_b=sgu_ln_b, sgu_w_s=sgu_w_s,
                   sgu_b=sgu_b, conv_w=conv_w, out_norm_g=out_norm_g, mix_w_out=mix_w_out, norm_ffn2_g=norm_ffn2_g,
                   ffn2_w_gu=ffn2_w_gu, ffn2_w_down=ffn2_w_down, final_norm_g=final_norm_g)
    m_in = dict(ada_w=m_ada_w, ada_b=m_ada_b, norm_ffn1_g=m_norm_ffn1_g, ffn1_w_gu=m_ffn1_w_gu,
                ffn1_w_down=m_ffn1_w_down, norm_mix_g=m_norm_mix_g, mix_w_in=m_mix_w_in, sgu_ln_g=m_sgu_ln_g,
                sgu_ln_b=m_sgu_ln_b, sgu_w_s=m_sgu_w_s, sgu_b=m_sgu_b, conv_w=m_conv_w, out_norm_g=m_out_norm_g,
                mix_w_out=m_mix_w_out, norm_ffn2_g=m_norm_ffn2_g, ffn2_w_gu=m_ffn2_w_gu, ffn2_w_down=m_ffn2_w_down,
                final_norm_g=m_final_norm_g)
    v_in = dict(ada_w=v_ada_w, ada_b=v_ada_b, norm_ffn1_g=v_norm_ffn1_g, ffn1_w_gu=v_ffn1_w_gu,
                ffn1_w_down=v_ffn1_w_down, norm_mix_g=v_norm_mix_g, mix_w_in=v_mix_w_in, sgu_ln_g=v_sgu_ln_g,
                sgu_ln_b=v_sgu_ln_b, sgu_w_s=v_sgu_w_s, sgu_b=v_sgu_b, conv_w=v_conv_w, out_norm_g=v_out_norm_g,
                mix_w_out=v_mix_w_out, norm_ffn2_g=v_norm_ffn2_g, ffn2_w_gu=v_ffn2_w_gu, ffn2_w_down=v_ffn2_w_down,
                final_norm_g=v_final_norm_g)

    B, S, D = x.shape
    T = B * S
    L = ada_w.shape[0]
    F = ffn1_w_down.shape[1] * N_CHIP
    P = mix_w_in.shape[2] * N_CHIP
    DA = D // 2
    DB = D - DA
    HD = DA // N_HEADS
    SA = ada_w.shape[2]
    n_all = B * N_DEV
    mx, my, mc = _position()
    chip = 2 * mx + my
    dev = 2 * chip + mc
    core = jnp.reshape(mc, (1,)).astype(jnp.int32)

    big = ["ffn1_w_gu", "ffn1_w_down", "mix_w_in", "mix_w_out", "ffn2_w_gu", "ffn2_w_down"]
    col_sharded = dict(ffn1_w_gu=True, ffn1_w_down=False, mix_w_in=True, mix_w_out=False,
                       ffn2_w_gu=True, ffn2_w_down=False)
    shards = {k: weights[k].astype(BF16) for k in big}
    gather = lambda l, *names: _gather_comm([(shards[k], l, col_sharded[k]) for k in names])
    full = [dict() for _ in range(L)]

    def arrived(l, names, res):
        full[l].update(zip(names, res))

    n_cw = L * conv_w.shape[1]
    cw_block = jnp.pad(conv_w.reshape(n_cw, conv_w.shape[2]), ((0, 8 - n_cw), (0, 0)))
    c_all, cw_all = _comm_call(_merge(_all_gather_comm(c.reshape(8, B * D // 8)), _all_gather_comm(cw_block)),
                               "gather_c")
    c_all = c_all.reshape(n_all, D)
    cw_all = cw_all.reshape(N_CHIP, 2, 8, conv_w.shape[2])[:, 0, :n_cw]
    conv_full = jnp.transpose(cw_all.reshape(N_CHIP, L, conv_w.shape[1], conv_w.shape[2]), (1, 2, 0, 3))
    conv_full = conv_full.reshape(L, conv_w.shape[1], DB)
    ada_b_mine = lax.dynamic_slice_in_dim(ada_b, chip * SA, SA, axis=1).reshape(L, 1, SA)
    c_act, ada_part = _ada_fwd(c_all, ada_w, ada_b_mine)
    ada_all, first_w = _comm_call(_merge(_all_gather_comm(ada_part.reshape(L * n_all, SA)), gather(0, big[0])),
                                  "gather_first")
    arrived(0, big[:1], [first_w])
    ada_all = ada_all.reshape(N_CHIP, 2, L, n_all, SA)[:, 0]
    ada_all = jnp.transpose(ada_all, (1, 2, 0, 3)).reshape(L, n_all, N_CHIP * SA)
    ada = lax.dynamic_slice_in_dim(ada_all, dev * B, B, axis=1).reshape(L, B, N_MOD, 1, D)
    mods = [[ada[l, :, j] for j in range(N_MOD)] for l in range(L)]

    x0 = x.reshape(T, D)
    gains = lambda name, l: weights[name][l].reshape(1, D)
    hmask = jnp.repeat(jnp.eye(N_HEADS, dtype=F32), HD, axis=0)
    pmat = (jnp.repeat(hmask, HD, axis=1) / HD).astype(BF16)

    def mix_consts(l):
        lng = jnp.tile(sgu_ln_g[l], N_HEADS).reshape(1, DA)
        lnb = jnp.tile(sgu_ln_b[l], N_HEADS).reshape(1, DA)
        wst = sgu_w_s[l].reshape(N_HEADS * CHUNK, CHUNK)
        wstt = jnp.swapaxes(sgu_w_s[l], 1, 2).reshape(N_HEADS * CHUNK, CHUNK)
        bias = jnp.repeat(jnp.transpose(sgu_b[l]), HD, axis=1)
        return lng, lnb, wst, wstt, bias

    def fetch(fn, *args, bring=()):
        bring = [(l, k) for l, k in bring if l < L]
        comm = _gather_comm([(shards[k], l, col_sharded[k]) for l, k in bring]) if bring else None
        res, got = fn(*args, comm)
        for (l, k), a in zip(bring, got):
            full[l][k] = a
        return res

    saved = []
    xc = x0
    for l in range(L):
        sh1, sc1, g1, sh2, sc2, g2, sh3, sc3, g3 = mods[l]
        lng, lnb, wst, wstt, bias = mix_consts(l)
        w = full[l]
        own = l == 0
        gu1, a1 = fetch(_ffn_up, xc, gains("norm_ffn1_g", l), sh1, sc1, w["ffn1_w_gu"],
                        bring=[(l, "ffn1_w_down"), (l, "mix_w_in"), (l, "mix_w_out")] if own else [(l, "ffn2_w_gu")])
        xa, f1 = fetch(_ffn_down, a1, xc, g1, w["ffn1_w_down"], bring=[(l, "ffn2_w_down")])
        proj, h2 = fetch(_mixin_fwd, xa, gains("norm_mix_g", l), sh2, sc2, w["mix_w_in"], bring=[(l + 1, "mix_w_in")])
        xb, yn = fetch(_mix_core_fwd, proj, xa, g2, w["mix_w_out"], lng, lnb, wst, bias, pmat, conv_full[l],
                       gains("out_norm_g", l), bring=[(l, "ffn2_w_gu")] if own else [])
        gu2, a2 = fetch(_ffn_up, xb, gains("norm_ffn2_g", l), sh3, sc3, w["ffn2_w_gu"],
                        bring=[(l + 1, "ffn1_w_gu"), (l + 1, "mix_w_out")])
        xd, f2 = fetch(_ffn_down, a2, xb, g3, w["ffn2_w_down"], bring=[(l + 1, "ffn1_w_down")])
        saved.append(dict(x0=xc, xa=xa, xb=xb, gu1=gu1, a1=a1, f1=f1, proj=proj, h2=h2, yn=yn,
                          gu2=gu2, a2=a2, f2=f2))
        xc = xd

    dx, loss_block, d_final = _loss_head(xc, loss_target.reshape(T, D), final_norm_g.reshape(1, D))
    loss = lax.psum(loss_block[0, 0], ("x", "y", "c"))

    reduced = dict.fromkeys(big)

    def halves(name, g):
        if g.ndim == 4:
            return g
        return g.reshape(N_CHIP, 2, weights[name].shape[1] // 2, g.shape[-1])

    class Reduction:
        def __init__(self, l, name, g):
            self.l, self.name, self.g, self.stage = l, name, halves(name, g), 0
            self.ici_bytes = 3 * (g.size // 8) * 2

        def step(self):
            self.stage += 1
            if self.stage == 1:
                return _sibling_half_comm([self.g])
            if self.stage == 2:
                return _scatter_comm([_pair_sum(self.g, self.got[0], core)])
            if self.stage == 3:
                reduced[self.name] = _chip_sum(self.got[0], core, self.l, L, reduced[self.name])
                return _share_comm([reduced[self.name]], self.l)
            reduced[self.name] = self.got[0]
            return None

    active, extra, gathered = [], [], {}

    def carry(fn, *args, us=None):
        left = None if us is None else us * SCATTER_BYTES_PER_US
        riders = []
        for r in active:
            if r.stage == 1 and left is not None:
                if r.ici_bytes > left * SCATTER_OVERSHOOT:
                    continue
                left -= r.ici_bytes
            riders.append(r)
        comms = [r.step() for r in riders] + [cm for cm, _ in extra]
        takers = [functools.partial(setattr, r, "got") for r in riders] + [cb for _, cb in extra]
        extra.clear()
        if fn is None:
            res, got = None, (_comm_call(_merge(*comms), "reduce_alone") if comms else [])
        else:
            res, got = fn(*args, comm=_merge(*comms))
        at = 0
        for cm, take in zip(comms, takers):
            take(got[at:at + len(cm.out_shape)])
            at += len(cm.out_shape)
        for r in riders:
            if r.stage == 3:
                r.step()
                active.remove(r)
        return res

    def reduce_later(l, name, g):
        active.append(Reduction(l, name, g))

    small = [None] * L
    d_ada = [None] * L
    for l in reversed(range(L)):
        sh1, sc1, g1, sh2, sc2, g2, sh3, sc3, g3 = mods[l]
        lng, lnb, wst, wstt, bias = mix_consts(l)
        s = saved[l]
        w = full[l]
        dgu, = carry(_ffn_bwd_down, dx, s["gu2"], g3, w["ffn2_w_down"], us=100)
        dx, h3, df, dsc3, dsh3, dgain3, dg3 = carry(_ffn_bwd_up, dx, s["xb"], dgu, s["f2"], gains("norm_ffn2_g", l),
                                                    sh3, sc3, g3, w["ffn2_w_gu"], us=120)
        reduce_later(l, "ffn2_w_gu", carry(_wgrad, h3, dgu, D, 2 * F // N_CHIP, True, "wgrad_gu", us=110)[0])
        reduce_later(l, "ffn2_w_down", carry(_wgrad, s["a2"], df[None], F // 2, D, False, "wgrad_down", us=50)[0])
        dproj, d_o, dg2, dog, dwst, dbias, dlng, dlnb, dconvw = carry(
            _mix_core_bwd, s["proj"], dx, g2, w["mix_w_out"], lng, lnb, wst, wstt, bias, pmat, conv_full[l],
            gains("out_norm_g", l), us=200)
        mix_grads = [
            lambda: reduce_later(l, "mix_w_out", carry(_wgrad, s["yn"], d_o[None], D, D, False, "wgrad_out", us=20)[0]),
            lambda: reduce_later(l, "mix_w_in", carry(_wgrad, s["h2"], dproj[None], D, P // N_CHIP, True, "wgrad_in",
                                                      us=65)[0])]
        last = l == 0
        if not last:
            mix_grads[0]()
        dx, dsc2, dsh2, dgain2 = carry(_mixin_bwd, dx, s["xa"], dproj, gains("norm_mix_g", l), sc2, w["mix_w_in"], us=60)
        if not last:
            mix_grads[1]()
        dgu, = carry(_ffn_bwd_down, dx, s["gu1"], g1, w["ffn1_w_down"], us=100)
        dx, h1, df, dsc1, dsh1, dgain1, dg1 = carry(_ffn_bwd_up, dx, s["x0"], dgu, s["f1"], gains("norm_ffn1_g", l),
                                                    sh1, sc1, g1, w["ffn1_w_gu"], us=120)
        d_ada[l] = jnp.concatenate([dsh1, dsc1, dg1, dsh2, dsc2, dg2, dsh3, dsc3, dg3], axis=1).reshape(B, N_MOD * D)
        small[l] = [dgain1, dgain2, dgain3, dog, dlng, dlnb, dwst, dbias[:, ::HD], dconvw]
        if last:
            flat = [a.reshape(-1, 128) for ll in range(L) for a in small[ll]] + [d_final.reshape(-1, 128)]
            pad = (-sum(a.shape[0] for a in flat)) % 8
            packed = jnp.concatenate(flat + [jnp.zeros((pad, 128), F32)], axis=0)
            extra.append((_all_gather_comm(jnp.stack(d_ada).reshape(L * B, N_MOD * D)),
                          lambda got: gathered.update(d_ada=got[0])))
            extra.append((_all_gather_comm(packed), lambda got: gathered.update(small=got[0])))
        reduce_later(l, "ffn1_w_gu", carry(_wgrad, h1, dgu, D, 2 * F // N_CHIP, True, "wgrad_gu", us=110)[0])
        reduce_later(l, "ffn1_w_down", carry(_wgrad, s["a1"], df[None], F // 2, D, False, "wgrad_down", us=50)[0])
        if last:
            mix_grads[1]()
            mix_grads[0]()
    grad_x = dx.reshape(B, S, D)

    def finished(name):
        while any(r.name == name for r in active):
            carry(None)
        return reduced[name].reshape(weights[name].shape)

    grads = {}
    d_ada_all = jnp.transpose(gathered["d_ada"].reshape(N_DEV, L, B, N_MOD * D), (1, 0, 2, 3))
    d_ada_all = d_ada_all.reshape(L, n_all, N_MOD * D)
    grads["ada_b"] = _colsum(d_ada_all).reshape(L, N_MOD * D)
    d_ada_mine = lax.dynamic_slice_in_dim(d_ada_all, chip * SA, SA, axis=2).astype(BF16)
    grads["ada_w"] = carry(_ada_bwd, c_act, d_ada_mine, us=8)[0]

    total = _sum_blocks(gathered["small"].reshape(-1, 128), N_DEV)
    pieces, at = [], 0
    for a in flat:
        pieces.append(total[at:at + a.shape[0]])
        at += a.shape[0]
    per_layer = len(small[0])
    stack = lambda j, shape: jnp.stack([pieces[l * per_layer + j].reshape(shape) for l in range(L)])
    grads["norm_ffn1_g"] = stack(0, (D,))
    grads["norm_mix_g"] = stack(1, (D,))
    grads["norm_ffn2_g"] = stack(2, (D,))
    grads["out_norm_g"] = stack(3, (D,))
    grads["sgu_ln_g"] = stack(4, (N_HEADS, HD)).sum(axis=1)
    grads["sgu_ln_b"] = stack(5, (N_HEADS, HD)).sum(axis=1)
    grads["sgu_w_s"] = stack(6, (N_HEADS, CHUNK, CHUNK))
    grads["sgu_b"] = jnp.swapaxes(stack(7, (CHUNK, N_HEADS)), 1, 2)
    g_conv = stack(8, (8, DB))[:, :conv_w.shape[1]]
    grads["conv_w"] = lax.dynamic_slice_in_dim(g_conv, chip * conv_w.shape[2], conv_w.shape[2], axis=2)
    grads["final_norm_g"] = pieces[-1].reshape(D)

    names = list(weights)
    delta, new_m, new_v = {}, {}, {}
    for k in big:
        grads[k] = finished(k)
    for k in names:
        wk = weights[k]
        view = (1, wk.shape[0]) if wk.ndim == 1 else (-1, wk.shape[-1])
        d, nm, nv, *g_again = _adamw(wk.reshape(view), grads[k].reshape(view), m_in[k].reshape(view),
                                     v_in[k].reshape(view), emit_grad=k in big)
        delta[k], new_m[k], new_v[k] = d.reshape(wk.shape), nm.reshape(wk.shape), nv.reshape(wk.shape)
        if g_again:
            grads[k] = g_again[0].reshape(wk.shape)

    return (loss, grad_x, *[grads[k] for k in names], *[delta[k] for k in names],
            *[new_m[k] for k in names], *[new_v[k] for k in names])
```

```python
import functools
import math

import jax
import jax.numpy as jnp
from jax import lax
from jax.experimental import pallas as pl
from jax.experimental.pallas import tpu as pltpu

F32 = jnp.float32
BF16 = jnp.bfloat16
MESH = pl.DeviceIdType.MESH

N_HEADS = 8
CHUNK = 128
N_MOD = 9
EPS = 1e-6
N_DEV = 8
N_CHIP = 4

ADAM_LR = 0.001
ADAM_B1 = 0.9
ADAM_B2 = 0.999
ADAM_EPS = 1e-08
ADAM_WD = 0.01
ADAM_STEP = 10

TOKEN_TILE = 512
FF_SLAB = 768
MIX_TILE = 256
WGRAD_TOKENS = 2048
VMEM_LIMIT = 56 * 1024 * 1024

SCATTER_BYTES_PER_US = 68_000
SCATTER_OVERSHOOT = 1.25

ANY = pl.BlockSpec(memory_space=pl.ANY)


def _tile(pref, n):
    t = min(pref, n)
    assert n % t == 0, (pref, n)
    return t


def _slabs(n, width):
    return [slice(c0, min(c0 + width, n)) for c0 in range(0, n, width)]


def _dot(a, b):
    return jnp.dot(a, b, preferred_element_type=F32)


def _dot_nt(a, b):
    return lax.dot_general(a, b, (((1,), (1,)), ((), ())), preferred_element_type=F32)


def _dot_tn(a, b):
    return lax.dot_general(a, b, (((0,), (0,)), ((), ())), preferred_element_type=F32)


def _sigmoid(x):
    return 1.0 / (1.0 + jnp.exp(-x))


def _sigmoid_fast(x):
    return pl.reciprocal(1.0 + jnp.exp(-x), approx=True)


def _rms(x):
    r = lax.rsqrt(jnp.mean(x * x, axis=-1, keepdims=True) + EPS)
    return x * r, r


def _norm_mod_bwd(x, dh, gain, sc):
    xh, r = _rms(x)
    dsc = jnp.sum(dh * (xh * gain), axis=0, keepdims=True)
    dsh = jnp.sum(dh, axis=0, keepdims=True)
    dn = dh * (1.0 + sc)
    dgain = jnp.sum(dn * xh, axis=0, keepdims=True)
    dy = dn * gain
    dx = r * (dy - xh * jnp.mean(dy * xh, axis=-1, keepdims=True))
    return dx, dsc, dsh, dgain


def _acc(ref, first, val):
    @pl.when(first)
    def _():
        ref[...] = val

    @pl.when(jnp.logical_not(first))
    def _():
        ref[...] += val


class _Comm:
    def __init__(self, args, out_shape, scratch, phases, aliases=None):
        self.args, self.out_shape, self.scratch = list(args), list(out_shape), list(scratch)
        self.phases, self.aliases = phases, dict(aliases or {})


def _merge(*comms):
    comms = [c for c in comms if c is not None]
    if len(comms) <= 1:
        return comms[0] if comms else None
    args = [a for c in comms for a in c.args]
    out_shape = [o for c in comms for o in c.out_shape]
    scratch = [s for c in comms for s in c.scratch]
    aliases, ai, oi = {}, 0, 0
    for c in comms:
        aliases.update({ai + i: oi + o for i, o in c.aliases.items()})
        ai += len(c.args)
        oi += len(c.out_shape)

    def phases(ins, outs, sems):
        parts, ai, oi, si = [], 0, 0, 0
        for c in comms:
            parts.append(c.phases(ins[ai:ai + len(c.args)], outs[oi:oi + len(c.out_shape)], sems[si:si + len(c.scratch)]))
            ai, oi, si = ai + len(c.args), oi + len(c.out_shape), si + len(c.scratch)

        def run(k):
            def go():
                for p in parts:
                    if p[k] is not None:
                        p[k]()
            return go
        return run(0), run(1), run(2)

    return _Comm(args, out_shape, scratch, phases, aliases)


def _call(body, name, grid, in_specs, out_specs, out_shape, scratch, args, comm=None):
    n_in, n_out, n_scr = len(in_specs), len(out_specs), len(scratch)
    sem = ("arbitrary",) * len(grid)
    params = pltpu.CompilerParams(dimension_semantics=sem, vmem_limit_bytes=VMEM_LIMIT)
    if comm is None:
        res = pl.pallas_call(body, name=name, grid=grid, in_specs=in_specs, out_specs=out_specs, out_shape=out_shape,
                             scratch_shapes=scratch, compiler_params=params)(*args)
        return list(res), []
    m_in, m_out = len(comm.args), len(comm.out_shape)

    def full(*refs):
        c_in, c_min = refs[:n_in], refs[n_in:n_in + m_in]
        o = n_in + m_in
        c_out, c_mout = refs[o:o + n_out], refs[o + n_out:o + n_out + m_out]
        o += n_out + m_out
        c_scr, c_sem = refs[o:o + n_scr], refs[o + n_scr:]
        start, mid, finish = comm.phases(c_min, c_mout, c_sem)
        ids = [pl.program_id(a) for a in range(len(grid))]
        first = functools.reduce(jnp.logical_and, [i == 0 for i in ids])
        last = functools.reduce(jnp.logical_and, [i == g - 1 for i, g in zip(ids, grid)])
        pl.when(first)(start)
        if mid is not None:
            pl.when(last)(mid)
        body(*c_in, *c_out, *c_scr)
        pl.when(last)(finish)

    res = pl.pallas_call(
        full, name=name, grid=grid,
        in_specs=list(in_specs) + [ANY] * m_in,
        out_specs=list(out_specs) + [ANY] * m_out,
        out_shape=list(out_shape) + comm.out_shape,
        scratch_shapes=list(scratch) + comm.scratch,
        input_output_aliases={n_in + i: n_out + o for i, o in comm.aliases.items()},
        compiler_params=params,
    )(*args, *comm.args)
    return list(res[:n_out]), list(res[n_out:])


def _comm_call(comm, name):
    m_in, m_out = len(comm.args), len(comm.out_shape)

    def body(*refs):
        start, mid, finish = comm.phases(refs[:m_in], refs[m_in:m_in + m_out], refs[m_in + m_out:])
        start()
        if mid is not None:
            mid()
        finish()

    res = pl.pallas_call(
        body, name=name, in_specs=[ANY] * m_in, out_specs=[ANY] * m_out, out_shape=comm.out_shape,
        scratch_shapes=comm.scratch, input_output_aliases=comm.aliases,
    )(*comm.args)
    return list(res)


def _position():
    return lax.axis_index("x"), lax.axis_index("y"), lax.axis_index("c")


def _gather_comm(items):
    n = len(items)
    half = [s.shape[1] // 2 for s, _, _ in items]

    def full_shape(i):
        s, _, col = items[i]
        _, R, C = s.shape
        return jax.ShapeDtypeStruct((R, N_CHIP * C) if col else (N_CHIP * R, C), s.dtype)

    def phases(ins, outs, sems):
        send_sems, recv_sems, local_sems = sems
        x, y, c = _position()

        def region(i, chip, h):
            s, _, col = items[i]
            _, R, C = s.shape
            if col:
                return outs[i].at[pl.ds(h * half[i], half[i]), pl.ds(chip * C, C)]
            return outs[i].at[pl.ds(chip * R + h * half[i], half[i]), :]

        def mine(i, h):
            return ins[i].at[items[i][1], pl.ds(h * half[i], half[i]), :]

        def copies(kx, ky, kc):
            k_me = 2 * kx + ky
            sibling = (kx, ky, 1 - kc)
            chips = [(1 - kx, ky), (kx, 1 - ky), (1 - kx, 1 - ky)]
            local, first, passed, arrive_ici, arrive_d2d = [], [], [], [], []

            def remote(src, dst, s, to):
                return pltpu.make_async_remote_copy(src_ref=src, dst_ref=dst, send_sem=send_sems.at[s],
                                                    recv_sem=recv_sems.at[s], device_id=to, device_id_type=MESH)

            for i in range(n):
                for h in range(2):
                    local.append(pltpu.make_async_copy(mine(i, h), region(i, k_me, h), local_sems.at[2 * i + h]))
                for j, (px, py) in enumerate(chips):
                    s = 6 * i + j
                    first.append(remote(mine(i, kc), region(i, k_me, kc), s, (px, py, kc)))
                    got = region(i, 2 * px + py, kc)
                    arrive_ici.append(remote(got, got, s, (px, py, kc)))
                    passed.append(remote(got, got, s + 3, sibling))
                    other = region(i, 2 * px + py, 1 - kc)
                    arrive_d2d.append(remote(other, other, s + 3, sibling))
            return local, first, passed, arrive_ici, arrive_d2d

        def on_each_device(fn):
            def go():
                for kx in range(2):
                    for ky in range(2):
                        for kc in range(2):
                            pl.when((x == kx) & (y == ky) & (c == kc))(functools.partial(fn, *copies(kx, ky, kc)))
            return go

        def start(local, first, passed, arrive_ici, arrive_d2d):
            for cp in local + first:
                cp.start()

        def mid(local, first, passed, arrive_ici, arrive_d2d):
            for a, p in zip(arrive_ici, passed):
                a.wait_recv()
                p.start()

        def finish(local, first, passed, arrive_ici, arrive_d2d):
            for a in arrive_d2d:
                a.wait_recv()
            for cp in first + passed:
                cp.wait_send()
            for cp in local:
                cp.wait()

        return on_each_device(start), on_each_device(mid), on_each_device(finish)

    scratch = [pltpu.SemaphoreType.DMA((6 * n,)), pltpu.SemaphoreType.DMA((6 * n,)), pltpu.SemaphoreType.DMA((2 * n,))]
    return _Comm([s for s, _, _ in items], [full_shape(i) for i in range(n)], scratch, phases)


def _sibling_half_comm(gs):
    n = len(gs)

    def phases(ins, outs, sems):
        send_sems, recv_sems = sems
        x, y, c = _position()

        def copies():
            return [pltpu.make_async_remote_copy(
                src_ref=ins[i].at[:, 1 - c], dst_ref=outs[i], send_sem=send_sems.at[i], recv_sem=recv_sems.at[i],
                device_id=(x, y, 1 - c), device_id_type=MESH) for i in range(n)]

        def start():
            for cp in copies():
                cp.start()

        def finish():
            for cp in copies():
                cp.wait()

        return start, None, finish

    out_shape = [jax.ShapeDtypeStruct(g.shape[:1] + g.shape[2:], g.dtype) for g in gs]
    return _Comm(gs, out_shape, [pltpu.SemaphoreType.DMA((n,)), pltpu.SemaphoreType.DMA((n,))], phases)


def _scatter_comm(ps):
    n = len(ps)

    def phases(ins, outs, sems):
        send_sems, recv_sems, local_sems = sems
        x, y, c = _position()
        k_me = 2 * x + y
        chips = [(1 - x, y), (x, 1 - y), (1 - x, 1 - y)]

        def copies():
            local = [pltpu.make_async_copy(ins[i].at[k_me], outs[i].at[k_me], local_sems.at[i]) for i in range(n)]
            remote = [pltpu.make_async_remote_copy(
                src_ref=ins[i].at[2 * px + py], dst_ref=outs[i].at[k_me],
                send_sem=send_sems.at[3 * i + j], recv_sem=recv_sems.at[3 * i + j],
                device_id=(px, py, c), device_id_type=MESH) for i in range(n) for j, (px, py) in enumerate(chips)]
            return local, remote

        def start():
            local, remote = copies()
            for cp in local + remote:
                cp.start()

        def finish():
            local, remote = copies()
            for cp in remote + local:
                cp.wait()

        return start, None, finish

    scratch = [pltpu.SemaphoreType.DMA((3 * n,)), pltpu.SemaphoreType.DMA((3 * n,)), pltpu.SemaphoreType.DMA((n,))]
    return _Comm(ps, [jax.ShapeDtypeStruct(p.shape, p.dtype) for p in ps], scratch, phases)


def _share_comm(rs, l):
    n = len(rs)

    def phases(ins, outs, sems):
        send_sems, recv_sems = sems
        x, y, c = _position()

        def copy(i, h):
            return pltpu.make_async_remote_copy(
                src_ref=outs[i].at[l, h], dst_ref=outs[i].at[l, h], send_sem=send_sems.at[i], recv_sem=recv_sems.at[i],
                device_id=(x, y, 1 - c), device_id_type=MESH)

        def start():
            for i in range(n):
                copy(i, c).start()

        def finish():
            for i in range(n):
                copy(i, 1 - c).wait_recv()
            for i in range(n):
                copy(i, c).wait_send()

        return start, None, finish

    return _Comm(rs, [jax.ShapeDtypeStruct(r.shape, r.dtype) for r in rs],
                 [pltpu.SemaphoreType.DMA((n,)), pltpu.SemaphoreType.DMA((n,))], phases,
                 aliases={i: i for i in range(n)})


def _all_gather_comm(block):
    def phases(ins, outs, sems):
        send_sems, recv_sems, local_sem = sems
        (src,), (out,) = ins, outs
        x, y, c = _position()
        sibling = (x, y, 1 - c)
        chips = [(1 - x, y), (x, 1 - y), (1 - x, 1 - y)]

        def slot(px, py, pc):
            return out.at[4 * px + 2 * py + pc]

        def copy(k, blk, to, own=False):
            return pltpu.make_async_remote_copy(
                src_ref=src if own else slot(*blk), dst_ref=slot(*blk),
                send_sem=send_sems.at[k], recv_sem=recv_sems.at[k], device_id=to, device_id_type=MESH)

        mine = lambda: pltpu.make_async_copy(src, slot(x, y, c), local_sem.at[0])
        first = lambda: [copy(0, (x, y, c), sibling, True)] + [
            copy(1 + j, (x, y, c), (*chip, c), True) for j, chip in enumerate(chips)]
        passed = lambda: [copy(4 + j, (*chip, c), sibling) for j, chip in enumerate(chips)]

        def start():
            mine().start()
            for cp in first():
                cp.start()

        def mid():
            for j, (chip, p) in enumerate(zip(chips, passed())):
                copy(1 + j, (*chip, c), (x, y, c)).wait_recv()
                p.start()

        def finish():
            copy(0, sibling, (x, y, c)).wait_recv()
            for j, chip in enumerate(chips):
                copy(4 + j, (*chip, 1 - c), (x, y, c)).wait_recv()
            for cp in first() + passed():
                cp.wait_send()
            mine().wait()

        return start, mid, finish

    scratch = [pltpu.SemaphoreType.DMA((7,)), pltpu.SemaphoreType.DMA((7,)), pltpu.SemaphoreType.DMA((1,))]
    return _Comm([block], [jax.ShapeDtypeStruct((N_DEV,) + block.shape, block.dtype)], scratch, phases)


def _ffn_up(x, gain, sh, sc, wgu, comm=None):
    T, D = x.shape
    F = wgu.shape[1] // 2
    B = sh.shape[0]
    tm = _tile(TOKEN_TILE, T // B)
    tps = (T // B) // tm
    slabs = _slabs(F, FF_SLAB)

    def body(x_ref, gain_ref, sh_ref, sc_ref, w_ref, gu_ref, a_ref):
        xh, _ = _rms(x_ref[...])
        h = (xh * gain_ref[...] * (1.0 + sc_ref[0]) + sh_ref[0]).astype(BF16)

        def dots(s):
            return _dot(h, w_ref[:, s]), _dot(h, w_ref[:, slice(F + s.start, F + s.stop)])

        nxt = dots(slabs[0])
        for j, s in enumerate(slabs):
            g, u = nxt
            if j + 1 < len(slabs):
                nxt = dots(slabs[j + 1])
            gu_ref[0, :, s] = g.astype(BF16)
            gu_ref[1, :, s] = u.astype(BF16)
            a_ref[:, s] = (g * _sigmoid(g) * u).astype(BF16)

    seq = lambda i: (i // tps, 0, 0)
    return _call(
        body, "ffn_up", (T // tm,),
        [
            pl.BlockSpec((tm, D), lambda i: (i, 0)),
            pl.BlockSpec((1, D), lambda i: (0, 0)),
            pl.BlockSpec((1, 1, D), seq),
            pl.BlockSpec((1, 1, D), seq),
            pl.BlockSpec((D, 2 * F), lambda i: (0, 0), pipeline_mode=pl.Buffered(1)),
        ],
        [
            pl.BlockSpec((2, tm, F), lambda i: (0, i, 0)),
            pl.BlockSpec((tm, F), lambda i: (i, 0)),
        ],
        [
            jax.ShapeDtypeStruct((2, T, F), BF16),
            jax.ShapeDtypeStruct((T, F), BF16),
        ],
        [],
        (x, gain, sh, sc, wgu), comm)


def _ffn_down(a, x, gate, wd, comm=None):
    T, F = a.shape
    D = x.shape[1]
    B = gate.shape[0]
    tm = _tile(TOKEN_TILE, T // B)
    tps = (T // B) // tm

    def body(a_ref, x_ref, gate_ref, wd_ref, xo_ref, f_ref):
        f = _dot(a_ref[...], wd_ref[...])
        f_ref[...] = f.astype(BF16)
        xo_ref[...] = x_ref[...] + 0.5 * gate_ref[0] * f

    return _call(
        body, "ffn_down", (T // tm,),
        [
            pl.BlockSpec((tm, F), lambda i: (i, 0)),
            pl.BlockSpec((tm, D), lambda i: (i, 0)),
            pl.BlockSpec((1, 1, D), lambda i: (i // tps, 0, 0)),
            pl.BlockSpec((F, D), lambda i: (0, 0)),
        ],
        [pl.BlockSpec((tm, D), lambda i: (i, 0)), pl.BlockSpec((tm, D), lambda i: (i, 0))],
        [jax.ShapeDtypeStruct((T, D), F32), jax.ShapeDtypeStruct((T, D), BF16)],
        [],
        (a, x, gate, wd), comm)


def _ffn_bwd_down(dxo, gu, gate, wd, comm=None):
    T, D = dxo.shape
    F = wd.shape[0]
    B = gate.shape[0]
    tm = _tile(TOKEN_TILE, T // B)
    tps = (T // B) // tm
    slabs = _slabs(F, FF_SLAB)

    def body(dxo_ref, gu_ref, gate_ref, wd_ref, dgu_ref):
        df = (0.5 * gate_ref[0] * dxo_ref[...]).astype(BF16)
        nxt = _dot_nt(df, wd_ref[slabs[0], :])
        for j, s in enumerate(slabs):
            da = nxt
            if j + 1 < len(slabs):
                nxt = _dot_nt(df, wd_ref[slabs[j + 1], :])
            g = gu_ref[0, :, s].astype(F32)
            sg = _sigmoid_fast(g)
            t = g * sg
            dgu_ref[1, :, s] = (da * t).astype(BF16)
            dgu_ref[0, :, s] = (da * gu_ref[1, :, s].astype(F32) * (sg + t - t * sg)).astype(BF16)

    return _call(
        body, "ffn_bwd_down", (T // tm,),
        [
            pl.BlockSpec((tm, D), lambda i: (i, 0)),
            pl.BlockSpec((2, tm, F), lambda i: (0, i, 0)),
            pl.BlockSpec((1, 1, D), lambda i: (i // tps, 0, 0)),
            pl.BlockSpec((F, D), lambda i: (0, 0), pipeline_mode=pl.Buffered(1)),
        ],
        [pl.BlockSpec((2, tm, F), lambda i: (0, i, 0))],
        [jax.ShapeDtypeStruct((2, T, F), BF16)],
        [],
        (dxo, gu, gate, wd), comm)


def _ffn_bwd_up(dxo, x, dgu, f, gain, sh, sc, gate, wgu, comm=None):
    T, D = x.shape
    F = wgu.shape[1] // 2
    B = sc.shape[0]
    tm = _tile(TOKEN_TILE, T // B)
    tps = (T // B) // tm

    def body(dxo_ref, x_ref, dgu_ref, f_ref, gain_ref, sh_ref, sc_ref, gate_ref, w_ref,
             dx_ref, h_ref, df_ref, dsc_ref, dsh_ref, dgain_ref, dgate_ref):
        i = pl.program_id(0)
        first_of_seq = (i % tps) == 0
        gain = gain_ref[...]
        sc = sc_ref[0]
        halves = _slabs(tm, tm // 2)

        def dots(r):
            return _dot_nt(dgu_ref[0, r, :], w_ref[:, 0:F]) + _dot_nt(dgu_ref[1, r, :], w_ref[:, F:])

        nxt = dots(halves[0])
        sums = None
        for j, r in enumerate(halves):
            dh = nxt
            if j + 1 < len(halves):
                nxt = dots(halves[j + 1])
            dxo = dxo_ref[r, :]
            x = x_ref[r, :]
            dx, dsc, dsh, dgain = _norm_mod_bwd(x, dh, gain, sc)
            dx_ref[r, :] = dxo + dx
            h_ref[r, :] = (_rms(x)[0] * gain * (1.0 + sc) + sh_ref[0]).astype(BF16)
            df_ref[r, :] = (0.5 * gate_ref[0] * dxo).astype(BF16)
            part = (dsc, dsh, dgain, 0.5 * jnp.sum(dxo * f_ref[r, :].astype(F32), axis=0, keepdims=True))
            sums = part if sums is None else tuple(a + b for a, b in zip(sums, part))
        _acc(dsc_ref.at[0], first_of_seq, sums[0])
        _acc(dsh_ref.at[0], first_of_seq, sums[1])
        _acc(dgain_ref, i == 0, sums[2])
        _acc(dgate_ref.at[0], first_of_seq, sums[3])

    seq = lambda i: (i // tps, 0, 0)
    row = lambda i: (i, 0)
    return _call(
        body, "ffn_bwd_up", (T // tm,),
        [
            pl.BlockSpec((tm, D), row),
            pl.BlockSpec((tm, D), row),
            pl.BlockSpec((2, tm, F), lambda i: (0, i, 0)),
            pl.BlockSpec((tm, D), row),
            pl.BlockSpec((1, D), lambda i: (0, 0)),
            pl.BlockSpec((1, 1, D), seq),
            pl.BlockSpec((1, 1, D), seq),
            pl.BlockSpec((1, 1, D), seq),
            pl.BlockSpec((D, 2 * F), lambda i: (0, 0), pipeline_mode=pl.Buffered(1)),
        ],
        [
            pl.BlockSpec((tm, D), row),
            pl.BlockSpec((tm, D), row),
            pl.BlockSpec((tm, D), row),
            pl.BlockSpec((1, 1, D), seq),
            pl.BlockSpec((1, 1, D), seq),
            pl.BlockSpec((1, D), lambda i: (0, 0)),
            pl.BlockSpec((1, 1, D), seq),
        ],
        [
            jax.ShapeDtypeStruct((T, D), F32),
            jax.ShapeDtypeStruct((T, D), BF16),
            jax.ShapeDtypeStruct((T, D), BF16),
            jax.ShapeDtypeStruct((B, 1, D), F32),
            jax.ShapeDtypeStruct((B, 1, D), F32),
            jax.ShapeDtypeStruct((1, D), F32),
            jax.ShapeDtypeStruct((B, 1, D), F32),
        ],
        [],
        (dxo, x, dgu, f, gain, sh, sc, gate, wgu), comm)


def _wgrad(a, b, tmm, tn, col_major, name, tokens=WGRAD_TOKENS, comm=None):
    T, M = a.shape
    nb, _, Nb = b.shape
    N = nb * Nb
    tk = _tile(tokens, T)
    span = 2 if col_major else 1
    wide = span * tn
    npb = Nb // wide
    assert M % tmm == 0 and Nb % wide == 0
    if col_major:
        assert tmm == M
        shape = (N // tn, 2, M // 2, tn)
        out_spec = pl.BlockSpec((span, 2, M // 2, tn), lambda i, j, t: (j, 0, 0, 0))
    else:
        shape = (M // tmm, tmm, N)
        out_spec = pl.BlockSpec((None, tmm, tn), lambda i, j, t: (i, 0, j))

    def body(a_ref, b_ref, o_ref):
        t = pl.program_id(2)
        res = _dot_tn(a_ref[...], b_ref[...])
        if col_major:
            for s in range(span):
                for h in range(2):
                    _acc(o_ref.at[s, h], t == 0, res[h * (M // 2):(h + 1) * (M // 2), s * tn:(s + 1) * tn])
        else:
            _acc(o_ref, t == 0, res)

    return _call(
        body, name, (M // tmm, N // wide, T // tk),
        [
            pl.BlockSpec((tk, tmm), lambda i, j, t: (t, i)),
            pl.BlockSpec((None, tk, wide), lambda i, j, t: (j // npb, t, j % npb)),
        ],
        [out_spec], [jax.ShapeDtypeStruct(shape, F32)], [],
        (a, b), comm)


def _mixin_fwd(x, gain, sh, sc, win, comm=None):
    T, D = x.shape
    P = win.shape[1]
    B = sh.shape[0]
    tm = _tile(TOKEN_TILE, T // B)
    tps = (T // B) // tm

    def body(x_ref, gain_ref, sh_ref, sc_ref, w_ref, proj_ref, h_ref):
        xh, _ = _rms(x_ref[...])
        h = (xh * gain_ref[...] * (1.0 + sc_ref[0]) + sh_ref[0]).astype(BF16)
        h_ref[...] = h
        proj_ref[...] = _dot(h, w_ref[...])

    seq = lambda i: (i // tps, 0, 0)
    return _call(
        body, "mixin_fwd", (T // tm,),
        [
            pl.BlockSpec((tm, D), lambda i: (i, 0)),
            pl.BlockSpec((1, D), lambda i: (0, 0)),
            pl.BlockSpec((1, 1, D), seq),
            pl.BlockSpec((1, 1, D), seq),
            pl.BlockSpec((D, P), lambda i: (0, 0)),
        ],
        [pl.BlockSpec((tm, P), lambda i: (i, 0)), pl.BlockSpec((tm, D), lambda i: (i, 0))],
        [jax.ShapeDtypeStruct((T, P), F32), jax.ShapeDtypeStruct((T, D), BF16)],
        [],
        (x, gain, sh, sc, win), comm)


def _mixin_bwd(dxo, x, dproj, gain, sc, win, comm=None):
    T, D = x.shape
    P = win.shape[1]
    B = sc.shape[0]
    tm = _tile(TOKEN_TILE, T // B)
    tps = (T // B) // tm

    def body(dxo_ref, x_ref, dp_ref, gain_ref, sc_ref, w_ref, dx_ref, dsc_ref, dsh_ref, dgain_ref):
        i = pl.program_id(0)
        first_of_seq = (i % tps) == 0
        halves = _slabs(tm, tm // 2)
        nxt = _dot_nt(dp_ref[halves[0], :], w_ref[...])
        sums = None
        for j, r in enumerate(halves):
            dh = nxt
            if j + 1 < len(halves):
                nxt = _dot_nt(dp_ref[halves[j + 1], :], w_ref[...])
            part = _norm_mod_bwd(x_ref[r, :], dh, gain_ref[...], sc_ref[0])
            dx_ref[r, :] = dxo_ref[r, :] + part[0]
            sums = part[1:] if sums is None else tuple(a + b for a, b in zip(sums, part[1:]))
        _acc(dsc_ref.at[0], first_of_seq, sums[0])
        _acc(dsh_ref.at[0], first_of_seq, sums[1])
        _acc(dgain_ref, i == 0, sums[2])

    seq = lambda i: (i // tps, 0, 0)
    row = lambda i: (i, 0)
    return _call(
        body, "mixin_bwd", (T // tm,),
        [
            pl.BlockSpec((tm, D), row),
            pl.BlockSpec((tm, D), row),
            pl.BlockSpec((tm, P), row),
            pl.BlockSpec((1, D), lambda i: (0, 0)),
            pl.BlockSpec((1, 1, D), seq),
            pl.BlockSpec((D, P), lambda i: (0, 0)),
        ],
        [
            pl.BlockSpec((tm, D), row),
            pl.BlockSpec((1, 1, D), seq),
            pl.BlockSpec((1, 1, D), seq),
            pl.BlockSpec((1, D), lambda i: (0, 0)),
        ],
        [
            jax.ShapeDtypeStruct((T, D), F32),
            jax.ShapeDtypeStruct((B, 1, D), F32),
            jax.ShapeDtypeStruct((B, 1, D), F32),
            jax.ShapeDtypeStruct((1, D), F32),
        ],
        [],
        (dxo, x, dproj, gain, sc, win), comm)


def _head_mean(z, pmat, exact=True):
    hi = z.astype(BF16)
    if not exact:
        return _dot(hi, pmat)
    lo = (z - hi.astype(F32)).astype(BF16)
    return _dot(hi, pmat) + _dot(lo, pmat)


def _gelu_parts(x):
    cdf = 0.5 * (1.0 + lax.erf(x * (1.0 / math.sqrt(2.0))))
    return x * cdf, cdf


def _gelu_grad(x, cdf):
    return cdf + x * jnp.exp(-0.5 * x * x) * (1.0 / math.sqrt(2.0 * math.pi))


LANES = 128


def _head_blocks(da):
    hd = da // N_HEADS
    lb = min(LANES, da)
    col = lax.broadcasted_iota(jnp.int32, (1, lb), 1)
    return lb, lb // hd, da // lb, [(col >= h * hd) & (col < (h + 1) * hd) for h in range(lb // hd)]


def _mix_heads(w_stack, v, da):
    lb, hpb, nb, masks = _head_blocks(da)
    outs = []
    for b in range(nb):
        res = _dot(w_stack[b * hpb * CHUNK:(b + 1) * hpb * CHUNK], v[:, b * lb:(b + 1) * lb])
        out = res[0:CHUNK]
        for h in range(1, hpb):
            out = jnp.where(masks[h], res[h * CHUNK:(h + 1) * CHUNK], out)
        outs.append(out)
    return outs[0] if nb == 1 else jnp.concatenate(outs, axis=1)


def _mix_heads_grad(dm, v, da):
    lb, hpb, nb, masks = _head_blocks(da)
    outs = []
    for b in range(nb):
        dmb = dm[:, b * lb:(b + 1) * lb]
        stack = jnp.concatenate([jnp.where(masks[h], dmb, jnp.zeros_like(dmb)) for h in range(hpb)], axis=0)
        outs.append(_dot_nt(stack, v[:, b * lb:(b + 1) * lb]))
    return outs[0] if nb == 1 else jnp.concatenate(outs, axis=0)


def _causal_stack(w, transposed):
    r = lax.broadcasted_iota(jnp.int32, w.shape, 0) % CHUNK
    c = lax.broadcasted_iota(jnp.int32, w.shape, 1)
    keep = (c >= r) if transposed else (c <= r)
    return jnp.where(keep, w, 0.0)


def _mix_core_forward(proj, zprev, prm, da, db, saved=None):
    n = proj.shape[0]
    ua = proj[:, 0:da]
    va = proj[:, da:2 * da]
    bg = proj[:, 2 * da:2 * da + db]
    cg = proj[:, 2 * da + db:2 * da + 2 * db]
    xb = proj[:, 2 * da + 2 * db:]
    if saved is None:
        ug, ucdf = _gelu_parts(ua)
        vg, vcdf = _gelu_parts(va)
        zc = vg - _head_mean(vg, prm["pmat"])
        rs = lax.rsqrt(_head_mean(zc * zc, prm["pmat"], exact=False) + EPS)
        vhat = zc * rs
        vln = (vhat * prm["lng"] + prm["lnb"]).astype(BF16)
        wst = _causal_stack(prm["wst"], False).astype(BF16)
        mixed = [_mix_heads(wst, vln[j * CHUNK:(j + 1) * CHUNK], da) + prm["bias"] for j in range(n // CHUNK)]
        mixed = mixed[0] if len(mixed) == 1 else jnp.concatenate(mixed, axis=0)
    else:
        ucdf, vcdf, vhat, rs, mixed = [saved[k].astype(F32) for k in range(5)]
        ug = ua * ucdf
        vln = (vhat * prm["lng"] + prm["lnb"]).astype(BF16)
    ya = ug * mixed
    z = cg * xb
    row = lax.broadcasted_iota(jnp.int32, z.shape, 0)
    z1 = jnp.where(row == 0, zprev[7:8], pltpu.roll(z, 1, 0))
    z2 = jnp.where(row == 0, zprev[6:7], jnp.where(row == 1, zprev[7:8], pltpu.roll(z, 2, 0)))
    cw = prm["convw"]
    conv = z2 * cw[0:1] + z1 * cw[1:2] + z * cw[2:3]
    yb = bg * conv
    yah, ra = _rms(ya)
    ybh, rb = _rms(yb)
    return dict(ua=ua, va=va, bg=bg, cg=cg, xb=xb, ug=ug, ucdf=ucdf, vcdf=vcdf, rs=rs, vhat=vhat, vln=vln,
                mixed=mixed, z=z, z1=z1, z2=z2, conv=conv, yah=yah, ra=ra, ybh=ybh, rb=rb)


def _mix_params(lng_ref, lnb_ref, wst_ref, bias_ref, pmat_ref, convw_ref):
    return dict(lng=lng_ref[...], lnb=lnb_ref[...], wst=wst_ref[...], bias=bias_ref[...],
                pmat=pmat_ref[...], convw=convw_ref[...])


def _mix_core_fwd(proj, x, gate, wout, lng, lnb, wst, bias, pmat, convw, og, comm=None):
    T, P = proj.shape
    D = x.shape[1]
    B = gate.shape[0]
    da = lng.shape[1]
    db = convw.shape[1]
    tm = _tile(MIX_TILE, T // B)
    tps = (T // B) // tm

    def body(proj_ref, x_ref, gate_ref, wout_ref, lng_ref, lnb_ref, wst_ref, bias_ref, pmat_ref, convw_ref,
             og_ref, xo_ref, yn_ref, sv_ref, halo):
        i = pl.program_id(0)

        @pl.when((i % tps) == 0)
        def _():
            halo[...] = jnp.zeros_like(halo)

        prm = _mix_params(lng_ref, lnb_ref, wst_ref, bias_ref, pmat_ref, convw_ref)
        r = _mix_core_forward(proj_ref[...], halo[...], prm, da, db)
        halo[...] = r["z"][tm - 8:tm]
        for k, name in enumerate(("ucdf", "vcdf", "vhat", "rs", "mixed")):
            sv_ref[k] = r[name].astype(BF16)
        og = og_ref[...]
        yn_ref[:, 0:da] = (r["yah"] * og[:, 0:da]).astype(BF16)
        yn_ref[:, da:] = (r["ybh"] * og[:, da:]).astype(BF16)
        xo_ref[...] = x_ref[...] + gate_ref[0] * _dot(yn_ref[...], wout_ref[...])

    full = lambda a: pl.BlockSpec(a.shape, lambda i: (0,) * a.ndim)
    return _call(
        body, "mix_core_fwd", (T // tm,),
        [
            pl.BlockSpec((tm, P), lambda i: (i, 0)),
            pl.BlockSpec((tm, D), lambda i: (i, 0)),
            pl.BlockSpec((1, 1, D), lambda i: (i // tps, 0, 0)),
            full(wout), full(lng), full(lnb), full(wst), full(bias), full(pmat), full(convw), full(og),
        ],
        [pl.BlockSpec((tm, D), lambda i: (i, 0)), pl.BlockSpec((tm, D), lambda i: (i, 0)),
         pl.BlockSpec((5, tm, da), lambda i: (0, i, 0))],
        [jax.ShapeDtypeStruct((T, D), F32), jax.ShapeDtypeStruct((T, D), BF16),
         jax.ShapeDtypeStruct((5, T, da), BF16)],
        [pltpu.VMEM((8, db), F32)],
        (proj, x, gate, wout, lng, lnb, wst, bias, pmat, convw, og), comm)


def _mix_core_bwd(proj, sv, dxo, gate, wout, lng, lnb, wstt, pmat, convw, og, comm=None):
    T, P = proj.shape
    D = dxo.shape[1]
    B = gate.shape[0]
    da = lng.shape[1]
    db = convw.shape[1]
    assert da == db and P == 2 * da + 3 * db
    tm = _tile(MIX_TILE, T // B)
    tps = (T // B) // tm
    nt = T // tm
    hd = da // N_HEADS

    def body(proj_ref, cgp_ref, xbp_ref, sv_ref, dxo_ref, gate_ref, wout_ref, lng_ref, lnb_ref, wstt_ref,
             pmat_ref, convw_ref, og_ref,
             dproj_ref, do_ref, dgate_ref, dog_ref, dwst_ref, dbias_ref, dlng_ref, dlnb_ref, dconvw_ref, carry):
        i = pl.program_id(0)
        ri = nt - 1 - i
        first = i == 0
        end_of_seq = (ri % tps) == tps - 1
        start_of_seq = (ri % tps) == 0

        @pl.when(end_of_seq)
        def _():
            carry[...] = jnp.zeros_like(carry)

        prm = dict(lng=lng_ref[...], lnb=lnb_ref[...], pmat=pmat_ref[...], convw=convw_ref[...])
        zprev = jnp.where(start_of_seq, 0.0, cgp_ref[...] * xbp_ref[...])
        r = _mix_core_forward(proj_ref[...], zprev, prm, da, db, saved=sv_ref)
        og = og_ref[...]
        pmat = prm["pmat"]

        yn = jnp.concatenate([(r["yah"] * og[:, 0:da]).astype(BF16), (r["ybh"] * og[:, da:]).astype(BF16)], axis=1)
        dxo = dxo_ref[...]
        o = _dot(yn, wout_ref[...])
        _acc(dgate_ref.at[0], end_of_seq, jnp.sum(dxo * o, axis=0, keepdims=True))
        d_o = (gate_ref[0] * dxo).astype(BF16)
        do_ref[...] = d_o
        dyn = _dot_nt(d_o, wout_ref[...])

        def rms_bwd(dyn_g, yh, rr, og_g):
            dog_g = jnp.sum(dyn_g * yh, axis=0, keepdims=True)
            dyh = dyn_g * og_g
            return rr * (dyh - yh * jnp.mean(dyh * yh, axis=-1, keepdims=True)), dog_g

        dya, dog_a = rms_bwd(dyn[:, 0:da], r["yah"], r["ra"], og[:, 0:da])
        dyb, dog_b = rms_bwd(dyn[:, da:], r["ybh"], r["rb"], og[:, da:])
        _acc(dog_ref, first, jnp.concatenate([dog_a, dog_b], axis=1))

        dug = dya * r["mixed"]
        dmixed = dya * r["ug"]
        wstt_b = _causal_stack(wstt_ref[...], True).astype(BF16)
        dbias = jnp.zeros((CHUNK, da), F32)
        dwst = jnp.zeros((N_HEADS * CHUNK, CHUNK), F32)
        dvln = []
        for j in range(tm // CHUNK):
            dm = dmixed[j * CHUNK:(j + 1) * CHUNK]
            dbias = dbias + dm
            dmb = dm.astype(BF16)
            dwst = dwst + _mix_heads_grad(dmb, r["vln"][j * CHUNK:(j + 1) * CHUNK], da)
            dvln.append(_mix_heads(wstt_b, dmb, da))
        dvln = dvln[0] if len(dvln) == 1 else jnp.concatenate(dvln, axis=0)
        _acc(dbias_ref, first, dbias)
        _acc(dwst_ref, first, dwst)
        _acc(dlng_ref, first, jnp.sum(dvln * r["vhat"], axis=0, keepdims=True))
        _acc(dlnb_ref, first, jnp.sum(dvln, axis=0, keepdims=True))
        dvhat = dvln * prm["lng"]
        dvg = r["rs"] * (dvhat - _head_mean(dvhat, pmat, exact=False)
                         - r["vhat"] * _head_mean(dvhat * r["vhat"], pmat, exact=False))
        dproj_ref[:, 0:da] = (dug * _gelu_grad(r["ua"], r["ucdf"])).astype(BF16)
        dproj_ref[:, da:2 * da] = (dvg * _gelu_grad(r["va"], r["vcdf"])).astype(BF16)

        dproj_ref[:, 2 * da:2 * da + db] = (dyb * r["conv"]).astype(BF16)
        dconv = dyb * r["bg"]
        dcw = jnp.concatenate([
            jnp.sum(dconv * r["z2"], axis=0, keepdims=True),
            jnp.sum(dconv * r["z1"], axis=0, keepdims=True),
            jnp.sum(dconv * r["z"], axis=0, keepdims=True),
            jnp.zeros((5, db), F32)], axis=0)
        _acc(dconvw_ref, first, dcw)
        nxt = carry[...]
        row = lax.broadcasted_iota(jnp.int32, dconv.shape, 0)
        dc1 = jnp.where(row == tm - 1, nxt[0:1], pltpu.roll(dconv, tm - 1, 0))
        dc2 = jnp.where(row == tm - 2, nxt[0:1], jnp.where(row == tm - 1, nxt[1:2], pltpu.roll(dconv, tm - 2, 0)))
        carry[...] = dconv[0:8]
        cw = prm["convw"]
        dz = dconv * cw[2:3] + dc1 * cw[1:2] + dc2 * cw[0:1]
        dproj_ref[:, 2 * da + db:2 * da + 2 * db] = (dz * r["xb"]).astype(BF16)
        dproj_ref[:, 2 * da + 2 * db:] = (dz * r["cg"]).astype(BF16)

        @pl.when(i == nt - 1)
        def _():
            dwst_ref[...] = _causal_stack(dwst_ref[...], False)
            dbias_ref[...] = _head_mean(dbias_ref[...], pmat) * float(hd)

    full = lambda a: pl.BlockSpec(a.shape, lambda i: (0,) * a.ndim)
    const = lambda i: (0, 0)
    rev = lambda i: (nt - 1 - i, 0)
    prev8 = lambda col: (lambda i: (jnp.maximum((nt - 1 - i) * (tm // 8) - 1, 0), col))
    return _call(
        body, "mix_core_bwd", (nt,),
        [
            pl.BlockSpec((tm, P), rev),
            pl.BlockSpec((8, db), prev8((2 * da + db) // db)),
            pl.BlockSpec((8, db), prev8((2 * da + 2 * db) // db)),
            pl.BlockSpec((5, tm, da), lambda i: (0, nt - 1 - i, 0)),
            pl.BlockSpec((tm, D), rev),
            pl.BlockSpec((1, 1, D), lambda i: ((nt - 1 - i) // tps, 0, 0)),
            full(wout), full(lng), full(lnb), full(wstt), full(pmat), full(convw), full(og),
        ],
        [
            pl.BlockSpec((tm, P), rev),
            pl.BlockSpec((tm, D), rev),
            pl.BlockSpec((1, 1, D), lambda i: ((nt - 1 - i) // tps, 0, 0)),
            pl.BlockSpec((1, D), const),
            pl.BlockSpec((N_HEADS * CHUNK, CHUNK), const),
            pl.BlockSpec((CHUNK, da), const),
            pl.BlockSpec((1, da), const),
            pl.BlockSpec((1, da), const),
            pl.BlockSpec((8, db), const),
        ],
        [
            jax.ShapeDtypeStruct((T, P), BF16),
            jax.ShapeDtypeStruct((T, D), BF16),
            jax.ShapeDtypeStruct((B, 1, D), F32),
            jax.ShapeDtypeStruct((1, D), F32),
            jax.ShapeDtypeStruct((N_HEADS * CHUNK, CHUNK), F32),
            jax.ShapeDtypeStruct((CHUNK, da), F32),
            jax.ShapeDtypeStruct((1, da), F32),
            jax.ShapeDtypeStruct((1, da), F32),
            jax.ShapeDtypeStruct((8, db), F32),
        ],
        [pltpu.VMEM((8, db), F32)],
        (proj, proj, proj, sv, dxo, gate, wout, lng, lnb, wstt, pmat, convw, og), comm)


def _loss_head(x, target, gain):
    T, D = x.shape
    tm = _tile(TOKEN_TILE, T)

    def body(x_ref, t_ref, gain_ref, dx_ref, loss_ref, dgain_ref):
        first = pl.program_id(0) == 0
        xh, r = _rms(x_ref[...])
        gain = gain_ref[...]
        err = xh * gain - t_ref[...]
        _acc(loss_ref, first, jnp.zeros((8, 128), F32) + 0.5 * jnp.sum(err * err) / D)
        dout = err * (1.0 / D)
        _acc(dgain_ref, first, jnp.sum(dout * xh, axis=0, keepdims=True))
        dy = dout * gain
        dx_ref[...] = r * (dy - xh * jnp.mean(dy * xh, axis=-1, keepdims=True))

    return _call(
        body, "loss_head", (T // tm,),
        [
            pl.BlockSpec((tm, D), lambda i: (i, 0)),
            pl.BlockSpec((tm, D), lambda i: (i, 0)),
            pl.BlockSpec((1, D), lambda i: (0, 0)),
        ],
        [
            pl.BlockSpec((tm, D), lambda i: (i, 0)),
            pl.BlockSpec((8, 128), lambda i: (0, 0)),
            pl.BlockSpec((1, D), lambda i: (0, 0)),
        ],
        [
            jax.ShapeDtypeStruct((T, D), F32),
            jax.ShapeDtypeStruct((8, 128), F32),
            jax.ShapeDtypeStruct((1, D), F32),
        ],
        [],
        (x, target, gain))[0]


def _ada_fwd(c_all, ada_w, ada_b):
    n, D = c_all.shape
    L, _, sa = ada_w.shape
    tn = _tile(768, sa)

    def body(c_ref, w_ref, b_ref, act_ref, o_ref):
        c = c_ref[...]
        act = (c * _sigmoid(c)).astype(BF16)
        act_ref[...] = act
        o_ref[...] = _dot(act, w_ref[...].astype(BF16)) + b_ref[...]

    return _call(
        body, "ada_fwd", (L, sa // tn),
        [
            pl.BlockSpec((n, D), lambda l, j: (0, 0)),
            pl.BlockSpec((None, D, tn), lambda l, j: (l, 0, j)),
            pl.BlockSpec((None, 1, tn), lambda l, j: (l, 0, j)),
        ],
        [
            pl.BlockSpec((n, D), lambda l, j: (0, 0)),
            pl.BlockSpec((None, n, tn), lambda l, j: (l, 0, j)),
        ],
        [jax.ShapeDtypeStruct((n, D), BF16), jax.ShapeDtypeStruct((L, n, sa), F32)],
        [],
        (c_all, ada_w, ada_b))[0]


def _ada_bwd(c_act, d_ada, comm=None):
    n, D = c_act.shape
    L, _, sa = d_ada.shape
    tn = _tile(768, sa)

    def body(c_ref, d_ref, o_ref):
        o_ref[...] = _dot_tn(c_ref[...], d_ref[...])

    return _call(
        body, "ada_bwd", (L, sa // tn),
        [pl.BlockSpec((n, D), lambda l, j: (0, 0)), pl.BlockSpec((None, n, tn), lambda l, j: (l, 0, j))],
        [pl.BlockSpec((None, D, tn), lambda l, j: (l, 0, j))],
        [jax.ShapeDtypeStruct((L, D, sa), F32)],
        [],
        (c_act, d_ada), comm)


def _colsum(a):
    L, n, C = a.shape

    def body(a_ref, o_ref):
        o_ref[...] = jnp.sum(a_ref[...], axis=0, keepdims=True)

    return _call(
        body, "colsum", (L,),
        [pl.BlockSpec((None, n, C), lambda l: (l, 0, 0))],
        [pl.BlockSpec((None, 1, C), lambda l: (l, 0, 0))],
        [jax.ShapeDtypeStruct((L, 1, C), F32)],
        [],
        (a,))[0][0]


def _row_tile(rows, cols, nbuf):
    budget = VMEM_LIMIT // 3 // (2 * nbuf * 4 * cols)
    t = rows
    while t > max(budget, 8) and t % 2 == 0 and (t // 2) % 8 == 0:
        t //= 2
    return t


def _pair_sum(g, recv, core):
    n, _, R, C = g.shape
    tr = _row_tile(R, C, 3)

    def body(core_ref, g_ref, r_ref, o_ref):
        o_ref[...] = (g_ref[...] + r_ref[...]).astype(BF16)

    return pl.pallas_call(
        body,
        name="pair_sum",
        grid_spec=pltpu.PrefetchScalarGridSpec(
            num_scalar_prefetch=1,
            grid=(n, R // tr),
            in_specs=[
                pl.BlockSpec((None, None, tr, C), lambda i, r, core_ref: (i, core_ref[0], r, 0)),
                pl.BlockSpec((None, tr, C), lambda i, r, core_ref: (i, r, 0)),
            ],
            out_specs=pl.BlockSpec((None, tr, C), lambda i, r, core_ref: (i, r, 0)),
        ),
        out_shape=jax.ShapeDtypeStruct((n, R, C), BF16),
        compiler_params=pltpu.CompilerParams(dimension_semantics=("arbitrary", "arbitrary"),
                                             vmem_limit_bytes=VMEM_LIMIT),
    )(core, g, recv)


def _chip_sum(q, core, l, n_layers, prev):
    nq, R, C = q.shape
    tr = _row_tile(R, C, 4)

    def body(core_ref, q_ref, *rest):
        o_ref = rest[-1]
        s = q_ref[0].astype(F32)
        for j in range(1, nq):
            s = s + q_ref[j].astype(F32)
        o_ref[...] = s

    in_specs = [pl.BlockSpec((nq, tr, C), lambda r, core_ref: (0, r, 0))]
    args = [core, q]
    aliases = {}
    if prev is not None:
        in_specs.append(ANY)
        args.append(prev)
        aliases = {2: 0}
    return pl.pallas_call(
        body,
        name="chip_sum",
        grid_spec=pltpu.PrefetchScalarGridSpec(
            num_scalar_prefetch=1,
            grid=(R // tr,),
            in_specs=in_specs,
            out_specs=pl.BlockSpec((None, None, tr, C), lambda r, core_ref: (l, core_ref[0], r, 0)),
        ),
        out_shape=jax.ShapeDtypeStruct((n_layers, 2, R, C), F32),
        input_output_aliases=aliases,
        compiler_params=pltpu.CompilerParams(dimension_semantics=("arbitrary",), vmem_limit_bytes=VMEM_LIMIT),
    )(*args)


def _sum_blocks(a, n):
    M = a.shape[0] // n
    C = a.shape[1]

    def body(a_ref, o_ref):
        s = a_ref[0:M]
        for j in range(1, n):
            s = s + a_ref[j * M:(j + 1) * M]
        o_ref[...] = s

    return pl.pallas_call(
        body,
        name="sum_blocks",
        out_shape=jax.ShapeDtypeStruct((M, C), F32),
        compiler_params=pltpu.CompilerParams(vmem_limit_bytes=VMEM_LIMIT),
    )(a)


def _adamw(w, g, m, v, emit_grad=False):
    R, C = w.shape
    n_out = 4 if emit_grad else 3
    tr = _row_tile(R, C, 4 + n_out) if R % 8 == 0 else R

    def body(w_ref, g_ref, m_ref, v_ref, d_ref, nm_ref, nv_ref, *g_out):
        g = g_ref[...]
        m = ADAM_B1 * m_ref[...] + (1.0 - ADAM_B1) * g
        v = ADAM_B2 * v_ref[...] + (1.0 - ADAM_B2) * (g * g)
        m_hat = m / (1.0 - ADAM_B1 ** ADAM_STEP)
        v_hat = v / (1.0 - ADAM_B2 ** ADAM_STEP)
        d_ref[...] = -ADAM_LR * (m_hat / (jnp.sqrt(v_hat) + ADAM_EPS) + ADAM_WD * w_ref[...])
        nm_ref[...] = m
        nv_ref[...] = v
        if emit_grad:
            g_out[0][...] = g

    spec = pl.BlockSpec((tr, C), lambda i: (i, 0))
    return _call(body, "adamw", (R // tr,), [spec] * 4, [spec] * n_out, [jax.ShapeDtypeStruct((R, C), F32)] * n_out,
                 [], (w, g, m, v))[0]


def kernel(x, c, ada_w, ada_b, norm_ffn1_g, ffn1_w_gu, ffn1_w_down, norm_mix_g, mix_w_in, sgu_ln_g, sgu_ln_b, sgu_w_s, sgu_b, conv_w, out_norm_g, mix_w_out, norm_ffn2_g, ffn2_w_gu, ffn2_w_down, final_norm_g, loss_target, m_ada_w, m_ada_b, m_norm_ffn1_g, m_ffn1_w_gu, m_ffn1_w_down, m_norm_mix_g, m_mix_w_in, m_sgu_ln_g, m_sgu_ln_b, m_sgu_w_s, m_sgu_b, m_conv_w, m_out_norm_g, m_mix_w_out, m_norm_ffn2_g, m_ffn2_w_gu, m_ffn2_w_down, m_final_norm_g, v_ada_w, v_ada_b, v_norm_ffn1_g, v_ffn1_w_gu, v_ffn1_w_down, v_norm_mix_g, v_mix_w_in, v_sgu_ln_g, v_sgu_ln_b, v_sgu_w_s, v_sgu_b, v_conv_w, v_out_norm_g, v_mix_w_out, v_norm_ffn2_g, v_ffn2_w_gu, v_ffn2_w_down, v_final_norm_g):
    weights = dict(ada_w=ada_w, ada_b=ada_b, norm_ffn1_g=norm_ffn1_g, ffn1_w_gu=ffn1_w_gu, ffn1_w_down=ffn1_w_down,
                   norm_mix_g=norm_mix_g, mix_w_in=mix_w_in, sgu_ln_g=sgu_ln_g, sgu_ln_b=sgu_ln_b, sgu_w_s=sgu_w_s,
                   sgu_b=sgu_b, conv_w=conv_w, out_norm_g=out_norm_g, mix_w_out=mix_w_out, norm_ffn2_g=norm_ffn2_g,
                   ffn2_w_gu=ffn2_w_gu, ffn2_w_down=ffn2_w_down, final_norm_g=final_norm_g)
    m_in = dict(ada_w=m_ada_w, ada_b=m_ada_b, norm_ffn1_g=m_norm_ffn1_g, ffn1_w_gu=m_ffn1_w_gu,
                ffn1_w_down=m_ffn1_w_down, norm_mix_g=m_norm_mix_g, mix_w_in=m_mix_w_in, sgu_ln_g=m_sgu_ln_g,
                sgu_ln_b=m_sgu_ln_b, sgu_w_s=m_sgu_w_s, sgu_b=m_sgu_b, conv_w=m_conv_w, out_norm_g=m_out_norm_g,
                mix_w_out=m_mix_w_out, norm_ffn2_g=m_norm_ffn2_g, ffn2_w_gu=m_ffn2_w_gu, ffn2_w_down=m_ffn2_w_down,
                final_norm_g=m_final_norm_g)
    v_in = dict(ada_w=v_ada_w, ada_b=v_ada_b, norm_ffn1_g=v_norm_ffn1_g, ffn1_w_gu=v_ffn1_w_gu,
                ffn1_w_down=v_ffn1_w_down, norm_mix_g=v_norm_mix_g, mix_w_in=v_mix_w_in, sgu_ln_g=v_sgu_ln_g,
                sgu_ln_b=v_sgu_ln_b, sgu_w_s=v_sgu_w_s, sgu_b=v_sgu_b, conv_w=v_conv_w, out_norm_g=v_out_norm_g,
                mix_w_out=v_mix_w_out, norm_ffn2_g=v_norm_ffn2_g, ffn2_w_gu=v_ffn2_w_gu, ffn2_w_down=v_ffn2_w_down,
                final_norm_g=v_final_norm_g)

    B, S, D = x.shape
    T = B * S
    L = ada_w.shape[0]
    F = ffn1_w_down.shape[1] * N_CHIP
    P = mix_w_in.shape[2] * N_CHIP
    DA = D // 2
    DB = D - DA
    HD = DA // N_HEADS
    SA = ada_w.shape[2]
    n_all = B * N_DEV
    mx, my, mc = _position()
    chip = 2 * mx + my
    dev = 2 * chip + mc
    core = jnp.reshape(mc, (1,)).astype(jnp.int32)

    big = ["ffn1_w_gu", "ffn1_w_down", "mix_w_in", "mix_w_out", "ffn2_w_gu", "ffn2_w_down"]
    col_sharded = dict(ffn1_w_gu=True, ffn1_w_down=False, mix_w_in=True, mix_w_out=False,
                       ffn2_w_gu=True, ffn2_w_down=False)
    shards = {k: weights[k].astype(BF16) for k in big}
    gather = lambda l, *names: _gather_comm([(shards[k], l, col_sharded[k]) for k in names])
    full = [dict() for _ in range(L)]

    def arrived(l, names, res):
        full[l].update(zip(names, res))

    n_cw = L * conv_w.shape[1]
    cw_block = jnp.pad(conv_w.reshape(n_cw, conv_w.shape[2]), ((0, 8 - n_cw), (0, 0)))
    c_all, cw_all = _comm_call(_merge(_all_gather_comm(c.reshape(8, B * D // 8)), _all_gather_comm(cw_block)),
                               "gather_c")
    c_all = c_all.reshape(n_all, D)
    cw_all = cw_all.reshape(N_CHIP, 2, 8, conv_w.shape[2])[:, 0, :n_cw]
    conv_full = jnp.transpose(cw_all.reshape(N_CHIP, L, conv_w.shape[1], conv_w.shape[2]), (1, 2, 0, 3))
    conv_full = conv_full.reshape(L, conv_w.shape[1], DB)
    ada_b_mine = lax.dynamic_slice_in_dim(ada_b, chip * SA, SA, axis=1).reshape(L, 1, SA)
    c_act, ada_part = _ada_fwd(c_all, ada_w, ada_b_mine)
    ada_all, first_w = _comm_call(_merge(_all_gather_comm(ada_part.reshape(L * n_all, SA)), gather(0, big[0])),
                                  "gather_first")
    arrived(0, big[:1], [first_w])
    ada_all = ada_all.reshape(N_CHIP, 2, L, n_all, SA)[:, 0]
    ada_all = jnp.transpose(ada_all, (1, 2, 0, 3)).reshape(L, n_all, N_CHIP * SA)
    ada = lax.dynamic_slice_in_dim(ada_all, dev * B, B, axis=1).reshape(L, B, N_MOD, 1, D)
    mods = [[ada[l, :, j] for j in range(N_MOD)] for l in range(L)]

    x0 = x.reshape(T, D)
    gains = lambda name, l: weights[name][l].reshape(1, D)
    hmask = jnp.repeat(jnp.eye(N_HEADS, dtype=F32), HD, axis=0)
    pmat = (jnp.repeat(hmask, HD, axis=1) / HD).astype(BF16)

    def mix_consts(l):
        lng = jnp.tile(sgu_ln_g[l], N_HEADS).reshape(1, DA)
        lnb = jnp.tile(sgu_ln_b[l], N_HEADS).reshape(1, DA)
        wst = sgu_w_s[l].reshape(N_HEADS * CHUNK, CHUNK)
        wstt = jnp.swapaxes(sgu_w_s[l], 1, 2).reshape(N_HEADS * CHUNK, CHUNK)
        bias = jnp.repeat(jnp.transpose(sgu_b[l]), HD, axis=1)
        return lng, lnb, wst, wstt, bias

    def fetch(fn, *args, bring=()):
        bring = [(l, k) for l, k in bring if l < L]
        comm = _gather_comm([(shards[k], l, col_sharded[k]) for l, k in bring]) if bring else None
        res, got = fn(*args, comm)
        for (l, k), a in zip(bring, got):
            full[l][k] = a
        return res

    saved = []
    xc = x0
    for l in range(L):
        sh1, sc1, g1, sh2, sc2, g2, sh3, sc3, g3 = mods[l]
        lng, lnb, wst, wstt, bias = mix_consts(l)
        w = full[l]
        own = l == 0
        gu1, a1 = fetch(_ffn_up, xc, gains("norm_ffn1_g", l), sh1, sc1, w["ffn1_w_gu"],
                        bring=[(l, "ffn1_w_down"), (l, "mix_w_in"), (l, "mix_w_out")] if own else [(l, "ffn2_w_gu")])
        xa, f1 = fetch(_ffn_down, a1, xc, g1, w["ffn1_w_down"], bring=[(l, "ffn2_w_down")])
        proj, h2 = fetch(_mixin_fwd, xa, gains("norm_mix_g", l), sh2, sc2, w["mix_w_in"], bring=[(l + 1, "mix_w_in")])
        xb, yn, sv = fetch(_mix_core_fwd, proj, xa, g2, w["mix_w_out"], lng, lnb, wst, bias, pmat, conv_full[l],
                           gains("out_norm_g", l), bring=[(l, "ffn2_w_gu")] if own else [])
        gu2, a2 = fetch(_ffn_up, xb, gains("norm_ffn2_g", l), sh3, sc3, w["ffn2_w_gu"],
                        bring=[(l + 1, "ffn1_w_gu"), (l + 1, "mix_w_out")])
        xd, f2 = fetch(_ffn_down, a2, xb, g3, w["ffn2_w_down"], bring=[(l + 1, "ffn1_w_down")])
        saved.append(dict(x0=xc, xa=xa, xb=xb, gu1=gu1, a1=a1, f1=f1, proj=proj, h2=h2, yn=yn, sv=sv,
                          gu2=gu2, a2=a2, f2=f2))
        xc = xd

    dx, loss_block, d_final = _loss_head(xc, loss_target.reshape(T, D), final_norm_g.reshape(1, D))
    loss = lax.psum(loss_block[0, 0], ("x", "y", "c"))

    reduced = dict.fromkeys(big)

    def halves(name, g):
        if g.ndim == 4:
            return g
        return g.reshape(N_CHIP, 2, weights[name].shape[1] // 2, g.shape[-1])

    class Reduction:
        def __init__(self, l, name, g):
            self.l, self.name, self.g, self.stage = l, name, halves(name, g), 0
            self.ici_bytes = 3 * (g.size // 8) * 2

        def step(self):
            self.stage += 1
            if self.stage == 1:
                return _sibling_half_comm([self.g])
            if self.stage == 2:
                return _scatter_comm([_pair_sum(self.g, self.got[0], core)])
            if self.stage == 3:
                reduced[self.name] = _chip_sum(self.got[0], core, self.l, L, reduced[self.name])
                return _share_comm([reduced[self.name]], self.l)
            reduced[self.name] = self.got[0]
            return None

    active, extra, gathered = [], [], {}

    def carry(fn, *args, us=None):
        left = None if us is None else us * SCATTER_BYTES_PER_US
        riders = []
        for r in active:
            if r.stage == 1 and left is not None:
                if r.ici_bytes > left * SCATTER_OVERSHOOT:
                    continue
                left -= r.ici_bytes
            riders.append(r)
        comms = [r.step() for r in riders] + [cm for cm, _ in extra]
        takers = [functools.partial(setattr, r, "got") for r in riders] + [cb for _, cb in extra]
        extra.clear()
        if fn is None:
            res, got = None, (_comm_call(_merge(*comms), "reduce_alone") if comms else [])
        else:
            res, got = fn(*args, comm=_merge(*comms))
        at = 0
        for cm, take in zip(comms, takers):
            take(got[at:at + len(cm.out_shape)])
            at += len(cm.out_shape)
        for r in riders:
            if r.stage == 3:
                r.step()
                active.remove(r)
        return res

    def reduce_later(l, name, g):
        active.append(Reduction(l, name, g))

    small = [None] * L
    d_ada = [None] * L
    for l in reversed(range(L)):
        sh1, sc1, g1, sh2, sc2, g2, sh3, sc3, g3 = mods[l]
        lng, lnb, wst, wstt, bias = mix_consts(l)
        s = saved[l]
        w = full[l]
        dgu, = carry(_ffn_bwd_down, dx, s["gu2"], g3, w["ffn2_w_down"], us=100)
        dx, h3, df, dsc3, dsh3, dgain3, dg3 = carry(_ffn_bwd_up, dx, s["xb"], dgu, s["f2"], gains("norm_ffn2_g", l),
                                                    sh3, sc3, g3, w["ffn2_w_gu"], us=120)
        reduce_later(l, "ffn2_w_gu", carry(_wgrad, h3, dgu, D, 2 * F // N_CHIP, True, "wgrad_gu", WGRAD_TOKENS // 2,
                                           us=110)[0])
        reduce_later(l, "ffn2_w_down", carry(_wgrad, s["a2"], df[None], F // 2, D, False, "wgrad_down", us=50)[0])
        dproj, d_o, dg2, dog, dwst, dbias, dlng, dlnb, dconvw = carry(
            _mix_core_bwd, s["proj"], s["sv"], dx, g2, w["mix_w_out"], lng, lnb, wstt, pmat, conv_full[l],
            gains("out_norm_g", l), us=150)
        mix_grads = [
            lambda: reduce_later(l, "mix_w_out", carry(_wgrad, s["yn"], d_o[None], D, D, False, "wgrad_out", us=20)[0]),
            lambda: reduce_later(l, "mix_w_in", carry(_wgrad, s["h2"], dproj[None], D, P // N_CHIP, True, "wgrad_in",
                                                      us=65)[0])]
        last = l == 0
        if not last:
            mix_grads[0]()
        dx, dsc2, dsh2, dgain2 = carry(_mixin_bwd, dx, s["xa"], dproj, gains("norm_mix_g", l), sc2, w["mix_w_in"], us=60)
        if not last:
            mix_grads[1]()
        dgu, = carry(_ffn_bwd_down, dx, s["gu1"], g1, w["ffn1_w_down"], us=100)
        dx, h1, df, dsc1, dsh1, dgain1, dg1 = carry(_ffn_bwd_up, dx, s["x0"], dgu, s["f1"], gains("norm_ffn1_g", l),
                                                    sh1, sc1, g1, w["ffn1_w_gu"], us=120)
        d_ada[l] = jnp.concatenate([dsh1, dsc1, dg1, dsh2, dsc2, dg2, dsh3, dsc3, dg3], axis=1).reshape(B, N_MOD * D)
        small[l] = [dgain1, dgain2, dgain3, dog, dlng, dlnb, dwst, dbias[:, ::HD], dconvw]
        if last:
            flat = [a.reshape(-1, 128) for ll in range(L) for a in small[ll]] + [d_final.reshape(-1, 128)]
            pad = (-sum(a.shape[0] for a in flat)) % 8
            packed = jnp.concatenate(flat + [jnp.zeros((pad, 128), F32)], axis=0)
            extra.append((_all_gather_comm(jnp.stack(d_ada).reshape(L * B, N_MOD * D)),
                          lambda got: gathered.update(d_ada=got[0])))
            extra.append((_all_gather_comm(packed), lambda got: gathered.update(small=got[0])))
        reduce_later(l, "ffn1_w_gu", carry(_wgrad, h1, dgu, D, 2 * F // N_CHIP, True, "wgrad_gu", WGRAD_TOKENS // 2,
                                           us=110)[0])
        reduce_later(l, "ffn1_w_down", carry(_wgrad, s["a1"], df[None], F // 2, D, False, "wgrad_down", us=50)[0])
        if last:
            mix_grads[1]()
            mix_grads[0]()
    grad_x = dx.reshape(B, S, D)

    def finished(name):
        while any(r.name == name for r in active):
            carry(None)
        return reduced[name].reshape(weights[name].shape)

    grads = {}
    d_ada_all = jnp.transpose(gathered["d_ada"].reshape(N_DEV, L, B, N_MOD * D), (1, 0, 2, 3))
    d_ada_all = d_ada_all.reshape(L, n_all, N_MOD * D)
    grads["ada_b"] = _colsum(d_ada_all).reshape(L, N_MOD * D)
    d_ada_mine = lax.dynamic_slice_in_dim(d_ada_all, chip * SA, SA, axis=2).astype(BF16)
    grads["ada_w"] = carry(_ada_bwd, c_act, d_ada_mine, us=8)[0]

    total = _sum_blocks(gathered["small"].reshape(-1, 128), N_DEV)
    pieces, at = [], 0
    for a in flat:
        pieces.append(total[at:at + a.shape[0]])
        at += a.shape[0]
    per_layer = len(small[0])
    stack = lambda j, shape: jnp.stack([pieces[l * per_layer + j].reshape(shape) for l in range(L)])
    grads["norm_ffn1_g"] = stack(0, (D,))
    grads["norm_mix_g"] = stack(1, (D,))
    grads["norm_ffn2_g"] = stack(2, (D,))
    grads["out_norm_g"] = stack(3, (D,))
    grads["sgu_ln_g"] = stack(4, (N_HEADS, HD)).sum(axis=1)
    grads["sgu_ln_b"] = stack(5, (N_HEADS, HD)).sum(axis=1)
    grads["sgu_w_s"] = stack(6, (N_HEADS, CHUNK, CHUNK))
    grads["sgu_b"] = jnp.swapaxes(stack(7, (CHUNK, N_HEADS)), 1, 2)
    g_conv = stack(8, (8, DB))[:, :conv_w.shape[1]]
    grads["conv_w"] = lax.dynamic_slice_in_dim(g_conv, chip * conv_w.shape[2], conv_w.shape[2], axis=2)
    grads["final_norm_g"] = pieces[-1].reshape(D)

    names = list(weights)
    delta, new_m, new_v = {}, {}, {}
    for k in big:
        grads[k] = finished(k)
    for k in names:
        wk = weights[k]
        view = (1, wk.shape[0]) if wk.ndim == 1 else (-1, wk.shape[-1])
        d, nm, nv, *g_again = _adamw(wk.reshape(view), grads[k].reshape(view), m_in[k].reshape(view),
                                     v_in[k].reshape(view), emit_grad=k in big)
        delta[k], new_m[k], new_v[k] = d.reshape(wk.shape), nm.reshape(wk.shape), nv.reshape(wk.shape)
        if g_again:
            grads[k] = g_again[0].reshape(wk.shape)

    return (loss, grad_x, *[grads[k] for k in names], *[delta[k] for k in names],
            *[new_m[k] for k in names], *[new_v[k] for k in names])
```

```python
import functools
import math

import jax
import jax.numpy as jnp
from jax import lax
from jax.experimental import pallas as pl
from jax.experimental.pallas import tpu as pltpu

F32 = jnp.float32
BF16 = jnp.bfloat16
MESH = pl.DeviceIdType.MESH

N_HEADS = 8
CHUNK = 128
N_MOD = 9
EPS = 1e-6
N_DEV = 8
N_CHIP = 4

ADAM_LR = 0.001
ADAM_B1 = 0.9
ADAM_B2 = 0.999
ADAM_EPS = 1e-08
ADAM_WD = 0.01
ADAM_STEP = 10

TOKEN_TILE = 512
FF_SLAB = 768
MIX_TILE = 256
WGRAD_TOKENS = 2048
VMEM_LIMIT = 56 * 1024 * 1024

SCATTER_BYTES_PER_US = 68_000
SCATTER_OVERSHOOT = 1.25

ANY = pl.BlockSpec(memory_space=pl.ANY)


def _tile(pref, n):
    t = min(pref, n)
    assert n % t == 0, (pref, n)
    return t


def _slabs(n, width):
    return [slice(c0, min(c0 + width, n)) for c0 in range(0, n, width)]


def _dot(a, b):
    return jnp.dot(a, b, preferred_element_type=F32)


def _dot_nt(a, b):
    return lax.dot_general(a, b, (((1,), (1,)), ((), ())), preferred_element_type=F32)


def _dot_tn(a, b):
    return lax.dot_general(a, b, (((0,), (0,)), ((), ())), preferred_element_type=F32)


def _sigmoid(x):
    return 1.0 / (1.0 + jnp.exp(-x))


def _sigmoid_fast(x):
    return pl.reciprocal(1.0 + jnp.exp(-x), approx=True)


def _rms(x):
    r = lax.rsqrt(jnp.mean(x * x, axis=-1, keepdims=True) + EPS)
    return x * r, r


def _norm_mod_bwd(x, dh, gain, sc):
    xh, r = _rms(x)
    dsc = jnp.sum(dh * (xh * gain), axis=0, keepdims=True)
    dsh = jnp.sum(dh, axis=0, keepdims=True)
    dn = dh * (1.0 + sc)
    dgain = jnp.sum(dn * xh, axis=0, keepdims=True)
    dy = dn * gain
    dx = r * (dy - xh * jnp.mean(dy * xh, axis=-1, keepdims=True))
    return dx, dsc, dsh, dgain


def _acc(ref, first, val):
    @pl.when(first)
    def _():
        ref[...] = val

    @pl.when(jnp.logical_not(first))
    def _():
        ref[...] += val


class _Comm:
    def __init__(self, args, out_shape, scratch, phases, aliases=None):
        self.args, self.out_shape, self.scratch = list(args), list(out_shape), list(scratch)
        self.phases, self.aliases = phases, dict(aliases or {})


def _merge(*comms):
    comms = [c for c in comms if c is not None]
    if len(comms) <= 1:
        return comms[0] if comms else None
    args = [a for c in comms for a in c.args]
    out_shape = [o for c in comms for o in c.out_shape]
    scratch = [s for c in comms for s in c.scratch]
    aliases, ai, oi = {}, 0, 0
    for c in comms:
        aliases.update({ai + i: oi + o for i, o in c.aliases.items()})
        ai += len(c.args)
        oi += len(c.out_shape)

    def phases(ins, outs, sems):
        parts, ai, oi, si = [], 0, 0, 0
        for c in comms:
            parts.append(c.phases(ins[ai:ai + len(c.args)], outs[oi:oi + len(c.out_shape)], sems[si:si + len(c.scratch)]))
            ai, oi, si = ai + len(c.args), oi + len(c.out_shape), si + len(c.scratch)

        def run(k):
            def go():
                for p in parts:
                    if p[k] is not None:
                        p[k]()
            return go
        return run(0), run(1), run(2)

    return _Comm(args, out_shape, scratch, phases, aliases)


def _call(body, name, grid, in_specs, out_specs, out_shape, scratch, args, comm=None):
    n_in, n_out, n_scr = len(in_specs), len(out_specs), len(scratch)
    sem = ("arbitrary",) * len(grid)
    params = pltpu.CompilerParams(dimension_semantics=sem, vmem_limit_bytes=VMEM_LIMIT)
    if comm is None:
        res = pl.pallas_call(body, name=name, grid=grid, in_specs=in_specs, out_specs=out_specs, out_shape=out_shape,
                             scratch_shapes=scratch, compiler_params=params)(*args)
        return list(res), []
    m_in, m_out = len(comm.args), len(comm.out_shape)

    def full(*refs):
        c_in, c_min = refs[:n_in], refs[n_in:n_in + m_in]
        o = n_in + m_in
        c_out, c_mout = refs[o:o + n_out], refs[o + n_out:o + n_out + m_out]
        o += n_out + m_out
        c_scr, c_sem = refs[o:o + n_scr], refs[o + n_scr:]
        start, mid, finish = comm.phases(c_min, c_mout, c_sem)
        ids = [pl.program_id(a) for a in range(len(grid))]
        first = functools.reduce(jnp.logical_and, [i == 0 for i in ids])
        last = functools.reduce(jnp.logical_and, [i == g - 1 for i, g in zip(ids, grid)])
        pl.when(first)(start)
        if mid is not None:
            pl.when(last)(mid)
        body(*c_in, *c_out, *c_scr)
        pl.when(last)(finish)

    res = pl.pallas_call(
        full, name=name, grid=grid,
        in_specs=list(in_specs) + [ANY] * m_in,
        out_specs=list(out_specs) + [ANY] * m_out,
        out_shape=list(out_shape) + comm.out_shape,
        scratch_shapes=list(scratch) + comm.scratch,
        input_output_aliases={n_in + i: n_out + o for i, o in comm.aliases.items()},
        compiler_params=params,
    )(*args, *comm.args)
    return list(res[:n_out]), list(res[n_out:])


def _comm_call(comm, name):
    m_in, m_out = len(comm.args), len(comm.out_shape)

    def body(*refs):
        start, mid, finish = comm.phases(refs[:m_in], refs[m_in:m_in + m_out], refs[m_in + m_out:])
        start()
        if mid is not None:
            mid()
        finish()

    res = pl.pallas_call(
        body, name=name, in_specs=[ANY] * m_in, out_specs=[ANY] * m_out, out_shape=comm.out_shape,
        scratch_shapes=comm.scratch, input_output_aliases=comm.aliases,
    )(*comm.args)
    return list(res)


def _position():
    return lax.axis_index("x"), lax.axis_index("y"), lax.axis_index("c")


def _gather_comm(items):
    n = len(items)
    half = [s.shape[1] // 2 for s, _, _ in items]

    def full_shape(i):
        s, _, col = items[i]
        _, R, C = s.shape
        return jax.ShapeDtypeStruct((R, N_CHIP * C) if col else (N_CHIP * R, C), s.dtype)

    def phases(ins, outs, sems):
        send_sems, recv_sems, local_sems = sems
        x, y, c = _position()

        def region(i, chip, h):
            s, _, col = items[i]
            _, R, C = s.shape
            if col:
                return outs[i].at[pl.ds(h * half[i], half[i]), pl.ds(chip * C, C)]
            return outs[i].at[pl.ds(chip * R + h * half[i], half[i]), :]

        def mine(i, h):
            return ins[i].at[items[i][1], pl.ds(h * half[i], half[i]), :]

        def copies(kx, ky, kc):
            k_me = 2 * kx + ky
            sibling = (kx, ky, 1 - kc)
            chips = [(1 - kx, ky), (kx, 1 - ky), (1 - kx, 1 - ky)]
            local, first, passed, arrive_ici, arrive_d2d = [], [], [], [], []

            def remote(src, dst, s, to):
                return pltpu.make_async_remote_copy(src_ref=src, dst_ref=dst, send_sem=send_sems.at[s],
                                                    recv_sem=recv_sems.at[s], device_id=to, device_id_type=MESH)

            for i in range(n):
                for h in range(2):
                    local.append(pltpu.make_async_copy(mine(i, h), region(i, k_me, h), local_sems.at[2 * i + h]))
                for j, (px, py) in enumerate(chips):
                    s = 6 * i + j
                    first.append(remote(mine(i, kc), region(i, k_me, kc), s, (px, py, kc)))
                    got = region(i, 2 * px + py, kc)
                    arrive_ici.append(remote(got, got, s, (px, py, kc)))
                    passed.append(remote(got, got, s + 3, sibling))
                    other = region(i, 2 * px + py, 1 - kc)
                    arrive_d2d.append(remote(other, other, s + 3, sibling))
            return local, first, passed, arrive_ici, arrive_d2d

        def on_each_device(fn):
            def go():
                for kx in range(2):
                    for ky in range(2):
                        for kc in range(2):
                            pl.when((x == kx) & (y == ky) & (c == kc))(functools.partial(fn, *copies(kx, ky, kc)))
            return go

        def start(local, first, passed, arrive_ici, arrive_d2d):
            for cp in local + first:
                cp.start()

        def mid(local, first, passed, arrive_ici, arrive_d2d):
            for a, p in zip(arrive_ici, passed):
                a.wait_recv()
                p.start()

        def finish(local, first, passed, arrive_ici, arrive_d2d):
            for a in arrive_d2d:
                a.wait_recv()
            for cp in first + passed:
                cp.wait_send()
            for cp in local:
                cp.wait()

        return on_each_device(start), on_each_device(mid), on_each_device(finish)

    scratch = [pltpu.SemaphoreType.DMA((6 * n,)), pltpu.SemaphoreType.DMA((6 * n,)), pltpu.SemaphoreType.DMA((2 * n,))]
    return _Comm([s for s, _, _ in items], [full_shape(i) for i in range(n)], scratch, phases)


def _sibling_half_comm(gs):
    n = len(gs)

    def phases(ins, outs, sems):
        send_sems, recv_sems = sems
        x, y, c = _position()

        def copies():
            return [pltpu.make_async_remote_copy(
                src_ref=ins[i].at[:, 1 - c], dst_ref=outs[i], send_sem=send_sems.at[i], recv_sem=recv_sems.at[i],
                device_id=(x, y, 1 - c), device_id_type=MESH) for i in range(n)]

        def start():
            for cp in copies():
                cp.start()

        def finish():
            for cp in copies():
                cp.wait()

        return start, None, finish

    out_shape = [jax.ShapeDtypeStruct(g.shape[:1] + g.shape[2:], g.dtype) for g in gs]
    return _Comm(gs, out_shape, [pltpu.SemaphoreType.DMA((n,)), pltpu.SemaphoreType.DMA((n,))], phases)


def _scatter_comm(ps):
    n = len(ps)

    def phases(ins, outs, sems):
        send_sems, recv_sems, local_sems = sems
        x, y, c = _position()
        k_me = 2 * x + y
        chips = [(1 - x, y), (x, 1 - y), (1 - x, 1 - y)]

        def copies():
            local = [pltpu.make_async_copy(ins[i].at[k_me], outs[i].at[k_me], local_sems.at[i]) for i in range(n)]
            remote = [pltpu.make_async_remote_copy(
                src_ref=ins[i].at[2 * px + py], dst_ref=outs[i].at[k_me],
                send_sem=send_sems.at[3 * i + j], recv_sem=recv_sems.at[3 * i + j],
                device_id=(px, py, c), device_id_type=MESH) for i in range(n) for j, (px, py) in enumerate(chips)]
            return local, remote

        def start():
            local, remote = copies()
            for cp in local + remote:
                cp.start()

        def finish():
            local, remote = copies()
            for cp in remote + local:
                cp.wait()

        return start, None, finish

    scratch = [pltpu.SemaphoreType.DMA((3 * n,)), pltpu.SemaphoreType.DMA((3 * n,)), pltpu.SemaphoreType.DMA((n,))]
    return _Comm(ps, [jax.ShapeDtypeStruct(p.shape, p.dtype) for p in ps], scratch, phases)


def _share_comm(rs, l):
    n = len(rs)

    def phases(ins, outs, sems):
        send_sems, recv_sems = sems
        x, y, c = _position()

        def copy(i, h):
            return pltpu.make_async_remote_copy(
                src_ref=outs[i].at[l, h], dst_ref=outs[i].at[l, h], send_sem=send_sems.at[i], recv_sem=recv_sems.at[i],
                device_id=(x, y, 1 - c), device_id_type=MESH)

        def start():
            for i in range(n):
                copy(i, c).start()

        def finish():
            for i in range(n):
                copy(i, 1 - c).wait_recv()
            for i in range(n):
                copy(i, c).wait_send()

        return start, None, finish

    return _Comm(rs, [jax.ShapeDtypeStruct(r.shape, r.dtype) for r in rs],
                 [pltpu.SemaphoreType.DMA((n,)), pltpu.SemaphoreType.DMA((n,))], phases,
                 aliases={i: i for i in range(n)})


def _all_gather_comm(block):
    def phases(ins, outs, sems):
        send_sems, recv_sems, local_sem = sems
        (src,), (out,) = ins, outs
        x, y, c = _position()
        sibling = (x, y, 1 - c)
        chips = [(1 - x, y), (x, 1 - y), (1 - x, 1 - y)]

        def slot(px, py, pc):
            return out.at[4 * px + 2 * py + pc]

        def copy(k, blk, to, own=False):
            return pltpu.make_async_remote_copy(
                src_ref=src if own else slot(*blk), dst_ref=slot(*blk),
                send_sem=send_sems.at[k], recv_sem=recv_sems.at[k], device_id=to, device_id_type=MESH)

        mine = lambda: pltpu.make_async_copy(src, slot(x, y, c), local_sem.at[0])
        first = lambda: [copy(0, (x, y, c), sibling, True)] + [
            copy(1 + j, (x, y, c), (*chip, c), True) for j, chip in enumerate(chips)]
        passed = lambda: [copy(4 + j, (*chip, c), sibling) for j, chip in enumerate(chips)]

        def start():
            mine().start()
            for cp in first():
                cp.start()

        def mid():
            for j, (chip, p) in enumerate(zip(chips, passed())):
                copy(1 + j, (*chip, c), (x, y, c)).wait_recv()
                p.start()

        def finish():
            copy(0, sibling, (x, y, c)).wait_recv()
            for j, chip in enumerate(chips):
                copy(4 + j, (*chip, 1 - c), (x, y, c)).wait_recv()
            for cp in first() + passed():
                cp.wait_send()
            mine().wait()

        return start, mid, finish

    scratch = [pltpu.SemaphoreType.DMA((7,)), pltpu.SemaphoreType.DMA((7,)), pltpu.SemaphoreType.DMA((1,))]
    return _Comm([block], [jax.ShapeDtypeStruct((N_DEV,) + block.shape, block.dtype)], scratch, phases)


def _ffn_up(x, gain, sh, sc, wgu, comm=None):
    T, D = x.shape
    F = wgu.shape[1] // 2
    B = sh.shape[0]
    tm = _tile(TOKEN_TILE, T // B)
    tps = (T // B) // tm
    slabs = _slabs(F, FF_SLAB)

    def body(x_ref, gain_ref, sh_ref, sc_ref, w_ref, gu_ref, a_ref):
        xh, _ = _rms(x_ref[...])
        h = (xh * gain_ref[...] * (1.0 + sc_ref[0]) + sh_ref[0]).astype(BF16)

        def dots(s):
            return _dot(h, w_ref[:, s]), _dot(h, w_ref[:, slice(F + s.start, F + s.stop)])

        nxt = dots(slabs[0])
        for j, s in enumerate(slabs):
            g, u = nxt
            if j + 1 < len(slabs):
                nxt = dots(slabs[j + 1])
            gu_ref[0, :, s] = g.astype(BF16)
            gu_ref[1, :, s] = u.astype(BF16)
            a_ref[:, s] = (g * _sigmoid(g) * u).astype(BF16)

    seq = lambda i: (i // tps, 0, 0)
    return _call(
        body, "ffn_up", (T // tm,),
        [
            pl.BlockSpec((tm, D), lambda i: (i, 0)),
            pl.BlockSpec((1, D), lambda i: (0, 0)),
            pl.BlockSpec((1, 1, D), seq),
            pl.BlockSpec((1, 1, D), seq),
            pl.BlockSpec((D, 2 * F), lambda i: (0, 0), pipeline_mode=pl.Buffered(1)),
        ],
        [
            pl.BlockSpec((2, tm, F), lambda i: (0, i, 0)),
            pl.BlockSpec((tm, F), lambda i: (i, 0)),
        ],
        [
            jax.ShapeDtypeStruct((2, T, F), BF16),
            jax.ShapeDtypeStruct((T, F), BF16),
        ],
        [],
        (x, gain, sh, sc, wgu), comm)


def _ffn_down(a, x, gate, wd, comm=None):
    T, F = a.shape
    D = x.shape[1]
    B = gate.shape[0]
    tm = _tile(TOKEN_TILE, T // B)
    tps = (T // B) // tm

    def body(a_ref, x_ref, gate_ref, wd_ref, xo_ref, f_ref):
        f = _dot(a_ref[...], wd_ref[...])
        f_ref[...] = f.astype(BF16)
        xo_ref[...] = x_ref[...] + 0.5 * gate_ref[0] * f

    return _call(
        body, "ffn_down", (T // tm,),
        [
            pl.BlockSpec((tm, F), lambda i: (i, 0)),
            pl.BlockSpec((tm, D), lambda i: (i, 0)),
            pl.BlockSpec((1, 1, D), lambda i: (i // tps, 0, 0)),
            pl.BlockSpec((F, D), lambda i: (0, 0)),
        ],
        [pl.BlockSpec((tm, D), lambda i: (i, 0)), pl.BlockSpec((tm, D), lambda i: (i, 0))],
        [jax.ShapeDtypeStruct((T, D), F32), jax.ShapeDtypeStruct((T, D), BF16)],
        [],
        (a, x, gate, wd), comm)


def _ffn_bwd_down(dxo, gu, gate, wd, comm=None):
    T, D = dxo.shape
    F = wd.shape[0]
    B = gate.shape[0]
    tm = _tile(TOKEN_TILE, T // B)
    tps = (T // B) // tm
    slabs = _slabs(F, FF_SLAB)

    def body(dxo_ref, gu_ref, gate_ref, wd_ref, dgu_ref):
        df = (0.5 * gate_ref[0] * dxo_ref[...]).astype(BF16)
        nxt = _dot_nt(df, wd_ref[slabs[0], :])
        for j, s in enumerate(slabs):
            da = nxt
            if j + 1 < len(slabs):
                nxt = _dot_nt(df, wd_ref[slabs[j + 1], :])
            g = gu_ref[0, :, s]
            sg = 1.0 / (1.0 + jnp.exp(-g))
            t = g * sg
            dab = da.astype(BF16)
            dgu_ref[1, :, s] = dab * t
            dgu_ref[0, :, s] = dab * gu_ref[1, :, s] * (sg + t - t * sg)

    return _call(
        body, "ffn_bwd_down", (T // tm,),
        [
            pl.BlockSpec((tm, D), lambda i: (i, 0)),
            pl.BlockSpec((2, tm, F), lambda i: (0, i, 0)),
            pl.BlockSpec((1, 1, D), lambda i: (i // tps, 0, 0)),
            pl.BlockSpec((F, D), lambda i: (0, 0), pipeline_mode=pl.Buffered(1)),
        ],
        [pl.BlockSpec((2, tm, F), lambda i: (0, i, 0))],
        [jax.ShapeDtypeStruct((2, T, F), BF16)],
        [],
        (dxo, gu, gate, wd), comm)


def _ffn_bwd_up(dxo, x, dgu, f, gain, sh, sc, gate, wgu, comm=None):
    T, D = x.shape
    F = wgu.shape[1] // 2
    B = sc.shape[0]
    tm = _tile(TOKEN_TILE, T // B)
    tps = (T // B) // tm

    def body(dxo_ref, x_ref, dgu_ref, f_ref, gain_ref, sh_ref, sc_ref, gate_ref, w_ref,
             dx_ref, h_ref, df_ref, dsc_ref, dsh_ref, dgain_ref, dgate_ref):
        i = pl.program_id(0)
        first_of_seq = (i % tps) == 0
        gain = gain_ref[...]
        sc = sc_ref[0]
        halves = _slabs(tm, tm // 2)

        def dots(r):
            return _dot_nt(dgu_ref[0, r, :], w_ref[:, 0:F]) + _dot_nt(dgu_ref[1, r, :], w_ref[:, F:])

        nxt = dots(halves[0])
        sums = None
        for j, r in enumerate(halves):
            dh = nxt
            if j + 1 < len(halves):
                nxt = dots(halves[j + 1])
            dxo = dxo_ref[r, :]
            x = x_ref[r, :]
            dx, dsc, dsh, dgain = _norm_mod_bwd(x, dh, gain, sc)
            dx_ref[r, :] = dxo + dx
            h_ref[r, :] = (_rms(x)[0] * gain * (1.0 + sc) + sh_ref[0]).astype(BF16)
            df_ref[r, :] = (0.5 * gate_ref[0] * dxo).astype(BF16)
            part = (dsc, dsh, dgain, 0.5 * jnp.sum(dxo * f_ref[r, :].astype(F32), axis=0, keepdims=True))
            sums = part if sums is None else tuple(a + b for a, b in zip(sums, part))
        _acc(dsc_ref.at[0], first_of_seq, sums[0])
        _acc(dsh_ref.at[0], first_of_seq, sums[1])
        _acc(dgain_ref, i == 0, sums[2])
        _acc(dgate_ref.at[0], first_of_seq, sums[3])

    seq = lambda i: (i // tps, 0, 0)
    row = lambda i: (i, 0)
    return _call(
        body, "ffn_bwd_up", (T // tm,),
        [
            pl.BlockSpec((tm, D), row),
            pl.BlockSpec((tm, D), row),
            pl.BlockSpec((2, tm, F), lambda i: (0, i, 0)),
            pl.BlockSpec((tm, D), row),
            pl.BlockSpec((1, D), lambda i: (0, 0)),
            pl.BlockSpec((1, 1, D), seq),
            pl.BlockSpec((1, 1, D), seq),
            pl.BlockSpec((1, 1, D), seq),
            pl.BlockSpec((D, 2 * F), lambda i: (0, 0), pipeline_mode=pl.Buffered(1)),
        ],
        [
            pl.BlockSpec((tm, D), row),
            pl.BlockSpec((tm, D), row),
            pl.BlockSpec((tm, D), row),
            pl.BlockSpec((1, 1, D), seq),
            pl.BlockSpec((1, 1, D), seq),
            pl.BlockSpec((1, D), lambda i: (0, 0)),
            pl.BlockSpec((1, 1, D), seq),
        ],
        [
            jax.ShapeDtypeStruct((T, D), F32),
            jax.ShapeDtypeStruct((T, D), BF16),
            jax.ShapeDtypeStruct((T, D), BF16),
            jax.ShapeDtypeStruct((B, 1, D), F32),
            jax.ShapeDtypeStruct((B, 1, D), F32),
            jax.ShapeDtypeStruct((1, D), F32),
            jax.ShapeDtypeStruct((B, 1, D), F32),
        ],
        [],
        (dxo, x, dgu, f, gain, sh, sc, gate, wgu), comm)


def _wgrad(a, b, tmm, tn, col_major, name, tokens=WGRAD_TOKENS, comm=None):
    T, M = a.shape
    nb, _, Nb = b.shape
    N = nb * Nb
    tk = _tile(tokens, T)
    span = 2 if col_major else 1
    wide = span * tn
    npb = Nb // wide
    assert M % tmm == 0 and Nb % wide == 0
    if col_major:
        assert tmm == M
        shape = (N // tn, 2, M // 2, tn)
        out_spec = pl.BlockSpec((span, 2, M // 2, tn), lambda i, j, t: (j, 0, 0, 0))
    else:
        shape = (M // tmm, tmm, N)
        out_spec = pl.BlockSpec((None, tmm, tn), lambda i, j, t: (i, 0, j))

    def body(a_ref, b_ref, o_ref):
        t = pl.program_id(2)
        res = _dot_tn(a_ref[...], b_ref[...])
        if col_major:
            for s in range(span):
                for h in range(2):
                    _acc(o_ref.at[s, h], t == 0, res[h * (M // 2):(h + 1) * (M // 2), s * tn:(s + 1) * tn])
        else:
            _acc(o_ref, t == 0, res)

    return _call(
        body, name, (M // tmm, N // wide, T // tk),
        [
            pl.BlockSpec((tk, tmm), lambda i, j, t: (t, i)),
            pl.BlockSpec((None, tk, wide), lambda i, j, t: (j // npb, t, j % npb)),
        ],
        [out_spec], [jax.ShapeDtypeStruct(shape, F32)], [],
        (a, b), comm)


def _mixin_fwd(x, gain, sh, sc, win, comm=None):
    T, D = x.shape
    P = win.shape[1]
    B = sh.shape[0]
    tm = _tile(TOKEN_TILE, T // B)
    tps = (T // B) // tm

    def body(x_ref, gain_ref, sh_ref, sc_ref, w_ref, proj_ref, h_ref):
        xh, _ = _rms(x_ref[...])
        h = (xh * gain_ref[...] * (1.0 + sc_ref[0]) + sh_ref[0]).astype(BF16)
        h_ref[...] = h
        proj_ref[...] = _dot(h, w_ref[...])

    seq = lambda i: (i // tps, 0, 0)
    return _call(
        body, "mixin_fwd", (T // tm,),
        [
            pl.BlockSpec((tm, D), lambda i: (i, 0)),
            pl.BlockSpec((1, D), lambda i: (0, 0)),
            pl.BlockSpec((1, 1, D), seq),
            pl.BlockSpec((1, 1, D), seq),
            pl.BlockSpec((D, P), lambda i: (0, 0)),
        ],
        [pl.BlockSpec((tm, P), lambda i: (i, 0)), pl.BlockSpec((tm, D), lambda i: (i, 0))],
        [jax.ShapeDtypeStruct((T, P), F32), jax.ShapeDtypeStruct((T, D), BF16)],
        [],
        (x, gain, sh, sc, win), comm)


def _mixin_bwd(dxo, x, dproj, gain, sc, win, comm=None):
    T, D = x.shape
    P = win.shape[1]
    B = sc.shape[0]
    tm = _tile(TOKEN_TILE, T // B)
    tps = (T // B) // tm

    def body(dxo_ref, x_ref, dp_ref, gain_ref, sc_ref, w_ref, dx_ref, dsc_ref, dsh_ref, dgain_ref):
        i = pl.program_id(0)
        first_of_seq = (i % tps) == 0
        halves = _slabs(tm, tm // 2)
        nxt = _dot_nt(dp_ref[halves[0], :], w_ref[...])
        sums = None
        for j, r in enumerate(halves):
            dh = nxt
            if j + 1 < len(halves):
                nxt = _dot_nt(dp_ref[halves[j + 1], :], w_ref[...])
            part = _norm_mod_bwd(x_ref[r, :], dh, gain_ref[...], sc_ref[0])
            dx_ref[r, :] = dxo_ref[r, :] + part[0]
            sums = part[1:] if sums is None else tuple(a + b for a, b in zip(sums, part[1:]))
        _acc(dsc_ref.at[0], first_of_seq, sums[0])
        _acc(dsh_ref.at[0], first_of_seq, sums[1])
        _acc(dgain_ref, i == 0, sums[2])

    seq = lambda i: (i // tps, 0, 0)
    row = lambda i: (i, 0)
    return _call(
        body, "mixin_bwd", (T // tm,),
        [
            pl.BlockSpec((tm, D), row),
            pl.BlockSpec((tm, D), row),
            pl.BlockSpec((tm, P), row),
            pl.BlockSpec((1, D), lambda i: (0, 0)),
            pl.BlockSpec((1, 1, D), seq),
            pl.BlockSpec((D, P), lambda i: (0, 0)),
        ],
        [
            pl.BlockSpec((tm, D), row),
            pl.BlockSpec((1, 1, D), seq),
            pl.BlockSpec((1, 1, D), seq),
            pl.BlockSpec((1, D), lambda i: (0, 0)),
        ],
        [
            jax.ShapeDtypeStruct((T, D), F32),
            jax.ShapeDtypeStruct((B, 1, D), F32),
            jax.ShapeDtypeStruct((B, 1, D), F32),
            jax.ShapeDtypeStruct((1, D), F32),
        ],
        [],
        (dxo, x, dproj, gain, sc, win), comm)


def _head_mean(z, pmat, exact=True):
    hi = z.astype(BF16)
    if not exact:
        return _dot(hi, pmat)
    lo = (z - hi.astype(F32)).astype(BF16)
    return _dot(hi, pmat) + _dot(lo, pmat)


def _gelu_parts(x):
    cdf = 0.5 * (1.0 + lax.erf(x * (1.0 / math.sqrt(2.0))))
    return x * cdf, cdf


def _gelu_grad(x, cdf):
    return cdf + x * jnp.exp(-0.5 * x * x) * (1.0 / math.sqrt(2.0 * math.pi))


LANES = 128


def _head_blocks(da):
    hd = da // N_HEADS
    lb = min(LANES, da)
    col = lax.broadcasted_iota(jnp.int32, (1, lb), 1)
    return lb, lb // hd, da // lb, [(col >= h * hd) & (col < (h + 1) * hd) for h in range(lb // hd)]


def _mix_heads(w_stack, v, da):
    lb, hpb, nb, masks = _head_blocks(da)
    outs = []
    for b in range(nb):
        res = _dot(w_stack[b * hpb * CHUNK:(b + 1) * hpb * CHUNK], v[:, b * lb:(b + 1) * lb])
        out = res[0:CHUNK]
        for h in range(1, hpb):
            out = jnp.where(masks[h], res[h * CHUNK:(h + 1) * CHUNK], out)
        outs.append(out)
    return outs[0] if nb == 1 else jnp.concatenate(outs, axis=1)


def _mix_heads_grad(dm, v, da):
    lb, hpb, nb, masks = _head_blocks(da)
    outs = []
    for b in range(nb):
        dmb = dm[:, b * lb:(b + 1) * lb]
        stack = jnp.concatenate([jnp.where(masks[h], dmb, jnp.zeros_like(dmb)) for h in range(hpb)], axis=0)
        outs.append(_dot_nt(stack, v[:, b * lb:(b + 1) * lb]))
    return outs[0] if nb == 1 else jnp.concatenate(outs, axis=0)


def _causal_stack(w, transposed):
    r = lax.broadcasted_iota(jnp.int32, w.shape, 0) % CHUNK
    c = lax.broadcasted_iota(jnp.int32, w.shape, 1)
    keep = (c >= r) if transposed else (c <= r)
    return jnp.where(keep, w, 0.0)


def _mix_core_forward(proj, zprev, prm, da, db, saved=None):
    n = proj.shape[0]
    ua = proj[:, 0:da]
    va = proj[:, da:2 * da]
    bg = proj[:, 2 * da:2 * da + db]
    cg = proj[:, 2 * da + db:2 * da + 2 * db]
    xb = proj[:, 2 * da + 2 * db:]
    if saved is None:
        ug, ucdf = _gelu_parts(ua)
        vg, vcdf = _gelu_parts(va)
        zc = vg - _head_mean(vg, prm["pmat"])
        rs = lax.rsqrt(_head_mean(zc * zc, prm["pmat"], exact=False) + EPS)
        vhat = zc * rs
        vln = (vhat * prm["lng"] + prm["lnb"]).astype(BF16)
        wst = _causal_stack(prm["wst"], False).astype(BF16)
        mixed = [_mix_heads(wst, vln[j * CHUNK:(j + 1) * CHUNK], da) + prm["bias"] for j in range(n // CHUNK)]
        mixed = mixed[0] if len(mixed) == 1 else jnp.concatenate(mixed, axis=0)
    else:
        ucdf, vcdf, vhat, rs, mixed = [saved[k].astype(F32) for k in range(5)]
        ug = ua * ucdf
        vln = (vhat * prm["lng"] + prm["lnb"]).astype(BF16)
    ya = ug * mixed
    z = cg * xb
    row = lax.broadcasted_iota(jnp.int32, z.shape, 0)
    z1 = jnp.where(row == 0, zprev[7:8], pltpu.roll(z, 1, 0))
    z2 = jnp.where(row == 0, zprev[6:7], jnp.where(row == 1, zprev[7:8], pltpu.roll(z, 2, 0)))
    cw = prm["convw"]
    conv = z2 * cw[0:1] + z1 * cw[1:2] + z * cw[2:3]
    yb = bg * conv
    yah, ra = _rms(ya)
    ybh, rb = _rms(yb)
    return dict(ua=ua, va=va, bg=bg, cg=cg, xb=xb, ug=ug, ucdf=ucdf, vcdf=vcdf, rs=rs, vhat=vhat, vln=vln,
                mixed=mixed, z=z, z1=z1, z2=z2, conv=conv, yah=yah, ra=ra, ybh=ybh, rb=rb)


def _mix_params(lng_ref, lnb_ref, wst_ref, bias_ref, pmat_ref, convw_ref):
    return dict(lng=lng_ref[...], lnb=lnb_ref[...], wst=wst_ref[...], bias=bias_ref[...],
                pmat=pmat_ref[...], convw=convw_ref[...])


def _mix_core_fwd(proj, x, gate, wout, lng, lnb, wst, bias, pmat, convw, og, comm=None):
    T, P = proj.shape
    D = x.shape[1]
    B = gate.shape[0]
    da = lng.shape[1]
    db = convw.shape[1]
    tm = _tile(MIX_TILE, T // B)
    tps = (T // B) // tm

    def body(proj_ref, x_ref, gate_ref, wout_ref, lng_ref, lnb_ref, wst_ref, bias_ref, pmat_ref, convw_ref,
             og_ref, xo_ref, yn_ref, sv_ref, halo):
        i = pl.program_id(0)

        @pl.when((i % tps) == 0)
        def _():
            halo[...] = jnp.zeros_like(halo)

        prm = _mix_params(lng_ref, lnb_ref, wst_ref, bias_ref, pmat_ref, convw_ref)
        r = _mix_core_forward(proj_ref[...], halo[...], prm, da, db)
        halo[...] = r["z"][tm - 8:tm]
        for k, name in enumerate(("ucdf", "vcdf", "vhat", "rs", "mixed")):
            sv_ref[k] = r[name].astype(BF16)
        og = og_ref[...]
        yn_ref[:, 0:da] = (r["yah"] * og[:, 0:da]).astype(BF16)
        yn_ref[:, da:] = (r["ybh"] * og[:, da:]).astype(BF16)
        xo_ref[...] = x_ref[...] + gate_ref[0] * _dot(yn_ref[...], wout_ref[...])

    full = lambda a: pl.BlockSpec(a.shape, lambda i: (0,) * a.ndim)
    return _call(
        body, "mix_core_fwd", (T // tm,),
        [
            pl.BlockSpec((tm, P), lambda i: (i, 0)),
            pl.BlockSpec((tm, D), lambda i: (i, 0)),
            pl.BlockSpec((1, 1, D), lambda i: (i // tps, 0, 0)),
            full(wout), full(lng), full(lnb), full(wst), full(bias), full(pmat), full(convw), full(og),
        ],
        [pl.BlockSpec((tm, D), lambda i: (i, 0)), pl.BlockSpec((tm, D), lambda i: (i, 0)),
         pl.BlockSpec((5, tm, da), lambda i: (0, i, 0))],
        [jax.ShapeDtypeStruct((T, D), F32), jax.ShapeDtypeStruct((T, D), BF16),
         jax.ShapeDtypeStruct((5, T, da), BF16)],
        [pltpu.VMEM((8, db), F32)],
        (proj, x, gate, wout, lng, lnb, wst, bias, pmat, convw, og), comm)


def _mix_core_bwd(proj, sv, dxo, gate, wout, lng, lnb, wstt, pmat, convw, og, comm=None):
    T, P = proj.shape
    D = dxo.shape[1]
    B = gate.shape[0]
    da = lng.shape[1]
    db = convw.shape[1]
    assert da == db and P == 2 * da + 3 * db
    tm = _tile(MIX_TILE, T // B)
    tps = (T // B) // tm
    nt = T // tm
    hd = da // N_HEADS

    def body(proj_ref, cgp_ref, xbp_ref, sv_ref, dxo_ref, gate_ref, wout_ref, lng_ref, lnb_ref, wstt_ref,
             pmat_ref, convw_ref, og_ref,
             dproj_ref, do_ref, dgate_ref, dog_ref, dwst_ref, dbias_ref, dlng_ref, dlnb_ref, dconvw_ref, carry):
        i = pl.program_id(0)
        ri = nt - 1 - i
        first = i == 0
        end_of_seq = (ri % tps) == tps - 1
        start_of_seq = (ri % tps) == 0

        @pl.when(end_of_seq)
        def _():
            carry[...] = jnp.zeros_like(carry)

        prm = dict(lng=lng_ref[...], lnb=lnb_ref[...], pmat=pmat_ref[...], convw=convw_ref[...])
        zprev = jnp.where(start_of_seq, 0.0, cgp_ref[...] * xbp_ref[...])
        r = _mix_core_forward(proj_ref[...], zprev, prm, da, db, saved=sv_ref)
        og = og_ref[...]
        pmat = prm["pmat"]

        yn = jnp.concatenate([(r["yah"] * og[:, 0:da]).astype(BF16), (r["ybh"] * og[:, da:]).astype(BF16)], axis=1)
        dxo = dxo_ref[...]
        o = _dot(yn, wout_ref[...])
        _acc(dgate_ref.at[0], end_of_seq, jnp.sum(dxo * o, axis=0, keepdims=True))
        d_o = (gate_ref[0] * dxo).astype(BF16)
        do_ref[...] = d_o
        dyn = _dot_nt(d_o, wout_ref[...])

        def rms_bwd(dyn_g, yh, rr, og_g):
            dog_g = jnp.sum(dyn_g * yh, axis=0, keepdims=True)
            dyh = dyn_g * og_g
            return rr * (dyh - yh * jnp.mean(dyh * yh, axis=-1, keepdims=True)), dog_g

        dya, dog_a = rms_bwd(dyn[:, 0:da], r["yah"], r["ra"], og[:, 0:da])
        dyb, dog_b = rms_bwd(dyn[:, da:], r["ybh"], r["rb"], og[:, da:])
        _acc(dog_ref, first, jnp.concatenate([dog_a, dog_b], axis=1))

        dug = dya * r["mixed"]
        dmixed = dya * r["ug"]
        wstt_b = _causal_stack(wstt_ref[...], True).astype(BF16)
        dbias = jnp.zeros((CHUNK, da), F32)
        dwst = jnp.zeros((N_HEADS * CHUNK, CHUNK), F32)
        dvln = []
        for j in range(tm // CHUNK):
            dm = dmixed[j * CHUNK:(j + 1) * CHUNK]
            dbias = dbias + dm
            dmb = dm.astype(BF16)
            dwst = dwst + _mix_heads_grad(dmb, r["vln"][j * CHUNK:(j + 1) * CHUNK], da)
            dvln.append(_mix_heads(wstt_b, dmb, da))
        dvln = dvln[0] if len(dvln) == 1 else jnp.concatenate(dvln, axis=0)
        _acc(dbias_ref, first, dbias)
        _acc(dwst_ref, first, dwst)
        _acc(dlng_ref, first, jnp.sum(dvln * r["vhat"], axis=0, keepdims=True))
        _acc(dlnb_ref, first, jnp.sum(dvln, axis=0, keepdims=True))
        dvhat = dvln * prm["lng"]
        dvg = r["rs"] * (dvhat - _head_mean(dvhat, pmat, exact=False)
                         - r["vhat"] * _head_mean(dvhat * r["vhat"], pmat, exact=False))
        dproj_ref[:, 0:da] = (dug * _gelu_grad(r["ua"], r["ucdf"])).astype(BF16)
        dproj_ref[:, da:2 * da] = (dvg * _gelu_grad(r["va"], r["vcdf"])).astype(BF16)

        dproj_ref[:, 2 * da:2 * da + db] = (dyb * r["conv"]).astype(BF16)
        dconv = dyb * r["bg"]
        dcw = jnp.concatenate([
            jnp.sum(dconv * r["z2"], axis=0, keepdims=True),
            jnp.sum(dconv * r["z1"], axis=0, keepdims=True),
            jnp.sum(dconv * r["z"], axis=0, keepdims=True),
            jnp.zeros((5, db), F32)], axis=0)
        _acc(dconvw_ref, first, dcw)
        nxt = carry[...]
        row = lax.broadcasted_iota(jnp.int32, dconv.shape, 0)
        dc1 = jnp.where(row == tm - 1, nxt[0:1], pltpu.roll(dconv, tm - 1, 0))
        dc2 = jnp.where(row == tm - 2, nxt[0:1], jnp.where(row == tm - 1, nxt[1:2], pltpu.roll(dconv, tm - 2, 0)))
        carry[...] = dconv[0:8]
        cw = prm["convw"]
        dz = dconv * cw[2:3] + dc1 * cw[1:2] + dc2 * cw[0:1]
        dproj_ref[:, 2 * da + db:2 * da + 2 * db] = (dz * r["xb"]).astype(BF16)
        dproj_ref[:, 2 * da + 2 * db:] = (dz * r["cg"]).astype(BF16)

        @pl.when(i == nt - 1)
        def _():
            dwst_ref[...] = _causal_stack(dwst_ref[...], False)
            dbias_ref[...] = _head_mean(dbias_ref[...], pmat) * float(hd)

    full = lambda a: pl.BlockSpec(a.shape, lambda i: (0,) * a.ndim)
    const = lambda i: (0, 0)
    rev = lambda i: (nt - 1 - i, 0)
    prev8 = lambda col: (lambda i: (jnp.maximum((nt - 1 - i) * (tm // 8) - 1, 0), col))
    return _call(
        body, "mix_core_bwd", (nt,),
        [
            pl.BlockSpec((tm, P), rev),
            pl.BlockSpec((8, db), prev8((2 * da + db) // db)),
            pl.BlockSpec((8, db), prev8((2 * da + 2 * db) // db)),
            pl.BlockSpec((5, tm, da), lambda i: (0, nt - 1 - i, 0)),
            pl.BlockSpec((tm, D), rev),
            pl.BlockSpec((1, 1, D), lambda i: ((nt - 1 - i) // tps, 0, 0)),
            full(wout), full(lng), full(lnb), full(wstt), full(pmat), full(convw), full(og),
        ],
        [
            pl.BlockSpec((tm, P), rev),
            pl.BlockSpec((tm, D), rev),
            pl.BlockSpec((1, 1, D), lambda i: ((nt - 1 - i) // tps, 0, 0)),
            pl.BlockSpec((1, D), const),
            pl.BlockSpec((N_HEADS * CHUNK, CHUNK), const),
            pl.BlockSpec((CHUNK, da), const),
            pl.BlockSpec((1, da), const),
            pl.BlockSpec((1, da), const),
            pl.BlockSpec((8, db), const),
        ],
        [
            jax.ShapeDtypeStruct((T, P), BF16),
            jax.ShapeDtypeStruct((T, D), BF16),
            jax.ShapeDtypeStruct((B, 1, D), F32),
            jax.ShapeDtypeStruct((1, D), F32),
            jax.ShapeDtypeStruct((N_HEADS * CHUNK, CHUNK), F32),
            jax.ShapeDtypeStruct((CHUNK, da), F32),
            jax.ShapeDtypeStruct((1, da), F32),
            jax.ShapeDtypeStruct((1, da), F32),
            jax.ShapeDtypeStruct((8, db), F32),
        ],
        [pltpu.VMEM((8, db), F32)],
        (proj, proj, proj, sv, dxo, gate, wout, lng, lnb, wstt, pmat, convw, og), comm)


def _loss_head(x, target, gain):
    T, D = x.shape
    tm = _tile(TOKEN_TILE, T)

    def body(x_ref, t_ref, gain_ref, dx_ref, loss_ref, dgain_ref):
        first = pl.program_id(0) == 0
        xh, r = _rms(x_ref[...])
        gain = gain_ref[...]
        err = xh * gain - t_ref[...]
        _acc(loss_ref, first, jnp.zeros((8, 128), F32) + 0.5 * jnp.sum(err * err) / D)
        dout = err * (1.0 / D)
        _acc(dgain_ref, first, jnp.sum(dout * xh, axis=0, keepdims=True))
        dy = dout * gain
        dx_ref[...] = r * (dy - xh * jnp.mean(dy * xh, axis=-1, keepdims=True))

    return _call(
        body, "loss_head", (T // tm,),
        [
            pl.BlockSpec((tm, D), lambda i: (i, 0)),
            pl.BlockSpec((tm, D), lambda i: (i, 0)),
            pl.BlockSpec((1, D), lambda i: (0, 0)),
        ],
        [
            pl.BlockSpec((tm, D), lambda i: (i, 0)),
            pl.BlockSpec((8, 128), lambda i: (0, 0)),
            pl.BlockSpec((1, D), lambda i: (0, 0)),
        ],
        [
            jax.ShapeDtypeStruct((T, D), F32),
            jax.ShapeDtypeStruct((8, 128), F32),
            jax.ShapeDtypeStruct((1, D), F32),
        ],
        [],
        (x, target, gain))[0]


def _ada_fwd(c_all, ada_w, ada_b):
    n, D = c_all.shape
    L, _, sa = ada_w.shape
    tn = _tile(768, sa)

    def body(c_ref, w_ref, b_ref, act_ref, o_ref):
        c = c_ref[...]
        act = (c * _sigmoid(c)).astype(BF16)
        act_ref[...] = act
        o_ref[...] = _dot(act, w_ref[...].astype(BF16)) + b_ref[...]

    return _call(
        body, "ada_fwd", (L, sa // tn),
        [
            pl.BlockSpec((n, D), lambda l, j: (0, 0)),
            pl.BlockSpec((None, D, tn), lambda l, j: (l, 0, j)),
            pl.BlockSpec((None, 1, tn), lambda l, j: (l, 0, j)),
        ],
        [
            pl.BlockSpec((n, D), lambda l, j: (0, 0)),
            pl.BlockSpec((None, n, tn), lambda l, j: (l, 0, j)),
        ],
        [jax.ShapeDtypeStruct((n, D), BF16), jax.ShapeDtypeStruct((L, n, sa), F32)],
        [],
        (c_all, ada_w, ada_b))[0]


def _ada_bwd(c_act, d_ada, comm=None):
    n, D = c_act.shape
    L, _, sa = d_ada.shape
    tn = _tile(768, sa)

    def body(c_ref, d_ref, o_ref):
        o_ref[...] = _dot_tn(c_ref[...], d_ref[...])

    return _call(
        body, "ada_bwd", (L, sa // tn),
        [pl.BlockSpec((n, D), lambda l, j: (0, 0)), pl.BlockSpec((None, n, tn), lambda l, j: (l, 0, j))],
        [pl.BlockSpec((None, D, tn), lambda l, j: (l, 0, j))],
        [jax.ShapeDtypeStruct((L, D, sa), F32)],
        [],
        (c_act, d_ada), comm)


def _colsum(a):
    L, n, C = a.shape

    def body(a_ref, o_ref):
        o_ref[...] = jnp.sum(a_ref[...], axis=0, keepdims=True)

    return _call(
        body, "colsum", (L,),
        [pl.BlockSpec((None, n, C), lambda l: (l, 0, 0))],
        [pl.BlockSpec((None, 1, C), lambda l: (l, 0, 0))],
        [jax.ShapeDtypeStruct((L, 1, C), F32)],
        [],
        (a,))[0][0]


def _row_tile(rows, cols, nbuf):
    budget = VMEM_LIMIT // 3 // (2 * nbuf * 4 * cols)
    t = rows
    while t > max(budget, 8) and t % 2 == 0 and (t // 2) % 8 == 0:
        t //= 2
    return t


def _pair_sum(g, recv, core):
    n, _, R, C = g.shape
    tr = _row_tile(R, C, 3)

    def body(core_ref, g_ref, r_ref, o_ref):
        o_ref[...] = (g_ref[...] + r_ref[...]).astype(BF16)

    return pl.pallas_call(
        body,
        name="pair_sum",
        grid_spec=pltpu.PrefetchScalarGridSpec(
            num_scalar_prefetch=1,
            grid=(n, R // tr),
            in_specs=[
                pl.BlockSpec((None, None, tr, C), lambda i, r, core_ref: (i, core_ref[0], r, 0)),
                pl.BlockSpec((None, tr, C), lambda i, r, core_ref: (i, r, 0)),
            ],
            out_specs=pl.BlockSpec((None, tr, C), lambda i, r, core_ref: (i, r, 0)),
        ),
        out_shape=jax.ShapeDtypeStruct((n, R, C), BF16),
        compiler_params=pltpu.CompilerParams(dimension_semantics=("arbitrary", "arbitrary"),
                                             vmem_limit_bytes=VMEM_LIMIT),
    )(core, g, recv)


def _chip_sum(q, core, l, n_layers, prev):
    nq, R, C = q.shape
    tr = _row_tile(R, C, 4)

    def body(core_ref, q_ref, *rest):
        o_ref = rest[-1]
        s = q_ref[0].astype(F32)
        for j in range(1, nq):
            s = s + q_ref[j].astype(F32)
        o_ref[...] = s

    in_specs = [pl.BlockSpec((nq, tr, C), lambda r, core_ref: (0, r, 0))]
    args = [core, q]
    aliases = {}
    if prev is not None:
        in_specs.append(ANY)
        args.append(prev)
        aliases = {2: 0}
    return pl.pallas_call(
        body,
        name="chip_sum",
        grid_spec=pltpu.PrefetchScalarGridSpec(
            num_scalar_prefetch=1,
            grid=(R // tr,),
            in_specs=in_specs,
            out_specs=pl.BlockSpec((None, None, tr, C), lambda r, core_ref: (l, core_ref[0], r, 0)),
        ),
        out_shape=jax.ShapeDtypeStruct((n_layers, 2, R, C), F32),
        input_output_aliases=aliases,
        compiler_params=pltpu.CompilerParams(dimension_semantics=("arbitrary",), vmem_limit_bytes=VMEM_LIMIT),
    )(*args)


def _sum_blocks(a, n):
    M = a.shape[0] // n
    C = a.shape[1]

    def body(a_ref, o_ref):
        s = a_ref[0:M]
        for j in range(1, n):
            s = s + a_ref[j * M:(j + 1) * M]
        o_ref[...] = s

    return pl.pallas_call(
        body,
        name="sum_blocks",
        out_shape=jax.ShapeDtypeStruct((M, C), F32),
        compiler_params=pltpu.CompilerParams(vmem_limit_bytes=VMEM_LIMIT),
    )(a)


def _adamw(w, g, m, v, emit_grad=False):
    R, C = w.shape
    n_out = 4 if emit_grad else 3
    tr = _row_tile(R, C, 4 + n_out) if R % 8 == 0 else R

    def body(w_ref, g_ref, m_ref, v_ref, d_ref, nm_ref, nv_ref, *g_out):
        g = g_ref[...]
        m = ADAM_B1 * m_ref[...] + (1.0 - ADAM_B1) * g
        v = ADAM_B2 * v_ref[...] + (1.0 - ADAM_B2) * (g * g)
        m_hat = m / (1.0 - ADAM_B1 ** ADAM_STEP)
        v_hat = v / (1.0 - ADAM_B2 ** ADAM_STEP)
        d_ref[...] = -ADAM_LR * (m_hat / (jnp.sqrt(v_hat) + ADAM_EPS) + ADAM_WD * w_ref[...])
        nm_ref[...] = m
        nv_ref[...] = v
        if emit_grad:
            g_out[0][...] = g

    spec = pl.BlockSpec((tr, C), lambda i: (i, 0))
    return _call(body, "adamw", (R // tr,), [spec] * 4, [spec] * n_out, [jax.ShapeDtypeStruct((R, C), F32)] * n_out,
                 [], (w, g, m, v))[0]


def kernel(x, c, ada_w, ada_b, norm_ffn1_g, ffn1_w_gu, ffn1_w_down, norm_mix_g, mix_w_in, sgu_ln_g, sgu_ln_b, sgu_w_s, sgu_b, conv_w, out_norm_g, mix_w_out, norm_ffn2_g, ffn2_w_gu, ffn2_w_down, final_norm_g, loss_target, m_ada_w, m_ada_b, m_norm_ffn1_g, m_ffn1_w_gu, m_ffn1_w_down, m_norm_mix_g, m_mix_w_in, m_sgu_ln_g, m_sgu_ln_b, m_sgu_w_s, m_sgu_b, m_conv_w, m_out_norm_g, m_mix_w_out, m_norm_ffn2_g, m_ffn2_w_gu, m_ffn2_w_down, m_final_norm_g, v_ada_w, v_ada_b, v_norm_ffn1_g, v_ffn1_w_gu, v_ffn1_w_down, v_norm_mix_g, v_mix_w_in, v_sgu_ln_g, v_sgu_ln_b, v_sgu_w_s, v_sgu_b, v_conv_w, v_out_norm_g, v_mix_w_out, v_norm_ffn2_g, v_ffn2_w_gu, v_ffn2_w_down, v_final_norm_g):
    weights = dict(ada_w=ada_w, ada_b=ada_b, norm_ffn1_g=norm_ffn1_g, ffn1_w_gu=ffn1_w_gu, ffn1_w_down=ffn1_w_down,
                   norm_mix_g=norm_mix_g, mix_w_in=mix_w_in, sgu_ln_g=sgu_ln_g, sgu_ln_b=sgu_ln_b, sgu_w_s=sgu_w_s,
                   sgu_b=sgu_b, conv_w=conv_w, out_norm_g=out_norm_g, mix_w_out=mix_w_out, norm_ffn2_g=norm_ffn2_g,
                   ffn2_w_gu=ffn2_w_gu, ffn2_w_down=ffn2_w_down, final_norm_g=final_norm_g)
    m_in = dict(ada_w=m_ada_w, ada_b=m_ada_b, norm_ffn1_g=m_norm_ffn1_g, ffn1_w_gu=m_ffn1_w_gu,
                ffn1_w_down=m_ffn1_w_down, norm_mix_g=m_norm_mix_g, mix_w_in=m_mix_w_in, sgu_ln_g=m_sgu_ln_g,
                sgu_ln_b=m_sgu_ln_b, sgu_w_s=m_sgu_w_s, sgu_b=m_sgu_b, conv_w=m_conv_w, out_norm_g=m_out_norm_g,
                mix_w_out=m_mix_w_out, norm_ffn2_g=m_norm_ffn2_g, ffn2_w_gu=m_ffn2_w_gu, ffn2_w_down=m_ffn2_w_down,
                final_norm_g=m_final_norm_g)
    v_in = dict(ada_w=v_ada_w, ada_b=v_ada_b, norm_ffn1_g=v_norm_ffn1_g, ffn1_w_gu=v_ffn1_w_gu,
                ffn1_w_down=v_ffn1_w_down, norm_mix_g=v_norm_mix_g, mix_w_in=v_mix_w_in, sgu_ln_g=v_sgu_ln_g,
                sgu_ln_b=v_sgu_ln_b, sgu_w_s=v_sgu_w_s, sgu_b=v_sgu_b, conv_w=v_conv_w, out_norm_g=v_out_norm_g,
                mix_w_out=v_mix_w_out, norm_ffn2_g=v_norm_ffn2_g, ffn2_w_gu=v_ffn2_w_gu, ffn2_w_down=v_ffn2_w_down,
                final_norm_g=v_final_norm_g)

    B, S, D = x.shape
    T = B * S
    L = ada_w.shape[0]
    F = ffn1_w_down.shape[1] * N_CHIP
    P = mix_w_in.shape[2] * N_CHIP
    DA = D // 2
    DB = D - DA
    HD = DA // N_HEADS
    SA = ada_w.shape[2]
    n_all = B * N_DEV
    mx, my, mc = _position()
    chip = 2 * mx + my
    dev = 2 * chip + mc
    core = jnp.reshape(mc, (1,)).astype(jnp.int32)

    big = ["ffn1_w_gu", "ffn1_w_down", "mix_w_in", "mix_w_out", "ffn2_w_gu", "ffn2_w_down"]
    col_sharded = dict(ffn1_w_gu=True, ffn1_w_down=False, mix_w_in=True, mix_w_out=False,
                       ffn2_w_gu=True, ffn2_w_down=False)
    shards = {k: weights[k].astype(BF16) for k in big}
    gather = lambda l, *names: _gather_comm([(shards[k], l, col_sharded[k]) for k in names])
    full = [dict() for _ in range(L)]

    def arrived(l, names, res):
        full[l].update(zip(names, res))

    n_cw = L * conv_w.shape[1]
    cw_block = jnp.pad(conv_w.reshape(n_cw, conv_w.shape[2]), ((0, 8 - n_cw), (0, 0)))
    c_all, cw_all = _comm_call(_merge(_all_gather_comm(c.reshape(8, B * D // 8)), _all_gather_comm(cw_block)),
                               "gather_c")
    c_all = c_all.reshape(n_all, D)
    cw_all = cw_all.reshape(N_CHIP, 2, 8, conv_w.shape[2])[:, 0, :n_cw]
    conv_full = jnp.transpose(cw_all.reshape(N_CHIP, L, conv_w.shape[1], conv_w.shape[2]), (1, 2, 0, 3))
    conv_full = conv_full.reshape(L, conv_w.shape[1], DB)
    ada_b_mine = lax.dynamic_slice_in_dim(ada_b, chip * SA, SA, axis=1).reshape(L, 1, SA)
    c_act, ada_part = _ada_fwd(c_all, ada_w, ada_b_mine)
    ada_all, first_w = _comm_call(_merge(_all_gather_comm(ada_part.reshape(L * n_all, SA)), gather(0, big[0])),
                                  "gather_first")
    arrived(0, big[:1], [first_w])
    ada_all = ada_all.reshape(N_CHIP, 2, L, n_all, SA)[:, 0]
    ada_all = jnp.transpose(ada_all, (1, 2, 0, 3)).reshape(L, n_all, N_CHIP * SA)
    ada = lax.dynamic_slice_in_dim(ada_all, dev * B, B, axis=1).reshape(L, B, N_MOD, 1, D)
    mods = [[ada[l, :, j] for j in range(N_MOD)] for l in range(L)]

    x0 = x.reshape(T, D)
    gains = lambda name, l: weights[name][l].reshape(1, D)
    hmask = jnp.repeat(jnp.eye(N_HEADS, dtype=F32), HD, axis=0)
    pmat = (jnp.repeat(hmask, HD, axis=1) / HD).astype(BF16)

    def mix_consts(l):
        lng = jnp.tile(sgu_ln_g[l], N_HEADS).reshape(1, DA)
        lnb = jnp.tile(sgu_ln_b[l], N_HEADS).reshape(1, DA)
        wst = sgu_w_s[l].reshape(N_HEADS * CHUNK, CHUNK)
        wstt = jnp.swapaxes(sgu_w_s[l], 1, 2).reshape(N_HEADS * CHUNK, CHUNK)
        bias = jnp.repeat(jnp.transpose(sgu_b[l]), HD, axis=1)
        return lng, lnb, wst, wstt, bias

    def fetch(fn, *args, bring=()):
        bring = [(l, k) for l, k in bring if l < L]
        comm = _gather_comm([(shards[k], l, col_sharded[k]) for l, k in bring]) if bring else None
        res, got = fn(*args, comm)
        for (l, k), a in zip(bring, got):
            full[l][k] = a
        return res

    saved = []
    xc = x0
    for l in range(L):
        sh1, sc1, g1, sh2, sc2, g2, sh3, sc3, g3 = mods[l]
        lng, lnb, wst, wstt, bias = mix_consts(l)
        w = full[l]
        own = l == 0
        gu1, a1 = fetch(_ffn_up, xc, gains("norm_ffn1_g", l), sh1, sc1, w["ffn1_w_gu"],
                        bring=[(l, "ffn1_w_down"), (l, "mix_w_in"), (l, "mix_w_out")] if own else [(l, "ffn2_w_gu")])
        xa, f1 = fetch(_ffn_down, a1, xc, g1, w["ffn1_w_down"], bring=[(l, "ffn2_w_down")])
        proj, h2 = fetch(_mixin_fwd, xa, gains("norm_mix_g", l), sh2, sc2, w["mix_w_in"], bring=[(l + 1, "mix_w_in")])
        xb, yn, sv = fetch(_mix_core_fwd, proj, xa, g2, w["mix_w_out"], lng, lnb, wst, bias, pmat, conv_full[l],
                           gains("out_norm_g", l), bring=[(l, "ffn2_w_gu")] if own else [])
        gu2, a2 = fetch(_ffn_up, xb, gains("norm_ffn2_g", l), sh3, sc3, w["ffn2_w_gu"],
                        bring=[(l + 1, "ffn1_w_gu"), (l + 1, "mix_w_out")])
        xd, f2 = fetch(_ffn_down, a2, xb, g3, w["ffn2_w_down"], bring=[(l + 1, "ffn1_w_down")])
        saved.append(dict(x0=xc, xa=xa, xb=xb, gu1=gu1, a1=a1, f1=f1, proj=proj, h2=h2, yn=yn, sv=sv,
                          gu2=gu2, a2=a2, f2=f2))
        xc = xd

    dx, loss_block, d_final = _loss_head(xc, loss_target.reshape(T, D), final_norm_g.reshape(1, D))
    loss = lax.psum(loss_block[0, 0], ("x", "y", "c"))

    reduced = dict.fromkeys(big)

    def halves(name, g):
        if g.ndim == 4:
            return g
        return g.reshape(N_CHIP, 2, weights[name].shape[1] // 2, g.shape[-1])

    class Reduction:
        def __init__(self, l, name, g):
            self.l, self.name, self.g, self.stage = l, name, halves(name, g), 0
            self.ici_bytes = 3 * (g.size // 8) * 2

        def step(self):
            self.stage += 1
            if self.stage == 1:
                return _sibling_half_comm([self.g])
            if self.stage == 2:
                return _scatter_comm([_pair_sum(self.g, self.got[0], core)])
            if self.stage == 3:
                reduced[self.name] = _chip_sum(self.got[0], core, self.l, L, reduced[self.name])
                return _share_comm([reduced[self.name]], self.l)
            reduced[self.name] = self.got[0]
            return None

    active, extra, gathered = [], [], {}

    def carry(fn, *args, us=None):
        left = None if us is None else us * SCATTER_BYTES_PER_US
        riders = []
        for r in active:
            if r.stage == 1 and left is not None:
                if r.ici_bytes > left * SCATTER_OVERSHOOT:
                    continue
                left -= r.ici_bytes
            riders.append(r)
        comms = [r.step() for r in riders] + [cm for cm, _ in extra]
        takers = [functools.partial(setattr, r, "got") for r in riders] + [cb for _, cb in extra]
        extra.clear()
        if fn is None:
            res, got = None, (_comm_call(_merge(*comms), "reduce_alone") if comms else [])
        else:
            res, got = fn(*args, comm=_merge(*comms))
        at = 0
        for cm, take in zip(comms, takers):
            take(got[at:at + len(cm.out_shape)])
            at += len(cm.out_shape)
        for r in riders:
            if r.stage == 3:
                r.step()
                active.remove(r)
        return res

    def reduce_later(l, name, g):
        active.append(Reduction(l, name, g))

    small = [None] * L
    d_ada = [None] * L
    for l in reversed(range(L)):
        sh1, sc1, g1, sh2, sc2, g2, sh3, sc3, g3 = mods[l]
        lng, lnb, wst, wstt, bias = mix_consts(l)
        s = saved[l]
        w = full[l]
        dgu, = carry(_ffn_bwd_down, dx, s["gu2"], g3, w["ffn2_w_down"], us=100)
        dx, h3, df, dsc3, dsh3, dgain3, dg3 = carry(_ffn_bwd_up, dx, s["xb"], dgu, s["f2"], gains("norm_ffn2_g", l),
                                                    sh3, sc3, g3, w["ffn2_w_gu"], us=120)
        reduce_later(l, "ffn2_w_gu", carry(_wgrad, h3, dgu, D, 2 * F // N_CHIP, True, "wgrad_gu", WGRAD_TOKENS // 2,
                                           us=110)[0])
        reduce_later(l, "ffn2_w_down", carry(_wgrad, s["a2"], df[None], F // 2, D, False, "wgrad_down", us=50)[0])
        dproj, d_o, dg2, dog, dwst, dbias, dlng, dlnb, dconvw = carry(
            _mix_core_bwd, s["proj"], s["sv"], dx, g2, w["mix_w_out"], lng, lnb, wstt, pmat, conv_full[l],
            gains("out_norm_g", l), us=150)
        mix_grads = [
            lambda: reduce_later(l, "mix_w_out", carry(_wgrad, s["yn"], d_o[None], D, D, False, "wgrad_out", us=20)[0]),
            lambda: reduce_later(l, "mix_w_in", carry(_wgrad, s["h2"], dproj[None], D, P // N_CHIP, True, "wgrad_in",
                                                      us=65)[0])]
        last = l == 0
        if not last:
            mix_grads[0]()
        dx, dsc2, dsh2, dgain2 = carry(_mixin_bwd, dx, s["xa"], dproj, gains("norm_mix_g", l), sc2, w["mix_w_in"], us=60)
        if not last:
            mix_grads[1]()
        dgu, = carry(_ffn_bwd_down, dx, s["gu1"], g1, w["ffn1_w_down"], us=100)
        dx, h1, df, dsc1, dsh1, dgain1, dg1 = carry(_ffn_bwd_up, dx, s["x0"], dgu, s["f1"], gains("norm_ffn1_g", l),
                                                    sh1, sc1, g1, w["ffn1_w_gu"], us=120)
        d_ada[l] = jnp.concatenate([dsh1, dsc1, dg1, dsh2, dsc2, dg2, dsh3, dsc3, dg3], axis=1).reshape(B, N_MOD * D)
        small[l] = [dgain1, dgain2, dgain3, dog, dlng, dlnb, dwst, dbias[:, ::HD], dconvw]
        if last:
            flat = [a.reshape(-1, 128) for ll in range(L) for a in small[ll]] + [d_final.reshape(-1, 128)]
            pad = (-sum(a.shape[0] for a in flat)) % 8
            packed = jnp.concatenate(flat + [jnp.zeros((pad, 128), F32)], axis=0)
            extra.append((_all_gather_comm(jnp.stack(d_ada).reshape(L * B, N_MOD * D)),
                          lambda got: gathered.update(d_ada=got[0])))
            extra.append((_all_gather_comm(packed), lambda got: gathered.update(small=got[0])))
        reduce_later(l, "ffn1_w_gu", carry(_wgrad, h1, dgu, D, 2 * F // N_CHIP, True, "wgrad_gu", WGRAD_TOKENS // 2,
                                           us=110)[0])
        reduce_later(l, "ffn1_w_down", carry(_wgrad, s["a1"], df[None], F // 2, D, False, "wgrad_down", us=50)[0])
        if last:
            mix_grads[1]()
            mix_grads[0]()
    grad_x = dx.reshape(B, S, D)

    def finished(name):
        while any(r.name == name for r in active):
            carry(None)
        return reduced[name].reshape(weights[name].shape)

    grads = {}
    d_ada_all = jnp.transpose(gathered["d_ada"].reshape(N_DEV, L, B, N_MOD * D), (1, 0, 2, 3))
    d_ada_all = d_ada_all.reshape(L, n_all, N_MOD * D)
    grads["ada_b"] = _colsum(d_ada_all).reshape(L, N_MOD * D)
    d_ada_mine = lax.dynamic_slice_in_dim(d_ada_all, chip * SA, SA, axis=2).astype(BF16)
    grads["ada_w"] = carry(_ada_bwd, c_act, d_ada_mine, us=8)[0]

    total = _sum_blocks(gathered["small"].reshape(-1, 128), N_DEV)
    pieces, at = [], 0
    for a in flat:
        pieces.append(total[at:at + a.shape[0]])
        at += a.shape[0]
    per_layer = len(small[0])
    stack = lambda j, shape: jnp.stack([pieces[l * per_layer + j].reshape(shape) for l in range(L)])
    grads["norm_ffn1_g"] = stack(0, (D,))
    grads["norm_mix_g"] = stack(1, (D,))
    grads["norm_ffn2_g"] = stack(2, (D,))
    grads["out_norm_g"] = stack(3, (D,))
    grads["sgu_ln_g"] = stack(4, (N_HEADS, HD)).sum(axis=1)
    grads["sgu_ln_b"] = stack(5, (N_HEADS, HD)).sum(axis=1)
    grads["sgu_w_s"] = stack(6, (N_HEADS, CHUNK, CHUNK))
    grads["sgu_b"] = jnp.swapaxes(stack(7, (CHUNK, N_HEADS)), 1, 2)
    g_conv = stack(8, (8, DB))[:, :conv_w.shape[1]]
    grads["conv_w"] = lax.dynamic_slice_in_dim(g_conv, chip * conv_w.shape[2], conv_w.shape[2], axis=2)
    grads["final_norm_g"] = pieces[-1].reshape(D)

    names = list(weights)
    delta, new_m, new_v = {}, {}, {}
    for k in big:
        grads[k] = finished(k)
    for k in names:
        wk = weights[k]
        view = (1, wk.shape[0]) if wk.ndim == 1 else (-1, wk.shape[-1])
        d, nm, nv, *g_again = _adamw(wk.reshape(view), grads[k].reshape(view), m_in[k].reshape(view),
                                     v_in[k].reshape(view), emit_grad=k in big)
        delta[k], new_m[k], new_v[k] = d.reshape(wk.shape), nm.reshape(wk.shape), nv.reshape(wk.shape)
        if g_again:
            grads[k] = g_again[0].reshape(wk.shape)

    return (loss, grad_x, *[grads[k] for k in names], *[delta[k] for k in names],
            *[new_m[k] for k in names], *[new_v[k] for k in names])
```

```python
import functools
import math

import jax
import jax.numpy as jnp
from jax import lax
from jax.experimental import pallas as pl
from jax.experimental.pallas import tpu as pltpu

F32 = jnp.float32
BF16 = jnp.bfloat16
MESH = pl.DeviceIdType.MESH

N_HEADS = 8
CHUNK = 128
N_MOD = 9
EPS = 1e-6
N_DEV = 8
N_CHIP = 4

ADAM_LR = 0.001
ADAM_B1 = 0.9
ADAM_B2 = 0.999
ADAM_EPS = 1e-08
ADAM_WD = 0.01
ADAM_STEP = 10

TOKEN_TILE = 512
FF_SLAB = 768
MIX_TILE = 256
WGRAD_TOKENS = 2048
VMEM_LIMIT = 56 * 1024 * 1024

SCATTER_BYTES_PER_US = 68_000
SCATTER_OVERSHOOT = 1.25

ANY = pl.BlockSpec(memory_space=pl.ANY)


def _tile(pref, n):
    t = min(pref, n)
    assert n % t == 0, (pref, n)
    return t


def _slabs(n, width):
    return [slice(c0, min(c0 + width, n)) for c0 in range(0, n, width)]


def _dot(a, b):
    return jnp.dot(a, b, preferred_element_type=F32)


def _dot_nt(a, b):
    return lax.dot_general(a, b, (((1,), (1,)), ((), ())), preferred_element_type=F32)


def _dot_tn(a, b):
    return lax.dot_general(a, b, (((0,), (0,)), ((), ())), preferred_element_type=F32)


def _sigmoid(x):
    return 1.0 / (1.0 + jnp.exp(-x))


def _sigmoid_fast(x):
    return pl.reciprocal(1.0 + jnp.exp(-x), approx=True)


def _rms(x):
    r = lax.rsqrt(jnp.mean(x * x, axis=-1, keepdims=True) + EPS)
    return x * r, r


def _norm_mod_bwd(x, dh, gain, sc):
    xh, r = _rms(x)
    dsc = jnp.sum(dh * (xh * gain), axis=0, keepdims=True)
    dsh = jnp.sum(dh, axis=0, keepdims=True)
    dn = dh * (1.0 + sc)
    dgain = jnp.sum(dn * xh, axis=0, keepdims=True)
    dy = dn * gain
    dx = r * (dy - xh * jnp.mean(dy * xh, axis=-1, keepdims=True))
    return dx, dsc, dsh, dgain


def _acc(ref, first, val):
    @pl.when(first)
    def _():
        ref[...] = val

    @pl.when(jnp.logical_not(first))
    def _():
        ref[...] += val


class _Comm:
    def __init__(self, args, out_shape, scratch, phases, aliases=None):
        self.args, self.out_shape, self.scratch = list(args), list(out_shape), list(scratch)
        self.phases, self.aliases = phases, dict(aliases or {})


def _merge(*comms):
    comms = [c for c in comms if c is not None]
    if len(comms) <= 1:
        return comms[0] if comms else None
    args = [a for c in comms for a in c.args]
    out_shape = [o for c in comms for o in c.out_shape]
    scratch = [s for c in comms for s in c.scratch]
    aliases, ai, oi = {}, 0, 0
    for c in comms:
        aliases.update({ai + i: oi + o for i, o in c.aliases.items()})
        ai += len(c.args)
        oi += len(c.out_shape)

    def phases(ins, outs, sems):
        parts, ai, oi, si = [], 0, 0, 0
        for c in comms:
            parts.append(c.phases(ins[ai:ai + len(c.args)], outs[oi:oi + len(c.out_shape)], sems[si:si + len(c.scratch)]))
            ai, oi, si = ai + len(c.args), oi + len(c.out_shape), si + len(c.scratch)

        def run(k):
            def go():
                for p in parts:
                    if p[k] is not None:
                        p[k]()
            return go
        return run(0), run(1), run(2)

    return _Comm(args, out_shape, scratch, phases, aliases)


def _call(body, name, grid, in_specs, out_specs, out_shape, scratch, args, comm=None):
    n_in, n_out, n_scr = len(in_specs), len(out_specs), len(scratch)
    sem = ("arbitrary",) * len(grid)
    params = pltpu.CompilerParams(dimension_semantics=sem, vmem_limit_bytes=VMEM_LIMIT)
    if comm is None:
        res = pl.pallas_call(body, name=name, grid=grid, in_specs=in_specs, out_specs=out_specs, out_shape=out_shape,
                             scratch_shapes=scratch, compiler_params=params)(*args)
        return list(res), []
    m_in, m_out = len(comm.args), len(comm.out_shape)

    def full(*refs):
        c_in, c_min = refs[:n_in], refs[n_in:n_in + m_in]
        o = n_in + m_in
        c_out, c_mout = refs[o:o + n_out], refs[o + n_out:o + n_out + m_out]
        o += n_out + m_out
        c_scr, c_sem = refs[o:o + n_scr], refs[o + n_scr:]
        start, mid, finish = comm.phases(c_min, c_mout, c_sem)
        ids = [pl.program_id(a) for a in range(len(grid))]
        first = functools.reduce(jnp.logical_and, [i == 0 for i in ids])
        last = functools.reduce(jnp.logical_and, [i == g - 1 for i, g in zip(ids, grid)])
        pl.when(first)(start)
        if mid is not None:
            pl.when(last)(mid)
        body(*c_in, *c_out, *c_scr)
        pl.when(last)(finish)

    res = pl.pallas_call(
        full, name=name, grid=grid,
        in_specs=list(in_specs) + [ANY] * m_in,
        out_specs=list(out_specs) + [ANY] * m_out,
        out_shape=list(out_shape) + comm.out_shape,
        scratch_shapes=list(scratch) + comm.scratch,
        input_output_aliases={n_in + i: n_out + o for i, o in comm.aliases.items()},
        compiler_params=params,
    )(*args, *comm.args)
    return list(res[:n_out]), list(res[n_out:])


def _comm_call(comm, name):
    m_in, m_out = len(comm.args), len(comm.out_shape)

    def body(*refs):
        start, mid, finish = comm.phases(refs[:m_in], refs[m_in:m_in + m_out], refs[m_in + m_out:])
        start()
        if mid is not None:
            mid()
        finish()

    res = pl.pallas_call(
        body, name=name, in_specs=[ANY] * m_in, out_specs=[ANY] * m_out, out_shape=comm.out_shape,
        scratch_shapes=comm.scratch, input_output_aliases=comm.aliases,
    )(*comm.args)
    return list(res)


def _position():
    return lax.axis_index("x"), lax.axis_index("y"), lax.axis_index("c")


def _gather_comm(items):
    n = len(items)
    half = [s.shape[1] // 2 for s, _, _ in items]

    def full_shape(i):
        s, _, col = items[i]
        _, R, C = s.shape
        return jax.ShapeDtypeStruct((R, N_CHIP * C) if col else (N_CHIP * R, C), s.dtype)

    def phases(ins, outs, sems):
        send_sems, recv_sems, local_sems = sems
        x, y, c = _position()

        def region(i, chip, h):
            s, _, col = items[i]
            _, R, C = s.shape
            if col:
                return outs[i].at[pl.ds(h * half[i], half[i]), pl.ds(chip * C, C)]
            return outs[i].at[pl.ds(chip * R + h * half[i], half[i]), :]

        def mine(i, h):
            return ins[i].at[items[i][1], pl.ds(h * half[i], half[i]), :]

        def copies(kx, ky, kc):
            k_me = 2 * kx + ky
            sibling = (kx, ky, 1 - kc)
            chips = [(1 - kx, ky), (kx, 1 - ky), (1 - kx, 1 - ky)]
            local, first, passed, arrive_ici, arrive_d2d = [], [], [], [], []

            def remote(src, dst, s, to):
                return pltpu.make_async_remote_copy(src_ref=src, dst_ref=dst, send_sem=send_sems.at[s],
                                                    recv_sem=recv_sems.at[s], device_id=to, device_id_type=MESH)

            for i in range(n):
                for h in range(2):
                    local.append(pltpu.make_async_copy(mine(i, h), region(i, k_me, h), local_sems.at[2 * i + h]))
                for j, (px, py) in enumerate(chips):
                    s = 6 * i + j
                    first.append(remote(mine(i, kc), region(i, k_me, kc), s, (px, py, kc)))
                    got = region(i, 2 * px + py, kc)
                    arrive_ici.append(remote(got, got, s, (px, py, kc)))
                    passed.append(remote(got, got, s + 3, sibling))
                    other = region(i, 2 * px + py, 1 - kc)
                    arrive_d2d.append(remote(other, other, s + 3, sibling))
            return local, first, passed, arrive_ici, arrive_d2d

        def on_each_device(fn):
            def go():
                for kx in range(2):
                    for ky in range(2):
                        for kc in range(2):
                            pl.when((x == kx) & (y == ky) & (c == kc))(functools.partial(fn, *copies(kx, ky, kc)))
            return go

        def start(local, first, passed, arrive_ici, arrive_d2d):
            for cp in local + first:
                cp.start()

        def mid(local, first, passed, arrive_ici, arrive_d2d):
            for a, p in zip(arrive_ici, passed):
                a.wait_recv()
                p.start()

        def finish(local, first, passed, arrive_ici, arrive_d2d):
            for a in arrive_d2d:
                a.wait_recv()
            for cp in first + passed:
                cp.wait_send()
            for cp in local:
                cp.wait()

        return on_each_device(start), on_each_device(mid), on_each_device(finish)

    scratch = [pltpu.SemaphoreType.DMA((6 * n,)), pltpu.SemaphoreType.DMA((6 * n,)), pltpu.SemaphoreType.DMA((2 * n,))]
    return _Comm([s for s, _, _ in items], [full_shape(i) for i in range(n)], scratch, phases)


def _sibling_half_comm(gs):
    n = len(gs)

    def phases(ins, outs, sems):
        send_sems, recv_sems = sems
        x, y, c = _position()

        def copies():
            return [pltpu.make_async_remote_copy(
                src_ref=ins[i].at[:, 1 - c], dst_ref=outs[i], send_sem=send_sems.at[i], recv_sem=recv_sems.at[i],
                device_id=(x, y, 1 - c), device_id_type=MESH) for i in range(n)]

        def start():
            for cp in copies():
                cp.start()

        def finish():
            for cp in copies():
                cp.wait()

        return start, None, finish

    out_shape = [jax.ShapeDtypeStruct(g.shape[:1] + g.shape[2:], g.dtype) for g in gs]
    return _Comm(gs, out_shape, [pltpu.SemaphoreType.DMA((n,)), pltpu.SemaphoreType.DMA((n,))], phases)


def _scatter_comm(ps):
    n = len(ps)

    def phases(ins, outs, sems):
        send_sems, recv_sems, local_sems = sems
        x, y, c = _position()
        k_me = 2 * x + y
        chips = [(1 - x, y), (x, 1 - y), (1 - x, 1 - y)]

        def copies():
            local = [pltpu.make_async_copy(ins[i].at[k_me], outs[i].at[k_me], local_sems.at[i]) for i in range(n)]
            remote = [pltpu.make_async_remote_copy(
                src_ref=ins[i].at[2 * px + py], dst_ref=outs[i].at[k_me],
                send_sem=send_sems.at[3 * i + j], recv_sem=recv_sems.at[3 * i + j],
                device_id=(px, py, c), device_id_type=MESH) for i in range(n) for j, (px, py) in enumerate(chips)]
            return local, remote

        def start():
            local, remote = copies()
            for cp in local + remote:
                cp.start()

        def finish():
            local, remote = copies()
            for cp in remote + local:
                cp.wait()

        return start, None, finish

    scratch = [pltpu.SemaphoreType.DMA((3 * n,)), pltpu.SemaphoreType.DMA((3 * n,)), pltpu.SemaphoreType.DMA((n,))]
    return _Comm(ps, [jax.ShapeDtypeStruct(p.shape, p.dtype) for p in ps], scratch, phases)


def _share_comm(rs, l):
    n = len(rs)

    def phases(ins, outs, sems):
        send_sems, recv_sems = sems
        x, y, c = _position()

        def copy(i, h):
            return pltpu.make_async_remote_copy(
                src_ref=outs[i].at[l, h], dst_ref=outs[i].at[l, h], send_sem=send_sems.at[i], recv_sem=recv_sems.at[i],
                device_id=(x, y, 1 - c), device_id_type=MESH)

        def start():
            for i in range(n):
                copy(i, c).start()

        def finish():
            for i in range(n):
                copy(i, 1 - c).wait_recv()
            for i in range(n):
                copy(i, c).wait_send()

        return start, None, finish

    return _Comm(rs, [jax.ShapeDtypeStruct(r.shape, r.dtype) for r in rs],
                 [pltpu.SemaphoreType.DMA((n,)), pltpu.SemaphoreType.DMA((n,))], phases,
                 aliases={i: i for i in range(n)})


def _all_gather_comm(block):
    def phases(ins, outs, sems):
        send_sems, recv_sems, local_sem = sems
        (src,), (out,) = ins, outs
        x, y, c = _position()
        sibling = (x, y, 1 - c)
        chips = [(1 - x, y), (x, 1 - y), (1 - x, 1 - y)]

        def slot(px, py, pc):
            return out.at[4 * px + 2 * py + pc]

        def copy(k, blk, to, own=False):
            return pltpu.make_async_remote_copy(
                src_ref=src if own else slot(*blk), dst_ref=slot(*blk),
                send_sem=send_sems.at[k], recv_sem=recv_sems.at[k], device_id=to, device_id_type=MESH)

        mine = lambda: pltpu.make_async_copy(src, slot(x, y, c), local_sem.at[0])
        first = lambda: [copy(0, (x, y, c), sibling, True)] + [
            copy(1 + j, (x, y, c), (*chip, c), True) for j, chip in enumerate(chips)]
        passed = lambda: [copy(4 + j, (*chip, c), sibling) for j, chip in enumerate(chips)]

        def start():
            mine().start()
            for cp in first():
                cp.start()

        def mid():
            for j, (chip, p) in enumerate(zip(chips, passed())):
                copy(1 + j, (*chip, c), (x, y, c)).wait_recv()
                p.start()

        def finish():
            copy(0, sibling, (x, y, c)).wait_recv()
            for j, chip in enumerate(chips):
                copy(4 + j, (*chip, 1 - c), (x, y, c)).wait_recv()
            for cp in first() + passed():
                cp.wait_send()
            mine().wait()

        return start, mid, finish

    scratch = [pltpu.SemaphoreType.DMA((7,)), pltpu.SemaphoreType.DMA((7,)), pltpu.SemaphoreType.DMA((1,))]
    return _Comm([block], [jax.ShapeDtypeStruct((N_DEV,) + block.shape, block.dtype)], scratch, phases)


def _ffn_up(x, gain, sh, sc, wgu, comm=None):
    T, D = x.shape
    F = wgu.shape[1] // 2
    B = sh.shape[0]
    tm = _tile(TOKEN_TILE, T // B)
    tps = (T // B) // tm
    slabs = _slabs(F, FF_SLAB)

    def body(x_ref, gain_ref, sh_ref, sc_ref, w_ref, gu_ref, a_ref):
        xh, _ = _rms(x_ref[...])
        h = (xh * gain_ref[...] * (1.0 + sc_ref[0]) + sh_ref[0]).astype(BF16)

        def dots(s):
            return _dot(h, w_ref[:, s]), _dot(h, w_ref[:, slice(F + s.start, F + s.stop)])

        nxt = dots(slabs[0])
        for j, s in enumerate(slabs):
            g, u = nxt
            if j + 1 < len(slabs):
                nxt = dots(slabs[j + 1])
            gu_ref[0, :, s] = g.astype(BF16)
            gu_ref[1, :, s] = u.astype(BF16)
            a_ref[:, s] = (g * _sigmoid(g) * u).astype(BF16)

    seq = lambda i: (i // tps, 0, 0)
    return _call(
        body, "ffn_up", (T // tm,),
        [
            pl.BlockSpec((tm, D), lambda i: (i, 0)),
            pl.BlockSpec((1, D), lambda i: (0, 0)),
            pl.BlockSpec((1, 1, D), seq),
            pl.BlockSpec((1, 1, D), seq),
            pl.BlockSpec((D, 2 * F), lambda i: (0, 0), pipeline_mode=pl.Buffered(1)),
        ],
        [
            pl.BlockSpec((2, tm, F), lambda i: (0, i, 0)),
            pl.BlockSpec((tm, F), lambda i: (i, 0)),
        ],
        [
            jax.ShapeDtypeStruct((2, T, F), BF16),
            jax.ShapeDtypeStruct((T, F), BF16),
        ],
        [],
        (x, gain, sh, sc, wgu), comm)


def _ffn_down(a, x, gate, wd, comm=None):
    T, F = a.shape
    D = x.shape[1]
    B = gate.shape[0]
    tm = _tile(2 * TOKEN_TILE, T // B)
    tps = (T // B) // tm

    def body(a_ref, x_ref, gate_ref, wd_ref, xo_ref, f_ref):
        f = _dot(a_ref[...], wd_ref[...])
        f_ref[...] = f.astype(BF16)
        xo_ref[...] = x_ref[...] + 0.5 * gate_ref[0] * f

    return _call(
        body, "ffn_down", (T // tm,),
        [
            pl.BlockSpec((tm, F), lambda i: (i, 0)),
            pl.BlockSpec((tm, D), lambda i: (i, 0)),
            pl.BlockSpec((1, 1, D), lambda i: (i // tps, 0, 0)),
            pl.BlockSpec((F, D), lambda i: (0, 0), pipeline_mode=pl.Buffered(1)),
        ],
        [pl.BlockSpec((tm, D), lambda i: (i, 0)), pl.BlockSpec((tm, D), lambda i: (i, 0))],
        [jax.ShapeDtypeStruct((T, D), F32), jax.ShapeDtypeStruct((T, D), BF16)],
        [],
        (a, x, gate, wd), comm)


def _ffn_bwd_down(dxo, gu, gate, wd, comm=None):
    T, D = dxo.shape
    F = wd.shape[0]
    B = gate.shape[0]
    tm = _tile(TOKEN_TILE, T // B)
    tps = (T // B) // tm
    slabs = _slabs(F, FF_SLAB)

    def body(dxo_ref, gu_ref, gate_ref, wd_ref, dgu_ref):
        df = (0.5 * gate_ref[0] * dxo_ref[...]).astype(BF16)
        nxt = _dot_nt(df, wd_ref[slabs[0], :])
        for j, s in enumerate(slabs):
            da = nxt
            if j + 1 < len(slabs):
                nxt = _dot_nt(df, wd_ref[slabs[j + 1], :])
            g = gu_ref[0, :, s]
            sg = 1.0 / (1.0 + jnp.exp(-g))
            t = g * sg
            dab = da.astype(BF16)
            dgu_ref[1, :, s] = dab * t
            dgu_ref[0, :, s] = dab * gu_ref[1, :, s] * (sg + t - t * sg)

    return _call(
        body, "ffn_bwd_down", (T // tm,),
        [
            pl.BlockSpec((tm, D), lambda i: (i, 0)),
            pl.BlockSpec((2, tm, F), lambda i: (0, i, 0)),
            pl.BlockSpec((1, 1, D), lambda i: (i // tps, 0, 0)),
            pl.BlockSpec((F, D), lambda i: (0, 0), pipeline_mode=pl.Buffered(1)),
        ],
        [pl.BlockSpec((2, tm, F), lambda i: (0, i, 0))],
        [jax.ShapeDtypeStruct((2, T, F), BF16)],
        [],
        (dxo, gu, gate, wd), comm)


def _ffn_bwd_up(dxo, x, dgu, f, gain, sh, sc, gate, wgu, comm=None):
    T, D = x.shape
    F = wgu.shape[1] // 2
    B = sc.shape[0]
    tm = _tile(TOKEN_TILE, T // B)
    tps = (T // B) // tm

    def body(dxo_ref, x_ref, dgu_ref, f_ref, gain_ref, sh_ref, sc_ref, gate_ref, w_ref,
             dx_ref, h_ref, df_ref, dsc_ref, dsh_ref, dgain_ref, dgate_ref):
        i = pl.program_id(0)
        first_of_seq = (i % tps) == 0
        gain = gain_ref[...]
        sc = sc_ref[0]
        halves = _slabs(tm, tm // 2)

        def dots(r):
            return _dot_nt(dgu_ref[0, r, :], w_ref[:, 0:F]) + _dot_nt(dgu_ref[1, r, :], w_ref[:, F:])

        nxt = dots(halves[0])
        sums = None
        for j, r in enumerate(halves):
            dh = nxt
            if j + 1 < len(halves):
                nxt = dots(halves[j + 1])
            dxo = dxo_ref[r, :]
            x = x_ref[r, :]
            dx, dsc, dsh, dgain = _norm_mod_bwd(x, dh, gain, sc)
            dx_ref[r, :] = dxo + dx
            h_ref[r, :] = (_rms(x)[0] * gain * (1.0 + sc) + sh_ref[0]).astype(BF16)
            df_ref[r, :] = (0.5 * gate_ref[0] * dxo).astype(BF16)
            part = (dsc, dsh, dgain, 0.5 * jnp.sum(dxo * f_ref[r, :].astype(F32), axis=0, keepdims=True))
            sums = part if sums is None else tuple(a + b for a, b in zip(sums, part))
        _acc(dsc_ref.at[0], first_of_seq, sums[0])
        _acc(dsh_ref.at[0], first_of_seq, sums[1])
        _acc(dgain_ref, i == 0, sums[2])
        _acc(dgate_ref.at[0], first_of_seq, sums[3])

    seq = lambda i: (i // tps, 0, 0)
    row = lambda i: (i, 0)
    return _call(
        body, "ffn_bwd_up", (T // tm,),
        [
            pl.BlockSpec((tm, D), row),
            pl.BlockSpec((tm, D), row),
            pl.BlockSpec((2, tm, F), lambda i: (0, i, 0)),
            pl.BlockSpec((tm, D), row),
            pl.BlockSpec((1, D), lambda i: (0, 0)),
            pl.BlockSpec((1, 1, D), seq),
            pl.BlockSpec((1, 1, D), seq),
            pl.BlockSpec((1, 1, D), seq),
            pl.BlockSpec((D, 2 * F), lambda i: (0, 0), pipeline_mode=pl.Buffered(1)),
        ],
        [
            pl.BlockSpec((tm, D), row),
            pl.BlockSpec((tm, D), row),
            pl.BlockSpec((tm, D), row),
            pl.BlockSpec((1, 1, D), seq),
            pl.BlockSpec((1, 1, D), seq),
            pl.BlockSpec((1, D), lambda i: (0, 0)),
            pl.BlockSpec((1, 1, D), seq),
        ],
        [
            jax.ShapeDtypeStruct((T, D), F32),
            jax.ShapeDtypeStruct((T, D), BF16),
            jax.ShapeDtypeStruct((T, D), BF16),
            jax.ShapeDtypeStruct((B, 1, D), F32),
            jax.ShapeDtypeStruct((B, 1, D), F32),
            jax.ShapeDtypeStruct((1, D), F32),
            jax.ShapeDtypeStruct((B, 1, D), F32),
        ],
        [],
        (dxo, x, dgu, f, gain, sh, sc, gate, wgu), comm)


def _wgrad(a, b, tmm, tn, col_major, name, tokens=WGRAD_TOKENS, comm=None):
    T, M = a.shape
    nb, _, Nb = b.shape
    N = nb * Nb
    tk = _tile(tokens, T)
    span = 2 if col_major else 1
    wide = span * tn
    npb = Nb // wide
    assert M % tmm == 0 and Nb % wide == 0
    if col_major:
        assert tmm == M
        shape = (N // tn, 2, M // 2, tn)
        out_spec = pl.BlockSpec((span, 2, M // 2, tn), lambda i, j, t: (j, 0, 0, 0))
    else:
        shape = (M // tmm, tmm, N)
        out_spec = pl.BlockSpec((None, tmm, tn), lambda i, j, t: (i, 0, j))

    def body(a_ref, b_ref, o_ref):
        t = pl.program_id(2)
        res = _dot_tn(a_ref[...], b_ref[...])
        if col_major:
            for s in range(span):
                for h in range(2):
                    _acc(o_ref.at[s, h], t == 0, res[h * (M // 2):(h + 1) * (M // 2), s * tn:(s + 1) * tn])
        else:
            _acc(o_ref, t == 0, res)

    return _call(
        body, name, (M // tmm, N // wide, T // tk),
        [
            pl.BlockSpec((tk, tmm), lambda i, j, t: (t, i)),
            pl.BlockSpec((None, tk, wide), lambda i, j, t: (j // npb, t, j % npb)),
        ],
        [out_spec], [jax.ShapeDtypeStruct(shape, F32)], [],
        (a, b), comm)


def _mixin_fwd(x, gain, sh, sc, win, comm=None):
    T, D = x.shape
    P = win.shape[1]
    B = sh.shape[0]
    tm = _tile(TOKEN_TILE, T // B)
    tps = (T // B) // tm

    def body(x_ref, gain_ref, sh_ref, sc_ref, w_ref, proj_ref, h_ref):
        xh, _ = _rms(x_ref[...])
        h = (xh * gain_ref[...] * (1.0 + sc_ref[0]) + sh_ref[0]).astype(BF16)
        h_ref[...] = h
        proj_ref[...] = _dot(h, w_ref[...])

    seq = lambda i: (i // tps, 0, 0)
    return _call(
        body, "mixin_fwd", (T // tm,),
        [
            pl.BlockSpec((tm, D), lambda i: (i, 0)),
            pl.BlockSpec((1, D), lambda i: (0, 0)),
            pl.BlockSpec((1, 1, D), seq),
            pl.BlockSpec((1, 1, D), seq),
            pl.BlockSpec((D, P), lambda i: (0, 0)),
        ],
        [pl.BlockSpec((tm, P), lambda i: (i, 0)), pl.BlockSpec((tm, D), lambda i: (i, 0))],
        [jax.ShapeDtypeStruct((T, P), F32), jax.ShapeDtypeStruct((T, D), BF16)],
        [],
        (x, gain, sh, sc, win), comm)


def _mixin_bwd(dxo, x, dproj, gain, sc, win, comm=None):
    T, D = x.shape
    P = win.shape[1]
    B = sc.shape[0]
    tm = _tile(TOKEN_TILE, T // B)
    tps = (T // B) // tm

    def body(dxo_ref, x_ref, dp_ref, gain_ref, sc_ref, w_ref, dx_ref, dsc_ref, dsh_ref, dgain_ref):
        i = pl.program_id(0)
        first_of_seq = (i % tps) == 0
        halves = _slabs(tm, tm // 2)
        nxt = _dot_nt(dp_ref[halves[0], :], w_ref[...])
        sums = None
        for j, r in enumerate(halves):
            dh = nxt
            if j + 1 < len(halves):
                nxt = _dot_nt(dp_ref[halves[j + 1], :], w_ref[...])
            part = _norm_mod_bwd(x_ref[r, :], dh, gain_ref[...], sc_ref[0])
            dx_ref[r, :] = dxo_ref[r, :] + part[0]
            sums = part[1:] if sums is None else tuple(a + b for a, b in zip(sums, part[1:]))
        _acc(dsc_ref.at[0], first_of_seq, sums[0])
        _acc(dsh_ref.at[0], first_of_seq, sums[1])
        _acc(dgain_ref, i == 0, sums[2])

    seq = lambda i: (i // tps, 0, 0)
    row = lambda i: (i, 0)
    return _call(
        body, "mixin_bwd", (T // tm,),
        [
            pl.BlockSpec((tm, D), row),
            pl.BlockSpec((tm, D), row),
            pl.BlockSpec((tm, P), row),
            pl.BlockSpec((1, D), lambda i: (0, 0)),
            pl.BlockSpec((1, 1, D), seq),
            pl.BlockSpec((D, P), lambda i: (0, 0)),
        ],
        [
            pl.BlockSpec((tm, D), row),
            pl.BlockSpec((1, 1, D), seq),
            pl.BlockSpec((1, 1, D), seq),
            pl.BlockSpec((1, D), lambda i: (0, 0)),
        ],
        [
            jax.ShapeDtypeStruct((T, D), F32),
            jax.ShapeDtypeStruct((B, 1, D), F32),
            jax.ShapeDtypeStruct((B, 1, D), F32),
            jax.ShapeDtypeStruct((1, D), F32),
        ],
        [],
        (dxo, x, dproj, gain, sc, win), comm)


def _head_mean(z, pmat, exact=True):
    hi = z.astype(BF16)
    if not exact:
        return _dot(hi, pmat)
    lo = (z - hi.astype(F32)).astype(BF16)
    return _dot(hi, pmat) + _dot(lo, pmat)


def _gelu_parts(x):
    cdf = 0.5 * (1.0 + lax.erf(x * (1.0 / math.sqrt(2.0))))
    return x * cdf, cdf


def _gelu_grad(x, cdf):
    return cdf + x * jnp.exp(-0.5 * x * x) * (1.0 / math.sqrt(2.0 * math.pi))


LANES = 128


def _head_blocks(da):
    hd = da // N_HEADS
    lb = min(LANES, da)
    col = lax.broadcasted_iota(jnp.int32, (1, lb), 1)
    return lb, lb // hd, da // lb, [(col >= h * hd) & (col < (h + 1) * hd) for h in range(lb // hd)]


def _mix_heads(w_stack, v, da):
    lb, hpb, nb, masks = _head_blocks(da)
    outs = []
    for b in range(nb):
        res = _dot(w_stack[b * hpb * CHUNK:(b + 1) * hpb * CHUNK], v[:, b * lb:(b + 1) * lb])
        out = res[0:CHUNK]
        for h in range(1, hpb):
            out = jnp.where(masks[h], res[h * CHUNK:(h + 1) * CHUNK], out)
        outs.append(out)
    return outs[0] if nb == 1 else jnp.concatenate(outs, axis=1)


def _mix_heads_grad(dm, v, da):
    lb, hpb, nb, masks = _head_blocks(da)
    outs = []
    for b in range(nb):
        dmb = dm[:, b * lb:(b + 1) * lb]
        stack = jnp.concatenate([jnp.where(masks[h], dmb, jnp.zeros_like(dmb)) for h in range(hpb)], axis=0)
        outs.append(_dot_nt(stack, v[:, b * lb:(b + 1) * lb]))
    return outs[0] if nb == 1 else jnp.concatenate(outs, axis=0)


def _causal_stack(w, transposed):
    r = lax.broadcasted_iota(jnp.int32, w.shape, 0) % CHUNK
    c = lax.broadcasted_iota(jnp.int32, w.shape, 1)
    keep = (c >= r) if transposed else (c <= r)
    return jnp.where(keep, w, 0.0)


def _mix_core_forward(proj, zprev, prm, da, db, saved=None):
    n = proj.shape[0]
    ua = proj[:, 0:da]
    va = proj[:, da:2 * da]
    bg = proj[:, 2 * da:2 * da + db]
    cg = proj[:, 2 * da + db:2 * da + 2 * db]
    xb = proj[:, 2 * da + 2 * db:]
    if saved is None:
        ug, ucdf = _gelu_parts(ua)
        vg, vcdf = _gelu_parts(va)
        zc = vg - _head_mean(vg, prm["pmat"])
        rs = lax.rsqrt(_head_mean(zc * zc, prm["pmat"], exact=False) + EPS)
        vhat = zc * rs
        vln = (vhat * prm["lng"] + prm["lnb"]).astype(BF16)
        wst = _causal_stack(prm["wst"], False).astype(BF16)
        mixed = [_mix_heads(wst, vln[j * CHUNK:(j + 1) * CHUNK], da) + prm["bias"] for j in range(n // CHUNK)]
        mixed = mixed[0] if len(mixed) == 1 else jnp.concatenate(mixed, axis=0)
    else:
        ucdf, vcdf, vhat, rs, mixed = [saved[k].astype(F32) for k in range(5)]
        ug = ua * ucdf
        vln = (vhat * prm["lng"] + prm["lnb"]).astype(BF16)
    ya = ug * mixed
    z = cg * xb
    row = lax.broadcasted_iota(jnp.int32, z.shape, 0)
    z1 = jnp.where(row == 0, zprev[7:8], pltpu.roll(z, 1, 0))
    z2 = jnp.where(row == 0, zprev[6:7], jnp.where(row == 1, zprev[7:8], pltpu.roll(z, 2, 0)))
    cw = prm["convw"]
    conv = z2 * cw[0:1] + z1 * cw[1:2] + z * cw[2:3]
    yb = bg * conv
    yah, ra = _rms(ya)
    ybh, rb = _rms(yb)
    return dict(ua=ua, va=va, bg=bg, cg=cg, xb=xb, ug=ug, ucdf=ucdf, vcdf=vcdf, rs=rs, vhat=vhat, vln=vln,
                mixed=mixed, z=z, z1=z1, z2=z2, conv=conv, yah=yah, ra=ra, ybh=ybh, rb=rb)


def _mix_params(lng_ref, lnb_ref, wst_ref, bias_ref, pmat_ref, convw_ref):
    return dict(lng=lng_ref[...], lnb=lnb_ref[...], wst=wst_ref[...], bias=bias_ref[...],
                pmat=pmat_ref[...], convw=convw_ref[...])


def _mix_core_fwd(proj, x, gate, wout, lng, lnb, wst, bias, pmat, convw, og, comm=None):
    T, P = proj.shape
    D = x.shape[1]
    B = gate.shape[0]
    da = lng.shape[1]
    db = convw.shape[1]
    tm = _tile(MIX_TILE, T // B)
    tps = (T // B) // tm

    def body(proj_ref, x_ref, gate_ref, wout_ref, lng_ref, lnb_ref, wst_ref, bias_ref, pmat_ref, convw_ref,
             og_ref, xo_ref, yn_ref, sv_ref, halo):
        i = pl.program_id(0)

        @pl.when((i % tps) == 0)
        def _():
            halo[...] = jnp.zeros_like(halo)

        prm = _mix_params(lng_ref, lnb_ref, wst_ref, bias_ref, pmat_ref, convw_ref)
        r = _mix_core_forward(proj_ref[...], halo[...], prm, da, db)
        halo[...] = r["z"][tm - 8:tm]
        for k, name in enumerate(("ucdf", "vcdf", "vhat", "rs", "mixed")):
            sv_ref[k] = r[name].astype(BF16)
        og = og_ref[...]
        yn_ref[:, 0:da] = (r["yah"] * og[:, 0:da]).astype(BF16)
        yn_ref[:, da:] = (r["ybh"] * og[:, da:]).astype(BF16)
        xo_ref[...] = x_ref[...] + gate_ref[0] * _dot(yn_ref[...], wout_ref[...])

    full = lambda a: pl.BlockSpec(a.shape, lambda i: (0,) * a.ndim)
    return _call(
        body, "mix_core_fwd", (T // tm,),
        [
            pl.BlockSpec((tm, P), lambda i: (i, 0)),
            pl.BlockSpec((tm, D), lambda i: (i, 0)),
            pl.BlockSpec((1, 1, D), lambda i: (i // tps, 0, 0)),
            full(wout), full(lng), full(lnb), full(wst), full(bias), full(pmat), full(convw), full(og),
        ],
        [pl.BlockSpec((tm, D), lambda i: (i, 0)), pl.BlockSpec((tm, D), lambda i: (i, 0)),
         pl.BlockSpec((5, tm, da), lambda i: (0, i, 0))],
        [jax.ShapeDtypeStruct((T, D), F32), jax.ShapeDtypeStruct((T, D), BF16),
         jax.ShapeDtypeStruct((5, T, da), BF16)],
        [pltpu.VMEM((8, db), F32)],
        (proj, x, gate, wout, lng, lnb, wst, bias, pmat, convw, og), comm)


def _mix_core_bwd(proj, sv, dxo, gate, wout, lng, lnb, wstt, pmat, convw, og, comm=None):
    T, P = proj.shape
    D = dxo.shape[1]
    B = gate.shape[0]
    da = lng.shape[1]
    db = convw.shape[1]
    assert da == db and P == 2 * da + 3 * db
    tm = _tile(MIX_TILE, T // B)
    tps = (T // B) // tm
    nt = T // tm
    hd = da // N_HEADS

    def body(proj_ref, cgp_ref, xbp_ref, sv_ref, dxo_ref, gate_ref, wout_ref, lng_ref, lnb_ref, wstt_ref,
             pmat_ref, convw_ref, og_ref,
             dproj_ref, do_ref, dgate_ref, dog_ref, dwst_ref, dbias_ref, dlng_ref, dlnb_ref, dconvw_ref, carry):
        i = pl.program_id(0)
        ri = nt - 1 - i
        first = i == 0
        end_of_seq = (ri % tps) == tps - 1
        start_of_seq = (ri % tps) == 0

        @pl.when(end_of_seq)
        def _():
            carry[...] = jnp.zeros_like(carry)

        prm = dict(lng=lng_ref[...], lnb=lnb_ref[...], pmat=pmat_ref[...], convw=convw_ref[...])
        zprev = jnp.where(start_of_seq, 0.0, cgp_ref[...] * xbp_ref[...])
        r = _mix_core_forward(proj_ref[...], zprev, prm, da, db, saved=sv_ref)
        og = og_ref[...]
        pmat = prm["pmat"]

        yn = jnp.concatenate([(r["yah"] * og[:, 0:da]).astype(BF16), (r["ybh"] * og[:, da:]).astype(BF16)], axis=1)
        dxo = dxo_ref[...]
        o = _dot(yn, wout_ref[...])
        _acc(dgate_ref.at[0], end_of_seq, jnp.sum(dxo * o, axis=0, keepdims=True))
        d_o = (gate_ref[0] * dxo).astype(BF16)
        do_ref[...] = d_o
        dyn = _dot_nt(d_o, wout_ref[...])

        def rms_bwd(dyn_g, yh, rr, og_g):
            dog_g = jnp.sum(dyn_g * yh, axis=0, keepdims=True)
            dyh = dyn_g * og_g
            return rr * (dyh - yh * jnp.mean(dyh * yh, axis=-1, keepdims=True)), dog_g

        dya, dog_a = rms_bwd(dyn[:, 0:da], r["yah"], r["ra"], og[:, 0:da])
        dyb, dog_b = rms_bwd(dyn[:, da:], r["ybh"], r["rb"], og[:, da:])
        _acc(dog_ref, first, jnp.concatenate([dog_a, dog_b], axis=1))

        dug = dya * r["mixed"]
        dmixed = dya * r["ug"]
        wstt_b = _causal_stack(wstt_ref[...], True).astype(BF16)
        dbias = jnp.zeros((CHUNK, da), F32)
        dwst = jnp.zeros((N_HEADS * CHUNK, CHUNK), F32)
        dvln = []
        for j in range(tm // CHUNK):
            dm = dmixed[j * CHUNK:(j + 1) * CHUNK]
            dbias = dbias + dm
            dmb = dm.astype(BF16)
            dwst = dwst + _mix_heads_grad(dmb, r["vln"][j * CHUNK:(j + 1) * CHUNK], da)
            dvln.append(_mix_heads(wstt_b, dmb, da))
        dvln = dvln[0] if len(dvln) == 1 else jnp.concatenate(dvln, axis=0)
        _acc(dbias_ref, first, dbias)
        _acc(dwst_ref, first, dwst)
        _acc(dlng_ref, first, jnp.sum(dvln * r["vhat"], axis=0, keepdims=True))
        _acc(dlnb_ref, first, jnp.sum(dvln, axis=0, keepdims=True))
        dvhat = dvln * prm["lng"]
        dvg = r["rs"] * (dvhat - _head_mean(dvhat, pmat, exact=False)
                         - r["vhat"] * _head_mean(dvhat * r["vhat"], pmat, exact=False))
        dproj_ref[:, 0:da] = (dug * _gelu_grad(r["ua"], r["ucdf"])).astype(BF16)
        dproj_ref[:, da:2 * da] = (dvg * _gelu_grad(r["va"], r["vcdf"])).astype(BF16)

        dproj_ref[:, 2 * da:2 * da + db] = (dyb * r["conv"]).astype(BF16)
        dconv = dyb * r["bg"]
        dcw = jnp.concatenate([
            jnp.sum(dconv * r["z2"], axis=0, keepdims=True),
            jnp.sum(dconv * r["z1"], axis=0, keepdims=True),
            jnp.sum(dconv * r["z"], axis=0, keepdims=True),
            jnp.zeros((5, db), F32)], axis=0)
        _acc(dconvw_ref, first, dcw)
        nxt = carry[...]
        row = lax.broadcasted_iota(jnp.int32, dconv.shape, 0)
        dc1 = jnp.where(row == tm - 1, nxt[0:1], pltpu.roll(dconv, tm - 1, 0))
        dc2 = jnp.where(row == tm - 2, nxt[0:1], jnp.where(row == tm - 1, nxt[1:2], pltpu.roll(dconv, tm - 2, 0)))
        carry[...] = dconv[0:8]
        cw = prm["convw"]
        dz = dconv * cw[2:3] + dc1 * cw[1:2] + dc2 * cw[0:1]
        dproj_ref[:, 2 * da + db:2 * da + 2 * db] = (dz * r["xb"]).astype(BF16)
        dproj_ref[:, 2 * da + 2 * db:] = (dz * r["cg"]).astype(BF16)

        @pl.when(i == nt - 1)
        def _():
            dwst_ref[...] = _causal_stack(dwst_ref[...], False)
            dbias_ref[...] = _head_mean(dbias_ref[...], pmat) * float(hd)

    full = lambda a: pl.BlockSpec(a.shape, lambda i: (0,) * a.ndim)
    const = lambda i: (0, 0)
    rev = lambda i: (nt - 1 - i, 0)
    prev8 = lambda col: (lambda i: (jnp.maximum((nt - 1 - i) * (tm // 8) - 1, 0), col))
    return _call(
        body, "mix_core_bwd", (nt,),
        [
            pl.BlockSpec((tm, P), rev),
            pl.BlockSpec((8, db), prev8((2 * da + db) // db)),
            pl.BlockSpec((8, db), prev8((2 * da + 2 * db) // db)),
            pl.BlockSpec((5, tm, da), lambda i: (0, nt - 1 - i, 0)),
            pl.BlockSpec((tm, D), rev),
            pl.BlockSpec((1, 1, D), lambda i: ((nt - 1 - i) // tps, 0, 0)),
            full(wout), full(lng), full(lnb), full(wstt), full(pmat), full(convw), full(og),
        ],
        [
            pl.BlockSpec((tm, P), rev),
            pl.BlockSpec((tm, D), rev),
            pl.BlockSpec((1, 1, D), lambda i: ((nt - 1 - i) // tps, 0, 0)),
            pl.BlockSpec((1, D), const),
            pl.BlockSpec((N_HEADS * CHUNK, CHUNK), const),
            pl.BlockSpec((CHUNK, da), const),
            pl.BlockSpec((1, da), const),
            pl.BlockSpec((1, da), const),
            pl.BlockSpec((8, db), const),
        ],
        [
            jax.ShapeDtypeStruct((T, P), BF16),
            jax.ShapeDtypeStruct((T, D), BF16),
            jax.ShapeDtypeStruct((B, 1, D), F32),
            jax.ShapeDtypeStruct((1, D), F32),
            jax.ShapeDtypeStruct((N_HEADS * CHUNK, CHUNK), F32),
            jax.ShapeDtypeStruct((CHUNK, da), F32),
            jax.ShapeDtypeStruct((1, da), F32),
            jax.ShapeDtypeStruct((1, da), F32),
            jax.ShapeDtypeStruct((8, db), F32),
        ],
        [pltpu.VMEM((8, db), F32)],
        (proj, proj, proj, sv, dxo, gate, wout, lng, lnb, wstt, pmat, convw, og), comm)


def _loss_head(x, target, gain):
    T, D = x.shape
    tm = _tile(TOKEN_TILE, T)

    def body(x_ref, t_ref, gain_ref, dx_ref, loss_ref, dgain_ref):
        first = pl.program_id(0) == 0
        xh, r = _rms(x_ref[...])
        gain = gain_ref[...]
        err = xh * gain - t_ref[...]
        _acc(loss_ref, first, jnp.zeros((8, 128), F32) + 0.5 * jnp.sum(err * err) / D)
        dout = err * (1.0 / D)
        _acc(dgain_ref, first, jnp.sum(dout * xh, axis=0, keepdims=True))
        dy = dout * gain
        dx_ref[...] = r * (dy - xh * jnp.mean(dy * xh, axis=-1, keepdims=True))

    return _call(
        body, "loss_head", (T // tm,),
        [
            pl.BlockSpec((tm, D), lambda i: (i, 0)),
            pl.BlockSpec((tm, D), lambda i: (i, 0)),
            pl.BlockSpec((1, D), lambda i: (0, 0)),
        ],
        [
            pl.BlockSpec((tm, D), lambda i: (i, 0)),
            pl.BlockSpec((8, 128), lambda i: (0, 0)),
            pl.BlockSpec((1, D), lambda i: (0, 0)),
        ],
        [
            jax.ShapeDtypeStruct((T, D), F32),
            jax.ShapeDtypeStruct((8, 128), F32),
            jax.ShapeDtypeStruct((1, D), F32),
        ],
        [],
        (x, target, gain))[0]


def _ada_fwd(c_all, ada_w, ada_b):
    n, D = c_all.shape
    L, _, sa = ada_w.shape
    tn = _tile(768, sa)

    def body(c_ref, w_ref, b_ref, act_ref, o_ref):
        c = c_ref[...]
        act = (c * _sigmoid(c)).astype(BF16)
        act_ref[...] = act
        o_ref[...] = _dot(act, w_ref[...].astype(BF16)) + b_ref[...]

    return _call(
        body, "ada_fwd", (L, sa // tn),
        [
            pl.BlockSpec((n, D), lambda l, j: (0, 0)),
            pl.BlockSpec((None, D, tn), lambda l, j: (l, 0, j)),
            pl.BlockSpec((None, 1, tn), lambda l, j: (l, 0, j)),
        ],
        [
            pl.BlockSpec((n, D), lambda l, j: (0, 0)),
            pl.BlockSpec((None, n, tn), lambda l, j: (l, 0, j)),
        ],
        [jax.ShapeDtypeStruct((n, D), BF16), jax.ShapeDtypeStruct((L, n, sa), F32)],
        [],
        (c_all, ada_w, ada_b))[0]


def _ada_bwd(c_act, d_ada, comm=None):
    n, D = c_act.shape
    L, _, sa = d_ada.shape
    tn = _tile(768, sa)

    def body(c_ref, d_ref, o_ref):
        o_ref[...] = _dot_tn(c_ref[...], d_ref[...])

    return _call(
        body, "ada_bwd", (L, sa // tn),
        [pl.BlockSpec((n, D), lambda l, j: (0, 0)), pl.BlockSpec((None, n, tn), lambda l, j: (l, 0, j))],
        [pl.BlockSpec((None, D, tn), lambda l, j: (l, 0, j))],
        [jax.ShapeDtypeStruct((L, D, sa), F32)],
        [],
        (c_act, d_ada), comm)


def _colsum(a):
    L, n, C = a.shape

    def body(a_ref, o_ref):
        o_ref[...] = jnp.sum(a_ref[...], axis=0, keepdims=True)

    return _call(
        body, "colsum", (L,),
        [pl.BlockSpec((None, n, C), lambda l: (l, 0, 0))],
        [pl.BlockSpec((None, 1, C), lambda l: (l, 0, 0))],
        [jax.ShapeDtypeStruct((L, 1, C), F32)],
        [],
        (a,))[0][0]


def _row_tile(rows, cols, nbuf):
    budget = VMEM_LIMIT // 3 // (2 * nbuf * 4 * cols)
    t = rows
    while t > max(budget, 8) and t % 2 == 0 and (t // 2) % 8 == 0:
        t //= 2
    return t


def _pair_sum(g, recv, core):
    n, _, R, C = g.shape
    tr = _row_tile(R, C, 3)

    def body(core_ref, g_ref, r_ref, o_ref):
        o_ref[...] = (g_ref[...] + r_ref[...]).astype(BF16)

    return pl.pallas_call(
        body,
        name="pair_sum",
        grid_spec=pltpu.PrefetchScalarGridSpec(
            num_scalar_prefetch=1,
            grid=(n, R // tr),
            in_specs=[
                pl.BlockSpec((None, None, tr, C), lambda i, r, core_ref: (i, core_ref[0], r, 0)),
                pl.BlockSpec((None, tr, C), lambda i, r, core_ref: (i, r, 0)),
            ],
            out_specs=pl.BlockSpec((None, tr, C), lambda i, r, core_ref: (i, r, 0)),
        ),
        out_shape=jax.ShapeDtypeStruct((n, R, C), BF16),
        compiler_params=pltpu.CompilerParams(dimension_semantics=("arbitrary", "arbitrary"),
                                             vmem_limit_bytes=VMEM_LIMIT),
    )(core, g, recv)


def _chip_sum(q, core, l, n_layers, prev):
    nq, R, C = q.shape
    tr = _row_tile(R, C, 4)

    def body(core_ref, q_ref, *rest):
        o_ref = rest[-1]
        s = q_ref[0].astype(F32)
        for j in range(1, nq):
            s = s + q_ref[j].astype(F32)
        o_ref[...] = s

    in_specs = [pl.BlockSpec((nq, tr, C), lambda r, core_ref: (0, r, 0))]
    args = [core, q]
    aliases = {}
    if prev is not None:
        in_specs.append(ANY)
        args.append(prev)
        aliases = {2: 0}
    return pl.pallas_call(
        body,
        name="chip_sum",
        grid_spec=pltpu.PrefetchScalarGridSpec(
            num_scalar_prefetch=1,
            grid=(R // tr,),
            in_specs=in_specs,
            out_specs=pl.BlockSpec((None, None, tr, C), lambda r, core_ref: (l, core_ref[0], r, 0)),
        ),
        out_shape=jax.ShapeDtypeStruct((n_layers, 2, R, C), F32),
        input_output_aliases=aliases,
        compiler_params=pltpu.CompilerParams(dimension_semantics=("arbitrary",), vmem_limit_bytes=VMEM_LIMIT),
    )(*args)


def _sum_blocks(a, n):
    M = a.shape[0] // n
    C = a.shape[1]

    def body(a_ref, o_ref):
        s = a_ref[0:M]
        for j in range(1, n):
            s = s + a_ref[j * M:(j + 1) * M]
        o_ref[...] = s

    return pl.pallas_call(
        body,
        name="sum_blocks",
        out_shape=jax.ShapeDtypeStruct((M, C), F32),
        compiler_params=pltpu.CompilerParams(vmem_limit_bytes=VMEM_LIMIT),
    )(a)


def _adamw(w, g, m, v, emit_grad=False):
    R, C = w.shape
    n_out = 4 if emit_grad else 3
    tr = _row_tile(R, C, 4 + n_out) if R % 8 == 0 else R

    def body(w_ref, g_ref, m_ref, v_ref, d_ref, nm_ref, nv_ref, *g_out):
        g = g_ref[...]
        m = ADAM_B1 * m_ref[...] + (1.0 - ADAM_B1) * g
        v = ADAM_B2 * v_ref[...] + (1.0 - ADAM_B2) * (g * g)
        m_hat = m / (1.0 - ADAM_B1 ** ADAM_STEP)
        v_hat = v / (1.0 - ADAM_B2 ** ADAM_STEP)
        d_ref[...] = -ADAM_LR * (m_hat / (jnp.sqrt(v_hat) + ADAM_EPS) + ADAM_WD * w_ref[...])
        nm_ref[...] = m
        nv_ref[...] = v
        if emit_grad:
            g_out[0][...] = g

    spec = pl.BlockSpec((tr, C), lambda i: (i, 0))
    return _call(body, "adamw", (R // tr,), [spec] * 4, [spec] * n_out, [jax.ShapeDtypeStruct((R, C), F32)] * n_out,
                 [], (w, g, m, v))[0]


def kernel(x, c, ada_w, ada_b, norm_ffn1_g, ffn1_w_gu, ffn1_w_down, norm_mix_g, mix_w_in, sgu_ln_g, sgu_ln_b, sgu_w_s, sgu_b, conv_w, out_norm_g, mix_w_out, norm_ffn2_g, ffn2_w_gu, ffn2_w_down, final_norm_g, loss_target, m_ada_w, m_ada_b, m_norm_ffn1_g, m_ffn1_w_gu, m_ffn1_w_down, m_norm_mix_g, m_mix_w_in, m_sgu_ln_g, m_sgu_ln_b, m_sgu_w_s, m_sgu_b, m_conv_w, m_out_norm_g, m_mix_w_out, m_norm_ffn2_g, m_ffn2_w_gu, m_ffn2_w_down, m_final_norm_g, v_ada_w, v_ada_b, v_norm_ffn1_g, v_ffn1_w_gu, v_ffn1_w_down, v_norm_mix_g, v_mix_w_in, v_sgu_ln_g, v_sgu_ln_b, v_sgu_w_s, v_sgu_b, v_conv_w, v_out_norm_g, v_mix_w_out, v_norm_ffn2_g, v_ffn2_w_gu, v_ffn2_w_down, v_final_norm_g):
    weights = dict(ada_w=ada_w, ada_b=ada_b, norm_ffn1_g=norm_ffn1_g, ffn1_w_gu=ffn1_w_gu, ffn1_w_down=ffn1_w_down,
                   norm_mix_g=norm_mix_g, mix_w_in=mix_w_in, sgu_ln_g=sgu_ln_g, sgu_ln_b=sgu_ln_b, sgu_w_s=sgu_w_s,
                   sgu_b=sgu_b, conv_w=conv_w, out_norm_g=out_norm_g, mix_w_out=mix_w_out, norm_ffn2_g=norm_ffn2_g,
                   ffn2_w_gu=ffn2_w_gu, ffn2_w_down=ffn2_w_down, final_norm_g=final_norm_g)
    m_in = dict(ada_w=m_ada_w, ada_b=m_ada_b, norm_ffn1_g=m_norm_ffn1_g, ffn1_w_gu=m_ffn1_w_gu,
                ffn1_w_down=m_ffn1_w_down, norm_mix_g=m_norm_mix_g, mix_w_in=m_mix_w_in, sgu_ln_g=m_sgu_ln_g,
                sgu_ln_b=m_sgu_ln_b, sgu_w_s=m_sgu_w_s, sgu_b=m_sgu_b, conv_w=m_conv_w, out_norm_g=m_out_norm_g,
                mix_w_out=m_mix_w_out, norm_ffn2_g=m_norm_ffn2_g, ffn2_w_gu=m_ffn2_w_gu, ffn2_w_down=m_ffn2_w_down,
                final_norm_g=m_final_norm_g)
    v_in = dict(ada_w=v_ada_w, ada_b=v_ada_b, norm_ffn1_g=v_norm_ffn1_g, ffn1_w_gu=v_ffn1_w_gu,
                ffn1_w_down=v_ffn1_w_down, norm_mix_g=v_norm_mix_g, mix_w_in=v_mix_w_in, sgu_ln_g=v_sgu_ln_g,
                sgu_ln_b=v_sgu_ln_b, sgu_w_s=v_sgu_w_s, sgu_b=v_sgu_b, conv_w=v_conv_w, out_norm_g=v_out_norm_g,
                mix_w_out=v_mix_w_out, norm_ffn2_g=v_norm_ffn2_g, ffn2_w_gu=v_ffn2_w_gu, ffn2_w_down=v_ffn2_w_down,
                final_norm_g=v_final_norm_g)

    B, S, D = x.shape
    T = B * S
    L = ada_w.shape[0]
    F = ffn1_w_down.shape[1] * N_CHIP
    P = mix_w_in.shape[2] * N_CHIP
    DA = D // 2
    DB = D - DA
    HD = DA // N_HEADS
    SA = ada_w.shape[2]
    n_all = B * N_DEV
    mx, my, mc = _position()
    chip = 2 * mx + my
    dev = 2 * chip + mc
    core = jnp.reshape(mc, (1,)).astype(jnp.int32)

    big = ["ffn1_w_gu", "ffn1_w_down", "mix_w_in", "mix_w_out", "ffn2_w_gu", "ffn2_w_down"]
    col_sharded = dict(ffn1_w_gu=True, ffn1_w_down=False, mix_w_in=True, mix_w_out=False,
                       ffn2_w_gu=True, ffn2_w_down=False)
    shards = {k: weights[k].astype(BF16) for k in big}
    gather = lambda l, *names: _gather_comm([(shards[k], l, col_sharded[k]) for k in names])
    full = [dict() for _ in range(L)]

    def arrived(l, names, res):
        full[l].update(zip(names, res))

    n_cw = L * conv_w.shape[1]
    cw_block = jnp.pad(conv_w.reshape(n_cw, conv_w.shape[2]), ((0, 8 - n_cw), (0, 0)))
    c_all, cw_all = _comm_call(_merge(_all_gather_comm(c.reshape(8, B * D // 8)), _all_gather_comm(cw_block)),
                               "gather_c")
    c_all = c_all.reshape(n_all, D)
    cw_all = cw_all.reshape(N_CHIP, 2, 8, conv_w.shape[2])[:, 0, :n_cw]
    conv_full = jnp.transpose(cw_all.reshape(N_CHIP, L, conv_w.shape[1], conv_w.shape[2]), (1, 2, 0, 3))
    conv_full = conv_full.reshape(L, conv_w.shape[1], DB)
    ada_b_mine = lax.dynamic_slice_in_dim(ada_b, chip * SA, SA, axis=1).reshape(L, 1, SA)
    c_act, ada_part = _ada_fwd(c_all, ada_w, ada_b_mine)
    ada_all, first_w = _comm_call(_merge(_all_gather_comm(ada_part.reshape(L * n_all, SA)), gather(0, big[0])),
                                  "gather_first")
    arrived(0, big[:1], [first_w])
    ada_all = ada_all.reshape(N_CHIP, 2, L, n_all, SA)[:, 0]
    ada_all = jnp.transpose(ada_all, (1, 2, 0, 3)).reshape(L, n_all, N_CHIP * SA)
    ada = lax.dynamic_slice_in_dim(ada_all, dev * B, B, axis=1).reshape(L, B, N_MOD, 1, D)
    mods = [[ada[l, :, j] for j in range(N_MOD)] for l in range(L)]

    x0 = x.reshape(T, D)
    gains = lambda name, l: weights[name][l].reshape(1, D)
    hmask = jnp.repeat(jnp.eye(N_HEADS, dtype=F32), HD, axis=0)
    pmat = (jnp.repeat(hmask, HD, axis=1) / HD).astype(BF16)

    def mix_consts(l):
        lng = jnp.tile(sgu_ln_g[l], N_HEADS).reshape(1, DA)
        lnb = jnp.tile(sgu_ln_b[l], N_HEADS).reshape(1, DA)
        wst = sgu_w_s[l].reshape(N_HEADS * CHUNK, CHUNK)
        wstt = jnp.swapaxes(sgu_w_s[l], 1, 2).reshape(N_HEADS * CHUNK, CHUNK)
        bias = jnp.repeat(jnp.transpose(sgu_b[l]), HD, axis=1)
        return lng, lnb, wst, wstt, bias

    def fetch(fn, *args, bring=()):
        bring = [(l, k) for l, k in bring if l < L]
        comm = _gather_comm([(shards[k], l, col_sharded[k]) for l, k in bring]) if bring else None
        res, got = fn(*args, comm)
        for (l, k), a in zip(bring, got):
            full[l][k] = a
        return res

    saved = []
    xc = x0
    for l in range(L):
        sh1, sc1, g1, sh2, sc2, g2, sh3, sc3, g3 = mods[l]
        lng, lnb, wst, wstt, bias = mix_consts(l)
        w = full[l]
        own = l == 0
        gu1, a1 = fetch(_ffn_up, xc, gains("norm_ffn1_g", l), sh1, sc1, w["ffn1_w_gu"],
                        bring=[(l, "ffn1_w_down"), (l, "mix_w_in"), (l, "mix_w_out")] if own else [(l, "ffn2_w_gu")])
        xa, f1 = fetch(_ffn_down, a1, xc, g1, w["ffn1_w_down"], bring=[(l, "ffn2_w_down")])
        proj, h2 = fetch(_mixin_fwd, xa, gains("norm_mix_g", l), sh2, sc2, w["mix_w_in"], bring=[(l + 1, "mix_w_in")])
        xb, yn, sv = fetch(_mix_core_fwd, proj, xa, g2, w["mix_w_out"], lng, lnb, wst, bias, pmat, conv_full[l],
                           gains("out_norm_g", l), bring=[(l, "ffn2_w_gu")] if own else [])
        gu2, a2 = fetch(_ffn_up, xb, gains("norm_ffn2_g", l), sh3, sc3, w["ffn2_w_gu"],
                        bring=[(l + 1, "ffn1_w_gu"), (l + 1, "mix_w_out")])
        xd, f2 = fetch(_ffn_down, a2, xb, g3, w["ffn2_w_down"], bring=[(l + 1, "ffn1_w_down")])
        saved.append(dict(x0=xc, xa=xa, xb=xb, gu1=gu1, a1=a1, f1=f1, proj=proj, h2=h2, yn=yn, sv=sv,
                          gu2=gu2, a2=a2, f2=f2))
        xc = xd

    dx, loss_block, d_final = _loss_head(xc, loss_target.reshape(T, D), final_norm_g.reshape(1, D))

    reduced = dict.fromkeys(big)

    def halves(name, g):
        if g.ndim == 4:
            return g
        return g.reshape(N_CHIP, 2, weights[name].shape[1] // 2, g.shape[-1])

    class Reduction:
        def __init__(self, l, name, g):
            self.l, self.name, self.g, self.stage = l, name, halves(name, g), 0
            self.ici_bytes = 3 * (g.size // 8) * 2

        def step(self):
            self.stage += 1
            if self.stage == 1:
                return _sibling_half_comm([self.g])
            if self.stage == 2:
                return _scatter_comm([_pair_sum(self.g, self.got[0], core)])
            if self.stage == 3:
                reduced[self.name] = _chip_sum(self.got[0], core, self.l, L, reduced[self.name])
                return _share_comm([reduced[self.name]], self.l)
            reduced[self.name] = self.got[0]
            return None

    active, extra, gathered = [], [], {}

    def carry(fn, *args, us=None):
        left = None if us is None else us * SCATTER_BYTES_PER_US
        riders = []
        for r in active:
            if r.stage == 1 and left is not None:
                if r.ici_bytes > left * SCATTER_OVERSHOOT:
                    continue
                left -= r.ici_bytes
            riders.append(r)
        comms = [r.step() for r in riders] + [cm for cm, _ in extra]
        takers = [functools.partial(setattr, r, "got") for r in riders] + [cb for _, cb in extra]
        extra.clear()
        if fn is None:
            res, got = None, (_comm_call(_merge(*comms), "reduce_alone") if comms else [])
        else:
            res, got = fn(*args, comm=_merge(*comms))
        at = 0
        for cm, take in zip(comms, takers):
            take(got[at:at + len(cm.out_shape)])
            at += len(cm.out_shape)
        for r in riders:
            if r.stage == 3:
                r.step()
                active.remove(r)
        return res

    def reduce_later(l, name, g):
        active.append(Reduction(l, name, g))

    small = [None] * L
    dwsts = [None] * L
    d_ada = [None] * L
    for l in reversed(range(L)):
        sh1, sc1, g1, sh2, sc2, g2, sh3, sc3, g3 = mods[l]
        lng, lnb, wst, wstt, bias = mix_consts(l)
        s = saved[l]
        w = full[l]
        last = l == 0
        dgu2, = carry(_ffn_bwd_down, dx, s["gu2"], g3, w["ffn2_w_down"], us=100)
        dx, h3, df2, dsc3, dsh3, dgain3, dg3 = carry(_ffn_bwd_up, dx, s["xb"], dgu2, s["f2"], gains("norm_ffn2_g", l),
                                                     sh3, sc3, g3, w["ffn2_w_gu"], us=120)
        ffn2_grads = [
            lambda: reduce_later(l, "ffn2_w_gu", carry(_wgrad, h3, dgu2, D, 2 * F // N_CHIP, True, "wgrad_gu",
                                                       WGRAD_TOKENS // 2, us=110)[0]),
            lambda: reduce_later(l, "ffn2_w_down", carry(_wgrad, s["a2"], df2[None], F // 2, D, False, "wgrad_down",
                                                         us=50)[0])]
        if not last:
            ffn2_grads[0]()
            ffn2_grads[1]()
        dproj, d_o, dg2, dog, dwst, dbias, dlng, dlnb, dconvw = carry(
            _mix_core_bwd, s["proj"], s["sv"], dx, g2, w["mix_w_out"], lng, lnb, wstt, pmat, conv_full[l],
            gains("out_norm_g", l), us=150)
        mix_grads = [
            lambda: reduce_later(l, "mix_w_out", carry(_wgrad, s["yn"], d_o[None], D, D, False, "wgrad_out", us=20)[0]),
            lambda: reduce_later(l, "mix_w_in", carry(_wgrad, s["h2"], dproj[None], D, P // N_CHIP, True, "wgrad_in",
                                                      us=65)[0])]
        if not last:
            mix_grads[0]()
        dx, dsc2, dsh2, dgain2 = carry(_mixin_bwd, dx, s["xa"], dproj, gains("norm_mix_g", l), sc2, w["mix_w_in"], us=60)
        if not last:
            mix_grads[1]()
        dgu, = carry(_ffn_bwd_down, dx, s["gu1"], g1, w["ffn1_w_down"], us=100)
        dx, h1, df, dsc1, dsh1, dgain1, dg1 = carry(_ffn_bwd_up, dx, s["x0"], dgu, s["f1"], gains("norm_ffn1_g", l),
                                                    sh1, sc1, g1, w["ffn1_w_gu"], us=120)
        d_ada[l] = jnp.concatenate([dsh1, dsc1, dg1, dsh2, dsc2, dg2, dsh3, dsc3, dg3], axis=1).reshape(B, N_MOD * D)
        small[l] = [dgain1, dgain2, dgain3, dog, dlng, dlnb, dbias[:, ::HD], dconvw]
        dwsts[l] = dwst
        if last:
            flat = [a.reshape(-1, 128) for ll in range(L) for a in small[ll]]
            flat += [d_final.reshape(-1, 128), loss_block[0:1]]
            pad = (-sum(a.shape[0] for a in flat)) % 8
            packed = jnp.concatenate(flat + [jnp.zeros((pad, 128), F32)], axis=0)
            extra.append((_all_gather_comm(jnp.stack(d_ada).reshape(L * B, N_MOD * D)),
                          lambda got: gathered.update(d_ada=got[0])))
            extra.append((_all_gather_comm(packed), lambda got: gathered.update(small=got[0])))
            for ll in range(L):
                extra.append((_all_gather_comm(dwsts[ll]), lambda got, ll=ll: gathered.update({("dwst", ll): got[0]})))
        reduce_later(l, "ffn1_w_gu", carry(_wgrad, h1, dgu, D, 2 * F // N_CHIP, True, "wgrad_gu", WGRAD_TOKENS // 2,
                                           us=110)[0])
        reduce_later(l, "ffn1_w_down", carry(_wgrad, s["a1"], df[None], F // 2, D, False, "wgrad_down", us=50)[0])
        if last:
            ffn2_grads[0]()
            ffn2_grads[1]()
            mix_grads[1]()
            mix_grads[0]()
    grad_x = dx.reshape(B, S, D)

    def finished(name):
        while any(r.name == name for r in active):
            carry(None)
        return reduced[name].reshape(weights[name].shape)

    grads = {}
    d_ada_all = jnp.transpose(gathered["d_ada"].reshape(N_DEV, L, B, N_MOD * D), (1, 0, 2, 3))
    d_ada_all = d_ada_all.reshape(L, n_all, N_MOD * D)
    grads["ada_b"] = _colsum(d_ada_all).reshape(L, N_MOD * D)
    d_ada_mine = lax.dynamic_slice_in_dim(d_ada_all, chip * SA, SA, axis=2).astype(BF16)
    grads["ada_w"] = _ada_bwd(c_act, d_ada_mine)[0][0]

    total = _sum_blocks(gathered["small"].reshape(-1, 128), N_DEV)
    pieces, at = [], 0
    for a in flat:
        pieces.append(total[at:at + a.shape[0]])
        at += a.shape[0]
    per_layer = len(small[0])
    stack = lambda j, shape: jnp.stack([pieces[l * per_layer + j].reshape(shape) for l in range(L)])
    grads["norm_ffn1_g"] = stack(0, (D,))
    grads["norm_mix_g"] = stack(1, (D,))
    grads["norm_ffn2_g"] = stack(2, (D,))
    grads["out_norm_g"] = stack(3, (D,))
    grads["sgu_ln_g"] = stack(4, (N_HEADS, HD)).sum(axis=1)
    grads["sgu_ln_b"] = stack(5, (N_HEADS, HD)).sum(axis=1)
    grads["sgu_b"] = jnp.swapaxes(stack(6, (CHUNK, N_HEADS)), 1, 2)
    g_conv = stack(7, (8, DB))[:, :conv_w.shape[1]]
    grads["conv_w"] = lax.dynamic_slice_in_dim(g_conv, chip * conv_w.shape[2], conv_w.shape[2], axis=2)
    grads["final_norm_g"] = pieces[-2].reshape(D)
    loss = pieces[-1][0, 0]
    grads["sgu_w_s"] = jnp.stack([_sum_blocks(gathered["dwst", l].reshape(-1, CHUNK), N_DEV) for l in range(L)])
    grads["sgu_w_s"] = grads["sgu_w_s"].reshape(L, N_HEADS, CHUNK, CHUNK)

    names = list(weights)
    delta, new_m, new_v = {}, {}, {}
    for k in big:
        grads[k] = finished(k)
    for k in names:
        wk = weights[k]
        view = (1, wk.shape[0]) if wk.ndim == 1 else (-1, wk.shape[-1])
        d, nm, nv, *g_again = _adamw(wk.reshape(view), grads[k].reshape(view), m_in[k].reshape(view),
                                     v_in[k].reshape(view), emit_grad=k in big)
        delta[k], new_m[k], new_v[k] = d.reshape(wk.shape), nm.reshape(wk.shape), nv.reshape(wk.shape)
        if g_again:
            grads[k] = g_again[0].reshape(wk.shape)

    return (loss, grad_x, *[grads[k] for k in names], *[delta[k] for k in names],
            *[new_m[k] for k in names], *[new_v[k] for k in names])
```

```python
import functools
import math

import jax
import jax.numpy as jnp
from jax import lax
from jax.experimental import pallas as pl
from jax.experimental.pallas import tpu as pltpu

F32 = jnp.float32
BF16 = jnp.bfloat16
MESH = pl.DeviceIdType.MESH

N_HEADS = 8
CHUNK = 128
N_MOD = 9
EPS = 1e-6
N_DEV = 8
N_CHIP = 4

ADAM_LR = 0.001
ADAM_B1 = 0.9
ADAM_B2 = 0.999
ADAM_EPS = 1e-08
ADAM_WD = 0.01
ADAM_STEP = 10

TOKEN_TILE = 512
FF_SLAB = 768
MIX_TILE = 256
WGRAD_TOKENS = 2048
VMEM_LIMIT = 56 * 1024 * 1024

SCATTER_BYTES_PER_US = 68_000
SCATTER_OVERSHOOT = 1.25

ANY = pl.BlockSpec(memory_space=pl.ANY)


def _tile(pref, n):
    t = min(pref, n)
    assert n % t == 0, (pref, n)
    return t


def _slabs(n, width):
    return [slice(c0, min(c0 + width, n)) for c0 in range(0, n, width)]


def _dot(a, b):
    return jnp.dot(a, b, preferred_element_type=F32)


def _dot_nt(a, b):
    return lax.dot_general(a, b, (((1,), (1,)), ((), ())), preferred_element_type=F32)


def _dot_tn(a, b):
    return lax.dot_general(a, b, (((0,), (0,)), ((), ())), preferred_element_type=F32)


def _sigmoid(x):
    return 1.0 / (1.0 + jnp.exp(-x))


def _sigmoid_fast(x):
    return pl.reciprocal(1.0 + jnp.exp(-x), approx=True)


def _rms(x):
    r = lax.rsqrt(jnp.mean(x * x, axis=-1, keepdims=True) + EPS)
    return x * r, r


def _norm_mod_bwd(x, dh, gain, sc):
    xh, r = _rms(x)
    dsc = jnp.sum(dh * (xh * gain), axis=0, keepdims=True)
    dsh = jnp.sum(dh, axis=0, keepdims=True)
    dn = dh * (1.0 + sc)
    dgain = jnp.sum(dn * xh, axis=0, keepdims=True)
    dy = dn * gain
    dx = r * (dy - xh * jnp.mean(dy * xh, axis=-1, keepdims=True))
    return dx, dsc, dsh, dgain


def _acc(ref, first, val):
    @pl.when(first)
    def _():
        ref[...] = val

    @pl.when(jnp.logical_not(first))
    def _():
        ref[...] += val


class _Comm:
    def __init__(self, args, out_shape, scratch, phases, aliases=None):
        self.args, self.out_shape, self.scratch = list(args), list(out_shape), list(scratch)
        self.phases, self.aliases = phases, dict(aliases or {})


def _merge(*comms):
    comms = [c for c in comms if c is not None]
    if len(comms) <= 1:
        return comms[0] if comms else None
    args = [a for c in comms for a in c.args]
    out_shape = [o for c in comms for o in c.out_shape]
    scratch = [s for c in comms for s in c.scratch]
    aliases, ai, oi = {}, 0, 0
    for c in comms:
        aliases.update({ai + i: oi + o for i, o in c.aliases.items()})
        ai += len(c.args)
        oi += len(c.out_shape)

    def phases(ins, outs, sems):
        parts, ai, oi, si = [], 0, 0, 0
        for c in comms:
            parts.append(c.phases(ins[ai:ai + len(c.args)], outs[oi:oi + len(c.out_shape)], sems[si:si + len(c.scratch)]))
            ai, oi, si = ai + len(c.args), oi + len(c.out_shape), si + len(c.scratch)

        def run(k):
            def go():
                for p in parts:
                    if p[k] is not None:
                        p[k]()
            return go
        return run(0), run(1), run(2)

    return _Comm(args, out_shape, scratch, phases, aliases)


def _call(body, name, grid, in_specs, out_specs, out_shape, scratch, args, comm=None):
    n_in, n_out, n_scr = len(in_specs), len(out_specs), len(scratch)
    sem = ("arbitrary",) * len(grid)
    params = pltpu.CompilerParams(dimension_semantics=sem, vmem_limit_bytes=VMEM_LIMIT)
    if comm is None:
        res = pl.pallas_call(body, name=name, grid=grid, in_specs=in_specs, out_specs=out_specs, out_shape=out_shape,
                             scratch_shapes=scratch, compiler_params=params)(*args)
        return list(res), []
    m_in, m_out = len(comm.args), len(comm.out_shape)

    def full(*refs):
        c_in, c_min = refs[:n_in], refs[n_in:n_in + m_in]
        o = n_in + m_in
        c_out, c_mout = refs[o:o + n_out], refs[o + n_out:o + n_out + m_out]
        o += n_out + m_out
        c_scr, c_sem = refs[o:o + n_scr], refs[o + n_scr:]
        start, mid, finish = comm.phases(c_min, c_mout, c_sem)
        ids = [pl.program_id(a) for a in range(len(grid))]
        first = functools.reduce(jnp.logical_and, [i == 0 for i in ids])
        last = functools.reduce(jnp.logical_and, [i == g - 1 for i, g in zip(ids, grid)])
        pl.when(first)(start)
        if mid is not None:
            pl.when(last)(mid)
        body(*c_in, *c_out, *c_scr)
        pl.when(last)(finish)

    res = pl.pallas_call(
        full, name=name, grid=grid,
        in_specs=list(in_specs) + [ANY] * m_in,
        out_specs=list(out_specs) + [ANY] * m_out,
        out_shape=list(out_shape) + comm.out_shape,
        scratch_shapes=list(scratch) + comm.scratch,
        input_output_aliases={n_in + i: n_out + o for i, o in comm.aliases.items()},
        compiler_params=params,
    )(*args, *comm.args)
    return list(res[:n_out]), list(res[n_out:])


def _comm_call(comm, name):
    m_in, m_out = len(comm.args), len(comm.out_shape)

    def body(*refs):
        start, mid, finish = comm.phases(refs[:m_in], refs[m_in:m_in + m_out], refs[m_in + m_out:])
        start()
        if mid is not None:
            mid()
        finish()

    res = pl.pallas_call(
        body, name=name, in_specs=[ANY] * m_in, out_specs=[ANY] * m_out, out_shape=comm.out_shape,
        scratch_shapes=comm.scratch, input_output_aliases=comm.aliases,
    )(*comm.args)
    return list(res)


def _position():
    return lax.axis_index("x"), lax.axis_index("y"), lax.axis_index("c")


def _gather_comm(items):
    n = len(items)
    half = [s.shape[1] // 2 for s, _, _ in items]

    def full_shape(i):
        s, _, col = items[i]
        _, R, C = s.shape
        return jax.ShapeDtypeStruct((R, N_CHIP * C) if col else (N_CHIP * R, C), s.dtype)

    def phases(ins, outs, sems):
        send_sems, recv_sems, local_sems = sems
        x, y, c = _position()

        def region(i, chip, h):
            s, _, col = items[i]
            _, R, C = s.shape
            if col:
                return outs[i].at[pl.ds(h * half[i], half[i]), pl.ds(chip * C, C)]
            return outs[i].at[pl.ds(chip * R + h * half[i], half[i]), :]

        def mine(i, h):
            return ins[i].at[items[i][1], pl.ds(h * half[i], half[i]), :]

        def copies(kx, ky, kc):
            k_me = 2 * kx + ky
            sibling = (kx, ky, 1 - kc)
            chips = [(1 - kx, ky), (kx, 1 - ky), (1 - kx, 1 - ky)]
            local, first, passed, arrive_ici, arrive_d2d = [], [], [], [], []

            def remote(src, dst, s, to):
                return pltpu.make_async_remote_copy(src_ref=src, dst_ref=dst, send_sem=send_sems.at[s],
                                                    recv_sem=recv_sems.at[s], device_id=to, device_id_type=MESH)

            for i in range(n):
                for h in range(2):
                    local.append(pltpu.make_async_copy(mine(i, h), region(i, k_me, h), local_sems.at[2 * i + h]))
                for j, (px, py) in enumerate(chips):
                    s = 6 * i + j
                    first.append(remote(mine(i, kc), region(i, k_me, kc), s, (px, py, kc)))
                    got = region(i, 2 * px + py, kc)
                    arrive_ici.append(remote(got, got, s, (px, py, kc)))
                    passed.append(remote(got, got, s + 3, sibling))
                    other = region(i, 2 * px + py, 1 - kc)
                    arrive_d2d.append(remote(other, other, s + 3, sibling))
            return local, first, passed, arrive_ici, arrive_d2d

        def on_each_device(fn):
            def go():
                for kx in range(2):
                    for ky in range(2):
                        for kc in range(2):
                            pl.when((x == kx) & (y == ky) & (c == kc))(functools.partial(fn, *copies(kx, ky, kc)))
            return go

        def start(local, first, passed, arrive_ici, arrive_d2d):
            for cp in local + first:
                cp.start()

        def mid(local, first, passed, arrive_ici, arrive_d2d):
            for a, p in zip(arrive_ici, passed):
                a.wait_recv()
                p.start()

        def finish(local, first, passed, arrive_ici, arrive_d2d):
            for a in arrive_d2d:
                a.wait_recv()
            for cp in first + passed:
                cp.wait_send()
            for cp in local:
                cp.wait()

        return on_each_device(start), on_each_device(mid), on_each_device(finish)

    scratch = [pltpu.SemaphoreType.DMA((6 * n,)), pltpu.SemaphoreType.DMA((6 * n,)), pltpu.SemaphoreType.DMA((2 * n,))]
    return _Comm([s for s, _, _ in items], [full_shape(i) for i in range(n)], scratch, phases)


def _sibling_half_comm(gs):
    n = len(gs)

    def phases(ins, outs, sems):
        send_sems, recv_sems = sems
        x, y, c = _position()

        def copies():
            return [pltpu.make_async_remote_copy(
                src_ref=ins[i].at[:, 1 - c], dst_ref=outs[i], send_sem=send_sems.at[i], recv_sem=recv_sems.at[i],
                device_id=(x, y, 1 - c), device_id_type=MESH) for i in range(n)]

        def start():
            for cp in copies():
                cp.start()

        def finish():
            for cp in copies():
                cp.wait()

        return start, None, finish

    out_shape = [jax.ShapeDtypeStruct(g.shape[:1] + g.shape[2:], g.dtype) for g in gs]
    return _Comm(gs, out_shape, [pltpu.SemaphoreType.DMA((n,)), pltpu.SemaphoreType.DMA((n,))], phases)


def _scatter_comm(ps):
    n = len(ps)

    def phases(ins, outs, sems):
        send_sems, recv_sems, local_sems = sems
        x, y, c = _position()
        k_me = 2 * x + y
        chips = [(1 - x, y), (x, 1 - y), (1 - x, 1 - y)]

        def copies():
            local = [pltpu.make_async_copy(ins[i].at[k_me], outs[i].at[k_me], local_sems.at[i]) for i in range(n)]
            remote = [pltpu.make_async_remote_copy(
                src_ref=ins[i].at[2 * px + py], dst_ref=outs[i].at[k_me],
                send_sem=send_sems.at[3 * i + j], recv_sem=recv_sems.at[3 * i + j],
                device_id=(px, py, c), device_id_type=MESH) for i in range(n) for j, (px, py) in enumerate(chips)]
            return local, remote

        def start():
            local, remote = copies()
            for cp in local + remote:
                cp.start()

        def finish():
            local, remote = copies()
            for cp in remote + local:
                cp.wait()

        return start, None, finish

    scratch = [pltpu.SemaphoreType.DMA((3 * n,)), pltpu.SemaphoreType.DMA((3 * n,)), pltpu.SemaphoreType.DMA((n,))]
    return _Comm(ps, [jax.ShapeDtypeStruct(p.shape, p.dtype) for p in ps], scratch, phases)


def _share_comm(rs, l):
    n = len(rs)

    def phases(ins, outs, sems):
        send_sems, recv_sems = sems
        x, y, c = _position()

        def copy(i, h):
            return pltpu.make_async_remote_copy(
                src_ref=outs[i].at[l, h], dst_ref=outs[i].at[l, h], send_sem=send_sems.at[i], recv_sem=recv_sems.at[i],
                device_id=(x, y, 1 - c), device_id_type=MESH)

        def start():
            for i in range(n):
                copy(i, c).start()

        def finish():
            for i in range(n):
                copy(i, 1 - c).wait_recv()
            for i in range(n):
                copy(i, c).wait_send()

        return start, None, finish

    return _Comm(rs, [jax.ShapeDtypeStruct(r.shape, r.dtype) for r in rs],
                 [pltpu.SemaphoreType.DMA((n,)), pltpu.SemaphoreType.DMA((n,))], phases,
                 aliases={i: i for i in range(n)})


def _all_gather_comm(block):
    def phases(ins, outs, sems):
        send_sems, recv_sems, local_sem = sems
        (src,), (out,) = ins, outs
        x, y, c = _position()
        sibling = (x, y, 1 - c)
        chips = [(1 - x, y), (x, 1 - y), (1 - x, 1 - y)]

        def slot(px, py, pc):
            return out.at[4 * px + 2 * py + pc]

        def copy(k, blk, to, own=False):
            return pltpu.make_async_remote_copy(
                src_ref=src if own else slot(*blk), dst_ref=slot(*blk),
                send_sem=send_sems.at[k], recv_sem=recv_sems.at[k], device_id=to, device_id_type=MESH)

        mine = lambda: pltpu.make_async_copy(src, slot(x, y, c), local_sem.at[0])
        first = lambda: [copy(0, (x, y, c), sibling, True)] + [
            copy(1 + j, (x, y, c), (*chip, c), True) for j, chip in enumerate(chips)]
        passed = lambda: [copy(4 + j, (*chip, c), sibling) for j, chip in enumerate(chips)]

        def start():
            mine().start()
            for cp in first():
                cp.start()

        def mid():
            for j, (chip, p) in enumerate(zip(chips, passed())):
                copy(1 + j, (*chip, c), (x, y, c)).wait_recv()
                p.start()

        def finish():
            copy(0, sibling, (x, y, c)).wait_recv()
            for j, chip in enumerate(chips):
                copy(4 + j, (*chip, 1 - c), (x, y, c)).wait_recv()
            for cp in first() + passed():
                cp.wait_send()
            mine().wait()

        return start, mid, finish

    scratch = [pltpu.SemaphoreType.DMA((7,)), pltpu.SemaphoreType.DMA((7,)), pltpu.SemaphoreType.DMA((1,))]
    return _Comm([block], [jax.ShapeDtypeStruct((N_DEV,) + block.shape, block.dtype)], scratch, phases)


def _ffn_up(x, gain, sh, sc, wgu, comm=None):
    T, D = x.shape
    F = wgu.shape[1] // 2
    B = sh.shape[0]
    tm = _tile(TOKEN_TILE, T // B)
    tps = (T // B) // tm
    slabs = _slabs(F, FF_SLAB)

    def body(x_ref, gain_ref, sh_ref, sc_ref, w_ref, gu_ref, a_ref):
        xh, _ = _rms(x_ref[...])
        h = (xh * gain_ref[...] * (1.0 + sc_ref[0]) + sh_ref[0]).astype(BF16)

        def dots(s):
            return _dot(h, w_ref[:, s]), _dot(h, w_ref[:, slice(F + s.start, F + s.stop)])

        nxt = dots(slabs[0])
        for j, s in enumerate(slabs):
            g, u = nxt
            if j + 1 < len(slabs):
                nxt = dots(slabs[j + 1])
            gu_ref[0, :, s] = g.astype(BF16)
            gu_ref[1, :, s] = u.astype(BF16)
            a_ref[:, s] = (g * _sigmoid(g) * u).astype(BF16)

    seq = lambda i: (i // tps, 0, 0)
    return _call(
        body, "ffn_up", (T // tm,),
        [
            pl.BlockSpec((tm, D), lambda i: (i, 0)),
            pl.BlockSpec((1, D), lambda i: (0, 0)),
            pl.BlockSpec((1, 1, D), seq),
            pl.BlockSpec((1, 1, D), seq),
            pl.BlockSpec((D, 2 * F), lambda i: (0, 0), pipeline_mode=pl.Buffered(1)),
        ],
        [
            pl.BlockSpec((2, tm, F), lambda i: (0, i, 0)),
            pl.BlockSpec((tm, F), lambda i: (i, 0)),
        ],
        [
            jax.ShapeDtypeStruct((2, T, F), BF16),
            jax.ShapeDtypeStruct((T, F), BF16),
        ],
        [],
        (x, gain, sh, sc, wgu), comm)


def _ffn_down(a, x, gate, wd, comm=None):
    T, F = a.shape
    D = x.shape[1]
    B = gate.shape[0]
    tm = _tile(2 * TOKEN_TILE, T // B)
    tps = (T // B) // tm

    def body(a_ref, x_ref, gate_ref, wd_ref, xo_ref, f_ref):
        f = _dot(a_ref[...], wd_ref[...])
        f_ref[...] = f.astype(BF16)
        xo_ref[...] = x_ref[...] + 0.5 * gate_ref[0] * f

    return _call(
        body, "ffn_down", (T // tm,),
        [
            pl.BlockSpec((tm, F), lambda i: (i, 0)),
            pl.BlockSpec((tm, D), lambda i: (i, 0)),
            pl.BlockSpec((1, 1, D), lambda i: (i // tps, 0, 0)),
            pl.BlockSpec((F, D), lambda i: (0, 0), pipeline_mode=pl.Buffered(1)),
        ],
        [pl.BlockSpec((tm, D), lambda i: (i, 0)), pl.BlockSpec((tm, D), lambda i: (i, 0))],
        [jax.ShapeDtypeStruct((T, D), F32), jax.ShapeDtypeStruct((T, D), BF16)],
        [],
        (a, x, gate, wd), comm)


def _ffn_bwd_down(dxo, gu, gate, wd, comm=None):
    T, D = dxo.shape
    F = wd.shape[0]
    B = gate.shape[0]
    tm = _tile(TOKEN_TILE, T // B)
    tps = (T // B) // tm
    slabs = _slabs(F, FF_SLAB)

    def body(dxo_ref, gu_ref, gate_ref, wd_ref, dgu_ref):
        df = (0.5 * gate_ref[0] * dxo_ref[...]).astype(BF16)
        nxt = _dot_nt(df, wd_ref[slabs[0], :])
        for j, s in enumerate(slabs):
            da = nxt
            if j + 1 < len(slabs):
                nxt = _dot_nt(df, wd_ref[slabs[j + 1], :])
            g = gu_ref[0, :, s]
            sg = 1.0 / (1.0 + jnp.exp(-g))
            t = g * sg
            dab = da.astype(BF16)
            dgu_ref[1, :, s] = dab * t
            dgu_ref[0, :, s] = dab * gu_ref[1, :, s] * (sg + t - t * sg)

    return _call(
        body, "ffn_bwd_down", (T // tm,),
        [
            pl.BlockSpec((tm, D), lambda i: (i, 0)),
            pl.BlockSpec((2, tm, F), lambda i: (0, i, 0)),
            pl.BlockSpec((1, 1, D), lambda i: (i // tps, 0, 0)),
            pl.BlockSpec((F, D), lambda i: (0, 0), pipeline_mode=pl.Buffered(1)),
        ],
        [pl.BlockSpec((2, tm, F), lambda i: (0, i, 0))],
        [jax.ShapeDtypeStruct((2, T, F), BF16)],
        [],
        (dxo, gu, gate, wd), comm)


def _ffn_bwd_up(dxo, x, dgu, f, gain, sh, sc, gate, wgu, comm=None):
    T, D = x.shape
    F = wgu.shape[1] // 2
    B = sc.shape[0]
    tm = _tile(TOKEN_TILE, T // B)
    tps = (T // B) // tm

    def body(dxo_ref, x_ref, dgu_ref, f_ref, gain_ref, sh_ref, sc_ref, gate_ref, w_ref,
             dx_ref, h_ref, df_ref, dsc_ref, dsh_ref, dgain_ref, dgate_ref):
        i = pl.program_id(0)
        first_of_seq = (i % tps) == 0
        gain = gain_ref[...]
        sc = sc_ref[0]
        halves = _slabs(tm, tm // 2)

        def dots(r):
            return _dot_nt(dgu_ref[0, r, :], w_ref[:, 0:F]) + _dot_nt(dgu_ref[1, r, :], w_ref[:, F:])

        nxt = dots(halves[0])
        sums = None
        for j, r in enumerate(halves):
            dh = nxt
            if j + 1 < len(halves):
                nxt = dots(halves[j + 1])
            dxo = dxo_ref[r, :]
            x = x_ref[r, :]
            dx, dsc, dsh, dgain = _norm_mod_bwd(x, dh, gain, sc)
            dx_ref[r, :] = dxo + dx
            h_ref[r, :] = (_rms(x)[0] * gain * (1.0 + sc) + sh_ref[0]).astype(BF16)
            df_ref[r, :] = (0.5 * gate_ref[0] * dxo).astype(BF16)
            part = (dsc, dsh, dgain, 0.5 * jnp.sum(dxo * f_ref[r, :].astype(F32), axis=0, keepdims=True))
            sums = part if sums is None else tuple(a + b for a, b in zip(sums, part))
        _acc(dsc_ref.at[0], first_of_seq, sums[0])
        _acc(dsh_ref.at[0], first_of_seq, sums[1])
        _acc(dgain_ref, i == 0, sums[2])
        _acc(dgate_ref.at[0], first_of_seq, sums[3])

    seq = lambda i: (i // tps, 0, 0)
    row = lambda i: (i, 0)
    return _call(
        body, "ffn_bwd_up", (T // tm,),
        [
            pl.BlockSpec((tm, D), row),
            pl.BlockSpec((tm, D), row),
            pl.BlockSpec((2, tm, F), lambda i: (0, i, 0)),
            pl.BlockSpec((tm, D), row),
            pl.BlockSpec((1, D), lambda i: (0, 0)),
            pl.BlockSpec((1, 1, D), seq),
            pl.BlockSpec((1, 1, D), seq),
            pl.BlockSpec((1, 1, D), seq),
            pl.BlockSpec((D, 2 * F), lambda i: (0, 0), pipeline_mode=pl.Buffered(1)),
        ],
        [
            pl.BlockSpec((tm, D), row),
            pl.BlockSpec((tm, D), row),
            pl.BlockSpec((tm, D), row),
            pl.BlockSpec((1, 1, D), seq),
            pl.BlockSpec((1, 1, D), seq),
            pl.BlockSpec((1, D), lambda i: (0, 0)),
            pl.BlockSpec((1, 1, D), seq),
        ],
        [
            jax.ShapeDtypeStruct((T, D), F32),
            jax.ShapeDtypeStruct((T, D), BF16),
            jax.ShapeDtypeStruct((T, D), BF16),
            jax.ShapeDtypeStruct((B, 1, D), F32),
            jax.ShapeDtypeStruct((B, 1, D), F32),
            jax.ShapeDtypeStruct((1, D), F32),
            jax.ShapeDtypeStruct((B, 1, D), F32),
        ],
        [],
        (dxo, x, dgu, f, gain, sh, sc, gate, wgu), comm)


def _wgrad(a, b, tmm, tn, col_major, name, tokens=WGRAD_TOKENS, comm=None):
    T, M = a.shape
    nb, _, Nb = b.shape
    N = nb * Nb
    tk = _tile(tokens, T)
    span = 2 if col_major else 1
    wide = span * tn
    npb = Nb // wide
    assert M % tmm == 0 and Nb % wide == 0
    if col_major:
        assert tmm == M
        shape = (N // tn, 2, M // 2, tn)
        out_spec = pl.BlockSpec((span, 2, M // 2, tn), lambda i, j, t: (j, 0, 0, 0))
    else:
        shape = (M // tmm, tmm, N)
        out_spec = pl.BlockSpec((None, tmm, tn), lambda i, j, t: (i, 0, j))

    def body(a_ref, b_ref, o_ref):
        @pl.when(pl.program_id(2) == 0)
        def _():
            o_ref[...] = jnp.zeros_like(o_ref)

        res = _dot_tn(a_ref[...], b_ref[...])
        if col_major:
            for s in range(span):
                for h in range(2):
                    o_ref[s, h] += res[h * (M // 2):(h + 1) * (M // 2), s * tn:(s + 1) * tn]
        else:
            o_ref[...] += res

    return _call(
        body, name, (M // tmm, N // wide, T // tk),
        [
            pl.BlockSpec((tk, tmm), lambda i, j, t: (t, i)),
            pl.BlockSpec((None, tk, wide), lambda i, j, t: (j // npb, t, j % npb)),
        ],
        [out_spec], [jax.ShapeDtypeStruct(shape, F32)], [],
        (a, b), comm)


def _mixin_fwd(x, gain, sh, sc, win, comm=None):
    T, D = x.shape
    P = win.shape[1]
    B = sh.shape[0]
    tm = _tile(TOKEN_TILE, T // B)
    tps = (T // B) // tm

    def body(x_ref, gain_ref, sh_ref, sc_ref, w_ref, proj_ref, h_ref):
        xh, _ = _rms(x_ref[...])
        h = (xh * gain_ref[...] * (1.0 + sc_ref[0]) + sh_ref[0]).astype(BF16)
        h_ref[...] = h
        proj_ref[...] = _dot(h, w_ref[...])

    seq = lambda i: (i // tps, 0, 0)
    return _call(
        body, "mixin_fwd", (T // tm,),
        [
            pl.BlockSpec((tm, D), lambda i: (i, 0)),
            pl.BlockSpec((1, D), lambda i: (0, 0)),
            pl.BlockSpec((1, 1, D), seq),
            pl.BlockSpec((1, 1, D), seq),
            pl.BlockSpec((D, P), lambda i: (0, 0)),
        ],
        [pl.BlockSpec((tm, P), lambda i: (i, 0)), pl.BlockSpec((tm, D), lambda i: (i, 0))],
        [jax.ShapeDtypeStruct((T, P), F32), jax.ShapeDtypeStruct((T, D), BF16)],
        [],
        (x, gain, sh, sc, win), comm)


def _mixin_bwd(dxo, x, dproj, gain, sc, win, comm=None):
    T, D = x.shape
    P = win.shape[1]
    B = sc.shape[0]
    tm = _tile(TOKEN_TILE, T // B)
    tps = (T // B) // tm

    def body(dxo_ref, x_ref, dp_ref, gain_ref, sc_ref, w_ref, dx_ref, dsc_ref, dsh_ref, dgain_ref):
        i = pl.program_id(0)
        first_of_seq = (i % tps) == 0
        halves = _slabs(tm, tm // 2)
        nxt = _dot_nt(dp_ref[halves[0], :], w_ref[...])
        sums = None
        for j, r in enumerate(halves):
            dh = nxt
            if j + 1 < len(halves):
                nxt = _dot_nt(dp_ref[halves[j + 1], :], w_ref[...])
            part = _norm_mod_bwd(x_ref[r, :], dh, gain_ref[...], sc_ref[0])
            dx_ref[r, :] = dxo_ref[r, :] + part[0]
            sums = part[1:] if sums is None else tuple(a + b for a, b in zip(sums, part[1:]))
        _acc(dsc_ref.at[0], first_of_seq, sums[0])
        _acc(dsh_ref.at[0], first_of_seq, sums[1])
        _acc(dgain_ref, i == 0, sums[2])

    seq = lambda i: (i // tps, 0, 0)
    row = lambda i: (i, 0)
    return _call(
        body, "mixin_bwd", (T // tm,),
        [
            pl.BlockSpec((tm, D), row),
            pl.BlockSpec((tm, D), row),
            pl.BlockSpec((tm, P), row),
            pl.BlockSpec((1, D), lambda i: (0, 0)),
            pl.BlockSpec((1, 1, D), seq),
            pl.BlockSpec((D, P), lambda i: (0, 0)),
        ],
        [
            pl.BlockSpec((tm, D), row),
            pl.BlockSpec((1, 1, D), seq),
            pl.BlockSpec((1, 1, D), seq),
            pl.BlockSpec((1, D), lambda i: (0, 0)),
        ],
        [
            jax.ShapeDtypeStruct((T, D), F32),
            jax.ShapeDtypeStruct((B, 1, D), F32),
            jax.ShapeDtypeStruct((B, 1, D), F32),
            jax.ShapeDtypeStruct((1, D), F32),
        ],
        [],
        (dxo, x, dproj, gain, sc, win), comm)


def _head_mean(z, pmat, exact=True):
    hi = z.astype(BF16)
    if not exact:
        return _dot(hi, pmat)
    lo = (z - hi.astype(F32)).astype(BF16)
    return _dot(hi, pmat) + _dot(lo, pmat)


def _gelu_parts(x):
    cdf = 0.5 * (1.0 + lax.erf(x * (1.0 / math.sqrt(2.0))))
    return x * cdf, cdf


def _gelu_grad(x, cdf):
    return cdf + x * jnp.exp(-0.5 * x * x) * (1.0 / math.sqrt(2.0 * math.pi))


LANES = 128


def _head_blocks(da):
    hd = da // N_HEADS
    lb = min(LANES, da)
    col = lax.broadcasted_iota(jnp.int32, (1, lb), 1)
    return lb, lb // hd, da // lb, [(col >= h * hd) & (col < (h + 1) * hd) for h in range(lb // hd)]


def _mix_heads(w_stack, v, da):
    lb, hpb, nb, masks = _head_blocks(da)
    outs = []
    for b in range(nb):
        res = _dot(w_stack[b * hpb * CHUNK:(b + 1) * hpb * CHUNK], v[:, b * lb:(b + 1) * lb])
        out = res[0:CHUNK]
        for h in range(1, hpb):
            out = jnp.where(masks[h], res[h * CHUNK:(h + 1) * CHUNK], out)
        outs.append(out)
    return outs[0] if nb == 1 else jnp.concatenate(outs, axis=1)


def _mix_heads_grad(dm, v, da):
    lb, hpb, nb, masks = _head_blocks(da)
    outs = []
    for b in range(nb):
        dmb = dm[:, b * lb:(b + 1) * lb]
        stack = jnp.concatenate([jnp.where(masks[h], dmb, jnp.zeros_like(dmb)) for h in range(hpb)], axis=0)
        outs.append(_dot_nt(stack, v[:, b * lb:(b + 1) * lb]))
    return outs[0] if nb == 1 else jnp.concatenate(outs, axis=0)


def _causal_stack(w, transposed):
    r = lax.broadcasted_iota(jnp.int32, w.shape, 0) % CHUNK
    c = lax.broadcasted_iota(jnp.int32, w.shape, 1)
    keep = (c >= r) if transposed else (c <= r)
    return jnp.where(keep, w, 0.0)


def _mix_core_forward(proj, zprev, prm, da, db, saved=None):
    n = proj.shape[0]
    ua = proj[:, 0:da]
    va = proj[:, da:2 * da]
    bg = proj[:, 2 * da:2 * da + db]
    cg = proj[:, 2 * da + db:2 * da + 2 * db]
    xb = proj[:, 2 * da + 2 * db:]
    if saved is None:
        ug, ucdf = _gelu_parts(ua)
        vg, vcdf = _gelu_parts(va)
        zc = vg - _head_mean(vg, prm["pmat"])
        rs = lax.rsqrt(_head_mean(zc * zc, prm["pmat"], exact=False) + EPS)
        vhat = zc * rs
        vln = (vhat * prm["lng"] + prm["lnb"]).astype(BF16)
        wst = _causal_stack(prm["wst"], False).astype(BF16)
        mixed = [_mix_heads(wst, vln[j * CHUNK:(j + 1) * CHUNK], da) + prm["bias"] for j in range(n // CHUNK)]
        mixed = mixed[0] if len(mixed) == 1 else jnp.concatenate(mixed, axis=0)
    else:
        ucdf, vcdf, vhat, rs, mixed = [saved[k].astype(F32) for k in range(5)]
        ug = ua * ucdf
        vln = (vhat * prm["lng"] + prm["lnb"]).astype(BF16)
    ya = ug * mixed
    z = cg * xb
    row = lax.broadcasted_iota(jnp.int32, z.shape, 0)
    z1 = jnp.where(row == 0, zprev[7:8], pltpu.roll(z, 1, 0))
    z2 = jnp.where(row == 0, zprev[6:7], jnp.where(row == 1, zprev[7:8], pltpu.roll(z, 2, 0)))
    cw = prm["convw"]
    conv = z2 * cw[0:1] + z1 * cw[1:2] + z * cw[2:3]
    yb = bg * conv
    yah, ra = _rms(ya)
    ybh, rb = _rms(yb)
    return dict(ua=ua, va=va, bg=bg, cg=cg, xb=xb, ug=ug, ucdf=ucdf, vcdf=vcdf, rs=rs, vhat=vhat, vln=vln,
                mixed=mixed, z=z, z1=z1, z2=z2, conv=conv, yah=yah, ra=ra, ybh=ybh, rb=rb)


def _mix_params(lng_ref, lnb_ref, wst_ref, bias_ref, pmat_ref, convw_ref):
    return dict(lng=lng_ref[...], lnb=lnb_ref[...], wst=wst_ref[...], bias=bias_ref[...],
                pmat=pmat_ref[...], convw=convw_ref[...])


def _mix_core_fwd(proj, x, gate, wout, lng, lnb, wst, bias, pmat, convw, og, comm=None):
    T, P = proj.shape
    D = x.shape[1]
    B = gate.shape[0]
    da = lng.shape[1]
    db = convw.shape[1]
    tm = _tile(MIX_TILE, T // B)
    tps = (T // B) // tm

    def body(proj_ref, x_ref, gate_ref, wout_ref, lng_ref, lnb_ref, wst_ref, bias_ref, pmat_ref, convw_ref,
             og_ref, xo_ref, yn_ref, sv_ref, halo):
        i = pl.program_id(0)

        @pl.when((i % tps) == 0)
        def _():
            halo[...] = jnp.zeros_like(halo)

        prm = _mix_params(lng_ref, lnb_ref, wst_ref, bias_ref, pmat_ref, convw_ref)
        r = _mix_core_forward(proj_ref[...], halo[...], prm, da, db)
        halo[...] = r["z"][tm - 8:tm]
        for k, name in enumerate(("ucdf", "vcdf", "vhat", "rs", "mixed")):
            sv_ref[k] = r[name].astype(BF16)
        og = og_ref[...]
        yn_ref[:, 0:da] = (r["yah"] * og[:, 0:da]).astype(BF16)
        yn_ref[:, da:] = (r["ybh"] * og[:, da:]).astype(BF16)
        xo_ref[...] = x_ref[...] + gate_ref[0] * _dot(yn_ref[...], wout_ref[...])

    full = lambda a: pl.BlockSpec(a.shape, lambda i: (0,) * a.ndim)
    return _call(
        body, "mix_core_fwd", (T // tm,),
        [
            pl.BlockSpec((tm, P), lambda i: (i, 0)),
            pl.BlockSpec((tm, D), lambda i: (i, 0)),
            pl.BlockSpec((1, 1, D), lambda i: (i // tps, 0, 0)),
            full(wout), full(lng), full(lnb), full(wst), full(bias), full(pmat), full(convw), full(og),
        ],
        [pl.BlockSpec((tm, D), lambda i: (i, 0)), pl.BlockSpec((tm, D), lambda i: (i, 0)),
         pl.BlockSpec((5, tm, da), lambda i: (0, i, 0))],
        [jax.ShapeDtypeStruct((T, D), F32), jax.ShapeDtypeStruct((T, D), BF16),
         jax.ShapeDtypeStruct((5, T, da), BF16)],
        [pltpu.VMEM((8, db), F32)],
        (proj, x, gate, wout, lng, lnb, wst, bias, pmat, convw, og), comm)


def _mix_core_bwd(proj, sv, dxo, gate, wout, lng, lnb, wstt, pmat, convw, og, comm=None):
    T, P = proj.shape
    D = dxo.shape[1]
    B = gate.shape[0]
    da = lng.shape[1]
    db = convw.shape[1]
    assert da == db and P == 2 * da + 3 * db
    tm = _tile(MIX_TILE, T // B)
    tps = (T // B) // tm
    nt = T // tm
    hd = da // N_HEADS

    def body(proj_ref, cgp_ref, xbp_ref, sv_ref, dxo_ref, gate_ref, wout_ref, lng_ref, lnb_ref, wstt_ref,
             pmat_ref, convw_ref, og_ref,
             dproj_ref, do_ref, dgate_ref, dog_ref, dwst_ref, dbias_ref, dlng_ref, dlnb_ref, dconvw_ref, carry):
        i = pl.program_id(0)
        ri = nt - 1 - i
        first = i == 0
        end_of_seq = (ri % tps) == tps - 1
        start_of_seq = (ri % tps) == 0

        @pl.when(end_of_seq)
        def _():
            carry[...] = jnp.zeros_like(carry)

        prm = dict(lng=lng_ref[...], lnb=lnb_ref[...], pmat=pmat_ref[...], convw=convw_ref[...])
        zprev = jnp.where(start_of_seq, 0.0, cgp_ref[...] * xbp_ref[...])
        r = _mix_core_forward(proj_ref[...], zprev, prm, da, db, saved=sv_ref)
        og = og_ref[...]
        pmat = prm["pmat"]

        yn = jnp.concatenate([(r["yah"] * og[:, 0:da]).astype(BF16), (r["ybh"] * og[:, da:]).astype(BF16)], axis=1)
        dxo = dxo_ref[...]
        o = _dot(yn, wout_ref[...])
        _acc(dgate_ref.at[0], end_of_seq, jnp.sum(dxo * o, axis=0, keepdims=True))
        d_o = (gate_ref[0] * dxo).astype(BF16)
        do_ref[...] = d_o
        dyn = _dot_nt(d_o, wout_ref[...])

        def rms_bwd(dyn_g, yh, rr, og_g):
            dog_g = jnp.sum(dyn_g * yh, axis=0, keepdims=True)
            dyh = dyn_g * og_g
            return rr * (dyh - yh * jnp.mean(dyh * yh, axis=-1, keepdims=True)), dog_g

        dya, dog_a = rms_bwd(dyn[:, 0:da], r["yah"], r["ra"], og[:, 0:da])
        dyb, dog_b = rms_bwd(dyn[:, da:], r["ybh"], r["rb"], og[:, da:])
        _acc(dog_ref, first, jnp.concatenate([dog_a, dog_b], axis=1))

        dug = dya * r["mixed"]
        dmixed = dya * r["ug"]
        wstt_b = _causal_stack(wstt_ref[...], True).astype(BF16)
        dbias = jnp.zeros((CHUNK, da), F32)
        dwst = jnp.zeros((N_HEADS * CHUNK, CHUNK), F32)
        dvln = []
        for j in range(tm // CHUNK):
            dm = dmixed[j * CHUNK:(j + 1) * CHUNK]
            dbias = dbias + dm
            dmb = dm.astype(BF16)
            dwst = dwst + _mix_heads_grad(dmb, r["vln"][j * CHUNK:(j + 1) * CHUNK], da)
            dvln.append(_mix_heads(wstt_b, dmb, da))
        dvln = dvln[0] if len(dvln) == 1 else jnp.concatenate(dvln, axis=0)
        _acc(dbias_ref, first, dbias)
        _acc(dwst_ref, first, dwst)
        _acc(dlng_ref, first, jnp.sum(dvln * r["vhat"], axis=0, keepdims=True))
        _acc(dlnb_ref, first, jnp.sum(dvln, axis=0, keepdims=True))
        dvhat = dvln * prm["lng"]
        dvg = r["rs"] * (dvhat - _head_mean(dvhat, pmat, exact=False)
                         - r["vhat"] * _head_mean(dvhat * r["vhat"], pmat, exact=False))
        dproj_ref[:, 0:da] = (dug * _gelu_grad(r["ua"], r["ucdf"])).astype(BF16)
        dproj_ref[:, da:2 * da] = (dvg * _gelu_grad(r["va"], r["vcdf"])).astype(BF16)

        dproj_ref[:, 2 * da:2 * da + db] = (dyb * r["conv"]).astype(BF16)
        dconv = dyb * r["bg"]
        dcw = jnp.concatenate([
            jnp.sum(dconv * r["z2"], axis=0, keepdims=True),
            jnp.sum(dconv * r["z1"], axis=0, keepdims=True),
            jnp.sum(dconv * r["z"], axis=0, keepdims=True),
            jnp.zeros((5, db), F32)], axis=0)
        _acc(dconvw_ref, first, dcw)
        nxt = carry[...]
        row = lax.broadcasted_iota(jnp.int32, dconv.shape, 0)
        dc1 = jnp.where(row == tm - 1, nxt[0:1], pltpu.roll(dconv, tm - 1, 0))
        dc2 = jnp.where(row == tm - 2, nxt[0:1], jnp.where(row == tm - 1, nxt[1:2], pltpu.roll(dconv, tm - 2, 0)))
        carry[...] = dconv[0:8]
        cw = prm["convw"]
        dz = dconv * cw[2:3] + dc1 * cw[1:2] + dc2 * cw[0:1]
        dproj_ref[:, 2 * da + db:2 * da + 2 * db] = (dz * r["xb"]).astype(BF16)
        dproj_ref[:, 2 * da + 2 * db:] = (dz * r["cg"]).astype(BF16)

        @pl.when(i == nt - 1)
        def _():
            dwst_ref[...] = _causal_stack(dwst_ref[...], False)
            dbias_ref[...] = _head_mean(dbias_ref[...], pmat) * float(hd)

    full = lambda a: pl.BlockSpec(a.shape, lambda i: (0,) * a.ndim)
    const = lambda i: (0, 0)
    rev = lambda i: (nt - 1 - i, 0)
    prev8 = lambda col: (lambda i: (jnp.maximum((nt - 1 - i) * (tm // 8) - 1, 0), col))
    return _call(
        body, "mix_core_bwd", (nt,),
        [
            pl.BlockSpec((tm, P), rev),
            pl.BlockSpec((8, db), prev8((2 * da + db) // db)),
            pl.BlockSpec((8, db), prev8((2 * da + 2 * db) // db)),
            pl.BlockSpec((5, tm, da), lambda i: (0, nt - 1 - i, 0)),
            pl.BlockSpec((tm, D), rev),
            pl.BlockSpec((1, 1, D), lambda i: ((nt - 1 - i) // tps, 0, 0)),
            full(wout), full(lng), full(lnb), full(wstt), full(pmat), full(convw), full(og),
        ],
        [
            pl.BlockSpec((tm, P), rev),
            pl.BlockSpec((tm, D), rev),
            pl.BlockSpec((1, 1, D), lambda i: ((nt - 1 - i) // tps, 0, 0)),
            pl.BlockSpec((1, D), const),
            pl.BlockSpec((N_HEADS * CHUNK, CHUNK), const),
            pl.BlockSpec((CHUNK, da), const),
            pl.BlockSpec((1, da), const),
            pl.BlockSpec((1, da), const),
            pl.BlockSpec((8, db), const),
        ],
        [
            jax.ShapeDtypeStruct((T, P), BF16),
            jax.ShapeDtypeStruct((T, D), BF16),
            jax.ShapeDtypeStruct((B, 1, D), F32),
            jax.ShapeDtypeStruct((1, D), F32),
            jax.ShapeDtypeStruct((N_HEADS * CHUNK, CHUNK), F32),
            jax.ShapeDtypeStruct((CHUNK, da), F32),
            jax.ShapeDtypeStruct((1, da), F32),
            jax.ShapeDtypeStruct((1, da), F32),
            jax.ShapeDtypeStruct((8, db), F32),
        ],
        [pltpu.VMEM((8, db), F32)],
        (proj, proj, proj, sv, dxo, gate, wout, lng, lnb, wstt, pmat, convw, og), comm)


def _loss_head(x, target, gain):
    T, D = x.shape
    tm = _tile(TOKEN_TILE, T)

    def body(x_ref, t_ref, gain_ref, dx_ref, loss_ref, dgain_ref):
        first = pl.program_id(0) == 0
        xh, r = _rms(x_ref[...])
        gain = gain_ref[...]
        err = xh * gain - t_ref[...]
        _acc(loss_ref, first, jnp.zeros((8, 128), F32) + 0.5 * jnp.sum(err * err) / D)
        dout = err * (1.0 / D)
        _acc(dgain_ref, first, jnp.sum(dout * xh, axis=0, keepdims=True))
        dy = dout * gain
        dx_ref[...] = r * (dy - xh * jnp.mean(dy * xh, axis=-1, keepdims=True))

    return _call(
        body, "loss_head", (T // tm,),
        [
            pl.BlockSpec((tm, D), lambda i: (i, 0)),
            pl.BlockSpec((tm, D), lambda i: (i, 0)),
            pl.BlockSpec((1, D), lambda i: (0, 0)),
        ],
        [
            pl.BlockSpec((tm, D), lambda i: (i, 0)),
            pl.BlockSpec((8, 128), lambda i: (0, 0)),
            pl.BlockSpec((1, D), lambda i: (0, 0)),
        ],
        [
            jax.ShapeDtypeStruct((T, D), F32),
            jax.ShapeDtypeStruct((8, 128), F32),
            jax.ShapeDtypeStruct((1, D), F32),
        ],
        [],
        (x, target, gain))[0]


def _ada_fwd(c_all, ada_w, ada_b):
    n, D = c_all.shape
    L, _, sa = ada_w.shape
    tn = _tile(768, sa)

    def body(c_ref, w_ref, b_ref, act_ref, o_ref):
        c = c_ref[...]
        act = (c * _sigmoid(c)).astype(BF16)
        act_ref[...] = act
        o_ref[...] = _dot(act, w_ref[...].astype(BF16)) + b_ref[...]

    return _call(
        body, "ada_fwd", (L, sa // tn),
        [
            pl.BlockSpec((n, D), lambda l, j: (0, 0)),
            pl.BlockSpec((None, D, tn), lambda l, j: (l, 0, j)),
            pl.BlockSpec((None, 1, tn), lambda l, j: (l, 0, j)),
        ],
        [
            pl.BlockSpec((n, D), lambda l, j: (0, 0)),
            pl.BlockSpec((None, n, tn), lambda l, j: (l, 0, j)),
        ],
        [jax.ShapeDtypeStruct((n, D), BF16), jax.ShapeDtypeStruct((L, n, sa), F32)],
        [],
        (c_all, ada_w, ada_b))[0]


def _ada_bwd(c_act, d_ada, comm=None):
    n, D = c_act.shape
    L, _, sa = d_ada.shape
    tn = _tile(768, sa)

    def body(c_ref, d_ref, o_ref):
        o_ref[...] = _dot_tn(c_ref[...], d_ref[...])

    return _call(
        body, "ada_bwd", (L, sa // tn),
        [pl.BlockSpec((n, D), lambda l, j: (0, 0)), pl.BlockSpec((None, n, tn), lambda l, j: (l, 0, j))],
        [pl.BlockSpec((None, D, tn), lambda l, j: (l, 0, j))],
        [jax.ShapeDtypeStruct((L, D, sa), F32)],
        [],
        (c_act, d_ada), comm)


def _colsum(a):
    L, n, C = a.shape

    def body(a_ref, o_ref):
        o_ref[...] = jnp.sum(a_ref[...], axis=0, keepdims=True)

    return _call(
        body, "colsum", (L,),
        [pl.BlockSpec((None, n, C), lambda l: (l, 0, 0))],
        [pl.BlockSpec((None, 1, C), lambda l: (l, 0, 0))],
        [jax.ShapeDtypeStruct((L, 1, C), F32)],
        [],
        (a,))[0][0]


def _row_tile(rows, cols, nbuf):
    budget = VMEM_LIMIT // 3 // (2 * nbuf * 4 * cols)
    t = rows
    while t > max(budget, 8) and t % 2 == 0 and (t // 2) % 8 == 0:
        t //= 2
    return t


def _pair_sum(g, recv, core):
    n, _, R, C = g.shape
    tr = _row_tile(R, C, 3)

    def body(core_ref, g_ref, r_ref, o_ref):
        o_ref[...] = (g_ref[...] + r_ref[...]).astype(BF16)

    return pl.pallas_call(
        body,
        name="pair_sum",
        grid_spec=pltpu.PrefetchScalarGridSpec(
            num_scalar_prefetch=1,
            grid=(n, R // tr),
            in_specs=[
                pl.BlockSpec((None, None, tr, C), lambda i, r, core_ref: (i, core_ref[0], r, 0)),
                pl.BlockSpec((None, tr, C), lambda i, r, core_ref: (i, r, 0)),
            ],
            out_specs=pl.BlockSpec((None, tr, C), lambda i, r, core_ref: (i, r, 0)),
        ),
        out_shape=jax.ShapeDtypeStruct((n, R, C), BF16),
        compiler_params=pltpu.CompilerParams(dimension_semantics=("arbitrary", "arbitrary"),
                                             vmem_limit_bytes=VMEM_LIMIT),
    )(core, g, recv)


def _chip_sum(q, core, l, n_layers, prev):
    nq, R, C = q.shape
    tr = _row_tile(R, C, 4)

    def body(core_ref, q_ref, *rest):
        o_ref = rest[-1]
        s = q_ref[0].astype(F32)
        for j in range(1, nq):
            s = s + q_ref[j].astype(F32)
        o_ref[...] = s

    in_specs = [pl.BlockSpec((nq, tr, C), lambda r, core_ref: (0, r, 0))]
    args = [core, q]
    aliases = {}
    if prev is not None:
        in_specs.append(ANY)
        args.append(prev)
        aliases = {2: 0}
    return pl.pallas_call(
        body,
        name="chip_sum",
        grid_spec=pltpu.PrefetchScalarGridSpec(
            num_scalar_prefetch=1,
            grid=(R // tr,),
            in_specs=in_specs,
            out_specs=pl.BlockSpec((None, None, tr, C), lambda r, core_ref: (l, core_ref[0], r, 0)),
        ),
        out_shape=jax.ShapeDtypeStruct((n_layers, 2, R, C), F32),
        input_output_aliases=aliases,
        compiler_params=pltpu.CompilerParams(dimension_semantics=("arbitrary",), vmem_limit_bytes=VMEM_LIMIT),
    )(*args)


def _sum_blocks(a, n):
    M = a.shape[0] // n
    C = a.shape[1]

    def body(a_ref, o_ref):
        s = a_ref[0:M]
        for j in range(1, n):
            s = s + a_ref[j * M:(j + 1) * M]
        o_ref[...] = s

    return pl.pallas_call(
        body,
        name="sum_blocks",
        out_shape=jax.ShapeDtypeStruct((M, C), F32),
        compiler_params=pltpu.CompilerParams(vmem_limit_bytes=VMEM_LIMIT),
    )(a)


def _adamw(w, g, m, v, emit_grad=False):
    R, C = w.shape
    n_out = 4 if emit_grad else 3
    tr = _row_tile(R, C, 4 + n_out) if R % 8 == 0 else R

    def body(w_ref, g_ref, m_ref, v_ref, d_ref, nm_ref, nv_ref, *g_out):
        g = g_ref[...]
        m = ADAM_B1 * m_ref[...] + (1.0 - ADAM_B1) * g
        v = ADAM_B2 * v_ref[...] + (1.0 - ADAM_B2) * (g * g)
        m_hat = m / (1.0 - ADAM_B1 ** ADAM_STEP)
        v_hat = v / (1.0 - ADAM_B2 ** ADAM_STEP)
        d_ref[...] = -ADAM_LR * (m_hat / (jnp.sqrt(v_hat) + ADAM_EPS) + ADAM_WD * w_ref[...])
        nm_ref[...] = m
        nv_ref[...] = v
        if emit_grad:
            g_out[0][...] = g

    spec = pl.BlockSpec((tr, C), lambda i: (i, 0))
    return _call(body, "adamw", (R // tr,), [spec] * 4, [spec] * n_out, [jax.ShapeDtypeStruct((R, C), F32)] * n_out,
                 [], (w, g, m, v))[0]


def kernel(x, c, ada_w, ada_b, norm_ffn1_g, ffn1_w_gu, ffn1_w_down, norm_mix_g, mix_w_in, sgu_ln_g, sgu_ln_b, sgu_w_s, sgu_b, conv_w, out_norm_g, mix_w_out, norm_ffn2_g, ffn2_w_gu, ffn2_w_down, final_norm_g, loss_target, m_ada_w, m_ada_b, m_norm_ffn1_g, m_ffn1_w_gu, m_ffn1_w_down, m_norm_mix_g, m_mix_w_in, m_sgu_ln_g, m_sgu_ln_b, m_sgu_w_s, m_sgu_b, m_conv_w, m_out_norm_g, m_mix_w_out, m_norm_ffn2_g, m_ffn2_w_gu, m_ffn2_w_down, m_final_norm_g, v_ada_w, v_ada_b, v_norm_ffn1_g, v_ffn1_w_gu, v_ffn1_w_down, v_norm_mix_g, v_mix_w_in, v_sgu_ln_g, v_sgu_ln_b, v_sgu_w_s, v_sgu_b, v_conv_w, v_out_norm_g, v_mix_w_out, v_norm_ffn2_g, v_ffn2_w_gu, v_ffn2_w_down, v_final_norm_g):
    weights = dict(ada_w=ada_w, ada_b=ada_b, norm_ffn1_g=norm_ffn1_g, ffn1_w_gu=ffn1_w_gu, ffn1_w_down=ffn1_w_down,
                   norm_mix_g=norm_mix_g, mix_w_in=mix_w_in, sgu_ln_g=sgu_ln_g, sgu_ln_b=sgu_ln_b, sgu_w_s=sgu_w_s,
                   sgu_b=sgu_b, conv_w=conv_w, out_norm_g=out_norm_g, mix_w_out=mix_w_out, norm_ffn2_g=norm_ffn2_g,
                   ffn2_w_gu=ffn2_w_gu, ffn2_w_down=ffn2_w_down, final_norm_g=final_norm_g)
    m_in = dict(ada_w=m_ada_w, ada_b=m_ada_b, norm_ffn1_g=m_norm_ffn1_g, ffn1_w_gu=m_ffn1_w_gu,
                ffn1_w_down=m_ffn1_w_down, norm_mix_g=m_norm_mix_g, mix_w_in=m_mix_w_in, sgu_ln_g=m_sgu_ln_g,
                sgu_ln_b=m_sgu_ln_b, sgu_w_s=m_sgu_w_s, sgu_b=m_sgu_b, conv_w=m_conv_w, out_norm_g=m_out_norm_g,
                mix_w_out=m_mix_w_out, norm_ffn2_g=m_norm_ffn2_g, ffn2_w_gu=m_ffn2_w_gu, ffn2_w_down=m_ffn2_w_down,
                final_norm_g=m_final_norm_g)
    v_in = dict(ada_w=v_ada_w, ada_b=v_ada_b, norm_ffn1_g=v_norm_ffn1_g, ffn1_w_gu=v_ffn1_w_gu,
                ffn1_w_down=v_ffn1_w_down, norm_mix_g=v_norm_mix_g, mix_w_in=v_mix_w_in, sgu_ln_g=v_sgu_ln_g,
                sgu_ln_b=v_sgu_ln_b, sgu_w_s=v_sgu_w_s, sgu_b=v_sgu_b, conv_w=v_conv_w, out_norm_g=v_out_norm_g,
                mix_w_out=v_mix_w_out, norm_ffn2_g=v_norm_ffn2_g, ffn2_w_gu=v_ffn2_w_gu, ffn2_w_down=v_ffn2_w_down,
                final_norm_g=v_final_norm_g)

    B, S, D = x.shape
    T = B * S
    L = ada_w.shape[0]
    F = ffn1_w_down.shape[1] * N_CHIP
    P = mix_w_in.shape[2] * N_CHIP
    DA = D // 2
    DB = D - DA
    HD = DA // N_HEADS
    SA = ada_w.shape[2]
    n_all = B * N_DEV
    mx, my, mc = _position()
    chip = 2 * mx + my
    dev = 2 * chip + mc
    core = jnp.reshape(mc, (1,)).astype(jnp.int32)

    big = ["ffn1_w_gu", "ffn1_w_down", "mix_w_in", "mix_w_out", "ffn2_w_gu", "ffn2_w_down"]
    col_sharded = dict(ffn1_w_gu=True, ffn1_w_down=False, mix_w_in=True, mix_w_out=False,
                       ffn2_w_gu=True, ffn2_w_down=False)
    shards = {k: weights[k].astype(BF16) for k in big}
    gather = lambda l, *names: _gather_comm([(shards[k], l, col_sharded[k]) for k in names])
    full = [dict() for _ in range(L)]

    def arrived(l, names, res):
        full[l].update(zip(names, res))

    n_cw = L * conv_w.shape[1]
    cw_block = jnp.pad(conv_w.reshape(n_cw, conv_w.shape[2]), ((0, 8 - n_cw), (0, 0)))
    c_all, cw_all = _comm_call(_merge(_all_gather_comm(c.reshape(8, B * D // 8)), _all_gather_comm(cw_block)),
                               "gather_c")
    c_all = c_all.reshape(n_all, D)
    cw_all = cw_all.reshape(N_CHIP, 2, 8, conv_w.shape[2])[:, 0, :n_cw]
    conv_full = jnp.transpose(cw_all.reshape(N_CHIP, L, conv_w.shape[1], conv_w.shape[2]), (1, 2, 0, 3))
    conv_full = conv_full.reshape(L, conv_w.shape[1], DB)
    ada_b_mine = lax.dynamic_slice_in_dim(ada_b, chip * SA, SA, axis=1).reshape(L, 1, SA)
    c_act, ada_part = _ada_fwd(c_all, ada_w, ada_b_mine)
    ada_all, first_w = _comm_call(_merge(_all_gather_comm(ada_part.reshape(L * n_all, SA)), gather(0, big[0])),
                                  "gather_first")
    arrived(0, big[:1], [first_w])
    ada_all = ada_all.reshape(N_CHIP, 2, L, n_all, SA)[:, 0]
    ada_all = jnp.transpose(ada_all, (1, 2, 0, 3)).reshape(L, n_all, N_CHIP * SA)
    ada = lax.dynamic_slice_in_dim(ada_all, dev * B, B, axis=1).reshape(L, B, N_MOD, 1, D)
    mods = [[ada[l, :, j] for j in range(N_MOD)] for l in range(L)]

    x0 = x.reshape(T, D)
    gains = lambda name, l: weights[name][l].reshape(1, D)
    hmask = jnp.repeat(jnp.eye(N_HEADS, dtype=F32), HD, axis=0)
    pmat = (jnp.repeat(hmask, HD, axis=1) / HD).astype(BF16)

    def mix_consts(l):
        lng = jnp.tile(sgu_ln_g[l], N_HEADS).reshape(1, DA)
        lnb = jnp.tile(sgu_ln_b[l], N_HEADS).reshape(1, DA)
        wst = sgu_w_s[l].reshape(N_HEADS * CHUNK, CHUNK)
        wstt = jnp.swapaxes(sgu_w_s[l], 1, 2).reshape(N_HEADS * CHUNK, CHUNK)
        bias = jnp.repeat(jnp.transpose(sgu_b[l]), HD, axis=1)
        return lng, lnb, wst, wstt, bias

    def fetch(fn, *args, bring=()):
        bring = [(l, k) for l, k in bring if l < L]
        comm = _gather_comm([(shards[k], l, col_sharded[k]) for l, k in bring]) if bring else None
        res, got = fn(*args, comm)
        for (l, k), a in zip(bring, got):
            full[l][k] = a
        return res

    saved = []
    xc = x0
    for l in range(L):
        sh1, sc1, g1, sh2, sc2, g2, sh3, sc3, g3 = mods[l]
        lng, lnb, wst, wstt, bias = mix_consts(l)
        w = full[l]
        own = l == 0
        gu1, a1 = fetch(_ffn_up, xc, gains("norm_ffn1_g", l), sh1, sc1, w["ffn1_w_gu"],
                        bring=[(l, "ffn1_w_down"), (l, "mix_w_in"), (l, "mix_w_out")] if own else [(l, "ffn2_w_gu")])
        xa, f1 = fetch(_ffn_down, a1, xc, g1, w["ffn1_w_down"], bring=[(l, "ffn2_w_down")])
        proj, h2 = fetch(_mixin_fwd, xa, gains("norm_mix_g", l), sh2, sc2, w["mix_w_in"], bring=[(l + 1, "mix_w_in")])
        xb, yn, sv = fetch(_mix_core_fwd, proj, xa, g2, w["mix_w_out"], lng, lnb, wst, bias, pmat, conv_full[l],
                           gains("out_norm_g", l), bring=[(l, "ffn2_w_gu")] if own else [])
        gu2, a2 = fetch(_ffn_up, xb, gains("norm_ffn2_g", l), sh3, sc3, w["ffn2_w_gu"],
                        bring=[(l + 1, "ffn1_w_gu"), (l + 1, "mix_w_out")])
        xd, f2 = fetch(_ffn_down, a2, xb, g3, w["ffn2_w_down"], bring=[(l + 1, "ffn1_w_down")])
        saved.append(dict(x0=xc, xa=xa, xb=xb, gu1=gu1, a1=a1, f1=f1, proj=proj, h2=h2, yn=yn, sv=sv,
                          gu2=gu2, a2=a2, f2=f2))
        xc = xd

    dx, loss_block, d_final = _loss_head(xc, loss_target.reshape(T, D), final_norm_g.reshape(1, D))

    reduced = dict.fromkeys(big)

    def halves(name, g):
        if g.ndim == 4:
            return g
        return g.reshape(N_CHIP, 2, weights[name].shape[1] // 2, g.shape[-1])

    class Reduction:
        def __init__(self, l, name, g):
            self.l, self.name, self.g, self.stage = l, name, halves(name, g), 0
            self.ici_bytes = 3 * (g.size // 8) * 2

        def step(self):
            self.stage += 1
            if self.stage == 1:
                return _sibling_half_comm([self.g])
            if self.stage == 2:
                return _scatter_comm([_pair_sum(self.g, self.got[0], core)])
            if self.stage == 3:
                reduced[self.name] = _chip_sum(self.got[0], core, self.l, L, reduced[self.name])
                return _share_comm([reduced[self.name]], self.l)
            reduced[self.name] = self.got[0]
            return None

    active, extra, gathered = [], [], {}

    def carry(fn, *args, us=None):
        left = None if us is None else us * SCATTER_BYTES_PER_US
        riders = []
        for r in active:
            if r.stage == 1 and left is not None:
                if r.ici_bytes > left * SCATTER_OVERSHOOT:
                    continue
                left -= r.ici_bytes
            riders.append(r)
        comms = [r.step() for r in riders] + [cm for cm, _ in extra]
        takers = [functools.partial(setattr, r, "got") for r in riders] + [cb for _, cb in extra]
        extra.clear()
        if fn is None:
            res, got = None, (_comm_call(_merge(*comms), "reduce_alone") if comms else [])
        else:
            res, got = fn(*args, comm=_merge(*comms))
        at = 0
        for cm, take in zip(comms, takers):
            take(got[at:at + len(cm.out_shape)])
            at += len(cm.out_shape)
        for r in riders:
            if r.stage == 3:
                r.step()
                active.remove(r)
        return res

    def reduce_later(l, name, g):
        active.append(Reduction(l, name, g))

    small = [None] * L
    dwsts = [None] * L
    d_ada = [None] * L
    for l in reversed(range(L)):
        sh1, sc1, g1, sh2, sc2, g2, sh3, sc3, g3 = mods[l]
        lng, lnb, wst, wstt, bias = mix_consts(l)
        s = saved[l]
        w = full[l]
        last = l == 0
        dgu2, = carry(_ffn_bwd_down, dx, s["gu2"], g3, w["ffn2_w_down"], us=100)
        dx, h3, df2, dsc3, dsh3, dgain3, dg3 = carry(_ffn_bwd_up, dx, s["xb"], dgu2, s["f2"], gains("norm_ffn2_g", l),
                                                     sh3, sc3, g3, w["ffn2_w_gu"], us=120)
        ffn2_grads = [
            lambda: reduce_later(l, "ffn2_w_gu", carry(_wgrad, h3, dgu2, D, 2 * F // N_CHIP, True, "wgrad_gu",
                                                       WGRAD_TOKENS // 2, us=110)[0]),
            lambda: reduce_later(l, "ffn2_w_down", carry(_wgrad, s["a2"], df2[None], F // 2, D, False, "wgrad_down",
                                                         us=50)[0])]
        if not last:
            ffn2_grads[0]()
            ffn2_grads[1]()
        dproj, d_o, dg2, dog, dwst, dbias, dlng, dlnb, dconvw = carry(
            _mix_core_bwd, s["proj"], s["sv"], dx, g2, w["mix_w_out"], lng, lnb, wstt, pmat, conv_full[l],
            gains("out_norm_g", l), us=150)
        mix_grads = [
            lambda: reduce_later(l, "mix_w_out", carry(_wgrad, s["yn"], d_o[None], D, D, False, "wgrad_out", us=30)[0]),
            lambda: reduce_later(l, "mix_w_in", carry(_wgrad, s["h2"], dproj[None], D, P // N_CHIP, True, "wgrad_in",
                                                      us=65)[0])]
        if not last:
            mix_grads[0]()
        dx, dsc2, dsh2, dgain2 = carry(_mixin_bwd, dx, s["xa"], dproj, gains("norm_mix_g", l), sc2, w["mix_w_in"], us=60)
        if not last:
            mix_grads[1]()
        dgu, = carry(_ffn_bwd_down, dx, s["gu1"], g1, w["ffn1_w_down"], us=100)
        dx, h1, df, dsc1, dsh1, dgain1, dg1 = carry(_ffn_bwd_up, dx, s["x0"], dgu, s["f1"], gains("norm_ffn1_g", l),
                                                    sh1, sc1, g1, w["ffn1_w_gu"], us=120)
        d_ada[l] = jnp.concatenate([dsh1, dsc1, dg1, dsh2, dsc2, dg2, dsh3, dsc3, dg3], axis=1).reshape(B, N_MOD * D)
        small[l] = [dgain1, dgain2, dgain3, dog, dlng, dlnb, dbias[:, ::HD], dconvw]
        dwsts[l] = dwst
        if last:
            flat = [a.reshape(-1, 128) for ll in range(L) for a in small[ll]]
            flat += [d_final.reshape(-1, 128), loss_block[0:1]]
            pad = (-sum(a.shape[0] for a in flat)) % 8
            packed = jnp.concatenate(flat + [jnp.zeros((pad, 128), F32)], axis=0)
            extra.append((_all_gather_comm(jnp.stack(d_ada).reshape(L * B, N_MOD * D)),
                          lambda got: gathered.update(d_ada=got[0])))
            extra.append((_all_gather_comm(packed), lambda got: gathered.update(small=got[0])))
            for ll in range(L):
                extra.append((_all_gather_comm(dwsts[ll]), lambda got, ll=ll: gathered.update({("dwst", ll): got[0]})))
        reduce_later(l, "ffn1_w_gu", carry(_wgrad, h1, dgu, D, 2 * F // N_CHIP, True, "wgrad_gu", WGRAD_TOKENS // 2,
                                           us=110)[0])
        reduce_later(l, "ffn1_w_down", carry(_wgrad, s["a1"], df[None], F // 2, D, False, "wgrad_down", us=50)[0])
        if last:
            ffn2_grads[0]()
            ffn2_grads[1]()
            mix_grads[1]()
            mix_grads[0]()
    grad_x = dx.reshape(B, S, D)

    def finished(name):
        while any(r.name == name for r in active):
            carry(None)
        return reduced[name].reshape(weights[name].shape)

    grads = {}
    d_ada_all = jnp.transpose(gathered["d_ada"].reshape(N_DEV, L, B, N_MOD * D), (1, 0, 2, 3))
    d_ada_all = d_ada_all.reshape(L, n_all, N_MOD * D)
    grads["ada_b"] = _colsum(d_ada_all).reshape(L, N_MOD * D)
    d_ada_mine = lax.dynamic_slice_in_dim(d_ada_all, chip * SA, SA, axis=2).astype(BF16)
    grads["ada_w"] = _ada_bwd(c_act, d_ada_mine)[0][0]

    total = _sum_blocks(gathered["small"].reshape(-1, 128), N_DEV)
    pieces, at = [], 0
    for a in flat:
        pieces.append(total[at:at + a.shape[0]])
        at += a.shape[0]
    per_layer = len(small[0])
    stack = lambda j, shape: jnp.stack([pieces[l * per_layer + j].reshape(shape) for l in range(L)])
    grads["norm_ffn1_g"] = stack(0, (D,))
    grads["norm_mix_g"] = stack(1, (D,))
    grads["norm_ffn2_g"] = stack(2, (D,))
    grads["out_norm_g"] = stack(3, (D,))
    grads["sgu_ln_g"] = stack(4, (N_HEADS, HD)).sum(axis=1)
    grads["sgu_ln_b"] = stack(5, (N_HEADS, HD)).sum(axis=1)
    grads["sgu_b"] = jnp.swapaxes(stack(6, (CHUNK, N_HEADS)), 1, 2)
    g_conv = stack(7, (8, DB))[:, :conv_w.shape[1]]
    grads["conv_w"] = lax.dynamic_slice_in_dim(g_conv, chip * conv_w.shape[2], conv_w.shape[2], axis=2)
    grads["final_norm_g"] = pieces[-2].reshape(D)
    loss = pieces[-1][0, 0]
    grads["sgu_w_s"] = jnp.stack([_sum_blocks(gathered["dwst", l].reshape(-1, CHUNK), N_DEV) for l in range(L)])
    grads["sgu_w_s"] = grads["sgu_w_s"].reshape(L, N_HEADS, CHUNK, CHUNK)

    names = list(weights)
    delta, new_m, new_v = {}, {}, {}
    for k in big:
        grads[k] = finished(k)
    for k in names:
        wk = weights[k]
        view = (1, wk.shape[0]) if wk.ndim == 1 else (-1, wk.shape[-1])
        d, nm, nv, *g_again = _adamw(wk.reshape(view), grads[k].reshape(view), m_in[k].reshape(view),
                                     v_in[k].reshape(view), emit_grad=k in big)
        delta[k], new_m[k], new_v[k] = d.reshape(wk.shape), nm.reshape(wk.shape), nv.reshape(wk.shape)
        if g_again:
            grads[k] = g_again[0].reshape(wk.shape)

    return (loss, grad_x, *[grads[k] for k in names], *[delta[k] for k in names],
            *[new_m[k] for k in names], *[new_v[k] for k in names])
```

```python
import functools
import math

import jax
import jax.numpy as jnp
from jax import lax
from jax.experimental import pallas as pl
from jax.experimental.pallas import tpu as pltpu

F32 = jnp.float32
BF16 = jnp.bfloat16
MESH = pl.DeviceIdType.MESH

N_HEADS = 8
CHUNK = 128
N_MOD = 9
EPS = 1e-6
N_DEV = 8
N_CHIP = 4

ADAM_LR = 0.001
ADAM_B1 = 0.9
ADAM_B2 = 0.999
ADAM_EPS = 1e-08
ADAM_WD = 0.01
ADAM_STEP = 10

TOKEN_TILE = 512
BWD_TILE = 256
FF_SLAB = 768
MIX_TILE = 256
WGRAD_TOKENS = 2048
VMEM_LIMIT = 56 * 1024 * 1024

SCATTER_BYTES_PER_US = 68_000
SCATTER_OVERSHOOT = 1.25

ANY = pl.BlockSpec(memory_space=pl.ANY)


def _tile(pref, n):
    t = min(pref, n)
    assert n % t == 0, (pref, n)
    return t


def _slabs(n, width):
    return [slice(c0, min(c0 + width, n)) for c0 in range(0, n, width)]


def _dot(a, b):
    return jnp.dot(a, b, preferred_element_type=F32)


def _dot_nt(a, b):
    return lax.dot_general(a, b, (((1,), (1,)), ((), ())), preferred_element_type=F32)


def _dot_tn(a, b):
    return lax.dot_general(a, b, (((0,), (0,)), ((), ())), preferred_element_type=F32)


def _sigmoid(x):
    return 1.0 / (1.0 + jnp.exp(-x))


def _sigmoid_fast(x):
    return pl.reciprocal(1.0 + jnp.exp(-x), approx=True)


def _rms(x):
    r = lax.rsqrt(jnp.mean(x * x, axis=-1, keepdims=True) + EPS)
    return x * r, r


def _norm_mod_bwd(x, dh, gain, sc):
    xh, r = _rms(x)
    dsc = jnp.sum(dh * (xh * gain), axis=0, keepdims=True)
    dsh = jnp.sum(dh, axis=0, keepdims=True)
    dn = dh * (1.0 + sc)
    dgain = jnp.sum(dn * xh, axis=0, keepdims=True)
    dy = dn * gain
    dx = r * (dy - xh * jnp.mean(dy * xh, axis=-1, keepdims=True))
    return dx, dsc, dsh, dgain


def _acc(ref, first, val):
    @pl.when(first)
    def _():
        ref[...] = val

    @pl.when(jnp.logical_not(first))
    def _():
        ref[...] += val


class _Comm:
    def __init__(self, args, out_shape, scratch, phases, aliases=None):
        self.args, self.out_shape, self.scratch = list(args), list(out_shape), list(scratch)
        self.phases, self.aliases = phases, dict(aliases or {})


def _merge(*comms):
    comms = [c for c in comms if c is not None]
    if len(comms) <= 1:
        return comms[0] if comms else None
    args = [a for c in comms for a in c.args]
    out_shape = [o for c in comms for o in c.out_shape]
    scratch = [s for c in comms for s in c.scratch]
    aliases, ai, oi = {}, 0, 0
    for c in comms:
        aliases.update({ai + i: oi + o for i, o in c.aliases.items()})
        ai += len(c.args)
        oi += len(c.out_shape)

    def phases(ins, outs, sems):
        parts, ai, oi, si = [], 0, 0, 0
        for c in comms:
            parts.append(c.phases(ins[ai:ai + len(c.args)], outs[oi:oi + len(c.out_shape)], sems[si:si + len(c.scratch)]))
            ai, oi, si = ai + len(c.args), oi + len(c.out_shape), si + len(c.scratch)

        def run(k):
            def go():
                for p in parts:
                    if p[k] is not None:
                        p[k]()
            return go
        return run(0), run(1), run(2)

    return _Comm(args, out_shape, scratch, phases, aliases)


def _call(body, name, grid, in_specs, out_specs, out_shape, scratch, args, comm=None):
    n_in, n_out, n_scr = len(in_specs), len(out_specs), len(scratch)
    sem = ("arbitrary",) * len(grid)
    params = pltpu.CompilerParams(dimension_semantics=sem, vmem_limit_bytes=VMEM_LIMIT)
    if comm is None:
        res = pl.pallas_call(body, name=name, grid=grid, in_specs=in_specs, out_specs=out_specs, out_shape=out_shape,
                             scratch_shapes=scratch, compiler_params=params)(*args)
        return list(res), []
    m_in, m_out = len(comm.args), len(comm.out_shape)

    def full(*refs):
        c_in, c_min = refs[:n_in], refs[n_in:n_in + m_in]
        o = n_in + m_in
        c_out, c_mout = refs[o:o + n_out], refs[o + n_out:o + n_out + m_out]
        o += n_out + m_out
        c_scr, c_sem = refs[o:o + n_scr], refs[o + n_scr:]
        start, mid, finish = comm.phases(c_min, c_mout, c_sem)
        ids = [pl.program_id(a) for a in range(len(grid))]
        first = functools.reduce(jnp.logical_and, [i == 0 for i in ids])
        last = functools.reduce(jnp.logical_and, [i == g - 1 for i, g in zip(ids, grid)])
        pl.when(first)(start)
        if mid is not None:
            pl.when(last)(mid)
        body(*c_in, *c_out, *c_scr)
        pl.when(last)(finish)

    res = pl.pallas_call(
        full, name=name, grid=grid,
        in_specs=list(in_specs) + [ANY] * m_in,
        out_specs=list(out_specs) + [ANY] * m_out,
        out_shape=list(out_shape) + comm.out_shape,
        scratch_shapes=list(scratch) + comm.scratch,
        input_output_aliases={n_in + i: n_out + o for i, o in comm.aliases.items()},
        compiler_params=params,
    )(*args, *comm.args)
    return list(res[:n_out]), list(res[n_out:])


def _comm_call(comm, name):
    m_in, m_out = len(comm.args), len(comm.out_shape)

    def body(*refs):
        start, mid, finish = comm.phases(refs[:m_in], refs[m_in:m_in + m_out], refs[m_in + m_out:])
        start()
        if mid is not None:
            mid()
        finish()

    res = pl.pallas_call(
        body, name=name, in_specs=[ANY] * m_in, out_specs=[ANY] * m_out, out_shape=comm.out_shape,
        scratch_shapes=comm.scratch, input_output_aliases=comm.aliases,
    )(*comm.args)
    return list(res)


def _position():
    return lax.axis_index("x"), lax.axis_index("y"), lax.axis_index("c")


def _gather_comm(items):
    n = len(items)
    half = [s.shape[1] // 2 for s, _, _ in items]

    def full_shape(i):
        s, _, col = items[i]
        _, R, C = s.shape
        return jax.ShapeDtypeStruct((R, N_CHIP * C) if col else (N_CHIP * R, C), s.dtype)

    def phases(ins, outs, sems):
        send_sems, recv_sems, local_sems = sems
        x, y, c = _position()

        def region(i, chip, h):
            s, _, col = items[i]
            _, R, C = s.shape
            if col:
                return outs[i].at[pl.ds(h * half[i], half[i]), pl.ds(chip * C, C)]
            return outs[i].at[pl.ds(chip * R + h * half[i], half[i]), :]

        def mine(i, h):
            return ins[i].at[items[i][1], pl.ds(h * half[i], half[i]), :]

        def copies(kx, ky, kc):
            k_me = 2 * kx + ky
            sibling = (kx, ky, 1 - kc)
            chips = [(1 - kx, ky), (kx, 1 - ky), (1 - kx, 1 - ky)]
            local, first, passed, arrive_ici, arrive_d2d = [], [], [], [], []

            def remote(src, dst, s, to):
                return pltpu.make_async_remote_copy(src_ref=src, dst_ref=dst, send_sem=send_sems.at[s],
                                                    recv_sem=recv_sems.at[s], device_id=to, device_id_type=MESH)

            for i in range(n):
                for h in range(2):
                    local.append(pltpu.make_async_copy(mine(i, h), region(i, k_me, h), local_sems.at[2 * i + h]))
                for j, (px, py) in enumerate(chips):
                    s = 6 * i + j
                    first.append(remote(mine(i, kc), region(i, k_me, kc), s, (px, py, kc)))
                    got = region(i, 2 * px + py, kc)
                    arrive_ici.append(remote(got, got, s, (px, py, kc)))
                    passed.append(remote(got, got, s + 3, sibling))
                    other = region(i, 2 * px + py, 1 - kc)
                    arrive_d2d.append(remote(other, other, s + 3, sibling))
            return local, first, passed, arrive_ici, arrive_d2d

        def on_each_device(fn):
            def go():
                for kx in range(2):
                    for ky in range(2):
                        for kc in range(2):
                            pl.when((x == kx) & (y == ky) & (c == kc))(functools.partial(fn, *copies(kx, ky, kc)))
            return go

        def start(local, first, passed, arrive_ici, arrive_d2d):
            for cp in local + first:
                cp.start()

        def mid(local, first, passed, arrive_ici, arrive_d2d):
            for a, p in zip(arrive_ici, passed):
                a.wait_recv()
                p.start()

        def finish(local, first, passed, arrive_ici, arrive_d2d):
            for a in arrive_d2d:
                a.wait_recv()
            for cp in first + passed:
                cp.wait_send()
            for cp in local:
                cp.wait()

        return on_each_device(start), on_each_device(mid), on_each_device(finish)

    scratch = [pltpu.SemaphoreType.DMA((6 * n,)), pltpu.SemaphoreType.DMA((6 * n,)), pltpu.SemaphoreType.DMA((2 * n,))]
    return _Comm([s for s, _, _ in items], [full_shape(i) for i in range(n)], scratch, phases)


def _sibling_half_comm(gs):
    n = len(gs)

    def phases(ins, outs, sems):
        send_sems, recv_sems = sems
        x, y, c = _position()

        def copies():
            return [pltpu.make_async_remote_copy(
                src_ref=ins[i].at[:, 1 - c], dst_ref=outs[i], send_sem=send_sems.at[i], recv_sem=recv_sems.at[i],
                device_id=(x, y, 1 - c), device_id_type=MESH) for i in range(n)]

        def start():
            for cp in copies():
                cp.start()

        def finish():
            for cp in copies():
                cp.wait()

        return start, None, finish

    out_shape = [jax.ShapeDtypeStruct(g.shape[:1] + g.shape[2:], g.dtype) for g in gs]
    return _Comm(gs, out_shape, [pltpu.SemaphoreType.DMA((n,)), pltpu.SemaphoreType.DMA((n,))], phases)


def _scatter_comm(ps):
    n = len(ps)

    def phases(ins, outs, sems):
        send_sems, recv_sems, local_sems = sems
        x, y, c = _position()
        k_me = 2 * x + y
        chips = [(1 - x, y), (x, 1 - y), (1 - x, 1 - y)]

        def copies():
            local = [pltpu.make_async_copy(ins[i].at[k_me], outs[i].at[k_me], local_sems.at[i]) for i in range(n)]
            remote = [pltpu.make_async_remote_copy(
                src_ref=ins[i].at[2 * px + py], dst_ref=outs[i].at[k_me],
                send_sem=send_sems.at[3 * i + j], recv_sem=recv_sems.at[3 * i + j],
                device_id=(px, py, c), device_id_type=MESH) for i in range(n) for j, (px, py) in enumerate(chips)]
            return local, remote

        def start():
            local, remote = copies()
            for cp in local + remote:
                cp.start()

        def finish():
            local, remote = copies()
            for cp in remote + local:
                cp.wait()

        return start, None, finish

    scratch = [pltpu.SemaphoreType.DMA((3 * n,)), pltpu.SemaphoreType.DMA((3 * n,)), pltpu.SemaphoreType.DMA((n,))]
    return _Comm(ps, [jax.ShapeDtypeStruct(p.shape, p.dtype) for p in ps], scratch, phases)


def _share_comm(rs, l):
    n = len(rs)

    def phases(ins, outs, sems):
        send_sems, recv_sems = sems
        x, y, c = _position()

        def copy(i, h):
            return pltpu.make_async_remote_copy(
                src_ref=outs[i].at[l, h], dst_ref=outs[i].at[l, h], send_sem=send_sems.at[i], recv_sem=recv_sems.at[i],
                device_id=(x, y, 1 - c), device_id_type=MESH)

        def start():
            for i in range(n):
                copy(i, c).start()

        def finish():
            for i in range(n):
                copy(i, 1 - c).wait_recv()
            for i in range(n):
                copy(i, c).wait_send()

        return start, None, finish

    return _Comm(rs, [jax.ShapeDtypeStruct(r.shape, r.dtype) for r in rs],
                 [pltpu.SemaphoreType.DMA((n,)), pltpu.SemaphoreType.DMA((n,))], phases,
                 aliases={i: i for i in range(n)})


def _all_gather_comm(block):
    def phases(ins, outs, sems):
        send_sems, recv_sems, local_sem = sems
        (src,), (out,) = ins, outs
        x, y, c = _position()
        sibling = (x, y, 1 - c)
        chips = [(1 - x, y), (x, 1 - y), (1 - x, 1 - y)]

        def slot(px, py, pc):
            return out.at[4 * px + 2 * py + pc]

        def copy(k, blk, to, own=False):
            return pltpu.make_async_remote_copy(
                src_ref=src if own else slot(*blk), dst_ref=slot(*blk),
                send_sem=send_sems.at[k], recv_sem=recv_sems.at[k], device_id=to, device_id_type=MESH)

        mine = lambda: pltpu.make_async_copy(src, slot(x, y, c), local_sem.at[0])
        first = lambda: [copy(0, (x, y, c), sibling, True)] + [
            copy(1 + j, (x, y, c), (*chip, c), True) for j, chip in enumerate(chips)]
        passed = lambda: [copy(4 + j, (*chip, c), sibling) for j, chip in enumerate(chips)]

        def start():
            mine().start()
            for cp in first():
                cp.start()

        def mid():
            for j, (chip, p) in enumerate(zip(chips, passed())):
                copy(1 + j, (*chip, c), (x, y, c)).wait_recv()
                p.start()

        def finish():
            copy(0, sibling, (x, y, c)).wait_recv()
            for j, chip in enumerate(chips):
                copy(4 + j, (*chip, 1 - c), (x, y, c)).wait_recv()
            for cp in first() + passed():
                cp.wait_send()
            mine().wait()

        return start, mid, finish

    scratch = [pltpu.SemaphoreType.DMA((7,)), pltpu.SemaphoreType.DMA((7,)), pltpu.SemaphoreType.DMA((1,))]
    return _Comm([block], [jax.ShapeDtypeStruct((N_DEV,) + block.shape, block.dtype)], scratch, phases)


def _ffn_up(x, gain, sh, sc, wgu, comm=None):
    T, D = x.shape
    F = wgu.shape[1] // 2
    B = sh.shape[0]
    tm = _tile(TOKEN_TILE, T // B)
    tps = (T // B) // tm
    slabs = _slabs(F, FF_SLAB)

    def body(x_ref, gain_ref, sh_ref, sc_ref, w_ref, gu_ref, a_ref):
        xh, _ = _rms(x_ref[...])
        h = (xh * gain_ref[...] * (1.0 + sc_ref[0]) + sh_ref[0]).astype(BF16)

        def dots(s):
            return _dot(h, w_ref[:, s]), _dot(h, w_ref[:, slice(F + s.start, F + s.stop)])

        nxt = dots(slabs[0])
        for j, s in enumerate(slabs):
            g, u = nxt
            if j + 1 < len(slabs):
                nxt = dots(slabs[j + 1])
            gu_ref[0, :, s] = g.astype(BF16)
            gu_ref[1, :, s] = u.astype(BF16)
            a_ref[:, s] = (g * _sigmoid(g) * u).astype(BF16)

    seq = lambda i: (i // tps, 0, 0)
    return _call(
        body, "ffn_up", (T // tm,),
        [
            pl.BlockSpec((tm, D), lambda i: (i, 0)),
            pl.BlockSpec((1, D), lambda i: (0, 0)),
            pl.BlockSpec((1, 1, D), seq),
            pl.BlockSpec((1, 1, D), seq),
            pl.BlockSpec((D, 2 * F), lambda i: (0, 0), pipeline_mode=pl.Buffered(1)),
        ],
        [
            pl.BlockSpec((2, tm, F), lambda i: (0, i, 0)),
            pl.BlockSpec((tm, F), lambda i: (i, 0)),
        ],
        [
            jax.ShapeDtypeStruct((2, T, F), BF16),
            jax.ShapeDtypeStruct((T, F), BF16),
        ],
        [],
        (x, gain, sh, sc, wgu), comm)


def _ffn_down(a, x, gate, wd, comm=None):
    T, F = a.shape
    D = x.shape[1]
    B = gate.shape[0]
    tm = _tile(2 * TOKEN_TILE, T // B)
    tps = (T // B) // tm

    def body(a_ref, x_ref, gate_ref, wd_ref, xo_ref, f_ref):
        f = _dot(a_ref[...], wd_ref[...])
        f_ref[...] = f.astype(BF16)
        xo_ref[...] = x_ref[...] + 0.5 * gate_ref[0] * f

    return _call(
        body, "ffn_down", (T // tm,),
        [
            pl.BlockSpec((tm, F), lambda i: (i, 0)),
            pl.BlockSpec((tm, D), lambda i: (i, 0)),
            pl.BlockSpec((1, 1, D), lambda i: (i // tps, 0, 0)),
            pl.BlockSpec((F, D), lambda i: (0, 0), pipeline_mode=pl.Buffered(1)),
        ],
        [pl.BlockSpec((tm, D), lambda i: (i, 0)), pl.BlockSpec((tm, D), lambda i: (i, 0))],
        [jax.ShapeDtypeStruct((T, D), F32), jax.ShapeDtypeStruct((T, D), BF16)],
        [],
        (a, x, gate, wd), comm)


def _ffn_bwd(dxo, x, gu, f, gain, sh, sc, gate, wgu, wd, comm=None):
    T, D = x.shape
    F = wd.shape[0]
    B = sc.shape[0]
    tm = _tile(BWD_TILE, T // B)
    tps = (T // B) // tm
    slabs = _slabs(F, FF_SLAB)

    def body(dxo_ref, x_ref, gu_ref, f_ref, gain_ref, sh_ref, sc_ref, gate_ref, w_ref, wd_ref,
             dx_ref, dgu_ref, h_ref, df_ref, dsc_ref, dsh_ref, dgain_ref, dgate_ref):
        i = pl.program_id(0)
        first_of_seq = (i % tps) == 0
        gain = gain_ref[...]
        sc = sc_ref[0]
        dxo = dxo_ref[...]
        x = x_ref[...]
        df = (0.5 * gate_ref[0] * dxo).astype(BF16)
        df_ref[...] = df
        nxt = _dot_nt(df, wd_ref[slabs[0], :])
        for j, s in enumerate(slabs):
            da = nxt
            if j + 1 < len(slabs):
                nxt = _dot_nt(df, wd_ref[slabs[j + 1], :])
            g = gu_ref[0, :, s]
            sg = 1.0 / (1.0 + jnp.exp(-g))
            t = g * sg
            dab = da.astype(BF16)
            dgu_ref[1, :, s] = dab * t
            dgu_ref[0, :, s] = dab * gu_ref[1, :, s] * (sg + t - t * sg)
        dh = _dot_nt(dgu_ref[0], w_ref[:, 0:F]) + _dot_nt(dgu_ref[1], w_ref[:, F:])
        dx, dsc, dsh, dgain = _norm_mod_bwd(x, dh, gain, sc)
        dx_ref[...] = dxo + dx
        h_ref[...] = (_rms(x)[0] * gain * (1.0 + sc) + sh_ref[0]).astype(BF16)
        _acc(dsc_ref.at[0], first_of_seq, dsc)
        _acc(dsh_ref.at[0], first_of_seq, dsh)
        _acc(dgain_ref, i == 0, dgain)
        _acc(dgate_ref.at[0], first_of_seq, 0.5 * jnp.sum(dxo * f_ref[...].astype(F32), axis=0, keepdims=True))

    seq = lambda i: (i // tps, 0, 0)
    row = lambda i: (i, 0)
    return _call(
        body, "ffn_bwd", (T // tm,),
        [
            pl.BlockSpec((tm, D), row),
            pl.BlockSpec((tm, D), row),
            pl.BlockSpec((2, tm, F), lambda i: (0, i, 0)),
            pl.BlockSpec((tm, D), row),
            pl.BlockSpec((1, D), lambda i: (0, 0)),
            pl.BlockSpec((1, 1, D), seq),
            pl.BlockSpec((1, 1, D), seq),
            pl.BlockSpec((1, 1, D), seq),
            pl.BlockSpec((D, 2 * F), lambda i: (0, 0), pipeline_mode=pl.Buffered(1)),
            pl.BlockSpec((F, D), lambda i: (0, 0), pipeline_mode=pl.Buffered(1)),
        ],
        [
            pl.BlockSpec((tm, D), row),
            pl.BlockSpec((2, tm, F), lambda i: (0, i, 0)),
            pl.BlockSpec((tm, D), row),
            pl.BlockSpec((tm, D), row),
            pl.BlockSpec((1, 1, D), seq),
            pl.BlockSpec((1, 1, D), seq),
            pl.BlockSpec((1, D), lambda i: (0, 0)),
            pl.BlockSpec((1, 1, D), seq),
        ],
        [
            jax.ShapeDtypeStruct((T, D), F32),
            jax.ShapeDtypeStruct((2, T, F), BF16),
            jax.ShapeDtypeStruct((T, D), BF16),
            jax.ShapeDtypeStruct((T, D), BF16),
            jax.ShapeDtypeStruct((B, 1, D), F32),
            jax.ShapeDtypeStruct((B, 1, D), F32),
            jax.ShapeDtypeStruct((1, D), F32),
            jax.ShapeDtypeStruct((B, 1, D), F32),
        ],
        [],
        (dxo, x, gu, f, gain, sh, sc, gate, wgu, wd), comm)


def _wgrad(a, b, tmm, tn, col_major, name, tokens=WGRAD_TOKENS, comm=None):
    T, M = a.shape
    nb, _, Nb = b.shape
    N = nb * Nb
    tk = _tile(tokens, T)
    span = 2 if col_major else 1
    wide = span * tn
    npb = Nb // wide
    assert M % tmm == 0 and Nb % wide == 0
    if col_major:
        assert tmm == M
        shape = (N // tn, 2, M // 2, tn)
        out_spec = pl.BlockSpec((span, 2, M // 2, tn), lambda i, j, t: (j, 0, 0, 0))
    else:
        shape = (M // tmm, tmm, N)
        out_spec = pl.BlockSpec((None, tmm, tn), lambda i, j, t: (i, 0, j))

    def body(a_ref, b_ref, o_ref):
        @pl.when(pl.program_id(2) == 0)
        def _():
            o_ref[...] = jnp.zeros_like(o_ref)

        res = _dot_tn(a_ref[...], b_ref[...])
        if col_major:
            for s in range(span):
                for h in range(2):
                    o_ref[s, h] += res[h * (M // 2):(h + 1) * (M // 2), s * tn:(s + 1) * tn]
        else:
            o_ref[...] += res

    return _call(
        body, name, (M // tmm, N // wide, T // tk),
        [
            pl.BlockSpec((tk, tmm), lambda i, j, t: (t, i)),
            pl.BlockSpec((None, tk, wide), lambda i, j, t: (j // npb, t, j % npb)),
        ],
        [out_spec], [jax.ShapeDtypeStruct(shape, F32)], [],
        (a, b), comm)


def _mixin_fwd(x, gain, sh, sc, win, comm=None):
    T, D = x.shape
    P = win.shape[1]
    B = sh.shape[0]
    tm = _tile(TOKEN_TILE, T // B)
    tps = (T // B) // tm

    def body(x_ref, gain_ref, sh_ref, sc_ref, w_ref, proj_ref, h_ref):
        xh, _ = _rms(x_ref[...])
        h = (xh * gain_ref[...] * (1.0 + sc_ref[0]) + sh_ref[0]).astype(BF16)
        h_ref[...] = h
        proj_ref[...] = _dot(h, w_ref[...])

    seq = lambda i: (i // tps, 0, 0)
    return _call(
        body, "mixin_fwd", (T // tm,),
        [
            pl.BlockSpec((tm, D), lambda i: (i, 0)),
            pl.BlockSpec((1, D), lambda i: (0, 0)),
            pl.BlockSpec((1, 1, D), seq),
            pl.BlockSpec((1, 1, D), seq),
            pl.BlockSpec((D, P), lambda i: (0, 0)),
        ],
        [pl.BlockSpec((tm, P), lambda i: (i, 0)), pl.BlockSpec((tm, D), lambda i: (i, 0))],
        [jax.ShapeDtypeStruct((T, P), F32), jax.ShapeDtypeStruct((T, D), BF16)],
        [],
        (x, gain, sh, sc, win), comm)


def _mixin_bwd(dxo, x, dproj, gain, sc, win, comm=None):
    T, D = x.shape
    P = win.shape[1]
    B = sc.shape[0]
    tm = _tile(TOKEN_TILE, T // B)
    tps = (T // B) // tm

    def body(dxo_ref, x_ref, dp_ref, gain_ref, sc_ref, w_ref, dx_ref, dsc_ref, dsh_ref, dgain_ref):
        i = pl.program_id(0)
        first_of_seq = (i % tps) == 0
        halves = _slabs(tm, tm // 2)
        nxt = _dot_nt(dp_ref[halves[0], :], w_ref[...])
        sums = None
        for j, r in enumerate(halves):
            dh = nxt
            if j + 1 < len(halves):
                nxt = _dot_nt(dp_ref[halves[j + 1], :], w_ref[...])
            part = _norm_mod_bwd(x_ref[r, :], dh, gain_ref[...], sc_ref[0])
            dx_ref[r, :] = dxo_ref[r, :] + part[0]
            sums = part[1:] if sums is None else tuple(a + b for a, b in zip(sums, part[1:]))
        _acc(dsc_ref.at[0], first_of_seq, sums[0])
        _acc(dsh_ref.at[0], first_of_seq, sums[1])
        _acc(dgain_ref, i == 0, sums[2])

    seq = lambda i: (i // tps, 0, 0)
    row = lambda i: (i, 0)
    return _call(
        body, "mixin_bwd", (T // tm,),
        [
            pl.BlockSpec((tm, D), row),
            pl.BlockSpec((tm, D), row),
            pl.BlockSpec((tm, P), row),
            pl.BlockSpec((1, D), lambda i: (0, 0)),
            pl.BlockSpec((1, 1, D), seq),
            pl.BlockSpec((D, P), lambda i: (0, 0)),
        ],
        [
            pl.BlockSpec((tm, D), row),
            pl.BlockSpec((1, 1, D), seq),
            pl.BlockSpec((1, 1, D), seq),
            pl.BlockSpec((1, D), lambda i: (0, 0)),
        ],
        [
            jax.ShapeDtypeStruct((T, D), F32),
            jax.ShapeDtypeStruct((B, 1, D), F32),
            jax.ShapeDtypeStruct((B, 1, D), F32),
            jax.ShapeDtypeStruct((1, D), F32),
        ],
        [],
        (dxo, x, dproj, gain, sc, win), comm)


def _head_mean(z, pmat, exact=True):
    hi = z.astype(BF16)
    if not exact:
        return _dot(hi, pmat)
    lo = (z - hi.astype(F32)).astype(BF16)
    return _dot(hi, pmat) + _dot(lo, pmat)


def _gelu_parts(x):
    cdf = 0.5 * (1.0 + lax.erf(x * (1.0 / math.sqrt(2.0))))
    return x * cdf, cdf


def _gelu_grad(x, cdf):
    return cdf + x * jnp.exp(-0.5 * x * x) * (1.0 / math.sqrt(2.0 * math.pi))


LANES = 128


def _head_blocks(da):
    hd = da // N_HEADS
    lb = min(LANES, da)
    col = lax.broadcasted_iota(jnp.int32, (1, lb), 1)
    return lb, lb // hd, da // lb, [(col >= h * hd) & (col < (h + 1) * hd) for h in range(lb // hd)]


def _mix_heads(w_stack, v, da):
    lb, hpb, nb, masks = _head_blocks(da)
    outs = []
    for b in range(nb):
        res = _dot(w_stack[b * hpb * CHUNK:(b + 1) * hpb * CHUNK], v[:, b * lb:(b + 1) * lb])
        out = res[0:CHUNK]
        for h in range(1, hpb):
            out = jnp.where(masks[h], res[h * CHUNK:(h + 1) * CHUNK], out)
        outs.append(out)
    return outs[0] if nb == 1 else jnp.concatenate(outs, axis=1)


def _mix_heads_grad(dm, v, da):
    lb, hpb, nb, masks = _head_blocks(da)
    outs = []
    for b in range(nb):
        dmb = dm[:, b * lb:(b + 1) * lb]
        stack = jnp.concatenate([jnp.where(masks[h], dmb, jnp.zeros_like(dmb)) for h in range(hpb)], axis=0)
        outs.append(_dot_nt(stack, v[:, b * lb:(b + 1) * lb]))
    return outs[0] if nb == 1 else jnp.concatenate(outs, axis=0)


def _causal_stack(w, transposed):
    r = lax.broadcasted_iota(jnp.int32, w.shape, 0) % CHUNK
    c = lax.broadcasted_iota(jnp.int32, w.shape, 1)
    keep = (c >= r) if transposed else (c <= r)
    return jnp.where(keep, w, 0.0)


def _mix_core_forward(proj, zprev, prm, da, db, saved=None):
    n = proj.shape[0]
    ua = proj[:, 0:da]
    va = proj[:, da:2 * da]
    bg = proj[:, 2 * da:2 * da + db]
    cg = proj[:, 2 * da + db:2 * da + 2 * db]
    xb = proj[:, 2 * da + 2 * db:]
    if saved is None:
        ug, ucdf = _gelu_parts(ua)
        vg, vcdf = _gelu_parts(va)
        zc = vg - _head_mean(vg, prm["pmat"])
        rs = lax.rsqrt(_head_mean(zc * zc, prm["pmat"], exact=False) + EPS)
        vhat = zc * rs
        vln = (vhat * prm["lng"] + prm["lnb"]).astype(BF16)
        wst = _causal_stack(prm["wst"], False).astype(BF16)
        mixed = [_mix_heads(wst, vln[j * CHUNK:(j + 1) * CHUNK], da) + prm["bias"] for j in range(n // CHUNK)]
        mixed = mixed[0] if len(mixed) == 1 else jnp.concatenate(mixed, axis=0)
    else:
        ucdf, vcdf, vhat, rs, mixed = [saved[k].astype(F32) for k in range(5)]
        ug = ua * ucdf
        vln = (vhat * prm["lng"] + prm["lnb"]).astype(BF16)
    ya = ug * mixed
    z = cg * xb
    row = lax.broadcasted_iota(jnp.int32, z.shape, 0)
    z1 = jnp.where(row == 0, zprev[7:8], pltpu.roll(z, 1, 0))
    z2 = jnp.where(row == 0, zprev[6:7], jnp.where(row == 1, zprev[7:8], pltpu.roll(z, 2, 0)))
    cw = prm["convw"]
    conv = z2 * cw[0:1] + z1 * cw[1:2] + z * cw[2:3]
    yb = bg * conv
    yah, ra = _rms(ya)
    ybh, rb = _rms(yb)
    return dict(ua=ua, va=va, bg=bg, cg=cg, xb=xb, ug=ug, ucdf=ucdf, vcdf=vcdf, rs=rs, vhat=vhat, vln=vln,
                mixed=mixed, z=z, z1=z1, z2=z2, conv=conv, yah=yah, ra=ra, ybh=ybh, rb=rb)


def _mix_params(lng_ref, lnb_ref, wst_ref, bias_ref, pmat_ref, convw_ref):
    return dict(lng=lng_ref[...], lnb=lnb_ref[...], wst=wst_ref[...], bias=bias_ref[...],
                pmat=pmat_ref[...], convw=convw_ref[...])


def _mix_core_fwd(proj, x, gate, wout, lng, lnb, wst, bias, pmat, convw, og, comm=None):
    T, P = proj.shape
    D = x.shape[1]
    B = gate.shape[0]
    da = lng.shape[1]
    db = convw.shape[1]
    tm = _tile(MIX_TILE, T // B)
    tps = (T // B) // tm

    def body(proj_ref, x_ref, gate_ref, wout_ref, lng_ref, lnb_ref, wst_ref, bias_ref, pmat_ref, convw_ref,
             og_ref, xo_ref, yn_ref, sv_ref, halo):
        i = pl.program_id(0)

        @pl.when((i % tps) == 0)
        def _():
            halo[...] = jnp.zeros_like(halo)

        prm = _mix_params(lng_ref, lnb_ref, wst_ref, bias_ref, pmat_ref, convw_ref)
        r = _mix_core_forward(proj_ref[...], halo[...], prm, da, db)
        halo[...] = r["z"][tm - 8:tm]
        for k, name in enumerate(("ucdf", "vcdf", "vhat", "rs", "mixed")):
            sv_ref[k] = r[name].astype(BF16)
        og = og_ref[...]
        yn_ref[:, 0:da] = (r["yah"] * og[:, 0:da]).astype(BF16)
        yn_ref[:, da:] = (r["ybh"] * og[:, da:]).astype(BF16)
        xo_ref[...] = x_ref[...] + gate_ref[0] * _dot(yn_ref[...], wout_ref[...])

    full = lambda a: pl.BlockSpec(a.shape, lambda i: (0,) * a.ndim)
    return _call(
        body, "mix_core_fwd", (T // tm,),
        [
            pl.BlockSpec((tm, P), lambda i: (i, 0)),
            pl.BlockSpec((tm, D), lambda i: (i, 0)),
            pl.BlockSpec((1, 1, D), lambda i: (i // tps, 0, 0)),
            full(wout), full(lng), full(lnb), full(wst), full(bias), full(pmat), full(convw), full(og),
        ],
        [pl.BlockSpec((tm, D), lambda i: (i, 0)), pl.BlockSpec((tm, D), lambda i: (i, 0)),
         pl.BlockSpec((5, tm, da), lambda i: (0, i, 0))],
        [jax.ShapeDtypeStruct((T, D), F32), jax.ShapeDtypeStruct((T, D), BF16),
         jax.ShapeDtypeStruct((5, T, da), BF16)],
        [pltpu.VMEM((8, db), F32)],
        (proj, x, gate, wout, lng, lnb, wst, bias, pmat, convw, og), comm)


def _mix_core_bwd(proj, sv, dxo, gate, wout, lng, lnb, wstt, pmat, convw, og, comm=None):
    T, P = proj.shape
    D = dxo.shape[1]
    B = gate.shape[0]
    da = lng.shape[1]
    db = convw.shape[1]
    assert da == db and P == 2 * da + 3 * db
    tm = _tile(MIX_TILE, T // B)
    tps = (T // B) // tm
    nt = T // tm
    hd = da // N_HEADS

    def body(proj_ref, cgp_ref, xbp_ref, sv_ref, dxo_ref, gate_ref, wout_ref, lng_ref, lnb_ref, wstt_ref,
             pmat_ref, convw_ref, og_ref,
             dproj_ref, do_ref, dgate_ref, dog_ref, dwst_ref, dbias_ref, dlng_ref, dlnb_ref, dconvw_ref, carry):
        i = pl.program_id(0)
        ri = nt - 1 - i
        first = i == 0
        end_of_seq = (ri % tps) == tps - 1
        start_of_seq = (ri % tps) == 0

        @pl.when(end_of_seq)
        def _():
            carry[...] = jnp.zeros_like(carry)

        prm = dict(lng=lng_ref[...], lnb=lnb_ref[...], pmat=pmat_ref[...], convw=convw_ref[...])
        zprev = jnp.where(start_of_seq, 0.0, cgp_ref[...] * xbp_ref[...])
        r = _mix_core_forward(proj_ref[...], zprev, prm, da, db, saved=sv_ref)
        og = og_ref[...]
        pmat = prm["pmat"]

        yn = jnp.concatenate([(r["yah"] * og[:, 0:da]).astype(BF16), (r["ybh"] * og[:, da:]).astype(BF16)], axis=1)
        dxo = dxo_ref[...]
        o = _dot(yn, wout_ref[...])
        _acc(dgate_ref.at[0], end_of_seq, jnp.sum(dxo * o, axis=0, keepdims=True))
        d_o = (gate_ref[0] * dxo).astype(BF16)
        do_ref[...] = d_o
        dyn = _dot_nt(d_o, wout_ref[...])

        def rms_bwd(dyn_g, yh, rr, og_g):
            dog_g = jnp.sum(dyn_g * yh, axis=0, keepdims=True)
            dyh = dyn_g * og_g
            return rr * (dyh - yh * jnp.mean(dyh * yh, axis=-1, keepdims=True)), dog_g

        dya, dog_a = rms_bwd(dyn[:, 0:da], r["yah"], r["ra"], og[:, 0:da])
        dyb, dog_b = rms_bwd(dyn[:, da:], r["ybh"], r["rb"], og[:, da:])
        _acc(dog_ref, first, jnp.concatenate([dog_a, dog_b], axis=1))

        dug = dya * r["mixed"]
        dmixed = dya * r["ug"]
        wstt_b = _causal_stack(wstt_ref[...], True).astype(BF16)
        dbias = jnp.zeros((CHUNK, da), F32)
        dwst = jnp.zeros((N_HEADS * CHUNK, CHUNK), F32)
        dvln = []
        for j in range(tm // CHUNK):
            dm = dmixed[j * CHUNK:(j + 1) * CHUNK]
            dbias = dbias + dm
            dmb = dm.astype(BF16)
            dwst = dwst + _mix_heads_grad(dmb, r["vln"][j * CHUNK:(j + 1) * CHUNK], da)
            dvln.append(_mix_heads(wstt_b, dmb, da))
        dvln = dvln[0] if len(dvln) == 1 else jnp.concatenate(dvln, axis=0)
        _acc(dbias_ref, first, dbias)
        _acc(dwst_ref, first, dwst)
        _acc(dlng_ref, first, jnp.sum(dvln * r["vhat"], axis=0, keepdims=True))
        _acc(dlnb_ref, first, jnp.sum(dvln, axis=0, keepdims=True))
        dvhat = dvln * prm["lng"]
        dvg = r["rs"] * (dvhat - _head_mean(dvhat, pmat, exact=False)
                         - r["vhat"] * _head_mean(dvhat * r["vhat"], pmat, exact=False))
        dproj_ref[:, 0:da] = (dug * _gelu_grad(r["ua"], r["ucdf"])).astype(BF16)
        dproj_ref[:, da:2 * da] = (dvg * _gelu_grad(r["va"], r["vcdf"])).astype(BF16)

        dproj_ref[:, 2 * da:2 * da + db] = (dyb * r["conv"]).astype(BF16)
        dconv = dyb * r["bg"]
        dcw = jnp.concatenate([
            jnp.sum(dconv * r["z2"], axis=0, keepdims=True),
            jnp.sum(dconv * r["z1"], axis=0, keepdims=True),
            jnp.sum(dconv * r["z"], axis=0, keepdims=True),
            jnp.zeros((5, db), F32)], axis=0)
        _acc(dconvw_ref, first, dcw)
        nxt = carry[...]
        row = lax.broadcasted_iota(jnp.int32, dconv.shape, 0)
        dc1 = jnp.where(row == tm - 1, nxt[0:1], pltpu.roll(dconv, tm - 1, 0))
        dc2 = jnp.where(row == tm - 2, nxt[0:1], jnp.where(row == tm - 1, nxt[1:2], pltpu.roll(dconv, tm - 2, 0)))
        carry[...] = dconv[0:8]
        cw = prm["convw"]
        dz = dconv * cw[2:3] + dc1 * cw[1:2] + dc2 * cw[0:1]
        dproj_ref[:, 2 * da + db:2 * da + 2 * db] = (dz * r["xb"]).astype(BF16)
        dproj_ref[:, 2 * da + 2 * db:] = (dz * r["cg"]).astype(BF16)

        @pl.when(i == nt - 1)
        def _():
            dwst_ref[...] = _causal_stack(dwst_ref[...], False)
            dbias_ref[...] = _head_mean(dbias_ref[...], pmat) * float(hd)

    full = lambda a: pl.BlockSpec(a.shape, lambda i: (0,) * a.ndim)
    const = lambda i: (0, 0)
    rev = lambda i: (nt - 1 - i, 0)
    prev8 = lambda col: (lambda i: (jnp.maximum((nt - 1 - i) * (tm // 8) - 1, 0), col))
    return _call(
        body, "mix_core_bwd", (nt,),
        [
            pl.BlockSpec((tm, P), rev),
            pl.BlockSpec((8, db), prev8((2 * da + db) // db)),
            pl.BlockSpec((8, db), prev8((2 * da + 2 * db) // db)),
            pl.BlockSpec((5, tm, da), lambda i: (0, nt - 1 - i, 0)),
            pl.BlockSpec((tm, D), rev),
            pl.BlockSpec((1, 1, D), lambda i: ((nt - 1 - i) // tps, 0, 0)),
            full(wout), full(lng), full(lnb), full(wstt), full(pmat), full(convw), full(og),
        ],
        [
            pl.BlockSpec((tm, P), rev),
            pl.BlockSpec((tm, D), rev),
            pl.BlockSpec((1, 1, D), lambda i: ((nt - 1 - i) // tps, 0, 0)),
            pl.BlockSpec((1, D), const),
            pl.BlockSpec((N_HEADS * CHUNK, CHUNK), const),
            pl.BlockSpec((CHUNK, da), const),
            pl.BlockSpec((1, da), const),
            pl.BlockSpec((1, da), const),
            pl.BlockSpec((8, db), const),
        ],
        [
            jax.ShapeDtypeStruct((T, P), BF16),
            jax.ShapeDtypeStruct((T, D), BF16),
            jax.ShapeDtypeStruct((B, 1, D), F32),
            jax.ShapeDtypeStruct((1, D), F32),
            jax.ShapeDtypeStruct((N_HEADS * CHUNK, CHUNK), F32),
            jax.ShapeDtypeStruct((CHUNK, da), F32),
            jax.ShapeDtypeStruct((1, da), F32),
            jax.ShapeDtypeStruct((1, da), F32),
            jax.ShapeDtypeStruct((8, db), F32),
        ],
        [pltpu.VMEM((8, db), F32)],
        (proj, proj, proj, sv, dxo, gate, wout, lng, lnb, wstt, pmat, convw, og), comm)


def _loss_head(x, target, gain):
    T, D = x.shape
    tm = _tile(TOKEN_TILE, T)

    def body(x_ref, t_ref, gain_ref, dx_ref, loss_ref, dgain_ref):
        first = pl.program_id(0) == 0
        xh, r = _rms(x_ref[...])
        gain = gain_ref[...]
        err = xh * gain - t_ref[...]
        _acc(loss_ref, first, jnp.zeros((8, 128), F32) + 0.5 * jnp.sum(err * err) / D)
        dout = err * (1.0 / D)
        _acc(dgain_ref, first, jnp.sum(dout * xh, axis=0, keepdims=True))
        dy = dout * gain
        dx_ref[...] = r * (dy - xh * jnp.mean(dy * xh, axis=-1, keepdims=True))

    return _call(
        body, "loss_head", (T // tm,),
        [
            pl.BlockSpec((tm, D), lambda i: (i, 0)),
            pl.BlockSpec((tm, D), lambda i: (i, 0)),
            pl.BlockSpec((1, D), lambda i: (0, 0)),
        ],
        [
            pl.BlockSpec((tm, D), lambda i: (i, 0)),
            pl.BlockSpec((8, 128), lambda i: (0, 0)),
            pl.BlockSpec((1, D), lambda i: (0, 0)),
        ],
        [
            jax.ShapeDtypeStruct((T, D), F32),
            jax.ShapeDtypeStruct((8, 128), F32),
            jax.ShapeDtypeStruct((1, D), F32),
        ],
        [],
        (x, target, gain))[0]


def _ada_fwd(c_all, ada_w, ada_b):
    n, D = c_all.shape
    L, _, sa = ada_w.shape
    tn = _tile(768, sa)

    def body(c_ref, w_ref, b_ref, act_ref, o_ref):
        c = c_ref[...]
        act = (c * _sigmoid(c)).astype(BF16)
        act_ref[...] = act
        o_ref[...] = _dot(act, w_ref[...].astype(BF16)) + b_ref[...]

    return _call(
        body, "ada_fwd", (L, sa // tn),
        [
            pl.BlockSpec((n, D), lambda l, j: (0, 0)),
            pl.BlockSpec((None, D, tn), lambda l, j: (l, 0, j)),
            pl.BlockSpec((None, 1, tn), lambda l, j: (l, 0, j)),
        ],
        [
            pl.BlockSpec((n, D), lambda l, j: (0, 0)),
            pl.BlockSpec((None, n, tn), lambda l, j: (l, 0, j)),
        ],
        [jax.ShapeDtypeStruct((n, D), BF16), jax.ShapeDtypeStruct((L, n, sa), F32)],
        [],
        (c_all, ada_w, ada_b))[0]


def _ada_bwd(c_act, d_ada, comm=None):
    n, D = c_act.shape
    L, _, sa = d_ada.shape
    tn = _tile(768, sa)

    def body(c_ref, d_ref, o_ref):
        o_ref[...] = _dot_tn(c_ref[...], d_ref[...])

    return _call(
        body, "ada_bwd", (L, sa // tn),
        [pl.BlockSpec((n, D), lambda l, j: (0, 0)), pl.BlockSpec((None, n, tn), lambda l, j: (l, 0, j))],
        [pl.BlockSpec((None, D, tn), lambda l, j: (l, 0, j))],
        [jax.ShapeDtypeStruct((L, D, sa), F32)],
        [],
        (c_act, d_ada), comm)


def _colsum(a):
    L, n, C = a.shape

    def body(a_ref, o_ref):
        o_ref[...] = jnp.sum(a_ref[...], axis=0, keepdims=True)

    return _call(
        body, "colsum", (L,),
        [pl.BlockSpec((None, n, C), lambda l: (l, 0, 0))],
        [pl.BlockSpec((None, 1, C), lambda l: (l, 0, 0))],
        [jax.ShapeDtypeStruct((L, 1, C), F32)],
        [],
        (a,))[0][0]


def _row_tile(rows, cols, nbuf):
    budget = VMEM_LIMIT // 3 // (2 * nbuf * 4 * cols)
    t = rows
    while t > max(budget, 8) and t % 2 == 0 and (t // 2) % 8 == 0:
        t //= 2
    return t


def _pair_sum(g, recv, core):
    n, _, R, C = g.shape
    tr = _row_tile(R, C, 3)

    def body(core_ref, g_ref, r_ref, o_ref):
        o_ref[...] = (g_ref[...] + r_ref[...]).astype(BF16)

    return pl.pallas_call(
        body,
        name="pair_sum",
        grid_spec=pltpu.PrefetchScalarGridSpec(
            num_scalar_prefetch=1,
            grid=(n, R // tr),
            in_specs=[
                pl.BlockSpec((None, None, tr, C), lambda i, r, core_ref: (i, core_ref[0], r, 0)),
                pl.BlockSpec((None, tr, C), lambda i, r, core_ref: (i, r, 0)),
            ],
            out_specs=pl.BlockSpec((None, tr, C), lambda i, r, core_ref: (i, r, 0)),
        ),
        out_shape=jax.ShapeDtypeStruct((n, R, C), BF16),
        compiler_params=pltpu.CompilerParams(dimension_semantics=("arbitrary", "arbitrary"),
                                             vmem_limit_bytes=VMEM_LIMIT),
    )(core, g, recv)


def _chip_sum(q, core, l, n_layers, prev):
    nq, R, C = q.shape
    tr = _row_tile(R, C, 4)

    def body(core_ref, q_ref, *rest):
        o_ref = rest[-1]
        s = q_ref[0].astype(F32)
        for j in range(1, nq):
            s = s + q_ref[j].astype(F32)
        o_ref[...] = s

    in_specs = [pl.BlockSpec((nq, tr, C), lambda r, core_ref: (0, r, 0))]
    args = [core, q]
    aliases = {}
    if prev is not None:
        in_specs.append(ANY)
        args.append(prev)
        aliases = {2: 0}
    return pl.pallas_call(
        body,
        name="chip_sum",
        grid_spec=pltpu.PrefetchScalarGridSpec(
            num_scalar_prefetch=1,
            grid=(R // tr,),
            in_specs=in_specs,
            out_specs=pl.BlockSpec((None, None, tr, C), lambda r, core_ref: (l, core_ref[0], r, 0)),
        ),
        out_shape=jax.ShapeDtypeStruct((n_layers, 2, R, C), F32),
        input_output_aliases=aliases,
        compiler_params=pltpu.CompilerParams(dimension_semantics=("arbitrary",), vmem_limit_bytes=VMEM_LIMIT),
    )(*args)


def _sum_blocks(a, n):
    M = a.shape[0] // n
    C = a.shape[1]

    def body(a_ref, o_ref):
        s = a_ref[0:M]
        for j in range(1, n):
            s = s + a_ref[j * M:(j + 1) * M]
        o_ref[...] = s

    return pl.pallas_call(
        body,
        name="sum_blocks",
        out_shape=jax.ShapeDtypeStruct((M, C), F32),
        compiler_params=pltpu.CompilerParams(vmem_limit_bytes=VMEM_LIMIT),
    )(a)


def _adamw(w, g, m, v, emit_grad=False):
    R, C = w.shape
    n_out = 4 if emit_grad else 3
    tr = _row_tile(R, C, 4 + n_out) if R % 8 == 0 else R

    def body(w_ref, g_ref, m_ref, v_ref, d_ref, nm_ref, nv_ref, *g_out):
        g = g_ref[...]
        m = ADAM_B1 * m_ref[...] + (1.0 - ADAM_B1) * g
        v = ADAM_B2 * v_ref[...] + (1.0 - ADAM_B2) * (g * g)
        m_hat = m / (1.0 - ADAM_B1 ** ADAM_STEP)
        v_hat = v / (1.0 - ADAM_B2 ** ADAM_STEP)
        d_ref[...] = -ADAM_LR * (m_hat / (jnp.sqrt(v_hat) + ADAM_EPS) + ADAM_WD * w_ref[...])
        nm_ref[...] = m
        nv_ref[...] = v
        if emit_grad:
            g_out[0][...] = g

    spec = pl.BlockSpec((tr, C), lambda i: (i, 0))
    return _call(body, "adamw", (R // tr,), [spec] * 4, [spec] * n_out, [jax.ShapeDtypeStruct((R, C), F32)] * n_out,
                 [], (w, g, m, v))[0]


def kernel(x, c, ada_w, ada_b, norm_ffn1_g, ffn1_w_gu, ffn1_w_down, norm_mix_g, mix_w_in, sgu_ln_g, sgu_ln_b, sgu_w_s, sgu_b, conv_w, out_norm_g, mix_w_out, norm_ffn2_g, ffn2_w_gu, ffn2_w_down, final_norm_g, loss_target, m_ada_w, m_ada_b, m_norm_ffn1_g, m_ffn1_w_gu, m_ffn1_w_down, m_norm_mix_g, m_mix_w_in, m_sgu_ln_g, m_sgu_ln_b, m_sgu_w_s, m_sgu_b, m_conv_w, m_out_norm_g, m_mix_w_out, m_norm_ffn2_g, m_ffn2_w_gu, m_ffn2_w_down, m_final_norm_g, v_ada_w, v_ada_b, v_norm_ffn1_g, v_ffn1_w_gu, v_ffn1_w_down, v_norm_mix_g, v_mix_w_in, v_sgu_ln_g, v_sgu_ln_b, v_sgu_w_s, v_sgu_b, v_conv_w, v_out_norm_g, v_mix_w_out, v_norm_ffn2_g, v_ffn2_w_gu, v_ffn2_w_down, v_final_norm_g):
    weights = dict(ada_w=ada_w, ada_b=ada_b, norm_ffn1_g=norm_ffn1_g, ffn1_w_gu=ffn1_w_gu, ffn1_w_down=ffn1_w_down,
                   norm_mix_g=norm_mix_g, mix_w_in=mix_w_in, sgu_ln_g=sgu_ln_g, sgu_ln_b=sgu_ln_b, sgu_w_s=sgu_w_s,
                   sgu_b=sgu_b, conv_w=conv_w, out_norm_g=out_norm_g, mix_w_out=mix_w_out, norm_ffn2_g=norm_ffn2_g,
                   ffn2_w_gu=ffn2_w_gu, ffn2_w_down=ffn2_w_down, final_norm_g=final_norm_g)
    m_in = dict(ada_w=m_ada_w, ada_b=m_ada_b, norm_ffn1_g=m_norm_ffn1_g, ffn1_w_gu=m_ffn1_w_gu,
                ffn1_w_down=m_ffn1_w_down, norm_mix_g=m_norm_mix_g, mix_w_in=m_mix_w_in, sgu_ln_g=m_sgu_ln_g,
                sgu_ln_b=m_sgu_ln_b, sgu_w_s=m_sgu_w_s, sgu_b=m_sgu_b, conv_w=m_conv_w, out_norm_g=m_out_norm_g,
                mix_w_out=m_mix_w_out, norm_ffn2_g=m_norm_ffn2_g, ffn2_w_gu=m_ffn2_w_gu, ffn2_w_down=m_ffn2_w_down,
                final_norm_g=m_final_norm_g)
    v_in = dict(ada_w=v_ada_w, ada_b=v_ada_b, norm_ffn1_g=v_norm_ffn1_g, ffn1_w_gu=v_ffn1_w_gu,
                ffn1_w_down=v_ffn1_w_down, norm_mix_g=v_norm_mix_g, mix_w_in=v_mix_w_in, sgu_ln_g=v_sgu_ln_g,
                sgu_ln_b=v_sgu_ln_b, sgu_w_s=v_sgu_w_s, sgu_b=v_sgu_b, conv_w=v_conv_w, out_norm_g=v_out_norm_g,
                mix_w_out=v_mix_w_out, norm_ffn2_g=v_norm_ffn2_g, ffn2_w_gu=v_ffn2_w_gu, ffn2_w_down=v_ffn2_w_down,
                final_norm_g=v_final_norm_g)

    B, S, D = x.shape
    T = B * S
    L = ada_w.shape[0]
    F = ffn1_w_down.shape[1] * N_CHIP
    P = mix_w_in.shape[2] * N_CHIP
    DA = D // 2
    DB = D - DA
    HD = DA // N_HEADS
    SA = ada_w.shape[2]
    n_all = B * N_DEV
    mx, my, mc = _position()
    chip = 2 * mx + my
    dev = 2 * chip + mc
    core = jnp.reshape(mc, (1,)).astype(jnp.int32)

    big = ["ffn1_w_gu", "ffn1_w_down", "mix_w_in", "mix_w_out", "ffn2_w_gu", "ffn2_w_down"]
    col_sharded = dict(ffn1_w_gu=True, ffn1_w_down=False, mix_w_in=True, mix_w_out=False,
                       ffn2_w_gu=True, ffn2_w_down=False)
    shards = {k: weights[k].astype(BF16) for k in big}
    gather = lambda l, *names: _gather_comm([(shards[k], l, col_sharded[k]) for k in names])
    full = [dict() for _ in range(L)]

    def arrived(l, names, res):
        full[l].update(zip(names, res))

    n_cw = L * conv_w.shape[1]
    cw_block = jnp.pad(conv_w.reshape(n_cw, conv_w.shape[2]), ((0, 8 - n_cw), (0, 0)))
    c_all, cw_all = _comm_call(_merge(_all_gather_comm(c.reshape(8, B * D // 8)), _all_gather_comm(cw_block)),
                               "gather_c")
    c_all = c_all.reshape(n_all, D)
    cw_all = cw_all.reshape(N_CHIP, 2, 8, conv_w.shape[2])[:, 0, :n_cw]
    conv_full = jnp.transpose(cw_all.reshape(N_CHIP, L, conv_w.shape[1], conv_w.shape[2]), (1, 2, 0, 3))
    conv_full = conv_full.reshape(L, conv_w.shape[1], DB)
    ada_b_mine = lax.dynamic_slice_in_dim(ada_b, chip * SA, SA, axis=1).reshape(L, 1, SA)
    c_act, ada_part = _ada_fwd(c_all, ada_w, ada_b_mine)
    ada_all, first_w = _comm_call(_merge(_all_gather_comm(ada_part.reshape(L * n_all, SA)), gather(0, big[0])),
                                  "gather_first")
    arrived(0, big[:1], [first_w])
    ada_all = ada_all.reshape(N_CHIP, 2, L, n_all, SA)[:, 0]
    ada_all = jnp.transpose(ada_all, (1, 2, 0, 3)).reshape(L, n_all, N_CHIP * SA)
    ada = lax.dynamic_slice_in_dim(ada_all, dev * B, B, axis=1).reshape(L, B, N_MOD, 1, D)
    mods = [[ada[l, :, j] for j in range(N_MOD)] for l in range(L)]

    x0 = x.reshape(T, D)
    gains = lambda name, l: weights[name][l].reshape(1, D)
    hmask = jnp.repeat(jnp.eye(N_HEADS, dtype=F32), HD, axis=0)
    pmat = (jnp.repeat(hmask, HD, axis=1) / HD).astype(BF16)

    def mix_consts(l):
        lng = jnp.tile(sgu_ln_g[l], N_HEADS).reshape(1, DA)
        lnb = jnp.tile(sgu_ln_b[l], N_HEADS).reshape(1, DA)
        wst = sgu_w_s[l].reshape(N_HEADS * CHUNK, CHUNK)
        wstt = jnp.swapaxes(sgu_w_s[l], 1, 2).reshape(N_HEADS * CHUNK, CHUNK)
        bias = jnp.repeat(jnp.transpose(sgu_b[l]), HD, axis=1)
        return lng, lnb, wst, wstt, bias

    def fetch(fn, *args, bring=()):
        bring = [(l, k) for l, k in bring if l < L]
        comm = _gather_comm([(shards[k], l, col_sharded[k]) for l, k in bring]) if bring else None
        res, got = fn(*args, comm)
        for (l, k), a in zip(bring, got):
            full[l][k] = a
        return res

    saved = []
    xc = x0
    for l in range(L):
        sh1, sc1, g1, sh2, sc2, g2, sh3, sc3, g3 = mods[l]
        lng, lnb, wst, wstt, bias = mix_consts(l)
        w = full[l]
        own = l == 0
        gu1, a1 = fetch(_ffn_up, xc, gains("norm_ffn1_g", l), sh1, sc1, w["ffn1_w_gu"],
                        bring=[(l, "ffn1_w_down"), (l, "mix_w_in"), (l, "mix_w_out")] if own else [(l, "ffn2_w_gu")])
        xa, f1 = fetch(_ffn_down, a1, xc, g1, w["ffn1_w_down"], bring=[(l, "ffn2_w_down")])
        proj, h2 = fetch(_mixin_fwd, xa, gains("norm_mix_g", l), sh2, sc2, w["mix_w_in"], bring=[(l + 1, "mix_w_in")])
        xb, yn, sv = fetch(_mix_core_fwd, proj, xa, g2, w["mix_w_out"], lng, lnb, wst, bias, pmat, conv_full[l],
                           gains("out_norm_g", l), bring=[(l, "ffn2_w_gu")] if own else [])
        gu2, a2 = fetch(_ffn_up, xb, gains("norm_ffn2_g", l), sh3, sc3, w["ffn2_w_gu"],
                        bring=[(l + 1, "ffn1_w_gu"), (l + 1, "mix_w_out")])
        xd, f2 = fetch(_ffn_down, a2, xb, g3, w["ffn2_w_down"], bring=[(l + 1, "ffn1_w_down")])
        saved.append(dict(x0=xc, xa=xa, xb=xb, gu1=gu1, a1=a1, f1=f1, proj=proj, h2=h2, yn=yn, sv=sv,
                          gu2=gu2, a2=a2, f2=f2))
        xc = xd

    dx, loss_block, d_final = _loss_head(xc, loss_target.reshape(T, D), final_norm_g.reshape(1, D))

    reduced = dict.fromkeys(big)

    def halves(name, g):
        if g.ndim == 4:
            return g
        return g.reshape(N_CHIP, 2, weights[name].shape[1] // 2, g.shape[-1])

    class Reduction:
        def __init__(self, l, name, g):
            self.l, self.name, self.g, self.stage = l, name, halves(name, g), 0
            self.ici_bytes = 3 * (g.size // 8) * 2

        def step(self):
            self.stage += 1
            if self.stage == 1:
                return _sibling_half_comm([self.g])
            if self.stage == 2:
                return _scatter_comm([_pair_sum(self.g, self.got[0], core)])
            if self.stage == 3:
                reduced[self.name] = _chip_sum(self.got[0], core, self.l, L, reduced[self.name])
                return _share_comm([reduced[self.name]], self.l)
            reduced[self.name] = self.got[0]
            return None

    active, extra, gathered = [], [], {}

    def carry(fn, *args, us=None):
        left = None if us is None else us * SCATTER_BYTES_PER_US
        riders = []
        for r in active:
            if r.stage == 1 and left is not None:
                if r.ici_bytes > left * SCATTER_OVERSHOOT:
                    continue
                left -= r.ici_bytes
            riders.append(r)
        comms = [r.step() for r in riders] + [cm for cm, _ in extra]
        takers = [functools.partial(setattr, r, "got") for r in riders] + [cb for _, cb in extra]
        extra.clear()
        if fn is None:
            res, got = None, (_comm_call(_merge(*comms), "reduce_alone") if comms else [])
        else:
            res, got = fn(*args, comm=_merge(*comms))
        at = 0
        for cm, take in zip(comms, takers):
            take(got[at:at + len(cm.out_shape)])
            at += len(cm.out_shape)
        for r in riders:
            if r.stage == 3:
                r.step()
                active.remove(r)
        return res

    def reduce_later(l, name, g):
        active.append(Reduction(l, name, g))

    small = [None] * L
    dwsts = [None] * L
    d_ada = [None] * L
    for l in reversed(range(L)):
        sh1, sc1, g1, sh2, sc2, g2, sh3, sc3, g3 = mods[l]
        lng, lnb, wst, wstt, bias = mix_consts(l)
        s = saved[l]
        w = full[l]
        last = l == 0
        dx, dgu2, h3, df2, dsc3, dsh3, dgain3, dg3 = carry(
            _ffn_bwd, dx, s["xb"], s["gu2"], s["f2"], gains("norm_ffn2_g", l), sh3, sc3, g3, w["ffn2_w_gu"],
            w["ffn2_w_down"], us=170)
        ffn2_grads = [
            lambda: reduce_later(l, "ffn2_w_gu", carry(_wgrad, h3, dgu2, D, 2 * F // N_CHIP, True, "wgrad_gu",
                                                       WGRAD_TOKENS // 2, us=110)[0]),
            lambda: reduce_later(l, "ffn2_w_down", carry(_wgrad, s["a2"], df2[None], F // 2, D, False, "wgrad_down",
                                                         us=50)[0])]
        if not last:
            ffn2_grads[0]()
            ffn2_grads[1]()
        dproj, d_o, dg2, dog, dwst, dbias, dlng, dlnb, dconvw = carry(
            _mix_core_bwd, s["proj"], s["sv"], dx, g2, w["mix_w_out"], lng, lnb, wstt, pmat, conv_full[l],
            gains("out_norm_g", l), us=150)
        mix_grads = [
            lambda: reduce_later(l, "mix_w_out", carry(_wgrad, s["yn"], d_o[None], D, D, False, "wgrad_out", us=30)[0]),
            lambda: reduce_later(l, "mix_w_in", carry(_wgrad, s["h2"], dproj[None], D, P // N_CHIP, True, "wgrad_in",
                                                      us=65)[0])]
        if not last:
            mix_grads[0]()
        dx, dsc2, dsh2, dgain2 = carry(_mixin_bwd, dx, s["xa"], dproj, gains("norm_mix_g", l), sc2, w["mix_w_in"], us=60)
        if not last:
            mix_grads[1]()
        dx, dgu, h1, df, dsc1, dsh1, dgain1, dg1 = carry(
            _ffn_bwd, dx, s["x0"], s["gu1"], s["f1"], gains("norm_ffn1_g", l), sh1, sc1, g1, w["ffn1_w_gu"],
            w["ffn1_w_down"], us=170)
        d_ada[l] = jnp.concatenate([dsh1, dsc1, dg1, dsh2, dsc2, dg2, dsh3, dsc3, dg3], axis=1).reshape(B, N_MOD * D)
        small[l] = [dgain1, dgain2, dgain3, dog, dlng, dlnb, dbias[:, ::HD], dconvw]
        dwsts[l] = dwst
        if last:
            flat = [a.reshape(-1, 128) for ll in range(L) for a in small[ll]]
            flat += [d_final.reshape(-1, 128), loss_block[0:1]]
            pad = (-sum(a.shape[0] for a in flat)) % 8
            packed = jnp.concatenate(flat + [jnp.zeros((pad, 128), F32)], axis=0)
            extra.append((_all_gather_comm(jnp.stack(d_ada).reshape(L * B, N_MOD * D)),
                          lambda got: gathered.update(d_ada=got[0])))
            extra.append((_all_gather_comm(packed), lambda got: gathered.update(small=got[0])))
            for ll in range(L):
                extra.append((_all_gather_comm(dwsts[ll]), lambda got, ll=ll: gathered.update({("dwst", ll): got[0]})))
        reduce_later(l, "ffn1_w_gu", carry(_wgrad, h1, dgu, D, 2 * F // N_CHIP, True, "wgrad_gu", WGRAD_TOKENS // 2,
                                           us=110)[0])
        reduce_later(l, "ffn1_w_down", carry(_wgrad, s["a1"], df[None], F // 2, D, False, "wgrad_down", us=50)[0])
        if last:
            ffn2_grads[0]()
            ffn2_grads[1]()
            mix_grads[1]()
            mix_grads[0]()
    grad_x = dx.reshape(B, S, D)

    def finished(name):
        while any(r.name == name for r in active):
            carry(None)
        return reduced[name].reshape(weights[name].shape)

    grads = {}
    d_ada_all = jnp.transpose(gathered["d_ada"].reshape(N_DEV, L, B, N_MOD * D), (1, 0, 2, 3))
    d_ada_all = d_ada_all.reshape(L, n_all, N_MOD * D)
    grads["ada_b"] = _colsum(d_ada_all).reshape(L, N_MOD * D)
    d_ada_mine = lax.dynamic_slice_in_dim(d_ada_all, chip * SA, SA, axis=2).astype(BF16)
    grads["ada_w"] = _ada_bwd(c_act, d_ada_mine)[0][0]

    total = _sum_blocks(gathered["small"].reshape(-1, 128), N_DEV)
    pieces, at = [], 0
    for a in flat:
        pieces.append(total[at:at + a.shape[0]])
        at += a.shape[0]
    per_layer = len(small[0])
    stack = lambda j, shape: jnp.stack([pieces[l * per_layer + j].reshape(shape) for l in range(L)])
    grads["norm_ffn1_g"] = stack(0, (D,))
    grads["norm_mix_g"] = stack(1, (D,))
    grads["norm_ffn2_g"] = stack(2, (D,))
    grads["out_norm_g"] = stack(3, (D,))
    grads["sgu_ln_g"] = stack(4, (N_HEADS, HD)).sum(axis=1)
    grads["sgu_ln_b"] = stack(5, (N_HEADS, HD)).sum(axis=1)
    grads["sgu_b"] = jnp.swapaxes(stack(6, (CHUNK, N_HEADS)), 1, 2)
    g_conv = stack(7, (8, DB))[:, :conv_w.shape[1]]
    grads["conv_w"] = lax.dynamic_slice_in_dim(g_conv, chip * conv_w.shape[2], conv_w.shape[2], axis=2)
    grads["final_norm_g"] = pieces[-2].reshape(D)
    loss = pieces[-1][0, 0]
    grads["sgu_w_s"] = jnp.stack([_sum_blocks(gathered["dwst", l].reshape(-1, CHUNK), N_DEV) for l in range(L)])
    grads["sgu_w_s"] = grads["sgu_w_s"].reshape(L, N_HEADS, CHUNK, CHUNK)

    names = list(weights)
    delta, new_m, new_v = {}, {}, {}
    for k in big:
        grads[k] = finished(k)
    for k in names:
        wk = weights[k]
        view = (1, wk.shape[0]) if wk.ndim == 1 else (-1, wk.shape[-1])
        d, nm, nv, *g_again = _adamw(wk.reshape(view), grads[k].reshape(view), m_in[k].reshape(view),
                                     v_in[k].reshape(view), emit_grad=k in big)
        delta[k], new_m[k], new_v[k] = d.reshape(wk.shape), nm.reshape(wk.shape), nv.reshape(wk.shape)
        if g_again:
            grads[k] = g_again[0].reshape(wk.shape)

    return (loss, grad_x, *[grads[k] for k in names], *[delta[k] for k in names],
            *[new_m[k] for k in names], *[new_v[k] for k in names])
```

```python
import functools
import math

import jax
import jax.numpy as jnp
from jax import lax
from jax.experimental import pallas as pl
from jax.experimental.pallas import tpu as pltpu

F32 = jnp.float32
BF16 = jnp.bfloat16
MESH = pl.DeviceIdType.MESH

N_HEADS = 8
CHUNK = 128
N_MOD = 9
EPS = 1e-6
N_DEV = 8
N_CHIP = 4

ADAM_LR = 0.001
ADAM_B1 = 0.9
ADAM_B2 = 0.999
ADAM_EPS = 1e-08
ADAM_WD = 0.01
ADAM_STEP = 10

TOKEN_TILE = 512
BWD_TILE = 256
FWD_TILE = 512
FF_SLAB = 768
MIX_TILE = 256
WGRAD_TOKENS = 2048
VMEM_LIMIT = 56 * 1024 * 1024

SCATTER_BYTES_PER_US = 68_000
SCATTER_OVERSHOOT = 1.25

ANY = pl.BlockSpec(memory_space=pl.ANY)


def _tile(pref, n):
    t = min(pref, n)
    assert n % t == 0, (pref, n)
    return t


def _slabs(n, width):
    return [slice(c0, min(c0 + width, n)) for c0 in range(0, n, width)]


def _dot(a, b):
    return jnp.dot(a, b, preferred_element_type=F32)


def _dot_nt(a, b):
    return lax.dot_general(a, b, (((1,), (1,)), ((), ())), preferred_element_type=F32)


def _dot_tn(a, b):
    return lax.dot_general(a, b, (((0,), (0,)), ((), ())), preferred_element_type=F32)


def _sigmoid(x):
    return 1.0 / (1.0 + jnp.exp(-x))


def _sigmoid_fast(x):
    return pl.reciprocal(1.0 + jnp.exp(-x), approx=True)


def _rms(x):
    r = lax.rsqrt(jnp.mean(x * x, axis=-1, keepdims=True) + EPS)
    return x * r, r


def _norm_mod_bwd(x, dh, gain, sc):
    xh, r = _rms(x)
    dsc = jnp.sum(dh * (xh * gain), axis=0, keepdims=True)
    dsh = jnp.sum(dh, axis=0, keepdims=True)
    dn = dh * (1.0 + sc)
    dgain = jnp.sum(dn * xh, axis=0, keepdims=True)
    dy = dn * gain
    dx = r * (dy - xh * jnp.mean(dy * xh, axis=-1, keepdims=True))
    return dx, dsc, dsh, dgain


def _acc(ref, first, val):
    @pl.when(first)
    def _():
        ref[...] = val

    @pl.when(jnp.logical_not(first))
    def _():
        ref[...] += val


class _Comm:
    def __init__(self, args, out_shape, scratch, phases, aliases=None):
        self.args, self.out_shape, self.scratch = list(args), list(out_shape), list(scratch)
        self.phases, self.aliases = phases, dict(aliases or {})


def _merge(*comms):
    comms = [c for c in comms if c is not None]
    if len(comms) <= 1:
        return comms[0] if comms else None
    args = [a for c in comms for a in c.args]
    out_shape = [o for c in comms for o in c.out_shape]
    scratch = [s for c in comms for s in c.scratch]
    aliases, ai, oi = {}, 0, 0
    for c in comms:
        aliases.update({ai + i: oi + o for i, o in c.aliases.items()})
        ai += len(c.args)
        oi += len(c.out_shape)

    def phases(ins, outs, sems):
        parts, ai, oi, si = [], 0, 0, 0
        for c in comms:
            parts.append(c.phases(ins[ai:ai + len(c.args)], outs[oi:oi + len(c.out_shape)], sems[si:si + len(c.scratch)]))
            ai, oi, si = ai + len(c.args), oi + len(c.out_shape), si + len(c.scratch)

        def run(k):
            def go():
                for p in parts:
                    if p[k] is not None:
                        p[k]()
            return go
        return run(0), run(1), run(2)

    return _Comm(args, out_shape, scratch, phases, aliases)


def _call(body, name, grid, in_specs, out_specs, out_shape, scratch, args, comm=None):
    n_in, n_out, n_scr = len(in_specs), len(out_specs), len(scratch)
    sem = ("arbitrary",) * len(grid)
    params = pltpu.CompilerParams(dimension_semantics=sem, vmem_limit_bytes=VMEM_LIMIT)
    if comm is None:
        res = pl.pallas_call(body, name=name, grid=grid, in_specs=in_specs, out_specs=out_specs, out_shape=out_shape,
                             scratch_shapes=scratch, compiler_params=params)(*args)
        return list(res), []
    m_in, m_out = len(comm.args), len(comm.out_shape)

    def full(*refs):
        c_in, c_min = refs[:n_in], refs[n_in:n_in + m_in]
        o = n_in + m_in
        c_out, c_mout = refs[o:o + n_out], refs[o + n_out:o + n_out + m_out]
        o += n_out + m_out
        c_scr, c_sem = refs[o:o + n_scr], refs[o + n_scr:]
        start, mid, finish = comm.phases(c_min, c_mout, c_sem)
        ids = [pl.program_id(a) for a in range(len(grid))]
        first = functools.reduce(jnp.logical_and, [i == 0 for i in ids])
        last = functools.reduce(jnp.logical_and, [i == g - 1 for i, g in zip(ids, grid)])
        pl.when(first)(start)
        if mid is not None:
            pl.when(last)(mid)
        body(*c_in, *c_out, *c_scr)
        pl.when(last)(finish)

    res = pl.pallas_call(
        full, name=name, grid=grid,
        in_specs=list(in_specs) + [ANY] * m_in,
        out_specs=list(out_specs) + [ANY] * m_out,
        out_shape=list(out_shape) + comm.out_shape,
        scratch_shapes=list(scratch) + comm.scratch,
        input_output_aliases={n_in + i: n_out + o for i, o in comm.aliases.items()},
        compiler_params=params,
    )(*args, *comm.args)
    return list(res[:n_out]), list(res[n_out:])


def _comm_call(comm, name):
    m_in, m_out = len(comm.args), len(comm.out_shape)

    def body(*refs):
        start, mid, finish = comm.phases(refs[:m_in], refs[m_in:m_in + m_out], refs[m_in + m_out:])
        start()
        if mid is not None:
            mid()
        finish()

    res = pl.pallas_call(
        body, name=name, in_specs=[ANY] * m_in, out_specs=[ANY] * m_out, out_shape=comm.out_shape,
        scratch_shapes=comm.scratch, input_output_aliases=comm.aliases,
    )(*comm.args)
    return list(res)


def _position():
    return lax.axis_index("x"), lax.axis_index("y"), lax.axis_index("c")


def _gather_comm(items):
    n = len(items)
    half = [s.shape[1] // 2 for s, _, _ in items]

    def full_shape(i):
        s, _, col = items[i]
        _, R, C = s.shape
        return jax.ShapeDtypeStruct((R, N_CHIP * C) if col else (N_CHIP * R, C), s.dtype)

    def phases(ins, outs, sems):
        send_sems, recv_sems, local_sems = sems
        x, y, c = _position()

        def region(i, chip, h):
            s, _, col = items[i]
            _, R, C = s.shape
            if col:
                return outs[i].at[pl.ds(h * half[i], half[i]), pl.ds(chip * C, C)]
            return outs[i].at[pl.ds(chip * R + h * half[i], half[i]), :]

        def mine(i, h):
            return ins[i].at[items[i][1], pl.ds(h * half[i], half[i]), :]

        def copies(kx, ky, kc):
            k_me = 2 * kx + ky
            sibling = (kx, ky, 1 - kc)
            chips = [(1 - kx, ky), (kx, 1 - ky), (1 - kx, 1 - ky)]
            local, first, passed, arrive_ici, arrive_d2d = [], [], [], [], []

            def remote(src, dst, s, to):
                return pltpu.make_async_remote_copy(src_ref=src, dst_ref=dst, send_sem=send_sems.at[s],
                                                    recv_sem=recv_sems.at[s], device_id=to, device_id_type=MESH)

            for i in range(n):
                for h in range(2):
                    local.append(pltpu.make_async_copy(mine(i, h), region(i, k_me, h), local_sems.at[2 * i + h]))
                for j, (px, py) in enumerate(chips):
                    s = 6 * i + j
                    first.append(remote(mine(i, kc), region(i, k_me, kc), s, (px, py, kc)))
                    got = region(i, 2 * px + py, kc)
                    arrive_ici.append(remote(got, got, s, (px, py, kc)))
                    passed.append(remote(got, got, s + 3, sibling))
                    other = region(i, 2 * px + py, 1 - kc)
                    arrive_d2d.append(remote(other, other, s + 3, sibling))
            return local, first, passed, arrive_ici, arrive_d2d

        def on_each_device(fn):
            def go():
                for kx in range(2):
                    for ky in range(2):
                        for kc in range(2):
                            pl.when((x == kx) & (y == ky) & (c == kc))(functools.partial(fn, *copies(kx, ky, kc)))
            return go

        def start(local, first, passed, arrive_ici, arrive_d2d):
            for cp in local + first:
                cp.start()

        def mid(local, first, passed, arrive_ici, arrive_d2d):
            for a, p in zip(arrive_ici, passed):
                a.wait_recv()
                p.start()

        def finish(local, first, passed, arrive_ici, arrive_d2d):
            for a in arrive_d2d:
                a.wait_recv()
            for cp in first + passed:
                cp.wait_send()
            for cp in local:
                cp.wait()

        return on_each_device(start), on_each_device(mid), on_each_device(finish)

    scratch = [pltpu.SemaphoreType.DMA((6 * n,)), pltpu.SemaphoreType.DMA((6 * n,)), pltpu.SemaphoreType.DMA((2 * n,))]
    return _Comm([s for s, _, _ in items], [full_shape(i) for i in range(n)], scratch, phases)


def _sibling_half_comm(gs):
    n = len(gs)

    def phases(ins, outs, sems):
        send_sems, recv_sems = sems
        x, y, c = _position()

        def copies():
            return [pltpu.make_async_remote_copy(
                src_ref=ins[i].at[:, 1 - c], dst_ref=outs[i], send_sem=send_sems.at[i], recv_sem=recv_sems.at[i],
                device_id=(x, y, 1 - c), device_id_type=MESH) for i in range(n)]

        def start():
            for cp in copies():
                cp.start()

        def finish():
            for cp in copies():
                cp.wait()

        return start, None, finish

    out_shape = [jax.ShapeDtypeStruct(g.shape[:1] + g.shape[2:], g.dtype) for g in gs]
    return _Comm(gs, out_shape, [pltpu.SemaphoreType.DMA((n,)), pltpu.SemaphoreType.DMA((n,))], phases)


def _scatter_comm(ps):
    n = len(ps)

    def phases(ins, outs, sems):
        send_sems, recv_sems, local_sems = sems
        x, y, c = _position()
        k_me = 2 * x + y
        chips = [(1 - x, y), (x, 1 - y), (1 - x, 1 - y)]

        def copies():
            local = [pltpu.make_async_copy(ins[i].at[k_me], outs[i].at[k_me], local_sems.at[i]) for i in range(n)]
            remote = [pltpu.make_async_remote_copy(
                src_ref=ins[i].at[2 * px + py], dst_ref=outs[i].at[k_me],
                send_sem=send_sems.at[3 * i + j], recv_sem=recv_sems.at[3 * i + j],
                device_id=(px, py, c), device_id_type=MESH) for i in range(n) for j, (px, py) in enumerate(chips)]
            return local, remote

        def start():
            local, remote = copies()
            for cp in local + remote:
                cp.start()

        def finish():
            local, remote = copies()
            for cp in remote + local:
                cp.wait()

        return start, None, finish

    scratch = [pltpu.SemaphoreType.DMA((3 * n,)), pltpu.SemaphoreType.DMA((3 * n,)), pltpu.SemaphoreType.DMA((n,))]
    return _Comm(ps, [jax.ShapeDtypeStruct(p.shape, p.dtype) for p in ps], scratch, phases)


def _share_comm(rs, l):
    n = len(rs)

    def phases(ins, outs, sems):
        send_sems, recv_sems = sems
        x, y, c = _position()

        def copy(i, h):
            return pltpu.make_async_remote_copy(
                src_ref=outs[i].at[l, h], dst_ref=outs[i].at[l, h], send_sem=send_sems.at[i], recv_sem=recv_sems.at[i],
                device_id=(x, y, 1 - c), device_id_type=MESH)

        def start():
            for i in range(n):
                copy(i, c).start()

        def finish():
            for i in range(n):
                copy(i, 1 - c).wait_recv()
            for i in range(n):
                copy(i, c).wait_send()

        return start, None, finish

    return _Comm(rs, [jax.ShapeDtypeStruct(r.shape, r.dtype) for r in rs],
                 [pltpu.SemaphoreType.DMA((n,)), pltpu.SemaphoreType.DMA((n,))], phases,
                 aliases={i: i for i in range(n)})


def _all_gather_comm(block):
    def phases(ins, outs, sems):
        send_sems, recv_sems, local_sem = sems
        (src,), (out,) = ins, outs
        x, y, c = _position()
        sibling = (x, y, 1 - c)
        chips = [(1 - x, y), (x, 1 - y), (1 - x, 1 - y)]

        def slot(px, py, pc):
            return out.at[4 * px + 2 * py + pc]

        def copy(k, blk, to, own=False):
            return pltpu.make_async_remote_copy(
                src_ref=src if own else slot(*blk), dst_ref=slot(*blk),
                send_sem=send_sems.at[k], recv_sem=recv_sems.at[k], device_id=to, device_id_type=MESH)

        mine = lambda: pltpu.make_async_copy(src, slot(x, y, c), local_sem.at[0])
        first = lambda: [copy(0, (x, y, c), sibling, True)] + [
            copy(1 + j, (x, y, c), (*chip, c), True) for j, chip in enumerate(chips)]
        passed = lambda: [copy(4 + j, (*chip, c), sibling) for j, chip in enumerate(chips)]

        def start():
            mine().start()
            for cp in first():
                cp.start()

        def mid():
            for j, (chip, p) in enumerate(zip(chips, passed())):
                copy(1 + j, (*chip, c), (x, y, c)).wait_recv()
                p.start()

        def finish():
            copy(0, sibling, (x, y, c)).wait_recv()
            for j, chip in enumerate(chips):
                copy(4 + j, (*chip, 1 - c), (x, y, c)).wait_recv()
            for cp in first() + passed():
                cp.wait_send()
            mine().wait()

        return start, mid, finish

    scratch = [pltpu.SemaphoreType.DMA((7,)), pltpu.SemaphoreType.DMA((7,)), pltpu.SemaphoreType.DMA((1,))]
    return _Comm([block], [jax.ShapeDtypeStruct((N_DEV,) + block.shape, block.dtype)], scratch, phases)


def _ffn_up(x, gain, sh, sc, wgu, comm=None):
    T, D = x.shape
    F = wgu.shape[1] // 2
    B = sh.shape[0]
    tm = _tile(TOKEN_TILE, T // B)
    tps = (T // B) // tm
    slabs = _slabs(F, FF_SLAB)

    def body(x_ref, gain_ref, sh_ref, sc_ref, w_ref, gu_ref, a_ref):
        xh, _ = _rms(x_ref[...])
        h = (xh * gain_ref[...] * (1.0 + sc_ref[0]) + sh_ref[0]).astype(BF16)

        def dots(s):
            return _dot(h, w_ref[:, s]), _dot(h, w_ref[:, slice(F + s.start, F + s.stop)])

        nxt = dots(slabs[0])
        for j, s in enumerate(slabs):
            g, u = nxt
            if j + 1 < len(slabs):
                nxt = dots(slabs[j + 1])
            gu_ref[0, :, s] = g.astype(BF16)
            gu_ref[1, :, s] = u.astype(BF16)
            a_ref[:, s] = (g * _sigmoid(g) * u).astype(BF16)

    seq = lambda i: (i // tps, 0, 0)
    return _call(
        body, "ffn_up", (T // tm,),
        [
            pl.BlockSpec((tm, D), lambda i: (i, 0)),
            pl.BlockSpec((1, D), lambda i: (0, 0)),
            pl.BlockSpec((1, 1, D), seq),
            pl.BlockSpec((1, 1, D), seq),
            pl.BlockSpec((D, 2 * F), lambda i: (0, 0), pipeline_mode=pl.Buffered(1)),
        ],
        [
            pl.BlockSpec((2, tm, F), lambda i: (0, i, 0)),
            pl.BlockSpec((tm, F), lambda i: (i, 0)),
        ],
        [
            jax.ShapeDtypeStruct((2, T, F), BF16),
            jax.ShapeDtypeStruct((T, F), BF16),
        ],
        [],
        (x, gain, sh, sc, wgu), comm)


def _ffn_down(a, x, gate, wd, comm=None):
    T, F = a.shape
    D = x.shape[1]
    B = gate.shape[0]
    tm = _tile(2 * TOKEN_TILE, T // B)
    tps = (T // B) // tm

    def body(a_ref, x_ref, gate_ref, wd_ref, xo_ref, f_ref):
        f = _dot(a_ref[...], wd_ref[...])
        f_ref[...] = f.astype(BF16)
        xo_ref[...] = x_ref[...] + 0.5 * gate_ref[0] * f

    return _call(
        body, "ffn_down", (T // tm,),
        [
            pl.BlockSpec((tm, F), lambda i: (i, 0)),
            pl.BlockSpec((tm, D), lambda i: (i, 0)),
            pl.BlockSpec((1, 1, D), lambda i: (i // tps, 0, 0)),
            pl.BlockSpec((F, D), lambda i: (0, 0), pipeline_mode=pl.Buffered(1)),
        ],
        [pl.BlockSpec((tm, D), lambda i: (i, 0)), pl.BlockSpec((tm, D), lambda i: (i, 0))],
        [jax.ShapeDtypeStruct((T, D), F32), jax.ShapeDtypeStruct((T, D), BF16)],
        [],
        (a, x, gate, wd), comm)


def _ffn_fwd(x, gain, sh, sc, gate, wgu, wd, comm=None):
    T, D = x.shape
    F = wd.shape[0]
    B = sh.shape[0]
    tm = _tile(FWD_TILE, T // B)
    tps = (T // B) // tm
    slabs = _slabs(F, FF_SLAB)

    def body(x_ref, gain_ref, sh_ref, sc_ref, gate_ref, w_ref, wd_ref, xo_ref, gu_ref, a_ref, f_ref):
        x = x_ref[...]
        h = (_rms(x)[0] * gain_ref[...] * (1.0 + sc_ref[0]) + sh_ref[0]).astype(BF16)

        def dots(s):
            return _dot(h, w_ref[:, s]), _dot(h, w_ref[:, slice(F + s.start, F + s.stop)])

        nxt = dots(slabs[0])
        for j, s in enumerate(slabs):
            g, u = nxt
            if j + 1 < len(slabs):
                nxt = dots(slabs[j + 1])
            gu_ref[0, :, s] = g.astype(BF16)
            gu_ref[1, :, s] = u.astype(BF16)
            a_ref[:, s] = (g * _sigmoid(g) * u).astype(BF16)
        f = _dot(a_ref[...], wd_ref[...])
        f_ref[...] = f.astype(BF16)
        xo_ref[...] = x + 0.5 * gate_ref[0] * f

    seq = lambda i: (i // tps, 0, 0)
    row = lambda i: (i, 0)
    return _call(
        body, "ffn_fwd", (T // tm,),
        [
            pl.BlockSpec((tm, D), row),
            pl.BlockSpec((1, D), lambda i: (0, 0)),
            pl.BlockSpec((1, 1, D), seq),
            pl.BlockSpec((1, 1, D), seq),
            pl.BlockSpec((1, 1, D), seq),
            pl.BlockSpec((D, 2 * F), lambda i: (0, 0), pipeline_mode=pl.Buffered(1)),
            pl.BlockSpec((F, D), lambda i: (0, 0), pipeline_mode=pl.Buffered(1)),
        ],
        [
            pl.BlockSpec((tm, D), row),
            pl.BlockSpec((2, tm, F), lambda i: (0, i, 0)),
            pl.BlockSpec((tm, F), row),
            pl.BlockSpec((tm, D), row),
        ],
        [
            jax.ShapeDtypeStruct((T, D), F32),
            jax.ShapeDtypeStruct((2, T, F), BF16),
            jax.ShapeDtypeStruct((T, F), BF16),
            jax.ShapeDtypeStruct((T, D), BF16),
        ],
        [],
        (x, gain, sh, sc, gate, wgu, wd), comm)


def _ffn_bwd(dxo, x, gu, f, gain, sh, sc, gate, wgu, wd, comm=None):
    T, D = x.shape
    F = wd.shape[0]
    B = sc.shape[0]
    tm = _tile(BWD_TILE, T // B)
    tps = (T // B) // tm
    slabs = _slabs(F, FF_SLAB)

    def body(dxo_ref, x_ref, gu_ref, f_ref, gain_ref, sh_ref, sc_ref, gate_ref, w_ref, wd_ref,
             dx_ref, dgu_ref, h_ref, df_ref, dsc_ref, dsh_ref, dgain_ref, dgate_ref):
        i = pl.program_id(0)
        first_of_seq = (i % tps) == 0
        gain = gain_ref[...]
        sc = sc_ref[0]
        dxo = dxo_ref[...]
        x = x_ref[...]
        df = (0.5 * gate_ref[0] * dxo).astype(BF16)
        df_ref[...] = df
        nxt = _dot_nt(df, wd_ref[slabs[0], :])
        for j, s in enumerate(slabs):
            da = nxt
            if j + 1 < len(slabs):
                nxt = _dot_nt(df, wd_ref[slabs[j + 1], :])
            g = gu_ref[0, :, s]
            sg = 1.0 / (1.0 + jnp.exp(-g))
            t = g * sg
            dab = da.astype(BF16)
            dgu_ref[1, :, s] = dab * t
            dgu_ref[0, :, s] = dab * gu_ref[1, :, s] * (sg + t - t * sg)
        dh = _dot_nt(dgu_ref[0], w_ref[:, 0:F]) + _dot_nt(dgu_ref[1], w_ref[:, F:])
        dx, dsc, dsh, dgain = _norm_mod_bwd(x, dh, gain, sc)
        dx_ref[...] = dxo + dx
        h_ref[...] = (_rms(x)[0] * gain * (1.0 + sc) + sh_ref[0]).astype(BF16)
        _acc(dsc_ref.at[0], first_of_seq, dsc)
        _acc(dsh_ref.at[0], first_of_seq, dsh)
        _acc(dgain_ref, i == 0, dgain)
        _acc(dgate_ref.at[0], first_of_seq, 0.5 * jnp.sum(dxo * f_ref[...].astype(F32), axis=0, keepdims=True))

    seq = lambda i: (i // tps, 0, 0)
    row = lambda i: (i, 0)
    return _call(
        body, "ffn_bwd", (T // tm,),
        [
            pl.BlockSpec((tm, D), row),
            pl.BlockSpec((tm, D), row),
            pl.BlockSpec((2, tm, F), lambda i: (0, i, 0)),
            pl.BlockSpec((tm, D), row),
            pl.BlockSpec((1, D), lambda i: (0, 0)),
            pl.BlockSpec((1, 1, D), seq),
            pl.BlockSpec((1, 1, D), seq),
            pl.BlockSpec((1, 1, D), seq),
            pl.BlockSpec((D, 2 * F), lambda i: (0, 0), pipeline_mode=pl.Buffered(1)),
            pl.BlockSpec((F, D), lambda i: (0, 0), pipeline_mode=pl.Buffered(1)),
        ],
        [
            pl.BlockSpec((tm, D), row),
            pl.BlockSpec((2, tm, F), lambda i: (0, i, 0)),
            pl.BlockSpec((tm, D), row),
            pl.BlockSpec((tm, D), row),
            pl.BlockSpec((1, 1, D), seq),
            pl.BlockSpec((1, 1, D), seq),
            pl.BlockSpec((1, D), lambda i: (0, 0)),
            pl.BlockSpec((1, 1, D), seq),
        ],
        [
            jax.ShapeDtypeStruct((T, D), F32),
            jax.ShapeDtypeStruct((2, T, F), BF16),
            jax.ShapeDtypeStruct((T, D), BF16),
            jax.ShapeDtypeStruct((T, D), BF16),
            jax.ShapeDtypeStruct((B, 1, D), F32),
            jax.ShapeDtypeStruct((B, 1, D), F32),
            jax.ShapeDtypeStruct((1, D), F32),
            jax.ShapeDtypeStruct((B, 1, D), F32),
        ],
        [],
        (dxo, x, gu, f, gain, sh, sc, gate, wgu, wd), comm)


def _wgrad(a, b, tmm, tn, col_major, name, tokens=WGRAD_TOKENS, comm=None):
    T, M = a.shape
    nb, _, Nb = b.shape
    N = nb * Nb
    tk = _tile(tokens, T)
    span = 2 if col_major else 1
    wide = span * tn
    npb = Nb // wide
    assert M % tmm == 0 and Nb % wide == 0
    if col_major:
        assert tmm == M
        shape = (N // tn, 2, M // 2, tn)
        out_spec = pl.BlockSpec((span, 2, M // 2, tn), lambda i, j, t: (j, 0, 0, 0))
    else:
        shape = (M // tmm, tmm, N)
        out_spec = pl.BlockSpec((None, tmm, tn), lambda i, j, t: (i, 0, j))

    def body(a_ref, b_ref, o_ref):
        @pl.when(pl.program_id(2) == 0)
        def _():
            o_ref[...] = jnp.zeros_like(o_ref)

        res = _dot_tn(a_ref[...], b_ref[...])
        if col_major:
            for s in range(span):
                for h in range(2):
                    o_ref[s, h] += res[h * (M // 2):(h + 1) * (M // 2), s * tn:(s + 1) * tn]
        else:
            o_ref[...] += res

    return _call(
        body, name, (M // tmm, N // wide, T // tk),
        [
            pl.BlockSpec((tk, tmm), lambda i, j, t: (t, i)),
            pl.BlockSpec((None, tk, wide), lambda i, j, t: (j // npb, t, j % npb)),
        ],
        [out_spec], [jax.ShapeDtypeStruct(shape, F32)], [],
        (a, b), comm)


def _mixin_fwd(x, gain, sh, sc, win, comm=None):
    T, D = x.shape
    P = win.shape[1]
    B = sh.shape[0]
    tm = _tile(TOKEN_TILE, T // B)
    tps = (T // B) // tm

    def body(x_ref, gain_ref, sh_ref, sc_ref, w_ref, proj_ref, h_ref):
        xh, _ = _rms(x_ref[...])
        h = (xh * gain_ref[...] * (1.0 + sc_ref[0]) + sh_ref[0]).astype(BF16)
        h_ref[...] = h
        proj_ref[...] = _dot(h, w_ref[...])

    seq = lambda i: (i // tps, 0, 0)
    return _call(
        body, "mixin_fwd", (T // tm,),
        [
            pl.BlockSpec((tm, D), lambda i: (i, 0)),
            pl.BlockSpec((1, D), lambda i: (0, 0)),
            pl.BlockSpec((1, 1, D), seq),
            pl.BlockSpec((1, 1, D), seq),
            pl.BlockSpec((D, P), lambda i: (0, 0)),
        ],
        [pl.BlockSpec((tm, P), lambda i: (i, 0)), pl.BlockSpec((tm, D), lambda i: (i, 0))],
        [jax.ShapeDtypeStruct((T, P), F32), jax.ShapeDtypeStruct((T, D), BF16)],
        [],
        (x, gain, sh, sc, win), comm)


def _mixin_bwd(dxo, x, dproj, gain, sc, win, comm=None):
    T, D = x.shape
    P = win.shape[1]
    B = sc.shape[0]
    tm = _tile(TOKEN_TILE, T // B)
    tps = (T // B) // tm

    def body(dxo_ref, x_ref, dp_ref, gain_ref, sc_ref, w_ref, dx_ref, dsc_ref, dsh_ref, dgain_ref):
        i = pl.program_id(0)
        first_of_seq = (i % tps) == 0
        halves = _slabs(tm, tm // 2)
        nxt = _dot_nt(dp_ref[halves[0], :], w_ref[...])
        sums = None
        for j, r in enumerate(halves):
            dh = nxt
            if j + 1 < len(halves):
                nxt = _dot_nt(dp_ref[halves[j + 1], :], w_ref[...])
            part = _norm_mod_bwd(x_ref[r, :], dh, gain_ref[...], sc_ref[0])
            dx_ref[r, :] = dxo_ref[r, :] + part[0]
            sums = part[1:] if sums is None else tuple(a + b for a, b in zip(sums, part[1:]))
        _acc(dsc_ref.at[0], first_of_seq, sums[0])
        _acc(dsh_ref.at[0], first_of_seq, sums[1])
        _acc(dgain_ref, i == 0, sums[2])

    seq = lambda i: (i // tps, 0, 0)
    row = lambda i: (i, 0)
    return _call(
        body, "mixin_bwd", (T // tm,),
        [
            pl.BlockSpec((tm, D), row),
            pl.BlockSpec((tm, D), row),
            pl.BlockSpec((tm, P), row),
            pl.BlockSpec((1, D), lambda i: (0, 0)),
            pl.BlockSpec((1, 1, D), seq),
            pl.BlockSpec((D, P), lambda i: (0, 0)),
        ],
        [
            pl.BlockSpec((tm, D), row),
            pl.BlockSpec((1, 1, D), seq),
            pl.BlockSpec((1, 1, D), seq),
            pl.BlockSpec((1, D), lambda i: (0, 0)),
        ],
        [
            jax.ShapeDtypeStruct((T, D), F32),
            jax.ShapeDtypeStruct((B, 1, D), F32),
            jax.ShapeDtypeStruct((B, 1, D), F32),
            jax.ShapeDtypeStruct((1, D), F32),
        ],
        [],
        (dxo, x, dproj, gain, sc, win), comm)


def _head_mean(z, pmat, exact=True):
    hi = z.astype(BF16)
    if not exact:
        return _dot(hi, pmat)
    lo = (z - hi.astype(F32)).astype(BF16)
    return _dot(hi, pmat) + _dot(lo, pmat)


def _gelu_parts(x):
    cdf = 0.5 * (1.0 + lax.erf(x * (1.0 / math.sqrt(2.0))))
    return x * cdf, cdf


def _gelu_grad(x, cdf):
    return cdf + x * jnp.exp(-0.5 * x * x) * (1.0 / math.sqrt(2.0 * math.pi))


LANES = 128


def _head_blocks(da):
    hd = da // N_HEADS
    lb = min(LANES, da)
    col = lax.broadcasted_iota(jnp.int32, (1, lb), 1)
    return lb, lb // hd, da // lb, [(col >= h * hd) & (col < (h + 1) * hd) for h in range(lb // hd)]


def _mix_heads(w_stack, v, da):
    lb, hpb, nb, masks = _head_blocks(da)
    outs = []
    for b in range(nb):
        res = _dot(w_stack[b * hpb * CHUNK:(b + 1) * hpb * CHUNK], v[:, b * lb:(b + 1) * lb])
        out = res[0:CHUNK]
        for h in range(1, hpb):
            out = jnp.where(masks[h], res[h * CHUNK:(h + 1) * CHUNK], out)
        outs.append(out)
    return outs[0] if nb == 1 else jnp.concatenate(outs, axis=1)


def _mix_heads_grad(dm, v, da):
    lb, hpb, nb, masks = _head_blocks(da)
    outs = []
    for b in range(nb):
        dmb = dm[:, b * lb:(b + 1) * lb]
        stack = jnp.concatenate([jnp.where(masks[h], dmb, jnp.zeros_like(dmb)) for h in range(hpb)], axis=0)
        outs.append(_dot_nt(stack, v[:, b * lb:(b + 1) * lb]))
    return outs[0] if nb == 1 else jnp.concatenate(outs, axis=0)


def _causal_stack(w, transposed):
    r = lax.broadcasted_iota(jnp.int32, w.shape, 0) % CHUNK
    c = lax.broadcasted_iota(jnp.int32, w.shape, 1)
    keep = (c >= r) if transposed else (c <= r)
    return jnp.where(keep, w, 0.0)


def _mix_core_forward(proj, zprev, prm, da, db, saved=None):
    n = proj.shape[0]
    ua = proj[:, 0:da]
    va = proj[:, da:2 * da]
    bg = proj[:, 2 * da:2 * da + db]
    cg = proj[:, 2 * da + db:2 * da + 2 * db]
    xb = proj[:, 2 * da + 2 * db:]
    if saved is None:
        ug, ucdf = _gelu_parts(ua)
        vg, vcdf = _gelu_parts(va)
        zc = vg - _head_mean(vg, prm["pmat"])
        rs = lax.rsqrt(_head_mean(zc * zc, prm["pmat"], exact=False) + EPS)
        vhat = zc * rs
        vln = (vhat * prm["lng"] + prm["lnb"]).astype(BF16)
        wst = _causal_stack(prm["wst"], False).astype(BF16)
        mixed = [_mix_heads(wst, vln[j * CHUNK:(j + 1) * CHUNK], da) + prm["bias"] for j in range(n // CHUNK)]
        mixed = mixed[0] if len(mixed) == 1 else jnp.concatenate(mixed, axis=0)
    else:
        ucdf, vcdf, vhat, rs, mixed = [saved[k].astype(F32) for k in range(5)]
        ug = ua * ucdf
        vln = (vhat * prm["lng"] + prm["lnb"]).astype(BF16)
    ya = ug * mixed
    z = cg * xb
    row = lax.broadcasted_iota(jnp.int32, z.shape, 0)
    z1 = jnp.where(row == 0, zprev[7:8], pltpu.roll(z, 1, 0))
    z2 = jnp.where(row == 0, zprev[6:7], jnp.where(row == 1, zprev[7:8], pltpu.roll(z, 2, 0)))
    cw = prm["convw"]
    conv = z2 * cw[0:1] + z1 * cw[1:2] + z * cw[2:3]
    yb = bg * conv
    yah, ra = _rms(ya)
    ybh, rb = _rms(yb)
    return dict(ua=ua, va=va, bg=bg, cg=cg, xb=xb, ug=ug, ucdf=ucdf, vcdf=vcdf, rs=rs, vhat=vhat, vln=vln,
                mixed=mixed, z=z, z1=z1, z2=z2, conv=conv, yah=yah, ra=ra, ybh=ybh, rb=rb)


def _mix_params(lng_ref, lnb_ref, wst_ref, bias_ref, pmat_ref, convw_ref):
    return dict(lng=lng_ref[...], lnb=lnb_ref[...], wst=wst_ref[...], bias=bias_ref[...],
                pmat=pmat_ref[...], convw=convw_ref[...])


def _mix_core_fwd(proj, x, gate, wout, lng, lnb, wst, bias, pmat, convw, og, comm=None):
    T, P = proj.shape
    D = x.shape[1]
    B = gate.shape[0]
    da = lng.shape[1]
    db = convw.shape[1]
    tm = _tile(MIX_TILE, T // B)
    tps = (T // B) // tm

    def body(proj_ref, x_ref, gate_ref, wout_ref, lng_ref, lnb_ref, wst_ref, bias_ref, pmat_ref, convw_ref,
             og_ref, xo_ref, yn_ref, sv_ref, halo):
        i = pl.program_id(0)

        @pl.when((i % tps) == 0)
        def _():
            halo[...] = jnp.zeros_like(halo)

        prm = _mix_params(lng_ref, lnb_ref, wst_ref, bias_ref, pmat_ref, convw_ref)
        r = _mix_core_forward(proj_ref[...], halo[...], prm, da, db)
        halo[...] = r["z"][tm - 8:tm]
        for k, name in enumerate(("ucdf", "vcdf", "vhat", "rs", "mixed")):
            sv_ref[k] = r[name].astype(BF16)
        og = og_ref[...]
        yn_ref[:, 0:da] = (r["yah"] * og[:, 0:da]).astype(BF16)
        yn_ref[:, da:] = (r["ybh"] * og[:, da:]).astype(BF16)
        xo_ref[...] = x_ref[...] + gate_ref[0] * _dot(yn_ref[...], wout_ref[...])

    full = lambda a: pl.BlockSpec(a.shape, lambda i: (0,) * a.ndim)
    return _call(
        body, "mix_core_fwd", (T // tm,),
        [
            pl.BlockSpec((tm, P), lambda i: (i, 0)),
            pl.BlockSpec((tm, D), lambda i: (i, 0)),
            pl.BlockSpec((1, 1, D), lambda i: (i // tps, 0, 0)),
            full(wout), full(lng), full(lnb), full(wst), full(bias), full(pmat), full(convw), full(og),
        ],
        [pl.BlockSpec((tm, D), lambda i: (i, 0)), pl.BlockSpec((tm, D), lambda i: (i, 0)),
         pl.BlockSpec((5, tm, da), lambda i: (0, i, 0))],
        [jax.ShapeDtypeStruct((T, D), F32), jax.ShapeDtypeStruct((T, D), BF16),
         jax.ShapeDtypeStruct((5, T, da), BF16)],
        [pltpu.VMEM((8, db), F32)],
        (proj, x, gate, wout, lng, lnb, wst, bias, pmat, convw, og), comm)


def _mix_core_bwd(proj, sv, dxo, gate, wout, lng, lnb, wstt, pmat, convw, og, comm=None):
    T, P = proj.shape
    D = dxo.shape[1]
    B = gate.shape[0]
    da = lng.shape[1]
    db = convw.shape[1]
    assert da == db and P == 2 * da + 3 * db
    tm = _tile(MIX_TILE, T // B)
    tps = (T // B) // tm
    nt = T // tm
    hd = da // N_HEADS

    def body(proj_ref, cgp_ref, xbp_ref, sv_ref, dxo_ref, gate_ref, wout_ref, lng_ref, lnb_ref, wstt_ref,
             pmat_ref, convw_ref, og_ref,
             dproj_ref, do_ref, dgate_ref, dog_ref, dwst_ref, dbias_ref, dlng_ref, dlnb_ref, dconvw_ref, carry):
        i = pl.program_id(0)
        ri = nt - 1 - i
        first = i == 0
        end_of_seq = (ri % tps) == tps - 1
        start_of_seq = (ri % tps) == 0

        @pl.when(end_of_seq)
        def _():
            carry[...] = jnp.zeros_like(carry)

        prm = dict(lng=lng_ref[...], lnb=lnb_ref[...], pmat=pmat_ref[...], convw=convw_ref[...])
        zprev = jnp.where(start_of_seq, 0.0, cgp_ref[...] * xbp_ref[...])
        r = _mix_core_forward(proj_ref[...], zprev, prm, da, db, saved=sv_ref)
        og = og_ref[...]
        pmat = prm["pmat"]

        yn = jnp.concatenate([(r["yah"] * og[:, 0:da]).astype(BF16), (r["ybh"] * og[:, da:]).astype(BF16)], axis=1)
        dxo = dxo_ref[...]
        o = _dot(yn, wout_ref[...])
        _acc(dgate_ref.at[0], end_of_seq, jnp.sum(dxo * o, axis=0, keepdims=True))
        d_o = (gate_ref[0] * dxo).astype(BF16)
        do_ref[...] = d_o
        dyn = _dot_nt(d_o, wout_ref[...])

        def rms_bwd(dyn_g, yh, rr, og_g):
            dog_g = jnp.sum(dyn_g * yh, axis=0, keepdims=True)
            dyh = dyn_g * og_g
            return rr * (dyh - yh * jnp.mean(dyh * yh, axis=-1, keepdims=True)), dog_g

        dya, dog_a = rms_bwd(dyn[:, 0:da], r["yah"], r["ra"], og[:, 0:da])
        dyb, dog_b = rms_bwd(dyn[:, da:], r["ybh"], r["rb"], og[:, da:])
        _acc(dog_ref, first, jnp.concatenate([dog_a, dog_b], axis=1))

        dug = dya * r["mixed"]
        dmixed = dya * r["ug"]
        wstt_b = _causal_stack(wstt_ref[...], True).astype(BF16)
        dbias = jnp.zeros((CHUNK, da), F32)
        dwst = jnp.zeros((N_HEADS * CHUNK, CHUNK), F32)
        dvln = []
        for j in range(tm // CHUNK):
            dm = dmixed[j * CHUNK:(j + 1) * CHUNK]
            dbias = dbias + dm
            dmb = dm.astype(BF16)
            dwst = dwst + _mix_heads_grad(dmb, r["vln"][j * CHUNK:(j + 1) * CHUNK], da)
            dvln.append(_mix_heads(wstt_b, dmb, da))
        dvln = dvln[0] if len(dvln) == 1 else jnp.concatenate(dvln, axis=0)
        _acc(dbias_ref, first, dbias)
        _acc(dwst_ref, first, dwst)
        _acc(dlng_ref, first, jnp.sum(dvln * r["vhat"], axis=0, keepdims=True))
        _acc(dlnb_ref, first, jnp.sum(dvln, axis=0, keepdims=True))
        dvhat = dvln * prm["lng"]
        dvg = r["rs"] * (dvhat - _head_mean(dvhat, pmat, exact=False)
                         - r["vhat"] * _head_mean(dvhat * r["vhat"], pmat, exact=False))
        dproj_ref[:, 0:da] = (dug * _gelu_grad(r["ua"], r["ucdf"])).astype(BF16)
        dproj_ref[:, da:2 * da] = (dvg * _gelu_grad(r["va"], r["vcdf"])).astype(BF16)

        dproj_ref[:, 2 * da:2 * da + db] = (dyb * r["conv"]).astype(BF16)
        dconv = dyb * r["bg"]
        dcw = jnp.concatenate([
            jnp.sum(dconv * r["z2"], axis=0, keepdims=True),
            jnp.sum(dconv * r["z1"], axis=0, keepdims=True),
            jnp.sum(dconv * r["z"], axis=0, keepdims=True),
            jnp.zeros((5, db), F32)], axis=0)
        _acc(dconvw_ref, first, dcw)
        nxt = carry[...]
        row = lax.broadcasted_iota(jnp.int32, dconv.shape, 0)
        dc1 = jnp.where(row == tm - 1, nxt[0:1], pltpu.roll(dconv, tm - 1, 0))
        dc2 = jnp.where(row == tm - 2, nxt[0:1], jnp.where(row == tm - 1, nxt[1:2], pltpu.roll(dconv, tm - 2, 0)))
        carry[...] = dconv[0:8]
        cw = prm["convw"]
        dz = dconv * cw[2:3] + dc1 * cw[1:2] + dc2 * cw[0:1]
        dproj_ref[:, 2 * da + db:2 * da + 2 * db] = (dz * r["xb"]).astype(BF16)
        dproj_ref[:, 2 * da + 2 * db:] = (dz * r["cg"]).astype(BF16)

        @pl.when(i == nt - 1)
        def _():
            dwst_ref[...] = _causal_stack(dwst_ref[...], False)
            dbias_ref[...] = _head_mean(dbias_ref[...], pmat) * float(hd)

    full = lambda a: pl.BlockSpec(a.shape, lambda i: (0,) * a.ndim)
    const = lambda i: (0, 0)
    rev = lambda i: (nt - 1 - i, 0)
    prev8 = lambda col: (lambda i: (jnp.maximum((nt - 1 - i) * (tm // 8) - 1, 0), col))
    return _call(
        body, "mix_core_bwd", (nt,),
        [
            pl.BlockSpec((tm, P), rev),
            pl.BlockSpec((8, db), prev8((2 * da + db) // db)),
            pl.BlockSpec((8, db), prev8((2 * da + 2 * db) // db)),
            pl.BlockSpec((5, tm, da), lambda i: (0, nt - 1 - i, 0)),
            pl.BlockSpec((tm, D), rev),
            pl.BlockSpec((1, 1, D), lambda i: ((nt - 1 - i) // tps, 0, 0)),
            full(wout), full(lng), full(lnb), full(wstt), full(pmat), full(convw), full(og),
        ],
        [
            pl.BlockSpec((tm, P), rev),
            pl.BlockSpec((tm, D), rev),
            pl.BlockSpec((1, 1, D), lambda i: ((nt - 1 - i) // tps, 0, 0)),
            pl.BlockSpec((1, D), const),
            pl.BlockSpec((N_HEADS * CHUNK, CHUNK), const),
            pl.BlockSpec((CHUNK, da), const),
            pl.BlockSpec((1, da), const),
            pl.BlockSpec((1, da), const),
            pl.BlockSpec((8, db), const),
        ],
        [
            jax.ShapeDtypeStruct((T, P), BF16),
            jax.ShapeDtypeStruct((T, D), BF16),
            jax.ShapeDtypeStruct((B, 1, D), F32),
            jax.ShapeDtypeStruct((1, D), F32),
            jax.ShapeDtypeStruct((N_HEADS * CHUNK, CHUNK), F32),
            jax.ShapeDtypeStruct((CHUNK, da), F32),
            jax.ShapeDtypeStruct((1, da), F32),
            jax.ShapeDtypeStruct((1, da), F32),
            jax.ShapeDtypeStruct((8, db), F32),
        ],
        [pltpu.VMEM((8, db), F32)],
        (proj, proj, proj, sv, dxo, gate, wout, lng, lnb, wstt, pmat, convw, og), comm)


def _loss_head(x, target, gain):
    T, D = x.shape
    tm = _tile(TOKEN_TILE, T)

    def body(x_ref, t_ref, gain_ref, dx_ref, loss_ref, dgain_ref):
        first = pl.program_id(0) == 0
        xh, r = _rms(x_ref[...])
        gain = gain_ref[...]
        err = xh * gain - t_ref[...]
        _acc(loss_ref, first, jnp.zeros((8, 128), F32) + 0.5 * jnp.sum(err * err) / D)
        dout = err * (1.0 / D)
        _acc(dgain_ref, first, jnp.sum(dout * xh, axis=0, keepdims=True))
        dy = dout * gain
        dx_ref[...] = r * (dy - xh * jnp.mean(dy * xh, axis=-1, keepdims=True))

    return _call(
        body, "loss_head", (T // tm,),
        [
            pl.BlockSpec((tm, D), lambda i: (i, 0)),
            pl.BlockSpec((tm, D), lambda i: (i, 0)),
            pl.BlockSpec((1, D), lambda i: (0, 0)),
        ],
        [
            pl.BlockSpec((tm, D), lambda i: (i, 0)),
            pl.BlockSpec((8, 128), lambda i: (0, 0)),
            pl.BlockSpec((1, D), lambda i: (0, 0)),
        ],
        [
            jax.ShapeDtypeStruct((T, D), F32),
            jax.ShapeDtypeStruct((8, 128), F32),
            jax.ShapeDtypeStruct((1, D), F32),
        ],
        [],
        (x, target, gain))[0]


def _ada_fwd(c_all, ada_w, ada_b):
    n, D = c_all.shape
    L, _, sa = ada_w.shape
    tn = _tile(768, sa)

    def body(c_ref, w_ref, b_ref, act_ref, o_ref):
        c = c_ref[...]
        act = (c * _sigmoid(c)).astype(BF16)
        act_ref[...] = act
        o_ref[...] = _dot(act, w_ref[...].astype(BF16)) + b_ref[...]

    return _call(
        body, "ada_fwd", (L, sa // tn),
        [
            pl.BlockSpec((n, D), lambda l, j: (0, 0)),
            pl.BlockSpec((None, D, tn), lambda l, j: (l, 0, j)),
            pl.BlockSpec((None, 1, tn), lambda l, j: (l, 0, j)),
        ],
        [
            pl.BlockSpec((n, D), lambda l, j: (0, 0)),
            pl.BlockSpec((None, n, tn), lambda l, j: (l, 0, j)),
        ],
        [jax.ShapeDtypeStruct((n, D), BF16), jax.ShapeDtypeStruct((L, n, sa), F32)],
        [],
        (c_all, ada_w, ada_b))[0]


def _ada_bwd(c_act, d_ada, comm=None):
    n, D = c_act.shape
    L, _, sa = d_ada.shape
    tn = _tile(768, sa)

    def body(c_ref, d_ref, o_ref):
        o_ref[...] = _dot_tn(c_ref[...], d_ref[...])

    return _call(
        body, "ada_bwd", (L, sa // tn),
        [pl.BlockSpec((n, D), lambda l, j: (0, 0)), pl.BlockSpec((None, n, tn), lambda l, j: (l, 0, j))],
        [pl.BlockSpec((None, D, tn), lambda l, j: (l, 0, j))],
        [jax.ShapeDtypeStruct((L, D, sa), F32)],
        [],
        (c_act, d_ada), comm)


def _colsum(a):
    L, n, C = a.shape

    def body(a_ref, o_ref):
        o_ref[...] = jnp.sum(a_ref[...], axis=0, keepdims=True)

    return _call(
        body, "colsum", (L,),
        [pl.BlockSpec((None, n, C), lambda l: (l, 0, 0))],
        [pl.BlockSpec((None, 1, C), lambda l: (l, 0, 0))],
        [jax.ShapeDtypeStruct((L, 1, C), F32)],
        [],
        (a,))[0][0]


def _row_tile(rows, cols, nbuf):
    budget = VMEM_LIMIT // 3 // (2 * nbuf * 4 * cols)
    t = rows
    while t > max(budget, 8) and t % 2 == 0 and (t // 2) % 8 == 0:
        t //= 2
    return t


def _pair_sum(g, recv, core):
    n, _, R, C = g.shape
    tr = _row_tile(R, C, 3)

    def body(core_ref, g_ref, r_ref, o_ref):
        o_ref[...] = (g_ref[...] + r_ref[...]).astype(BF16)

    return pl.pallas_call(
        body,
        name="pair_sum",
        grid_spec=pltpu.PrefetchScalarGridSpec(
            num_scalar_prefetch=1,
            grid=(n, R // tr),
            in_specs=[
                pl.BlockSpec((None, None, tr, C), lambda i, r, core_ref: (i, core_ref[0], r, 0)),
                pl.BlockSpec((None, tr, C), lambda i, r, core_ref: (i, r, 0)),
            ],
            out_specs=pl.BlockSpec((None, tr, C), lambda i, r, core_ref: (i, r, 0)),
        ),
        out_shape=jax.ShapeDtypeStruct((n, R, C), BF16),
        compiler_params=pltpu.CompilerParams(dimension_semantics=("arbitrary", "arbitrary"),
                                             vmem_limit_bytes=VMEM_LIMIT),
    )(core, g, recv)


def _chip_sum(q, core, l, n_layers, prev):
    nq, R, C = q.shape
    tr = _row_tile(R, C, 4)

    def body(core_ref, q_ref, *rest):
        o_ref = rest[-1]
        s = q_ref[0].astype(F32)
        for j in range(1, nq):
            s = s + q_ref[j].astype(F32)
        o_ref[...] = s

    in_specs = [pl.BlockSpec((nq, tr, C), lambda r, core_ref: (0, r, 0))]
    args = [core, q]
    aliases = {}
    if prev is not None:
        in_specs.append(ANY)
        args.append(prev)
        aliases = {2: 0}
    return pl.pallas_call(
        body,
        name="chip_sum",
        grid_spec=pltpu.PrefetchScalarGridSpec(
            num_scalar_prefetch=1,
            grid=(R // tr,),
            in_specs=in_specs,
            out_specs=pl.BlockSpec((None, None, tr, C), lambda r, core_ref: (l, core_ref[0], r, 0)),
        ),
        out_shape=jax.ShapeDtypeStruct((n_layers, 2, R, C), F32),
        input_output_aliases=aliases,
        compiler_params=pltpu.CompilerParams(dimension_semantics=("arbitrary",), vmem_limit_bytes=VMEM_LIMIT),
    )(*args)


def _sum_blocks(a, n):
    M = a.shape[0] // n
    C = a.shape[1]

    def body(a_ref, o_ref):
        s = a_ref[0:M]
        for j in range(1, n):
            s = s + a_ref[j * M:(j + 1) * M]
        o_ref[...] = s

    return pl.pallas_call(
        body,
        name="sum_blocks",
        out_shape=jax.ShapeDtypeStruct((M, C), F32),
        compiler_params=pltpu.CompilerParams(vmem_limit_bytes=VMEM_LIMIT),
    )(a)


def _adamw(w, g, m, v, emit_grad=False):
    R, C = w.shape
    n_out = 4 if emit_grad else 3
    tr = _row_tile(R, C, 4 + n_out) if R % 8 == 0 else R

    def body(w_ref, g_ref, m_ref, v_ref, d_ref, nm_ref, nv_ref, *g_out):
        g = g_ref[...]
        m = ADAM_B1 * m_ref[...] + (1.0 - ADAM_B1) * g
        v = ADAM_B2 * v_ref[...] + (1.0 - ADAM_B2) * (g * g)
        m_hat = m / (1.0 - ADAM_B1 ** ADAM_STEP)
        v_hat = v / (1.0 - ADAM_B2 ** ADAM_STEP)
        d_ref[...] = -ADAM_LR * (m_hat / (jnp.sqrt(v_hat) + ADAM_EPS) + ADAM_WD * w_ref[...])
        nm_ref[...] = m
        nv_ref[...] = v
        if emit_grad:
            g_out[0][...] = g

    spec = pl.BlockSpec((tr, C), lambda i: (i, 0))
    return _call(body, "adamw", (R // tr,), [spec] * 4, [spec] * n_out, [jax.ShapeDtypeStruct((R, C), F32)] * n_out,
                 [], (w, g, m, v))[0]


def kernel(x, c, ada_w, ada_b, norm_ffn1_g, ffn1_w_gu, ffn1_w_down, norm_mix_g, mix_w_in, sgu_ln_g, sgu_ln_b, sgu_w_s, sgu_b, conv_w, out_norm_g, mix_w_out, norm_ffn2_g, ffn2_w_gu, ffn2_w_down, final_norm_g, loss_target, m_ada_w, m_ada_b, m_norm_ffn1_g, m_ffn1_w_gu, m_ffn1_w_down, m_norm_mix_g, m_mix_w_in, m_sgu_ln_g, m_sgu_ln_b, m_sgu_w_s, m_sgu_b, m_conv_w, m_out_norm_g, m_mix_w_out, m_norm_ffn2_g, m_ffn2_w_gu, m_ffn2_w_down, m_final_norm_g, v_ada_w, v_ada_b, v_norm_ffn1_g, v_ffn1_w_gu, v_ffn1_w_down, v_norm_mix_g, v_mix_w_in, v_sgu_ln_g, v_sgu_ln_b, v_sgu_w_s, v_sgu_b, v_conv_w, v_out_norm_g, v_mix_w_out, v_norm_ffn2_g, v_ffn2_w_gu, v_ffn2_w_down, v_final_norm_g):
    weights = dict(ada_w=ada_w, ada_b=ada_b, norm_ffn1_g=norm_ffn1_g, ffn1_w_gu=ffn1_w_gu, ffn1_w_down=ffn1_w_down,
                   norm_mix_g=norm_mix_g, mix_w_in=mix_w_in, sgu_ln_g=sgu_ln_g, sgu_ln_b=sgu_ln_b, sgu_w_s=sgu_w_s,
                   sgu_b=sgu_b, conv_w=conv_w, out_norm_g=out_norm_g, mix_w_out=mix_w_out, norm_ffn2_g=norm_ffn2_g,
                   ffn2_w_gu=ffn2_w_gu, ffn2_w_down=ffn2_w_down, final_norm_g=final_norm_g)
    m_in = dict(ada_w=m_ada_w, ada_b=m_ada_b, norm_ffn1_g=m_norm_ffn1_g, ffn1_w_gu=m_ffn1_w_gu,
                ffn1_w_down=m_ffn1_w_down, norm_mix_g=m_norm_mix_g, mix_w_in=m_mix_w_in, sgu_ln_g=m_sgu_ln_g,
                sgu_ln_b=m_sgu_ln_b, sgu_w_s=m_sgu_w_s, sgu_b=m_sgu_b, conv_w=m_conv_w, out_norm_g=m_out_norm_g,
                mix_w_out=m_mix_w_out, norm_ffn2_g=m_norm_ffn2_g, ffn2_w_gu=m_ffn2_w_gu, ffn2_w_down=m_ffn2_w_down,
                final_norm_g=m_final_norm_g)
    v_in = dict(ada_w=v_ada_w, ada_b=v_ada_b, norm_ffn1_g=v_norm_ffn1_g, ffn1_w_gu=v_ffn1_w_gu,
                ffn1_w_down=v_ffn1_w_down, norm_mix_g=v_norm_mix_g, mix_w_in=v_mix_w_in, sgu_ln_g=v_sgu_ln_g,
                sgu_ln_b=v_sgu_ln_b, sgu_w_s=v_sgu_w_s, sgu_b=v_sgu_b, conv_w=v_conv_w, out_norm_g=v_out_norm_g,
                mix_w_out=v_mix_w_out, norm_ffn2_g=v_norm_ffn2_g, ffn2_w_gu=v_ffn2_w_gu, ffn2_w_down=v_ffn2_w_down,
                final_norm_g=v_final_norm_g)

    B, S, D = x.shape
    T = B * S
    L = ada_w.shape[0]
    F = ffn1_w_down.shape[1] * N_CHIP
    P = mix_w_in.shape[2] * N_CHIP
    DA = D // 2
    DB = D - DA
    HD = DA // N_HEADS
    SA = ada_w.shape[2]
    n_all = B * N_DEV
    mx, my, mc = _position()
    chip = 2 * mx + my
    dev = 2 * chip + mc
    core = jnp.reshape(mc, (1,)).astype(jnp.int32)

    big = ["ffn1_w_gu", "ffn1_w_down", "mix_w_in", "mix_w_out", "ffn2_w_gu", "ffn2_w_down"]
    col_sharded = dict(ffn1_w_gu=True, ffn1_w_down=False, mix_w_in=True, mix_w_out=False,
                       ffn2_w_gu=True, ffn2_w_down=False)
    shards = {k: weights[k].astype(BF16) for k in big}
    gather = lambda l, *names: _gather_comm([(shards[k], l, col_sharded[k]) for k in names])
    full = [dict() for _ in range(L)]

    def arrived(l, names, res):
        full[l].update(zip(names, res))

    n_cw = L * conv_w.shape[1]
    cw_block = jnp.pad(conv_w.reshape(n_cw, conv_w.shape[2]), ((0, 8 - n_cw), (0, 0)))
    c_all, cw_all = _comm_call(_merge(_all_gather_comm(c.reshape(8, B * D // 8)), _all_gather_comm(cw_block)),
                               "gather_c")
    c_all = c_all.reshape(n_all, D)
    cw_all = cw_all.reshape(N_CHIP, 2, 8, conv_w.shape[2])[:, 0, :n_cw]
    conv_full = jnp.transpose(cw_all.reshape(N_CHIP, L, conv_w.shape[1], conv_w.shape[2]), (1, 2, 0, 3))
    conv_full = conv_full.reshape(L, conv_w.shape[1], DB)
    ada_b_mine = lax.dynamic_slice_in_dim(ada_b, chip * SA, SA, axis=1).reshape(L, 1, SA)
    c_act, ada_part = _ada_fwd(c_all, ada_w, ada_b_mine)
    ada_all, first_w = _comm_call(_merge(_all_gather_comm(ada_part.reshape(L * n_all, SA)), gather(0, big[0])),
                                  "gather_first")
    arrived(0, big[:1], [first_w])
    ada_all = ada_all.reshape(N_CHIP, 2, L, n_all, SA)[:, 0]
    ada_all = jnp.transpose(ada_all, (1, 2, 0, 3)).reshape(L, n_all, N_CHIP * SA)
    ada = lax.dynamic_slice_in_dim(ada_all, dev * B, B, axis=1).reshape(L, B, N_MOD, 1, D)
    mods = [[ada[l, :, j] for j in range(N_MOD)] for l in range(L)]

    x0 = x.reshape(T, D)
    gains = lambda name, l: weights[name][l].reshape(1, D)
    hmask = jnp.repeat(jnp.eye(N_HEADS, dtype=F32), HD, axis=0)
    pmat = (jnp.repeat(hmask, HD, axis=1) / HD).astype(BF16)

    def mix_consts(l):
        lng = jnp.tile(sgu_ln_g[l], N_HEADS).reshape(1, DA)
        lnb = jnp.tile(sgu_ln_b[l], N_HEADS).reshape(1, DA)
        wst = sgu_w_s[l].reshape(N_HEADS * CHUNK, CHUNK)
        wstt = jnp.swapaxes(sgu_w_s[l], 1, 2).reshape(N_HEADS * CHUNK, CHUNK)
        bias = jnp.repeat(jnp.transpose(sgu_b[l]), HD, axis=1)
        return lng, lnb, wst, wstt, bias

    def fetch(fn, *args, bring=()):
        bring = [(l, k) for l, k in bring if l < L]
        comm = _gather_comm([(shards[k], l, col_sharded[k]) for l, k in bring]) if bring else None
        res, got = fn(*args, comm)
        for (l, k), a in zip(bring, got):
            full[l][k] = a
        return res

    saved = []
    xc = x0
    for l in range(L):
        sh1, sc1, g1, sh2, sc2, g2, sh3, sc3, g3 = mods[l]
        lng, lnb, wst, wstt, bias = mix_consts(l)
        w = full[l]
        if l == 0:
            gu1, a1 = fetch(_ffn_up, xc, gains("norm_ffn1_g", l), sh1, sc1, w["ffn1_w_gu"],
                            bring=[(l, "ffn1_w_down"), (l, "mix_w_in"), (l, "mix_w_out")])
            xa, f1 = fetch(_ffn_down, a1, xc, g1, w["ffn1_w_down"], bring=[(l, "ffn2_w_down")])
        else:
            xa, gu1, a1, f1 = fetch(_ffn_fwd, xc, gains("norm_ffn1_g", l), sh1, sc1, g1, w["ffn1_w_gu"],
                                    w["ffn1_w_down"], bring=[(l, "ffn2_w_gu"), (l, "ffn2_w_down")])
        proj, h2 = fetch(_mixin_fwd, xa, gains("norm_mix_g", l), sh2, sc2, w["mix_w_in"], bring=[(l + 1, "mix_w_in")])
        xb, yn, sv = fetch(_mix_core_fwd, proj, xa, g2, w["mix_w_out"], lng, lnb, wst, bias, pmat, conv_full[l],
                           gains("out_norm_g", l), bring=[(l, "ffn2_w_gu")] if l == 0 else [])
        xd, gu2, a2, f2 = fetch(_ffn_fwd, xb, gains("norm_ffn2_g", l), sh3, sc3, g3, w["ffn2_w_gu"], w["ffn2_w_down"],
                                bring=[(l + 1, "ffn1_w_gu"), (l + 1, "mix_w_out"), (l + 1, "ffn1_w_down")])
        saved.append(dict(x0=xc, xa=xa, xb=xb, gu1=gu1, a1=a1, f1=f1, proj=proj, h2=h2, yn=yn, sv=sv,
                          gu2=gu2, a2=a2, f2=f2))
        xc = xd

    dx, loss_block, d_final = _loss_head(xc, loss_target.reshape(T, D), final_norm_g.reshape(1, D))

    reduced = dict.fromkeys(big)

    def halves(name, g):
        if g.ndim == 4:
            return g
        return g.reshape(N_CHIP, 2, weights[name].shape[1] // 2, g.shape[-1])

    class Reduction:
        def __init__(self, l, name, g):
            self.l, self.name, self.g, self.stage = l, name, halves(name, g), 0
            self.ici_bytes = 3 * (g.size // 8) * 2

        def step(self):
            self.stage += 1
            if self.stage == 1:
                return _sibling_half_comm([self.g])
            if self.stage == 2:
                return _scatter_comm([_pair_sum(self.g, self.got[0], core)])
            if self.stage == 3:
                reduced[self.name] = _chip_sum(self.got[0], core, self.l, L, reduced[self.name])
                return _share_comm([reduced[self.name]], self.l)
            reduced[self.name] = self.got[0]
            return None

    active, extra, gathered = [], [], {}

    def carry(fn, *args, us=None):
        left = None if us is None else us * SCATTER_BYTES_PER_US
        riders = []
        for r in active:
            if r.stage == 1 and left is not None:
                if r.ici_bytes > left * SCATTER_OVERSHOOT:
                    continue
                left -= r.ici_bytes
            riders.append(r)
        comms = [r.step() for r in riders] + [cm for cm, _ in extra]
        takers = [functools.partial(setattr, r, "got") for r in riders] + [cb for _, cb in extra]
        extra.clear()
        if fn is None:
            res, got = None, (_comm_call(_merge(*comms), "reduce_alone") if comms else [])
        else:
            res, got = fn(*args, comm=_merge(*comms))
        at = 0
        for cm, take in zip(comms, takers):
            take(got[at:at + len(cm.out_shape)])
            at += len(cm.out_shape)
        for r in riders:
            if r.stage == 3:
                r.step()
                active.remove(r)
        return res

    def reduce_later(l, name, g):
        active.append(Reduction(l, name, g))

    small = [None] * L
    dwsts = [None] * L
    d_ada = [None] * L
    for l in reversed(range(L)):
        sh1, sc1, g1, sh2, sc2, g2, sh3, sc3, g3 = mods[l]
        lng, lnb, wst, wstt, bias = mix_consts(l)
        s = saved[l]
        w = full[l]
        last = l == 0
        dx, dgu2, h3, df2, dsc3, dsh3, dgain3, dg3 = carry(
            _ffn_bwd, dx, s["xb"], s["gu2"], s["f2"], gains("norm_ffn2_g", l), sh3, sc3, g3, w["ffn2_w_gu"],
            w["ffn2_w_down"], us=170)
        ffn2_grads = [
            lambda: reduce_later(l, "ffn2_w_gu", carry(_wgrad, h3, dgu2, D, 2 * F // N_CHIP, True, "wgrad_gu",
                                                       WGRAD_TOKENS // 2, us=110)[0]),
            lambda: reduce_later(l, "ffn2_w_down", carry(_wgrad, s["a2"], df2[None], F // 2, D, False, "wgrad_down",
                                                         us=50)[0])]
        if not last:
            ffn2_grads[0]()
            ffn2_grads[1]()
        dproj, d_o, dg2, dog, dwst, dbias, dlng, dlnb, dconvw = carry(
            _mix_core_bwd, s["proj"], s["sv"], dx, g2, w["mix_w_out"], lng, lnb, wstt, pmat, conv_full[l],
            gains("out_norm_g", l), us=150)
        mix_grads = [
            lambda: reduce_later(l, "mix_w_out", carry(_wgrad, s["yn"], d_o[None], D, D, False, "wgrad_out", us=30)[0]),
            lambda: reduce_later(l, "mix_w_in", carry(_wgrad, s["h2"], dproj[None], D, P // N_CHIP, True, "wgrad_in",
                                                      us=65)[0])]
        if not last:
            mix_grads[0]()
        dx, dsc2, dsh2, dgain2 = carry(_mixin_bwd, dx, s["xa"], dproj, gains("norm_mix_g", l), sc2, w["mix_w_in"], us=60)
        if not last:
            mix_grads[1]()
        dx, dgu, h1, df, dsc1, dsh1, dgain1, dg1 = carry(
            _ffn_bwd, dx, s["x0"], s["gu1"], s["f1"], gains("norm_ffn1_g", l), sh1, sc1, g1, w["ffn1_w_gu"],
            w["ffn1_w_down"], us=170)
        d_ada[l] = jnp.concatenate([dsh1, dsc1, dg1, dsh2, dsc2, dg2, dsh3, dsc3, dg3], axis=1).reshape(B, N_MOD * D)
        small[l] = [dgain1, dgain2, dgain3, dog, dlng, dlnb, dbias[:, ::HD], dconvw]
        dwsts[l] = dwst
        if last:
            flat = [a.reshape(-1, 128) for ll in range(L) for a in small[ll]]
            flat += [d_final.reshape(-1, 128), loss_block[0:1]]
            pad = (-sum(a.shape[0] for a in flat)) % 8
            packed = jnp.concatenate(flat + [jnp.zeros((pad, 128), F32)], axis=0)
            extra.append((_all_gather_comm(jnp.stack(d_ada).reshape(L * B, N_MOD * D)),
                          lambda got: gathered.update(d_ada=got[0])))
            extra.append((_all_gather_comm(packed), lambda got: gathered.update(small=got[0])))
            for ll in range(L):
                extra.append((_all_gather_comm(dwsts[ll]), lambda got, ll=ll: gathered.update({("dwst", ll): got[0]})))
        reduce_later(l, "ffn1_w_gu", carry(_wgrad, h1, dgu, D, 2 * F // N_CHIP, True, "wgrad_gu", WGRAD_TOKENS // 2,
                                           us=110)[0])
        reduce_later(l, "ffn1_w_down", carry(_wgrad, s["a1"], df[None], F // 2, D, False, "wgrad_down", us=50)[0])
        if last:
            ffn2_grads[0]()
            ffn2_grads[1]()
            mix_grads[1]()
            mix_grads[0]()
    grad_x = dx.reshape(B, S, D)

    def finished(name):
        while any(r.name == name for r in active):
            carry(None)
        return reduced[name].reshape(weights[name].shape)

    grads = {}
    d_ada_all = jnp.transpose(gathered["d_ada"].reshape(N_DEV, L, B, N_MOD * D), (1, 0, 2, 3))
    d_ada_all = d_ada_all.reshape(L, n_all, N_MOD * D)
    grads["ada_b"] = _colsum(d_ada_all).reshape(L, N_MOD * D)
    d_ada_mine = lax.dynamic_slice_in_dim(d_ada_all, chip * SA, SA, axis=2).astype(BF16)
    grads["ada_w"] = _ada_bwd(c_act, d_ada_mine)[0][0]

    total = _sum_blocks(gathered["small"].reshape(-1, 128), N_DEV)
    pieces, at = [], 0
    for a in flat:
        pieces.append(total[at:at + a.shape[0]])
        at += a.shape[0]
    per_layer = len(small[0])
    stack = lambda j, shape: jnp.stack([pieces[l * per_layer + j].reshape(shape) for l in range(L)])
    grads["norm_ffn1_g"] = stack(0, (D,))
    grads["norm_mix_g"] = stack(1, (D,))
    grads["norm_ffn2_g"] = stack(2, (D,))
    grads["out_norm_g"] = stack(3, (D,))
    grads["sgu_ln_g"] = stack(4, (N_HEADS, HD)).sum(axis=1)
    grads["sgu_ln_b"] = stack(5, (N_HEADS, HD)).sum(axis=1)
    grads["sgu_b"] = jnp.swapaxes(stack(6, (CHUNK, N_HEADS)), 1, 2)
    g_conv = stack(7, (8, DB))[:, :conv_w.shape[1]]
    grads["conv_w"] = lax.dynamic_slice_in_dim(g_conv, chip * conv_w.shape[2], conv_w.shape[2], axis=2)
    grads["final_norm_g"] = pieces[-2].reshape(D)
    loss = pieces[-1][0, 0]
    grads["sgu_w_s"] = jnp.stack([_sum_blocks(gathered["dwst", l].reshape(-1, CHUNK), N_DEV) for l in range(L)])
    grads["sgu_w_s"] = grads["sgu_w_s"].reshape(L, N_HEADS, CHUNK, CHUNK)

    names = list(weights)
    delta, new_m, new_v = {}, {}, {}
    for k in big:
        grads[k] = finished(k)
    for k in names:
        wk = weights[k]
        view = (1, wk.shape[0]) if wk.ndim == 1 else (-1, wk.shape[-1])
        d, nm, nv, *g_again = _adamw(wk.reshape(view), grads[k].reshape(view), m_in[k].reshape(view),
                                     v_in[k].reshape(view), emit_grad=k in big)
        delta[k], new_m[k], new_v[k] = d.reshape(wk.shape), nm.reshape(wk.shape), nv.reshape(wk.shape)
        if g_again:
            grads[k] = g_again[0].reshape(wk.shape)

    return (loss, grad_x, *[grads[k] for k in names], *[delta[k] for k in names],
            *[new_m[k] for k in names], *[new_v[k] for k in names])
```

```python
import functools
import math

import jax
import jax.numpy as jnp
from jax import lax
from jax.experimental import pallas as pl
from jax.experimental.pallas import tpu as pltpu

F32 = jnp.float32
BF16 = jnp.bfloat16
MESH = pl.DeviceIdType.MESH

N_HEADS = 8
CHUNK = 128
N_MOD = 9
EPS = 1e-6
N_DEV = 8
N_CHIP = 4

ADAM_LR = 0.001
ADAM_B1 = 0.9
ADAM_B2 = 0.999
ADAM_EPS = 1e-08
ADAM_WD = 0.01
ADAM_STEP = 10

TOKEN_TILE = 512
BWD_TILE = 256
FWD_TILE = 512
FF_SLAB = 768
MIX_TILE = 256
WGRAD_TOKENS = 2048
VMEM_LIMIT = 56 * 1024 * 1024

SCATTER_BYTES_PER_US = 68_000
SCATTER_OVERSHOOT = 1.25

ANY = pl.BlockSpec(memory_space=pl.ANY)


def _tile(pref, n):
    t = min(pref, n)
    assert n % t == 0, (pref, n)
    return t


def _slabs(n, width):
    return [slice(c0, min(c0 + width, n)) for c0 in range(0, n, width)]


def _dot(a, b):
    return jnp.dot(a, b, preferred_element_type=F32)


def _dot_nt(a, b):
    return lax.dot_general(a, b, (((1,), (1,)), ((), ())), preferred_element_type=F32)


def _dot_tn(a, b):
    return lax.dot_general(a, b, (((0,), (0,)), ((), ())), preferred_element_type=F32)


def _sigmoid(x):
    return 1.0 / (1.0 + jnp.exp(-x))


def _sigmoid_fast(x):
    return pl.reciprocal(1.0 + jnp.exp(-x), approx=True)


def _rms(x):
    r = lax.rsqrt(jnp.mean(x * x, axis=-1, keepdims=True) + EPS)
    return x * r, r


def _norm_mod_bwd(x, dh, gain, sc):
    xh, r = _rms(x)
    dsc = jnp.sum(dh * (xh * gain), axis=0, keepdims=True)
    dsh = jnp.sum(dh, axis=0, keepdims=True)
    dn = dh * (1.0 + sc)
    dgain = jnp.sum(dn * xh, axis=0, keepdims=True)
    dy = dn * gain
    dx = r * (dy - xh * jnp.mean(dy * xh, axis=-1, keepdims=True))
    return dx, dsc, dsh, dgain


def _acc(ref, first, val):
    @pl.when(first)
    def _():
        ref[...] = val

    @pl.when(jnp.logical_not(first))
    def _():
        ref[...] += val


class _Comm:
    def __init__(self, args, out_shape, scratch, phases, aliases=None):
        self.args, self.out_shape, self.scratch = list(args), list(out_shape), list(scratch)
        self.phases, self.aliases = phases, dict(aliases or {})


def _merge(*comms):
    comms = [c for c in comms if c is not None]
    if len(comms) <= 1:
        return comms[0] if comms else None
    args = [a for c in comms for a in c.args]
    out_shape = [o for c in comms for o in c.out_shape]
    scratch = [s for c in comms for s in c.scratch]
    aliases, ai, oi = {}, 0, 0
    for c in comms:
        aliases.update({ai + i: oi + o for i, o in c.aliases.items()})
        ai += len(c.args)
        oi += len(c.out_shape)

    def phases(ins, outs, sems):
        parts, ai, oi, si = [], 0, 0, 0
        for c in comms:
            parts.append(c.phases(ins[ai:ai + len(c.args)], outs[oi:oi + len(c.out_shape)], sems[si:si + len(c.scratch)]))
            ai, oi, si = ai + len(c.args), oi + len(c.out_shape), si + len(c.scratch)

        def run(k):
            def go():
                for p in parts:
                    if p[k] is not None:
                        p[k]()
            return go
        return run(0), run(1), run(2)

    return _Comm(args, out_shape, scratch, phases, aliases)


def _call(body, name, grid, in_specs, out_specs, out_shape, scratch, args, comm=None):
    n_in, n_out, n_scr = len(in_specs), len(out_specs), len(scratch)
    sem = ("arbitrary",) * len(grid)
    params = pltpu.CompilerParams(dimension_semantics=sem, vmem_limit_bytes=VMEM_LIMIT)
    if comm is None:
        res = pl.pallas_call(body, name=name, grid=grid, in_specs=in_specs, out_specs=out_specs, out_shape=out_shape,
                             scratch_shapes=scratch, compiler_params=params)(*args)
        return list(res), []
    m_in, m_out = len(comm.args), len(comm.out_shape)

    def full(*refs):
        c_in, c_min = refs[:n_in], refs[n_in:n_in + m_in]
        o = n_in + m_in
        c_out, c_mout = refs[o:o + n_out], refs[o + n_out:o + n_out + m_out]
        o += n_out + m_out
        c_scr, c_sem = refs[o:o + n_scr], refs[o + n_scr:]
        start, mid, finish = comm.phases(c_min, c_mout, c_sem)
        ids = [pl.program_id(a) for a in range(len(grid))]
        first = functools.reduce(jnp.logical_and, [i == 0 for i in ids])
        last = functools.reduce(jnp.logical_and, [i == g - 1 for i, g in zip(ids, grid)])
        pl.when(first)(start)
        if mid is not None:
            pl.when(last)(mid)
        body(*c_in, *c_out, *c_scr)
        pl.when(last)(finish)

    res = pl.pallas_call(
        full, name=name, grid=grid,
        in_specs=list(in_specs) + [ANY] * m_in,
        out_specs=list(out_specs) + [ANY] * m_out,
        out_shape=list(out_shape) + comm.out_shape,
        scratch_shapes=list(scratch) + comm.scratch,
        input_output_aliases={n_in + i: n_out + o for i, o in comm.aliases.items()},
        compiler_params=params,
    )(*args, *comm.args)
    return list(res[:n_out]), list(res[n_out:])


def _comm_call(comm, name):
    m_in, m_out = len(comm.args), len(comm.out_shape)

    def body(*refs):
        start, mid, finish = comm.phases(refs[:m_in], refs[m_in:m_in + m_out], refs[m_in + m_out:])
        start()
        if mid is not None:
            mid()
        finish()

    res = pl.pallas_call(
        body, name=name, in_specs=[ANY] * m_in, out_specs=[ANY] * m_out, out_shape=comm.out_shape,
        scratch_shapes=comm.scratch, input_output_aliases=comm.aliases,
    )(*comm.args)
    return list(res)


def _position():
    return lax.axis_index("x"), lax.axis_index("y"), lax.axis_index("c")


def _gather_comm(items):
    n = len(items)
    half = [s.shape[1] // 2 for s, _, _ in items]

    def full_shape(i):
        s, _, col = items[i]
        _, R, C = s.shape
        return jax.ShapeDtypeStruct((R, N_CHIP * C) if col else (N_CHIP * R, C), s.dtype)

    def phases(ins, outs, sems):
        send_sems, recv_sems, local_sems = sems
        x, y, c = _position()

        def region(i, chip, h):
            s, _, col = items[i]
            _, R, C = s.shape
            if col:
                return outs[i].at[pl.ds(h * half[i], half[i]), pl.ds(chip * C, C)]
            return outs[i].at[pl.ds(chip * R + h * half[i], half[i]), :]

        def mine(i, h):
            return ins[i].at[items[i][1], pl.ds(h * half[i], half[i]), :]

        def copies(kx, ky, kc):
            k_me = 2 * kx + ky
            sibling = (kx, ky, 1 - kc)
            chips = [(1 - kx, ky), (kx, 1 - ky), (1 - kx, 1 - ky)]
            local, first, passed, arrive_ici, arrive_d2d = [], [], [], [], []

            def remote(src, dst, s, to):
                return pltpu.make_async_remote_copy(src_ref=src, dst_ref=dst, send_sem=send_sems.at[s],
                                                    recv_sem=recv_sems.at[s], device_id=to, device_id_type=MESH)

            for i in range(n):
                for h in range(2):
                    local.append(pltpu.make_async_copy(mine(i, h), region(i, k_me, h), local_sems.at[2 * i + h]))
                for j, (px, py) in enumerate(chips):
                    s = 6 * i + j
                    first.append(remote(mine(i, kc), region(i, k_me, kc), s, (px, py, kc)))
                    got = region(i, 2 * px + py, kc)
                    arrive_ici.append(remote(got, got, s, (px, py, kc)))
                    passed.append(remote(got, got, s + 3, sibling))
                    other = region(i, 2 * px + py, 1 - kc)
                    arrive_d2d.append(remote(other, other, s + 3, sibling))
            return local, first, passed, arrive_ici, arrive_d2d

        def on_each_device(fn):
            def go():
                for kx in range(2):
                    for ky in range(2):
                        for kc in range(2):
                            pl.when((x == kx) & (y == ky) & (c == kc))(functools.partial(fn, *copies(kx, ky, kc)))
            return go

        def start(local, first, passed, arrive_ici, arrive_d2d):
            for cp in local + first:
                cp.start()

        def mid(local, first, passed, arrive_ici, arrive_d2d):
            for a, p in zip(arrive_ici, passed):
                a.wait_recv()
                p.start()

        def finish(local, first, passed, arrive_ici, arrive_d2d):
            for a in arrive_d2d:
                a.wait_recv()
            for cp in first + passed:
                cp.wait_send()
            for cp in local:
                cp.wait()

        return on_each_device(start), on_each_device(mid), on_each_device(finish)

    scratch = [pltpu.SemaphoreType.DMA((6 * n,)), pltpu.SemaphoreType.DMA((6 * n,)), pltpu.SemaphoreType.DMA((2 * n,))]
    return _Comm([s for s, _, _ in items], [full_shape(i) for i in range(n)], scratch, phases)


def _sibling_half_comm(gs):
    n = len(gs)

    def phases(ins, outs, sems):
        send_sems, recv_sems = sems
        x, y, c = _position()

        def copies():
            return [pltpu.make_async_remote_copy(
                src_ref=ins[i].at[:, 1 - c], dst_ref=outs[i], send_sem=send_sems.at[i], recv_sem=recv_sems.at[i],
                device_id=(x, y, 1 - c), device_id_type=MESH) for i in range(n)]

        def start():
            for cp in copies():
                cp.start()

        def finish():
            for cp in copies():
                cp.wait()

        return start, None, finish

    out_shape = [jax.ShapeDtypeStruct(g.shape[:1] + g.shape[2:], g.dtype) for g in gs]
    return _Comm(gs, out_shape, [pltpu.SemaphoreType.DMA((n,)), pltpu.SemaphoreType.DMA((n,))], phases)


def _scatter_comm(ps):
    n = len(ps)

    def phases(ins, outs, sems):
        send_sems, recv_sems, local_sems = sems
        x, y, c = _position()
        k_me = 2 * x + y
        chips = [(1 - x, y), (x, 1 - y), (1 - x, 1 - y)]

        def copies():
            local = [pltpu.make_async_copy(ins[i].at[k_me], outs[i].at[k_me], local_sems.at[i]) for i in range(n)]
            remote = [pltpu.make_async_remote_copy(
                src_ref=ins[i].at[2 * px + py], dst_ref=outs[i].at[k_me],
                send_sem=send_sems.at[3 * i + j], recv_sem=recv_sems.at[3 * i + j],
                device_id=(px, py, c), device_id_type=MESH) for i in range(n) for j, (px, py) in enumerate(chips)]
            return local, remote

        def start():
            local, remote = copies()
            for cp in local + remote:
                cp.start()

        def finish():
            local, remote = copies()
            for cp in remote + local:
                cp.wait()

        return start, None, finish

    scratch = [pltpu.SemaphoreType.DMA((3 * n,)), pltpu.SemaphoreType.DMA((3 * n,)), pltpu.SemaphoreType.DMA((n,))]
    return _Comm(ps, [jax.ShapeDtypeStruct(p.shape, p.dtype) for p in ps], scratch, phases)


def _share_comm(rs, l):
    n = len(rs)

    def phases(ins, outs, sems):
        send_sems, recv_sems = sems
        x, y, c = _position()

        def copy(i, h):
            return pltpu.make_async_remote_copy(
                src_ref=outs[i].at[l, h], dst_ref=outs[i].at[l, h], send_sem=send_sems.at[i], recv_sem=recv_sems.at[i],
                device_id=(x, y, 1 - c), device_id_type=MESH)

        def start():
            for i in range(n):
                copy(i, c).start()

        def finish():
            for i in range(n):
                copy(i, 1 - c).wait_recv()
            for i in range(n):
                copy(i, c).wait_send()

        return start, None, finish

    return _Comm(rs, [jax.ShapeDtypeStruct(r.shape, r.dtype) for r in rs],
                 [pltpu.SemaphoreType.DMA((n,)), pltpu.SemaphoreType.DMA((n,))], phases,
                 aliases={i: i for i in range(n)})


def _all_gather_comm(block):
    def phases(ins, outs, sems):
        send_sems, recv_sems, local_sem = sems
        (src,), (out,) = ins, outs
        x, y, c = _position()
        sibling = (x, y, 1 - c)
        chips = [(1 - x, y), (x, 1 - y), (1 - x, 1 - y)]

        def slot(px, py, pc):
            return out.at[4 * px + 2 * py + pc]

        def copy(k, blk, to, own=False):
            return pltpu.make_async_remote_copy(
                src_ref=src if own else slot(*blk), dst_ref=slot(*blk),
                send_sem=send_sems.at[k], recv_sem=recv_sems.at[k], device_id=to, device_id_type=MESH)

        mine = lambda: pltpu.make_async_copy(src, slot(x, y, c), local_sem.at[0])
        first = lambda: [copy(0, (x, y, c), sibling, True)] + [
            copy(1 + j, (x, y, c), (*chip, c), True) for j, chip in enumerate(chips)]
        passed = lambda: [copy(4 + j, (*chip, c), sibling) for j, chip in enumerate(chips)]

        def start():
            mine().start()
            for cp in first():
                cp.start()

        def mid():
            for j, (chip, p) in enumerate(zip(chips, passed())):
                copy(1 + j, (*chip, c), (x, y, c)).wait_recv()
                p.start()

        def finish():
            copy(0, sibling, (x, y, c)).wait_recv()
            for j, chip in enumerate(chips):
                copy(4 + j, (*chip, 1 - c), (x, y, c)).wait_recv()
            for cp in first() + passed():
                cp.wait_send()
            mine().wait()

        return start, mid, finish

    scratch = [pltpu.SemaphoreType.DMA((7,)), pltpu.SemaphoreType.DMA((7,)), pltpu.SemaphoreType.DMA((1,))]
    return _Comm([block], [jax.ShapeDtypeStruct((N_DEV,) + block.shape, block.dtype)], scratch, phases)


def _ffn_up(x, gain, sh, sc, wgu, comm=None):
    T, D = x.shape
    F = wgu.shape[1] // 2
    B = sh.shape[0]
    tm = _tile(TOKEN_TILE, T // B)
    tps = (T // B) // tm
    slabs = _slabs(F, FF_SLAB)

    def body(x_ref, gain_ref, sh_ref, sc_ref, w_ref, gu_ref, a_ref):
        xh, _ = _rms(x_ref[...])
        h = (xh * gain_ref[...] * (1.0 + sc_ref[0]) + sh_ref[0]).astype(BF16)

        def dots(s):
            return _dot(h, w_ref[:, s]), _dot(h, w_ref[:, slice(F + s.start, F + s.stop)])

        nxt = dots(slabs[0])
        for j, s in enumerate(slabs):
            g, u = nxt
            if j + 1 < len(slabs):
                nxt = dots(slabs[j + 1])
            gu_ref[0, :, s] = g.astype(BF16)
            gu_ref[1, :, s] = u.astype(BF16)
            a_ref[:, s] = (g * _sigmoid(g) * u).astype(BF16)

    seq = lambda i: (i // tps, 0, 0)
    return _call(
        body, "ffn_up", (T // tm,),
        [
            pl.BlockSpec((tm, D), lambda i: (i, 0)),
            pl.BlockSpec((1, D), lambda i: (0, 0)),
            pl.BlockSpec((1, 1, D), seq),
            pl.BlockSpec((1, 1, D), seq),
            pl.BlockSpec((D, 2 * F), lambda i: (0, 0), pipeline_mode=pl.Buffered(1)),
        ],
        [
            pl.BlockSpec((2, tm, F), lambda i: (0, i, 0)),
            pl.BlockSpec((tm, F), lambda i: (i, 0)),
        ],
        [
            jax.ShapeDtypeStruct((2, T, F), BF16),
            jax.ShapeDtypeStruct((T, F), BF16),
        ],
        [],
        (x, gain, sh, sc, wgu), comm)


def _ffn_down(a, x, gate, wd, comm=None):
    T, F = a.shape
    D = x.shape[1]
    B = gate.shape[0]
    tm = _tile(2 * TOKEN_TILE, T // B)
    tps = (T // B) // tm

    def body(a_ref, x_ref, gate_ref, wd_ref, xo_ref, f_ref):
        f = _dot(a_ref[...], wd_ref[...])
        f_ref[...] = f.astype(BF16)
        xo_ref[...] = x_ref[...] + 0.5 * gate_ref[0] * f

    return _call(
        body, "ffn_down", (T // tm,),
        [
            pl.BlockSpec((tm, F), lambda i: (i, 0)),
            pl.BlockSpec((tm, D), lambda i: (i, 0)),
            pl.BlockSpec((1, 1, D), lambda i: (i // tps, 0, 0)),
            pl.BlockSpec((F, D), lambda i: (0, 0), pipeline_mode=pl.Buffered(1)),
        ],
        [pl.BlockSpec((tm, D), lambda i: (i, 0)), pl.BlockSpec((tm, D), lambda i: (i, 0))],
        [jax.ShapeDtypeStruct((T, D), F32), jax.ShapeDtypeStruct((T, D), BF16)],
        [],
        (a, x, gate, wd), comm)


def _ffn_fwd(x, gain, sh, sc, gate, wgu, wd, comm=None):
    T, D = x.shape
    F = wd.shape[0]
    B = sh.shape[0]
    tm = _tile(FWD_TILE, T // B)
    tps = (T // B) // tm
    slabs = _slabs(F, FF_SLAB)

    def body(x_ref, gain_ref, sh_ref, sc_ref, gate_ref, w_ref, wd_ref, xo_ref, gu_ref, a_ref, f_ref):
        x = x_ref[...]
        h = (_rms(x)[0] * gain_ref[...] * (1.0 + sc_ref[0]) + sh_ref[0]).astype(BF16)

        def dots(s):
            return _dot(h, w_ref[:, s]), _dot(h, w_ref[:, slice(F + s.start, F + s.stop)])

        nxt = dots(slabs[0])
        for j, s in enumerate(slabs):
            g, u = nxt
            if j + 1 < len(slabs):
                nxt = dots(slabs[j + 1])
            gu_ref[0, :, s] = g.astype(BF16)
            gu_ref[1, :, s] = u.astype(BF16)
            a_ref[:, s] = (g * _sigmoid(g) * u).astype(BF16)
        f = _dot(a_ref[...], wd_ref[...])
        f_ref[...] = f.astype(BF16)
        xo_ref[...] = x + 0.5 * gate_ref[0] * f

    seq = lambda i: (i // tps, 0, 0)
    row = lambda i: (i, 0)
    return _call(
        body, "ffn_fwd", (T // tm,),
        [
            pl.BlockSpec((tm, D), row),
            pl.BlockSpec((1, D), lambda i: (0, 0)),
            pl.BlockSpec((1, 1, D), seq),
            pl.BlockSpec((1, 1, D), seq),
            pl.BlockSpec((1, 1, D), seq),
            pl.BlockSpec((D, 2 * F), lambda i: (0, 0), pipeline_mode=pl.Buffered(1)),
            pl.BlockSpec((F, D), lambda i: (0, 0), pipeline_mode=pl.Buffered(1)),
        ],
        [
            pl.BlockSpec((tm, D), row),
            pl.BlockSpec((2, tm, F), lambda i: (0, i, 0)),
            pl.BlockSpec((tm, F), row),
            pl.BlockSpec((tm, D), row),
        ],
        [
            jax.ShapeDtypeStruct((T, D), F32),
            jax.ShapeDtypeStruct((2, T, F), BF16),
            jax.ShapeDtypeStruct((T, F), BF16),
            jax.ShapeDtypeStruct((T, D), BF16),
        ],
        [],
        (x, gain, sh, sc, gate, wgu, wd), comm)


def _ffn_bwd(dxo, x, gu, f, gain, sh, sc, gate, wgu, wd, comm=None):
    T, D = x.shape
    F = wd.shape[0]
    B = sc.shape[0]
    tm = _tile(BWD_TILE, T // B)
    tps = (T // B) // tm
    slabs = _slabs(F, FF_SLAB)

    def body(dxo_ref, x_ref, gu_ref, f_ref, gain_ref, sh_ref, sc_ref, gate_ref, w_ref, wd_ref,
             dx_ref, dgu_ref, h_ref, df_ref, dsc_ref, dsh_ref, dgain_ref, dgate_ref):
        i = pl.program_id(0)
        first_of_seq = (i % tps) == 0
        gain = gain_ref[...]
        sc = sc_ref[0]
        dxo = dxo_ref[...]
        x = x_ref[...]
        df = (0.5 * gate_ref[0] * dxo).astype(BF16)
        df_ref[...] = df
        nxt = _dot_nt(df, wd_ref[slabs[0], :])
        for j, s in enumerate(slabs):
            da = nxt
            if j + 1 < len(slabs):
                nxt = _dot_nt(df, wd_ref[slabs[j + 1], :])
            g = gu_ref[0, :, s]
            sg = 1.0 / (1.0 + jnp.exp(-g))
            t = g * sg
            dab = da.astype(BF16)
            dgu_ref[1, :, s] = dab * t
            dgu_ref[0, :, s] = dab * gu_ref[1, :, s] * (sg + t - t * sg)
        dh = _dot_nt(dgu_ref[0], w_ref[:, 0:F]) + _dot_nt(dgu_ref[1], w_ref[:, F:])
        dx, dsc, dsh, dgain = _norm_mod_bwd(x, dh, gain, sc)
        dx_ref[...] = dxo + dx
        h_ref[...] = (_rms(x)[0] * gain * (1.0 + sc) + sh_ref[0]).astype(BF16)
        _acc(dsc_ref.at[0], first_of_seq, dsc)
        _acc(dsh_ref.at[0], first_of_seq, dsh)
        _acc(dgain_ref, i == 0, dgain)
        _acc(dgate_ref.at[0], first_of_seq, 0.5 * jnp.sum(dxo * f_ref[...].astype(F32), axis=0, keepdims=True))

    seq = lambda i: (i // tps, 0, 0)
    row = lambda i: (i, 0)
    return _call(
        body, "ffn_bwd", (T // tm,),
        [
            pl.BlockSpec((tm, D), row),
            pl.BlockSpec((tm, D), row),
            pl.BlockSpec((2, tm, F), lambda i: (0, i, 0)),
            pl.BlockSpec((tm, D), row),
            pl.BlockSpec((1, D), lambda i: (0, 0)),
            pl.BlockSpec((1, 1, D), seq),
            pl.BlockSpec((1, 1, D), seq),
            pl.BlockSpec((1, 1, D), seq),
            pl.BlockSpec((D, 2 * F), lambda i: (0, 0), pipeline_mode=pl.Buffered(1)),
            pl.BlockSpec((F, D), lambda i: (0, 0), pipeline_mode=pl.Buffered(1)),
        ],
        [
            pl.BlockSpec((tm, D), row),
            pl.BlockSpec((2, tm, F), lambda i: (0, i, 0)),
            pl.BlockSpec((tm, D), row),
            pl.BlockSpec((tm, D), row),
            pl.BlockSpec((1, 1, D), seq),
            pl.BlockSpec((1, 1, D), seq),
            pl.BlockSpec((1, D), lambda i: (0, 0)),
            pl.BlockSpec((1, 1, D), seq),
        ],
        [
            jax.ShapeDtypeStruct((T, D), F32),
            jax.ShapeDtypeStruct((2, T, F), BF16),
            jax.ShapeDtypeStruct((T, D), BF16),
            jax.ShapeDtypeStruct((T, D), BF16),
            jax.ShapeDtypeStruct((B, 1, D), F32),
            jax.ShapeDtypeStruct((B, 1, D), F32),
            jax.ShapeDtypeStruct((1, D), F32),
            jax.ShapeDtypeStruct((B, 1, D), F32),
        ],
        [],
        (dxo, x, gu, f, gain, sh, sc, gate, wgu, wd), comm)


def _wgrad(a, b, tmm, tn, col_major, name, tokens=WGRAD_TOKENS, comm=None):
    T, M = a.shape
    nb, _, Nb = b.shape
    N = nb * Nb
    tk = _tile(tokens, T)
    span = 2 if col_major else 1
    wide = span * tn
    npb = Nb // wide
    assert M % tmm == 0 and Nb % wide == 0
    if col_major:
        assert tmm == M
        shape = (N // tn, 2, M // 2, tn)
        out_spec = pl.BlockSpec((span, 2, M // 2, tn), lambda i, j, t: (j, 0, 0, 0))
    else:
        shape = (M // tmm, tmm, N)
        out_spec = pl.BlockSpec((None, tmm, tn), lambda i, j, t: (i, 0, j))

    def body(a_ref, b_ref, o_ref):
        @pl.when(pl.program_id(2) == 0)
        def _():
            o_ref[...] = jnp.zeros_like(o_ref)

        res = _dot_tn(a_ref[...], b_ref[...])
        if col_major:
            for s in range(span):
                for h in range(2):
                    o_ref[s, h] += res[h * (M // 2):(h + 1) * (M // 2), s * tn:(s + 1) * tn]
        else:
            o_ref[...] += res

    return _call(
        body, name, (M // tmm, N // wide, T // tk),
        [
            pl.BlockSpec((tk, tmm), lambda i, j, t: (t, i)),
            pl.BlockSpec((None, tk, wide), lambda i, j, t: (j // npb, t, j % npb)),
        ],
        [out_spec], [jax.ShapeDtypeStruct(shape, F32)], [],
        (a, b), comm)


def _mixin_fwd(x, gain, sh, sc, win, comm=None):
    T, D = x.shape
    P = win.shape[1]
    B = sh.shape[0]
    tm = _tile(TOKEN_TILE, T // B)
    tps = (T // B) // tm

    def body(x_ref, gain_ref, sh_ref, sc_ref, w_ref, proj_ref, h_ref):
        xh, _ = _rms(x_ref[...])
        h = (xh * gain_ref[...] * (1.0 + sc_ref[0]) + sh_ref[0]).astype(BF16)
        h_ref[...] = h
        proj_ref[...] = _dot(h, w_ref[...])

    seq = lambda i: (i // tps, 0, 0)
    return _call(
        body, "mixin_fwd", (T // tm,),
        [
            pl.BlockSpec((tm, D), lambda i: (i, 0)),
            pl.BlockSpec((1, D), lambda i: (0, 0)),
            pl.BlockSpec((1, 1, D), seq),
            pl.BlockSpec((1, 1, D), seq),
            pl.BlockSpec((D, P), lambda i: (0, 0)),
        ],
        [pl.BlockSpec((tm, P), lambda i: (i, 0)), pl.BlockSpec((tm, D), lambda i: (i, 0))],
        [jax.ShapeDtypeStruct((T, P), F32), jax.ShapeDtypeStruct((T, D), BF16)],
        [],
        (x, gain, sh, sc, win), comm)


def _mixin_bwd(dxo, x, dproj, gain, sc, win, comm=None):
    T, D = x.shape
    P = win.shape[1]
    B = sc.shape[0]
    tm = _tile(TOKEN_TILE, T // B)
    tps = (T // B) // tm

    def body(dxo_ref, x_ref, dp_ref, gain_ref, sc_ref, w_ref, dx_ref, dsc_ref, dsh_ref, dgain_ref):
        i = pl.program_id(0)
        first_of_seq = (i % tps) == 0
        halves = _slabs(tm, tm // 2)
        nxt = _dot_nt(dp_ref[halves[0], :], w_ref[...])
        sums = None
        for j, r in enumerate(halves):
            dh = nxt
            if j + 1 < len(halves):
                nxt = _dot_nt(dp_ref[halves[j + 1], :], w_ref[...])
            part = _norm_mod_bwd(x_ref[r, :], dh, gain_ref[...], sc_ref[0])
            dx_ref[r, :] = dxo_ref[r, :] + part[0]
            sums = part[1:] if sums is None else tuple(a + b for a, b in zip(sums, part[1:]))
        _acc(dsc_ref.at[0], first_of_seq, sums[0])
        _acc(dsh_ref.at[0], first_of_seq, sums[1])
        _acc(dgain_ref, i == 0, sums[2])

    seq = lambda i: (i // tps, 0, 0)
    row = lambda i: (i, 0)
    return _call(
        body, "mixin_bwd", (T // tm,),
        [
            pl.BlockSpec((tm, D), row),
            pl.BlockSpec((tm, D), row),
            pl.BlockSpec((tm, P), row),
            pl.BlockSpec((1, D), lambda i: (0, 0)),
            pl.BlockSpec((1, 1, D), seq),
            pl.BlockSpec((D, P), lambda i: (0, 0)),
        ],
        [
            pl.BlockSpec((tm, D), row),
            pl.BlockSpec((1, 1, D), seq),
            pl.BlockSpec((1, 1, D), seq),
            pl.BlockSpec((1, D), lambda i: (0, 0)),
        ],
        [
            jax.ShapeDtypeStruct((T, D), F32),
            jax.ShapeDtypeStruct((B, 1, D), F32),
            jax.ShapeDtypeStruct((B, 1, D), F32),
            jax.ShapeDtypeStruct((1, D), F32),
        ],
        [],
        (dxo, x, dproj, gain, sc, win), comm)


def _head_mean(z, pmat, exact=True):
    hi = z.astype(BF16)
    if not exact:
        return _dot(hi, pmat)
    lo = (z - hi.astype(F32)).astype(BF16)
    return _dot(hi, pmat) + _dot(lo, pmat)


def _gelu_parts(x):
    cdf = 0.5 * (1.0 + lax.erf(x * (1.0 / math.sqrt(2.0))))
    return x * cdf, cdf


def _gelu_grad(x, cdf):
    return cdf + x * jnp.exp(-0.5 * x * x) * (1.0 / math.sqrt(2.0 * math.pi))


LANES = 128


def _head_blocks(da):
    hd = da // N_HEADS
    lb = min(LANES, da)
    col = lax.broadcasted_iota(jnp.int32, (1, lb), 1)
    return lb, lb // hd, da // lb, [(col >= h * hd) & (col < (h + 1) * hd) for h in range(lb // hd)]


def _mix_heads(w_stack, v, da):
    lb, hpb, nb, masks = _head_blocks(da)
    outs = []
    for b in range(nb):
        res = _dot(w_stack[b * hpb * CHUNK:(b + 1) * hpb * CHUNK], v[:, b * lb:(b + 1) * lb])
        out = res[0:CHUNK]
        for h in range(1, hpb):
            out = jnp.where(masks[h], res[h * CHUNK:(h + 1) * CHUNK], out)
        outs.append(out)
    return outs[0] if nb == 1 else jnp.concatenate(outs, axis=1)


def _mix_heads_grad(dm, v, da):
    lb, hpb, nb, masks = _head_blocks(da)
    outs = []
    for b in range(nb):
        dmb = dm[:, b * lb:(b + 1) * lb]
        stack = jnp.concatenate([jnp.where(masks[h], dmb, jnp.zeros_like(dmb)) for h in range(hpb)], axis=0)
        outs.append(_dot_nt(stack, v[:, b * lb:(b + 1) * lb]))
    return outs[0] if nb == 1 else jnp.concatenate(outs, axis=0)


def _causal_stack(w, transposed):
    r = lax.broadcasted_iota(jnp.int32, w.shape, 0) % CHUNK
    c = lax.broadcasted_iota(jnp.int32, w.shape, 1)
    keep = (c >= r) if transposed else (c <= r)
    return jnp.where(keep, w, 0.0)


def _mix_core_forward(proj, zprev, prm, da, db, saved=None):
    n = proj.shape[0]
    ua = proj[:, 0:da]
    va = proj[:, da:2 * da]
    bg = proj[:, 2 * da:2 * da + db]
    cg = proj[:, 2 * da + db:2 * da + 2 * db]
    xb = proj[:, 2 * da + 2 * db:]
    if saved is None:
        ug, ucdf = _gelu_parts(ua)
        vg, vcdf = _gelu_parts(va)
        zc = vg - _head_mean(vg, prm["pmat"])
        rs = lax.rsqrt(_head_mean(zc * zc, prm["pmat"], exact=False) + EPS)
        vhat = zc * rs
        vln = (vhat * prm["lng"] + prm["lnb"]).astype(BF16)
        wst = _causal_stack(prm["wst"], False).astype(BF16)
        mixed = [_mix_heads(wst, vln[j * CHUNK:(j + 1) * CHUNK], da) + prm["bias"] for j in range(n // CHUNK)]
        mixed = mixed[0] if len(mixed) == 1 else jnp.concatenate(mixed, axis=0)
    else:
        ucdf, vcdf, vhat, rs, mixed = [saved[k].astype(F32) for k in range(5)]
        ug = ua * ucdf
        vln = (vhat * prm["lng"] + prm["lnb"]).astype(BF16)
    ya = ug * mixed
    z = cg * xb
    row = lax.broadcasted_iota(jnp.int32, z.shape, 0)
    z1 = jnp.where(row == 0, zprev[7:8], pltpu.roll(z, 1, 0))
    z2 = jnp.where(row == 0, zprev[6:7], jnp.where(row == 1, zprev[7:8], pltpu.roll(z, 2, 0)))
    cw = prm["convw"]
    conv = z2 * cw[0:1] + z1 * cw[1:2] + z * cw[2:3]
    yb = bg * conv
    yah, ra = _rms(ya)
    ybh, rb = _rms(yb)
    return dict(ua=ua, va=va, bg=bg, cg=cg, xb=xb, ug=ug, ucdf=ucdf, vcdf=vcdf, rs=rs, vhat=vhat, vln=vln,
                mixed=mixed, z=z, z1=z1, z2=z2, conv=conv, yah=yah, ra=ra, ybh=ybh, rb=rb)


def _mix_params(lng_ref, lnb_ref, wst_ref, bias_ref, pmat_ref, convw_ref):
    return dict(lng=lng_ref[...], lnb=lnb_ref[...], wst=wst_ref[...], bias=bias_ref[...],
                pmat=pmat_ref[...], convw=convw_ref[...])


def _mix_fwd(x, gain, sh, sc, gate, win, wout, lng, lnb, wst, bias, pmat, convw, og, comm=None):
    T, D = x.shape
    P = win.shape[1]
    B = gate.shape[0]
    da = lng.shape[1]
    db = convw.shape[1]
    tm = _tile(MIX_TILE, T // B)
    tps = (T // B) // tm

    def body(x_ref, gain_ref, sh_ref, sc_ref, gate_ref, win_ref, wout_ref, lng_ref, lnb_ref, wst_ref, bias_ref,
             pmat_ref, convw_ref, og_ref, xo_ref, proj_ref, h_ref, yn_ref, sv_ref, halo):
        i = pl.program_id(0)

        @pl.when((i % tps) == 0)
        def _():
            halo[...] = jnp.zeros_like(halo)

        h = (_rms(x_ref[...])[0] * gain_ref[...] * (1.0 + sc_ref[0]) + sh_ref[0]).astype(BF16)
        h_ref[...] = h
        proj_ref[...] = _dot(h, win_ref[...])
        prm = _mix_params(lng_ref, lnb_ref, wst_ref, bias_ref, pmat_ref, convw_ref)
        r = _mix_core_forward(proj_ref[...], halo[...], prm, da, db)
        halo[...] = r["z"][tm - 8:tm]
        for k, name in enumerate(("ucdf", "vcdf", "vhat", "rs", "mixed")):
            sv_ref[k] = r[name].astype(BF16)
        og = og_ref[...]
        yn_ref[:, 0:da] = (r["yah"] * og[:, 0:da]).astype(BF16)
        yn_ref[:, da:] = (r["ybh"] * og[:, da:]).astype(BF16)
        xo_ref[...] = x_ref[...] + gate_ref[0] * _dot(yn_ref[...], wout_ref[...])

    full = lambda a: pl.BlockSpec(a.shape, lambda i: (0,) * a.ndim)
    seq = lambda i: (i // tps, 0, 0)
    row = lambda i: (i, 0)
    return _call(
        body, "mix_fwd", (T // tm,),
        [
            pl.BlockSpec((tm, D), row),
            pl.BlockSpec((1, D), lambda i: (0, 0)),
            pl.BlockSpec((1, 1, D), seq),
            pl.BlockSpec((1, 1, D), seq),
            pl.BlockSpec((1, 1, D), seq),
            pl.BlockSpec((D, P), lambda i: (0, 0), pipeline_mode=pl.Buffered(1)),
            full(wout), full(lng), full(lnb), full(wst), full(bias), full(pmat), full(convw), full(og),
        ],
        [pl.BlockSpec((tm, D), row), pl.BlockSpec((tm, P), row), pl.BlockSpec((tm, D), row),
         pl.BlockSpec((tm, D), row), pl.BlockSpec((5, tm, da), lambda i: (0, i, 0))],
        [jax.ShapeDtypeStruct((T, D), F32), jax.ShapeDtypeStruct((T, P), F32), jax.ShapeDtypeStruct((T, D), BF16),
         jax.ShapeDtypeStruct((T, D), BF16), jax.ShapeDtypeStruct((5, T, da), BF16)],
        [pltpu.VMEM((8, db), F32)],
        (x, gain, sh, sc, gate, win, wout, lng, lnb, wst, bias, pmat, convw, og), comm)


def _mix_core_bwd(proj, sv, dxo, gate, wout, lng, lnb, wstt, pmat, convw, og, comm=None):
    T, P = proj.shape
    D = dxo.shape[1]
    B = gate.shape[0]
    da = lng.shape[1]
    db = convw.shape[1]
    assert da == db and P == 2 * da + 3 * db
    tm = _tile(MIX_TILE, T // B)
    tps = (T // B) // tm
    nt = T // tm
    hd = da // N_HEADS

    def body(proj_ref, cgp_ref, xbp_ref, sv_ref, dxo_ref, gate_ref, wout_ref, lng_ref, lnb_ref, wstt_ref,
             pmat_ref, convw_ref, og_ref,
             dproj_ref, do_ref, dgate_ref, dog_ref, dwst_ref, dbias_ref, dlng_ref, dlnb_ref, dconvw_ref, carry):
        i = pl.program_id(0)
        ri = nt - 1 - i
        first = i == 0
        end_of_seq = (ri % tps) == tps - 1
        start_of_seq = (ri % tps) == 0

        @pl.when(end_of_seq)
        def _():
            carry[...] = jnp.zeros_like(carry)

        prm = dict(lng=lng_ref[...], lnb=lnb_ref[...], pmat=pmat_ref[...], convw=convw_ref[...])
        zprev = jnp.where(start_of_seq, 0.0, cgp_ref[...] * xbp_ref[...])
        r = _mix_core_forward(proj_ref[...], zprev, prm, da, db, saved=sv_ref)
        og = og_ref[...]
        pmat = prm["pmat"]

        yn = jnp.concatenate([(r["yah"] * og[:, 0:da]).astype(BF16), (r["ybh"] * og[:, da:]).astype(BF16)], axis=1)
        dxo = dxo_ref[...]
        o = _dot(yn, wout_ref[...])
        _acc(dgate_ref.at[0], end_of_seq, jnp.sum(dxo * o, axis=0, keepdims=True))
        d_o = (gate_ref[0] * dxo).astype(BF16)
        do_ref[...] = d_o
        dyn = _dot_nt(d_o, wout_ref[...])

        def rms_bwd(dyn_g, yh, rr, og_g):
            dog_g = jnp.sum(dyn_g * yh, axis=0, keepdims=True)
            dyh = dyn_g * og_g
            return rr * (dyh - yh * jnp.mean(dyh * yh, axis=-1, keepdims=True)), dog_g

        dya, dog_a = rms_bwd(dyn[:, 0:da], r["yah"], r["ra"], og[:, 0:da])
        dyb, dog_b = rms_bwd(dyn[:, da:], r["ybh"], r["rb"], og[:, da:])
        _acc(dog_ref, first, jnp.concatenate([dog_a, dog_b], axis=1))

        dug = dya * r["mixed"]
        dmixed = dya * r["ug"]
        wstt_b = _causal_stack(wstt_ref[...], True).astype(BF16)
        dbias = jnp.zeros((CHUNK, da), F32)
        dwst = jnp.zeros((N_HEADS * CHUNK, CHUNK), F32)
        dvln = []
        for j in range(tm // CHUNK):
            dm = dmixed[j * CHUNK:(j + 1) * CHUNK]
            dbias = dbias + dm
            dmb = dm.astype(BF16)
            dwst = dwst + _mix_heads_grad(dmb, r["vln"][j * CHUNK:(j + 1) * CHUNK], da)
            dvln.append(_mix_heads(wstt_b, dmb, da))
        dvln = dvln[0] if len(dvln) == 1 else jnp.concatenate(dvln, axis=0)
        _acc(dbias_ref, first, dbias)
        _acc(dwst_ref, first, dwst)
        _acc(dlng_ref, first, jnp.sum(dvln * r["vhat"], axis=0, keepdims=True))
        _acc(dlnb_ref, first, jnp.sum(dvln, axis=0, keepdims=True))
        dvhat = dvln * prm["lng"]
        dvg = r["rs"] * (dvhat - _head_mean(dvhat, pmat, exact=False)
                         - r["vhat"] * _head_mean(dvhat * r["vhat"], pmat, exact=False))
        dproj_ref[:, 0:da] = (dug * _gelu_grad(r["ua"], r["ucdf"])).astype(BF16)
        dproj_ref[:, da:2 * da] = (dvg * _gelu_grad(r["va"], r["vcdf"])).astype(BF16)

        dproj_ref[:, 2 * da:2 * da + db] = (dyb * r["conv"]).astype(BF16)
        dconv = dyb * r["bg"]
        dcw = jnp.concatenate([
            jnp.sum(dconv * r["z2"], axis=0, keepdims=True),
            jnp.sum(dconv * r["z1"], axis=0, keepdims=True),
            jnp.sum(dconv * r["z"], axis=0, keepdims=True),
            jnp.zeros((5, db), F32)], axis=0)
        _acc(dconvw_ref, first, dcw)
        nxt = carry[...]
        row = lax.broadcasted_iota(jnp.int32, dconv.shape, 0)
        dc1 = jnp.where(row == tm - 1, nxt[0:1], pltpu.roll(dconv, tm - 1, 0))
        dc2 = jnp.where(row == tm - 2, nxt[0:1], jnp.where(row == tm - 1, nxt[1:2], pltpu.roll(dconv, tm - 2, 0)))
        carry[...] = dconv[0:8]
        cw = prm["convw"]
        dz = dconv * cw[2:3] + dc1 * cw[1:2] + dc2 * cw[0:1]
        dproj_ref[:, 2 * da + db:2 * da + 2 * db] = (dz * r["xb"]).astype(BF16)
        dproj_ref[:, 2 * da + 2 * db:] = (dz * r["cg"]).astype(BF16)

        @pl.when(i == nt - 1)
        def _():
            dwst_ref[...] = _causal_stack(dwst_ref[...], False)
            dbias_ref[...] = _head_mean(dbias_ref[...], pmat) * float(hd)

    full = lambda a: pl.BlockSpec(a.shape, lambda i: (0,) * a.ndim)
    const = lambda i: (0, 0)
    rev = lambda i: (nt - 1 - i, 0)
    prev8 = lambda col: (lambda i: (jnp.maximum((nt - 1 - i) * (tm // 8) - 1, 0), col))
    return _call(
        body, "mix_core_bwd", (nt,),
        [
            pl.BlockSpec((tm, P), rev),
            pl.BlockSpec((8, db), prev8((2 * da + db) // db)),
            pl.BlockSpec((8, db), prev8((2 * da + 2 * db) // db)),
            pl.BlockSpec((5, tm, da), lambda i: (0, nt - 1 - i, 0)),
            pl.BlockSpec((tm, D), rev),
            pl.BlockSpec((1, 1, D), lambda i: ((nt - 1 - i) // tps, 0, 0)),
            full(wout), full(lng), full(lnb), full(wstt), full(pmat), full(convw), full(og),
        ],
        [
            pl.BlockSpec((tm, P), rev),
            pl.BlockSpec((tm, D), rev),
            pl.BlockSpec((1, 1, D), lambda i: ((nt - 1 - i) // tps, 0, 0)),
            pl.BlockSpec((1, D), const),
            pl.BlockSpec((N_HEADS * CHUNK, CHUNK), const),
            pl.BlockSpec((CHUNK, da), const),
            pl.BlockSpec((1, da), const),
            pl.BlockSpec((1, da), const),
            pl.BlockSpec((8, db), const),
        ],
        [
            jax.ShapeDtypeStruct((T, P), BF16),
            jax.ShapeDtypeStruct((T, D), BF16),
            jax.ShapeDtypeStruct((B, 1, D), F32),
            jax.ShapeDtypeStruct((1, D), F32),
            jax.ShapeDtypeStruct((N_HEADS * CHUNK, CHUNK), F32),
            jax.ShapeDtypeStruct((CHUNK, da), F32),
            jax.ShapeDtypeStruct((1, da), F32),
            jax.ShapeDtypeStruct((1, da), F32),
            jax.ShapeDtypeStruct((8, db), F32),
        ],
        [pltpu.VMEM((8, db), F32)],
        (proj, proj, proj, sv, dxo, gate, wout, lng, lnb, wstt, pmat, convw, og), comm)


def _loss_head(x, target, gain):
    T, D = x.shape
    tm = _tile(TOKEN_TILE, T)

    def body(x_ref, t_ref, gain_ref, dx_ref, loss_ref, dgain_ref):
        first = pl.program_id(0) == 0
        xh, r = _rms(x_ref[...])
        gain = gain_ref[...]
        err = xh * gain - t_ref[...]
        _acc(loss_ref, first, jnp.zeros((8, 128), F32) + 0.5 * jnp.sum(err * err) / D)
        dout = err * (1.0 / D)
        _acc(dgain_ref, first, jnp.sum(dout * xh, axis=0, keepdims=True))
        dy = dout * gain
        dx_ref[...] = r * (dy - xh * jnp.mean(dy * xh, axis=-1, keepdims=True))

    return _call(
        body, "loss_head", (T // tm,),
        [
            pl.BlockSpec((tm, D), lambda i: (i, 0)),
            pl.BlockSpec((tm, D), lambda i: (i, 0)),
            pl.BlockSpec((1, D), lambda i: (0, 0)),
        ],
        [
            pl.BlockSpec((tm, D), lambda i: (i, 0)),
            pl.BlockSpec((8, 128), lambda i: (0, 0)),
            pl.BlockSpec((1, D), lambda i: (0, 0)),
        ],
        [
            jax.ShapeDtypeStruct((T, D), F32),
            jax.ShapeDtypeStruct((8, 128), F32),
            jax.ShapeDtypeStruct((1, D), F32),
        ],
        [],
        (x, target, gain))[0]


def _ada_fwd(c_all, ada_w, ada_b):
    n, D = c_all.shape
    L, _, sa = ada_w.shape
    tn = _tile(768, sa)

    def body(c_ref, w_ref, b_ref, act_ref, o_ref):
        c = c_ref[...]
        act = (c * _sigmoid(c)).astype(BF16)
        act_ref[...] = act
        o_ref[...] = _dot(act, w_ref[...].astype(BF16)) + b_ref[...]

    return _call(
        body, "ada_fwd", (L, sa // tn),
        [
            pl.BlockSpec((n, D), lambda l, j: (0, 0)),
            pl.BlockSpec((None, D, tn), lambda l, j: (l, 0, j)),
            pl.BlockSpec((None, 1, tn), lambda l, j: (l, 0, j)),
        ],
        [
            pl.BlockSpec((n, D), lambda l, j: (0, 0)),
            pl.BlockSpec((None, n, tn), lambda l, j: (l, 0, j)),
        ],
        [jax.ShapeDtypeStruct((n, D), BF16), jax.ShapeDtypeStruct((L, n, sa), F32)],
        [],
        (c_all, ada_w, ada_b))[0]


def _ada_bwd(c_act, d_ada, comm=None):
    n, D = c_act.shape
    L, _, sa = d_ada.shape
    tn = _tile(768, sa)

    def body(c_ref, d_ref, o_ref):
        o_ref[...] = _dot_tn(c_ref[...], d_ref[...])

    return _call(
        body, "ada_bwd", (L, sa // tn),
        [pl.BlockSpec((n, D), lambda l, j: (0, 0)), pl.BlockSpec((None, n, tn), lambda l, j: (l, 0, j))],
        [pl.BlockSpec((None, D, tn), lambda l, j: (l, 0, j))],
        [jax.ShapeDtypeStruct((L, D, sa), F32)],
        [],
        (c_act, d_ada), comm)


def _colsum(a):
    L, n, C = a.shape

    def body(a_ref, o_ref):
        o_ref[...] = jnp.sum(a_ref[...], axis=0, keepdims=True)

    return _call(
        body, "colsum", (L,),
        [pl.BlockSpec((None, n, C), lambda l: (l, 0, 0))],
        [pl.BlockSpec((None, 1, C), lambda l: (l, 0, 0))],
        [jax.ShapeDtypeStruct((L, 1, C), F32)],
        [],
        (a,))[0][0]


def _row_tile(rows, cols, nbuf):
    budget = VMEM_LIMIT // 3 // (2 * nbuf * 4 * cols)
    t = rows
    while t > max(budget, 8) and t % 2 == 0 and (t // 2) % 8 == 0:
        t //= 2
    return t


def _pair_sum(g, recv, core):
    n, _, R, C = g.shape
    tr = _row_tile(R, C, 3)

    def body(core_ref, g_ref, r_ref, o_ref):
        o_ref[...] = (g_ref[...] + r_ref[...]).astype(BF16)

    return pl.pallas_call(
        body,
        name="pair_sum",
        grid_spec=pltpu.PrefetchScalarGridSpec(
            num_scalar_prefetch=1,
            grid=(n, R // tr),
            in_specs=[
                pl.BlockSpec((None, None, tr, C), lambda i, r, core_ref: (i, core_ref[0], r, 0)),
                pl.BlockSpec((None, tr, C), lambda i, r, core_ref: (i, r, 0)),
            ],
            out_specs=pl.BlockSpec((None, tr, C), lambda i, r, core_ref: (i, r, 0)),
        ),
        out_shape=jax.ShapeDtypeStruct((n, R, C), BF16),
        compiler_params=pltpu.CompilerParams(dimension_semantics=("arbitrary", "arbitrary"),
                                             vmem_limit_bytes=VMEM_LIMIT),
    )(core, g, recv)


def _chip_sum(q, core, l, n_layers, prev):
    nq, R, C = q.shape
    tr = _row_tile(R, C, 4)

    def body(core_ref, q_ref, *rest):
        o_ref = rest[-1]
        s = q_ref[0].astype(F32)
        for j in range(1, nq):
            s = s + q_ref[j].astype(F32)
        o_ref[...] = s

    in_specs = [pl.BlockSpec((nq, tr, C), lambda r, core_ref: (0, r, 0))]
    args = [core, q]
    aliases = {}
    if prev is not None:
        in_specs.append(ANY)
        args.append(prev)
        aliases = {2: 0}
    return pl.pallas_call(
        body,
        name="chip_sum",
        grid_spec=pltpu.PrefetchScalarGridSpec(
            num_scalar_prefetch=1,
            grid=(R // tr,),
            in_specs=in_specs,
            out_specs=pl.BlockSpec((None, None, tr, C), lambda r, core_ref: (l, core_ref[0], r, 0)),
        ),
        out_shape=jax.ShapeDtypeStruct((n_layers, 2, R, C), F32),
        input_output_aliases=aliases,
        compiler_params=pltpu.CompilerParams(dimension_semantics=("arbitrary",), vmem_limit_bytes=VMEM_LIMIT),
    )(*args)


def _sum_blocks(a, n):
    M = a.shape[0] // n
    C = a.shape[1]

    def body(a_ref, o_ref):
        s = a_ref[0:M]
        for j in range(1, n):
            s = s + a_ref[j * M:(j + 1) * M]
        o_ref[...] = s

    return pl.pallas_call(
        body,
        name="sum_blocks",
        out_shape=jax.ShapeDtypeStruct((M, C), F32),
        compiler_params=pltpu.CompilerParams(vmem_limit_bytes=VMEM_LIMIT),
    )(a)


def _adamw(w, g, m, v, emit_grad=False):
    R, C = w.shape
    n_out = 4 if emit_grad else 3
    tr = _row_tile(R, C, 4 + n_out) if R % 8 == 0 else R

    def body(w_ref, g_ref, m_ref, v_ref, d_ref, nm_ref, nv_ref, *g_out):
        g = g_ref[...]
        m = ADAM_B1 * m_ref[...] + (1.0 - ADAM_B1) * g
        v = ADAM_B2 * v_ref[...] + (1.0 - ADAM_B2) * (g * g)
        m_hat = m / (1.0 - ADAM_B1 ** ADAM_STEP)
        v_hat = v / (1.0 - ADAM_B2 ** ADAM_STEP)
        d_ref[...] = -ADAM_LR * (m_hat / (jnp.sqrt(v_hat) + ADAM_EPS) + ADAM_WD * w_ref[...])
        nm_ref[...] = m
        nv_ref[...] = v
        if emit_grad:
            g_out[0][...] = g

    spec = pl.BlockSpec((tr, C), lambda i: (i, 0))
    return _call(body, "adamw", (R // tr,), [spec] * 4, [spec] * n_out, [jax.ShapeDtypeStruct((R, C), F32)] * n_out,
                 [], (w, g, m, v))[0]


def kernel(x, c, ada_w, ada_b, norm_ffn1_g, ffn1_w_gu, ffn1_w_down, norm_mix_g, mix_w_in, sgu_ln_g, sgu_ln_b, sgu_w_s, sgu_b, conv_w, out_norm_g, mix_w_out, norm_ffn2_g, ffn2_w_gu, ffn2_w_down, final_norm_g, loss_target, m_ada_w, m_ada_b, m_norm_ffn1_g, m_ffn1_w_gu, m_ffn1_w_down, m_norm_mix_g, m_mix_w_in, m_sgu_ln_g, m_sgu_ln_b, m_sgu_w_s, m_sgu_b, m_conv_w, m_out_norm_g, m_mix_w_out, m_norm_ffn2_g, m_ffn2_w_gu, m_ffn2_w_down, m_final_norm_g, v_ada_w, v_ada_b, v_norm_ffn1_g, v_ffn1_w_gu, v_ffn1_w_down, v_norm_mix_g, v_mix_w_in, v_sgu_ln_g, v_sgu_ln_b, v_sgu_w_s, v_sgu_b, v_conv_w, v_out_norm_g, v_mix_w_out, v_norm_ffn2_g, v_ffn2_w_gu, v_ffn2_w_down, v_final_norm_g):
    weights = dict(ada_w=ada_w, ada_b=ada_b, norm_ffn1_g=norm_ffn1_g, ffn1_w_gu=ffn1_w_gu, ffn1_w_down=ffn1_w_down,
                   norm_mix_g=norm_mix_g, mix_w_in=mix_w_in, sgu_ln_g=sgu_ln_g, sgu_ln_b=sgu_ln_b, sgu_w_s=sgu_w_s,
                   sgu_b=sgu_b, conv_w=conv_w, out_norm_g=out_norm_g, mix_w_out=mix_w_out, norm_ffn2_g=norm_ffn2_g,
                   ffn2_w_gu=ffn2_w_gu, ffn2_w_down=ffn2_w_down, final_norm_g=final_norm_g)
    m_in = dict(ada_w=m_ada_w, ada_b=m_ada_b, norm_ffn1_g=m_norm_ffn1_g, ffn1_w_gu=m_ffn1_w_gu,
                ffn1_w_down=m_ffn1_w_down, norm_mix_g=m_norm_mix_g, mix_w_in=m_mix_w_in, sgu_ln_g=m_sgu_ln_g,
                sgu_ln_b=m_sgu_ln_b, sgu_w_s=m_sgu_w_s, sgu_b=m_sgu_b, conv_w=m_conv_w, out_norm_g=m_out_norm_g,
                mix_w_out=m_mix_w_out, norm_ffn2_g=m_norm_ffn2_g, ffn2_w_gu=m_ffn2_w_gu, ffn2_w_down=m_ffn2_w_down,
                final_norm_g=m_final_norm_g)
    v_in = dict(ada_w=v_ada_w, ada_b=v_ada_b, norm_ffn1_g=v_norm_ffn1_g, ffn1_w_gu=v_ffn1_w_gu,
                ffn1_w_down=v_ffn1_w_down, norm_mix_g=v_norm_mix_g, mix_w_in=v_mix_w_in, sgu_ln_g=v_sgu_ln_g,
                sgu_ln_b=v_sgu_ln_b, sgu_w_s=v_sgu_w_s, sgu_b=v_sgu_b, conv_w=v_conv_w, out_norm_g=v_out_norm_g,
                mix_w_out=v_mix_w_out, norm_ffn2_g=v_norm_ffn2_g, ffn2_w_gu=v_ffn2_w_gu, ffn2_w_down=v_ffn2_w_down,
                final_norm_g=v_final_norm_g)

    B, S, D = x.shape
    T = B * S
    L = ada_w.shape[0]
    F = ffn1_w_down.shape[1] * N_CHIP
    P = mix_w_in.shape[2] * N_CHIP
    DA = D // 2
    DB = D - DA
    HD = DA // N_HEADS
    SA = ada_w.shape[2]
    n_all = B * N_DEV
    mx, my, mc = _position()
    chip = 2 * mx + my
    dev = 2 * chip + mc
    core = jnp.reshape(mc, (1,)).astype(jnp.int32)

    big = ["ffn1_w_gu", "ffn1_w_down", "mix_w_in", "mix_w_out", "ffn2_w_gu", "ffn2_w_down"]
    col_sharded = dict(ffn1_w_gu=True, ffn1_w_down=False, mix_w_in=True, mix_w_out=False,
                       ffn2_w_gu=True, ffn2_w_down=False)
    shards = {k: weights[k].astype(BF16) for k in big}
    gather = lambda l, *names: _gather_comm([(shards[k], l, col_sharded[k]) for k in names])
    full = [dict() for _ in range(L)]

    def arrived(l, names, res):
        full[l].update(zip(names, res))

    n_cw = L * conv_w.shape[1]
    cw_block = jnp.pad(conv_w.reshape(n_cw, conv_w.shape[2]), ((0, 8 - n_cw), (0, 0)))
    c_all, cw_all = _comm_call(_merge(_all_gather_comm(c.reshape(8, B * D // 8)), _all_gather_comm(cw_block)),
                               "gather_c")
    c_all = c_all.reshape(n_all, D)
    cw_all = cw_all.reshape(N_CHIP, 2, 8, conv_w.shape[2])[:, 0, :n_cw]
    conv_full = jnp.transpose(cw_all.reshape(N_CHIP, L, conv_w.shape[1], conv_w.shape[2]), (1, 2, 0, 3))
    conv_full = conv_full.reshape(L, conv_w.shape[1], DB)
    ada_b_mine = lax.dynamic_slice_in_dim(ada_b, chip * SA, SA, axis=1).reshape(L, 1, SA)
    c_act, ada_part = _ada_fwd(c_all, ada_w, ada_b_mine)
    ada_all, first_w = _comm_call(_merge(_all_gather_comm(ada_part.reshape(L * n_all, SA)), gather(0, big[0])),
                                  "gather_first")
    arrived(0, big[:1], [first_w])
    ada_all = ada_all.reshape(N_CHIP, 2, L, n_all, SA)[:, 0]
    ada_all = jnp.transpose(ada_all, (1, 2, 0, 3)).reshape(L, n_all, N_CHIP * SA)
    ada = lax.dynamic_slice_in_dim(ada_all, dev * B, B, axis=1).reshape(L, B, N_MOD, 1, D)
    mods = [[ada[l, :, j] for j in range(N_MOD)] for l in range(L)]

    x0 = x.reshape(T, D)
    gains = lambda name, l: weights[name][l].reshape(1, D)
    hmask = jnp.repeat(jnp.eye(N_HEADS, dtype=F32), HD, axis=0)
    pmat = (jnp.repeat(hmask, HD, axis=1) / HD).astype(BF16)

    def mix_consts(l):
        lng = jnp.tile(sgu_ln_g[l], N_HEADS).reshape(1, DA)
        lnb = jnp.tile(sgu_ln_b[l], N_HEADS).reshape(1, DA)
        wst = sgu_w_s[l].reshape(N_HEADS * CHUNK, CHUNK)
        wstt = jnp.swapaxes(sgu_w_s[l], 1, 2).reshape(N_HEADS * CHUNK, CHUNK)
        bias = jnp.repeat(jnp.transpose(sgu_b[l]), HD, axis=1)
        return lng, lnb, wst, wstt, bias

    def fetch(fn, *args, bring=()):
        bring = [(l, k) for l, k in bring if l < L]
        comm = _gather_comm([(shards[k], l, col_sharded[k]) for l, k in bring]) if bring else None
        res, got = fn(*args, comm)
        for (l, k), a in zip(bring, got):
            full[l][k] = a
        return res

    saved = []
    xc = x0
    for l in range(L):
        sh1, sc1, g1, sh2, sc2, g2, sh3, sc3, g3 = mods[l]
        lng, lnb, wst, wstt, bias = mix_consts(l)
        w = full[l]
        if l == 0:
            gu1, a1 = fetch(_ffn_up, xc, gains("norm_ffn1_g", l), sh1, sc1, w["ffn1_w_gu"],
                            bring=[(l, "ffn1_w_down"), (l, "mix_w_in"), (l, "mix_w_out")])
            xa, f1 = fetch(_ffn_down, a1, xc, g1, w["ffn1_w_down"], bring=[(l, "ffn2_w_down")])
        else:
            xa, gu1, a1, f1 = fetch(_ffn_fwd, xc, gains("norm_ffn1_g", l), sh1, sc1, g1, w["ffn1_w_gu"],
                                    w["ffn1_w_down"], bring=[(l, "ffn2_w_gu"), (l, "mix_w_in")])
        xb, proj, h2, yn, sv = fetch(_mix_fwd, xa, gains("norm_mix_g", l), sh2, sc2, g2, w["mix_w_in"], w["mix_w_out"],
                                     lng, lnb, wst, bias, pmat, conv_full[l], gains("out_norm_g", l),
                                     bring=[(l, "ffn2_w_gu")] if l == 0 else [(l, "ffn2_w_down")])
        xd, gu2, a2, f2 = fetch(_ffn_fwd, xb, gains("norm_ffn2_g", l), sh3, sc3, g3, w["ffn2_w_gu"], w["ffn2_w_down"],
                                bring=[(l + 1, "ffn1_w_gu"), (l + 1, "mix_w_out"), (l + 1, "ffn1_w_down")])
        saved.append(dict(x0=xc, xa=xa, xb=xb, gu1=gu1, a1=a1, f1=f1, proj=proj, h2=h2, yn=yn, sv=sv,
                          gu2=gu2, a2=a2, f2=f2))
        xc = xd

    dx, loss_block, d_final = _loss_head(xc, loss_target.reshape(T, D), final_norm_g.reshape(1, D))

    reduced = dict.fromkeys(big)

    def halves(name, g):
        if g.ndim == 4:
            return g
        return g.reshape(N_CHIP, 2, weights[name].shape[1] // 2, g.shape[-1])

    class Reduction:
        def __init__(self, l, name, g):
            self.l, self.name, self.g, self.stage = l, name, halves(name, g), 0
            self.ici_bytes = 3 * (g.size // 8) * 2

        def step(self):
            self.stage += 1
            if self.stage == 1:
                return _sibling_half_comm([self.g])
            if self.stage == 2:
                return _scatter_comm([_pair_sum(self.g, self.got[0], core)])
            if self.stage == 3:
                reduced[self.name] = _chip_sum(self.got[0], core, self.l, L, reduced[self.name])
                return _share_comm([reduced[self.name]], self.l)
            reduced[self.name] = self.got[0]
            return None

    active, extra, gathered = [], [], {}

    def carry(fn, *args, us=None):
        left = None if us is None else us * SCATTER_BYTES_PER_US
        riders = []
        for r in active:
            if r.stage == 1 and left is not None:
                if r.ici_bytes > left * SCATTER_OVERSHOOT:
                    continue
                left -= r.ici_bytes
            riders.append(r)
        comms = [r.step() for r in riders] + [cm for cm, _ in extra]
        takers = [functools.partial(setattr, r, "got") for r in riders] + [cb for _, cb in extra]
        extra.clear()
        if fn is None:
            res, got = None, (_comm_call(_merge(*comms), "reduce_alone") if comms else [])
        else:
            res, got = fn(*args, comm=_merge(*comms))
        at = 0
        for cm, take in zip(comms, takers):
            take(got[at:at + len(cm.out_shape)])
            at += len(cm.out_shape)
        for r in riders:
            if r.stage == 3:
                r.step()
                active.remove(r)
        return res

    def reduce_later(l, name, g):
        active.append(Reduction(l, name, g))

    small = [None] * L
    dwsts = [None] * L
    d_ada = [None] * L
    for l in reversed(range(L)):
        sh1, sc1, g1, sh2, sc2, g2, sh3, sc3, g3 = mods[l]
        lng, lnb, wst, wstt, bias = mix_consts(l)
        s = saved[l]
        w = full[l]
        last = l == 0
        dx, dgu2, h3, df2, dsc3, dsh3, dgain3, dg3 = carry(
            _ffn_bwd, dx, s["xb"], s["gu2"], s["f2"], gains("norm_ffn2_g", l), sh3, sc3, g3, w["ffn2_w_gu"],
            w["ffn2_w_down"], us=170)
        ffn2_grads = [
            lambda: reduce_later(l, "ffn2_w_gu", carry(_wgrad, h3, dgu2, D, 2 * F // N_CHIP, True, "wgrad_gu",
                                                       WGRAD_TOKENS // 2, us=110)[0]),
            lambda: reduce_later(l, "ffn2_w_down", carry(_wgrad, s["a2"], df2[None], F // 2, D, False, "wgrad_down",
                                                         us=50)[0])]
        if not last:
            ffn2_grads[0]()
            ffn2_grads[1]()
        dproj, d_o, dg2, dog, dwst, dbias, dlng, dlnb, dconvw = carry(
            _mix_core_bwd, s["proj"], s["sv"], dx, g2, w["mix_w_out"], lng, lnb, wstt, pmat, conv_full[l],
            gains("out_norm_g", l), us=150)
        mix_grads = [
            lambda: reduce_later(l, "mix_w_out", carry(_wgrad, s["yn"], d_o[None], D, D, False, "wgrad_out", us=30)[0]),
            lambda: reduce_later(l, "mix_w_in", carry(_wgrad, s["h2"], dproj[None], D, P // N_CHIP, True, "wgrad_in",
                                                      us=65)[0])]
        if not last:
            mix_grads[0]()
        dx, dsc2, dsh2, dgain2 = carry(_mixin_bwd, dx, s["xa"], dproj, gains("norm_mix_g", l), sc2, w["mix_w_in"], us=60)
        if not last:
            mix_grads[1]()
        dx, dgu, h1, df, dsc1, dsh1, dgain1, dg1 = carry(
            _ffn_bwd, dx, s["x0"], s["gu1"], s["f1"], gains("norm_ffn1_g", l), sh1, sc1, g1, w["ffn1_w_gu"],
            w["ffn1_w_down"], us=170)
        d_ada[l] = jnp.concatenate([dsh1, dsc1, dg1, dsh2, dsc2, dg2, dsh3, dsc3, dg3], axis=1).reshape(B, N_MOD * D)
        small[l] = [dgain1, dgain2, dgain3, dog, dlng, dlnb, dbias[:, ::HD], dconvw]
        dwsts[l] = dwst
        if last:
            flat = [a.reshape(-1, 128) for ll in range(L) for a in small[ll]]
            flat += [d_final.reshape(-1, 128), loss_block[0:1]]
            pad = (-sum(a.shape[0] for a in flat)) % 8
            packed = jnp.concatenate(flat + [jnp.zeros((pad, 128), F32)], axis=0)
            extra.append((_all_gather_comm(jnp.stack(d_ada).reshape(L * B, N_MOD * D)),
                          lambda got: gathered.update(d_ada=got[0])))
            extra.append((_all_gather_comm(packed), lambda got: gathered.update(small=got[0])))
            for ll in range(L):
                extra.append((_all_gather_comm(dwsts[ll]), lambda got, ll=ll: gathered.update({("dwst", ll): got[0]})))
        reduce_later(l, "ffn1_w_gu", carry(_wgrad, h1, dgu, D, 2 * F // N_CHIP, True, "wgrad_gu", WGRAD_TOKENS // 2,
                                           us=110)[0])
        reduce_later(l, "ffn1_w_down", carry(_wgrad, s["a1"], df[None], F // 2, D, False, "wgrad_down", us=50)[0])
        if last:
            ffn2_grads[0]()
            ffn2_grads[1]()
            mix_grads[1]()
            mix_grads[0]()
    grad_x = dx.reshape(B, S, D)

    def finished(name):
        while any(r.name == name for r in active):
            carry(None)
        return reduced[name].reshape(weights[name].shape)

    grads = {}
    d_ada_all = jnp.transpose(gathered["d_ada"].reshape(N_DEV, L, B, N_MOD * D), (1, 0, 2, 3))
    d_ada_all = d_ada_all.reshape(L, n_all, N_MOD * D)
    grads["ada_b"] = _colsum(d_ada_all).reshape(L, N_MOD * D)
    d_ada_mine = lax.dynamic_slice_in_dim(d_ada_all, chip * SA, SA, axis=2).astype(BF16)
    grads["ada_w"] = _ada_bwd(c_act, d_ada_mine)[0][0]

    total = _sum_blocks(gathered["small"].reshape(-1, 128), N_DEV)
    pieces, at = [], 0
    for a in flat:
        pieces.append(total[at:at + a.shape[0]])
        at += a.shape[0]
    per_layer = len(small[0])
    stack = lambda j, shape: jnp.stack([pieces[l * per_layer + j].reshape(shape) for l in range(L)])
    grads["norm_ffn1_g"] = stack(0, (D,))
    grads["norm_mix_g"] = stack(1, (D,))
    grads["norm_ffn2_g"] = stack(2, (D,))
    grads["out_norm_g"] = stack(3, (D,))
    grads["sgu_ln_g"] = stack(4, (N_HEADS, HD)).sum(axis=1)
    grads["sgu_ln_b"] = stack(5, (N_HEADS, HD)).sum(axis=1)
    grads["sgu_b"] = jnp.swapaxes(stack(6, (CHUNK, N_HEADS)), 1, 2)
    g_conv = stack(7, (8, DB))[:, :conv_w.shape[1]]
    grads["conv_w"] = lax.dynamic_slice_in_dim(g_conv, chip * conv_w.shape[2], conv_w.shape[2], axis=2)
    grads["final_norm_g"] = pieces[-2].reshape(D)
    loss = pieces[-1][0, 0]
    grads["sgu_w_s"] = jnp.stack([_sum_blocks(gathered["dwst", l].reshape(-1, CHUNK), N_DEV) for l in range(L)])
    grads["sgu_w_s"] = grads["sgu_w_s"].reshape(L, N_HEADS, CHUNK, CHUNK)

    names = list(weights)
    delta, new_m, new_v = {}, {}, {}
    for k in big:
        grads[k] = finished(k)
    for k in names:
        wk = weights[k]
        view = (1, wk.shape[0]) if wk.ndim == 1 else (-1, wk.shape[-1])
        d, nm, nv, *g_again = _adamw(wk.reshape(view), grads[k].reshape(view), m_in[k].reshape(view),
                                     v_in[k].reshape(view), emit_grad=k in big)
        delta[k], new_m[k], new_v[k] = d.reshape(wk.shape), nm.reshape(wk.shape), nv.reshape(wk.shape)
        if g_again:
            grads[k] = g_again[0].reshape(wk.shape)

    return (loss, grad_x, *[grads[k] for k in names], *[delta[k] for k in names],
            *[new_m[k] for k in names], *[new_v[k] for k in names])
```

```python
import functools
import math

import jax
import jax.numpy as jnp
from jax import lax
from jax.experimental import pallas as pl
from jax.experimental.pallas import tpu as pltpu

F32 = jnp.float32
BF16 = jnp.bfloat16
MESH = pl.DeviceIdType.MESH

N_HEADS = 8
CHUNK = 128
N_MOD = 9
EPS = 1e-6
N_DEV = 8
N_CHIP = 4

ADAM_LR = 0.001
ADAM_B1 = 0.9
ADAM_B2 = 0.999
ADAM_EPS = 1e-08
ADAM_WD = 0.01
ADAM_STEP = 10

TOKEN_TILE = 512
BWD_TILE = 256
FWD_TILE = 512
FF_SLAB = 768
MIX_TILE = 256
WGRAD_TOKENS = 2048
VMEM_LIMIT = 56 * 1024 * 1024

SCATTER_BYTES_PER_US = 68_000
SCATTER_OVERSHOOT = 1.25

ANY = pl.BlockSpec(memory_space=pl.ANY)


def _tile(pref, n):
    t = min(pref, n)
    assert n % t == 0, (pref, n)
    return t


def _slabs(n, width):
    return [slice(c0, min(c0 + width, n)) for c0 in range(0, n, width)]


def _dot(a, b):
    return jnp.dot(a, b, preferred_element_type=F32)


def _dot_nt(a, b):
    return lax.dot_general(a, b, (((1,), (1,)), ((), ())), preferred_element_type=F32)


def _dot_tn(a, b):
    return lax.dot_general(a, b, (((0,), (0,)), ((), ())), preferred_element_type=F32)


def _sigmoid(x):
    return 1.0 / (1.0 + jnp.exp(-x))


def _sigmoid_fast(x):
    return pl.reciprocal(1.0 + jnp.exp(-x), approx=True)


def _rms(x):
    r = lax.rsqrt(jnp.mean(x * x, axis=-1, keepdims=True) + EPS)
    return x * r, r


def _norm_mod_bwd(x, dh, gain, sc):
    xh, r = _rms(x)
    dsc = jnp.sum(dh * (xh * gain), axis=0, keepdims=True)
    dsh = jnp.sum(dh, axis=0, keepdims=True)
    dn = dh * (1.0 + sc)
    dgain = jnp.sum(dn * xh, axis=0, keepdims=True)
    dy = dn * gain
    dx = r * (dy - xh * jnp.mean(dy * xh, axis=-1, keepdims=True))
    return dx, dsc, dsh, dgain


def _acc(ref, first, val):
    @pl.when(first)
    def _():
        ref[...] = val

    @pl.when(jnp.logical_not(first))
    def _():
        ref[...] += val


class _Comm:
    def __init__(self, args, out_shape, scratch, phases, aliases=None):
        self.args, self.out_shape, self.scratch = list(args), list(out_shape), list(scratch)
        self.phases, self.aliases = phases, dict(aliases or {})


def _merge(*comms):
    comms = [c for c in comms if c is not None]
    if len(comms) <= 1:
        return comms[0] if comms else None
    args = [a for c in comms for a in c.args]
    out_shape = [o for c in comms for o in c.out_shape]
    scratch = [s for c in comms for s in c.scratch]
    aliases, ai, oi = {}, 0, 0
    for c in comms:
        aliases.update({ai + i: oi + o for i, o in c.aliases.items()})
        ai += len(c.args)
        oi += len(c.out_shape)

    def phases(ins, outs, sems):
        parts, ai, oi, si = [], 0, 0, 0
        for c in comms:
            parts.append(c.phases(ins[ai:ai + len(c.args)], outs[oi:oi + len(c.out_shape)], sems[si:si + len(c.scratch)]))
            ai, oi, si = ai + len(c.args), oi + len(c.out_shape), si + len(c.scratch)

        def run(k):
            def go():
                for p in parts:
                    if p[k] is not None:
                        p[k]()
            return go
        return run(0), run(1), run(2)

    return _Comm(args, out_shape, scratch, phases, aliases)


def _call(body, name, grid, in_specs, out_specs, out_shape, scratch, args, comm=None):
    n_in, n_out, n_scr = len(in_specs), len(out_specs), len(scratch)
    sem = ("arbitrary",) * len(grid)
    params = pltpu.CompilerParams(dimension_semantics=sem, vmem_limit_bytes=VMEM_LIMIT)
    if comm is None:
        res = pl.pallas_call(body, name=name, grid=grid, in_specs=in_specs, out_specs=out_specs, out_shape=out_shape,
                             scratch_shapes=scratch, compiler_params=params)(*args)
        return list(res), []
    m_in, m_out = len(comm.args), len(comm.out_shape)

    def full(*refs):
        c_in, c_min = refs[:n_in], refs[n_in:n_in + m_in]
        o = n_in + m_in
        c_out, c_mout = refs[o:o + n_out], refs[o + n_out:o + n_out + m_out]
        o += n_out + m_out
        c_scr, c_sem = refs[o:o + n_scr], refs[o + n_scr:]
        start, mid, finish = comm.phases(c_min, c_mout, c_sem)
        ids = [pl.program_id(a) for a in range(len(grid))]
        first = functools.reduce(jnp.logical_and, [i == 0 for i in ids])
        last = functools.reduce(jnp.logical_and, [i == g - 1 for i, g in zip(ids, grid)])
        pl.when(first)(start)
        if mid is not None:
            pl.when(last)(mid)
        body(*c_in, *c_out, *c_scr)
        pl.when(last)(finish)

    res = pl.pallas_call(
        full, name=name, grid=grid,
        in_specs=list(in_specs) + [ANY] * m_in,
        out_specs=list(out_specs) + [ANY] * m_out,
        out_shape=list(out_shape) + comm.out_shape,
        scratch_shapes=list(scratch) + comm.scratch,
        input_output_aliases={n_in + i: n_out + o for i, o in comm.aliases.items()},
        compiler_params=params,
    )(*args, *comm.args)
    return list(res[:n_out]), list(res[n_out:])


def _comm_call(comm, name):
    m_in, m_out = len(comm.args), len(comm.out_shape)

    def body(*refs):
        start, mid, finish = comm.phases(refs[:m_in], refs[m_in:m_in + m_out], refs[m_in + m_out:])
        start()
        if mid is not None:
            mid()
        finish()

    res = pl.pallas_call(
        body, name=name, in_specs=[ANY] * m_in, out_specs=[ANY] * m_out, out_shape=comm.out_shape,
        scratch_shapes=comm.scratch, input_output_aliases=comm.aliases,
    )(*comm.args)
    return list(res)


def _position():
    return lax.axis_index("x"), lax.axis_index("y"), lax.axis_index("c")


def _gather_comm(items):
    n = len(items)
    half = [s.shape[1] // 2 for s, _, _ in items]

    def full_shape(i):
        s, _, col = items[i]
        _, R, C = s.shape
        return jax.ShapeDtypeStruct((R, N_CHIP * C) if col else (N_CHIP * R, C), s.dtype)

    def phases(ins, outs, sems):
        send_sems, recv_sems, local_sems = sems
        x, y, c = _position()

        def region(i, chip, h):
            s, _, col = items[i]
            _, R, C = s.shape
            if col:
                return outs[i].at[pl.ds(h * half[i], half[i]), pl.ds(chip * C, C)]
            return outs[i].at[pl.ds(chip * R + h * half[i], half[i]), :]

        def mine(i, h):
            return ins[i].at[items[i][1], pl.ds(h * half[i], half[i]), :]

        def copies(kx, ky, kc):
            k_me = 2 * kx + ky
            sibling = (kx, ky, 1 - kc)
            chips = [(1 - kx, ky), (kx, 1 - ky), (1 - kx, 1 - ky)]
            local, first, passed, arrive_ici, arrive_d2d = [], [], [], [], []

            def remote(src, dst, s, to):
                return pltpu.make_async_remote_copy(src_ref=src, dst_ref=dst, send_sem=send_sems.at[s],
                                                    recv_sem=recv_sems.at[s], device_id=to, device_id_type=MESH)

            for i in range(n):
                for h in range(2):
                    local.append(pltpu.make_async_copy(mine(i, h), region(i, k_me, h), local_sems.at[2 * i + h]))
                for j, (px, py) in enumerate(chips):
                    s = 6 * i + j
                    first.append(remote(mine(i, kc), region(i, k_me, kc), s, (px, py, kc)))
                    got = region(i, 2 * px + py, kc)
                    arrive_ici.append(remote(got, got, s, (px, py, kc)))
                    passed.append(remote(got, got, s + 3, sibling))
                    other = region(i, 2 * px + py, 1 - kc)
                    arrive_d2d.append(remote(other, other, s + 3, sibling))
            return local, first, passed, arrive_ici, arrive_d2d

        def on_each_device(fn):
            def go():
                for kx in range(2):
                    for ky in range(2):
                        for kc in range(2):
                            pl.when((x == kx) & (y == ky) & (c == kc))(functools.partial(fn, *copies(kx, ky, kc)))
            return go

        def start(local, first, passed, arrive_ici, arrive_d2d):
            for cp in local + first:
                cp.start()

        def mid(local, first, passed, arrive_ici, arrive_d2d):
            for a, p in zip(arrive_ici, passed):
                a.wait_recv()
                p.start()

        def finish(local, first, passed, arrive_ici, arrive_d2d):
            for a in arrive_d2d:
                a.wait_recv()
            for cp in first + passed:
                cp.wait_send()
            for cp in local:
                cp.wait()

        return on_each_device(start), on_each_device(mid), on_each_device(finish)

    scratch = [pltpu.SemaphoreType.DMA((6 * n,)), pltpu.SemaphoreType.DMA((6 * n,)), pltpu.SemaphoreType.DMA((2 * n,))]
    return _Comm([s for s, _, _ in items], [full_shape(i) for i in range(n)], scratch, phases)


def _sibling_half_comm(gs):
    n = len(gs)

    def phases(ins, outs, sems):
        send_sems, recv_sems = sems
        x, y, c = _position()

        def copies():
            return [pltpu.make_async_remote_copy(
                src_ref=ins[i].at[:, 1 - c], dst_ref=outs[i], send_sem=send_sems.at[i], recv_sem=recv_sems.at[i],
                device_id=(x, y, 1 - c), device_id_type=MESH) for i in range(n)]

        def start():
            for cp in copies():
                cp.start()

        def finish():
            for cp in copies():
                cp.wait()

        return start, None, finish

    out_shape = [jax.ShapeDtypeStruct(g.shape[:1] + g.shape[2:], g.dtype) for g in gs]
    return _Comm(gs, out_shape, [pltpu.SemaphoreType.DMA((n,)), pltpu.SemaphoreType.DMA((n,))], phases)


def _scatter_comm(ps):
    n = len(ps)

    def phases(ins, outs, sems):
        send_sems, recv_sems, local_sems = sems
        x, y, c = _position()
        k_me = 2 * x + y
        chips = [(1 - x, y), (x, 1 - y), (1 - x, 1 - y)]

        def copies():
            local = [pltpu.make_async_copy(ins[i].at[k_me], outs[i].at[k_me], local_sems.at[i]) for i in range(n)]
            remote = [pltpu.make_async_remote_copy(
                src_ref=ins[i].at[2 * px + py], dst_ref=outs[i].at[k_me],
                send_sem=send_sems.at[3 * i + j], recv_sem=recv_sems.at[3 * i + j],
                device_id=(px, py, c), device_id_type=MESH) for i in range(n) for j, (px, py) in enumerate(chips)]
            return local, remote

        def start():
            local, remote = copies()
            for cp in local + remote:
                cp.start()

        def finish():
            local, remote = copies()
            for cp in remote + local:
                cp.wait()

        return start, None, finish

    scratch = [pltpu.SemaphoreType.DMA((3 * n,)), pltpu.SemaphoreType.DMA((3 * n,)), pltpu.SemaphoreType.DMA((n,))]
    return _Comm(ps, [jax.ShapeDtypeStruct(p.shape, p.dtype) for p in ps], scratch, phases)


def _share_comm(rs, l):
    n = len(rs)

    def phases(ins, outs, sems):
        send_sems, recv_sems = sems
        x, y, c = _position()

        def copy(i, h):
            return pltpu.make_async_remote_copy(
                src_ref=outs[i].at[l, h], dst_ref=outs[i].at[l, h], send_sem=send_sems.at[i], recv_sem=recv_sems.at[i],
                device_id=(x, y, 1 - c), device_id_type=MESH)

        def start():
            for i in range(n):
                copy(i, c).start()

        def finish():
            for i in range(n):
                copy(i, 1 - c).wait_recv()
            for i in range(n):
                copy(i, c).wait_send()

        return start, None, finish

    return _Comm(rs, [jax.ShapeDtypeStruct(r.shape, r.dtype) for r in rs],
                 [pltpu.SemaphoreType.DMA((n,)), pltpu.SemaphoreType.DMA((n,))], phases,
                 aliases={i: i for i in range(n)})


def _all_gather_comm(block):
    def phases(ins, outs, sems):
        send_sems, recv_sems, local_sem = sems
        (src,), (out,) = ins, outs
        x, y, c = _position()
        sibling = (x, y, 1 - c)
        chips = [(1 - x, y), (x, 1 - y), (1 - x, 1 - y)]

        def slot(px, py, pc):
            return out.at[4 * px + 2 * py + pc]

        def copy(k, blk, to, own=False):
            return pltpu.make_async_remote_copy(
                src_ref=src if own else slot(*blk), dst_ref=slot(*blk),
                send_sem=send_sems.at[k], recv_sem=recv_sems.at[k], device_id=to, device_id_type=MESH)

        mine = lambda: pltpu.make_async_copy(src, slot(x, y, c), local_sem.at[0])
        first = lambda: [copy(0, (x, y, c), sibling, True)] + [
            copy(1 + j, (x, y, c), (*chip, c), True) for j, chip in enumerate(chips)]
        passed = lambda: [copy(4 + j, (*chip, c), sibling) for j, chip in enumerate(chips)]

        def start():
            mine().start()
            for cp in first():
                cp.start()

        def mid():
            for j, (chip, p) in enumerate(zip(chips, passed())):
                copy(1 + j, (*chip, c), (x, y, c)).wait_recv()
                p.start()

        def finish():
            copy(0, sibling, (x, y, c)).wait_recv()
            for j, chip in enumerate(chips):
                copy(4 + j, (*chip, 1 - c), (x, y, c)).wait_recv()
            for cp in first() + passed():
                cp.wait_send()
            mine().wait()

        return start, mid, finish

    scratch = [pltpu.SemaphoreType.DMA((7,)), pltpu.SemaphoreType.DMA((7,)), pltpu.SemaphoreType.DMA((1,))]
    return _Comm([block], [jax.ShapeDtypeStruct((N_DEV,) + block.shape, block.dtype)], scratch, phases)


def _ffn_up(x, gain, sh, sc, wgu, comm=None):
    T, D = x.shape
    F = wgu.shape[1] // 2
    B = sh.shape[0]
    tm = _tile(TOKEN_TILE, T // B)
    tps = (T // B) // tm
    slabs = _slabs(F, FF_SLAB)

    def body(x_ref, gain_ref, sh_ref, sc_ref, w_ref, gu_ref, a_ref):
        xh, _ = _rms(x_ref[...])
        h = (xh * gain_ref[...] * (1.0 + sc_ref[0]) + sh_ref[0]).astype(BF16)

        def dots(s):
            return _dot(h, w_ref[:, s]), _dot(h, w_ref[:, slice(F + s.start, F + s.stop)])

        nxt = dots(slabs[0])
        for j, s in enumerate(slabs):
            g, u = nxt
            if j + 1 < len(slabs):
                nxt = dots(slabs[j + 1])
            gu_ref[0, :, s] = g.astype(BF16)
            gu_ref[1, :, s] = u.astype(BF16)
            a_ref[:, s] = (g * _sigmoid(g) * u).astype(BF16)

    seq = lambda i: (i // tps, 0, 0)
    return _call(
        body, "ffn_up", (T // tm,),
        [
            pl.BlockSpec((tm, D), lambda i: (i, 0)),
            pl.BlockSpec((1, D), lambda i: (0, 0)),
            pl.BlockSpec((1, 1, D), seq),
            pl.BlockSpec((1, 1, D), seq),
            pl.BlockSpec((D, 2 * F), lambda i: (0, 0), pipeline_mode=pl.Buffered(1)),
        ],
        [
            pl.BlockSpec((2, tm, F), lambda i: (0, i, 0)),
            pl.BlockSpec((tm, F), lambda i: (i, 0)),
        ],
        [
            jax.ShapeDtypeStruct((2, T, F), BF16),
            jax.ShapeDtypeStruct((T, F), BF16),
        ],
        [],
        (x, gain, sh, sc, wgu), comm)


def _ffn_down(a, x, gate, wd, comm=None):
    T, F = a.shape
    D = x.shape[1]
    B = gate.shape[0]
    tm = _tile(2 * TOKEN_TILE, T // B)
    tps = (T // B) // tm

    def body(a_ref, x_ref, gate_ref, wd_ref, xo_ref, f_ref):
        f = _dot(a_ref[...], wd_ref[...])
        f_ref[...] = f.astype(BF16)
        xo_ref[...] = x_ref[...] + 0.5 * gate_ref[0] * f

    return _call(
        body, "ffn_down", (T // tm,),
        [
            pl.BlockSpec((tm, F), lambda i: (i, 0)),
            pl.BlockSpec((tm, D), lambda i: (i, 0)),
            pl.BlockSpec((1, 1, D), lambda i: (i // tps, 0, 0)),
            pl.BlockSpec((F, D), lambda i: (0, 0), pipeline_mode=pl.Buffered(1)),
        ],
        [pl.BlockSpec((tm, D), lambda i: (i, 0)), pl.BlockSpec((tm, D), lambda i: (i, 0))],
        [jax.ShapeDtypeStruct((T, D), F32), jax.ShapeDtypeStruct((T, D), BF16)],
        [],
        (a, x, gate, wd), comm)


def _ffn_fwd(x, gain, sh, sc, gate, wgu, wd, comm=None, head=None):
    T, D = x.shape
    F = wd.shape[0]
    B = sh.shape[0]
    tm = _tile(FWD_TILE, T // B)
    tps = (T // B) // tm
    slabs = _slabs(F, FF_SLAB)

    def body(x_ref, gain_ref, sh_ref, sc_ref, gate_ref, w_ref, wd_ref, *rest):
        if head is None:
            xo_ref, gu_ref, a_ref, f_ref = rest
        else:
            t_ref, fgain_ref, xo_ref, gu_ref, a_ref, f_ref, loss_ref, dfgain_ref = rest
        x = x_ref[...]
        h = (_rms(x)[0] * gain_ref[...] * (1.0 + sc_ref[0]) + sh_ref[0]).astype(BF16)

        def dots(s):
            return _dot(h, w_ref[:, s]), _dot(h, w_ref[:, slice(F + s.start, F + s.stop)])

        nxt = dots(slabs[0])
        for j, s in enumerate(slabs):
            g, u = nxt
            if j + 1 < len(slabs):
                nxt = dots(slabs[j + 1])
            gu_ref[0, :, s] = g.astype(BF16)
            gu_ref[1, :, s] = u.astype(BF16)
            a_ref[:, s] = (g * _sigmoid(g) * u).astype(BF16)
        f = _dot(a_ref[...], wd_ref[...])
        f_ref[...] = f.astype(BF16)
        xo = x + 0.5 * gate_ref[0] * f
        if head is None:
            xo_ref[...] = xo
        else:
            first = pl.program_id(0) == 0
            xh, r = _rms(xo)
            fgain = fgain_ref[...]
            err = xh * fgain - t_ref[...]
            _acc(loss_ref, first, jnp.zeros((8, 128), F32) + 0.5 * jnp.sum(err * err) / D)
            dout = err * (1.0 / D)
            _acc(dfgain_ref, first, jnp.sum(dout * xh, axis=0, keepdims=True))
            dy = dout * fgain
            xo_ref[...] = r * (dy - xh * jnp.mean(dy * xh, axis=-1, keepdims=True))

    seq = lambda i: (i // tps, 0, 0)
    row = lambda i: (i, 0)
    const = lambda i: (0, 0)
    in_specs = [
        pl.BlockSpec((tm, D), row),
        pl.BlockSpec((1, D), const),
        pl.BlockSpec((1, 1, D), seq),
        pl.BlockSpec((1, 1, D), seq),
        pl.BlockSpec((1, 1, D), seq),
        pl.BlockSpec((D, 2 * F), const, pipeline_mode=pl.Buffered(1)),
        pl.BlockSpec((F, D), const, pipeline_mode=pl.Buffered(1)),
    ]
    out_specs = [
        pl.BlockSpec((tm, D), row),
        pl.BlockSpec((2, tm, F), lambda i: (0, i, 0)),
        pl.BlockSpec((tm, F), row),
        pl.BlockSpec((tm, D), row),
    ]
    out_shape = [
        jax.ShapeDtypeStruct((T, D), F32),
        jax.ShapeDtypeStruct((2, T, F), BF16),
        jax.ShapeDtypeStruct((T, F), BF16),
        jax.ShapeDtypeStruct((T, D), BF16),
    ]
    args = (x, gain, sh, sc, gate, wgu, wd)
    if head is not None:
        in_specs += [pl.BlockSpec((tm, D), row), pl.BlockSpec((1, D), const)]
        out_specs += [pl.BlockSpec((8, 128), const), pl.BlockSpec((1, D), const)]
        out_shape += [jax.ShapeDtypeStruct((8, 128), F32), jax.ShapeDtypeStruct((1, D), F32)]
        args += tuple(head)
    return _call(body, "ffn_fwd", (T // tm,), in_specs, out_specs, out_shape, [], args, comm)


def _ffn_bwd(dxo, x, gu, f, gain, sh, sc, gate, wgu, wd, comm=None):
    T, D = x.shape
    F = wd.shape[0]
    B = sc.shape[0]
    tm = _tile(BWD_TILE, T // B)
    tps = (T // B) // tm
    slabs = _slabs(F, FF_SLAB)

    def body(dxo_ref, x_ref, gu_ref, f_ref, gain_ref, sh_ref, sc_ref, gate_ref, w_ref, wd_ref,
             dx_ref, dgu_ref, h_ref, df_ref, dsc_ref, dsh_ref, dgain_ref, dgate_ref):
        i = pl.program_id(0)
        first_of_seq = (i % tps) == 0
        gain = gain_ref[...]
        sc = sc_ref[0]
        dxo = dxo_ref[...]
        x = x_ref[...]
        df = (0.5 * gate_ref[0] * dxo).astype(BF16)
        df_ref[...] = df
        nxt = _dot_nt(df, wd_ref[slabs[0], :])
        for j, s in enumerate(slabs):
            da = nxt
            if j + 1 < len(slabs):
                nxt = _dot_nt(df, wd_ref[slabs[j + 1], :])
            g = gu_ref[0, :, s]
            sg = 1.0 / (1.0 + jnp.exp(-g))
            t = g * sg
            dab = da.astype(BF16)
            dgu_ref[1, :, s] = dab * t
            dgu_ref[0, :, s] = dab * gu_ref[1, :, s] * (sg + t - t * sg)
        dh = _dot_nt(dgu_ref[0], w_ref[:, 0:F]) + _dot_nt(dgu_ref[1], w_ref[:, F:])
        dx, dsc, dsh, dgain = _norm_mod_bwd(x, dh, gain, sc)
        dx_ref[...] = dxo + dx
        h_ref[...] = (_rms(x)[0] * gain * (1.0 + sc) + sh_ref[0]).astype(BF16)
        _acc(dsc_ref.at[0], first_of_seq, dsc)
        _acc(dsh_ref.at[0], first_of_seq, dsh)
        _acc(dgain_ref, i == 0, dgain)
        _acc(dgate_ref.at[0], first_of_seq, 0.5 * jnp.sum(dxo * f_ref[...].astype(F32), axis=0, keepdims=True))

    seq = lambda i: (i // tps, 0, 0)
    row = lambda i: (i, 0)
    return _call(
        body, "ffn_bwd", (T // tm,),
        [
            pl.BlockSpec((tm, D), row),
            pl.BlockSpec((tm, D), row),
            pl.BlockSpec((2, tm, F), lambda i: (0, i, 0)),
            pl.BlockSpec((tm, D), row),
            pl.BlockSpec((1, D), lambda i: (0, 0)),
            pl.BlockSpec((1, 1, D), seq),
            pl.BlockSpec((1, 1, D), seq),
            pl.BlockSpec((1, 1, D), seq),
            pl.BlockSpec((D, 2 * F), lambda i: (0, 0), pipeline_mode=pl.Buffered(1)),
            pl.BlockSpec((F, D), lambda i: (0, 0), pipeline_mode=pl.Buffered(1)),
        ],
        [
            pl.BlockSpec((tm, D), row),
            pl.BlockSpec((2, tm, F), lambda i: (0, i, 0)),
            pl.BlockSpec((tm, D), row),
            pl.BlockSpec((tm, D), row),
            pl.BlockSpec((1, 1, D), seq),
            pl.BlockSpec((1, 1, D), seq),
            pl.BlockSpec((1, D), lambda i: (0, 0)),
            pl.BlockSpec((1, 1, D), seq),
        ],
        [
            jax.ShapeDtypeStruct((T, D), F32),
            jax.ShapeDtypeStruct((2, T, F), BF16),
            jax.ShapeDtypeStruct((T, D), BF16),
            jax.ShapeDtypeStruct((T, D), BF16),
            jax.ShapeDtypeStruct((B, 1, D), F32),
            jax.ShapeDtypeStruct((B, 1, D), F32),
            jax.ShapeDtypeStruct((1, D), F32),
            jax.ShapeDtypeStruct((B, 1, D), F32),
        ],
        [],
        (dxo, x, gu, f, gain, sh, sc, gate, wgu, wd), comm)


def _wgrad(a, b, tmm, tn, col_major, name, tokens=WGRAD_TOKENS, comm=None):
    T, M = a.shape
    nb, _, Nb = b.shape
    N = nb * Nb
    tk = _tile(tokens, T)
    span = 2 if col_major else 1
    wide = span * tn
    npb = Nb // wide
    assert M % tmm == 0 and Nb % wide == 0
    if col_major:
        assert tmm == M
        shape = (N // tn, 2, M // 2, tn)
        out_spec = pl.BlockSpec((span, 2, M // 2, tn), lambda i, j, t: (j, 0, 0, 0))
    else:
        shape = (M // tmm, tmm, N)
        out_spec = pl.BlockSpec((None, tmm, tn), lambda i, j, t: (i, 0, j))

    def body(a_ref, b_ref, o_ref):
        @pl.when(pl.program_id(2) == 0)
        def _():
            o_ref[...] = jnp.zeros_like(o_ref)

        res = _dot_tn(a_ref[...], b_ref[...])
        if col_major:
            for s in range(span):
                for h in range(2):
                    o_ref[s, h] += res[h * (M // 2):(h + 1) * (M // 2), s * tn:(s + 1) * tn]
        else:
            o_ref[...] += res

    return _call(
        body, name, (M // tmm, N // wide, T // tk),
        [
            pl.BlockSpec((tk, tmm), lambda i, j, t: (t, i)),
            pl.BlockSpec((None, tk, wide), lambda i, j, t: (j // npb, t, j % npb)),
        ],
        [out_spec], [jax.ShapeDtypeStruct(shape, F32)], [],
        (a, b), comm)


def _mixin_bwd(dxo, x, dproj, gain, sc, win, comm=None):
    T, D = x.shape
    P = win.shape[1]
    B = sc.shape[0]
    tm = _tile(TOKEN_TILE, T // B)
    tps = (T // B) // tm

    def body(dxo_ref, x_ref, dp_ref, gain_ref, sc_ref, w_ref, dx_ref, dsc_ref, dsh_ref, dgain_ref):
        i = pl.program_id(0)
        first_of_seq = (i % tps) == 0
        halves = _slabs(tm, tm // 2)
        nxt = _dot_nt(dp_ref[halves[0], :], w_ref[...])
        sums = None
        for j, r in enumerate(halves):
            dh = nxt
            if j + 1 < len(halves):
                nxt = _dot_nt(dp_ref[halves[j + 1], :], w_ref[...])
            part = _norm_mod_bwd(x_ref[r, :], dh, gain_ref[...], sc_ref[0])
            dx_ref[r, :] = dxo_ref[r, :] + part[0]
            sums = part[1:] if sums is None else tuple(a + b for a, b in zip(sums, part[1:]))
        _acc(dsc_ref.at[0], first_of_seq, sums[0])
        _acc(dsh_ref.at[0], first_of_seq, sums[1])
        _acc(dgain_ref, i == 0, sums[2])

    seq = lambda i: (i // tps, 0, 0)
    row = lambda i: (i, 0)
    return _call(
        body, "mixin_bwd", (T // tm,),
        [
            pl.BlockSpec((tm, D), row),
            pl.BlockSpec((tm, D), row),
            pl.BlockSpec((tm, P), row),
            pl.BlockSpec((1, D), lambda i: (0, 0)),
            pl.BlockSpec((1, 1, D), seq),
            pl.BlockSpec((D, P), lambda i: (0, 0)),
        ],
        [
            pl.BlockSpec((tm, D), row),
            pl.BlockSpec((1, 1, D), seq),
            pl.BlockSpec((1, 1, D), seq),
            pl.BlockSpec((1, D), lambda i: (0, 0)),
        ],
        [
            jax.ShapeDtypeStruct((T, D), F32),
            jax.ShapeDtypeStruct((B, 1, D), F32),
            jax.ShapeDtypeStruct((B, 1, D), F32),
            jax.ShapeDtypeStruct((1, D), F32),
        ],
        [],
        (dxo, x, dproj, gain, sc, win), comm)


def _head_mean(z, pmat, exact=True):
    hi = z.astype(BF16)
    if not exact:
        return _dot(hi, pmat)
    lo = (z - hi.astype(F32)).astype(BF16)
    return _dot(hi, pmat) + _dot(lo, pmat)


def _gelu_parts(x):
    cdf = 0.5 * (1.0 + lax.erf(x * (1.0 / math.sqrt(2.0))))
    return x * cdf, cdf


def _gelu_grad(x, cdf):
    return cdf + x * jnp.exp(-0.5 * x * x) * (1.0 / math.sqrt(2.0 * math.pi))


LANES = 128


def _head_blocks(da):
    hd = da // N_HEADS
    lb = min(LANES, da)
    col = lax.broadcasted_iota(jnp.int32, (1, lb), 1)
    return lb, lb // hd, da // lb, [(col >= h * hd) & (col < (h + 1) * hd) for h in range(lb // hd)]


def _mix_heads(w_stack, v, da):
    lb, hpb, nb, masks = _head_blocks(da)
    outs = []
    for b in range(nb):
        res = _dot(w_stack[b * hpb * CHUNK:(b + 1) * hpb * CHUNK], v[:, b * lb:(b + 1) * lb])
        out = res[0:CHUNK]
        for h in range(1, hpb):
            out = jnp.where(masks[h], res[h * CHUNK:(h + 1) * CHUNK], out)
        outs.append(out)
    return outs[0] if nb == 1 else jnp.concatenate(outs, axis=1)


def _mix_heads_grad(dm, v, da):
    lb, hpb, nb, masks = _head_blocks(da)
    outs = []
    for b in range(nb):
        dmb = dm[:, b * lb:(b + 1) * lb]
        stack = jnp.concatenate([jnp.where(masks[h], dmb, jnp.zeros_like(dmb)) for h in range(hpb)], axis=0)
        outs.append(_dot_nt(stack, v[:, b * lb:(b + 1) * lb]))
    return outs[0] if nb == 1 else jnp.concatenate(outs, axis=0)


def _causal_stack(w, transposed):
    r = lax.broadcasted_iota(jnp.int32, w.shape, 0) % CHUNK
    c = lax.broadcasted_iota(jnp.int32, w.shape, 1)
    keep = (c >= r) if transposed else (c <= r)
    return jnp.where(keep, w, 0.0)


def _mix_core_forward(proj, zprev, prm, da, db, saved=None):
    n = proj.shape[0]
    ua = proj[:, 0:da]
    va = proj[:, da:2 * da]
    bg = proj[:, 2 * da:2 * da + db]
    cg = proj[:, 2 * da + db:2 * da + 2 * db]
    xb = proj[:, 2 * da + 2 * db:]
    if saved is None:
        ug, ucdf = _gelu_parts(ua)
        vg, vcdf = _gelu_parts(va)
        zc = vg - _head_mean(vg, prm["pmat"])
        rs = lax.rsqrt(_head_mean(zc * zc, prm["pmat"], exact=False) + EPS)
        vhat = zc * rs
        vln = (vhat * prm["lng"] + prm["lnb"]).astype(BF16)
        wst = _causal_stack(prm["wst"], False).astype(BF16)
        mixed = [_mix_heads(wst, vln[j * CHUNK:(j + 1) * CHUNK], da) + prm["bias"] for j in range(n // CHUNK)]
        mixed = mixed[0] if len(mixed) == 1 else jnp.concatenate(mixed, axis=0)
    else:
        ucdf, vcdf, vhat, rs, mixed = [saved[k].astype(F32) for k in range(5)]
        ug = ua * ucdf
        vln = (vhat * prm["lng"] + prm["lnb"]).astype(BF16)
    ya = ug * mixed
    z = cg * xb
    row = lax.broadcasted_iota(jnp.int32, z.shape, 0)
    z1 = jnp.where(row == 0, zprev[7:8], pltpu.roll(z, 1, 0))
    z2 = jnp.where(row == 0, zprev[6:7], jnp.where(row == 1, zprev[7:8], pltpu.roll(z, 2, 0)))
    cw = prm["convw"]
    conv = z2 * cw[0:1] + z1 * cw[1:2] + z * cw[2:3]
    yb = bg * conv
    yah, ra = _rms(ya)
    ybh, rb = _rms(yb)
    return dict(ua=ua, va=va, bg=bg, cg=cg, xb=xb, ug=ug, ucdf=ucdf, vcdf=vcdf, rs=rs, vhat=vhat, vln=vln,
                mixed=mixed, z=z, z1=z1, z2=z2, conv=conv, yah=yah, ra=ra, ybh=ybh, rb=rb)


def _mix_params(lng_ref, lnb_ref, wst_ref, bias_ref, pmat_ref, convw_ref):
    return dict(lng=lng_ref[...], lnb=lnb_ref[...], wst=wst_ref[...], bias=bias_ref[...],
                pmat=pmat_ref[...], convw=convw_ref[...])


def _mix_fwd(x, gain, sh, sc, gate, win, wout, lng, lnb, wst, bias, pmat, convw, og, comm=None):
    T, D = x.shape
    P = win.shape[1]
    B = gate.shape[0]
    da = lng.shape[1]
    db = convw.shape[1]
    tm = _tile(MIX_TILE, T // B)
    tps = (T // B) // tm

    def body(x_ref, gain_ref, sh_ref, sc_ref, gate_ref, win_ref, wout_ref, lng_ref, lnb_ref, wst_ref, bias_ref,
             pmat_ref, convw_ref, og_ref, xo_ref, proj_ref, h_ref, yn_ref, sv_ref, halo):
        i = pl.program_id(0)

        @pl.when((i % tps) == 0)
        def _():
            halo[...] = jnp.zeros_like(halo)

        h = (_rms(x_ref[...])[0] * gain_ref[...] * (1.0 + sc_ref[0]) + sh_ref[0]).astype(BF16)
        h_ref[...] = h
        proj_ref[...] = _dot(h, win_ref[...])
        prm = _mix_params(lng_ref, lnb_ref, wst_ref, bias_ref, pmat_ref, convw_ref)
        r = _mix_core_forward(proj_ref[...], halo[...], prm, da, db)
        halo[...] = r["z"][tm - 8:tm]
        for k, name in enumerate(("ucdf", "vcdf", "vhat", "rs", "mixed")):
            sv_ref[k] = r[name].astype(BF16)
        og = og_ref[...]
        yn_ref[:, 0:da] = (r["yah"] * og[:, 0:da]).astype(BF16)
        yn_ref[:, da:] = (r["ybh"] * og[:, da:]).astype(BF16)
        xo_ref[...] = x_ref[...] + gate_ref[0] * _dot(yn_ref[...], wout_ref[...])

    full = lambda a: pl.BlockSpec(a.shape, lambda i: (0,) * a.ndim)
    seq = lambda i: (i // tps, 0, 0)
    row = lambda i: (i, 0)
    return _call(
        body, "mix_fwd", (T // tm,),
        [
            pl.BlockSpec((tm, D), row),
            pl.BlockSpec((1, D), lambda i: (0, 0)),
            pl.BlockSpec((1, 1, D), seq),
            pl.BlockSpec((1, 1, D), seq),
            pl.BlockSpec((1, 1, D), seq),
            pl.BlockSpec((D, P), lambda i: (0, 0), pipeline_mode=pl.Buffered(1)),
            full(wout), full(lng), full(lnb), full(wst), full(bias), full(pmat), full(convw), full(og),
        ],
        [pl.BlockSpec((tm, D), row), pl.BlockSpec((tm, P), row), pl.BlockSpec((tm, D), row),
         pl.BlockSpec((tm, D), row), pl.BlockSpec((5, tm, da), lambda i: (0, i, 0))],
        [jax.ShapeDtypeStruct((T, D), F32), jax.ShapeDtypeStruct((T, P), F32), jax.ShapeDtypeStruct((T, D), BF16),
         jax.ShapeDtypeStruct((T, D), BF16), jax.ShapeDtypeStruct((5, T, da), BF16)],
        [pltpu.VMEM((8, db), F32)],
        (x, gain, sh, sc, gate, win, wout, lng, lnb, wst, bias, pmat, convw, og), comm)


def _mix_core_bwd(proj, sv, dxo, gate, wout, lng, lnb, wstt, pmat, convw, og, comm=None):
    T, P = proj.shape
    D = dxo.shape[1]
    B = gate.shape[0]
    da = lng.shape[1]
    db = convw.shape[1]
    assert da == db and P == 2 * da + 3 * db
    tm = _tile(MIX_TILE, T // B)
    tps = (T // B) // tm
    nt = T // tm
    hd = da // N_HEADS

    def body(proj_ref, cgp_ref, xbp_ref, sv_ref, dxo_ref, gate_ref, wout_ref, lng_ref, lnb_ref, wstt_ref,
             pmat_ref, convw_ref, og_ref,
             dproj_ref, do_ref, dgate_ref, dog_ref, dwst_ref, dbias_ref, dlng_ref, dlnb_ref, dconvw_ref, carry):
        i = pl.program_id(0)
        ri = nt - 1 - i
        first = i == 0
        end_of_seq = (ri % tps) == tps - 1
        start_of_seq = (ri % tps) == 0

        @pl.when(end_of_seq)
        def _():
            carry[...] = jnp.zeros_like(carry)

        prm = dict(lng=lng_ref[...], lnb=lnb_ref[...], pmat=pmat_ref[...], convw=convw_ref[...])
        zprev = jnp.where(start_of_seq, 0.0, cgp_ref[...] * xbp_ref[...])
        r = _mix_core_forward(proj_ref[...], zprev, prm, da, db, saved=sv_ref)
        og = og_ref[...]
        pmat = prm["pmat"]

        yn = jnp.concatenate([(r["yah"] * og[:, 0:da]).astype(BF16), (r["ybh"] * og[:, da:]).astype(BF16)], axis=1)
        dxo = dxo_ref[...]
        o = _dot(yn, wout_ref[...])
        _acc(dgate_ref.at[0], end_of_seq, jnp.sum(dxo * o, axis=0, keepdims=True))
        d_o = (gate_ref[0] * dxo).astype(BF16)
        do_ref[...] = d_o
        dyn = _dot_nt(d_o, wout_ref[...])

        def rms_bwd(dyn_g, yh, rr, og_g):
            dog_g = jnp.sum(dyn_g * yh, axis=0, keepdims=True)
            dyh = dyn_g * og_g
            return rr * (dyh - yh * jnp.mean(dyh * yh, axis=-1, keepdims=True)), dog_g

        dya, dog_a = rms_bwd(dyn[:, 0:da], r["yah"], r["ra"], og[:, 0:da])
        dyb, dog_b = rms_bwd(dyn[:, da:], r["ybh"], r["rb"], og[:, da:])
        _acc(dog_ref, first, jnp.concatenate([dog_a, dog_b], axis=1))

        dug = dya * r["mixed"]
        dmixed = dya * r["ug"]
        wstt_b = _causal_stack(wstt_ref[...], True).astype(BF16)
        dbias = jnp.zeros((CHUNK, da), F32)
        dwst = jnp.zeros((N_HEADS * CHUNK, CHUNK), F32)
        dvln = []
        for j in range(tm // CHUNK):
            dm = dmixed[j * CHUNK:(j + 1) * CHUNK]
            dbias = dbias + dm
            dmb = dm.astype(BF16)
            dwst = dwst + _mix_heads_grad(dmb, r["vln"][j * CHUNK:(j + 1) * CHUNK], da)
            dvln.append(_mix_heads(wstt_b, dmb, da))
        dvln = dvln[0] if len(dvln) == 1 else jnp.concatenate(dvln, axis=0)
        _acc(dbias_ref, first, dbias)
        _acc(dwst_ref, first, dwst)
        _acc(dlng_ref, first, jnp.sum(dvln * r["vhat"], axis=0, keepdims=True))
        _acc(dlnb_ref, first, jnp.sum(dvln, axis=0, keepdims=True))
        dvhat = dvln * prm["lng"]
        dvg = r["rs"] * (dvhat - _head_mean(dvhat, pmat, exact=False)
                         - r["vhat"] * _head_mean(dvhat * r["vhat"], pmat, exact=False))
        dproj_ref[:, 0:da] = (dug * _gelu_grad(r["ua"], r["ucdf"])).astype(BF16)
        dproj_ref[:, da:2 * da] = (dvg * _gelu_grad(r["va"], r["vcdf"])).astype(BF16)

        dproj_ref[:, 2 * da:2 * da + db] = (dyb * r["conv"]).astype(BF16)
        dconv = dyb * r["bg"]
        dcw = jnp.concatenate([
            jnp.sum(dconv * r["z2"], axis=0, keepdims=True),
            jnp.sum(dconv * r["z1"], axis=0, keepdims=True),
            jnp.sum(dconv * r["z"], axis=0, keepdims=True),
            jnp.zeros((5, db), F32)], axis=0)
        _acc(dconvw_ref, first, dcw)
        nxt = carry[...]
        row = lax.broadcasted_iota(jnp.int32, dconv.shape, 0)
        dc1 = jnp.where(row == tm - 1, nxt[0:1], pltpu.roll(dconv, tm - 1, 0))
        dc2 = jnp.where(row == tm - 2, nxt[0:1], jnp.where(row == tm - 1, nxt[1:2], pltpu.roll(dconv, tm - 2, 0)))
        carry[...] = dconv[0:8]
        cw = prm["convw"]
        dz = dconv * cw[2:3] + dc1 * cw[1:2] + dc2 * cw[0:1]
        dproj_ref[:, 2 * da + db:2 * da + 2 * db] = (dz * r["xb"]).astype(BF16)
        dproj_ref[:, 2 * da + 2 * db:] = (dz * r["cg"]).astype(BF16)

        @pl.when(i == nt - 1)
        def _():
            dwst_ref[...] = _causal_stack(dwst_ref[...], False)
            dbias_ref[...] = _head_mean(dbias_ref[...], pmat) * float(hd)

    full = lambda a: pl.BlockSpec(a.shape, lambda i: (0,) * a.ndim)
    const = lambda i: (0, 0)
    rev = lambda i: (nt - 1 - i, 0)
    prev8 = lambda col: (lambda i: (jnp.maximum((nt - 1 - i) * (tm // 8) - 1, 0), col))
    return _call(
        body, "mix_core_bwd", (nt,),
        [
            pl.BlockSpec((tm, P), rev),
            pl.BlockSpec((8, db), prev8((2 * da + db) // db)),
            pl.BlockSpec((8, db), prev8((2 * da + 2 * db) // db)),
            pl.BlockSpec((5, tm, da), lambda i: (0, nt - 1 - i, 0)),
            pl.BlockSpec((tm, D), rev),
            pl.BlockSpec((1, 1, D), lambda i: ((nt - 1 - i) // tps, 0, 0)),
            full(wout), full(lng), full(lnb), full(wstt), full(pmat), full(convw), full(og),
        ],
        [
            pl.BlockSpec((tm, P), rev),
            pl.BlockSpec((tm, D), rev),
            pl.BlockSpec((1, 1, D), lambda i: ((nt - 1 - i) // tps, 0, 0)),
            pl.BlockSpec((1, D), const),
            pl.BlockSpec((N_HEADS * CHUNK, CHUNK), const),
            pl.BlockSpec((CHUNK, da), const),
            pl.BlockSpec((1, da), const),
            pl.BlockSpec((1, da), const),
            pl.BlockSpec((8, db), const),
        ],
        [
            jax.ShapeDtypeStruct((T, P), BF16),
            jax.ShapeDtypeStruct((T, D), BF16),
            jax.ShapeDtypeStruct((B, 1, D), F32),
            jax.ShapeDtypeStruct((1, D), F32),
            jax.ShapeDtypeStruct((N_HEADS * CHUNK, CHUNK), F32),
            jax.ShapeDtypeStruct((CHUNK, da), F32),
            jax.ShapeDtypeStruct((1, da), F32),
            jax.ShapeDtypeStruct((1, da), F32),
            jax.ShapeDtypeStruct((8, db), F32),
        ],
        [pltpu.VMEM((8, db), F32)],
        (proj, proj, proj, sv, dxo, gate, wout, lng, lnb, wstt, pmat, convw, og), comm)


def _ada_fwd(c_all, ada_w, ada_b):
    n, D = c_all.shape
    L, _, sa = ada_w.shape
    tn = _tile(768, sa)

    def body(c_ref, w_ref, b_ref, act_ref, o_ref):
        c = c_ref[...]
        act = (c * _sigmoid(c)).astype(BF16)
        act_ref[...] = act
        o_ref[...] = _dot(act, w_ref[...].astype(BF16)) + b_ref[...]

    return _call(
        body, "ada_fwd", (L, sa // tn),
        [
            pl.BlockSpec((n, D), lambda l, j: (0, 0)),
            pl.BlockSpec((None, D, tn), lambda l, j: (l, 0, j)),
            pl.BlockSpec((None, 1, tn), lambda l, j: (l, 0, j)),
        ],
        [
            pl.BlockSpec((n, D), lambda l, j: (0, 0)),
            pl.BlockSpec((None, n, tn), lambda l, j: (l, 0, j)),
        ],
        [jax.ShapeDtypeStruct((n, D), BF16), jax.ShapeDtypeStruct((L, n, sa), F32)],
        [],
        (c_all, ada_w, ada_b))[0]


def _ada_bwd(c_act, d_ada, comm=None):
    n, D = c_act.shape
    L, _, sa = d_ada.shape
    tn = _tile(768, sa)

    def body(c_ref, d_ref, o_ref):
        o_ref[...] = _dot_tn(c_ref[...], d_ref[...])

    return _call(
        body, "ada_bwd", (L, sa // tn),
        [pl.BlockSpec((n, D), lambda l, j: (0, 0)), pl.BlockSpec((None, n, tn), lambda l, j: (l, 0, j))],
        [pl.BlockSpec((None, D, tn), lambda l, j: (l, 0, j))],
        [jax.ShapeDtypeStruct((L, D, sa), F32)],
        [],
        (c_act, d_ada), comm)


def _colsum(a):
    L, n, C = a.shape

    def body(a_ref, o_ref):
        o_ref[...] = jnp.sum(a_ref[...], axis=0, keepdims=True)

    return _call(
        body, "colsum", (L,),
        [pl.BlockSpec((None, n, C), lambda l: (l, 0, 0))],
        [pl.BlockSpec((None, 1, C), lambda l: (l, 0, 0))],
        [jax.ShapeDtypeStruct((L, 1, C), F32)],
        [],
        (a,))[0][0]


def _row_tile(rows, cols, nbuf):
    budget = VMEM_LIMIT // 3 // (2 * nbuf * 4 * cols)
    t = rows
    while t > max(budget, 8) and t % 2 == 0 and (t // 2) % 8 == 0:
        t //= 2
    return t


def _pair_sum(g, recv, core):
    n, _, R, C = g.shape
    tr = _row_tile(R, C, 3)

    def body(core_ref, g_ref, r_ref, o_ref):
        o_ref[...] = (g_ref[...] + r_ref[...]).astype(BF16)

    return pl.pallas_call(
        body,
        name="pair_sum",
        grid_spec=pltpu.PrefetchScalarGridSpec(
            num_scalar_prefetch=1,
            grid=(n, R // tr),
            in_specs=[
                pl.BlockSpec((None, None, tr, C), lambda i, r, core_ref: (i, core_ref[0], r, 0)),
                pl.BlockSpec((None, tr, C), lambda i, r, core_ref: (i, r, 0)),
            ],
            out_specs=pl.BlockSpec((None, tr, C), lambda i, r, core_ref: (i, r, 0)),
        ),
        out_shape=jax.ShapeDtypeStruct((n, R, C), BF16),
        compiler_params=pltpu.CompilerParams(dimension_semantics=("arbitrary", "arbitrary"),
                                             vmem_limit_bytes=VMEM_LIMIT),
    )(core, g, recv)


def _chip_sum(q, core, l, n_layers, prev):
    nq, R, C = q.shape
    tr = _row_tile(R, C, 4)

    def body(core_ref, q_ref, *rest):
        o_ref = rest[-1]
        s = q_ref[0].astype(F32)
        for j in range(1, nq):
            s = s + q_ref[j].astype(F32)
        o_ref[...] = s

    in_specs = [pl.BlockSpec((nq, tr, C), lambda r, core_ref: (0, r, 0))]
    args = [core, q]
    aliases = {}
    if prev is not None:
        in_specs.append(ANY)
        args.append(prev)
        aliases = {2: 0}
    return pl.pallas_call(
        body,
        name="chip_sum",
        grid_spec=pltpu.PrefetchScalarGridSpec(
            num_scalar_prefetch=1,
            grid=(R // tr,),
            in_specs=in_specs,
            out_specs=pl.BlockSpec((None, None, tr, C), lambda r, core_ref: (l, core_ref[0], r, 0)),
        ),
        out_shape=jax.ShapeDtypeStruct((n_layers, 2, R, C), F32),
        input_output_aliases=aliases,
        compiler_params=pltpu.CompilerParams(dimension_semantics=("arbitrary",), vmem_limit_bytes=VMEM_LIMIT),
    )(*args)


def _sum_blocks(a, n):
    M = a.shape[0] // n
    C = a.shape[1]

    def body(a_ref, o_ref):
        s = a_ref[0:M]
        for j in range(1, n):
            s = s + a_ref[j * M:(j + 1) * M]
        o_ref[...] = s

    return pl.pallas_call(
        body,
        name="sum_blocks",
        out_shape=jax.ShapeDtypeStruct((M, C), F32),
        compiler_params=pltpu.CompilerParams(vmem_limit_bytes=VMEM_LIMIT),
    )(a)


def _adamw(w, g, m, v, emit_grad=False):
    R, C = w.shape
    n_out = 4 if emit_grad else 3
    tr = _row_tile(R, C, 4 + n_out) if R % 8 == 0 else R

    def body(w_ref, g_ref, m_ref, v_ref, d_ref, nm_ref, nv_ref, *g_out):
        g = g_ref[...]
        m = ADAM_B1 * m_ref[...] + (1.0 - ADAM_B1) * g
        v = ADAM_B2 * v_ref[...] + (1.0 - ADAM_B2) * (g * g)
        m_hat = m / (1.0 - ADAM_B1 ** ADAM_STEP)
        v_hat = v / (1.0 - ADAM_B2 ** ADAM_STEP)
        d_ref[...] = -ADAM_LR * (m_hat / (jnp.sqrt(v_hat) + ADAM_EPS) + ADAM_WD * w_ref[...])
        nm_ref[...] = m
        nv_ref[...] = v
        if emit_grad:
            g_out[0][...] = g

    spec = pl.BlockSpec((tr, C), lambda i: (i, 0))
    return _call(body, "adamw", (R // tr,), [spec] * 4, [spec] * n_out, [jax.ShapeDtypeStruct((R, C), F32)] * n_out,
                 [], (w, g, m, v))[0]


def kernel(x, c, ada_w, ada_b, norm_ffn1_g, ffn1_w_gu, ffn1_w_down, norm_mix_g, mix_w_in, sgu_ln_g, sgu_ln_b, sgu_w_s, sgu_b, conv_w, out_norm_g, mix_w_out, norm_ffn2_g, ffn2_w_gu, ffn2_w_down, final_norm_g, loss_target, m_ada_w, m_ada_b, m_norm_ffn1_g, m_ffn1_w_gu, m_ffn1_w_down, m_norm_mix_g, m_mix_w_in, m_sgu_ln_g, m_sgu_ln_b, m_sgu_w_s, m_sgu_b, m_conv_w, m_out_norm_g, m_mix_w_out, m_norm_ffn2_g, m_ffn2_w_gu, m_ffn2_w_down, m_final_norm_g, v_ada_w, v_ada_b, v_norm_ffn1_g, v_ffn1_w_gu, v_ffn1_w_down, v_norm_mix_g, v_mix_w_in, v_sgu_ln_g, v_sgu_ln_b, v_sgu_w_s, v_sgu_b, v_conv_w, v_out_norm_g, v_mix_w_out, v_norm_ffn2_g, v_ffn2_w_gu, v_ffn2_w_down, v_final_norm_g):
    weights = dict(ada_w=ada_w, ada_b=ada_b, norm_ffn1_g=norm_ffn1_g, ffn1_w_gu=ffn1_w_gu, ffn1_w_down=ffn1_w_down,
                   norm_mix_g=norm_mix_g, mix_w_in=mix_w_in, sgu_ln_g=sgu_ln_g, sgu_ln_b=sgu_ln_b, sgu_w_s=sgu_w_s,
                   sgu_b=sgu_b, conv_w=conv_w, out_norm_g=out_norm_g, mix_w_out=mix_w_out, norm_ffn2_g=norm_ffn2_g,
                   ffn2_w_gu=ffn2_w_gu, ffn2_w_down=ffn2_w_down, final_norm_g=final_norm_g)
    m_in = dict(ada_w=m_ada_w, ada_b=m_ada_b, norm_ffn1_g=m_norm_ffn1_g, ffn1_w_gu=m_ffn1_w_gu,
                ffn1_w_down=m_ffn1_w_down, norm_mix_g=m_norm_mix_g, mix_w_in=m_mix_w_in, sgu_ln_g=m_sgu_ln_g,
                sgu_ln_b=m_sgu_ln_b, sgu_w_s=m_sgu_w_s, sgu_b=m_sgu_b, conv_w=m_conv_w, out_norm_g=m_out_norm_g,
                mix_w_out=m_mix_w_out, norm_ffn2_g=m_norm_ffn2_g, ffn2_w_gu=m_ffn2_w_gu, ffn2_w_down=m_ffn2_w_down,
                final_norm_g=m_final_norm_g)
    v_in = dict(ada_w=v_ada_w, ada_b=v_ada_b, norm_ffn1_g=v_norm_ffn1_g, ffn1_w_gu=v_ffn1_w_gu,
                ffn1_w_down=v_ffn1_w_down, norm_mix_g=v_norm_mix_g, mix_w_in=v_mix_w_in, sgu_ln_g=v_sgu_ln_g,
                sgu_ln_b=v_sgu_ln_b, sgu_w_s=v_sgu_w_s, sgu_b=v_sgu_b, conv_w=v_conv_w, out_norm_g=v_out_norm_g,
                mix_w_out=v_mix_w_out, norm_ffn2_g=v_norm_ffn2_g, ffn2_w_gu=v_ffn2_w_gu, ffn2_w_down=v_ffn2_w_down,
                final_norm_g=v_final_norm_g)

    B, S, D = x.shape
    T = B * S
    L = ada_w.shape[0]
    F = ffn1_w_down.shape[1] * N_CHIP
    P = mix_w_in.shape[2] * N_CHIP
    DA = D // 2
    DB = D - DA
    HD = DA // N_HEADS
    SA = ada_w.shape[2]
    n_all = B * N_DEV
    mx, my, mc = _position()
    chip = 2 * mx + my
    dev = 2 * chip + mc
    core = jnp.reshape(mc, (1,)).astype(jnp.int32)

    big = ["ffn1_w_gu", "ffn1_w_down", "mix_w_in", "mix_w_out", "ffn2_w_gu", "ffn2_w_down"]
    col_sharded = dict(ffn1_w_gu=True, ffn1_w_down=False, mix_w_in=True, mix_w_out=False,
                       ffn2_w_gu=True, ffn2_w_down=False)
    shards = {k: weights[k].astype(BF16) for k in big}
    gather = lambda l, *names: _gather_comm([(shards[k], l, col_sharded[k]) for k in names])
    full = [dict() for _ in range(L)]

    def arrived(l, names, res):
        full[l].update(zip(names, res))

    n_cw = L * conv_w.shape[1]
    cw_block = jnp.pad(conv_w.reshape(n_cw, conv_w.shape[2]), ((0, 8 - n_cw), (0, 0)))
    c_all, cw_all = _comm_call(_merge(_all_gather_comm(c.reshape(8, B * D // 8)), _all_gather_comm(cw_block)),
                               "gather_c")
    c_all = c_all.reshape(n_all, D)
    cw_all = cw_all.reshape(N_CHIP, 2, 8, conv_w.shape[2])[:, 0, :n_cw]
    conv_full = jnp.transpose(cw_all.reshape(N_CHIP, L, conv_w.shape[1], conv_w.shape[2]), (1, 2, 0, 3))
    conv_full = conv_full.reshape(L, conv_w.shape[1], DB)
    ada_b_mine = lax.dynamic_slice_in_dim(ada_b, chip * SA, SA, axis=1).reshape(L, 1, SA)
    c_act, ada_part = _ada_fwd(c_all, ada_w, ada_b_mine)
    ada_all, first_w = _comm_call(_merge(_all_gather_comm(ada_part.reshape(L * n_all, SA)), gather(0, big[0])),
                                  "gather_first")
    arrived(0, big[:1], [first_w])
    ada_all = ada_all.reshape(N_CHIP, 2, L, n_all, SA)[:, 0]
    ada_all = jnp.transpose(ada_all, (1, 2, 0, 3)).reshape(L, n_all, N_CHIP * SA)
    ada = lax.dynamic_slice_in_dim(ada_all, dev * B, B, axis=1).reshape(L, B, N_MOD, 1, D)
    mods = [[ada[l, :, j] for j in range(N_MOD)] for l in range(L)]

    x0 = x.reshape(T, D)
    gains = lambda name, l: weights[name][l].reshape(1, D)
    hmask = jnp.repeat(jnp.eye(N_HEADS, dtype=F32), HD, axis=0)
    pmat = (jnp.repeat(hmask, HD, axis=1) / HD).astype(BF16)

    def mix_consts(l):
        lng = jnp.tile(sgu_ln_g[l], N_HEADS).reshape(1, DA)
        lnb = jnp.tile(sgu_ln_b[l], N_HEADS).reshape(1, DA)
        wst = sgu_w_s[l].reshape(N_HEADS * CHUNK, CHUNK)
        wstt = jnp.swapaxes(sgu_w_s[l], 1, 2).reshape(N_HEADS * CHUNK, CHUNK)
        bias = jnp.repeat(jnp.transpose(sgu_b[l]), HD, axis=1)
        return lng, lnb, wst, wstt, bias

    def fetch(fn, *args, bring=(), **kw):
        bring = [(l, k) for l, k in bring if l < L]
        comm = _gather_comm([(shards[k], l, col_sharded[k]) for l, k in bring]) if bring else None
        res, got = fn(*args, comm, **kw)
        for (l, k), a in zip(bring, got):
            full[l][k] = a
        return res

    saved = []
    xc = x0
    for l in range(L):
        sh1, sc1, g1, sh2, sc2, g2, sh3, sc3, g3 = mods[l]
        lng, lnb, wst, wstt, bias = mix_consts(l)
        w = full[l]
        if l == 0:
            gu1, a1 = fetch(_ffn_up, xc, gains("norm_ffn1_g", l), sh1, sc1, w["ffn1_w_gu"],
                            bring=[(l, "ffn1_w_down"), (l, "mix_w_in"), (l, "mix_w_out")])
            xa, f1 = fetch(_ffn_down, a1, xc, g1, w["ffn1_w_down"], bring=[(l, "ffn2_w_down")])
        else:
            xa, gu1, a1, f1 = fetch(_ffn_fwd, xc, gains("norm_ffn1_g", l), sh1, sc1, g1, w["ffn1_w_gu"],
                                    w["ffn1_w_down"], bring=[(l, "ffn2_w_gu"), (l, "mix_w_in")])
        xb, proj, h2, yn, sv = fetch(_mix_fwd, xa, gains("norm_mix_g", l), sh2, sc2, g2, w["mix_w_in"], w["mix_w_out"],
                                     lng, lnb, wst, bias, pmat, conv_full[l], gains("out_norm_g", l),
                                     bring=[(l, "ffn2_w_gu")] if l == 0 else [(l, "ffn2_w_down")])
        if l + 1 < L:
            xd, gu2, a2, f2 = fetch(_ffn_fwd, xb, gains("norm_ffn2_g", l), sh3, sc3, g3, w["ffn2_w_gu"],
                                    w["ffn2_w_down"],
                                    bring=[(l + 1, "ffn1_w_gu"), (l + 1, "mix_w_out"), (l + 1, "ffn1_w_down")])
        else:
            dx, gu2, a2, f2, loss_block, d_final = fetch(
                _ffn_fwd, xb, gains("norm_ffn2_g", l), sh3, sc3, g3, w["ffn2_w_gu"], w["ffn2_w_down"],
                head=(loss_target.reshape(T, D), final_norm_g.reshape(1, D)))
            xd = None
        saved.append(dict(x0=xc, xa=xa, xb=xb, gu1=gu1, a1=a1, f1=f1, proj=proj, h2=h2, yn=yn, sv=sv,
                          gu2=gu2, a2=a2, f2=f2))
        xc = xd

    reduced = dict.fromkeys(big)

    def halves(name, g):
        if g.ndim == 4:
            return g
        return g.reshape(N_CHIP, 2, weights[name].shape[1] // 2, g.shape[-1])

    class Reduction:
        def __init__(self, l, name, g):
            self.l, self.name, self.g, self.stage = l, name, halves(name, g), 0
            self.ici_bytes = 3 * (g.size // 8) * 2

        def step(self):
            self.stage += 1
            if self.stage == 1:
                return _sibling_half_comm([self.g])
            if self.stage == 2:
                return _scatter_comm([_pair_sum(self.g, self.got[0], core)])
            if self.stage == 3:
                reduced[self.name] = _chip_sum(self.got[0], core, self.l, L, reduced[self.name])
                return _share_comm([reduced[self.name]], self.l)
            reduced[self.name] = self.got[0]
            return None

    active, extra, gathered = [], [], {}

    def carry(fn, *args, us=None):
        left = None if us is None else us * SCATTER_BYTES_PER_US
        riders = []
        for r in active:
            if r.stage == 1 and left is not None:
                if r.ici_bytes > left * SCATTER_OVERSHOOT:
                    continue
                left -= r.ici_bytes
            riders.append(r)
        comms = [r.step() for r in riders] + [cm for cm, _ in extra]
        takers = [functools.partial(setattr, r, "got") for r in riders] + [cb for _, cb in extra]
        extra.clear()
        if fn is None:
            res, got = None, (_comm_call(_merge(*comms), "reduce_alone") if comms else [])
        else:
            res, got = fn(*args, comm=_merge(*comms))
        at = 0
        for cm, take in zip(comms, takers):
            take(got[at:at + len(cm.out_shape)])
            at += len(cm.out_shape)
        for r in riders:
            if r.stage == 3:
                r.step()
                active.remove(r)
        return res

    def reduce_later(l, name, g):
        active.append(Reduction(l, name, g))

    small = [None] * L
    dwsts = [None] * L
    d_ada = [None] * L
    for l in reversed(range(L)):
        sh1, sc1, g1, sh2, sc2, g2, sh3, sc3, g3 = mods[l]
        lng, lnb, wst, wstt, bias = mix_consts(l)
        s = saved[l]
        w = full[l]
        last = l == 0
        dx, dgu2, h3, df2, dsc3, dsh3, dgain3, dg3 = carry(
            _ffn_bwd, dx, s["xb"], s["gu2"], s["f2"], gains("norm_ffn2_g", l), sh3, sc3, g3, w["ffn2_w_gu"],
            w["ffn2_w_down"], us=170)
        ffn2_grads = [
            lambda: reduce_later(l, "ffn2_w_gu", carry(_wgrad, h3, dgu2, D, 2 * F // N_CHIP, True, "wgrad_gu",
                                                       WGRAD_TOKENS // 2, us=110)[0]),
            lambda: reduce_later(l, "ffn2_w_down", carry(_wgrad, s["a2"], df2[None], F // 2, D, False, "wgrad_down",
                                                         us=50)[0])]
        if not last:
            ffn2_grads[0]()
            ffn2_grads[1]()
        dproj, d_o, dg2, dog, dwst, dbias, dlng, dlnb, dconvw = carry(
            _mix_core_bwd, s["proj"], s["sv"], dx, g2, w["mix_w_out"], lng, lnb, wstt, pmat, conv_full[l],
            gains("out_norm_g", l), us=150)
        mix_grads = [
            lambda: reduce_later(l, "mix_w_out", carry(_wgrad, s["yn"], d_o[None], D, D, False, "wgrad_out", us=30)[0]),
            lambda: reduce_later(l, "mix_w_in", carry(_wgrad, s["h2"], dproj[None], D, P // N_CHIP, True, "wgrad_in",
                                                      us=65)[0])]
        if not last:
            mix_grads[0]()
        dx, dsc2, dsh2, dgain2 = carry(_mixin_bwd, dx, s["xa"], dproj, gains("norm_mix_g", l), sc2, w["mix_w_in"], us=60)
        if not last:
            mix_grads[1]()
        dx, dgu, h1, df, dsc1, dsh1, dgain1, dg1 = carry(
            _ffn_bwd, dx, s["x0"], s["gu1"], s["f1"], gains("norm_ffn1_g", l), sh1, sc1, g1, w["ffn1_w_gu"],
            w["ffn1_w_down"], us=170)
        d_ada[l] = jnp.concatenate([dsh1, dsc1, dg1, dsh2, dsc2, dg2, dsh3, dsc3, dg3], axis=1).reshape(B, N_MOD * D)
        small[l] = [dgain1, dgain2, dgain3, dog, dlng, dlnb, dbias[:, ::HD], dconvw]
        dwsts[l] = dwst
        if last:
            flat = [a.reshape(-1, 128) for ll in range(L) for a in small[ll]]
            flat += [d_final.reshape(-1, 128), loss_block[0:1]]
            pad = (-sum(a.shape[0] for a in flat)) % 8
            packed = jnp.concatenate(flat + [jnp.zeros((pad, 128), F32)], axis=0)
            extra.append((_all_gather_comm(jnp.stack(d_ada).reshape(L * B, N_MOD * D)),
                          lambda got: gathered.update(d_ada=got[0])))
            extra.append((_all_gather_comm(packed), lambda got: gathered.update(small=got[0])))
            for ll in range(L):
                extra.append((_all_gather_comm(dwsts[ll]), lambda got, ll=ll: gathered.update({("dwst", ll): got[0]})))
        reduce_later(l, "ffn1_w_gu", carry(_wgrad, h1, dgu, D, 2 * F // N_CHIP, True, "wgrad_gu", WGRAD_TOKENS // 2,
                                           us=110)[0])
        reduce_later(l, "ffn1_w_down", carry(_wgrad, s["a1"], df[None], F // 2, D, False, "wgrad_down", us=50)[0])
        if last:
            ffn2_grads[0]()
            ffn2_grads[1]()
            mix_grads[1]()
            mix_grads[0]()
    grad_x = dx.reshape(B, S, D)

    def finished(name):
        while any(r.name == name for r in active):
            carry(None)
        return reduced[name].reshape(weights[name].shape)

    grads = {}
    d_ada_all = jnp.transpose(gathered["d_ada"].reshape(N_DEV, L, B, N_MOD * D), (1, 0, 2, 3))
    d_ada_all = d_ada_all.reshape(L, n_all, N_MOD * D)
    grads["ada_b"] = _colsum(d_ada_all).reshape(L, N_MOD * D)
    d_ada_mine = lax.dynamic_slice_in_dim(d_ada_all, chip * SA, SA, axis=2).astype(BF16)
    grads["ada_w"] = _ada_bwd(c_act, d_ada_mine)[0][0]

    total = _sum_blocks(gathered["small"].reshape(-1, 128), N_DEV)
    pieces, at = [], 0
    for a in flat:
        pieces.append(total[at:at + a.shape[0]])
        at += a.shape[0]
    per_layer = len(small[0])
    stack = lambda j, shape: jnp.stack([pieces[l * per_layer + j].reshape(shape) for l in range(L)])
    grads["norm_ffn1_g"] = stack(0, (D,))
    grads["norm_mix_g"] = stack(1, (D,))
    grads["norm_ffn2_g"] = stack(2, (D,))
    grads["out_norm_g"] = stack(3, (D,))
    grads["sgu_ln_g"] = stack(4, (N_HEADS, HD)).sum(axis=1)
    grads["sgu_ln_b"] = stack(5, (N_HEADS, HD)).sum(axis=1)
    grads["sgu_b"] = jnp.swapaxes(stack(6, (CHUNK, N_HEADS)), 1, 2)
    g_conv = stack(7, (8, DB))[:, :conv_w.shape[1]]
    grads["conv_w"] = lax.dynamic_slice_in_dim(g_conv, chip * conv_w.shape[2], conv_w.shape[2], axis=2)
    grads["final_norm_g"] = pieces[-2].reshape(D)
    loss = pieces[-1][0, 0]
    grads["sgu_w_s"] = jnp.stack([_sum_blocks(gathered["dwst", l].reshape(-1, CHUNK), N_DEV) for l in range(L)])
    grads["sgu_w_s"] = grads["sgu_w_s"].reshape(L, N_HEADS, CHUNK, CHUNK)

    names = list(weights)
    delta, new_m, new_v = {}, {}, {}
    for k in big:
        grads[k] = finished(k)
    for k in names:
        wk = weights[k]
        view = (1, wk.shape[0]) if wk.ndim == 1 else (-1, wk.shape[-1])
        d, nm, nv, *g_again = _adamw(wk.reshape(view), grads[k].reshape(view), m_in[k].reshape(view),
                                     v_in[k].reshape(view), emit_grad=k in big)
        delta[k], new_m[k], new_v[k] = d.reshape(wk.shape), nm.reshape(wk.shape), nv.reshape(wk.shape)
        if g_again:
            grads[k] = g_again[0].reshape(wk.shape)

    return (loss, grad_x, *[grads[k] for k in names], *[delta[k] for k in names],
            *[new_m[k] for k in names], *[new_v[k] for k in names])
```

```python
import functools
import math

import jax
import jax.numpy as jnp
from jax import lax
from jax.experimental import pallas as pl
from jax.experimental.pallas import tpu as pltpu

F32 = jnp.float32
BF16 = jnp.bfloat16
MESH = pl.DeviceIdType.MESH

N_HEADS = 8
CHUNK = 128
N_MOD = 9
EPS = 1e-6
N_DEV = 8
N_CHIP = 4

ADAM_LR = 0.001
ADAM_B1 = 0.9
ADAM_B2 = 0.999
ADAM_EPS = 1e-08
ADAM_WD = 0.01
ADAM_STEP = 10

TOKEN_TILE = 512
BWD_TILE = 256
FWD_TILE = 512
FF_SLAB = 768
MIX_TILE = 512
MIX_BWD_TILE = 512
WGRAD_TOKENS = 2048
VMEM_LIMIT = 56 * 1024 * 1024

SCATTER_BYTES_PER_US = 68_000
SCATTER_OVERSHOOT = 1.25

ANY = pl.BlockSpec(memory_space=pl.ANY)


def _tile(pref, n):
    t = min(pref, n)
    assert n % t == 0, (pref, n)
    return t


def _slabs(n, width):
    return [slice(c0, min(c0 + width, n)) for c0 in range(0, n, width)]


def _dot(a, b):
    return jnp.dot(a, b, preferred_element_type=F32)


def _dot_nt(a, b):
    return lax.dot_general(a, b, (((1,), (1,)), ((), ())), preferred_element_type=F32)


def _dot_tn(a, b):
    return lax.dot_general(a, b, (((0,), (0,)), ((), ())), preferred_element_type=F32)


def _sigmoid(x):
    return 1.0 / (1.0 + jnp.exp(-x))


def _sigmoid_fast(x):
    return pl.reciprocal(1.0 + jnp.exp(-x), approx=True)


def _rms(x):
    r = lax.rsqrt(jnp.mean(x * x, axis=-1, keepdims=True) + EPS)
    return x * r, r


def _norm_mod_bwd(x, dh, gain, sc):
    xh, r = _rms(x)
    dsc = jnp.sum(dh * (xh * gain), axis=0, keepdims=True)
    dsh = jnp.sum(dh, axis=0, keepdims=True)
    dn = dh * (1.0 + sc)
    dgain = jnp.sum(dn * xh, axis=0, keepdims=True)
    dy = dn * gain
    dx = r * (dy - xh * jnp.mean(dy * xh, axis=-1, keepdims=True))
    return dx, dsc, dsh, dgain


def _acc(ref, first, val):
    @pl.when(first)
    def _():
        ref[...] = val

    @pl.when(jnp.logical_not(first))
    def _():
        ref[...] += val


class _Comm:
    def __init__(self, args, out_shape, scratch, phases, aliases=None):
        self.args, self.out_shape, self.scratch = list(args), list(out_shape), list(scratch)
        self.phases, self.aliases = phases, dict(aliases or {})


def _merge(*comms):
    comms = [c for c in comms if c is not None]
    if len(comms) <= 1:
        return comms[0] if comms else None
    args = [a for c in comms for a in c.args]
    out_shape = [o for c in comms for o in c.out_shape]
    scratch = [s for c in comms for s in c.scratch]
    aliases, ai, oi = {}, 0, 0
    for c in comms:
        aliases.update({ai + i: oi + o for i, o in c.aliases.items()})
        ai += len(c.args)
        oi += len(c.out_shape)

    def phases(ins, outs, sems):
        parts, ai, oi, si = [], 0, 0, 0
        for c in comms:
            parts.append(c.phases(ins[ai:ai + len(c.args)], outs[oi:oi + len(c.out_shape)], sems[si:si + len(c.scratch)]))
            ai, oi, si = ai + len(c.args), oi + len(c.out_shape), si + len(c.scratch)

        def run(k):
            def go():
                for p in parts:
                    if p[k] is not None:
                        p[k]()
            return go
        return run(0), run(1), run(2)

    return _Comm(args, out_shape, scratch, phases, aliases)


def _call(body, name, grid, in_specs, out_specs, out_shape, scratch, args, comm=None):
    n_in, n_out, n_scr = len(in_specs), len(out_specs), len(scratch)
    sem = ("arbitrary",) * len(grid)
    params = pltpu.CompilerParams(dimension_semantics=sem, vmem_limit_bytes=VMEM_LIMIT)
    if comm is None:
        res = pl.pallas_call(body, name=name, grid=grid, in_specs=in_specs, out_specs=out_specs, out_shape=out_shape,
                             scratch_shapes=scratch, compiler_params=params)(*args)
        return list(res), []
    m_in, m_out = len(comm.args), len(comm.out_shape)

    def full(*refs):
        c_in, c_min = refs[:n_in], refs[n_in:n_in + m_in]
        o = n_in + m_in
        c_out, c_mout = refs[o:o + n_out], refs[o + n_out:o + n_out + m_out]
        o += n_out + m_out
        c_scr, c_sem = refs[o:o + n_scr], refs[o + n_scr:]
        start, mid, finish = comm.phases(c_min, c_mout, c_sem)
        ids = [pl.program_id(a) for a in range(len(grid))]
        first = functools.reduce(jnp.logical_and, [i == 0 for i in ids])
        last = functools.reduce(jnp.logical_and, [i == g - 1 for i, g in zip(ids, grid)])
        pl.when(first)(start)
        if mid is not None:
            pl.when(last)(mid)
        body(*c_in, *c_out, *c_scr)
        pl.when(last)(finish)

    res = pl.pallas_call(
        full, name=name, grid=grid,
        in_specs=list(in_specs) + [ANY] * m_in,
        out_specs=list(out_specs) + [ANY] * m_out,
        out_shape=list(out_shape) + comm.out_shape,
        scratch_shapes=list(scratch) + comm.scratch,
        input_output_aliases={n_in + i: n_out + o for i, o in comm.aliases.items()},
        compiler_params=params,
    )(*args, *comm.args)
    return list(res[:n_out]), list(res[n_out:])


def _comm_call(comm, name):
    m_in, m_out = len(comm.args), len(comm.out_shape)

    def body(*refs):
        start, mid, finish = comm.phases(refs[:m_in], refs[m_in:m_in + m_out], refs[m_in + m_out:])
        start()
        if mid is not None:
            mid()
        finish()

    res = pl.pallas_call(
        body, name=name, in_specs=[ANY] * m_in, out_specs=[ANY] * m_out, out_shape=comm.out_shape,
        scratch_shapes=comm.scratch, input_output_aliases=comm.aliases,
    )(*comm.args)
    return list(res)


def _position():
    return lax.axis_index("x"), lax.axis_index("y"), lax.axis_index("c")


def _gather_comm(items):
    n = len(items)
    half = [s.shape[1] // 2 for s, _, _ in items]

    def full_shape(i):
        s, _, col = items[i]
        _, R, C = s.shape
        return jax.ShapeDtypeStruct((R, N_CHIP * C) if col else (N_CHIP * R, C), s.dtype)

    def phases(ins, outs, sems):
        send_sems, recv_sems, local_sems = sems
        x, y, c = _position()

        def region(i, chip, h):
            s, _, col = items[i]
            _, R, C = s.shape
            if col:
                return outs[i].at[pl.ds(h * half[i], half[i]), pl.ds(chip * C, C)]
            return outs[i].at[pl.ds(chip * R + h * half[i], half[i]), :]

        def mine(i, h):
            return ins[i].at[items[i][1], pl.ds(h * half[i], half[i]), :]

        def copies(kx, ky, kc):
            k_me = 2 * kx + ky
            sibling = (kx, ky, 1 - kc)
            chips = [(1 - kx, ky), (kx, 1 - ky), (1 - kx, 1 - ky)]
            local, first, passed, arrive_ici, arrive_d2d = [], [], [], [], []

            def remote(src, dst, s, to):
                return pltpu.make_async_remote_copy(src_ref=src, dst_ref=dst, send_sem=send_sems.at[s],
                                                    recv_sem=recv_sems.at[s], device_id=to, device_id_type=MESH)

            for i in range(n):
                for h in range(2):
                    local.append(pltpu.make_async_copy(mine(i, h), region(i, k_me, h), local_sems.at[2 * i + h]))
                for j, (px, py) in enumerate(chips):
                    s = 6 * i + j
                    first.append(remote(mine(i, kc), region(i, k_me, kc), s, (px, py, kc)))
                    got = region(i, 2 * px + py, kc)
                    arrive_ici.append(remote(got, got, s, (px, py, kc)))
                    passed.append(remote(got, got, s + 3, sibling))
                    other = region(i, 2 * px + py, 1 - kc)
                    arrive_d2d.append(remote(other, other, s + 3, sibling))
            return local, first, passed, arrive_ici, arrive_d2d

        def on_each_device(fn):
            def go():
                for kx in range(2):
                    for ky in range(2):
                        for kc in range(2):
                            pl.when((x == kx) & (y == ky) & (c == kc))(functools.partial(fn, *copies(kx, ky, kc)))
            return go

        def start(local, first, passed, arrive_ici, arrive_d2d):
            for cp in local + first:
                cp.start()

        def mid(local, first, passed, arrive_ici, arrive_d2d):
            for a, p in zip(arrive_ici, passed):
                a.wait_recv()
                p.start()

        def finish(local, first, passed, arrive_ici, arrive_d2d):
            for a in arrive_d2d:
                a.wait_recv()
            for cp in first + passed:
                cp.wait_send()
            for cp in local:
                cp.wait()

        return on_each_device(start), on_each_device(mid), on_each_device(finish)

    scratch = [pltpu.SemaphoreType.DMA((6 * n,)), pltpu.SemaphoreType.DMA((6 * n,)), pltpu.SemaphoreType.DMA((2 * n,))]
    return _Comm([s for s, _, _ in items], [full_shape(i) for i in range(n)], scratch, phases)


def _sibling_half_comm(gs):
    n = len(gs)

    def phases(ins, outs, sems):
        send_sems, recv_sems = sems
        x, y, c = _position()

        def copies():
            return [pltpu.make_async_remote_copy(
                src_ref=ins[i].at[:, 1 - c], dst_ref=outs[i], send_sem=send_sems.at[i], recv_sem=recv_sems.at[i],
                device_id=(x, y, 1 - c), device_id_type=MESH) for i in range(n)]

        def start():
            for cp in copies():
                cp.start()

        def finish():
            for cp in copies():
                cp.wait()

        return start, None, finish

    out_shape = [jax.ShapeDtypeStruct(g.shape[:1] + g.shape[2:], g.dtype) for g in gs]
    return _Comm(gs, out_shape, [pltpu.SemaphoreType.DMA((n,)), pltpu.SemaphoreType.DMA((n,))], phases)


def _scatter_comm(ps):
    n = len(ps)

    def phases(ins, outs, sems):
        send_sems, recv_sems, local_sems = sems
        x, y, c = _position()
        k_me = 2 * x + y
        chips = [(1 - x, y), (x, 1 - y), (1 - x, 1 - y)]

        def copies():
            local = [pltpu.make_async_copy(ins[i].at[k_me], outs[i].at[k_me], local_sems.at[i]) for i in range(n)]
            remote = [pltpu.make_async_remote_copy(
                src_ref=ins[i].at[2 * px + py], dst_ref=outs[i].at[k_me],
                send_sem=send_sems.at[3 * i + j], recv_sem=recv_sems.at[3 * i + j],
                device_id=(px, py, c), device_id_type=MESH) for i in range(n) for j, (px, py) in enumerate(chips)]
            return local, remote

        def start():
            local, remote = copies()
            for cp in local + remote:
                cp.start()

        def finish():
            local, remote = copies()
            for cp in remote + local:
                cp.wait()

        return start, None, finish

    scratch = [pltpu.SemaphoreType.DMA((3 * n,)), pltpu.SemaphoreType.DMA((3 * n,)), pltpu.SemaphoreType.DMA((n,))]
    return _Comm(ps, [jax.ShapeDtypeStruct(p.shape, p.dtype) for p in ps], scratch, phases)


def _share_comm(rs, l):
    n = len(rs)

    def phases(ins, outs, sems):
        send_sems, recv_sems = sems
        x, y, c = _position()

        def copy(i, h):
            return pltpu.make_async_remote_copy(
                src_ref=outs[i].at[l, h], dst_ref=outs[i].at[l, h], send_sem=send_sems.at[i], recv_sem=recv_sems.at[i],
                device_id=(x, y, 1 - c), device_id_type=MESH)

        def start():
            for i in range(n):
                copy(i, c).start()

        def finish():
            for i in range(n):
                copy(i, 1 - c).wait_recv()
            for i in range(n):
                copy(i, c).wait_send()

        return start, None, finish

    return _Comm(rs, [jax.ShapeDtypeStruct(r.shape, r.dtype) for r in rs],
                 [pltpu.SemaphoreType.DMA((n,)), pltpu.SemaphoreType.DMA((n,))], phases,
                 aliases={i: i for i in range(n)})


def _all_gather_comm(block):
    def phases(ins, outs, sems):
        send_sems, recv_sems, local_sem = sems
        (src,), (out,) = ins, outs
        x, y, c = _position()
        sibling = (x, y, 1 - c)
        chips = [(1 - x, y), (x, 1 - y), (1 - x, 1 - y)]

        def slot(px, py, pc):
            return out.at[4 * px + 2 * py + pc]

        def copy(k, blk, to, own=False):
            return pltpu.make_async_remote_copy(
                src_ref=src if own else slot(*blk), dst_ref=slot(*blk),
                send_sem=send_sems.at[k], recv_sem=recv_sems.at[k], device_id=to, device_id_type=MESH)

        mine = lambda: pltpu.make_async_copy(src, slot(x, y, c), local_sem.at[0])
        first = lambda: [copy(0, (x, y, c), sibling, True)] + [
            copy(1 + j, (x, y, c), (*chip, c), True) for j, chip in enumerate(chips)]
        passed = lambda: [copy(4 + j, (*chip, c), sibling) for j, chip in enumerate(chips)]

        def start():
            mine().start()
            for cp in first():
                cp.start()

        def mid():
            for j, (chip, p) in enumerate(zip(chips, passed())):
                copy(1 + j, (*chip, c), (x, y, c)).wait_recv()
                p.start()

        def finish():
            copy(0, sibling, (x, y, c)).wait_recv()
            for j, chip in enumerate(chips):
                copy(4 + j, (*chip, 1 - c), (x, y, c)).wait_recv()
            for cp in first() + passed():
                cp.wait_send()
            mine().wait()

        return start, mid, finish

    scratch = [pltpu.SemaphoreType.DMA((7,)), pltpu.SemaphoreType.DMA((7,)), pltpu.SemaphoreType.DMA((1,))]
    return _Comm([block], [jax.ShapeDtypeStruct((N_DEV,) + block.shape, block.dtype)], scratch, phases)


def _ffn_up(x, gain, sh, sc, wgu, comm=None):
    T, D = x.shape
    F = wgu.shape[1] // 2
    B = sh.shape[0]
    tm = _tile(TOKEN_TILE, T // B)
    tps = (T // B) // tm
    slabs = _slabs(F, FF_SLAB)

    def body(x_ref, gain_ref, sh_ref, sc_ref, w_ref, gu_ref, a_ref):
        xh, _ = _rms(x_ref[...])
        h = (xh * gain_ref[...] * (1.0 + sc_ref[0]) + sh_ref[0]).astype(BF16)

        def dots(s):
            return _dot(h, w_ref[:, s]), _dot(h, w_ref[:, slice(F + s.start, F + s.stop)])

        nxt = dots(slabs[0])
        for j, s in enumerate(slabs):
            g, u = nxt
            if j + 1 < len(slabs):
                nxt = dots(slabs[j + 1])
            gu_ref[0, :, s] = g.astype(BF16)
            gu_ref[1, :, s] = u.astype(BF16)
            a_ref[:, s] = (g * _sigmoid(g) * u).astype(BF16)

    seq = lambda i: (i // tps, 0, 0)
    return _call(
        body, "ffn_up", (T // tm,),
        [
            pl.BlockSpec((tm, D), lambda i: (i, 0)),
            pl.BlockSpec((1, D), lambda i: (0, 0)),
            pl.BlockSpec((1, 1, D), seq),
            pl.BlockSpec((1, 1, D), seq),
            pl.BlockSpec((D, 2 * F), lambda i: (0, 0), pipeline_mode=pl.Buffered(1)),
        ],
        [
            pl.BlockSpec((2, tm, F), lambda i: (0, i, 0)),
            pl.BlockSpec((tm, F), lambda i: (i, 0)),
        ],
        [
            jax.ShapeDtypeStruct((2, T, F), BF16),
            jax.ShapeDtypeStruct((T, F), BF16),
        ],
        [],
        (x, gain, sh, sc, wgu), comm)


def _ffn_down(a, x, gate, wd, comm=None):
    T, F = a.shape
    D = x.shape[1]
    B = gate.shape[0]
    tm = _tile(2 * TOKEN_TILE, T // B)
    tps = (T // B) // tm

    def body(a_ref, x_ref, gate_ref, wd_ref, xo_ref, f_ref):
        f = _dot(a_ref[...], wd_ref[...])
        f_ref[...] = f.astype(BF16)
        xo_ref[...] = x_ref[...] + 0.5 * gate_ref[0] * f

    return _call(
        body, "ffn_down", (T // tm,),
        [
            pl.BlockSpec((tm, F), lambda i: (i, 0)),
            pl.BlockSpec((tm, D), lambda i: (i, 0)),
            pl.BlockSpec((1, 1, D), lambda i: (i // tps, 0, 0)),
            pl.BlockSpec((F, D), lambda i: (0, 0), pipeline_mode=pl.Buffered(1)),
        ],
        [pl.BlockSpec((tm, D), lambda i: (i, 0)), pl.BlockSpec((tm, D), lambda i: (i, 0))],
        [jax.ShapeDtypeStruct((T, D), F32), jax.ShapeDtypeStruct((T, D), BF16)],
        [],
        (a, x, gate, wd), comm)


def _ffn_fwd(x, gain, sh, sc, gate, wgu, wd, comm=None, head=None):
    T, D = x.shape
    F = wd.shape[0]
    B = sh.shape[0]
    tm = _tile(FWD_TILE, T // B)
    tps = (T // B) // tm
    slabs = _slabs(F, FF_SLAB)

    def body(x_ref, gain_ref, sh_ref, sc_ref, gate_ref, w_ref, wd_ref, *rest):
        if head is None:
            xo_ref, gu_ref, a_ref, f_ref = rest
        else:
            t_ref, fgain_ref, xo_ref, gu_ref, a_ref, f_ref, loss_ref, dfgain_ref = rest
        x = x_ref[...]
        h = (_rms(x)[0] * gain_ref[...] * (1.0 + sc_ref[0]) + sh_ref[0]).astype(BF16)

        def dots(s):
            return _dot(h, w_ref[:, s]), _dot(h, w_ref[:, slice(F + s.start, F + s.stop)])

        nxt = dots(slabs[0])
        for j, s in enumerate(slabs):
            g, u = nxt
            if j + 1 < len(slabs):
                nxt = dots(slabs[j + 1])
            gu_ref[0, :, s] = g.astype(BF16)
            gu_ref[1, :, s] = u.astype(BF16)
            a_ref[:, s] = (g * _sigmoid(g) * u).astype(BF16)
        f = _dot(a_ref[...], wd_ref[...])
        f_ref[...] = f.astype(BF16)
        xo = x + 0.5 * gate_ref[0] * f
        if head is None:
            xo_ref[...] = xo
        else:
            first = pl.program_id(0) == 0
            xh, r = _rms(xo)
            fgain = fgain_ref[...]
            err = xh * fgain - t_ref[...]
            _acc(loss_ref, first, jnp.zeros((8, 128), F32) + 0.5 * jnp.sum(err * err) / D)
            dout = err * (1.0 / D)
            _acc(dfgain_ref, first, jnp.sum(dout * xh, axis=0, keepdims=True))
            dy = dout * fgain
            xo_ref[...] = r * (dy - xh * jnp.mean(dy * xh, axis=-1, keepdims=True))

    seq = lambda i: (i // tps, 0, 0)
    row = lambda i: (i, 0)
    const = lambda i: (0, 0)
    in_specs = [
        pl.BlockSpec((tm, D), row),
        pl.BlockSpec((1, D), const),
        pl.BlockSpec((1, 1, D), seq),
        pl.BlockSpec((1, 1, D), seq),
        pl.BlockSpec((1, 1, D), seq),
        pl.BlockSpec((D, 2 * F), const, pipeline_mode=pl.Buffered(1)),
        pl.BlockSpec((F, D), const, pipeline_mode=pl.Buffered(1)),
    ]
    out_specs = [
        pl.BlockSpec((tm, D), row),
        pl.BlockSpec((2, tm, F), lambda i: (0, i, 0)),
        pl.BlockSpec((tm, F), row),
        pl.BlockSpec((tm, D), row),
    ]
    out_shape = [
        jax.ShapeDtypeStruct((T, D), F32),
        jax.ShapeDtypeStruct((2, T, F), BF16),
        jax.ShapeDtypeStruct((T, F), BF16),
        jax.ShapeDtypeStruct((T, D), BF16),
    ]
    args = (x, gain, sh, sc, gate, wgu, wd)
    if head is not None:
        in_specs += [pl.BlockSpec((tm, D), row), pl.BlockSpec((1, D), const)]
        out_specs += [pl.BlockSpec((8, 128), const), pl.BlockSpec((1, D), const)]
        out_shape += [jax.ShapeDtypeStruct((8, 128), F32), jax.ShapeDtypeStruct((1, D), F32)]
        args += tuple(head)
    return _call(body, "ffn_fwd", (T // tm,), in_specs, out_specs, out_shape, [], args, comm)


def _ffn_bwd(dxo, x, gu, f, gain, sh, sc, gate, wgu, wd, comm=None):
    T, D = x.shape
    F = wd.shape[0]
    B = sc.shape[0]
    tm = _tile(BWD_TILE, T // B)
    tps = (T // B) // tm
    slabs = _slabs(F, FF_SLAB)

    def body(dxo_ref, x_ref, gu_ref, f_ref, gain_ref, sh_ref, sc_ref, gate_ref, w_ref, wd_ref,
             dx_ref, dgu_ref, h_ref, df_ref, dsc_ref, dsh_ref, dgain_ref, dgate_ref):
        i = pl.program_id(0)
        first_of_seq = (i % tps) == 0
        gain = gain_ref[...]
        sc = sc_ref[0]
        dxo = dxo_ref[...]
        x = x_ref[...]
        df = (0.5 * gate_ref[0] * dxo).astype(BF16)
        df_ref[...] = df
        nxt = _dot_nt(df, wd_ref[slabs[0], :])
        for j, s in enumerate(slabs):
            da = nxt
            if j + 1 < len(slabs):
                nxt = _dot_nt(df, wd_ref[slabs[j + 1], :])
            g = gu_ref[0, :, s]
            sg = 1.0 / (1.0 + jnp.exp(-g))
            t = g * sg
            dab = da.astype(BF16)
            dgu_ref[1, :, s] = dab * t
            dgu_ref[0, :, s] = dab * gu_ref[1, :, s] * (sg + t - t * sg)
        dh = _dot_nt(dgu_ref[0], w_ref[:, 0:F]) + _dot_nt(dgu_ref[1], w_ref[:, F:])
        dx, dsc, dsh, dgain = _norm_mod_bwd(x, dh, gain, sc)
        dx_ref[...] = dxo + dx
        h_ref[...] = (_rms(x)[0] * gain * (1.0 + sc) + sh_ref[0]).astype(BF16)
        _acc(dsc_ref.at[0], first_of_seq, dsc)
        _acc(dsh_ref.at[0], first_of_seq, dsh)
        _acc(dgain_ref, i == 0, dgain)
        _acc(dgate_ref.at[0], first_of_seq, 0.5 * jnp.sum(dxo * f_ref[...].astype(F32), axis=0, keepdims=True))

    seq = lambda i: (i // tps, 0, 0)
    row = lambda i: (i, 0)
    return _call(
        body, "ffn_bwd", (T // tm,),
        [
            pl.BlockSpec((tm, D), row),
            pl.BlockSpec((tm, D), row),
            pl.BlockSpec((2, tm, F), lambda i: (0, i, 0)),
            pl.BlockSpec((tm, D), row),
            pl.BlockSpec((1, D), lambda i: (0, 0)),
            pl.BlockSpec((1, 1, D), seq),
            pl.BlockSpec((1, 1, D), seq),
            pl.BlockSpec((1, 1, D), seq),
            pl.BlockSpec((D, 2 * F), lambda i: (0, 0), pipeline_mode=pl.Buffered(1)),
            pl.BlockSpec((F, D), lambda i: (0, 0), pipeline_mode=pl.Buffered(1)),
        ],
        [
            pl.BlockSpec((tm, D), row),
            pl.BlockSpec((2, tm, F), lambda i: (0, i, 0)),
            pl.BlockSpec((tm, D), row),
            pl.BlockSpec((tm, D), row),
            pl.BlockSpec((1, 1, D), seq),
            pl.BlockSpec((1, 1, D), seq),
            pl.BlockSpec((1, D), lambda i: (0, 0)),
            pl.BlockSpec((1, 1, D), seq),
        ],
        [
            jax.ShapeDtypeStruct((T, D), F32),
            jax.ShapeDtypeStruct((2, T, F), BF16),
            jax.ShapeDtypeStruct((T, D), BF16),
            jax.ShapeDtypeStruct((T, D), BF16),
            jax.ShapeDtypeStruct((B, 1, D), F32),
            jax.ShapeDtypeStruct((B, 1, D), F32),
            jax.ShapeDtypeStruct((1, D), F32),
            jax.ShapeDtypeStruct((B, 1, D), F32),
        ],
        [],
        (dxo, x, gu, f, gain, sh, sc, gate, wgu, wd), comm)


def _wgrad(a, b, tmm, tn, col_major, name, tokens=WGRAD_TOKENS, comm=None):
    T, M = a.shape
    nb, _, Nb = b.shape
    N = nb * Nb
    tk = _tile(tokens, T)
    span = 2 if col_major else 1
    wide = span * tn
    npb = Nb // wide
    assert M % tmm == 0 and Nb % wide == 0
    if col_major:
        assert tmm == M
        shape = (N // tn, 2, M // 2, tn)
        out_spec = pl.BlockSpec((span, 2, M // 2, tn), lambda i, j, t: (j, 0, 0, 0))
    else:
        shape = (M // tmm, tmm, N)
        out_spec = pl.BlockSpec((None, tmm, tn), lambda i, j, t: (i, 0, j))

    def body(a_ref, b_ref, o_ref):
        @pl.when(pl.program_id(2) == 0)
        def _():
            o_ref[...] = jnp.zeros_like(o_ref)

        res = _dot_tn(a_ref[...], b_ref[...])
        if col_major:
            for s in range(span):
                for h in range(2):
                    o_ref[s, h] += res[h * (M // 2):(h + 1) * (M // 2), s * tn:(s + 1) * tn]
        else:
            o_ref[...] += res

    return _call(
        body, name, (M // tmm, N // wide, T // tk),
        [
            pl.BlockSpec((tk, tmm), lambda i, j, t: (t, i)),
            pl.BlockSpec((None, tk, wide), lambda i, j, t: (j // npb, t, j % npb)),
        ],
        [out_spec], [jax.ShapeDtypeStruct(shape, F32)], [],
        (a, b), comm)


def _mixin_bwd(dxo, x, dproj, gain, sc, win, comm=None):
    T, D = x.shape
    P = win.shape[1]
    B = sc.shape[0]
    tm = _tile(TOKEN_TILE, T // B)
    tps = (T // B) // tm

    def body(dxo_ref, x_ref, dp_ref, gain_ref, sc_ref, w_ref, dx_ref, dsc_ref, dsh_ref, dgain_ref):
        i = pl.program_id(0)
        first_of_seq = (i % tps) == 0
        halves = _slabs(tm, tm // 2)
        nxt = _dot_nt(dp_ref[halves[0], :], w_ref[...])
        sums = None
        for j, r in enumerate(halves):
            dh = nxt
            if j + 1 < len(halves):
                nxt = _dot_nt(dp_ref[halves[j + 1], :], w_ref[...])
            part = _norm_mod_bwd(x_ref[r, :], dh, gain_ref[...], sc_ref[0])
            dx_ref[r, :] = dxo_ref[r, :] + part[0]
            sums = part[1:] if sums is None else tuple(a + b for a, b in zip(sums, part[1:]))
        _acc(dsc_ref.at[0], first_of_seq, sums[0])
        _acc(dsh_ref.at[0], first_of_seq, sums[1])
        _acc(dgain_ref, i == 0, sums[2])

    seq = lambda i: (i // tps, 0, 0)
    row = lambda i: (i, 0)
    return _call(
        body, "mixin_bwd", (T // tm,),
        [
            pl.BlockSpec((tm, D), row),
            pl.BlockSpec((tm, D), row),
            pl.BlockSpec((tm, P), row),
            pl.BlockSpec((1, D), lambda i: (0, 0)),
            pl.BlockSpec((1, 1, D), seq),
            pl.BlockSpec((D, P), lambda i: (0, 0)),
        ],
        [
            pl.BlockSpec((tm, D), row),
            pl.BlockSpec((1, 1, D), seq),
            pl.BlockSpec((1, 1, D), seq),
            pl.BlockSpec((1, D), lambda i: (0, 0)),
        ],
        [
            jax.ShapeDtypeStruct((T, D), F32),
            jax.ShapeDtypeStruct((B, 1, D), F32),
            jax.ShapeDtypeStruct((B, 1, D), F32),
            jax.ShapeDtypeStruct((1, D), F32),
        ],
        [],
        (dxo, x, dproj, gain, sc, win), comm)


def _head_mean(z, pmat, exact=True):
    hi = z.astype(BF16)
    if not exact:
        return _dot(hi, pmat)
    lo = (z - hi.astype(F32)).astype(BF16)
    return _dot(hi, pmat) + _dot(lo, pmat)


def _gelu_parts(x):
    cdf = 0.5 * (1.0 + lax.erf(x * (1.0 / math.sqrt(2.0))))
    return x * cdf, cdf


def _gelu_grad(x, cdf):
    return cdf + x * jnp.exp(-0.5 * x * x) * (1.0 / math.sqrt(2.0 * math.pi))


LANES = 128


def _head_blocks(da):
    hd = da // N_HEADS
    lb = min(LANES, da)
    col = lax.broadcasted_iota(jnp.int32, (1, lb), 1)
    return lb, lb // hd, da // lb, [(col >= h * hd) & (col < (h + 1) * hd) for h in range(lb // hd)]


def _mix_heads(w_stack, v, da):
    lb, hpb, nb, masks = _head_blocks(da)
    outs = []
    for b in range(nb):
        res = _dot(w_stack[b * hpb * CHUNK:(b + 1) * hpb * CHUNK], v[:, b * lb:(b + 1) * lb])
        out = res[0:CHUNK]
        for h in range(1, hpb):
            out = jnp.where(masks[h], res[h * CHUNK:(h + 1) * CHUNK], out)
        outs.append(out)
    return outs[0] if nb == 1 else jnp.concatenate(outs, axis=1)


def _mix_heads_grad(dm, v, da):
    lb, hpb, nb, masks = _head_blocks(da)
    outs = []
    for b in range(nb):
        dmb = dm[:, b * lb:(b + 1) * lb]
        stack = jnp.concatenate([jnp.where(masks[h], dmb, jnp.zeros_like(dmb)) for h in range(hpb)], axis=0)
        outs.append(_dot_nt(stack, v[:, b * lb:(b + 1) * lb]))
    return outs[0] if nb == 1 else jnp.concatenate(outs, axis=0)


def _causal_stack(w, transposed):
    r = lax.broadcasted_iota(jnp.int32, w.shape, 0) % CHUNK
    c = lax.broadcasted_iota(jnp.int32, w.shape, 1)
    keep = (c >= r) if transposed else (c <= r)
    return jnp.where(keep, w, 0.0)


def _mix_core_forward(proj, zprev, prm, da, db, saved=None):
    n = proj.shape[0]
    ua = proj[:, 0:da]
    va = proj[:, da:2 * da]
    bg = proj[:, 2 * da:2 * da + db]
    cg = proj[:, 2 * da + db:2 * da + 2 * db]
    xb = proj[:, 2 * da + 2 * db:]
    if saved is None:
        ug, ucdf = _gelu_parts(ua)
        vg, vcdf = _gelu_parts(va)
        zc = vg - _head_mean(vg, prm["pmat"])
        rs = lax.rsqrt(_head_mean(zc * zc, prm["pmat"], exact=False) + EPS)
        vhat = zc * rs
        vln = (vhat * prm["lng"] + prm["lnb"]).astype(BF16)
        wst = _causal_stack(prm["wst"], False).astype(BF16)
        mixed = [_mix_heads(wst, vln[j * CHUNK:(j + 1) * CHUNK], da) + prm["bias"] for j in range(n // CHUNK)]
        mixed = mixed[0] if len(mixed) == 1 else jnp.concatenate(mixed, axis=0)
    else:
        ucdf, vcdf, vhat, rs, mixed = [saved[k].astype(F32) for k in range(5)]
        ug = ua * ucdf
        vln = (vhat * prm["lng"] + prm["lnb"]).astype(BF16)
    ya = ug * mixed
    z = cg * xb
    row = lax.broadcasted_iota(jnp.int32, z.shape, 0)
    z1 = jnp.where(row == 0, zprev[7:8], pltpu.roll(z, 1, 0))
    z2 = jnp.where(row == 0, zprev[6:7], jnp.where(row == 1, zprev[7:8], pltpu.roll(z, 2, 0)))
    cw = prm["convw"]
    conv = z2 * cw[0:1] + z1 * cw[1:2] + z * cw[2:3]
    yb = bg * conv
    yah, ra = _rms(ya)
    ybh, rb = _rms(yb)
    return dict(ua=ua, va=va, bg=bg, cg=cg, xb=xb, ug=ug, ucdf=ucdf, vcdf=vcdf, rs=rs, vhat=vhat, vln=vln,
                mixed=mixed, z=z, z1=z1, z2=z2, conv=conv, yah=yah, ra=ra, ybh=ybh, rb=rb)


def _mix_params(lng_ref, lnb_ref, wst_ref, bias_ref, pmat_ref, convw_ref):
    return dict(lng=lng_ref[...], lnb=lnb_ref[...], wst=wst_ref[...], bias=bias_ref[...],
                pmat=pmat_ref[...], convw=convw_ref[...])


def _mix_fwd(x, gain, sh, sc, gate, win, wout, lng, lnb, wst, bias, pmat, convw, og, comm=None):
    T, D = x.shape
    P = win.shape[1]
    B = gate.shape[0]
    da = lng.shape[1]
    db = convw.shape[1]
    tm = _tile(MIX_TILE, T // B)
    tps = (T // B) // tm

    def body(x_ref, gain_ref, sh_ref, sc_ref, gate_ref, win_ref, wout_ref, lng_ref, lnb_ref, wst_ref, bias_ref,
             pmat_ref, convw_ref, og_ref, xo_ref, proj_ref, h_ref, yn_ref, sv_ref, halo):
        i = pl.program_id(0)

        @pl.when((i % tps) == 0)
        def _():
            halo[...] = jnp.zeros_like(halo)

        h = (_rms(x_ref[...])[0] * gain_ref[...] * (1.0 + sc_ref[0]) + sh_ref[0]).astype(BF16)
        h_ref[...] = h
        proj_ref[...] = _dot(h, win_ref[...])
        prm = _mix_params(lng_ref, lnb_ref, wst_ref, bias_ref, pmat_ref, convw_ref)
        r = _mix_core_forward(proj_ref[...], halo[...], prm, da, db)
        halo[...] = r["z"][tm - 8:tm]
        for k, name in enumerate(("ucdf", "vcdf", "vhat", "rs", "mixed")):
            sv_ref[k] = r[name].astype(BF16)
        og = og_ref[...]
        yn_ref[:, 0:da] = (r["yah"] * og[:, 0:da]).astype(BF16)
        yn_ref[:, da:] = (r["ybh"] * og[:, da:]).astype(BF16)
        xo_ref[...] = x_ref[...] + gate_ref[0] * _dot(yn_ref[...], wout_ref[...])

    full = lambda a: pl.BlockSpec(a.shape, lambda i: (0,) * a.ndim)
    seq = lambda i: (i // tps, 0, 0)
    row = lambda i: (i, 0)
    return _call(
        body, "mix_fwd", (T // tm,),
        [
            pl.BlockSpec((tm, D), row),
            pl.BlockSpec((1, D), lambda i: (0, 0)),
            pl.BlockSpec((1, 1, D), seq),
            pl.BlockSpec((1, 1, D), seq),
            pl.BlockSpec((1, 1, D), seq),
            pl.BlockSpec((D, P), lambda i: (0, 0), pipeline_mode=pl.Buffered(1)),
            full(wout), full(lng), full(lnb), full(wst), full(bias), full(pmat), full(convw), full(og),
        ],
        [pl.BlockSpec((tm, D), row), pl.BlockSpec((tm, P), row), pl.BlockSpec((tm, D), row),
         pl.BlockSpec((tm, D), row), pl.BlockSpec((5, tm, da), lambda i: (0, i, 0))],
        [jax.ShapeDtypeStruct((T, D), F32), jax.ShapeDtypeStruct((T, P), F32), jax.ShapeDtypeStruct((T, D), BF16),
         jax.ShapeDtypeStruct((T, D), BF16), jax.ShapeDtypeStruct((5, T, da), BF16)],
        [pltpu.VMEM((8, db), F32)],
        (x, gain, sh, sc, gate, win, wout, lng, lnb, wst, bias, pmat, convw, og), comm)


def _mix_core_bwd(proj, sv, dxo, gate, wout, lng, lnb, wstt, pmat, convw, og, comm=None):
    T, P = proj.shape
    D = dxo.shape[1]
    B = gate.shape[0]
    da = lng.shape[1]
    db = convw.shape[1]
    assert da == db and P == 2 * da + 3 * db
    tm = _tile(MIX_BWD_TILE, T // B)
    tps = (T // B) // tm
    nt = T // tm
    hd = da // N_HEADS

    def body(proj_ref, cgp_ref, xbp_ref, sv_ref, dxo_ref, gate_ref, wout_ref, lng_ref, lnb_ref, wstt_ref,
             pmat_ref, convw_ref, og_ref,
             dproj_ref, do_ref, dgate_ref, dog_ref, dwst_ref, dbias_ref, dlng_ref, dlnb_ref, dconvw_ref, carry):
        i = pl.program_id(0)
        ri = nt - 1 - i
        first = i == 0
        end_of_seq = (ri % tps) == tps - 1
        start_of_seq = (ri % tps) == 0

        @pl.when(end_of_seq)
        def _():
            carry[...] = jnp.zeros_like(carry)

        prm = dict(lng=lng_ref[...], lnb=lnb_ref[...], pmat=pmat_ref[...], convw=convw_ref[...])
        zprev = jnp.where(start_of_seq, 0.0, cgp_ref[...] * xbp_ref[...])
        r = _mix_core_forward(proj_ref[...], zprev, prm, da, db, saved=sv_ref)
        og = og_ref[...]
        pmat = prm["pmat"]

        yn = jnp.concatenate([(r["yah"] * og[:, 0:da]).astype(BF16), (r["ybh"] * og[:, da:]).astype(BF16)], axis=1)
        dxo = dxo_ref[...]
        o = _dot(yn, wout_ref[...])
        _acc(dgate_ref.at[0], end_of_seq, jnp.sum(dxo * o, axis=0, keepdims=True))
        d_o = (gate_ref[0] * dxo).astype(BF16)
        do_ref[...] = d_o
        dyn = _dot_nt(d_o, wout_ref[...])

        def rms_bwd(dyn_g, yh, rr, og_g):
            dog_g = jnp.sum(dyn_g * yh, axis=0, keepdims=True)
            dyh = dyn_g * og_g
            return rr * (dyh - yh * jnp.mean(dyh * yh, axis=-1, keepdims=True)), dog_g

        dya, dog_a = rms_bwd(dyn[:, 0:da], r["yah"], r["ra"], og[:, 0:da])
        dyb, dog_b = rms_bwd(dyn[:, da:], r["ybh"], r["rb"], og[:, da:])
        _acc(dog_ref, first, jnp.concatenate([dog_a, dog_b], axis=1))

        dug = dya * r["mixed"]
        dmixed = dya * r["ug"]
        wstt_b = _causal_stack(wstt_ref[...], True).astype(BF16)
        dbias = jnp.zeros((CHUNK, da), F32)
        dwst = jnp.zeros((N_HEADS * CHUNK, CHUNK), F32)
        dvln = []
        for j in range(tm // CHUNK):
            dm = dmixed[j * CHUNK:(j + 1) * CHUNK]
            dbias = dbias + dm
            dmb = dm.astype(BF16)
            dwst = dwst + _mix_heads_grad(dmb, r["vln"][j * CHUNK:(j + 1) * CHUNK], da)
            dvln.append(_mix_heads(wstt_b, dmb, da))
        dvln = dvln[0] if len(dvln) == 1 else jnp.concatenate(dvln, axis=0)
        _acc(dbias_ref, first, dbias)
        _acc(dwst_ref, first, dwst)
        _acc(dlng_ref, first, jnp.sum(dvln * r["vhat"], axis=0, keepdims=True))
        _acc(dlnb_ref, first, jnp.sum(dvln, axis=0, keepdims=True))
        dvhat = dvln * prm["lng"]
        dvg = r["rs"] * (dvhat - _head_mean(dvhat, pmat, exact=False)
                         - r["vhat"] * _head_mean(dvhat * r["vhat"], pmat, exact=False))
        dproj_ref[:, 0:da] = (dug * _gelu_grad(r["ua"], r["ucdf"])).astype(BF16)
        dproj_ref[:, da:2 * da] = (dvg * _gelu_grad(r["va"], r["vcdf"])).astype(BF16)

        dproj_ref[:, 2 * da:2 * da + db] = (dyb * r["conv"]).astype(BF16)
        dconv = dyb * r["bg"]
        dcw = jnp.concatenate([
            jnp.sum(dconv * r["z2"], axis=0, keepdims=True),
            jnp.sum(dconv * r["z1"], axis=0, keepdims=True),
            jnp.sum(dconv * r["z"], axis=0, keepdims=True),
            jnp.zeros((5, db), F32)], axis=0)
        _acc(dconvw_ref, first, dcw)
        nxt = carry[...]
        row = lax.broadcasted_iota(jnp.int32, dconv.shape, 0)
        dc1 = jnp.where(row == tm - 1, nxt[0:1], pltpu.roll(dconv, tm - 1, 0))
        dc2 = jnp.where(row == tm - 2, nxt[0:1], jnp.where(row == tm - 1, nxt[1:2], pltpu.roll(dconv, tm - 2, 0)))
        carry[...] = dconv[0:8]
        cw = prm["convw"]
        dz = dconv * cw[2:3] + dc1 * cw[1:2] + dc2 * cw[0:1]
        dproj_ref[:, 2 * da + db:2 * da + 2 * db] = (dz * r["xb"]).astype(BF16)
        dproj_ref[:, 2 * da + 2 * db:] = (dz * r["cg"]).astype(BF16)

        @pl.when(i == nt - 1)
        def _():
            dwst_ref[...] = _causal_stack(dwst_ref[...], False)
            dbias_ref[...] = _head_mean(dbias_ref[...], pmat) * float(hd)

    full = lambda a: pl.BlockSpec(a.shape, lambda i: (0,) * a.ndim)
    const = lambda i: (0, 0)
    rev = lambda i: (nt - 1 - i, 0)
    prev8 = lambda col: (lambda i: (jnp.maximum((nt - 1 - i) * (tm // 8) - 1, 0), col))
    return _call(
        body, "mix_core_bwd", (nt,),
        [
            pl.BlockSpec((tm, P), rev),
            pl.BlockSpec((8, db), prev8((2 * da + db) // db)),
            pl.BlockSpec((8, db), prev8((2 * da + 2 * db) // db)),
            pl.BlockSpec((5, tm, da), lambda i: (0, nt - 1 - i, 0)),
            pl.BlockSpec((tm, D), rev),
            pl.BlockSpec((1, 1, D), lambda i: ((nt - 1 - i) // tps, 0, 0)),
            full(wout), full(lng), full(lnb), full(wstt), full(pmat), full(convw), full(og),
        ],
        [
            pl.BlockSpec((tm, P), rev),
            pl.BlockSpec((tm, D), rev),
            pl.BlockSpec((1, 1, D), lambda i: ((nt - 1 - i) // tps, 0, 0)),
            pl.BlockSpec((1, D), const),
            pl.BlockSpec((N_HEADS * CHUNK, CHUNK), const),
            pl.BlockSpec((CHUNK, da), const),
            pl.BlockSpec((1, da), const),
            pl.BlockSpec((1, da), const),
            pl.BlockSpec((8, db), const),
        ],
        [
            jax.ShapeDtypeStruct((T, P), BF16),
            jax.ShapeDtypeStruct((T, D), BF16),
            jax.ShapeDtypeStruct((B, 1, D), F32),
            jax.ShapeDtypeStruct((1, D), F32),
            jax.ShapeDtypeStruct((N_HEADS * CHUNK, CHUNK), F32),
            jax.ShapeDtypeStruct((CHUNK, da), F32),
            jax.ShapeDtypeStruct((1, da), F32),
            jax.ShapeDtypeStruct((1, da), F32),
            jax.ShapeDtypeStruct((8, db), F32),
        ],
        [pltpu.VMEM((8, db), F32)],
        (proj, proj, proj, sv, dxo, gate, wout, lng, lnb, wstt, pmat, convw, og), comm)


def _ada_fwd(c_all, ada_w, ada_b):
    n, D = c_all.shape
    L, _, sa = ada_w.shape
    tn = _tile(768, sa)

    def body(c_ref, w_ref, b_ref, act_ref, o_ref):
        c = c_ref[...]
        act = (c * _sigmoid(c)).astype(BF16)
        act_ref[...] = act
        o_ref[...] = _dot(act, w_ref[...].astype(BF16)) + b_ref[...]

    return _call(
        body, "ada_fwd", (L, sa // tn),
        [
            pl.BlockSpec((n, D), lambda l, j: (0, 0)),
            pl.BlockSpec((None, D, tn), lambda l, j: (l, 0, j)),
            pl.BlockSpec((None, 1, tn), lambda l, j: (l, 0, j)),
        ],
        [
            pl.BlockSpec((n, D), lambda l, j: (0, 0)),
            pl.BlockSpec((None, n, tn), lambda l, j: (l, 0, j)),
        ],
        [jax.ShapeDtypeStruct((n, D), BF16), jax.ShapeDtypeStruct((L, n, sa), F32)],
        [],
        (c_all, ada_w, ada_b))[0]


def _ada_bwd(c_act, d_ada, comm=None):
    n, D = c_act.shape
    L, _, sa = d_ada.shape
    tn = _tile(768, sa)

    def body(c_ref, d_ref, o_ref):
        o_ref[...] = _dot_tn(c_ref[...], d_ref[...])

    return _call(
        body, "ada_bwd", (L, sa // tn),
        [pl.BlockSpec((n, D), lambda l, j: (0, 0)), pl.BlockSpec((None, n, tn), lambda l, j: (l, 0, j))],
        [pl.BlockSpec((None, D, tn), lambda l, j: (l, 0, j))],
        [jax.ShapeDtypeStruct((L, D, sa), F32)],
        [],
        (c_act, d_ada), comm)


def _colsum(a):
    L, n, C = a.shape

    def body(a_ref, o_ref):
        o_ref[...] = jnp.sum(a_ref[...], axis=0, keepdims=True)

    return _call(
        body, "colsum", (L,),
        [pl.BlockSpec((None, n, C), lambda l: (l, 0, 0))],
        [pl.BlockSpec((None, 1, C), lambda l: (l, 0, 0))],
        [jax.ShapeDtypeStruct((L, 1, C), F32)],
        [],
        (a,))[0][0]


def _row_tile(rows, cols, nbuf):
    budget = VMEM_LIMIT // 3 // (2 * nbuf * 4 * cols)
    t = rows
    while t > max(budget, 8) and t % 2 == 0 and (t // 2) % 8 == 0:
        t //= 2
    return t


def _pair_sum(g, recv, core):
    n, _, R, C = g.shape
    tr = _row_tile(R, C, 3)

    def body(core_ref, g_ref, r_ref, o_ref):
        o_ref[...] = (g_ref[...] + r_ref[...]).astype(BF16)

    return pl.pallas_call(
        body,
        name="pair_sum",
        grid_spec=pltpu.PrefetchScalarGridSpec(
            num_scalar_prefetch=1,
            grid=(n, R // tr),
            in_specs=[
                pl.BlockSpec((None, None, tr, C), lambda i, r, core_ref: (i, core_ref[0], r, 0)),
                pl.BlockSpec((None, tr, C), lambda i, r, core_ref: (i, r, 0)),
            ],
            out_specs=pl.BlockSpec((None, tr, C), lambda i, r, core_ref: (i, r, 0)),
        ),
        out_shape=jax.ShapeDtypeStruct((n, R, C), BF16),
        compiler_params=pltpu.CompilerParams(dimension_semantics=("arbitrary", "arbitrary"),
                                             vmem_limit_bytes=VMEM_LIMIT),
    )(core, g, recv)


def _chip_sum(q, core, l, n_layers, prev):
    nq, R, C = q.shape
    tr = _row_tile(R, C, 4)

    def body(core_ref, q_ref, *rest):
        o_ref = rest[-1]
        s = q_ref[0].astype(F32)
        for j in range(1, nq):
            s = s + q_ref[j].astype(F32)
        o_ref[...] = s

    in_specs = [pl.BlockSpec((nq, tr, C), lambda r, core_ref: (0, r, 0))]
    args = [core, q]
    aliases = {}
    if prev is not None:
        in_specs.append(ANY)
        args.append(prev)
        aliases = {2: 0}
    return pl.pallas_call(
        body,
        name="chip_sum",
        grid_spec=pltpu.PrefetchScalarGridSpec(
            num_scalar_prefetch=1,
            grid=(R // tr,),
            in_specs=in_specs,
            out_specs=pl.BlockSpec((None, None, tr, C), lambda r, core_ref: (l, core_ref[0], r, 0)),
        ),
        out_shape=jax.ShapeDtypeStruct((n_layers, 2, R, C), F32),
        input_output_aliases=aliases,
        compiler_params=pltpu.CompilerParams(dimension_semantics=("arbitrary",), vmem_limit_bytes=VMEM_LIMIT),
    )(*args)


def _sum_blocks(a, n):
    M = a.shape[0] // n
    C = a.shape[1]

    def body(a_ref, o_ref):
        s = a_ref[0:M]
        for j in range(1, n):
            s = s + a_ref[j * M:(j + 1) * M]
        o_ref[...] = s

    return pl.pallas_call(
        body,
        name="sum_blocks",
        out_shape=jax.ShapeDtypeStruct((M, C), F32),
        compiler_params=pltpu.CompilerParams(vmem_limit_bytes=VMEM_LIMIT),
    )(a)


def _adamw(w, g, m, v, emit_grad=False):
    R, C = w.shape
    n_out = 4 if emit_grad else 3
    tr = _row_tile(R, C, 4 + n_out) if R % 8 == 0 else R

    def body(w_ref, g_ref, m_ref, v_ref, d_ref, nm_ref, nv_ref, *g_out):
        g = g_ref[...]
        m = ADAM_B1 * m_ref[...] + (1.0 - ADAM_B1) * g
        v = ADAM_B2 * v_ref[...] + (1.0 - ADAM_B2) * (g * g)
        m_hat = m / (1.0 - ADAM_B1 ** ADAM_STEP)
        v_hat = v / (1.0 - ADAM_B2 ** ADAM_STEP)
        d_ref[...] = -ADAM_LR * (m_hat / (jnp.sqrt(v_hat) + ADAM_EPS) + ADAM_WD * w_ref[...])
        nm_ref[...] = m
        nv_ref[...] = v
        if emit_grad:
            g_out[0][...] = g

    spec = pl.BlockSpec((tr, C), lambda i: (i, 0))
    return _call(body, "adamw", (R // tr,), [spec] * 4, [spec] * n_out, [jax.ShapeDtypeStruct((R, C), F32)] * n_out,
                 [], (w, g, m, v))[0]


def kernel(x, c, ada_w, ada_b, norm_ffn1_g, ffn1_w_gu, ffn1_w_down, norm_mix_g, mix_w_in, sgu_ln_g, sgu_ln_b, sgu_w_s, sgu_b, conv_w, out_norm_g, mix_w_out, norm_ffn2_g, ffn2_w_gu, ffn2_w_down, final_norm_g, loss_target, m_ada_w, m_ada_b, m_norm_ffn1_g, m_ffn1_w_gu, m_ffn1_w_down, m_norm_mix_g, m_mix_w_in, m_sgu_ln_g, m_sgu_ln_b, m_sgu_w_s, m_sgu_b, m_conv_w, m_out_norm_g, m_mix_w_out, m_norm_ffn2_g, m_ffn2_w_gu, m_ffn2_w_down, m_final_norm_g, v_ada_w, v_ada_b, v_norm_ffn1_g, v_ffn1_w_gu, v_ffn1_w_down, v_norm_mix_g, v_mix_w_in, v_sgu_ln_g, v_sgu_ln_b, v_sgu_w_s, v_sgu_b, v_conv_w, v_out_norm_g, v_mix_w_out, v_norm_ffn2_g, v_ffn2_w_gu, v_ffn2_w_down, v_final_norm_g):
    weights = dict(ada_w=ada_w, ada_b=ada_b, norm_ffn1_g=norm_ffn1_g, ffn1_w_gu=ffn1_w_gu, ffn1_w_down=ffn1_w_down,
                   norm_mix_g=norm_mix_g, mix_w_in=mix_w_in, sgu_ln_g=sgu_ln_g, sgu_ln_b=sgu_ln_b, sgu_w_s=sgu_w_s,
                   sgu_b=sgu_b, conv_w=conv_w, out_norm_g=out_norm_g, mix_w_out=mix_w_out, norm_ffn2_g=norm_ffn2_g,
                   ffn2_w_gu=ffn2_w_gu, ffn2_w_down=ffn2_w_down, final_norm_g=final_norm_g)
    m_in = dict(ada_w=m_ada_w, ada_b=m_ada_b, norm_ffn1_g=m_norm_ffn1_g, ffn1_w_gu=m_ffn1_w_gu,
                ffn1_w_down=m_ffn1_w_down, norm_mix_g=m_norm_mix_g, mix_w_in=m_mix_w_in, sgu_ln_g=m_sgu_ln_g,
                sgu_ln_b=m_sgu_ln_b, sgu_w_s=m_sgu_w_s, sgu_b=m_sgu_b, conv_w=m_conv_w, out_norm_g=m_out_norm_g,
                mix_w_out=m_mix_w_out, norm_ffn2_g=m_norm_ffn2_g, ffn2_w_gu=m_ffn2_w_gu, ffn2_w_down=m_ffn2_w_down,
                final_norm_g=m_final_norm_g)
    v_in = dict(ada_w=v_ada_w, ada_b=v_ada_b, norm_ffn1_g=v_norm_ffn1_g, ffn1_w_gu=v_ffn1_w_gu,
                ffn1_w_down=v_ffn1_w_down, norm_mix_g=v_norm_mix_g, mix_w_in=v_mix_w_in, sgu_ln_g=v_sgu_ln_g,
                sgu_ln_b=v_sgu_ln_b, sgu_w_s=v_sgu_w_s, sgu_b=v_sgu_b, conv_w=v_conv_w, out_norm_g=v_out_norm_g,
                mix_w_out=v_mix_w_out, norm_ffn2_g=v_norm_ffn2_g, ffn2_w_gu=v_ffn2_w_gu, ffn2_w_down=v_ffn2_w_down,
                final_norm_g=v_final_norm_g)

    B, S, D = x.shape
    T = B * S
    L = ada_w.shape[0]
    F = ffn1_w_down.shape[1] * N_CHIP
    P = mix_w_in.shape[2] * N_CHIP
    DA = D // 2
    DB = D - DA
    HD = DA // N_HEADS
    SA = ada_w.shape[2]
    n_all = B * N_DEV
    mx, my, mc = _position()
    chip = 2 * mx + my
    dev = 2 * chip + mc
    core = jnp.reshape(mc, (1,)).astype(jnp.int32)

    big = ["ffn1_w_gu", "ffn1_w_down", "mix_w_in", "mix_w_out", "ffn2_w_gu", "ffn2_w_down"]
    col_sharded = dict(ffn1_w_gu=True, ffn1_w_down=False, mix_w_in=True, mix_w_out=False,
                       ffn2_w_gu=True, ffn2_w_down=False)
    shards = {k: weights[k].astype(BF16) for k in big}
    gather = lambda l, *names: _gather_comm([(shards[k], l, col_sharded[k]) for k in names])
    full = [dict() for _ in range(L)]

    def arrived(l, names, res):
        full[l].update(zip(names, res))

    n_cw = L * conv_w.shape[1]
    cw_block = jnp.pad(conv_w.reshape(n_cw, conv_w.shape[2]), ((0, 8 - n_cw), (0, 0)))
    c_all, cw_all = _comm_call(_merge(_all_gather_comm(c.reshape(8, B * D // 8)), _all_gather_comm(cw_block)),
                               "gather_c")
    c_all = c_all.reshape(n_all, D)
    cw_all = cw_all.reshape(N_CHIP, 2, 8, conv_w.shape[2])[:, 0, :n_cw]
    conv_full = jnp.transpose(cw_all.reshape(N_CHIP, L, conv_w.shape[1], conv_w.shape[2]), (1, 2, 0, 3))
    conv_full = conv_full.reshape(L, conv_w.shape[1], DB)
    ada_b_mine = lax.dynamic_slice_in_dim(ada_b, chip * SA, SA, axis=1).reshape(L, 1, SA)
    c_act, ada_part = _ada_fwd(c_all, ada_w, ada_b_mine)
    ada_all, first_w = _comm_call(_merge(_all_gather_comm(ada_part.reshape(L * n_all, SA)), gather(0, big[0])),
                                  "gather_first")
    arrived(0, big[:1], [first_w])
    ada_all = ada_all.reshape(N_CHIP, 2, L, n_all, SA)[:, 0]
    ada_all = jnp.transpose(ada_all, (1, 2, 0, 3)).reshape(L, n_all, N_CHIP * SA)
    ada = lax.dynamic_slice_in_dim(ada_all, dev * B, B, axis=1).reshape(L, B, N_MOD, 1, D)
    mods = [[ada[l, :, j] for j in range(N_MOD)] for l in range(L)]

    x0 = x.reshape(T, D)
    gains = lambda name, l: weights[name][l].reshape(1, D)
    hmask = jnp.repeat(jnp.eye(N_HEADS, dtype=F32), HD, axis=0)
    pmat = (jnp.repeat(hmask, HD, axis=1) / HD).astype(BF16)

    def mix_consts(l):
        lng = jnp.tile(sgu_ln_g[l], N_HEADS).reshape(1, DA)
        lnb = jnp.tile(sgu_ln_b[l], N_HEADS).reshape(1, DA)
        wst = sgu_w_s[l].reshape(N_HEADS * CHUNK, CHUNK)
        wstt = jnp.swapaxes(sgu_w_s[l], 1, 2).reshape(N_HEADS * CHUNK, CHUNK)
        bias = jnp.repeat(jnp.transpose(sgu_b[l]), HD, axis=1)
        return lng, lnb, wst, wstt, bias

    def fetch(fn, *args, bring=(), **kw):
        bring = [(l, k) for l, k in bring if l < L]
        comm = _gather_comm([(shards[k], l, col_sharded[k]) for l, k in bring]) if bring else None
        res, got = fn(*args, comm, **kw)
        for (l, k), a in zip(bring, got):
            full[l][k] = a
        return res

    saved = []
    xc = x0
    for l in range(L):
        sh1, sc1, g1, sh2, sc2, g2, sh3, sc3, g3 = mods[l]
        lng, lnb, wst, wstt, bias = mix_consts(l)
        w = full[l]
        if l == 0:
            gu1, a1 = fetch(_ffn_up, xc, gains("norm_ffn1_g", l), sh1, sc1, w["ffn1_w_gu"],
                            bring=[(l, "ffn1_w_down"), (l, "mix_w_in"), (l, "mix_w_out")])
            xa, f1 = fetch(_ffn_down, a1, xc, g1, w["ffn1_w_down"], bring=[(l, "ffn2_w_down")])
        else:
            xa, gu1, a1, f1 = fetch(_ffn_fwd, xc, gains("norm_ffn1_g", l), sh1, sc1, g1, w["ffn1_w_gu"],
                                    w["ffn1_w_down"], bring=[(l, "ffn2_w_gu"), (l, "mix_w_in")])
        xb, proj, h2, yn, sv = fetch(_mix_fwd, xa, gains("norm_mix_g", l), sh2, sc2, g2, w["mix_w_in"], w["mix_w_out"],
                                     lng, lnb, wst, bias, pmat, conv_full[l], gains("out_norm_g", l),
                                     bring=[(l, "ffn2_w_gu")] if l == 0 else [(l, "ffn2_w_down")])
        if l + 1 < L:
            xd, gu2, a2, f2 = fetch(_ffn_fwd, xb, gains("norm_ffn2_g", l), sh3, sc3, g3, w["ffn2_w_gu"],
                                    w["ffn2_w_down"],
                                    bring=[(l + 1, "ffn1_w_gu"), (l + 1, "mix_w_out"), (l + 1, "ffn1_w_down")])
        else:
            dx, gu2, a2, f2, loss_block, d_final = fetch(
                _ffn_fwd, xb, gains("norm_ffn2_g", l), sh3, sc3, g3, w["ffn2_w_gu"], w["ffn2_w_down"],
                head=(loss_target.reshape(T, D), final_norm_g.reshape(1, D)))
            xd = None
        saved.append(dict(x0=xc, xa=xa, xb=xb, gu1=gu1, a1=a1, f1=f1, proj=proj, h2=h2, yn=yn, sv=sv,
                          gu2=gu2, a2=a2, f2=f2))
        xc = xd

    reduced = dict.fromkeys(big)

    def halves(name, g):
        if g.ndim == 4:
            return g
        return g.reshape(N_CHIP, 2, weights[name].shape[1] // 2, g.shape[-1])

    class Reduction:
        def __init__(self, l, name, g):
            self.l, self.name, self.g, self.stage = l, name, halves(name, g), 0
            self.ici_bytes = 3 * (g.size // 8) * 2

        def step(self):
            self.stage += 1
            if self.stage == 1:
                return _sibling_half_comm([self.g])
            if self.stage == 2:
                return _scatter_comm([_pair_sum(self.g, self.got[0], core)])
            if self.stage == 3:
                reduced[self.name] = _chip_sum(self.got[0], core, self.l, L, reduced[self.name])
                return _share_comm([reduced[self.name]], self.l)
            reduced[self.name] = self.got[0]
            return None

    active, extra, gathered = [], [], {}

    def carry(fn, *args, us=None):
        left = None if us is None else us * SCATTER_BYTES_PER_US
        riders = []
        for r in active:
            if r.stage == 1 and left is not None:
                if r.ici_bytes > left * SCATTER_OVERSHOOT:
                    continue
                left -= r.ici_bytes
            riders.append(r)
        comms = [r.step() for r in riders] + [cm for cm, _ in extra]
        takers = [functools.partial(setattr, r, "got") for r in riders] + [cb for _, cb in extra]
        extra.clear()
        if fn is None:
            res, got = None, (_comm_call(_merge(*comms), "reduce_alone") if comms else [])
        else:
            res, got = fn(*args, comm=_merge(*comms))
        at = 0
        for cm, take in zip(comms, takers):
            take(got[at:at + len(cm.out_shape)])
            at += len(cm.out_shape)
        for r in riders:
            if r.stage == 3:
                r.step()
                active.remove(r)
        return res

    def reduce_later(l, name, g):
        active.append(Reduction(l, name, g))

    small = [None] * L
    dwsts = [None] * L
    d_ada = [None] * L
    for l in reversed(range(L)):
        sh1, sc1, g1, sh2, sc2, g2, sh3, sc3, g3 = mods[l]
        lng, lnb, wst, wstt, bias = mix_consts(l)
        s = saved[l]
        w = full[l]
        last = l == 0
        dx, dgu2, h3, df2, dsc3, dsh3, dgain3, dg3 = carry(
            _ffn_bwd, dx, s["xb"], s["gu2"], s["f2"], gains("norm_ffn2_g", l), sh3, sc3, g3, w["ffn2_w_gu"],
            w["ffn2_w_down"], us=170)
        ffn2_grads = [
            lambda: reduce_later(l, "ffn2_w_gu", carry(_wgrad, h3, dgu2, D, 2 * F // N_CHIP, True, "wgrad_gu",
                                                       WGRAD_TOKENS // 2, us=110)[0]),
            lambda: reduce_later(l, "ffn2_w_down", carry(_wgrad, s["a2"], df2[None], F // 2, D, False, "wgrad_down",
                                                         us=50)[0])]
        if not last:
            ffn2_grads[0]()
            ffn2_grads[1]()
        dproj, d_o, dg2, dog, dwst, dbias, dlng, dlnb, dconvw = carry(
            _mix_core_bwd, s["proj"], s["sv"], dx, g2, w["mix_w_out"], lng, lnb, wstt, pmat, conv_full[l],
            gains("out_norm_g", l), us=150)
        mix_grads = [
            lambda: reduce_later(l, "mix_w_out", carry(_wgrad, s["yn"], d_o[None], D, D, False, "wgrad_out", us=30)[0]),
            lambda: reduce_later(l, "mix_w_in", carry(_wgrad, s["h2"], dproj[None], D, P // N_CHIP, True, "wgrad_in",
                                                      us=65)[0])]
        if not last:
            mix_grads[0]()
        dx, dsc2, dsh2, dgain2 = carry(_mixin_bwd, dx, s["xa"], dproj, gains("norm_mix_g", l), sc2, w["mix_w_in"], us=60)
        if not last:
            mix_grads[1]()
        dx, dgu, h1, df, dsc1, dsh1, dgain1, dg1 = carry(
            _ffn_bwd, dx, s["x0"], s["gu1"], s["f1"], gains("norm_ffn1_g", l), sh1, sc1, g1, w["ffn1_w_gu"],
            w["ffn1_w_down"], us=170)
        d_ada[l] = jnp.concatenate([dsh1, dsc1, dg1, dsh2, dsc2, dg2, dsh3, dsc3, dg3], axis=1).reshape(B, N_MOD * D)
        small[l] = [dgain1, dgain2, dgain3, dog, dlng, dlnb, dbias[:, ::HD], dconvw]
        dwsts[l] = dwst
        if last:
            flat = [a.reshape(-1, 128) for ll in range(L) for a in small[ll]]
            flat += [d_final.reshape(-1, 128), loss_block[0:1]]
            pad = (-sum(a.shape[0] for a in flat)) % 8
            packed = jnp.concatenate(flat + [jnp.zeros((pad, 128), F32)], axis=0)
            extra.append((_all_gather_comm(jnp.stack(d_ada).reshape(L * B, N_MOD * D)),
                          lambda got: gathered.update(d_ada=got[0])))
            extra.append((_all_gather_comm(packed), lambda got: gathered.update(small=got[0])))
            for ll in range(L):
                extra.append((_all_gather_comm(dwsts[ll]), lambda got, ll=ll: gathered.update({("dwst", ll): got[0]})))
        reduce_later(l, "ffn1_w_gu", carry(_wgrad, h1, dgu, D, 2 * F // N_CHIP, True, "wgrad_gu", WGRAD_TOKENS // 2,
                                           us=110)[0])
        reduce_later(l, "ffn1_w_down", carry(_wgrad, s["a1"], df[None], F // 2, D, False, "wgrad_down", us=50)[0])
        if last:
            ffn2_grads[0]()
            ffn2_grads[1]()
            mix_grads[1]()
            mix_grads[0]()
    grad_x = dx.reshape(B, S, D)

    def finished(name):
        while any(r.name == name for r in active):
            carry(None)
        return reduced[name].reshape(weights[name].shape)

    grads = {}
    d_ada_all = jnp.transpose(gathered["d_ada"].reshape(N_DEV, L, B, N_MOD * D), (1, 0, 2, 3))
    d_ada_all = d_ada_all.reshape(L, n_all, N_MOD * D)
    grads["ada_b"] = _colsum(d_ada_all).reshape(L, N_MOD * D)
    d_ada_mine = lax.dynamic_slice_in_dim(d_ada_all, chip * SA, SA, axis=2).astype(BF16)
    grads["ada_w"] = _ada_bwd(c_act, d_ada_mine)[0][0]

    total = _sum_blocks(gathered["small"].reshape(-1, 128), N_DEV)
    pieces, at = [], 0
    for a in flat:
        pieces.append(total[at:at + a.shape[0]])
        at += a.shape[0]
    per_layer = len(small[0])
    stack = lambda j, shape: jnp.stack([pieces[l * per_layer + j].reshape(shape) for l in range(L)])
    grads["norm_ffn1_g"] = stack(0, (D,))
    grads["norm_mix_g"] = stack(1, (D,))
    grads["norm_ffn2_g"] = stack(2, (D,))
    grads["out_norm_g"] = stack(3, (D,))
    grads["sgu_ln_g"] = stack(4, (N_HEADS, HD)).sum(axis=1)
    grads["sgu_ln_b"] = stack(5, (N_HEADS, HD)).sum(axis=1)
    grads["sgu_b"] = jnp.swapaxes(stack(6, (CHUNK, N_HEADS)), 1, 2)
    g_conv = stack(7, (8, DB))[:, :conv_w.shape[1]]
    grads["conv_w"] = lax.dynamic_slice_in_dim(g_conv, chip * conv_w.shape[2], conv_w.shape[2], axis=2)
    grads["final_norm_g"] = pieces[-2].reshape(D)
    loss = pieces[-1][0, 0]
    grads["sgu_w_s"] = jnp.stack([_sum_blocks(gathered["dwst", l].reshape(-1, CHUNK), N_DEV) for l in range(L)])
    grads["sgu_w_s"] = grads["sgu_w_s"].reshape(L, N_HEADS, CHUNK, CHUNK)

    names = list(weights)
    delta, new_m, new_v = {}, {}, {}
    for k in big:
        grads[k] = finished(k)
    for k in names:
        wk = weights[k]
        view = (1, wk.shape[0]) if wk.ndim == 1 else (-1, wk.shape[-1])
        d, nm, nv, *g_again = _adamw(wk.reshape(view), grads[k].reshape(view), m_in[k].reshape(view),
                                     v_in[k].reshape(view), emit_grad=k in big)
        delta[k], new_m[k], new_v[k] = d.reshape(wk.shape), nm.reshape(wk.shape), nv.reshape(wk.shape)
        if g_again:
            grads[k] = g_again[0].reshape(wk.shape)

    return (loss, grad_x, *[grads[k] for k in names], *[delta[k] for k in names],
            *[new_m[k] for k in names], *[new_v[k] for k in names])
```

```python
import functools
import math

import jax
import jax.numpy as jnp
from jax import lax
from jax.experimental import pallas as pl
from jax.experimental.pallas import tpu as pltpu

F32 = jnp.float32
BF16 = jnp.bfloat16
MESH = pl.DeviceIdType.MESH

N_HEADS = 8
CHUNK = 128
N_MOD = 9
EPS = 1e-6
N_DEV = 8
N_CHIP = 4

ADAM_LR = 0.001
ADAM_B1 = 0.9
ADAM_B2 = 0.999
ADAM_EPS = 1e-08
ADAM_WD = 0.01
ADAM_STEP = 10

TOKEN_TILE = 512
BWD_TILE = 256
FWD_TILE = 512
FF_SLAB = 768
MIX_TILE = 512
MIX_BWD_TILE = 512
WGRAD_TOKENS = 2048
VMEM_LIMIT = 56 * 1024 * 1024

SCATTER_BYTES_PER_US = 68_000
SCATTER_OVERSHOOT = 1.25

ANY = pl.BlockSpec(memory_space=pl.ANY)


def _tile(pref, n):
    t = min(pref, n)
    assert n % t == 0, (pref, n)
    return t


def _slabs(n, width):
    return [slice(c0, min(c0 + width, n)) for c0 in range(0, n, width)]


def _dot(a, b):
    return jnp.dot(a, b, preferred_element_type=F32)


def _dot_nt(a, b):
    return lax.dot_general(a, b, (((1,), (1,)), ((), ())), preferred_element_type=F32)


def _dot_tn(a, b):
    return lax.dot_general(a, b, (((0,), (0,)), ((), ())), preferred_element_type=F32)


def _sigmoid(x):
    return 1.0 / (1.0 + jnp.exp(-x))


def _sigmoid_fast(x):
    return pl.reciprocal(1.0 + jnp.exp(-x), approx=True)


def _rms(x):
    r = lax.rsqrt(jnp.mean(x * x, axis=-1, keepdims=True) + EPS)
    return x * r, r


def _norm_mod_bwd(x, dh, gain, sc):
    xh, r = _rms(x)
    dsc = jnp.sum(dh * (xh * gain), axis=0, keepdims=True)
    dsh = jnp.sum(dh, axis=0, keepdims=True)
    dn = dh * (1.0 + sc)
    dgain = jnp.sum(dn * xh, axis=0, keepdims=True)
    dy = dn * gain
    dx = r * (dy - xh * jnp.mean(dy * xh, axis=-1, keepdims=True))
    return dx, dsc, dsh, dgain


def _acc(ref, first, val):
    @pl.when(first)
    def _():
        ref[...] = val

    @pl.when(jnp.logical_not(first))
    def _():
        ref[...] += val


class _Comm:
    def __init__(self, args, out_shape, scratch, phases, aliases=None):
        self.args, self.out_shape, self.scratch = list(args), list(out_shape), list(scratch)
        self.phases, self.aliases = phases, dict(aliases or {})


def _merge(*comms):
    comms = [c for c in comms if c is not None]
    if len(comms) <= 1:
        return comms[0] if comms else None
    args = [a for c in comms for a in c.args]
    out_shape = [o for c in comms for o in c.out_shape]
    scratch = [s for c in comms for s in c.scratch]
    aliases, ai, oi = {}, 0, 0
    for c in comms:
        aliases.update({ai + i: oi + o for i, o in c.aliases.items()})
        ai += len(c.args)
        oi += len(c.out_shape)

    def phases(ins, outs, sems):
        parts, ai, oi, si = [], 0, 0, 0
        for c in comms:
            parts.append(c.phases(ins[ai:ai + len(c.args)], outs[oi:oi + len(c.out_shape)], sems[si:si + len(c.scratch)]))
            ai, oi, si = ai + len(c.args), oi + len(c.out_shape), si + len(c.scratch)

        def run(k):
            def go():
                for p in parts:
                    if p[k] is not None:
                        p[k]()
            return go
        return run(0), run(1), run(2)

    return _Comm(args, out_shape, scratch, phases, aliases)


def _call(body, name, grid, in_specs, out_specs, out_shape, scratch, args, comm=None):
    n_in, n_out, n_scr = len(in_specs), len(out_specs), len(scratch)
    sem = ("arbitrary",) * len(grid)
    params = pltpu.CompilerParams(dimension_semantics=sem, vmem_limit_bytes=VMEM_LIMIT)
    if comm is None:
        res = pl.pallas_call(body, name=name, grid=grid, in_specs=in_specs, out_specs=out_specs, out_shape=out_shape,
                             scratch_shapes=scratch, compiler_params=params)(*args)
        return list(res), []
    m_in, m_out = len(comm.args), len(comm.out_shape)

    def full(*refs):
        c_in, c_min = refs[:n_in], refs[n_in:n_in + m_in]
        o = n_in + m_in
        c_out, c_mout = refs[o:o + n_out], refs[o + n_out:o + n_out + m_out]
        o += n_out + m_out
        c_scr, c_sem = refs[o:o + n_scr], refs[o + n_scr:]
        start, mid, finish = comm.phases(c_min, c_mout, c_sem)
        ids = [pl.program_id(a) for a in range(len(grid))]
        first = functools.reduce(jnp.logical_and, [i == 0 for i in ids])
        last = functools.reduce(jnp.logical_and, [i == g - 1 for i, g in zip(ids, grid)])
        pl.when(first)(start)
        if mid is not None:
            pl.when(last)(mid)
        body(*c_in, *c_out, *c_scr)
        pl.when(last)(finish)

    res = pl.pallas_call(
        full, name=name, grid=grid,
        in_specs=list(in_specs) + [ANY] * m_in,
        out_specs=list(out_specs) + [ANY] * m_out,
        out_shape=list(out_shape) + comm.out_shape,
        scratch_shapes=list(scratch) + comm.scratch,
        input_output_aliases={n_in + i: n_out + o for i, o in comm.aliases.items()},
        compiler_params=params,
    )(*args, *comm.args)
    return list(res[:n_out]), list(res[n_out:])


def _comm_call(comm, name):
    m_in, m_out = len(comm.args), len(comm.out_shape)

    def body(*refs):
        start, mid, finish = comm.phases(refs[:m_in], refs[m_in:m_in + m_out], refs[m_in + m_out:])
        start()
        if mid is not None:
            mid()
        finish()

    res = pl.pallas_call(
        body, name=name, in_specs=[ANY] * m_in, out_specs=[ANY] * m_out, out_shape=comm.out_shape,
        scratch_shapes=comm.scratch, input_output_aliases=comm.aliases,
    )(*comm.args)
    return list(res)


def _position():
    return lax.axis_index("x"), lax.axis_index("y"), lax.axis_index("c")


def _gather_comm(items):
    n = len(items)
    half = [s.shape[1] // 2 for s, _, _ in items]

    def full_shape(i):
        s, _, col = items[i]
        _, R, C = s.shape
        return jax.ShapeDtypeStruct((R, N_CHIP * C) if col else (N_CHIP * R, C), s.dtype)

    def phases(ins, outs, sems):
        send_sems, recv_sems, local_sems = sems
        x, y, c = _position()

        def region(i, chip, h):
            s, _, col = items[i]
            _, R, C = s.shape
            if col:
                return outs[i].at[pl.ds(h * half[i], half[i]), pl.ds(chip * C, C)]
            return outs[i].at[pl.ds(chip * R + h * half[i], half[i]), :]

        def mine(i, h):
            return ins[i].at[items[i][1], pl.ds(h * half[i], half[i]), :]

        def copies(kx, ky, kc):
            k_me = 2 * kx + ky
            sibling = (kx, ky, 1 - kc)
            chips = [(1 - kx, ky), (kx, 1 - ky), (1 - kx, 1 - ky)]
            local, first, passed, arrive_ici, arrive_d2d = [], [], [], [], []

            def remote(src, dst, s, to):
                return pltpu.make_async_remote_copy(src_ref=src, dst_ref=dst, send_sem=send_sems.at[s],
                                                    recv_sem=recv_sems.at[s], device_id=to, device_id_type=MESH)

            for i in range(n):
                for h in range(2):
                    local.append(pltpu.make_async_copy(mine(i, h), region(i, k_me, h), local_sems.at[2 * i + h]))
                for j, (px, py) in enumerate(chips):
                    s = 6 * i + j
                    first.append(remote(mine(i, kc), region(i, k_me, kc), s, (px, py, kc)))
                    got = region(i, 2 * px + py, kc)
                    arrive_ici.append(remote(got, got, s, (px, py, kc)))
                    passed.append(remote(got, got, s + 3, sibling))
                    other = region(i, 2 * px + py, 1 - kc)
                    arrive_d2d.append(remote(other, other, s + 3, sibling))
            return local, first, passed, arrive_ici, arrive_d2d

        def on_each_device(fn):
            def go():
                for kx in range(2):
                    for ky in range(2):
                        for kc in range(2):
                            pl.when((x == kx) & (y == ky) & (c == kc))(functools.partial(fn, *copies(kx, ky, kc)))
            return go

        def start(local, first, passed, arrive_ici, arrive_d2d):
            for cp in local + first:
                cp.start()

        def mid(local, first, passed, arrive_ici, arrive_d2d):
            for a, p in zip(arrive_ici, passed):
                a.wait_recv()
                p.start()

        def finish(local, first, passed, arrive_ici, arrive_d2d):
            for a in arrive_d2d:
                a.wait_recv()
            for cp in first + passed:
                cp.wait_send()
            for cp in local:
                cp.wait()

        return on_each_device(start), on_each_device(mid), on_each_device(finish)

    scratch = [pltpu.SemaphoreType.DMA((6 * n,)), pltpu.SemaphoreType.DMA((6 * n,)), pltpu.SemaphoreType.DMA((2 * n,))]
    return _Comm([s for s, _, _ in items], [full_shape(i) for i in range(n)], scratch, phases)


def _sibling_half_comm(gs):
    n = len(gs)

    def phases(ins, outs, sems):
        send_sems, recv_sems = sems
        x, y, c = _position()

        def copies():
            return [pltpu.make_async_remote_copy(
                src_ref=ins[i].at[:, 1 - c], dst_ref=outs[i], send_sem=send_sems.at[i], recv_sem=recv_sems.at[i],
                device_id=(x, y, 1 - c), device_id_type=MESH) for i in range(n)]

        def start():
            for cp in copies():
                cp.start()

        def finish():
            for cp in copies():
                cp.wait()

        return start, None, finish

    out_shape = [jax.ShapeDtypeStruct(g.shape[:1] + g.shape[2:], g.dtype) for g in gs]
    return _Comm(gs, out_shape, [pltpu.SemaphoreType.DMA((n,)), pltpu.SemaphoreType.DMA((n,))], phases)


def _scatter_comm(ps):
    n = len(ps)

    def phases(ins, outs, sems):
        send_sems, recv_sems, local_sems = sems
        x, y, c = _position()
        k_me = 2 * x + y
        chips = [(1 - x, y), (x, 1 - y), (1 - x, 1 - y)]

        def copies():
            local = [pltpu.make_async_copy(ins[i].at[k_me], outs[i].at[k_me], local_sems.at[i]) for i in range(n)]
            remote = [pltpu.make_async_remote_copy(
                src_ref=ins[i].at[2 * px + py], dst_ref=outs[i].at[k_me],
                send_sem=send_sems.at[3 * i + j], recv_sem=recv_sems.at[3 * i + j],
                device_id=(px, py, c), device_id_type=MESH) for i in range(n) for j, (px, py) in enumerate(chips)]
            return local, remote

        def start():
            local, remote = copies()
            for cp in local + remote:
                cp.start()

        def finish():
            local, remote = copies()
            for cp in remote + local:
                cp.wait()

        return start, None, finish

    scratch = [pltpu.SemaphoreType.DMA((3 * n,)), pltpu.SemaphoreType.DMA((3 * n,)), pltpu.SemaphoreType.DMA((n,))]
    return _Comm(ps, [jax.ShapeDtypeStruct(p.shape, p.dtype) for p in ps], scratch, phases)


def _share_comm(rs, l):
    n = len(rs)

    def phases(ins, outs, sems):
        send_sems, recv_sems = sems
        x, y, c = _position()

        def copy(i, h):
            return pltpu.make_async_remote_copy(
                src_ref=outs[i].at[l, h], dst_ref=outs[i].at[l, h], send_sem=send_sems.at[i], recv_sem=recv_sems.at[i],
                device_id=(x, y, 1 - c), device_id_type=MESH)

        def start():
            for i in range(n):
                copy(i, c).start()

        def finish():
            for i in range(n):
                copy(i, 1 - c).wait_recv()
            for i in range(n):
                copy(i, c).wait_send()

        return start, None, finish

    return _Comm(rs, [jax.ShapeDtypeStruct(r.shape, r.dtype) for r in rs],
                 [pltpu.SemaphoreType.DMA((n,)), pltpu.SemaphoreType.DMA((n,))], phases,
                 aliases={i: i for i in range(n)})


def _all_gather_comm(block):
    def phases(ins, outs, sems):
        send_sems, recv_sems, local_sem = sems
        (src,), (out,) = ins, outs
        x, y, c = _position()
        sibling = (x, y, 1 - c)
        chips = [(1 - x, y), (x, 1 - y), (1 - x, 1 - y)]

        def slot(px, py, pc):
            return out.at[4 * px + 2 * py + pc]

        def copy(k, blk, to, own=False):
            return pltpu.make_async_remote_copy(
                src_ref=src if own else slot(*blk), dst_ref=slot(*blk),
                send_sem=send_sems.at[k], recv_sem=recv_sems.at[k], device_id=to, device_id_type=MESH)

        mine = lambda: pltpu.make_async_copy(src, slot(x, y, c), local_sem.at[0])
        first = lambda: [copy(0, (x, y, c), sibling, True)] + [
            copy(1 + j, (x, y, c), (*chip, c), True) for j, chip in enumerate(chips)]
        passed = lambda: [copy(4 + j, (*chip, c), sibling) for j, chip in enumerate(chips)]

        def start():
            mine().start()
            for cp in first():
                cp.start()

        def mid():
            for j, (chip, p) in enumerate(zip(chips, passed())):
                copy(1 + j, (*chip, c), (x, y, c)).wait_recv()
                p.start()

        def finish():
            copy(0, sibling, (x, y, c)).wait_recv()
            for j, chip in enumerate(chips):
                copy(4 + j, (*chip, 1 - c), (x, y, c)).wait_recv()
            for cp in first() + passed():
                cp.wait_send()
            mine().wait()

        return start, mid, finish

    scratch = [pltpu.SemaphoreType.DMA((7,)), pltpu.SemaphoreType.DMA((7,)), pltpu.SemaphoreType.DMA((1,))]
    return _Comm([block], [jax.ShapeDtypeStruct((N_DEV,) + block.shape, block.dtype)], scratch, phases)


def _ffn_up(x, gain, sh, sc, wgu, comm=None):
    T, D = x.shape
    F = wgu.shape[1] // 2
    B = sh.shape[0]
    tm = _tile(TOKEN_TILE, T // B)
    tps = (T // B) // tm
    slabs = _slabs(F, FF_SLAB)

    def body(x_ref, gain_ref, sh_ref, sc_ref, w_ref, gu_ref, a_ref):
        xh, _ = _rms(x_ref[...])
        h = (xh * gain_ref[...] * (1.0 + sc_ref[0]) + sh_ref[0]).astype(BF16)

        def dots(s):
            return _dot(h, w_ref[:, s]), _dot(h, w_ref[:, slice(F + s.start, F + s.stop)])

        nxt = dots(slabs[0])
        for j, s in enumerate(slabs):
            g, u = nxt
            if j + 1 < len(slabs):
                nxt = dots(slabs[j + 1])
            gu_ref[0, :, s] = g.astype(BF16)
            gu_ref[1, :, s] = u.astype(BF16)
            a_ref[:, s] = (g * _sigmoid(g) * u).astype(BF16)

    seq = lambda i: (i // tps, 0, 0)
    return _call(
        body, "ffn_up", (T // tm,),
        [
            pl.BlockSpec((tm, D), lambda i: (i, 0)),
            pl.BlockSpec((1, D), lambda i: (0, 0)),
            pl.BlockSpec((1, 1, D), seq),
            pl.BlockSpec((1, 1, D), seq),
            pl.BlockSpec((D, 2 * F), lambda i: (0, 0), pipeline_mode=pl.Buffered(1)),
        ],
        [
            pl.BlockSpec((2, tm, F), lambda i: (0, i, 0)),
            pl.BlockSpec((tm, F), lambda i: (i, 0)),
        ],
        [
            jax.ShapeDtypeStruct((2, T, F), BF16),
            jax.ShapeDtypeStruct((T, F), BF16),
        ],
        [],
        (x, gain, sh, sc, wgu), comm)


def _ffn_down(a, x, gate, wd, comm=None):
    T, F = a.shape
    D = x.shape[1]
    B = gate.shape[0]
    tm = _tile(2 * TOKEN_TILE, T // B)
    tps = (T // B) // tm

    def body(a_ref, x_ref, gate_ref, wd_ref, xo_ref, f_ref):
        f = _dot(a_ref[...], wd_ref[...])
        f_ref[...] = f.astype(BF16)
        xo_ref[...] = x_ref[...] + 0.5 * gate_ref[0] * f

    return _call(
        body, "ffn_down", (T // tm,),
        [
            pl.BlockSpec((tm, F), lambda i: (i, 0)),
            pl.BlockSpec((tm, D), lambda i: (i, 0)),
            pl.BlockSpec((1, 1, D), lambda i: (i // tps, 0, 0)),
            pl.BlockSpec((F, D), lambda i: (0, 0), pipeline_mode=pl.Buffered(1)),
        ],
        [pl.BlockSpec((tm, D), lambda i: (i, 0)), pl.BlockSpec((tm, D), lambda i: (i, 0))],
        [jax.ShapeDtypeStruct((T, D), F32), jax.ShapeDtypeStruct((T, D), BF16)],
        [],
        (a, x, gate, wd), comm)


def _ffn_fwd(x, gain, sh, sc, gate, wgu, wd, comm=None, head=None):
    T, D = x.shape
    F = wd.shape[0]
    B = sh.shape[0]
    tm = _tile(FWD_TILE, T // B)
    tps = (T // B) // tm
    slabs = _slabs(F, FF_SLAB)

    def body(x_ref, gain_ref, sh_ref, sc_ref, gate_ref, w_ref, wd_ref, *rest):
        if head is None:
            xo_ref, gu_ref, a_ref, f_ref = rest
        else:
            t_ref, fgain_ref, xo_ref, gu_ref, a_ref, f_ref, loss_ref, dfgain_ref = rest
        x = x_ref[...]
        h = (_rms(x)[0] * gain_ref[...] * (1.0 + sc_ref[0]) + sh_ref[0]).astype(BF16)

        def dots(s):
            return _dot(h, w_ref[:, s]), _dot(h, w_ref[:, slice(F + s.start, F + s.stop)])

        nxt = dots(slabs[0])
        for j, s in enumerate(slabs):
            g, u = nxt
            if j + 1 < len(slabs):
                nxt = dots(slabs[j + 1])
            gu_ref[0, :, s] = g.astype(BF16)
            gu_ref[1, :, s] = u.astype(BF16)
            a_ref[:, s] = (g * _sigmoid(g) * u).astype(BF16)
        f = _dot(a_ref[...], wd_ref[...])
        f_ref[...] = f.astype(BF16)
        xo = x + 0.5 * gate_ref[0] * f
        if head is None:
            xo_ref[...] = xo
        else:
            first = pl.program_id(0) == 0
            xh, r = _rms(xo)
            fgain = fgain_ref[...]
            err = xh * fgain - t_ref[...]
            _acc(loss_ref, first, jnp.zeros((8, 128), F32) + 0.5 * jnp.sum(err * err) / D)
            dout = err * (1.0 / D)
            _acc(dfgain_ref, first, jnp.sum(dout * xh, axis=0, keepdims=True))
            dy = dout * fgain
            xo_ref[...] = r * (dy - xh * jnp.mean(dy * xh, axis=-1, keepdims=True))

    seq = lambda i: (i // tps, 0, 0)
    row = lambda i: (i, 0)
    const = lambda i: (0, 0)
    in_specs = [
        pl.BlockSpec((tm, D), row),
        pl.BlockSpec((1, D), const),
        pl.BlockSpec((1, 1, D), seq),
        pl.BlockSpec((1, 1, D), seq),
        pl.BlockSpec((1, 1, D), seq),
        pl.BlockSpec((D, 2 * F), const, pipeline_mode=pl.Buffered(1)),
        pl.BlockSpec((F, D), const, pipeline_mode=pl.Buffered(1)),
    ]
    out_specs = [
        pl.BlockSpec((tm, D), row),
        pl.BlockSpec((2, tm, F), lambda i: (0, i, 0)),
        pl.BlockSpec((tm, F), row),
        pl.BlockSpec((tm, D), row),
    ]
    out_shape = [
        jax.ShapeDtypeStruct((T, D), F32),
        jax.ShapeDtypeStruct((2, T, F), BF16),
        jax.ShapeDtypeStruct((T, F), BF16),
        jax.ShapeDtypeStruct((T, D), BF16),
    ]
    args = (x, gain, sh, sc, gate, wgu, wd)
    if head is not None:
        in_specs += [pl.BlockSpec((tm, D), row), pl.BlockSpec((1, D), const)]
        out_specs += [pl.BlockSpec((8, 128), const), pl.BlockSpec((1, D), const)]
        out_shape += [jax.ShapeDtypeStruct((8, 128), F32), jax.ShapeDtypeStruct((1, D), F32)]
        args += tuple(head)
    return _call(body, "ffn_fwd", (T // tm,), in_specs, out_specs, out_shape, [], args, comm)


def _ffn_bwd(dxo, x, gu, f, gain, sh, sc, gate, wgu, wd, comm=None):
    T, D = x.shape
    F = wd.shape[0]
    B = sc.shape[0]
    tm = _tile(BWD_TILE, T // B)
    tps = (T // B) // tm
    slabs = _slabs(F, FF_SLAB)

    def body(dxo_ref, x_ref, gu_ref, f_ref, gain_ref, sh_ref, sc_ref, gate_ref, w_ref, wd_ref,
             dx_ref, dgu_ref, h_ref, df_ref, dsc_ref, dsh_ref, dgain_ref, dgate_ref):
        i = pl.program_id(0)
        first_of_seq = (i % tps) == 0
        gain = gain_ref[...]
        sc = sc_ref[0]
        dxo = dxo_ref[...]
        x = x_ref[...]
        df = (0.5 * gate_ref[0] * dxo).astype(BF16)
        df_ref[...] = df
        nxt = _dot_nt(df, wd_ref[slabs[0], :])
        for j, s in enumerate(slabs):
            da = nxt
            if j + 1 < len(slabs):
                nxt = _dot_nt(df, wd_ref[slabs[j + 1], :])
            g = gu_ref[0, :, s]
            sg = 1.0 / (1.0 + jnp.exp(-g))
            t = g * sg
            dab = da.astype(BF16)
            dgu_ref[1, :, s] = dab * t
            dgu_ref[0, :, s] = dab * gu_ref[1, :, s] * (sg + t - t * sg)
        dh = _dot_nt(dgu_ref[0], w_ref[:, 0:F]) + _dot_nt(dgu_ref[1], w_ref[:, F:])
        dx, dsc, dsh, dgain = _norm_mod_bwd(x, dh, gain, sc)
        dx_ref[...] = dxo + dx
        h_ref[...] = (_rms(x)[0] * gain * (1.0 + sc) + sh_ref[0]).astype(BF16)
        _acc(dsc_ref.at[0], first_of_seq, dsc)
        _acc(dsh_ref.at[0], first_of_seq, dsh)
        _acc(dgain_ref, i == 0, dgain)
        _acc(dgate_ref.at[0], first_of_seq, 0.5 * jnp.sum(dxo * f_ref[...].astype(F32), axis=0, keepdims=True))

    seq = lambda i: (i // tps, 0, 0)
    row = lambda i: (i, 0)
    return _call(
        body, "ffn_bwd", (T // tm,),
        [
            pl.BlockSpec((tm, D), row),
            pl.BlockSpec((tm, D), row),
            pl.BlockSpec((2, tm, F), lambda i: (0, i, 0)),
            pl.BlockSpec((tm, D), row),
            pl.BlockSpec((1, D), lambda i: (0, 0)),
            pl.BlockSpec((1, 1, D), seq),
            pl.BlockSpec((1, 1, D), seq),
            pl.BlockSpec((1, 1, D), seq),
            pl.BlockSpec((D, 2 * F), lambda i: (0, 0), pipeline_mode=pl.Buffered(1)),
            pl.BlockSpec((F, D), lambda i: (0, 0), pipeline_mode=pl.Buffered(1)),
        ],
        [
            pl.BlockSpec((tm, D), row),
            pl.BlockSpec((2, tm, F), lambda i: (0, i, 0)),
            pl.BlockSpec((tm, D), row),
            pl.BlockSpec((tm, D), row),
            pl.BlockSpec((1, 1, D), seq),
            pl.BlockSpec((1, 1, D), seq),
            pl.BlockSpec((1, D), lambda i: (0, 0)),
            pl.BlockSpec((1, 1, D), seq),
        ],
        [
            jax.ShapeDtypeStruct((T, D), F32),
            jax.ShapeDtypeStruct((2, T, F), BF16),
            jax.ShapeDtypeStruct((T, D), BF16),
            jax.ShapeDtypeStruct((T, D), BF16),
            jax.ShapeDtypeStruct((B, 1, D), F32),
            jax.ShapeDtypeStruct((B, 1, D), F32),
            jax.ShapeDtypeStruct((1, D), F32),
            jax.ShapeDtypeStruct((B, 1, D), F32),
        ],
        [],
        (dxo, x, gu, f, gain, sh, sc, gate, wgu, wd), comm)


def _wgrad(a, b, tmm, tn, col_major, name, tokens=WGRAD_TOKENS, comm=None):
    T, M = a.shape
    nb, _, Nb = b.shape
    N = nb * Nb
    tk = _tile(tokens, T)
    span = 2 if col_major else 1
    wide = span * tn
    npb = Nb // wide
    assert M % tmm == 0 and Nb % wide == 0
    if col_major:
        assert tmm == M
        shape = (N // tn, 2, M // 2, tn)
        out_spec = pl.BlockSpec((span, 2, M // 2, tn), lambda i, j, t: (j, 0, 0, 0))
    else:
        shape = (M // tmm, tmm, N)
        out_spec = pl.BlockSpec((None, tmm, tn), lambda i, j, t: (i, 0, j))

    def body(a_ref, b_ref, o_ref):
        @pl.when(pl.program_id(2) == 0)
        def _():
            o_ref[...] = jnp.zeros_like(o_ref)

        res = _dot_tn(a_ref[...], b_ref[...])
        if col_major:
            for s in range(span):
                for h in range(2):
                    o_ref[s, h] += res[h * (M // 2):(h + 1) * (M // 2), s * tn:(s + 1) * tn]
        else:
            o_ref[...] += res

    return _call(
        body, name, (M // tmm, N // wide, T // tk),
        [
            pl.BlockSpec((tk, tmm), lambda i, j, t: (t, i)),
            pl.BlockSpec((None, tk, wide), lambda i, j, t: (j // npb, t, j % npb)),
        ],
        [out_spec], [jax.ShapeDtypeStruct(shape, F32)], [],
        (a, b), comm)


def _mixin_bwd(dxo, x, dproj, gain, sc, win, comm=None):
    T, D = x.shape
    P = win.shape[1]
    B = sc.shape[0]
    tm = _tile(TOKEN_TILE, T // B)
    tps = (T // B) // tm

    def body(dxo_ref, x_ref, dp_ref, gain_ref, sc_ref, w_ref, dx_ref, dsc_ref, dsh_ref, dgain_ref):
        i = pl.program_id(0)
        first_of_seq = (i % tps) == 0
        halves = _slabs(tm, tm // 2)
        nxt = _dot_nt(dp_ref[halves[0], :], w_ref[...])
        sums = None
        for j, r in enumerate(halves):
            dh = nxt
            if j + 1 < len(halves):
                nxt = _dot_nt(dp_ref[halves[j + 1], :], w_ref[...])
            part = _norm_mod_bwd(x_ref[r, :], dh, gain_ref[...], sc_ref[0])
            dx_ref[r, :] = dxo_ref[r, :] + part[0]
            sums = part[1:] if sums is None else tuple(a + b for a, b in zip(sums, part[1:]))
        _acc(dsc_ref.at[0], first_of_seq, sums[0])
        _acc(dsh_ref.at[0], first_of_seq, sums[1])
        _acc(dgain_ref, i == 0, sums[2])

    seq = lambda i: (i // tps, 0, 0)
    row = lambda i: (i, 0)
    return _call(
        body, "mixin_bwd", (T // tm,),
        [
            pl.BlockSpec((tm, D), row),
            pl.BlockSpec((tm, D), row),
            pl.BlockSpec((tm, P), row),
            pl.BlockSpec((1, D), lambda i: (0, 0)),
            pl.BlockSpec((1, 1, D), seq),
            pl.BlockSpec((D, P), lambda i: (0, 0)),
        ],
        [
            pl.BlockSpec((tm, D), row),
            pl.BlockSpec((1, 1, D), seq),
            pl.BlockSpec((1, 1, D), seq),
            pl.BlockSpec((1, D), lambda i: (0, 0)),
        ],
        [
            jax.ShapeDtypeStruct((T, D), F32),
            jax.ShapeDtypeStruct((B, 1, D), F32),
            jax.ShapeDtypeStruct((B, 1, D), F32),
            jax.ShapeDtypeStruct((1, D), F32),
        ],
        [],
        (dxo, x, dproj, gain, sc, win), comm)


def _head_mean(z, pmat, exact=True):
    hi = z.astype(BF16)
    if not exact:
        return _dot(hi, pmat)
    lo = (z - hi.astype(F32)).astype(BF16)
    return _dot(hi, pmat) + _dot(lo, pmat)


def _gelu_parts(x):
    cdf = 0.5 * (1.0 + lax.erf(x * (1.0 / math.sqrt(2.0))))
    return x * cdf, cdf


def _gelu_grad(x, cdf):
    return cdf + x * jnp.exp(-0.5 * x * x) * (1.0 / math.sqrt(2.0 * math.pi))


LANES = 128


def _head_blocks(da):
    hd = da // N_HEADS
    lb = min(LANES, da)
    col = lax.broadcasted_iota(jnp.int32, (1, lb), 1)
    return lb, lb // hd, da // lb, [(col >= h * hd) & (col < (h + 1) * hd) for h in range(lb // hd)]


def _mix_heads(w_stack, v, da):
    lb, hpb, nb, masks = _head_blocks(da)
    outs = []
    for b in range(nb):
        res = _dot(w_stack[b * hpb * CHUNK:(b + 1) * hpb * CHUNK], v[:, b * lb:(b + 1) * lb])
        out = res[0:CHUNK]
        for h in range(1, hpb):
            out = jnp.where(masks[h], res[h * CHUNK:(h + 1) * CHUNK], out)
        outs.append(out)
    return outs[0] if nb == 1 else jnp.concatenate(outs, axis=1)


def _mix_heads_grad(dm, v, da):
    lb, hpb, nb, masks = _head_blocks(da)
    outs = []
    for b in range(nb):
        dmb = dm[:, b * lb:(b + 1) * lb]
        stack = jnp.concatenate([jnp.where(masks[h], dmb, jnp.zeros_like(dmb)) for h in range(hpb)], axis=0)
        outs.append(_dot_nt(stack, v[:, b * lb:(b + 1) * lb]))
    return outs[0] if nb == 1 else jnp.concatenate(outs, axis=0)


def _causal_stack(w, transposed):
    r = lax.broadcasted_iota(jnp.int32, w.shape, 0) % CHUNK
    c = lax.broadcasted_iota(jnp.int32, w.shape, 1)
    keep = (c >= r) if transposed else (c <= r)
    return jnp.where(keep, w, 0.0)


def _mix_core_forward(proj, zprev, prm, da, db, saved=None):
    n = proj.shape[0]
    ua = proj[:, 0:da]
    va = proj[:, da:2 * da]
    bg = proj[:, 2 * da:2 * da + db]
    cg = proj[:, 2 * da + db:2 * da + 2 * db]
    xb = proj[:, 2 * da + 2 * db:]
    if saved is None:
        ug, ucdf = _gelu_parts(ua)
        vg, vcdf = _gelu_parts(va)
        zc = vg - _head_mean(vg, prm["pmat"])
        rs = lax.rsqrt(_head_mean(zc * zc, prm["pmat"], exact=False) + EPS)
        vhat = zc * rs
        vln = (vhat * prm["lng"] + prm["lnb"]).astype(BF16)
        wst = _causal_stack(prm["wst"], False).astype(BF16)
        mixed = [_mix_heads(wst, vln[j * CHUNK:(j + 1) * CHUNK], da) + prm["bias"] for j in range(n // CHUNK)]
        mixed = mixed[0] if len(mixed) == 1 else jnp.concatenate(mixed, axis=0)
    else:
        ucdf, vcdf, vhat, rs, mixed = [saved[k].astype(F32) for k in range(5)]
        ug = ua * ucdf
        vln = (vhat * prm["lng"] + prm["lnb"]).astype(BF16)
    ya = ug * mixed
    z = cg * xb
    row = lax.broadcasted_iota(jnp.int32, z.shape, 0)
    z1 = jnp.where(row == 0, zprev[-1:], pltpu.roll(z, 1, 0))
    z2 = jnp.where(row == 0, zprev[-2:-1], jnp.where(row == 1, zprev[-1:], pltpu.roll(z, 2, 0)))
    cw = prm["convw"]
    conv = z2 * cw[0:1] + z1 * cw[1:2] + z * cw[2:3]
    yb = bg * conv
    yah, ra = _rms(ya)
    ybh, rb = _rms(yb)
    return dict(ua=ua, va=va, bg=bg, cg=cg, xb=xb, ug=ug, ucdf=ucdf, vcdf=vcdf, rs=rs, vhat=vhat, vln=vln,
                mixed=mixed, z=z, z1=z1, z2=z2, conv=conv, yah=yah, ra=ra, ybh=ybh, rb=rb)


def _mix_params(lng_ref, lnb_ref, wst_ref, bias_ref, pmat_ref, convw_ref):
    return dict(lng=lng_ref[...], lnb=lnb_ref[...], wst=wst_ref[...], bias=bias_ref[...],
                pmat=pmat_ref[...], convw=convw_ref[...])


def _mix_fwd(x, gain, sh, sc, gate, win, wout, lng, lnb, wst, bias, pmat, convw, og, comm=None):
    T, D = x.shape
    P = win.shape[1]
    B = gate.shape[0]
    da = lng.shape[1]
    db = convw.shape[1]
    tm = _tile(MIX_TILE, T // B)
    tps = (T // B) // tm

    def body(x_ref, gain_ref, sh_ref, sc_ref, gate_ref, win_ref, wout_ref, lng_ref, lnb_ref, wst_ref, bias_ref,
             pmat_ref, convw_ref, og_ref, xo_ref, proj_ref, h_ref, yn_ref, sv_ref, halo):
        i = pl.program_id(0)

        @pl.when((i % tps) == 0)
        def _():
            halo[...] = jnp.zeros_like(halo)

        h = (_rms(x_ref[...])[0] * gain_ref[...] * (1.0 + sc_ref[0]) + sh_ref[0]).astype(BF16)
        h_ref[...] = h
        proj = _dot(h, win_ref[...])
        proj_ref[...] = proj.astype(BF16)
        prm = _mix_params(lng_ref, lnb_ref, wst_ref, bias_ref, pmat_ref, convw_ref)
        r = _mix_core_forward(proj, halo[...], prm, da, db)
        halo[...] = r["z"][tm - 8:tm]
        for k, name in enumerate(("ucdf", "vcdf", "vhat", "rs", "mixed")):
            sv_ref[k] = r[name].astype(BF16)
        og = og_ref[...]
        yn_ref[:, 0:da] = (r["yah"] * og[:, 0:da]).astype(BF16)
        yn_ref[:, da:] = (r["ybh"] * og[:, da:]).astype(BF16)
        xo_ref[...] = x_ref[...] + gate_ref[0] * _dot(yn_ref[...], wout_ref[...])

    full = lambda a: pl.BlockSpec(a.shape, lambda i: (0,) * a.ndim)
    seq = lambda i: (i // tps, 0, 0)
    row = lambda i: (i, 0)
    return _call(
        body, "mix_fwd", (T // tm,),
        [
            pl.BlockSpec((tm, D), row),
            pl.BlockSpec((1, D), lambda i: (0, 0)),
            pl.BlockSpec((1, 1, D), seq),
            pl.BlockSpec((1, 1, D), seq),
            pl.BlockSpec((1, 1, D), seq),
            pl.BlockSpec((D, P), lambda i: (0, 0), pipeline_mode=pl.Buffered(1)),
            full(wout), full(lng), full(lnb), full(wst), full(bias), full(pmat), full(convw), full(og),
        ],
        [pl.BlockSpec((tm, D), row), pl.BlockSpec((tm, P), row), pl.BlockSpec((tm, D), row),
         pl.BlockSpec((tm, D), row), pl.BlockSpec((5, tm, da), lambda i: (0, i, 0))],
        [jax.ShapeDtypeStruct((T, D), F32), jax.ShapeDtypeStruct((T, P), BF16), jax.ShapeDtypeStruct((T, D), BF16),
         jax.ShapeDtypeStruct((T, D), BF16), jax.ShapeDtypeStruct((5, T, da), BF16)],
        [pltpu.VMEM((8, db), F32)],
        (x, gain, sh, sc, gate, win, wout, lng, lnb, wst, bias, pmat, convw, og), comm)


def _mix_core_bwd(proj, sv, dxo, gate, wout, lng, lnb, wstt, pmat, convw, og, comm=None):
    T, P = proj.shape
    D = dxo.shape[1]
    B = gate.shape[0]
    da = lng.shape[1]
    db = convw.shape[1]
    assert da == db and P == 2 * da + 3 * db
    tm = _tile(MIX_BWD_TILE, T // B)
    tps = (T // B) // tm
    nt = T // tm
    hd = da // N_HEADS

    def body(proj_ref, cgp_ref, xbp_ref, sv_ref, dxo_ref, gate_ref, wout_ref, lng_ref, lnb_ref, wstt_ref,
             pmat_ref, convw_ref, og_ref,
             dproj_ref, do_ref, dgate_ref, dog_ref, dwst_ref, dbias_ref, dlng_ref, dlnb_ref, dconvw_ref, carry):
        i = pl.program_id(0)
        ri = nt - 1 - i
        first = i == 0
        end_of_seq = (ri % tps) == tps - 1
        start_of_seq = (ri % tps) == 0

        @pl.when(end_of_seq)
        def _():
            carry[...] = jnp.zeros_like(carry)

        prm = dict(lng=lng_ref[...], lnb=lnb_ref[...], pmat=pmat_ref[...], convw=convw_ref[...])
        zprev = jnp.where(start_of_seq, 0.0, cgp_ref[...].astype(F32) * xbp_ref[...].astype(F32))
        r = _mix_core_forward(proj_ref[...].astype(F32), zprev, prm, da, db, saved=sv_ref)
        og = og_ref[...]
        pmat = prm["pmat"]

        yn = jnp.concatenate([(r["yah"] * og[:, 0:da]).astype(BF16), (r["ybh"] * og[:, da:]).astype(BF16)], axis=1)
        dxo = dxo_ref[...]
        o = _dot(yn, wout_ref[...])
        _acc(dgate_ref.at[0], end_of_seq, jnp.sum(dxo * o, axis=0, keepdims=True))
        d_o = (gate_ref[0] * dxo).astype(BF16)
        do_ref[...] = d_o
        dyn = _dot_nt(d_o, wout_ref[...])

        def rms_bwd(dyn_g, yh, rr, og_g):
            dog_g = jnp.sum(dyn_g * yh, axis=0, keepdims=True)
            dyh = dyn_g * og_g
            return rr * (dyh - yh * jnp.mean(dyh * yh, axis=-1, keepdims=True)), dog_g

        dya, dog_a = rms_bwd(dyn[:, 0:da], r["yah"], r["ra"], og[:, 0:da])
        dyb, dog_b = rms_bwd(dyn[:, da:], r["ybh"], r["rb"], og[:, da:])
        _acc(dog_ref, first, jnp.concatenate([dog_a, dog_b], axis=1))

        dug = dya * r["mixed"]
        dmixed = dya * r["ug"]
        wstt_b = _causal_stack(wstt_ref[...], True).astype(BF16)
        dbias = jnp.zeros((CHUNK, da), F32)
        dwst = jnp.zeros((N_HEADS * CHUNK, CHUNK), F32)
        dvln = []
        for j in range(tm // CHUNK):
            dm = dmixed[j * CHUNK:(j + 1) * CHUNK]
            dbias = dbias + dm
            dmb = dm.astype(BF16)
            dwst = dwst + _mix_heads_grad(dmb, r["vln"][j * CHUNK:(j + 1) * CHUNK], da)
            dvln.append(_mix_heads(wstt_b, dmb, da))
        dvln = dvln[0] if len(dvln) == 1 else jnp.concatenate(dvln, axis=0)
        _acc(dbias_ref, first, dbias)
        _acc(dwst_ref, first, dwst)
        _acc(dlng_ref, first, jnp.sum(dvln * r["vhat"], axis=0, keepdims=True))
        _acc(dlnb_ref, first, jnp.sum(dvln, axis=0, keepdims=True))
        dvhat = dvln * prm["lng"]
        dvg = r["rs"] * (dvhat - _head_mean(dvhat, pmat, exact=False)
                         - r["vhat"] * _head_mean(dvhat * r["vhat"], pmat, exact=False))
        dproj_ref[:, 0:da] = (dug * _gelu_grad(r["ua"], r["ucdf"])).astype(BF16)
        dproj_ref[:, da:2 * da] = (dvg * _gelu_grad(r["va"], r["vcdf"])).astype(BF16)

        dproj_ref[:, 2 * da:2 * da + db] = (dyb * r["conv"]).astype(BF16)
        dconv = dyb * r["bg"]
        dcw = jnp.concatenate([
            jnp.sum(dconv * r["z2"], axis=0, keepdims=True),
            jnp.sum(dconv * r["z1"], axis=0, keepdims=True),
            jnp.sum(dconv * r["z"], axis=0, keepdims=True),
            jnp.zeros((5, db), F32)], axis=0)
        _acc(dconvw_ref, first, dcw)
        nxt = carry[...]
        row = lax.broadcasted_iota(jnp.int32, dconv.shape, 0)
        dc1 = jnp.where(row == tm - 1, nxt[0:1], pltpu.roll(dconv, tm - 1, 0))
        dc2 = jnp.where(row == tm - 2, nxt[0:1], jnp.where(row == tm - 1, nxt[1:2], pltpu.roll(dconv, tm - 2, 0)))
        carry[...] = dconv[0:8]
        cw = prm["convw"]
        dz = dconv * cw[2:3] + dc1 * cw[1:2] + dc2 * cw[0:1]
        dproj_ref[:, 2 * da + db:2 * da + 2 * db] = (dz * r["xb"]).astype(BF16)
        dproj_ref[:, 2 * da + 2 * db:] = (dz * r["cg"]).astype(BF16)

        @pl.when(i == nt - 1)
        def _():
            dwst_ref[...] = _causal_stack(dwst_ref[...], False)
            dbias_ref[...] = _head_mean(dbias_ref[...], pmat) * float(hd)

    full = lambda a: pl.BlockSpec(a.shape, lambda i: (0,) * a.ndim)
    const = lambda i: (0, 0)
    rev = lambda i: (nt - 1 - i, 0)
    prev16 = lambda col: (lambda i: (jnp.maximum((nt - 1 - i) * (tm // 16) - 1, 0), col))
    return _call(
        body, "mix_core_bwd", (nt,),
        [
            pl.BlockSpec((tm, P), rev),
            pl.BlockSpec((16, db), prev16((2 * da + db) // db)),
            pl.BlockSpec((16, db), prev16((2 * da + 2 * db) // db)),
            pl.BlockSpec((5, tm, da), lambda i: (0, nt - 1 - i, 0)),
            pl.BlockSpec((tm, D), rev),
            pl.BlockSpec((1, 1, D), lambda i: ((nt - 1 - i) // tps, 0, 0)),
            full(wout), full(lng), full(lnb), full(wstt), full(pmat), full(convw), full(og),
        ],
        [
            pl.BlockSpec((tm, P), rev),
            pl.BlockSpec((tm, D), rev),
            pl.BlockSpec((1, 1, D), lambda i: ((nt - 1 - i) // tps, 0, 0)),
            pl.BlockSpec((1, D), const),
            pl.BlockSpec((N_HEADS * CHUNK, CHUNK), const),
            pl.BlockSpec((CHUNK, da), const),
            pl.BlockSpec((1, da), const),
            pl.BlockSpec((1, da), const),
            pl.BlockSpec((8, db), const),
        ],
        [
            jax.ShapeDtypeStruct((T, P), BF16),
            jax.ShapeDtypeStruct((T, D), BF16),
            jax.ShapeDtypeStruct((B, 1, D), F32),
            jax.ShapeDtypeStruct((1, D), F32),
            jax.ShapeDtypeStruct((N_HEADS * CHUNK, CHUNK), F32),
            jax.ShapeDtypeStruct((CHUNK, da), F32),
            jax.ShapeDtypeStruct((1, da), F32),
            jax.ShapeDtypeStruct((1, da), F32),
            jax.ShapeDtypeStruct((8, db), F32),
        ],
        [pltpu.VMEM((8, db), F32)],
        (proj, proj, proj, sv, dxo, gate, wout, lng, lnb, wstt, pmat, convw, og), comm)


def _ada_fwd(c_all, ada_w, ada_b):
    n, D = c_all.shape
    L, _, sa = ada_w.shape
    tn = _tile(768, sa)

    def body(c_ref, w_ref, b_ref, act_ref, o_ref):
        c = c_ref[...]
        act = (c * _sigmoid(c)).astype(BF16)
        act_ref[...] = act
        o_ref[...] = _dot(act, w_ref[...].astype(BF16)) + b_ref[...]

    return _call(
        body, "ada_fwd", (L, sa // tn),
        [
            pl.BlockSpec((n, D), lambda l, j: (0, 0)),
            pl.BlockSpec((None, D, tn), lambda l, j: (l, 0, j)),
            pl.BlockSpec((None, 1, tn), lambda l, j: (l, 0, j)),
        ],
        [
            pl.BlockSpec((n, D), lambda l, j: (0, 0)),
            pl.BlockSpec((None, n, tn), lambda l, j: (l, 0, j)),
        ],
        [jax.ShapeDtypeStruct((n, D), BF16), jax.ShapeDtypeStruct((L, n, sa), F32)],
        [],
        (c_all, ada_w, ada_b))[0]


def _ada_bwd(c_act, d_ada, comm=None):
    n, D = c_act.shape
    L, _, sa = d_ada.shape
    tn = _tile(768, sa)

    def body(c_ref, d_ref, o_ref):
        o_ref[...] = _dot_tn(c_ref[...], d_ref[...])

    return _call(
        body, "ada_bwd", (L, sa // tn),
        [pl.BlockSpec((n, D), lambda l, j: (0, 0)), pl.BlockSpec((None, n, tn), lambda l, j: (l, 0, j))],
        [pl.BlockSpec((None, D, tn), lambda l, j: (l, 0, j))],
        [jax.ShapeDtypeStruct((L, D, sa), F32)],
        [],
        (c_act, d_ada), comm)


def _colsum(a):
    L, n, C = a.shape

    def body(a_ref, o_ref):
        o_ref[...] = jnp.sum(a_ref[...], axis=0, keepdims=True)

    return _call(
        body, "colsum", (L,),
        [pl.BlockSpec((None, n, C), lambda l: (l, 0, 0))],
        [pl.BlockSpec((None, 1, C), lambda l: (l, 0, 0))],
        [jax.ShapeDtypeStruct((L, 1, C), F32)],
        [],
        (a,))[0][0]


def _row_tile(rows, cols, nbuf):
    budget = VMEM_LIMIT // 3 // (2 * nbuf * 4 * cols)
    t = rows
    while t > max(budget, 8) and t % 2 == 0 and (t // 2) % 8 == 0:
        t //= 2
    return t


def _pair_sum(g, recv, core):
    n, _, R, C = g.shape
    tr = _row_tile(R, C, 3)

    def body(core_ref, g_ref, r_ref, o_ref):
        o_ref[...] = (g_ref[...] + r_ref[...]).astype(BF16)

    return pl.pallas_call(
        body,
        name="pair_sum",
        grid_spec=pltpu.PrefetchScalarGridSpec(
            num_scalar_prefetch=1,
            grid=(n, R // tr),
            in_specs=[
                pl.BlockSpec((None, None, tr, C), lambda i, r, core_ref: (i, core_ref[0], r, 0)),
                pl.BlockSpec((None, tr, C), lambda i, r, core_ref: (i, r, 0)),
            ],
            out_specs=pl.BlockSpec((None, tr, C), lambda i, r, core_ref: (i, r, 0)),
        ),
        out_shape=jax.ShapeDtypeStruct((n, R, C), BF16),
        compiler_params=pltpu.CompilerParams(dimension_semantics=("arbitrary", "arbitrary"),
                                             vmem_limit_bytes=VMEM_LIMIT),
    )(core, g, recv)


def _chip_sum(q, core, l, n_layers, prev):
    nq, R, C = q.shape
    tr = _row_tile(R, C, 4)

    def body(core_ref, q_ref, *rest):
        o_ref = rest[-1]
        s = q_ref[0].astype(F32)
        for j in range(1, nq):
            s = s + q_ref[j].astype(F32)
        o_ref[...] = s

    in_specs = [pl.BlockSpec((nq, tr, C), lambda r, core_ref: (0, r, 0))]
    args = [core, q]
    aliases = {}
    if prev is not None:
        in_specs.append(ANY)
        args.append(prev)
        aliases = {2: 0}
    return pl.pallas_call(
        body,
        name="chip_sum",
        grid_spec=pltpu.PrefetchScalarGridSpec(
            num_scalar_prefetch=1,
            grid=(R // tr,),
            in_specs=in_specs,
            out_specs=pl.BlockSpec((None, None, tr, C), lambda r, core_ref: (l, core_ref[0], r, 0)),
        ),
        out_shape=jax.ShapeDtypeStruct((n_layers, 2, R, C), F32),
        input_output_aliases=aliases,
        compiler_params=pltpu.CompilerParams(dimension_semantics=("arbitrary",), vmem_limit_bytes=VMEM_LIMIT),
    )(*args)


def _sum_blocks(a, n):
    M = a.shape[0] // n
    C = a.shape[1]

    def body(a_ref, o_ref):
        s = a_ref[0:M]
        for j in range(1, n):
            s = s + a_ref[j * M:(j + 1) * M]
        o_ref[...] = s

    return pl.pallas_call(
        body,
        name="sum_blocks",
        out_shape=jax.ShapeDtypeStruct((M, C), F32),
        compiler_params=pltpu.CompilerParams(vmem_limit_bytes=VMEM_LIMIT),
    )(a)


def _adamw(w, g, m, v, emit_grad=False):
    R, C = w.shape
    n_out = 4 if emit_grad else 3
    tr = _row_tile(R, C, 4 + n_out) if R % 8 == 0 else R

    def body(w_ref, g_ref, m_ref, v_ref, d_ref, nm_ref, nv_ref, *g_out):
        g = g_ref[...]
        m = ADAM_B1 * m_ref[...] + (1.0 - ADAM_B1) * g
        v = ADAM_B2 * v_ref[...] + (1.0 - ADAM_B2) * (g * g)
        m_hat = m / (1.0 - ADAM_B1 ** ADAM_STEP)
        v_hat = v / (1.0 - ADAM_B2 ** ADAM_STEP)
        d_ref[...] = -ADAM_LR * (m_hat / (jnp.sqrt(v_hat) + ADAM_EPS) + ADAM_WD * w_ref[...])
        nm_ref[...] = m
        nv_ref[...] = v
        if emit_grad:
            g_out[0][...] = g

    spec = pl.BlockSpec((tr, C), lambda i: (i, 0))
    return _call(body, "adamw", (R // tr,), [spec] * 4, [spec] * n_out, [jax.ShapeDtypeStruct((R, C), F32)] * n_out,
                 [], (w, g, m, v))[0]


def kernel(x, c, ada_w, ada_b, norm_ffn1_g, ffn1_w_gu, ffn1_w_down, norm_mix_g, mix_w_in, sgu_ln_g, sgu_ln_b, sgu_w_s, sgu_b, conv_w, out_norm_g, mix_w_out, norm_ffn2_g, ffn2_w_gu, ffn2_w_down, final_norm_g, loss_target, m_ada_w, m_ada_b, m_norm_ffn1_g, m_ffn1_w_gu, m_ffn1_w_down, m_norm_mix_g, m_mix_w_in, m_sgu_ln_g, m_sgu_ln_b, m_sgu_w_s, m_sgu_b, m_conv_w, m_out_norm_g, m_mix_w_out, m_norm_ffn2_g, m_ffn2_w_gu, m_ffn2_w_down, m_final_norm_g, v_ada_w, v_ada_b, v_norm_ffn1_g, v_ffn1_w_gu, v_ffn1_w_down, v_norm_mix_g, v_mix_w_in, v_sgu_ln_g, v_sgu_ln_b, v_sgu_w_s, v_sgu_b, v_conv_w, v_out_norm_g, v_mix_w_out, v_norm_ffn2_g, v_ffn2_w_gu, v_ffn2_w_down, v_final_norm_g):
    weights = dict(ada_w=ada_w, ada_b=ada_b, norm_ffn1_g=norm_ffn1_g, ffn1_w_gu=ffn1_w_gu, ffn1_w_down=ffn1_w_down,
                   norm_mix_g=norm_mix_g, mix_w_in=mix_w_in, sgu_ln_g=sgu_ln_g, sgu_ln_b=sgu_ln_b, sgu_w_s=sgu_w_s,
                   sgu_b=sgu_b, conv_w=conv_w, out_norm_g=out_norm_g, mix_w_out=mix_w_out, norm_ffn2_g=norm_ffn2_g,
                   ffn2_w_gu=ffn2_w_gu, ffn2_w_down=ffn2_w_down, final_norm_g=final_norm_g)
    m_in = dict(ada_w=m_ada_w, ada_b=m_ada_b, norm_ffn1_g=m_norm_ffn1_g, ffn1_w_gu=m_ffn1_w_gu,
                ffn1_w_down=m_ffn1_w_down, norm_mix_g=m_norm_mix_g, mix_w_in=m_mix_w_in, sgu_ln_g=m_sgu_ln_g,
                sgu_ln_b=m_sgu_ln_b, sgu_w_s=m_sgu_w_s, sgu_b=m_sgu_b, conv_w=m_conv_w, out_norm_g=m_out_norm_g,
                mix_w_out=m_mix_w_out, norm_ffn2_g=m_norm_ffn2_g, ffn2_w_gu=m_ffn2_w_gu, ffn2_w_down=m_ffn2_w_down,
                final_norm_g=m_final_norm_g)
    v_in = dict(ada_w=v_ada_w, ada_b=v_ada_b, norm_ffn1_g=v_norm_ffn1_g, ffn1_w_gu=v_ffn1_w_gu,
                ffn1_w_down=v_ffn1_w_down, norm_mix_g=v_norm_mix_g, mix_w_in=v_mix_w_in, sgu_ln_g=v_sgu_ln_g,
                sgu_ln_b=v_sgu_ln_b, sgu_w_s=v_sgu_w_s, sgu_b=v_sgu_b, conv_w=v_conv_w, out_norm_g=v_out_norm_g,
                mix_w_out=v_mix_w_out, norm_ffn2_g=v_norm_ffn2_g, ffn2_w_gu=v_ffn2_w_gu, ffn2_w_down=v_ffn2_w_down,
                final_norm_g=v_final_norm_g)

    B, S, D = x.shape
    T = B * S
    L = ada_w.shape[0]
    F = ffn1_w_down.shape[1] * N_CHIP
    P = mix_w_in.shape[2] * N_CHIP
    DA = D // 2
    DB = D - DA
    HD = DA // N_HEADS
    SA = ada_w.shape[2]
    n_all = B * N_DEV
    mx, my, mc = _position()
    chip = 2 * mx + my
    dev = 2 * chip + mc
    core = jnp.reshape(mc, (1,)).astype(jnp.int32)

    big = ["ffn1_w_gu", "ffn1_w_down", "mix_w_in", "mix_w_out", "ffn2_w_gu", "ffn2_w_down"]
    col_sharded = dict(ffn1_w_gu=True, ffn1_w_down=False, mix_w_in=True, mix_w_out=False,
                       ffn2_w_gu=True, ffn2_w_down=False)
    shards = {k: weights[k].astype(BF16) for k in big}
    gather = lambda l, *names: _gather_comm([(shards[k], l, col_sharded[k]) for k in names])
    full = [dict() for _ in range(L)]

    def arrived(l, names, res):
        full[l].update(zip(names, res))

    n_cw = L * conv_w.shape[1]
    cw_block = jnp.pad(conv_w.reshape(n_cw, conv_w.shape[2]), ((0, 8 - n_cw), (0, 0)))
    c_all, cw_all = _comm_call(_merge(_all_gather_comm(c.reshape(8, B * D // 8)), _all_gather_comm(cw_block)),
                               "gather_c")
    c_all = c_all.reshape(n_all, D)
    cw_all = cw_all.reshape(N_CHIP, 2, 8, conv_w.shape[2])[:, 0, :n_cw]
    conv_full = jnp.transpose(cw_all.reshape(N_CHIP, L, conv_w.shape[1], conv_w.shape[2]), (1, 2, 0, 3))
    conv_full = conv_full.reshape(L, conv_w.shape[1], DB)
    ada_b_mine = lax.dynamic_slice_in_dim(ada_b, chip * SA, SA, axis=1).reshape(L, 1, SA)
    c_act, ada_part = _ada_fwd(c_all, ada_w, ada_b_mine)
    ada_all, first_w = _comm_call(_merge(_all_gather_comm(ada_part.reshape(L * n_all, SA)), gather(0, big[0])),
                                  "gather_first")
    arrived(0, big[:1], [first_w])
    ada_all = ada_all.reshape(N_CHIP, 2, L, n_all, SA)[:, 0]
    ada_all = jnp.transpose(ada_all, (1, 2, 0, 3)).reshape(L, n_all, N_CHIP * SA)
    ada = lax.dynamic_slice_in_dim(ada_all, dev * B, B, axis=1).reshape(L, B, N_MOD, 1, D)
    mods = [[ada[l, :, j] for j in range(N_MOD)] for l in range(L)]

    x0 = x.reshape(T, D)
    gains = lambda name, l: weights[name][l].reshape(1, D)
    hmask = jnp.repeat(jnp.eye(N_HEADS, dtype=F32), HD, axis=0)
    pmat = (jnp.repeat(hmask, HD, axis=1) / HD).astype(BF16)

    def mix_consts(l):
        lng = jnp.tile(sgu_ln_g[l], N_HEADS).reshape(1, DA)
        lnb = jnp.tile(sgu_ln_b[l], N_HEADS).reshape(1, DA)
        wst = sgu_w_s[l].reshape(N_HEADS * CHUNK, CHUNK)
        wstt = jnp.swapaxes(sgu_w_s[l], 1, 2).reshape(N_HEADS * CHUNK, CHUNK)
        bias = jnp.repeat(jnp.transpose(sgu_b[l]), HD, axis=1)
        return lng, lnb, wst, wstt, bias

    def fetch(fn, *args, bring=(), **kw):
        bring = [(l, k) for l, k in bring if l < L]
        comm = _gather_comm([(shards[k], l, col_sharded[k]) for l, k in bring]) if bring else None
        res, got = fn(*args, comm, **kw)
        for (l, k), a in zip(bring, got):
            full[l][k] = a
        return res

    saved = []
    xc = x0
    for l in range(L):
        sh1, sc1, g1, sh2, sc2, g2, sh3, sc3, g3 = mods[l]
        lng, lnb, wst, wstt, bias = mix_consts(l)
        w = full[l]
        if l == 0:
            gu1, a1 = fetch(_ffn_up, xc, gains("norm_ffn1_g", l), sh1, sc1, w["ffn1_w_gu"],
                            bring=[(l, "ffn1_w_down"), (l, "mix_w_in"), (l, "mix_w_out")])
            xa, f1 = fetch(_ffn_down, a1, xc, g1, w["ffn1_w_down"], bring=[(l, "ffn2_w_down")])
        else:
            xa, gu1, a1, f1 = fetch(_ffn_fwd, xc, gains("norm_ffn1_g", l), sh1, sc1, g1, w["ffn1_w_gu"],
                                    w["ffn1_w_down"], bring=[(l, "ffn2_w_gu"), (l, "mix_w_in")])
        xb, proj, h2, yn, sv = fetch(_mix_fwd, xa, gains("norm_mix_g", l), sh2, sc2, g2, w["mix_w_in"], w["mix_w_out"],
                                     lng, lnb, wst, bias, pmat, conv_full[l], gains("out_norm_g", l),
                                     bring=[(l, "ffn2_w_gu")] if l == 0 else [(l, "ffn2_w_down")])
        if l + 1 < L:
            xd, gu2, a2, f2 = fetch(_ffn_fwd, xb, gains("norm_ffn2_g", l), sh3, sc3, g3, w["ffn2_w_gu"],
                                    w["ffn2_w_down"],
                                    bring=[(l + 1, "ffn1_w_gu"), (l + 1, "mix_w_out"), (l + 1, "ffn1_w_down")])
        else:
            dx, gu2, a2, f2, loss_block, d_final = fetch(
                _ffn_fwd, xb, gains("norm_ffn2_g", l), sh3, sc3, g3, w["ffn2_w_gu"], w["ffn2_w_down"],
                head=(loss_target.reshape(T, D), final_norm_g.reshape(1, D)))
            xd = None
        saved.append(dict(x0=xc, xa=xa, xb=xb, gu1=gu1, a1=a1, f1=f1, proj=proj, h2=h2, yn=yn, sv=sv,
                          gu2=gu2, a2=a2, f2=f2))
        xc = xd

    reduced = dict.fromkeys(big)

    def halves(name, g):
        if g.ndim == 4:
            return g
        return g.reshape(N_CHIP, 2, weights[name].shape[1] // 2, g.shape[-1])

    class Reduction:
        def __init__(self, l, name, g):
            self.l, self.name, self.g, self.stage = l, name, halves(name, g), 0
            self.ici_bytes = 3 * (g.size // 8) * 2

        def step(self):
            self.stage += 1
            if self.stage == 1:
                return _sibling_half_comm([self.g])
            if self.stage == 2:
                return _scatter_comm([_pair_sum(self.g, self.got[0], core)])
            if self.stage == 3:
                reduced[self.name] = _chip_sum(self.got[0], core, self.l, L, reduced[self.name])
                return _share_comm([reduced[self.name]], self.l)
            reduced[self.name] = self.got[0]
            return None

    active, extra, gathered = [], [], {}

    def carry(fn, *args, us=None):
        left = None if us is None else us * SCATTER_BYTES_PER_US
        riders = []
        for r in active:
            if r.stage == 1 and left is not None:
                if r.ici_bytes > left * SCATTER_OVERSHOOT:
                    continue
                left -= r.ici_bytes
            riders.append(r)
        comms = [r.step() for r in riders] + [cm for cm, _ in extra]
        takers = [functools.partial(setattr, r, "got") for r in riders] + [cb for _, cb in extra]
        extra.clear()
        if fn is None:
            res, got = None, (_comm_call(_merge(*comms), "reduce_alone") if comms else [])
        else:
            res, got = fn(*args, comm=_merge(*comms))
        at = 0
        for cm, take in zip(comms, takers):
            take(got[at:at + len(cm.out_shape)])
            at += len(cm.out_shape)
        for r in riders:
            if r.stage == 3:
                r.step()
                active.remove(r)
        return res

    def reduce_later(l, name, g):
        active.append(Reduction(l, name, g))

    small = [None] * L
    dwsts = [None] * L
    d_ada = [None] * L
    for l in reversed(range(L)):
        sh1, sc1, g1, sh2, sc2, g2, sh3, sc3, g3 = mods[l]
        lng, lnb, wst, wstt, bias = mix_consts(l)
        s = saved[l]
        w = full[l]
        last = l == 0
        dx, dgu2, h3, df2, dsc3, dsh3, dgain3, dg3 = carry(
            _ffn_bwd, dx, s["xb"], s["gu2"], s["f2"], gains("norm_ffn2_g", l), sh3, sc3, g3, w["ffn2_w_gu"],
            w["ffn2_w_down"], us=170)
        ffn2_grads = [
            lambda: reduce_later(l, "ffn2_w_gu", carry(_wgrad, h3, dgu2, D, 2 * F // N_CHIP, True, "wgrad_gu",
                                                       WGRAD_TOKENS // 2, us=110)[0]),
            lambda: reduce_later(l, "ffn2_w_down", carry(_wgrad, s["a2"], df2[None], F // 2, D, False, "wgrad_down",
                                                         us=50)[0])]
        if not last:
            ffn2_grads[0]()
            ffn2_grads[1]()
        dproj, d_o, dg2, dog, dwst, dbias, dlng, dlnb, dconvw = carry(
            _mix_core_bwd, s["proj"], s["sv"], dx, g2, w["mix_w_out"], lng, lnb, wstt, pmat, conv_full[l],
            gains("out_norm_g", l), us=150)
        mix_grads = [
            lambda: reduce_later(l, "mix_w_out", carry(_wgrad, s["yn"], d_o[None], D, D, False, "wgrad_out", us=30)[0]),
            lambda: reduce_later(l, "mix_w_in", carry(_wgrad, s["h2"], dproj[None], D, P // N_CHIP, True, "wgrad_in",
                                                      us=65)[0])]
        if not last:
            mix_grads[0]()
        dx, dsc2, dsh2, dgain2 = carry(_mixin_bwd, dx, s["xa"], dproj, gains("norm_mix_g", l), sc2, w["mix_w_in"], us=60)
        if not last:
            mix_grads[1]()
        dx, dgu, h1, df, dsc1, dsh1, dgain1, dg1 = carry(
            _ffn_bwd, dx, s["x0"], s["gu1"], s["f1"], gains("norm_ffn1_g", l), sh1, sc1, g1, w["ffn1_w_gu"],
            w["ffn1_w_down"], us=170)
        d_ada[l] = jnp.concatenate([dsh1, dsc1, dg1, dsh2, dsc2, dg2, dsh3, dsc3, dg3], axis=1).reshape(B, N_MOD * D)
        small[l] = [dgain1, dgain2, dgain3, dog, dlng, dlnb, dbias[:, ::HD], dconvw]
        dwsts[l] = dwst
        if last:
            flat = [a.reshape(-1, 128) for ll in range(L) for a in small[ll]]
            flat += [d_final.reshape(-1, 128), loss_block[0:1]]
            pad = (-sum(a.shape[0] for a in flat)) % 8
            packed = jnp.concatenate(flat + [jnp.zeros((pad, 128), F32)], axis=0)
            extra.append((_all_gather_comm(jnp.stack(d_ada).reshape(L * B, N_MOD * D)),
                          lambda got: gathered.update(d_ada=got[0])))
            extra.append((_all_gather_comm(packed), lambda got: gathered.update(small=got[0])))
            for ll in range(L):
                extra.append((_all_gather_comm(dwsts[ll]), lambda got, ll=ll: gathered.update({("dwst", ll): got[0]})))
        reduce_later(l, "ffn1_w_gu", carry(_wgrad, h1, dgu, D, 2 * F // N_CHIP, True, "wgrad_gu", WGRAD_TOKENS // 2,
                                           us=110)[0])
        reduce_later(l, "ffn1_w_down", carry(_wgrad, s["a1"], df[None], F // 2, D, False, "wgrad_down", us=50)[0])
        if last:
            ffn2_grads[0]()
            ffn2_grads[1]()
            mix_grads[1]()
            mix_grads[0]()
    grad_x = dx.reshape(B, S, D)

    def finished(name):
        while any(r.name == name for r in active):
            carry(None)
        return reduced[name].reshape(weights[name].shape)

    grads = {}
    d_ada_all = jnp.transpose(gathered["d_ada"].reshape(N_DEV, L, B, N_MOD * D), (1, 0, 2, 3))
    d_ada_all = d_ada_all.reshape(L, n_all, N_MOD * D)
    grads["ada_b"] = _colsum(d_ada_all).reshape(L, N_MOD * D)
    d_ada_mine = lax.dynamic_slice_in_dim(d_ada_all, chip * SA, SA, axis=2).astype(BF16)
    grads["ada_w"] = _ada_bwd(c_act, d_ada_mine)[0][0]

    total = _sum_blocks(gathered["small"].reshape(-1, 128), N_DEV)
    pieces, at = [], 0
    for a in flat:
        pieces.append(total[at:at + a.shape[0]])
        at += a.shape[0]
    per_layer = len(small[0])
    stack = lambda j, shape: jnp.stack([pieces[l * per_layer + j].reshape(shape) for l in range(L)])
    grads["norm_ffn1_g"] = stack(0, (D,))
    grads["norm_mix_g"] = stack(1, (D,))
    grads["norm_ffn2_g"] = stack(2, (D,))
    grads["out_norm_g"] = stack(3, (D,))
    grads["sgu_ln_g"] = stack(4, (N_HEADS, HD)).sum(axis=1)
    grads["sgu_ln_b"] = stack(5, (N_HEADS, HD)).sum(axis=1)
    grads["sgu_b"] = jnp.swapaxes(stack(6, (CHUNK, N_HEADS)), 1, 2)
    g_conv = stack(7, (8, DB))[:, :conv_w.shape[1]]
    grads["conv_w"] = lax.dynamic_slice_in_dim(g_conv, chip * conv_w.shape[2], conv_w.shape[2], axis=2)
    grads["final_norm_g"] = pieces[-2].reshape(D)
    loss = pieces[-1][0, 0]
    grads["sgu_w_s"] = jnp.stack([_sum_blocks(gathered["dwst", l].reshape(-1, CHUNK), N_DEV) for l in range(L)])
    grads["sgu_w_s"] = grads["sgu_w_s"].reshape(L, N_HEADS, CHUNK, CHUNK)

    names = list(weights)
    delta, new_m, new_v = {}, {}, {}
    for k in big:
        grads[k] = finished(k)
    for k in names:
        wk = weights[k]
        view = (1, wk.shape[0]) if wk.ndim == 1 else (-1, wk.shape[-1])
        d, nm, nv, *g_again = _adamw(wk.reshape(view), grads[k].reshape(view), m_in[k].reshape(view),
                                     v_in[k].reshape(view), emit_grad=k in big)
        delta[k], new_m[k], new_v[k] = d.reshape(wk.shape), nm.reshape(wk.shape), nv.reshape(wk.shape)
        if g_again:
            grads[k] = g_again[0].reshape(wk.shape)

    return (loss, grad_x, *[grads[k] for k in names], *[delta[k] for k in names],
            *[new_m[k] for k in names], *[new_v[k] for k in names])
```

```python
import functools
import math

import jax
import jax.numpy as jnp
from jax import lax
from jax.experimental import pallas as pl
from jax.experimental.pallas import tpu as pltpu

F32 = jnp.float32
BF16 = jnp.bfloat16
MESH = pl.DeviceIdType.MESH

N_HEADS = 8
CHUNK = 128
N_MOD = 9
EPS = 1e-6
N_DEV = 8
N_CHIP = 4

ADAM_LR = 0.001
ADAM_B1 = 0.9
ADAM_B2 = 0.999
ADAM_EPS = 1e-08
ADAM_WD = 0.01
ADAM_STEP = 10

TOKEN_TILE = 512
BWD_TILE = 256
FWD_TILE = 512
FF_SLAB = 768
MIX_TILE = 512
MIX_BWD_TILE = 512
WGRAD_TOKENS = 2048
VMEM_LIMIT = 56 * 1024 * 1024

SCATTER_BYTES_PER_US = 68_000
SCATTER_OVERSHOOT = 1.25

ANY = pl.BlockSpec(memory_space=pl.ANY)


def _tile(pref, n):
    t = min(pref, n)
    assert n % t == 0, (pref, n)
    return t


def _slabs(n, width):
    return [slice(c0, min(c0 + width, n)) for c0 in range(0, n, width)]


def _dot(a, b):
    return jnp.dot(a, b, preferred_element_type=F32)


def _dot_nt(a, b):
    return lax.dot_general(a, b, (((1,), (1,)), ((), ())), preferred_element_type=F32)


def _dot_tn(a, b):
    return lax.dot_general(a, b, (((0,), (0,)), ((), ())), preferred_element_type=F32)


def _sigmoid(x):
    return 1.0 / (1.0 + jnp.exp(-x))


def _sigmoid_fast(x):
    return pl.reciprocal(1.0 + jnp.exp(-x), approx=True)


def _rms(x):
    r = lax.rsqrt(jnp.mean(x * x, axis=-1, keepdims=True) + EPS)
    return x * r, r


def _norm_mod_bwd(x, dh, gain, sc):
    xh, r = _rms(x)
    dsc = jnp.sum(dh * (xh * gain), axis=0, keepdims=True)
    dsh = jnp.sum(dh, axis=0, keepdims=True)
    dn = dh * (1.0 + sc)
    dgain = jnp.sum(dn * xh, axis=0, keepdims=True)
    dy = dn * gain
    dx = r * (dy - xh * jnp.mean(dy * xh, axis=-1, keepdims=True))
    return dx, dsc, dsh, dgain


def _acc(ref, first, val):
    @pl.when(first)
    def _():
        ref[...] = val

    @pl.when(jnp.logical_not(first))
    def _():
        ref[...] += val


class _Comm:
    def __init__(self, args, out_shape, scratch, phases, aliases=None):
        self.args, self.out_shape, self.scratch = list(args), list(out_shape), list(scratch)
        self.phases, self.aliases = phases, dict(aliases or {})


def _merge(*comms):
    comms = [c for c in comms if c is not None]
    if len(comms) <= 1:
        return comms[0] if comms else None
    args = [a for c in comms for a in c.args]
    out_shape = [o for c in comms for o in c.out_shape]
    scratch = [s for c in comms for s in c.scratch]
    aliases, ai, oi = {}, 0, 0
    for c in comms:
        aliases.update({ai + i: oi + o for i, o in c.aliases.items()})
        ai += len(c.args)
        oi += len(c.out_shape)

    def phases(ins, outs, sems):
        parts, ai, oi, si = [], 0, 0, 0
        for c in comms:
            parts.append(c.phases(ins[ai:ai + len(c.args)], outs[oi:oi + len(c.out_shape)], sems[si:si + len(c.scratch)]))
            ai, oi, si = ai + len(c.args), oi + len(c.out_shape), si + len(c.scratch)

        def run(k):
            def go():
                for p in parts:
                    if p[k] is not None:
                        p[k]()
            return go
        return run(0), run(1), run(2)

    return _Comm(args, out_shape, scratch, phases, aliases)


def _call(body, name, grid, in_specs, out_specs, out_shape, scratch, args, comm=None):
    n_in, n_out, n_scr = len(in_specs), len(out_specs), len(scratch)
    sem = ("arbitrary",) * len(grid)
    params = pltpu.CompilerParams(dimension_semantics=sem, vmem_limit_bytes=VMEM_LIMIT)
    if comm is None:
        res = pl.pallas_call(body, name=name, grid=grid, in_specs=in_specs, out_specs=out_specs, out_shape=out_shape,
                             scratch_shapes=scratch, compiler_params=params)(*args)
        return list(res), []
    m_in, m_out = len(comm.args), len(comm.out_shape)

    def full(*refs):
        c_in, c_min = refs[:n_in], refs[n_in:n_in + m_in]
        o = n_in + m_in
        c_out, c_mout = refs[o:o + n_out], refs[o + n_out:o + n_out + m_out]
        o += n_out + m_out
        c_scr, c_sem = refs[o:o + n_scr], refs[o + n_scr:]
        start, mid, finish = comm.phases(c_min, c_mout, c_sem)
        ids = [pl.program_id(a) for a in range(len(grid))]
        first = functools.reduce(jnp.logical_and, [i == 0 for i in ids])
        last = functools.reduce(jnp.logical_and, [i == g - 1 for i, g in zip(ids, grid)])
        pl.when(first)(start)
        if mid is not None:
            pl.when(last)(mid)
        body(*c_in, *c_out, *c_scr)
        pl.when(last)(finish)

    res = pl.pallas_call(
        full, name=name, grid=grid,
        in_specs=list(in_specs) + [ANY] * m_in,
        out_specs=list(out_specs) + [ANY] * m_out,
        out_shape=list(out_shape) + comm.out_shape,
        scratch_shapes=list(scratch) + comm.scratch,
        input_output_aliases={n_in + i: n_out + o for i, o in comm.aliases.items()},
        compiler_params=params,
    )(*args, *comm.args)
    return list(res[:n_out]), list(res[n_out:])


def _comm_call(comm, name):
    m_in, m_out = len(comm.args), len(comm.out_shape)

    def body(*refs):
        start, mid, finish = comm.phases(refs[:m_in], refs[m_in:m_in + m_out], refs[m_in + m_out:])
        start()
        if mid is not None:
            mid()
        finish()

    res = pl.pallas_call(
        body, name=name, in_specs=[ANY] * m_in, out_specs=[ANY] * m_out, out_shape=comm.out_shape,
        scratch_shapes=comm.scratch, input_output_aliases=comm.aliases,
    )(*comm.args)
    return list(res)


def _position():
    return lax.axis_index("x"), lax.axis_index("y"), lax.axis_index("c")


def _gather_comm(items):
    n = len(items)
    half = [s.shape[1] // 2 for s, _, _ in items]

    def full_shape(i):
        s, _, col = items[i]
        _, R, C = s.shape
        return jax.ShapeDtypeStruct((R, N_CHIP * C) if col else (N_CHIP * R, C), s.dtype)

    def phases(ins, outs, sems):
        send_sems, recv_sems, local_sems = sems
        x, y, c = _position()

        def region(i, chip, h):
            s, _, col = items[i]
            _, R, C = s.shape
            if col:
                return outs[i].at[pl.ds(h * half[i], half[i]), pl.ds(chip * C, C)]
            return outs[i].at[pl.ds(chip * R + h * half[i], half[i]), :]

        def mine(i, h):
            return ins[i].at[items[i][1], pl.ds(h * half[i], half[i]), :]

        def copies(kx, ky, kc):
            k_me = 2 * kx + ky
            sibling = (kx, ky, 1 - kc)
            chips = [(1 - kx, ky), (kx, 1 - ky), (1 - kx, 1 - ky)]
            local, first, passed, arrive_ici, arrive_d2d = [], [], [], [], []

            def remote(src, dst, s, to):
                return pltpu.make_async_remote_copy(src_ref=src, dst_ref=dst, send_sem=send_sems.at[s],
                                                    recv_sem=recv_sems.at[s], device_id=to, device_id_type=MESH)

            for i in range(n):
                for h in range(2):
                    local.append(pltpu.make_async_copy(mine(i, h), region(i, k_me, h), local_sems.at[2 * i + h]))
                for j, (px, py) in enumerate(chips):
                    s = 6 * i + j
                    first.append(remote(mine(i, kc), region(i, k_me, kc), s, (px, py, kc)))
                    got = region(i, 2 * px + py, kc)
                    arrive_ici.append(remote(got, got, s, (px, py, kc)))
                    passed.append(remote(got, got, s + 3, sibling))
                    other = region(i, 2 * px + py, 1 - kc)
                    arrive_d2d.append(remote(other, other, s + 3, sibling))
            return local, first, passed, arrive_ici, arrive_d2d

        def on_each_device(fn):
            def go():
                for kx in range(2):
                    for ky in range(2):
                        for kc in range(2):
                            pl.when((x == kx) & (y == ky) & (c == kc))(functools.partial(fn, *copies(kx, ky, kc)))
            return go

        def start(local, first, passed, arrive_ici, arrive_d2d):
            for cp in local + first:
                cp.start()

        def mid(local, first, passed, arrive_ici, arrive_d2d):
            for a, p in zip(arrive_ici, passed):
                a.wait_recv()
                p.start()

        def finish(local, first, passed, arrive_ici, arrive_d2d):
            for a in arrive_d2d:
                a.wait_recv()
            for cp in first + passed:
                cp.wait_send()
            for cp in local:
                cp.wait()

        return on_each_device(start), on_each_device(mid), on_each_device(finish)

    scratch = [pltpu.SemaphoreType.DMA((6 * n,)), pltpu.SemaphoreType.DMA((6 * n,)), pltpu.SemaphoreType.DMA((2 * n,))]
    return _Comm([s for s, _, _ in items], [full_shape(i) for i in range(n)], scratch, phases)


def _sibling_half_comm(gs):
    n = len(gs)

    def phases(ins, outs, sems):
        send_sems, recv_sems = sems
        x, y, c = _position()

        def copies():
            return [pltpu.make_async_remote_copy(
                src_ref=ins[i].at[:, 1 - c], dst_ref=outs[i], send_sem=send_sems.at[i], recv_sem=recv_sems.at[i],
                device_id=(x, y, 1 - c), device_id_type=MESH) for i in range(n)]

        def start():
            for cp in copies():
                cp.start()

        def finish():
            for cp in copies():
                cp.wait()

        return start, None, finish

    out_shape = [jax.ShapeDtypeStruct(g.shape[:1] + g.shape[2:], g.dtype) for g in gs]
    return _Comm(gs, out_shape, [pltpu.SemaphoreType.DMA((n,)), pltpu.SemaphoreType.DMA((n,))], phases)


def _scatter_comm(ps):
    n = len(ps)

    def phases(ins, outs, sems):
        send_sems, recv_sems, local_sems = sems
        x, y, c = _position()
        k_me = 2 * x + y
        chips = [(1 - x, y), (x, 1 - y), (1 - x, 1 - y)]

        def copies():
            local = [pltpu.make_async_copy(ins[i].at[k_me], outs[i].at[k_me], local_sems.at[i]) for i in range(n)]
            remote = [pltpu.make_async_remote_copy(
                src_ref=ins[i].at[2 * px + py], dst_ref=outs[i].at[k_me],
                send_sem=send_sems.at[3 * i + j], recv_sem=recv_sems.at[3 * i + j],
                device_id=(px, py, c), device_id_type=MESH) for i in range(n) for j, (px, py) in enumerate(chips)]
            return local, remote

        def start():
            local, remote = copies()
            for cp in local + remote:
                cp.start()

        def finish():
            local, remote = copies()
            for cp in remote + local:
                cp.wait()

        return start, None, finish

    scratch = [pltpu.SemaphoreType.DMA((3 * n,)), pltpu.SemaphoreType.DMA((3 * n,)), pltpu.SemaphoreType.DMA((n,))]
    return _Comm(ps, [jax.ShapeDtypeStruct(p.shape, p.dtype) for p in ps], scratch, phases)


def _share_comm(rs, l):
    n = len(rs)

    def phases(ins, outs, sems):
        send_sems, recv_sems = sems
        x, y, c = _position()

        def copy(i, h):
            return pltpu.make_async_remote_copy(
                src_ref=outs[i].at[l, h], dst_ref=outs[i].at[l, h], send_sem=send_sems.at[i], recv_sem=recv_sems.at[i],
                device_id=(x, y, 1 - c), device_id_type=MESH)

        def start():
            for i in range(n):
                copy(i, c).start()

        def finish():
            for i in range(n):
                copy(i, 1 - c).wait_recv()
            for i in range(n):
                copy(i, c).wait_send()

        return start, None, finish

    return _Comm(rs, [jax.ShapeDtypeStruct(r.shape, r.dtype) for r in rs],
                 [pltpu.SemaphoreType.DMA((n,)), pltpu.SemaphoreType.DMA((n,))], phases,
                 aliases={i: i for i in range(n)})


def _all_gather_comm(block):
    def phases(ins, outs, sems):
        send_sems, recv_sems, local_sem = sems
        (src,), (out,) = ins, outs
        x, y, c = _position()
        sibling = (x, y, 1 - c)
        chips = [(1 - x, y), (x, 1 - y), (1 - x, 1 - y)]

        def slot(px, py, pc):
            return out.at[4 * px + 2 * py + pc]

        def copy(k, blk, to, own=False):
            return pltpu.make_async_remote_copy(
                src_ref=src if own else slot(*blk), dst_ref=slot(*blk),
                send_sem=send_sems.at[k], recv_sem=recv_sems.at[k], device_id=to, device_id_type=MESH)

        mine = lambda: pltpu.make_async_copy(src, slot(x, y, c), local_sem.at[0])
        first = lambda: [copy(0, (x, y, c), sibling, True)] + [
            copy(1 + j, (x, y, c), (*chip, c), True) for j, chip in enumerate(chips)]
        passed = lambda: [copy(4 + j, (*chip, c), sibling) for j, chip in enumerate(chips)]

        def start():
            mine().start()
            for cp in first():
                cp.start()

        def mid():
            for j, (chip, p) in enumerate(zip(chips, passed())):
                copy(1 + j, (*chip, c), (x, y, c)).wait_recv()
                p.start()

        def finish():
            copy(0, sibling, (x, y, c)).wait_recv()
            for j, chip in enumerate(chips):
                copy(4 + j, (*chip, 1 - c), (x, y, c)).wait_recv()
            for cp in first() + passed():
                cp.wait_send()
            mine().wait()

        return start, mid, finish

    scratch = [pltpu.SemaphoreType.DMA((7,)), pltpu.SemaphoreType.DMA((7,)), pltpu.SemaphoreType.DMA((1,))]
    return _Comm([block], [jax.ShapeDtypeStruct((N_DEV,) + block.shape, block.dtype)], scratch, phases)


def _ffn_up(x, gain, sh, sc, wgu, comm=None):
    T, D = x.shape
    F = wgu.shape[1] // 2
    B = sh.shape[0]
    tm = _tile(TOKEN_TILE, T // B)
    tps = (T // B) // tm
    slabs = _slabs(F, FF_SLAB)

    def body(x_ref, gain_ref, sh_ref, sc_ref, w_ref, gu_ref, a_ref):
        xh, _ = _rms(x_ref[...])
        h = (xh * gain_ref[...] * (1.0 + sc_ref[0]) + sh_ref[0]).astype(BF16)

        def dots(s):
            return _dot(h, w_ref[:, s]), _dot(h, w_ref[:, slice(F + s.start, F + s.stop)])

        nxt = dots(slabs[0])
        for j, s in enumerate(slabs):
            g, u = nxt
            if j + 1 < len(slabs):
                nxt = dots(slabs[j + 1])
            gu_ref[0, :, s] = g.astype(BF16)
            gu_ref[1, :, s] = u.astype(BF16)
            a_ref[:, s] = (g * _sigmoid(g) * u).astype(BF16)

    seq = lambda i: (i // tps, 0, 0)
    return _call(
        body, "ffn_up", (T // tm,),
        [
            pl.BlockSpec((tm, D), lambda i: (i, 0)),
            pl.BlockSpec((1, D), lambda i: (0, 0)),
            pl.BlockSpec((1, 1, D), seq),
            pl.BlockSpec((1, 1, D), seq),
            pl.BlockSpec((D, 2 * F), lambda i: (0, 0), pipeline_mode=pl.Buffered(1)),
        ],
        [
            pl.BlockSpec((2, tm, F), lambda i: (0, i, 0)),
            pl.BlockSpec((tm, F), lambda i: (i, 0)),
        ],
        [
            jax.ShapeDtypeStruct((2, T, F), BF16),
            jax.ShapeDtypeStruct((T, F), BF16),
        ],
        [],
        (x, gain, sh, sc, wgu), comm)


def _ffn_down(a, x, gate, wd, comm=None):
    T, F = a.shape
    D = x.shape[1]
    B = gate.shape[0]
    tm = _tile(2 * TOKEN_TILE, T // B)
    tps = (T // B) // tm

    def body(a_ref, x_ref, gate_ref, wd_ref, xo_ref, f_ref):
        f = _dot(a_ref[...], wd_ref[...])
        f_ref[...] = f.astype(BF16)
        xo_ref[...] = x_ref[...] + 0.5 * gate_ref[0] * f

    return _call(
        body, "ffn_down", (T // tm,),
        [
            pl.BlockSpec((tm, F), lambda i: (i, 0)),
            pl.BlockSpec((tm, D), lambda i: (i, 0)),
            pl.BlockSpec((1, 1, D), lambda i: (i // tps, 0, 0)),
            pl.BlockSpec((F, D), lambda i: (0, 0), pipeline_mode=pl.Buffered(1)),
        ],
        [pl.BlockSpec((tm, D), lambda i: (i, 0)), pl.BlockSpec((tm, D), lambda i: (i, 0))],
        [jax.ShapeDtypeStruct((T, D), F32), jax.ShapeDtypeStruct((T, D), BF16)],
        [],
        (a, x, gate, wd), comm)


def _ffn_fwd(x, gain, sh, sc, gate, wgu, wd, comm=None, head=None):
    T, D = x.shape
    F = wd.shape[0]
    B = sh.shape[0]
    tm = _tile(FWD_TILE, T // B)
    tps = (T // B) // tm
    slabs = _slabs(F, FF_SLAB)

    def body(x_ref, gain_ref, sh_ref, sc_ref, gate_ref, w_ref, wd_ref, *rest):
        if head is None:
            xo_ref, gu_ref, a_ref, f_ref = rest
        else:
            t_ref, fgain_ref, xo_ref, gu_ref, a_ref, f_ref, loss_ref, dfgain_ref = rest
        x = x_ref[...]
        h = (_rms(x)[0] * gain_ref[...] * (1.0 + sc_ref[0]) + sh_ref[0]).astype(BF16)

        def dots(s):
            return _dot(h, w_ref[:, s]), _dot(h, w_ref[:, slice(F + s.start, F + s.stop)])

        nxt = dots(slabs[0])
        for j, s in enumerate(slabs):
            g, u = nxt
            if j + 1 < len(slabs):
                nxt = dots(slabs[j + 1])
            gu_ref[0, :, s] = g.astype(BF16)
            gu_ref[1, :, s] = u.astype(BF16)
            a_ref[:, s] = (g * _sigmoid(g) * u).astype(BF16)
        f = _dot(a_ref[...], wd_ref[...])
        f_ref[...] = f.astype(BF16)
        xo = x + 0.5 * gate_ref[0] * f
        if head is None:
            xo_ref[...] = xo
        else:
            first = pl.program_id(0) == 0
            xh, r = _rms(xo)
            fgain = fgain_ref[...]
            err = xh * fgain - t_ref[...]
            _acc(loss_ref, first, jnp.zeros((8, 128), F32) + 0.5 * jnp.sum(err * err) / D)
            dout = err * (1.0 / D)
            _acc(dfgain_ref, first, jnp.sum(dout * xh, axis=0, keepdims=True))
            dy = dout * fgain
            xo_ref[...] = r * (dy - xh * jnp.mean(dy * xh, axis=-1, keepdims=True))

    seq = lambda i: (i // tps, 0, 0)
    row = lambda i: (i, 0)
    const = lambda i: (0, 0)
    in_specs = [
        pl.BlockSpec((tm, D), row),
        pl.BlockSpec((1, D), const),
        pl.BlockSpec((1, 1, D), seq),
        pl.BlockSpec((1, 1, D), seq),
        pl.BlockSpec((1, 1, D), seq),
        pl.BlockSpec((D, 2 * F), const, pipeline_mode=pl.Buffered(1)),
        pl.BlockSpec((F, D), const, pipeline_mode=pl.Buffered(1)),
    ]
    out_specs = [
        pl.BlockSpec((tm, D), row),
        pl.BlockSpec((2, tm, F), lambda i: (0, i, 0)),
        pl.BlockSpec((tm, F), row),
        pl.BlockSpec((tm, D), row),
    ]
    out_shape = [
        jax.ShapeDtypeStruct((T, D), F32),
        jax.ShapeDtypeStruct((2, T, F), BF16),
        jax.ShapeDtypeStruct((T, F), BF16),
        jax.ShapeDtypeStruct((T, D), BF16),
    ]
    args = (x, gain, sh, sc, gate, wgu, wd)
    if head is not None:
        in_specs += [pl.BlockSpec((tm, D), row), pl.BlockSpec((1, D), const)]
        out_specs += [pl.BlockSpec((8, 128), const), pl.BlockSpec((1, D), const)]
        out_shape += [jax.ShapeDtypeStruct((8, 128), F32), jax.ShapeDtypeStruct((1, D), F32)]
        args += tuple(head)
    return _call(body, "ffn_fwd", (T // tm,), in_specs, out_specs, out_shape, [], args, comm)


def _ffn_bwd(dxo, x, gu, f, gain, sh, sc, gate, wgu, wd, comm=None):
    T, D = x.shape
    F = wd.shape[0]
    B = sc.shape[0]
    tm = _tile(BWD_TILE, T // B)
    tps = (T // B) // tm
    slabs = _slabs(F, FF_SLAB)

    def body(dxo_ref, x_ref, gu_ref, f_ref, gain_ref, sh_ref, sc_ref, gate_ref, w_ref, wd_ref,
             dx_ref, dgu_ref, h_ref, df_ref, dsc_ref, dsh_ref, dgain_ref, dgate_ref):
        i = pl.program_id(0)
        first_of_seq = (i % tps) == 0
        gain = gain_ref[...]
        sc = sc_ref[0]
        dxo = dxo_ref[...]
        x = x_ref[...]
        df = (0.5 * gate_ref[0] * dxo).astype(BF16)
        df_ref[...] = df
        nxt = _dot_nt(df, wd_ref[slabs[0], :])
        for j, s in enumerate(slabs):
            da = nxt
            if j + 1 < len(slabs):
                nxt = _dot_nt(df, wd_ref[slabs[j + 1], :])
            g = gu_ref[0, :, s]
            sg = 1.0 / (1.0 + jnp.exp(-g))
            t = g * sg
            dab = da.astype(BF16)
            dgu_ref[1, :, s] = dab * t
            dgu_ref[0, :, s] = dab * gu_ref[1, :, s] * (sg + t - t * sg)
        dh = _dot_nt(dgu_ref[0], w_ref[:, 0:F]) + _dot_nt(dgu_ref[1], w_ref[:, F:])
        dx, dsc, dsh, dgain = _norm_mod_bwd(x, dh, gain, sc)
        dx_ref[...] = dxo + dx
        h_ref[...] = (_rms(x)[0] * gain * (1.0 + sc) + sh_ref[0]).astype(BF16)
        _acc(dsc_ref.at[0], first_of_seq, dsc)
        _acc(dsh_ref.at[0], first_of_seq, dsh)
        _acc(dgain_ref, i == 0, dgain)
        _acc(dgate_ref.at[0], first_of_seq, 0.5 * jnp.sum(dxo * f_ref[...].astype(F32), axis=0, keepdims=True))

    seq = lambda i: (i // tps, 0, 0)
    row = lambda i: (i, 0)
    return _call(
        body, "ffn_bwd", (T // tm,),
        [
            pl.BlockSpec((tm, D), row),
            pl.BlockSpec((tm, D), row),
            pl.BlockSpec((2, tm, F), lambda i: (0, i, 0)),
            pl.BlockSpec((tm, D), row),
            pl.BlockSpec((1, D), lambda i: (0, 0)),
            pl.BlockSpec((1, 1, D), seq),
            pl.BlockSpec((1, 1, D), seq),
            pl.BlockSpec((1, 1, D), seq),
            pl.BlockSpec((D, 2 * F), lambda i: (0, 0), pipeline_mode=pl.Buffered(1)),
            pl.BlockSpec((F, D), lambda i: (0, 0), pipeline_mode=pl.Buffered(1)),
        ],
        [
            pl.BlockSpec((tm, D), row),
            pl.BlockSpec((2, tm, F), lambda i: (0, i, 0)),
            pl.BlockSpec((tm, D), row),
            pl.BlockSpec((tm, D), row),
            pl.BlockSpec((1, 1, D), seq),
            pl.BlockSpec((1, 1, D), seq),
            pl.BlockSpec((1, D), lambda i: (0, 0)),
            pl.BlockSpec((1, 1, D), seq),
        ],
        [
            jax.ShapeDtypeStruct((T, D), F32),
            jax.ShapeDtypeStruct((2, T, F), BF16),
            jax.ShapeDtypeStruct((T, D), BF16),
            jax.ShapeDtypeStruct((T, D), BF16),
            jax.ShapeDtypeStruct((B, 1, D), F32),
            jax.ShapeDtypeStruct((B, 1, D), F32),
            jax.ShapeDtypeStruct((1, D), F32),
            jax.ShapeDtypeStruct((B, 1, D), F32),
        ],
        [],
        (dxo, x, gu, f, gain, sh, sc, gate, wgu, wd), comm)


def _wgrad(a, b, tmm, tn, col_major, name, tokens=WGRAD_TOKENS, comm=None):
    T, M = a.shape
    nb, _, Nb = b.shape
    N = nb * Nb
    tk = _tile(tokens, T)
    span = 2 if col_major else 1
    wide = span * tn
    npb = Nb // wide
    assert M % tmm == 0 and Nb % wide == 0
    if col_major:
        assert tmm == M
        shape = (N // tn, 2, M // 2, tn)
        out_spec = pl.BlockSpec((span, 2, M // 2, tn), lambda i, j, t: (j, 0, 0, 0), pipeline_mode=pl.Buffered(1))
    else:
        shape = (M // tmm, tmm, N)
        out_spec = pl.BlockSpec((None, tmm, tn), lambda i, j, t: (i, 0, j))

    def body(a_ref, b_ref, o_ref):
        @pl.when(pl.program_id(2) == 0)
        def _():
            o_ref[...] = jnp.zeros_like(o_ref)

        res = _dot_tn(a_ref[...], b_ref[...])
        if col_major:
            for s in range(span):
                for h in range(2):
                    o_ref[s, h] += res[h * (M // 2):(h + 1) * (M // 2), s * tn:(s + 1) * tn]
        else:
            o_ref[...] += res

    return _call(
        body, name, (M // tmm, N // wide, T // tk),
        [
            pl.BlockSpec((tk, tmm), lambda i, j, t: (t, i)),
            pl.BlockSpec((None, tk, wide), lambda i, j, t: (j // npb, t, j % npb)),
        ],
        [out_spec], [jax.ShapeDtypeStruct(shape, F32)], [],
        (a, b), comm)


def _mixin_bwd(dxo, x, dproj, gain, sc, win, comm=None):
    T, D = x.shape
    P = win.shape[1]
    B = sc.shape[0]
    tm = _tile(TOKEN_TILE, T // B)
    tps = (T // B) // tm

    def body(dxo_ref, x_ref, dp_ref, gain_ref, sc_ref, w_ref, dx_ref, dsc_ref, dsh_ref, dgain_ref):
        i = pl.program_id(0)
        first_of_seq = (i % tps) == 0
        halves = _slabs(tm, tm // 2)
        nxt = _dot_nt(dp_ref[halves[0], :], w_ref[...])
        sums = None
        for j, r in enumerate(halves):
            dh = nxt
            if j + 1 < len(halves):
                nxt = _dot_nt(dp_ref[halves[j + 1], :], w_ref[...])
            part = _norm_mod_bwd(x_ref[r, :], dh, gain_ref[...], sc_ref[0])
            dx_ref[r, :] = dxo_ref[r, :] + part[0]
            sums = part[1:] if sums is None else tuple(a + b for a, b in zip(sums, part[1:]))
        _acc(dsc_ref.at[0], first_of_seq, sums[0])
        _acc(dsh_ref.at[0], first_of_seq, sums[1])
        _acc(dgain_ref, i == 0, sums[2])

    seq = lambda i: (i // tps, 0, 0)
    row = lambda i: (i, 0)
    return _call(
        body, "mixin_bwd", (T // tm,),
        [
            pl.BlockSpec((tm, D), row),
            pl.BlockSpec((tm, D), row),
            pl.BlockSpec((tm, P), row),
            pl.BlockSpec((1, D), lambda i: (0, 0)),
            pl.BlockSpec((1, 1, D), seq),
            pl.BlockSpec((D, P), lambda i: (0, 0)),
        ],
        [
            pl.BlockSpec((tm, D), row),
            pl.BlockSpec((1, 1, D), seq),
            pl.BlockSpec((1, 1, D), seq),
            pl.BlockSpec((1, D), lambda i: (0, 0)),
        ],
        [
            jax.ShapeDtypeStruct((T, D), F32),
            jax.ShapeDtypeStruct((B, 1, D), F32),
            jax.ShapeDtypeStruct((B, 1, D), F32),
            jax.ShapeDtypeStruct((1, D), F32),
        ],
        [],
        (dxo, x, dproj, gain, sc, win), comm)


def _head_mean(z, pmat, exact=True):
    hi = z.astype(BF16)
    if not exact:
        return _dot(hi, pmat)
    lo = (z - hi.astype(F32)).astype(BF16)
    return _dot(hi, pmat) + _dot(lo, pmat)


def _gelu_parts(x):
    cdf = 0.5 * (1.0 + lax.erf(x * (1.0 / math.sqrt(2.0))))
    return x * cdf, cdf


def _gelu_grad(x, cdf):
    return cdf + x * jnp.exp(-0.5 * x * x) * (1.0 / math.sqrt(2.0 * math.pi))


LANES = 128


def _head_blocks(da):
    hd = da // N_HEADS
    lb = min(LANES, da)
    col = lax.broadcasted_iota(jnp.int32, (1, lb), 1)
    return lb, lb // hd, da // lb, [(col >= h * hd) & (col < (h + 1) * hd) for h in range(lb // hd)]


def _mix_heads(w_stack, v, da):
    lb, hpb, nb, masks = _head_blocks(da)
    outs = []
    for b in range(nb):
        res = _dot(w_stack[b * hpb * CHUNK:(b + 1) * hpb * CHUNK], v[:, b * lb:(b + 1) * lb])
        out = res[0:CHUNK]
        for h in range(1, hpb):
            out = jnp.where(masks[h], res[h * CHUNK:(h + 1) * CHUNK], out)
        outs.append(out)
    return outs[0] if nb == 1 else jnp.concatenate(outs, axis=1)


def _mix_heads_grad(dm, v, da):
    lb, hpb, nb, masks = _head_blocks(da)
    outs = []
    for b in range(nb):
        dmb = dm[:, b * lb:(b + 1) * lb]
        stack = jnp.concatenate([jnp.where(masks[h], dmb, jnp.zeros_like(dmb)) for h in range(hpb)], axis=0)
        outs.append(_dot_nt(stack, v[:, b * lb:(b + 1) * lb]))
    return outs[0] if nb == 1 else jnp.concatenate(outs, axis=0)


def _causal_stack(w, transposed):
    r = lax.broadcasted_iota(jnp.int32, w.shape, 0) % CHUNK
    c = lax.broadcasted_iota(jnp.int32, w.shape, 1)
    keep = (c >= r) if transposed else (c <= r)
    return jnp.where(keep, w, 0.0)


def _mix_core_forward(proj, zprev, prm, da, db, saved=None):
    n = proj.shape[0]
    ua = proj[:, 0:da]
    va = proj[:, da:2 * da]
    bg = proj[:, 2 * da:2 * da + db]
    cg = proj[:, 2 * da + db:2 * da + 2 * db]
    xb = proj[:, 2 * da + 2 * db:]
    if saved is None:
        ug, ucdf = _gelu_parts(ua)
        vg, vcdf = _gelu_parts(va)
        zc = vg - _head_mean(vg, prm["pmat"])
        rs = lax.rsqrt(_head_mean(zc * zc, prm["pmat"], exact=False) + EPS)
        vhat = zc * rs
        vln = (vhat * prm["lng"] + prm["lnb"]).astype(BF16)
        wst = _causal_stack(prm["wst"], False).astype(BF16)
        mixed = [_mix_heads(wst, vln[j * CHUNK:(j + 1) * CHUNK], da) + prm["bias"] for j in range(n // CHUNK)]
        mixed = mixed[0] if len(mixed) == 1 else jnp.concatenate(mixed, axis=0)
    else:
        ucdf, vcdf, vhat, rs, mixed = [saved[k].astype(F32) for k in range(5)]
        ug = ua * ucdf
        vln = (vhat * prm["lng"] + prm["lnb"]).astype(BF16)
    ya = ug * mixed
    z = cg * xb
    row = lax.broadcasted_iota(jnp.int32, z.shape, 0)
    z1 = jnp.where(row == 0, zprev[7:8], pltpu.roll(z, 1, 0))
    z2 = jnp.where(row == 0, zprev[6:7], jnp.where(row == 1, zprev[7:8], pltpu.roll(z, 2, 0)))
    cw = prm["convw"]
    conv = z2 * cw[0:1] + z1 * cw[1:2] + z * cw[2:3]
    yb = bg * conv
    yah, ra = _rms(ya)
    ybh, rb = _rms(yb)
    return dict(ua=ua, va=va, bg=bg, cg=cg, xb=xb, ug=ug, ucdf=ucdf, vcdf=vcdf, rs=rs, vhat=vhat, vln=vln,
                mixed=mixed, z=z, z1=z1, z2=z2, conv=conv, yah=yah, ra=ra, ybh=ybh, rb=rb)


def _mix_params(lng_ref, lnb_ref, wst_ref, bias_ref, pmat_ref, convw_ref):
    return dict(lng=lng_ref[...], lnb=lnb_ref[...], wst=wst_ref[...], bias=bias_ref[...],
                pmat=pmat_ref[...], convw=convw_ref[...])


def _mix_fwd(x, gain, sh, sc, gate, win, wout, lng, lnb, wst, bias, pmat, convw, og, comm=None):
    T, D = x.shape
    P = win.shape[1]
    B = gate.shape[0]
    da = lng.shape[1]
    db = convw.shape[1]
    tm = _tile(MIX_TILE, T // B)
    tps = (T // B) // tm

    def body(x_ref, gain_ref, sh_ref, sc_ref, gate_ref, win_ref, wout_ref, lng_ref, lnb_ref, wst_ref, bias_ref,
             pmat_ref, convw_ref, og_ref, xo_ref, proj_ref, h_ref, yn_ref, sv_ref, halo):
        i = pl.program_id(0)

        @pl.when((i % tps) == 0)
        def _():
            halo[...] = jnp.zeros_like(halo)

        h = (_rms(x_ref[...])[0] * gain_ref[...] * (1.0 + sc_ref[0]) + sh_ref[0]).astype(BF16)
        h_ref[...] = h
        proj_ref[...] = _dot(h, win_ref[...])
        prm = _mix_params(lng_ref, lnb_ref, wst_ref, bias_ref, pmat_ref, convw_ref)
        r = _mix_core_forward(proj_ref[...], halo[...], prm, da, db)
        halo[...] = r["z"][tm - 8:tm]
        for k, name in enumerate(("ucdf", "vcdf", "vhat", "rs", "mixed")):
            sv_ref[k] = r[name].astype(BF16)
        og = og_ref[...]
        yn_ref[:, 0:da] = (r["yah"] * og[:, 0:da]).astype(BF16)
        yn_ref[:, da:] = (r["ybh"] * og[:, da:]).astype(BF16)
        xo_ref[...] = x_ref[...] + gate_ref[0] * _dot(yn_ref[...], wout_ref[...])

    full = lambda a: pl.BlockSpec(a.shape, lambda i: (0,) * a.ndim)
    seq = lambda i: (i // tps, 0, 0)
    row = lambda i: (i, 0)
    return _call(
        body, "mix_fwd", (T // tm,),
        [
            pl.BlockSpec((tm, D), row),
            pl.BlockSpec((1, D), lambda i: (0, 0)),
            pl.BlockSpec((1, 1, D), seq),
            pl.BlockSpec((1, 1, D), seq),
            pl.BlockSpec((1, 1, D), seq),
            pl.BlockSpec((D, P), lambda i: (0, 0), pipeline_mode=pl.Buffered(1)),
            full(wout), full(lng), full(lnb), full(wst), full(bias), full(pmat), full(convw), full(og),
        ],
        [pl.BlockSpec((tm, D), row), pl.BlockSpec((tm, P), row), pl.BlockSpec((tm, D), row),
         pl.BlockSpec((tm, D), row), pl.BlockSpec((5, tm, da), lambda i: (0, i, 0))],
        [jax.ShapeDtypeStruct((T, D), F32), jax.ShapeDtypeStruct((T, P), F32), jax.ShapeDtypeStruct((T, D), BF16),
         jax.ShapeDtypeStruct((T, D), BF16), jax.ShapeDtypeStruct((5, T, da), BF16)],
        [pltpu.VMEM((8, db), F32)],
        (x, gain, sh, sc, gate, win, wout, lng, lnb, wst, bias, pmat, convw, og), comm)


def _mix_core_bwd(proj, sv, dxo, gate, wout, lng, lnb, wstt, pmat, convw, og, comm=None):
    T, P = proj.shape
    D = dxo.shape[1]
    B = gate.shape[0]
    da = lng.shape[1]
    db = convw.shape[1]
    assert da == db and P == 2 * da + 3 * db
    tm = _tile(MIX_BWD_TILE, T // B)
    tps = (T // B) // tm
    nt = T // tm
    hd = da // N_HEADS

    def body(proj_ref, cgp_ref, xbp_ref, sv_ref, dxo_ref, gate_ref, wout_ref, lng_ref, lnb_ref, wstt_ref,
             pmat_ref, convw_ref, og_ref,
             dproj_ref, do_ref, dgate_ref, dog_ref, dwst_ref, dbias_ref, dlng_ref, dlnb_ref, dconvw_ref, carry):
        i = pl.program_id(0)
        ri = nt - 1 - i
        first = i == 0
        end_of_seq = (ri % tps) == tps - 1
        start_of_seq = (ri % tps) == 0

        @pl.when(end_of_seq)
        def _():
            carry[...] = jnp.zeros_like(carry)

        prm = dict(lng=lng_ref[...], lnb=lnb_ref[...], pmat=pmat_ref[...], convw=convw_ref[...])
        zprev = jnp.where(start_of_seq, 0.0, cgp_ref[...] * xbp_ref[...])
        r = _mix_core_forward(proj_ref[...], zprev, prm, da, db, saved=sv_ref)
        og = og_ref[...]
        pmat = prm["pmat"]

        yn = jnp.concatenate([(r["yah"] * og[:, 0:da]).astype(BF16), (r["ybh"] * og[:, da:]).astype(BF16)], axis=1)
        dxo = dxo_ref[...]
        o = _dot(yn, wout_ref[...])
        _acc(dgate_ref.at[0], end_of_seq, jnp.sum(dxo * o, axis=0, keepdims=True))
        d_o = (gate_ref[0] * dxo).astype(BF16)
        do_ref[...] = d_o
        dyn = _dot_nt(d_o, wout_ref[...])

        def rms_bwd(dyn_g, yh, rr, og_g):
            dog_g = jnp.sum(dyn_g * yh, axis=0, keepdims=True)
            dyh = dyn_g * og_g
            return rr * (dyh - yh * jnp.mean(dyh * yh, axis=-1, keepdims=True)), dog_g

        dya, dog_a = rms_bwd(dyn[:, 0:da], r["yah"], r["ra"], og[:, 0:da])
        dyb, dog_b = rms_bwd(dyn[:, da:], r["ybh"], r["rb"], og[:, da:])
        _acc(dog_ref, first, jnp.concatenate([dog_a, dog_b], axis=1))

        dug = dya * r["mixed"]
        dmixed = dya * r["ug"]
        wstt_b = _causal_stack(wstt_ref[...], True).astype(BF16)
        dbias = jnp.zeros((CHUNK, da), F32)
        dwst = jnp.zeros((N_HEADS * CHUNK, CHUNK), F32)
        dvln = []
        for j in range(tm // CHUNK):
            dm = dmixed[j * CHUNK:(j + 1) * CHUNK]
            dbias = dbias + dm
            dmb = dm.astype(BF16)
            dwst = dwst + _mix_heads_grad(dmb, r["vln"][j * CHUNK:(j + 1) * CHUNK], da)
            dvln.append(_mix_heads(wstt_b, dmb, da))
        dvln = dvln[0] if len(dvln) == 1 else jnp.concatenate(dvln, axis=0)
        _acc(dbias_ref, first, dbias)
        _acc(dwst_ref, first, dwst)
        _acc(dlng_ref, first, jnp.sum(dvln * r["vhat"], axis=0, keepdims=True))
        _acc(dlnb_ref, first, jnp.sum(dvln, axis=0, keepdims=True))
        dvhat = dvln * prm["lng"]
        dvg = r["rs"] * (dvhat - _head_mean(dvhat, pmat, exact=False)
                         - r["vhat"] * _head_mean(dvhat * r["vhat"], pmat, exact=False))
        dproj_ref[:, 0:da] = (dug * _gelu_grad(r["ua"], r["ucdf"])).astype(BF16)
        dproj_ref[:, da:2 * da] = (dvg * _gelu_grad(r["va"], r["vcdf"])).astype(BF16)

        dproj_ref[:, 2 * da:2 * da + db] = (dyb * r["conv"]).astype(BF16)
        dconv = dyb * r["bg"]
        dcw = jnp.concatenate([
            jnp.sum(dconv * r["z2"], axis=0, keepdims=True),
            jnp.sum(dconv * r["z1"], axis=0, keepdims=True),
            jnp.sum(dconv * r["z"], axis=0, keepdims=True),
            jnp.zeros((5, db), F32)], axis=0)
        _acc(dconvw_ref, first, dcw)
        nxt = carry[...]
        row = lax.broadcasted_iota(jnp.int32, dconv.shape, 0)
        dc1 = jnp.where(row == tm - 1, nxt[0:1], pltpu.roll(dconv, tm - 1, 0))
        dc2 = jnp.where(row == tm - 2, nxt[0:1], jnp.where(row == tm - 1, nxt[1:2], pltpu.roll(dconv, tm - 2, 0)))
        carry[...] = dconv[0:8]
        cw = prm["convw"]
        dz = dconv * cw[2:3] + dc1 * cw[1:2] + dc2 * cw[0:1]
        dproj_ref[:, 2 * da + db:2 * da + 2 * db] = (dz * r["xb"]).astype(BF16)
        dproj_ref[:, 2 * da + 2 * db:] = (dz * r["cg"]).astype(BF16)

        @pl.when(i == nt - 1)
        def _():
            dwst_ref[...] = _causal_stack(dwst_ref[...], False)
            dbias_ref[...] = _head_mean(dbias_ref[...], pmat) * float(hd)

    full = lambda a: pl.BlockSpec(a.shape, lambda i: (0,) * a.ndim)
    const = lambda i: (0, 0)
    rev = lambda i: (nt - 1 - i, 0)
    prev8 = lambda col: (lambda i: (jnp.maximum((nt - 1 - i) * (tm // 8) - 1, 0), col))
    return _call(
        body, "mix_core_bwd", (nt,),
        [
            pl.BlockSpec((tm, P), rev),
            pl.BlockSpec((8, db), prev8((2 * da + db) // db)),
            pl.BlockSpec((8, db), prev8((2 * da + 2 * db) // db)),
            pl.BlockSpec((5, tm, da), lambda i: (0, nt - 1 - i, 0)),
            pl.BlockSpec((tm, D), rev),
            pl.BlockSpec((1, 1, D), lambda i: ((nt - 1 - i) // tps, 0, 0)),
            full(wout), full(lng), full(lnb), full(wstt), full(pmat), full(convw), full(og),
        ],
        [
            pl.BlockSpec((tm, P), rev),
            pl.BlockSpec((tm, D), rev),
            pl.BlockSpec((1, 1, D), lambda i: ((nt - 1 - i) // tps, 0, 0)),
            pl.BlockSpec((1, D), const),
            pl.BlockSpec((N_HEADS * CHUNK, CHUNK), const),
            pl.BlockSpec((CHUNK, da), const),
            pl.BlockSpec((1, da), const),
            pl.BlockSpec((1, da), const),
            pl.BlockSpec((8, db), const),
        ],
        [
            jax.ShapeDtypeStruct((T, P), BF16),
            jax.ShapeDtypeStruct((T, D), BF16),
            jax.ShapeDtypeStruct((B, 1, D), F32),
            jax.ShapeDtypeStruct((1, D), F32),
            jax.ShapeDtypeStruct((N_HEADS * CHUNK, CHUNK), F32),
            jax.ShapeDtypeStruct((CHUNK, da), F32),
            jax.ShapeDtypeStruct((1, da), F32),
            jax.ShapeDtypeStruct((1, da), F32),
            jax.ShapeDtypeStruct((8, db), F32),
        ],
        [pltpu.VMEM((8, db), F32)],
        (proj, proj, proj, sv, dxo, gate, wout, lng, lnb, wstt, pmat, convw, og), comm)


def _ada_fwd(c_all, ada_w, ada_b):
    n, D = c_all.shape
    L, _, sa = ada_w.shape
    tn = _tile(768, sa)

    def body(c_ref, w_ref, b_ref, act_ref, o_ref):
        c = c_ref[...]
        act = (c * _sigmoid(c)).astype(BF16)
        act_ref[...] = act
        o_ref[...] = _dot(act, w_ref[...].astype(BF16)) + b_ref[...]

    return _call(
        body, "ada_fwd", (L, sa // tn),
        [
            pl.BlockSpec((n, D), lambda l, j: (0, 0)),
            pl.BlockSpec((None, D, tn), lambda l, j: (l, 0, j)),
            pl.BlockSpec((None, 1, tn), lambda l, j: (l, 0, j)),
        ],
        [
            pl.BlockSpec((n, D), lambda l, j: (0, 0)),
            pl.BlockSpec((None, n, tn), lambda l, j: (l, 0, j)),
        ],
        [jax.ShapeDtypeStruct((n, D), BF16), jax.ShapeDtypeStruct((L, n, sa), F32)],
        [],
        (c_all, ada_w, ada_b))[0]


def _ada_bwd(c_act, d_ada, comm=None):
    n, D = c_act.shape
    L, _, sa = d_ada.shape
    tn = _tile(768, sa)

    def body(c_ref, d_ref, o_ref):
        o_ref[...] = _dot_tn(c_ref[...], d_ref[...])

    return _call(
        body, "ada_bwd", (L, sa // tn),
        [pl.BlockSpec((n, D), lambda l, j: (0, 0)), pl.BlockSpec((None, n, tn), lambda l, j: (l, 0, j))],
        [pl.BlockSpec((None, D, tn), lambda l, j: (l, 0, j))],
        [jax.ShapeDtypeStruct((L, D, sa), F32)],
        [],
        (c_act, d_ada), comm)


def _colsum(a):
    L, n, C = a.shape

    def body(a_ref, o_ref):
        o_ref[...] = jnp.sum(a_ref[...], axis=0, keepdims=True)

    return _call(
        body, "colsum", (L,),
        [pl.BlockSpec((None, n, C), lambda l: (l, 0, 0))],
        [pl.BlockSpec((None, 1, C), lambda l: (l, 0, 0))],
        [jax.ShapeDtypeStruct((L, 1, C), F32)],
        [],
        (a,))[0][0]


def _row_tile(rows, cols, nbuf):
    budget = VMEM_LIMIT // 3 // (2 * nbuf * 4 * cols)
    t = rows
    while t > max(budget, 8) and t % 2 == 0 and (t // 2) % 8 == 0:
        t //= 2
    return t


def _pair_sum(g, recv, core):
    n, _, R, C = g.shape
    tr = _row_tile(R, C, 3)

    def body(core_ref, g_ref, r_ref, o_ref):
        o_ref[...] = (g_ref[...] + r_ref[...]).astype(BF16)

    return pl.pallas_call(
        body,
        name="pair_sum",
        grid_spec=pltpu.PrefetchScalarGridSpec(
            num_scalar_prefetch=1,
            grid=(n, R // tr),
            in_specs=[
                pl.BlockSpec((None, None, tr, C), lambda i, r, core_ref: (i, core_ref[0], r, 0)),
                pl.BlockSpec((None, tr, C), lambda i, r, core_ref: (i, r, 0)),
            ],
            out_specs=pl.BlockSpec((None, tr, C), lambda i, r, core_ref: (i, r, 0)),
        ),
        out_shape=jax.ShapeDtypeStruct((n, R, C), BF16),
        compiler_params=pltpu.CompilerParams(dimension_semantics=("arbitrary", "arbitrary"),
                                             vmem_limit_bytes=VMEM_LIMIT),
    )(core, g, recv)


def _chip_sum(q, core, l, n_layers, prev):
    nq, R, C = q.shape
    tr = _row_tile(R, C, 4)

    def body(core_ref, q_ref, *rest):
        o_ref = rest[-1]
        s = q_ref[0].astype(F32)
        for j in range(1, nq):
            s = s + q_ref[j].astype(F32)
        o_ref[...] = s

    in_specs = [pl.BlockSpec((nq, tr, C), lambda r, core_ref: (0, r, 0))]
    args = [core, q]
    aliases = {}
    if prev is not None:
        in_specs.append(ANY)
        args.append(prev)
        aliases = {2: 0}
    return pl.pallas_call(
        body,
        name="chip_sum",
        grid_spec=pltpu.PrefetchScalarGridSpec(
            num_scalar_prefetch=1,
            grid=(R // tr,),
            in_specs=in_specs,
            out_specs=pl.BlockSpec((None, None, tr, C), lambda r, core_ref: (l, core_ref[0], r, 0)),
        ),
        out_shape=jax.ShapeDtypeStruct((n_layers, 2, R, C), F32),
        input_output_aliases=aliases,
        compiler_params=pltpu.CompilerParams(dimension_semantics=("arbitrary",), vmem_limit_bytes=VMEM_LIMIT),
    )(*args)


def _sum_blocks(a, n):
    M = a.shape[0] // n
    C = a.shape[1]

    def body(a_ref, o_ref):
        s = a_ref[0:M]
        for j in range(1, n):
            s = s + a_ref[j * M:(j + 1) * M]
        o_ref[...] = s

    return pl.pallas_call(
        body,
        name="sum_blocks",
        out_shape=jax.ShapeDtypeStruct((M, C), F32),
        compiler_params=pltpu.CompilerParams(vmem_limit_bytes=VMEM_LIMIT),
    )(a)


def _adamw(w, g, m, v, emit_grad=False):
    R, C = w.shape
    n_out = 4 if emit_grad else 3
    tr = _row_tile(R, C, 4 + n_out) if R % 8 == 0 else R

    def body(w_ref, g_ref, m_ref, v_ref, d_ref, nm_ref, nv_ref, *g_out):
        g = g_ref[...]
        m = ADAM_B1 * m_ref[...] + (1.0 - ADAM_B1) * g
        v = ADAM_B2 * v_ref[...] + (1.0 - ADAM_B2) * (g * g)
        m_hat = m / (1.0 - ADAM_B1 ** ADAM_STEP)
        v_hat = v / (1.0 - ADAM_B2 ** ADAM_STEP)
        d_ref[...] = -ADAM_LR * (m_hat / (jnp.sqrt(v_hat) + ADAM_EPS) + ADAM_WD * w_ref[...])
        nm_ref[...] = m
        nv_ref[...] = v
        if emit_grad:
            g_out[0][...] = g

    spec = pl.BlockSpec((tr, C), lambda i: (i, 0))
    return _call(body, "adamw", (R // tr,), [spec] * 4, [spec] * n_out, [jax.ShapeDtypeStruct((R, C), F32)] * n_out,
                 [], (w, g, m, v))[0]


def kernel(x, c, ada_w, ada_b, norm_ffn1_g, ffn1_w_gu, ffn1_w_down, norm_mix_g, mix_w_in, sgu_ln_g, sgu_ln_b, sgu_w_s, sgu_b, conv_w, out_norm_g, mix_w_out, norm_ffn2_g, ffn2_w_gu, ffn2_w_down, final_norm_g, loss_target, m_ada_w, m_ada_b, m_norm_ffn1_g, m_ffn1_w_gu, m_ffn1_w_down, m_norm_mix_g, m_mix_w_in, m_sgu_ln_g, m_sgu_ln_b, m_sgu_w_s, m_sgu_b, m_conv_w, m_out_norm_g, m_mix_w_out, m_norm_ffn2_g, m_ffn2_w_gu, m_ffn2_w_down, m_final_norm_g, v_ada_w, v_ada_b, v_norm_ffn1_g, v_ffn1_w_gu, v_ffn1_w_down, v_norm_mix_g, v_mix_w_in, v_sgu_ln_g, v_sgu_ln_b, v_sgu_w_s, v_sgu_b, v_conv_w, v_out_norm_g, v_mix_w_out, v_norm_ffn2_g, v_ffn2_w_gu, v_ffn2_w_down, v_final_norm_g):
    weights = dict(ada_w=ada_w, ada_b=ada_b, norm_ffn1_g=norm_ffn1_g, ffn1_w_gu=ffn1_w_gu, ffn1_w_down=ffn1_w_down,
                   norm_mix_g=norm_mix_g, mix_w_in=mix_w_in, sgu_ln_g=sgu_ln_g, sgu_ln_b=sgu_ln_b, sgu_w_s=sgu_w_s,
                   sgu_b=sgu_b, conv_w=conv_w, out_norm_g=out_norm_g, mix_w_out=mix_w_out, norm_ffn2_g=norm_ffn2_g,
                   ffn2_w_gu=ffn2_w_gu, ffn2_w_down=ffn2_w_down, final_norm_g=final_norm_g)
    m_in = dict(ada_w=m_ada_w, ada_b=m_ada_b, norm_ffn1_g=m_norm_ffn1_g, ffn1_w_gu=m_ffn1_w_gu,
                ffn1_w_down=m_ffn1_w_down, norm_mix_g=m_norm_mix_g, mix_w_in=m_mix_w_in, sgu_ln_g=m_sgu_ln_g,
                sgu_ln_b=m_sgu_ln_b, sgu_w_s=m_sgu_w_s, sgu_b=m_sgu_b, conv_w=m_conv_w, out_norm_g=m_out_norm_g,
                mix_w_out=m_mix_w_out, norm_ffn2_g=m_norm_ffn2_g, ffn2_w_gu=m_ffn2_w_gu, ffn2_w_down=m_ffn2_w_down,
                final_norm_g=m_final_norm_g)
    v_in = dict(ada_w=v_ada_w, ada_b=v_ada_b, norm_ffn1_g=v_norm_ffn1_g, ffn1_w_gu=v_ffn1_w_gu,
                ffn1_w_down=v_ffn1_w_down, norm_mix_g=v_norm_mix_g, mix_w_in=v_mix_w_in, sgu_ln_g=v_sgu_ln_g,
                sgu_ln_b=v_sgu_ln_b, sgu_w_s=v_sgu_w_s, sgu_b=v_sgu_b, conv_w=v_conv_w, out_norm_g=v_out_norm_g,
                mix_w_out=v_mix_w_out, norm_ffn2_g=v_norm_ffn2_g, ffn2_w_gu=v_ffn2_w_gu, ffn2_w_down=v_ffn2_w_down,
                final_norm_g=v_final_norm_g)

    B, S, D = x.shape
    T = B * S
    L = ada_w.shape[0]
    F = ffn1_w_down.shape[1] * N_CHIP
    P = mix_w_in.shape[2] * N_CHIP
    DA = D // 2
    DB = D - DA
    HD = DA // N_HEADS
    SA = ada_w.shape[2]
    n_all = B * N_DEV
    mx, my, mc = _position()
    chip = 2 * mx + my
    dev = 2 * chip + mc
    core = jnp.reshape(mc, (1,)).astype(jnp.int32)

    big = ["ffn1_w_gu", "ffn1_w_down", "mix_w_in", "mix_w_out", "ffn2_w_gu", "ffn2_w_down"]
    col_sharded = dict(ffn1_w_gu=True, ffn1_w_down=False, mix_w_in=True, mix_w_out=False,
                       ffn2_w_gu=True, ffn2_w_down=False)
    shards = {k: weights[k].astype(BF16) for k in big}
    gather = lambda l, *names: _gather_comm([(shards[k], l, col_sharded[k]) for k in names])
    full = [dict() for _ in range(L)]

    def arrived(l, names, res):
        full[l].update(zip(names, res))

    n_cw = L * conv_w.shape[1]
    cw_block = jnp.pad(conv_w.reshape(n_cw, conv_w.shape[2]), ((0, 8 - n_cw), (0, 0)))
    c_all, cw_all = _comm_call(_merge(_all_gather_comm(c.reshape(8, B * D // 8)), _all_gather_comm(cw_block)),
                               "gather_c")
    c_all = c_all.reshape(n_all, D)
    cw_all = cw_all.reshape(N_CHIP, 2, 8, conv_w.shape[2])[:, 0, :n_cw]
    conv_full = jnp.transpose(cw_all.reshape(N_CHIP, L, conv_w.shape[1], conv_w.shape[2]), (1, 2, 0, 3))
    conv_full = conv_full.reshape(L, conv_w.shape[1], DB)
    ada_b_mine = lax.dynamic_slice_in_dim(ada_b, chip * SA, SA, axis=1).reshape(L, 1, SA)
    c_act, ada_part = _ada_fwd(c_all, ada_w, ada_b_mine)
    ada_all, first_w = _comm_call(_merge(_all_gather_comm(ada_part.reshape(L * n_all, SA)), gather(0, big[0])),
                                  "gather_first")
    arrived(0, big[:1], [first_w])
    ada_all = ada_all.reshape(N_CHIP, 2, L, n_all, SA)[:, 0]
    ada_all = jnp.transpose(ada_all, (1, 2, 0, 3)).reshape(L, n_all, N_CHIP * SA)
    ada = lax.dynamic_slice_in_dim(ada_all, dev * B, B, axis=1).reshape(L, B, N_MOD, 1, D)
    mods = [[ada[l, :, j] for j in range(N_MOD)] for l in range(L)]

    x0 = x.reshape(T, D)
    gains = lambda name, l: weights[name][l].reshape(1, D)
    hmask = jnp.repeat(jnp.eye(N_HEADS, dtype=F32), HD, axis=0)
    pmat = (jnp.repeat(hmask, HD, axis=1) / HD).astype(BF16)

    def mix_consts(l):
        lng = jnp.tile(sgu_ln_g[l], N_HEADS).reshape(1, DA)
        lnb = jnp.tile(sgu_ln_b[l], N_HEADS).reshape(1, DA)
        wst = sgu_w_s[l].reshape(N_HEADS * CHUNK, CHUNK)
        wstt = jnp.swapaxes(sgu_w_s[l], 1, 2).reshape(N_HEADS * CHUNK, CHUNK)
        bias = jnp.repeat(jnp.transpose(sgu_b[l]), HD, axis=1)
        return lng, lnb, wst, wstt, bias

    def fetch(fn, *args, bring=(), **kw):
        bring = [(l, k) for l, k in bring if l < L]
        comm = _gather_comm([(shards[k], l, col_sharded[k]) for l, k in bring]) if bring else None
        res, got = fn(*args, comm, **kw)
        for (l, k), a in zip(bring, got):
            full[l][k] = a
        return res

    saved = []
    xc = x0
    for l in range(L):
        sh1, sc1, g1, sh2, sc2, g2, sh3, sc3, g3 = mods[l]
        lng, lnb, wst, wstt, bias = mix_consts(l)
        w = full[l]
        if l == 0:
            gu1, a1 = fetch(_ffn_up, xc, gains("norm_ffn1_g", l), sh1, sc1, w["ffn1_w_gu"],
                            bring=[(l, "ffn1_w_down"), (l, "mix_w_in"), (l, "mix_w_out")])
            xa, f1 = fetch(_ffn_down, a1, xc, g1, w["ffn1_w_down"], bring=[(l, "ffn2_w_down")])
        else:
            xa, gu1, a1, f1 = fetch(_ffn_fwd, xc, gains("norm_ffn1_g", l), sh1, sc1, g1, w["ffn1_w_gu"],
                                    w["ffn1_w_down"], bring=[(l, "ffn2_w_gu"), (l, "mix_w_in")])
        xb, proj, h2, yn, sv = fetch(_mix_fwd, xa, gains("norm_mix_g", l), sh2, sc2, g2, w["mix_w_in"], w["mix_w_out"],
                                     lng, lnb, wst, bias, pmat, conv_full[l], gains("out_norm_g", l),
                                     bring=[(l, "ffn2_w_gu")] if l == 0 else [(l, "ffn2_w_down")])
        if l + 1 < L:
            xd, gu2, a2, f2 = fetch(_ffn_fwd, xb, gains("norm_ffn2_g", l), sh3, sc3, g3, w["ffn2_w_gu"],
                                    w["ffn2_w_down"],
                                    bring=[(l + 1, "ffn1_w_gu"), (l + 1, "mix_w_out"), (l + 1, "ffn1_w_down")])
        else:
            dx, gu2, a2, f2, loss_block, d_final = fetch(
                _ffn_fwd, xb, gains("norm_ffn2_g", l), sh3, sc3, g3, w["ffn2_w_gu"], w["ffn2_w_down"],
                head=(loss_target.reshape(T, D), final_norm_g.reshape(1, D)))
            xd = None
        saved.append(dict(x0=xc, xa=xa, xb=xb, gu1=gu1, a1=a1, f1=f1, proj=proj, h2=h2, yn=yn, sv=sv,
                          gu2=gu2, a2=a2, f2=f2))
        xc = xd

    reduced = dict.fromkeys(big)

    def halves(name, g):
        if g.ndim == 4:
            return g
        return g.reshape(N_CHIP, 2, weights[name].shape[1] // 2, g.shape[-1])

    class Reduction:
        def __init__(self, l, name, g):
            self.l, self.name, self.g, self.stage = l, name, halves(name, g), 0
            self.ici_bytes = 3 * (g.size // 8) * 2

        def step(self):
            self.stage += 1
            if self.stage == 1:
                return _sibling_half_comm([self.g])
            if self.stage == 2:
                return _scatter_comm([_pair_sum(self.g, self.got[0], core)])
            if self.stage == 3:
                reduced[self.name] = _chip_sum(self.got[0], core, self.l, L, reduced[self.name])
                return _share_comm([reduced[self.name]], self.l)
            reduced[self.name] = self.got[0]
            return None

    active, extra, gathered = [], [], {}

    def carry(fn, *args, us=None):
        left = None if us is None else us * SCATTER_BYTES_PER_US
        riders = []
        for r in active:
            if r.stage == 1 and left is not None:
                if r.ici_bytes > left * SCATTER_OVERSHOOT:
                    continue
                left -= r.ici_bytes
            riders.append(r)
        comms = [r.step() for r in riders] + [cm for cm, _ in extra]
        takers = [functools.partial(setattr, r, "got") for r in riders] + [cb for _, cb in extra]
        extra.clear()
        if fn is None:
            res, got = None, (_comm_call(_merge(*comms), "reduce_alone") if comms else [])
        else:
            res, got = fn(*args, comm=_merge(*comms))
        at = 0
        for cm, take in zip(comms, takers):
            take(got[at:at + len(cm.out_shape)])
            at += len(cm.out_shape)
        for r in riders:
            if r.stage == 3:
                r.step()
                active.remove(r)
        return res

    def reduce_later(l, name, g):
        active.append(Reduction(l, name, g))

    small = [None] * L
    dwsts = [None] * L
    d_ada = [None] * L
    for l in reversed(range(L)):
        sh1, sc1, g1, sh2, sc2, g2, sh3, sc3, g3 = mods[l]
        lng, lnb, wst, wstt, bias = mix_consts(l)
        s = saved[l]
        w = full[l]
        last = l == 0
        dx, dgu2, h3, df2, dsc3, dsh3, dgain3, dg3 = carry(
            _ffn_bwd, dx, s["xb"], s["gu2"], s["f2"], gains("norm_ffn2_g", l), sh3, sc3, g3, w["ffn2_w_gu"],
            w["ffn2_w_down"], us=170)
        ffn2_grads = [
            lambda: reduce_later(l, "ffn2_w_gu", carry(_wgrad, h3, dgu2, D, 2 * F // N_CHIP, True, "wgrad_gu",
                                                       WGRAD_TOKENS, us=110)[0]),
            lambda: reduce_later(l, "ffn2_w_down", carry(_wgrad, s["a2"], df2[None], F // 2, D, False, "wgrad_down",
                                                         us=50)[0])]
        if not last:
            ffn2_grads[0]()
            ffn2_grads[1]()
        dproj, d_o, dg2, dog, dwst, dbias, dlng, dlnb, dconvw = carry(
            _mix_core_bwd, s["proj"], s["sv"], dx, g2, w["mix_w_out"], lng, lnb, wstt, pmat, conv_full[l],
            gains("out_norm_g", l), us=150)
        mix_grads = [
            lambda: reduce_later(l, "mix_w_out", carry(_wgrad, s["yn"], d_o[None], D, D, False, "wgrad_out", us=30)[0]),
            lambda: reduce_later(l, "mix_w_in", carry(_wgrad, s["h2"], dproj[None], D, P // N_CHIP, True, "wgrad_in",
                                                      us=65)[0])]
        if not last:
            mix_grads[0]()
        dx, dsc2, dsh2, dgain2 = carry(_mixin_bwd, dx, s["xa"], dproj, gains("norm_mix_g", l), sc2, w["mix_w_in"], us=60)
        if not last:
            mix_grads[1]()
        dx, dgu, h1, df, dsc1, dsh1, dgain1, dg1 = carry(
            _ffn_bwd, dx, s["x0"], s["gu1"], s["f1"], gains("norm_ffn1_g", l), sh1, sc1, g1, w["ffn1_w_gu"],
            w["ffn1_w_down"], us=170)
        d_ada[l] = jnp.concatenate([dsh1, dsc1, dg1, dsh2, dsc2, dg2, dsh3, dsc3, dg3], axis=1).reshape(B, N_MOD * D)
        small[l] = [dgain1, dgain2, dgain3, dog, dlng, dlnb, dbias[:, ::HD], dconvw]
        dwsts[l] = dwst
        if last:
            flat = [a.reshape(-1, 128) for ll in range(L) for a in small[ll]]
            flat += [d_final.reshape(-1, 128), loss_block[0:1]]
            pad = (-sum(a.shape[0] for a in flat)) % 8
            packed = jnp.concatenate(flat + [jnp.zeros((pad, 128), F32)], axis=0)
            extra.append((_all_gather_comm(jnp.stack(d_ada).reshape(L * B, N_MOD * D)),
                          lambda got: gathered.update(d_ada=got[0])))
            extra.append((_all_gather_comm(packed), lambda got: gathered.update(small=got[0])))
            for ll in range(L):
                extra.append((_all_gather_comm(dwsts[ll]), lambda got, ll=ll: gathered.update({("dwst", ll): got[0]})))
        reduce_later(l, "ffn1_w_gu", carry(_wgrad, h1, dgu, D, 2 * F // N_CHIP, True, "wgrad_gu", WGRAD_TOKENS,
                                           us=110)[0])
        reduce_later(l, "ffn1_w_down", carry(_wgrad, s["a1"], df[None], F // 2, D, False, "wgrad_down", us=50)[0])
        if last:
            ffn2_grads[0]()
            ffn2_grads[1]()
            mix_grads[1]()
            mix_grads[0]()
    grad_x = dx.reshape(B, S, D)

    def finished(name):
        while any(r.name == name for r in active):
            carry(None)
        return reduced[name].reshape(weights[name].shape)

    grads = {}
    d_ada_all = jnp.transpose(gathered["d_ada"].reshape(N_DEV, L, B, N_MOD * D), (1, 0, 2, 3))
    d_ada_all = d_ada_all.reshape(L, n_all, N_MOD * D)
    grads["ada_b"] = _colsum(d_ada_all).reshape(L, N_MOD * D)
    d_ada_mine = lax.dynamic_slice_in_dim(d_ada_all, chip * SA, SA, axis=2).astype(BF16)
    grads["ada_w"] = _ada_bwd(c_act, d_ada_mine)[0][0]

    total = _sum_blocks(gathered["small"].reshape(-1, 128), N_DEV)
    pieces, at = [], 0
    for a in flat:
        pieces.append(total[at:at + a.shape[0]])
        at += a.shape[0]
    per_layer = len(small[0])
    stack = lambda j, shape: jnp.stack([pieces[l * per_layer + j].reshape(shape) for l in range(L)])
    grads["norm_ffn1_g"] = stack(0, (D,))
    grads["norm_mix_g"] = stack(1, (D,))
    grads["norm_ffn2_g"] = stack(2, (D,))
    grads["out_norm_g"] = stack(3, (D,))
    grads["sgu_ln_g"] = stack(4, (N_HEADS, HD)).sum(axis=1)
    grads["sgu_ln_b"] = stack(5, (N_HEADS, HD)).sum(axis=1)
    grads["sgu_b"] = jnp.swapaxes(stack(6, (CHUNK, N_HEADS)), 1, 2)
    g_conv = stack(7, (8, DB))[:, :conv_w.shape[1]]
    grads["conv_w"] = lax.dynamic_slice_in_dim(g_conv, chip * conv_w.shape[2], conv_w.shape[2], axis=2)
    grads["final_norm_g"] = pieces[-2].reshape(D)
    loss = pieces[-1][0, 0]
    grads["sgu_w_s"] = jnp.stack([_sum_blocks(gathered["dwst", l].reshape(-1, CHUNK), N_DEV) for l in range(L)])
    grads["sgu_w_s"] = grads["sgu_w_s"].reshape(L, N_HEADS, CHUNK, CHUNK)

    names = list(weights)
    delta, new_m, new_v = {}, {}, {}
    for k in big:
        grads[k] = finished(k)
    for k in names:
        wk = weights[k]
        view = (1, wk.shape[0]) if wk.ndim == 1 else (-1, wk.shape[-1])
        d, nm, nv, *g_again = _adamw(wk.reshape(view), grads[k].reshape(view), m_in[k].reshape(view),
                                     v_in[k].reshape(view), emit_grad=k in big)
        delta[k], new_m[k], new_v[k] = d.reshape(wk.shape), nm.reshape(wk.shape), nv.reshape(wk.shape)
        if g_again:
            grads[k] = g_again[0].reshape(wk.shape)

    return (loss, grad_x, *[grads[k] for k in names], *[delta[k] for k in names],
            *[new_m[k] for k in names], *[new_v[k] for k in names])
```

```python
import functools
import math

import jax
import jax.numpy as jnp
from jax import lax
from jax.experimental import pallas as pl
from jax.experimental.pallas import tpu as pltpu

F32 = jnp.float32
BF16 = jnp.bfloat16
MESH = pl.DeviceIdType.MESH

N_HEADS = 8
CHUNK = 128
N_MOD = 9
EPS = 1e-6
N_DEV = 8
N_CHIP = 4

ADAM_LR = 0.001
ADAM_B1 = 0.9
ADAM_B2 = 0.999
ADAM_EPS = 1e-08
ADAM_WD = 0.01
ADAM_STEP = 10

TOKEN_TILE = 512
BWD_TILE = 256
FWD_TILE = 512
FF_SLAB = 768
MIX_TILE = 512
MIX_BWD_TILE = 512
WGRAD_TOKENS = 2048
VMEM_LIMIT = 56 * 1024 * 1024

SCATTER_BYTES_PER_US = 68_000
SCATTER_OVERSHOOT = 1.25

ANY = pl.BlockSpec(memory_space=pl.ANY)


def _tile(pref, n):
    t = min(pref, n)
    assert n % t == 0, (pref, n)
    return t


def _slabs(n, width):
    return [slice(c0, min(c0 + width, n)) for c0 in range(0, n, width)]


def _dot(a, b):
    return jnp.dot(a, b, preferred_element_type=F32)


def _dot_nt(a, b):
    return lax.dot_general(a, b, (((1,), (1,)), ((), ())), preferred_element_type=F32)


def _dot_tn(a, b):
    return lax.dot_general(a, b, (((0,), (0,)), ((), ())), preferred_element_type=F32)


def _sigmoid(x):
    return 1.0 / (1.0 + jnp.exp(-x))


def _sigmoid_fast(x):
    return pl.reciprocal(1.0 + jnp.exp(-x), approx=True)


def _rms(x):
    r = lax.rsqrt(jnp.mean(x * x, axis=-1, keepdims=True) + EPS)
    return x * r, r


def _norm_mod_bwd(x, dh, gain, sc):
    xh, r = _rms(x)
    dsc = jnp.sum(dh * (xh * gain), axis=0, keepdims=True)
    dsh = jnp.sum(dh, axis=0, keepdims=True)
    dn = dh * (1.0 + sc)
    dgain = jnp.sum(dn * xh, axis=0, keepdims=True)
    dy = dn * gain
    dx = r * (dy - xh * jnp.mean(dy * xh, axis=-1, keepdims=True))
    return dx, dsc, dsh, dgain


def _acc(ref, first, val):
    @pl.when(first)
    def _():
        ref[...] = val

    @pl.when(jnp.logical_not(first))
    def _():
        ref[...] += val


class _Comm:
    def __init__(self, args, out_shape, scratch, phases, aliases=None):
        self.args, self.out_shape, self.scratch = list(args), list(out_shape), list(scratch)
        self.phases, self.aliases = phases, dict(aliases or {})


def _merge(*comms):
    comms = [c for c in comms if c is not None]
    if len(comms) <= 1:
        return comms[0] if comms else None
    args = [a for c in comms for a in c.args]
    out_shape = [o for c in comms for o in c.out_shape]
    scratch = [s for c in comms for s in c.scratch]
    aliases, ai, oi = {}, 0, 0
    for c in comms:
        aliases.update({ai + i: oi + o for i, o in c.aliases.items()})
        ai += len(c.args)
        oi += len(c.out_shape)

    def phases(ins, outs, sems):
        parts, ai, oi, si = [], 0, 0, 0
        for c in comms:
            parts.append(c.phases(ins[ai:ai + len(c.args)], outs[oi:oi + len(c.out_shape)], sems[si:si + len(c.scratch)]))
            ai, oi, si = ai + len(c.args), oi + len(c.out_shape), si + len(c.scratch)

        def run(k):
            def go():
                for p in parts:
                    if p[k] is not None:
                        p[k]()
            return go
        return run(0), run(1), run(2)

    return _Comm(args, out_shape, scratch, phases, aliases)


def _call(body, name, grid, in_specs, out_specs, out_shape, scratch, args, comm=None):
    n_in, n_out, n_scr = len(in_specs), len(out_specs), len(scratch)
    sem = ("arbitrary",) * len(grid)
    params = pltpu.CompilerParams(dimension_semantics=sem, vmem_limit_bytes=VMEM_LIMIT)
    if comm is None:
        res = pl.pallas_call(body, name=name, grid=grid, in_specs=in_specs, out_specs=out_specs, out_shape=out_shape,
                             scratch_shapes=scratch, compiler_params=params)(*args)
        return list(res), []
    m_in, m_out = len(comm.args), len(comm.out_shape)

    def full(*refs):
        c_in, c_min = refs[:n_in], refs[n_in:n_in + m_in]
        o = n_in + m_in
        c_out, c_mout = refs[o:o + n_out], refs[o + n_out:o + n_out + m_out]
        o += n_out + m_out
        c_scr, c_sem = refs[o:o + n_scr], refs[o + n_scr:]
        start, mid, finish = comm.phases(c_min, c_mout, c_sem)
        ids = [pl.program_id(a) for a in range(len(grid))]
        first = functools.reduce(jnp.logical_and, [i == 0 for i in ids])
        last = functools.reduce(jnp.logical_and, [i == g - 1 for i, g in zip(ids, grid)])
        pl.when(first)(start)
        if mid is not None:
            pl.when(last)(mid)
        body(*c_in, *c_out, *c_scr)
        pl.when(last)(finish)

    res = pl.pallas_call(
        full, name=name, grid=grid,
        in_specs=list(in_specs) + [ANY] * m_in,
        out_specs=list(out_specs) + [ANY] * m_out,
        out_shape=list(out_shape) + comm.out_shape,
        scratch_shapes=list(scratch) + comm.scratch,
        input_output_aliases={n_in + i: n_out + o for i, o in comm.aliases.items()},
        compiler_params=params,
    )(*args, *comm.args)
    return list(res[:n_out]), list(res[n_out:])


def _comm_call(comm, name):
    m_in, m_out = len(comm.args), len(comm.out_shape)

    def body(*refs):
        start, mid, finish = comm.phases(refs[:m_in], refs[m_in:m_in + m_out], refs[m_in + m_out:])
        start()
        if mid is not None:
            mid()
        finish()

    res = pl.pallas_call(
        body, name=name, in_specs=[ANY] * m_in, out_specs=[ANY] * m_out, out_shape=comm.out_shape,
        scratch_shapes=comm.scratch, input_output_aliases=comm.aliases,
    )(*comm.args)
    return list(res)


def _position():
    return lax.axis_index("x"), lax.axis_index("y"), lax.axis_index("c")


def _gather_comm(items):
    n = len(items)
    half = [s.shape[1] // 2 for s, _, _ in items]

    def full_shape(i):
        s, _, col = items[i]
        _, R, C = s.shape
        return jax.ShapeDtypeStruct((R, N_CHIP * C) if col else (N_CHIP * R, C), s.dtype)

    def phases(ins, outs, sems):
        send_sems, recv_sems, local_sems = sems
        x, y, c = _position()

        def region(i, chip, h):
            s, _, col = items[i]
            _, R, C = s.shape
            if col:
                return outs[i].at[pl.ds(h * half[i], half[i]), pl.ds(chip * C, C)]
            return outs[i].at[pl.ds(chip * R + h * half[i], half[i]), :]

        def mine(i, h):
            return ins[i].at[items[i][1], pl.ds(h * half[i], half[i]), :]

        def copies(kx, ky, kc):
            k_me = 2 * kx + ky
            sibling = (kx, ky, 1 - kc)
            chips = [(1 - kx, ky), (kx, 1 - ky), (1 - kx, 1 - ky)]
            local, first, passed, arrive_ici, arrive_d2d = [], [], [], [], []

            def remote(src, dst, s, to):
                return pltpu.make_async_remote_copy(src_ref=src, dst_ref=dst, send_sem=send_sems.at[s],
                                                    recv_sem=recv_sems.at[s], device_id=to, device_id_type=MESH)

            for i in range(n):
                for h in range(2):
                    local.append(pltpu.make_async_copy(mine(i, h), region(i, k_me, h), local_sems.at[2 * i + h]))
                for j, (px, py) in enumerate(chips):
                    s = 6 * i + j
                    first.append(remote(mine(i, kc), region(i, k_me, kc), s, (px, py, kc)))
                    got = region(i, 2 * px + py, kc)
                    arrive_ici.append(remote(got, got, s, (px, py, kc)))
                    passed.append(remote(got, got, s + 3, sibling))
                    other = region(i, 2 * px + py, 1 - kc)
                    arrive_d2d.append(remote(other, other, s + 3, sibling))
            return local, first, passed, arrive_ici, arrive_d2d

        def on_each_device(fn):
            def go():
                for kx in range(2):
                    for ky in range(2):
                        for kc in range(2):
                            pl.when((x == kx) & (y == ky) & (c == kc))(functools.partial(fn, *copies(kx, ky, kc)))
            return go

        def start(local, first, passed, arrive_ici, arrive_d2d):
            for cp in local + first:
                cp.start()

        def mid(local, first, passed, arrive_ici, arrive_d2d):
            for a, p in zip(arrive_ici, passed):
                a.wait_recv()
                p.start()

        def finish(local, first, passed, arrive_ici, arrive_d2d):
            for a in arrive_d2d:
                a.wait_recv()
            for cp in first + passed:
                cp.wait_send()
            for cp in local:
                cp.wait()

        return on_each_device(start), on_each_device(mid), on_each_device(finish)

    scratch = [pltpu.SemaphoreType.DMA((6 * n,)), pltpu.SemaphoreType.DMA((6 * n,)), pltpu.SemaphoreType.DMA((2 * n,))]
    return _Comm([s for s, _, _ in items], [full_shape(i) for i in range(n)], scratch, phases)


def _sibling_half_comm(gs):
    n = len(gs)

    def phases(ins, outs, sems):
        send_sems, recv_sems = sems
        x, y, c = _position()

        def copies():
            return [pltpu.make_async_remote_copy(
                src_ref=ins[i].at[:, 1 - c], dst_ref=outs[i], send_sem=send_sems.at[i], recv_sem=recv_sems.at[i],
                device_id=(x, y, 1 - c), device_id_type=MESH) for i in range(n)]

        def start():
            for cp in copies():
                cp.start()

        def finish():
            for cp in copies():
                cp.wait()

        return start, None, finish

    out_shape = [jax.ShapeDtypeStruct(g.shape[:1] + g.shape[2:], g.dtype) for g in gs]
    return _Comm(gs, out_shape, [pltpu.SemaphoreType.DMA((n,)), pltpu.SemaphoreType.DMA((n,))], phases)


def _scatter_comm(ps):
    n = len(ps)

    def phases(ins, outs, sems):
        send_sems, recv_sems, local_sems = sems
        x, y, c = _position()
        k_me = 2 * x + y
        chips = [(1 - x, y), (x, 1 - y), (1 - x, 1 - y)]

        def copies():
            local = [pltpu.make_async_copy(ins[i].at[k_me], outs[i].at[k_me], local_sems.at[i]) for i in range(n)]
            remote = [pltpu.make_async_remote_copy(
                src_ref=ins[i].at[2 * px + py], dst_ref=outs[i].at[k_me],
                send_sem=send_sems.at[3 * i + j], recv_sem=recv_sems.at[3 * i + j],
                device_id=(px, py, c), device_id_type=MESH) for i in range(n) for j, (px, py) in enumerate(chips)]
            return local, remote

        def start():
            local, remote = copies()
            for cp in local + remote:
                cp.start()

        def finish():
            local, remote = copies()
            for cp in remote + local:
                cp.wait()

        return start, None, finish

    scratch = [pltpu.SemaphoreType.DMA((3 * n,)), pltpu.SemaphoreType.DMA((3 * n,)), pltpu.SemaphoreType.DMA((n,))]
    return _Comm(ps, [jax.ShapeDtypeStruct(p.shape, p.dtype) for p in ps], scratch, phases)


def _share_comm(rs, l):
    n = len(rs)

    def phases(ins, outs, sems):
        send_sems, recv_sems = sems
        x, y, c = _position()

        def copy(i, h):
            return pltpu.make_async_remote_copy(
                src_ref=outs[i].at[l, h], dst_ref=outs[i].at[l, h], send_sem=send_sems.at[i], recv_sem=recv_sems.at[i],
                device_id=(x, y, 1 - c), device_id_type=MESH)

        def start():
            for i in range(n):
                copy(i, c).start()

        def finish():
            for i in range(n):
                copy(i, 1 - c).wait_recv()
            for i in range(n):
                copy(i, c).wait_send()

        return start, None, finish

    return _Comm(rs, [jax.ShapeDtypeStruct(r.shape, r.dtype) for r in rs],
                 [pltpu.SemaphoreType.DMA((n,)), pltpu.SemaphoreType.DMA((n,))], phases,
                 aliases={i: i for i in range(n)})


def _all_gather_comm(block):
    def phases(ins, outs, sems):
        send_sems, recv_sems, local_sem = sems
        (src,), (out,) = ins, outs
        x, y, c = _position()
        sibling = (x, y, 1 - c)
        chips = [(1 - x, y), (x, 1 - y), (1 - x, 1 - y)]

        def slot(px, py, pc):
            return out.at[4 * px + 2 * py + pc]

        def copy(k, blk, to, own=False):
            return pltpu.make_async_remote_copy(
                src_ref=src if own else slot(*blk), dst_ref=slot(*blk),
                send_sem=send_sems.at[k], recv_sem=recv_sems.at[k], device_id=to, device_id_type=MESH)

        mine = lambda: pltpu.make_async_copy(src, slot(x, y, c), local_sem.at[0])
        first = lambda: [copy(0, (x, y, c), sibling, True)] + [
            copy(1 + j, (x, y, c), (*chip, c), True) for j, chip in enumerate(chips)]
        passed = lambda: [copy(4 + j, (*chip, c), sibling) for j, chip in enumerate(chips)]

        def start():
            mine().start()
            for cp in first():
                cp.start()

        def mid():
            for j, (chip, p) in enumerate(zip(chips, passed())):
                copy(1 + j, (*chip, c), (x, y, c)).wait_recv()
                p.start()

        def finish():
            copy(0, sibling, (x, y, c)).wait_recv()
            for j, chip in enumerate(chips):
                copy(4 + j, (*chip, 1 - c), (x, y, c)).wait_recv()
            for cp in first() + passed():
                cp.wait_send()
            mine().wait()

        return start, mid, finish

    scratch = [pltpu.SemaphoreType.DMA((7,)), pltpu.SemaphoreType.DMA((7,)), pltpu.SemaphoreType.DMA((1,))]
    return _Comm([block], [jax.ShapeDtypeStruct((N_DEV,) + block.shape, block.dtype)], scratch, phases)


def _ffn_up(x, gain, sh, sc, wgu, comm=None):
    T, D = x.shape
    F = wgu.shape[1] // 2
    B = sh.shape[0]
    tm = _tile(TOKEN_TILE, T // B)
    tps = (T // B) // tm
    slabs = _slabs(F, FF_SLAB)

    def body(x_ref, gain_ref, sh_ref, sc_ref, w_ref, gu_ref, a_ref):
        xh, _ = _rms(x_ref[...])
        h = (xh * gain_ref[...] * (1.0 + sc_ref[0]) + sh_ref[0]).astype(BF16)

        def dots(s):
            return _dot(h, w_ref[:, s]), _dot(h, w_ref[:, slice(F + s.start, F + s.stop)])

        nxt = dots(slabs[0])
        for j, s in enumerate(slabs):
            g, u = nxt
            if j + 1 < len(slabs):
                nxt = dots(slabs[j + 1])
            gu_ref[0, :, s] = g.astype(BF16)
            gu_ref[1, :, s] = u.astype(BF16)
            a_ref[:, s] = (g * _sigmoid(g) * u).astype(BF16)

    seq = lambda i: (i // tps, 0, 0)
    return _call(
        body, "ffn_up", (T // tm,),
        [
            pl.BlockSpec((tm, D), lambda i: (i, 0)),
            pl.BlockSpec((1, D), lambda i: (0, 0)),
            pl.BlockSpec((1, 1, D), seq),
            pl.BlockSpec((1, 1, D), seq),
            pl.BlockSpec((D, 2 * F), lambda i: (0, 0), pipeline_mode=pl.Buffered(1)),
        ],
        [
            pl.BlockSpec((2, tm, F), lambda i: (0, i, 0)),
            pl.BlockSpec((tm, F), lambda i: (i, 0)),
        ],
        [
            jax.ShapeDtypeStruct((2, T, F), BF16),
            jax.ShapeDtypeStruct((T, F), BF16),
        ],
        [],
        (x, gain, sh, sc, wgu), comm)


def _ffn_down(a, x, gate, wd, comm=None):
    T, F = a.shape
    D = x.shape[1]
    B = gate.shape[0]
    tm = _tile(2 * TOKEN_TILE, T // B)
    tps = (T // B) // tm

    def body(a_ref, x_ref, gate_ref, wd_ref, xo_ref, f_ref):
        f = _dot(a_ref[...], wd_ref[...])
        f_ref[...] = f.astype(BF16)
        xo_ref[...] = x_ref[...] + 0.5 * gate_ref[0] * f

    return _call(
        body, "ffn_down", (T // tm,),
        [
            pl.BlockSpec((tm, F), lambda i: (i, 0)),
            pl.BlockSpec((tm, D), lambda i: (i, 0)),
            pl.BlockSpec((1, 1, D), lambda i: (i // tps, 0, 0)),
            pl.BlockSpec((F, D), lambda i: (0, 0), pipeline_mode=pl.Buffered(1)),
        ],
        [pl.BlockSpec((tm, D), lambda i: (i, 0)), pl.BlockSpec((tm, D), lambda i: (i, 0))],
        [jax.ShapeDtypeStruct((T, D), F32), jax.ShapeDtypeStruct((T, D), BF16)],
        [],
        (a, x, gate, wd), comm)


def _ffn_fwd(x, gain, sh, sc, gate, wgu, wd, comm=None, head=None):
    T, D = x.shape
    F = wd.shape[0]
    B = sh.shape[0]
    tm = _tile(FWD_TILE, T // B)
    tps = (T // B) // tm
    slabs = _slabs(F, FF_SLAB)

    def body(x_ref, gain_ref, sh_ref, sc_ref, gate_ref, w_ref, wd_ref, *rest):
        if head is None:
            xo_ref, gu_ref, a_ref, f_ref = rest
        else:
            t_ref, fgain_ref, xo_ref, gu_ref, a_ref, f_ref, loss_ref, dfgain_ref = rest
        x = x_ref[...]
        h = (_rms(x)[0] * gain_ref[...] * (1.0 + sc_ref[0]) + sh_ref[0]).astype(BF16)

        def dots(s):
            return _dot(h, w_ref[:, s]), _dot(h, w_ref[:, slice(F + s.start, F + s.stop)])

        nxt = dots(slabs[0])
        for j, s in enumerate(slabs):
            g, u = nxt
            if j + 1 < len(slabs):
                nxt = dots(slabs[j + 1])
            gu_ref[0, :, s] = g.astype(BF16)
            gu_ref[1, :, s] = u.astype(BF16)
            a_ref[:, s] = (g * _sigmoid(g) * u).astype(BF16)
        f = _dot(a_ref[...], wd_ref[...])
        f_ref[...] = f.astype(BF16)
        xo = x + 0.5 * gate_ref[0] * f
        if head is None:
            xo_ref[...] = xo
        else:
            first = pl.program_id(0) == 0
            xh, r = _rms(xo)
            fgain = fgain_ref[...]
            err = xh * fgain - t_ref[...]
            _acc(loss_ref, first, jnp.zeros((8, 128), F32) + 0.5 * jnp.sum(err * err) / D)
            dout = err * (1.0 / D)
            _acc(dfgain_ref, first, jnp.sum(dout * xh, axis=0, keepdims=True))
            dy = dout * fgain
            xo_ref[...] = r * (dy - xh * jnp.mean(dy * xh, axis=-1, keepdims=True))

    seq = lambda i: (i // tps, 0, 0)
    row = lambda i: (i, 0)
    const = lambda i: (0, 0)
    in_specs = [
        pl.BlockSpec((tm, D), row),
        pl.BlockSpec((1, D), const),
        pl.BlockSpec((1, 1, D), seq),
        pl.BlockSpec((1, 1, D), seq),
        pl.BlockSpec((1, 1, D), seq),
        pl.BlockSpec((D, 2 * F), const, pipeline_mode=pl.Buffered(1)),
        pl.BlockSpec((F, D), const, pipeline_mode=pl.Buffered(1)),
    ]
    out_specs = [
        pl.BlockSpec((tm, D), row),
        pl.BlockSpec((2, tm, F), lambda i: (0, i, 0)),
        pl.BlockSpec((tm, F), row),
        pl.BlockSpec((tm, D), row),
    ]
    out_shape = [
        jax.ShapeDtypeStruct((T, D), F32),
        jax.ShapeDtypeStruct((2, T, F), BF16),
        jax.ShapeDtypeStruct((T, F), BF16),
        jax.ShapeDtypeStruct((T, D), BF16),
    ]
    args = (x, gain, sh, sc, gate, wgu, wd)
    if head is not None:
        in_specs += [pl.BlockSpec((tm, D), row), pl.BlockSpec((1, D), const)]
        out_specs += [pl.BlockSpec((8, 128), const), pl.BlockSpec((1, D), const)]
        out_shape += [jax.ShapeDtypeStruct((8, 128), F32), jax.ShapeDtypeStruct((1, D), F32)]
        args += tuple(head)
    return _call(body, "ffn_fwd", (T // tm,), in_specs, out_specs, out_shape, [], args, comm)


def _ffn_bwd(dxo, x, gu, f, gain, sh, sc, gate, wgu, wd, comm=None):
    T, D = x.shape
    F = wd.shape[0]
    B = sc.shape[0]
    tm = _tile(BWD_TILE, T // B)
    tps = (T // B) // tm
    slabs = _slabs(F, FF_SLAB)

    def body(dxo_ref, x_ref, gu_ref, f_ref, gain_ref, sh_ref, sc_ref, gate_ref, w_ref, wd_ref,
             dx_ref, dgu_ref, h_ref, df_ref, dsc_ref, dsh_ref, dgain_ref, dgate_ref):
        i = pl.program_id(0)
        first_of_seq = (i % tps) == 0
        gain = gain_ref[...]
        sc = sc_ref[0]
        dxo = dxo_ref[...]
        x = x_ref[...]
        df = (0.5 * gate_ref[0] * dxo).astype(BF16)
        df_ref[...] = df
        nxt = _dot_nt(df, wd_ref[slabs[0], :])
        for j, s in enumerate(slabs):
            da = nxt
            if j + 1 < len(slabs):
                nxt = _dot_nt(df, wd_ref[slabs[j + 1], :])
            g = gu_ref[0, :, s]
            sg = 1.0 / (1.0 + jnp.exp(-g))
            t = g * sg
            dab = da.astype(BF16)
            dgu_ref[1, :, s] = dab * t
            dgu_ref[0, :, s] = dab * gu_ref[1, :, s] * (sg + t - t * sg)
        dh = _dot_nt(dgu_ref[0], w_ref[:, 0:F]) + _dot_nt(dgu_ref[1], w_ref[:, F:])
        dx, dsc, dsh, dgain = _norm_mod_bwd(x, dh, gain, sc)
        dx_ref[...] = dxo + dx
        h_ref[...] = (_rms(x)[0] * gain * (1.0 + sc) + sh_ref[0]).astype(BF16)
        _acc(dsc_ref.at[0], first_of_seq, dsc)
        _acc(dsh_ref.at[0], first_of_seq, dsh)
        _acc(dgain_ref, i == 0, dgain)
        _acc(dgate_ref.at[0], first_of_seq, 0.5 * jnp.sum(dxo * f_ref[...].astype(F32), axis=0, keepdims=True))

    seq = lambda i: (i // tps, 0, 0)
    row = lambda i: (i, 0)
    return _call(
        body, "ffn_bwd", (T // tm,),
        [
            pl.BlockSpec((tm, D), row),
            pl.BlockSpec((tm, D), row),
            pl.BlockSpec((2, tm, F), lambda i: (0, i, 0)),
            pl.BlockSpec((tm, D), row),
            pl.BlockSpec((1, D), lambda i: (0, 0)),
            pl.BlockSpec((1, 1, D), seq),
            pl.BlockSpec((1, 1, D), seq),
            pl.BlockSpec((1, 1, D), seq),
            pl.BlockSpec((D, 2 * F), lambda i: (0, 0), pipeline_mode=pl.Buffered(1)),
            pl.BlockSpec((F, D), lambda i: (0, 0), pipeline_mode=pl.Buffered(1)),
        ],
        [
            pl.BlockSpec((tm, D), row),
            pl.BlockSpec((2, tm, F), lambda i: (0, i, 0)),
            pl.BlockSpec((tm, D), row),
            pl.BlockSpec((tm, D), row),
            pl.BlockSpec((1, 1, D), seq),
            pl.BlockSpec((1, 1, D), seq),
            pl.BlockSpec((1, D), lambda i: (0, 0)),
            pl.BlockSpec((1, 1, D), seq),
        ],
        [
            jax.ShapeDtypeStruct((T, D), F32),
            jax.ShapeDtypeStruct((2, T, F), BF16),
            jax.ShapeDtypeStruct((T, D), BF16),
            jax.ShapeDtypeStruct((T, D), BF16),
            jax.ShapeDtypeStruct((B, 1, D), F32),
            jax.ShapeDtypeStruct((B, 1, D), F32),
            jax.ShapeDtypeStruct((1, D), F32),
            jax.ShapeDtypeStruct((B, 1, D), F32),
        ],
        [],
        (dxo, x, gu, f, gain, sh, sc, gate, wgu, wd), comm)


def _wgrad(a, b, tmm, tn, col_major, name, tokens=WGRAD_TOKENS, comm=None):
    T, M = a.shape
    nb, _, Nb = b.shape
    N = nb * Nb
    tk = _tile(tokens, T)
    span = 2 if col_major else 1
    wide = span * tn
    npb = Nb // wide
    assert M % tmm == 0 and Nb % wide == 0
    if col_major:
        assert tmm == M
        shape = (N // tn, 2, M // 2, tn)
        out_spec = pl.BlockSpec((span, 2, M // 2, tn), lambda i, j, t: (j, 0, 0, 0))
    else:
        shape = (M // tmm, tmm, N)
        out_spec = pl.BlockSpec((None, tmm, tn), lambda i, j, t: (i, 0, j))

    def body(a_ref, b_ref, o_ref):
        @pl.when(pl.program_id(2) == 0)
        def _():
            o_ref[...] = jnp.zeros_like(o_ref)

        res = _dot_tn(a_ref[...], b_ref[...])
        if col_major:
            for s in range(span):
                for h in range(2):
                    o_ref[s, h] += res[h * (M // 2):(h + 1) * (M // 2), s * tn:(s + 1) * tn]
        else:
            o_ref[...] += res

    return _call(
        body, name, (M // tmm, N // wide, T // tk),
        [
            pl.BlockSpec((tk, tmm), lambda i, j, t: (t, i)),
            pl.BlockSpec((None, tk, wide), lambda i, j, t: (j // npb, t, j % npb)),
        ],
        [out_spec], [jax.ShapeDtypeStruct(shape, F32)], [],
        (a, b), comm)


def _mixin_bwd(dxo, x, dproj, gain, sc, win, comm=None):
    T, D = x.shape
    P = win.shape[1]
    B = sc.shape[0]
    tm = _tile(TOKEN_TILE, T // B)
    tps = (T // B) // tm

    def body(dxo_ref, x_ref, dp_ref, gain_ref, sc_ref, w_ref, dx_ref, dsc_ref, dsh_ref, dgain_ref):
        i = pl.program_id(0)
        first_of_seq = (i % tps) == 0
        halves = _slabs(tm, tm // 2)
        nxt = _dot_nt(dp_ref[halves[0], :], w_ref[...])
        sums = None
        for j, r in enumerate(halves):
            dh = nxt
            if j + 1 < len(halves):
                nxt = _dot_nt(dp_ref[halves[j + 1], :], w_ref[...])
            part = _norm_mod_bwd(x_ref[r, :], dh, gain_ref[...], sc_ref[0])
            dx_ref[r, :] = dxo_ref[r, :] + part[0]
            sums = part[1:] if sums is None else tuple(a + b for a, b in zip(sums, part[1:]))
        _acc(dsc_ref.at[0], first_of_seq, sums[0])
        _acc(dsh_ref.at[0], first_of_seq, sums[1])
        _acc(dgain_ref, i == 0, sums[2])

    seq = lambda i: (i // tps, 0, 0)
    row = lambda i: (i, 0)
    return _call(
        body, "mixin_bwd", (T // tm,),
        [
            pl.BlockSpec((tm, D), row),
            pl.BlockSpec((tm, D), row),
            pl.BlockSpec((tm, P), row),
            pl.BlockSpec((1, D), lambda i: (0, 0)),
            pl.BlockSpec((1, 1, D), seq),
            pl.BlockSpec((D, P), lambda i: (0, 0)),
        ],
        [
            pl.BlockSpec((tm, D), row),
            pl.BlockSpec((1, 1, D), seq),
            pl.BlockSpec((1, 1, D), seq),
            pl.BlockSpec((1, D), lambda i: (0, 0)),
        ],
        [
            jax.ShapeDtypeStruct((T, D), F32),
            jax.ShapeDtypeStruct((B, 1, D), F32),
            jax.ShapeDtypeStruct((B, 1, D), F32),
            jax.ShapeDtypeStruct((1, D), F32),
        ],
        [],
        (dxo, x, dproj, gain, sc, win), comm)


def _head_mean(z, pmat, exact=True):
    hi = z.astype(BF16)
    if not exact:
        return _dot(hi, pmat)
    lo = (z - hi.astype(F32)).astype(BF16)
    return _dot(hi, pmat) + _dot(lo, pmat)


def _gelu_parts(x):
    cdf = 0.5 * (1.0 + lax.erf(x * (1.0 / math.sqrt(2.0))))
    return x * cdf, cdf


def _gelu_grad(x, cdf):
    return cdf + x * jnp.exp(-0.5 * x * x) * (1.0 / math.sqrt(2.0 * math.pi))


LANES = 128


def _head_blocks(da):
    hd = da // N_HEADS
    lb = min(LANES, da)
    col = lax.broadcasted_iota(jnp.int32, (1, lb), 1)
    return lb, lb // hd, da // lb, [(col >= h * hd) & (col < (h + 1) * hd) for h in range(lb // hd)]


def _mix_heads(w_stack, v, da):
    lb, hpb, nb, masks = _head_blocks(da)
    outs = []
    for b in range(nb):
        res = _dot(w_stack[b * hpb * CHUNK:(b + 1) * hpb * CHUNK], v[:, b * lb:(b + 1) * lb])
        out = res[0:CHUNK]
        for h in range(1, hpb):
            out = jnp.where(masks[h], res[h * CHUNK:(h + 1) * CHUNK], out)
        outs.append(out)
    return outs[0] if nb == 1 else jnp.concatenate(outs, axis=1)


def _mix_heads_grad(dm, v, da):
    lb, hpb, nb, masks = _head_blocks(da)
    outs = []
    for b in range(nb):
        dmb = dm[:, b * lb:(b + 1) * lb]
        stack = jnp.concatenate([jnp.where(masks[h], dmb, jnp.zeros_like(dmb)) for h in range(hpb)], axis=0)
        outs.append(_dot_nt(stack, v[:, b * lb:(b + 1) * lb]))
    return outs[0] if nb == 1 else jnp.concatenate(outs, axis=0)


def _causal_stack(w, transposed):
    r = lax.broadcasted_iota(jnp.int32, w.shape, 0) % CHUNK
    c = lax.broadcasted_iota(jnp.int32, w.shape, 1)
    keep = (c >= r) if transposed else (c <= r)
    return jnp.where(keep, w, 0.0)


def _mix_core_forward(proj, zprev, prm, da, db, saved=None):
    n = proj.shape[0]
    ua = proj[:, 0:da]
    va = proj[:, da:2 * da]
    bg = proj[:, 2 * da:2 * da + db]
    cg = proj[:, 2 * da + db:2 * da + 2 * db]
    xb = proj[:, 2 * da + 2 * db:]
    if saved is None:
        ug, ucdf = _gelu_parts(ua)
        vg, vcdf = _gelu_parts(va)
        zc = vg - _head_mean(vg, prm["pmat"])
        rs = lax.rsqrt(_head_mean(zc * zc, prm["pmat"], exact=False) + EPS)
        vhat = zc * rs
        vln = (vhat * prm["lng"] + prm["lnb"]).astype(BF16)
        wst = _causal_stack(prm["wst"], False).astype(BF16)
        mixed = [_mix_heads(wst, vln[j * CHUNK:(j + 1) * CHUNK], da) + prm["bias"] for j in range(n // CHUNK)]
        mixed = mixed[0] if len(mixed) == 1 else jnp.concatenate(mixed, axis=0)
    else:
        ucdf, vcdf, vhat, rs, mixed = [saved[k].astype(F32) for k in range(5)]
        ug = ua * ucdf
        vln = (vhat * prm["lng"] + prm["lnb"]).astype(BF16)
    ya = ug * mixed
    z = cg * xb
    row = lax.broadcasted_iota(jnp.int32, z.shape, 0)
    z1 = jnp.where(row == 0, zprev[7:8], pltpu.roll(z, 1, 0))
    z2 = jnp.where(row == 0, zprev[6:7], jnp.where(row == 1, zprev[7:8], pltpu.roll(z, 2, 0)))
    cw = prm["convw"]
    conv = z2 * cw[0:1] + z1 * cw[1:2] + z * cw[2:3]
    yb = bg * conv
    yah, ra = _rms(ya)
    ybh, rb = _rms(yb)
    return dict(ua=ua, va=va, bg=bg, cg=cg, xb=xb, ug=ug, ucdf=ucdf, vcdf=vcdf, rs=rs, vhat=vhat, vln=vln,
                mixed=mixed, z=z, z1=z1, z2=z2, conv=conv, yah=yah, ra=ra, ybh=ybh, rb=rb)


def _mix_params(lng_ref, lnb_ref, wst_ref, bias_ref, pmat_ref, convw_ref):
    return dict(lng=lng_ref[...], lnb=lnb_ref[...], wst=wst_ref[...], bias=bias_ref[...],
                pmat=pmat_ref[...], convw=convw_ref[...])


def _mix_fwd(x, gain, sh, sc, gate, win, wout, lng, lnb, wst, bias, pmat, convw, og, comm=None):
    T, D = x.shape
    P = win.shape[1]
    B = gate.shape[0]
    da = lng.shape[1]
    db = convw.shape[1]
    tm = _tile(MIX_TILE, T // B)
    tps = (T // B) // tm

    def body(x_ref, gain_ref, sh_ref, sc_ref, gate_ref, win_ref, wout_ref, lng_ref, lnb_ref, wst_ref, bias_ref,
             pmat_ref, convw_ref, og_ref, xo_ref, proj_ref, h_ref, yn_ref, sv_ref, halo):
        i = pl.program_id(0)

        @pl.when((i % tps) == 0)
        def _():
            halo[...] = jnp.zeros_like(halo)

        h = (_rms(x_ref[...])[0] * gain_ref[...] * (1.0 + sc_ref[0]) + sh_ref[0]).astype(BF16)
        h_ref[...] = h
        proj_ref[...] = _dot(h, win_ref[...])
        prm = _mix_params(lng_ref, lnb_ref, wst_ref, bias_ref, pmat_ref, convw_ref)
        r = _mix_core_forward(proj_ref[...], halo[...], prm, da, db)
        halo[...] = r["z"][tm - 8:tm]
        for k, name in enumerate(("ucdf", "vcdf", "vhat", "rs", "mixed")):
            sv_ref[k] = r[name].astype(BF16)
        og = og_ref[...]
        yn_ref[:, 0:da] = (r["yah"] * og[:, 0:da]).astype(BF16)
        yn_ref[:, da:] = (r["ybh"] * og[:, da:]).astype(BF16)
        xo_ref[...] = x_ref[...] + gate_ref[0] * _dot(yn_ref[...], wout_ref[...])

    full = lambda a: pl.BlockSpec(a.shape, lambda i: (0,) * a.ndim)
    seq = lambda i: (i // tps, 0, 0)
    row = lambda i: (i, 0)
    return _call(
        body, "mix_fwd", (T // tm,),
        [
            pl.BlockSpec((tm, D), row),
            pl.BlockSpec((1, D), lambda i: (0, 0)),
            pl.BlockSpec((1, 1, D), seq),
            pl.BlockSpec((1, 1, D), seq),
            pl.BlockSpec((1, 1, D), seq),
            pl.BlockSpec((D, P), lambda i: (0, 0), pipeline_mode=pl.Buffered(1)),
            full(wout), full(lng), full(lnb), full(wst), full(bias), full(pmat), full(convw), full(og),
        ],
        [pl.BlockSpec((tm, D), row), pl.BlockSpec((tm, P), row), pl.BlockSpec((tm, D), row),
         pl.BlockSpec((tm, D), row), pl.BlockSpec((5, tm, da), lambda i: (0, i, 0))],
        [jax.ShapeDtypeStruct((T, D), F32), jax.ShapeDtypeStruct((T, P), F32), jax.ShapeDtypeStruct((T, D), BF16),
         jax.ShapeDtypeStruct((T, D), BF16), jax.ShapeDtypeStruct((5, T, da), BF16)],
        [pltpu.VMEM((8, db), F32)],
        (x, gain, sh, sc, gate, win, wout, lng, lnb, wst, bias, pmat, convw, og), comm)


def _mix_core_bwd(proj, sv, dxo, gate, wout, lng, lnb, wstt, pmat, convw, og, comm=None):
    T, P = proj.shape
    D = dxo.shape[1]
    B = gate.shape[0]
    da = lng.shape[1]
    db = convw.shape[1]
    assert da == db and P == 2 * da + 3 * db
    tm = _tile(MIX_BWD_TILE, T // B)
    tps = (T // B) // tm
    nt = T // tm
    hd = da // N_HEADS

    def body(proj_ref, cgp_ref, xbp_ref, sv_ref, dxo_ref, gate_ref, wout_ref, lng_ref, lnb_ref, wstt_ref,
             pmat_ref, convw_ref, og_ref,
             dproj_ref, do_ref, dgate_ref, dog_ref, dwst_ref, dbias_ref, dlng_ref, dlnb_ref, dconvw_ref, carry):
        i = pl.program_id(0)
        ri = nt - 1 - i
        first = i == 0
        end_of_seq = (ri % tps) == tps - 1
        start_of_seq = (ri % tps) == 0

        @pl.when(end_of_seq)
        def _():
            carry[...] = jnp.zeros_like(carry)

        prm = dict(lng=lng_ref[...], lnb=lnb_ref[...], pmat=pmat_ref[...], convw=convw_ref[...])
        zprev = jnp.where(start_of_seq, 0.0, cgp_ref[...] * xbp_ref[...])
        r = _mix_core_forward(proj_ref[...], zprev, prm, da, db, saved=sv_ref)
        og = og_ref[...]
        pmat = prm["pmat"]

        yn = jnp.concatenate([(r["yah"] * og[:, 0:da]).astype(BF16), (r["ybh"] * og[:, da:]).astype(BF16)], axis=1)
        dxo = dxo_ref[...]
        o = _dot(yn, wout_ref[...])
        _acc(dgate_ref.at[0], end_of_seq, jnp.sum(dxo * o, axis=0, keepdims=True))
        d_o = (gate_ref[0] * dxo).astype(BF16)
        do_ref[...] = d_o
        dyn = _dot_nt(d_o, wout_ref[...])

        def rms_bwd(dyn_g, yh, rr, og_g):
            dog_g = jnp.sum(dyn_g * yh, axis=0, keepdims=True)
            dyh = dyn_g * og_g
            return rr * (dyh - yh * jnp.mean(dyh * yh, axis=-1, keepdims=True)), dog_g

        dya, dog_a = rms_bwd(dyn[:, 0:da], r["yah"], r["ra"], og[:, 0:da])
        dyb, dog_b = rms_bwd(dyn[:, da:], r["ybh"], r["rb"], og[:, da:])
        _acc(dog_ref, first, jnp.concatenate([dog_a, dog_b], axis=1))

        dug = dya * r["mixed"]
        dmixed = dya * r["ug"]
        wstt_b = _causal_stack(wstt_ref[...], True).astype(BF16)
        dbias = jnp.zeros((CHUNK, da), F32)
        dwst = jnp.zeros((N_HEADS * CHUNK, CHUNK), F32)
        dvln = []
        for j in range(tm // CHUNK):
            dm = dmixed[j * CHUNK:(j + 1) * CHUNK]
            dbias = dbias + dm
            dmb = dm.astype(BF16)
            dwst = dwst + _mix_heads_grad(dmb, r["vln"][j * CHUNK:(j + 1) * CHUNK], da)
            dvln.append(_mix_heads(wstt_b, dmb, da))
        dvln = dvln[0] if len(dvln) == 1 else jnp.concatenate(dvln, axis=0)
        _acc(dbias_ref, first, dbias)
        _acc(dwst_ref, first, dwst)
        _acc(dlng_ref, first, jnp.sum(dvln * r["vhat"], axis=0, keepdims=True))
        _acc(dlnb_ref, first, jnp.sum(dvln, axis=0, keepdims=True))
        dvhat = dvln * prm["lng"]
        dvg = r["rs"] * (dvhat - _head_mean(dvhat, pmat, exact=False)
                         - r["vhat"] * _head_mean(dvhat * r["vhat"], pmat, exact=False))
        dproj_ref[:, 0:da] = (dug * _gelu_grad(r["ua"], r["ucdf"])).astype(BF16)
        dproj_ref[:, da:2 * da] = (dvg * _gelu_grad(r["va"], r["vcdf"])).astype(BF16)

        dproj_ref[:, 2 * da:2 * da + db] = (dyb * r["conv"]).astype(BF16)
        dconv = dyb * r["bg"]
        dcw = jnp.concatenate([
            jnp.sum(dconv * r["z2"], axis=0, keepdims=True),
            jnp.sum(dconv * r["z1"], axis=0, keepdims=True),
            jnp.sum(dconv * r["z"], axis=0, keepdims=True),
            jnp.zeros((5, db), F32)], axis=0)
        _acc(dconvw_ref, first, dcw)
        nxt = carry[...]
        row = lax.broadcasted_iota(jnp.int32, dconv.shape, 0)
        dc1 = jnp.where(row == tm - 1, nxt[0:1], pltpu.roll(dconv, tm - 1, 0))
        dc2 = jnp.where(row == tm - 2, nxt[0:1], jnp.where(row == tm - 1, nxt[1:2], pltpu.roll(dconv, tm - 2, 0)))
        carry[...] = dconv[0:8]
        cw = prm["convw"]
        dz = dconv * cw[2:3] + dc1 * cw[1:2] + dc2 * cw[0:1]
        dproj_ref[:, 2 * da + db:2 * da + 2 * db] = (dz * r["xb"]).astype(BF16)
        dproj_ref[:, 2 * da + 2 * db:] = (dz * r["cg"]).astype(BF16)

        @pl.when(i == nt - 1)
        def _():
            dwst_ref[...] = _causal_stack(dwst_ref[...], False)
            dbias_ref[...] = _head_mean(dbias_ref[...], pmat) * float(hd)

    full = lambda a: pl.BlockSpec(a.shape, lambda i: (0,) * a.ndim)
    const = lambda i: (0, 0)
    rev = lambda i: (nt - 1 - i, 0)
    prev8 = lambda col: (lambda i: (jnp.maximum((nt - 1 - i) * (tm // 8) - 1, 0), col))
    return _call(
        body, "mix_core_bwd", (nt,),
        [
            pl.BlockSpec((tm, P), rev),
            pl.BlockSpec((8, db), prev8((2 * da + db) // db)),
            pl.BlockSpec((8, db), prev8((2 * da + 2 * db) // db)),
            pl.BlockSpec((5, tm, da), lambda i: (0, nt - 1 - i, 0)),
            pl.BlockSpec((tm, D), rev),
            pl.BlockSpec((1, 1, D), lambda i: ((nt - 1 - i) // tps, 0, 0)),
            full(wout), full(lng), full(lnb), full(wstt), full(pmat), full(convw), full(og),
        ],
        [
            pl.BlockSpec((tm, P), rev),
            pl.BlockSpec((tm, D), rev),
            pl.BlockSpec((1, 1, D), lambda i: ((nt - 1 - i) // tps, 0, 0)),
            pl.BlockSpec((1, D), const),
            pl.BlockSpec((N_HEADS * CHUNK, CHUNK), const),
            pl.BlockSpec((CHUNK, da), const),
            pl.BlockSpec((1, da), const),
            pl.BlockSpec((1, da), const),
            pl.BlockSpec((8, db), const),
        ],
        [
            jax.ShapeDtypeStruct((T, P), BF16),
            jax.ShapeDtypeStruct((T, D), BF16),
            jax.ShapeDtypeStruct((B, 1, D), F32),
            jax.ShapeDtypeStruct((1, D), F32),
            jax.ShapeDtypeStruct((N_HEADS * CHUNK, CHUNK), F32),
            jax.ShapeDtypeStruct((CHUNK, da), F32),
            jax.ShapeDtypeStruct((1, da), F32),
            jax.ShapeDtypeStruct((1, da), F32),
            jax.ShapeDtypeStruct((8, db), F32),
        ],
        [pltpu.VMEM((8, db), F32)],
        (proj, proj, proj, sv, dxo, gate, wout, lng, lnb, wstt, pmat, convw, og), comm)


def _ada_fwd(c_all, ada_w, ada_b):
    n, D = c_all.shape
    L, _, sa = ada_w.shape
    tn = _tile(768, sa)

    def body(c_ref, w_ref, b_ref, act_ref, o_ref):
        c = c_ref[...]
        act = (c * _sigmoid(c)).astype(BF16)
        act_ref[...] = act
        o_ref[...] = _dot(act, w_ref[...].astype(BF16)) + b_ref[...]

    return _call(
        body, "ada_fwd", (L, sa // tn),
        [
            pl.BlockSpec((n, D), lambda l, j: (0, 0)),
            pl.BlockSpec((None, D, tn), lambda l, j: (l, 0, j)),
            pl.BlockSpec((None, 1, tn), lambda l, j: (l, 0, j)),
        ],
        [
            pl.BlockSpec((n, D), lambda l, j: (0, 0)),
            pl.BlockSpec((None, n, tn), lambda l, j: (l, 0, j)),
        ],
        [jax.ShapeDtypeStruct((n, D), BF16), jax.ShapeDtypeStruct((L, n, sa), F32)],
        [],
        (c_all, ada_w, ada_b))[0]


def _ada_bwd(c_act, d_ada, comm=None):
    n, D = c_act.shape
    L, _, sa = d_ada.shape
    tn = _tile(768, sa)

    def body(c_ref, d_ref, o_ref):
        o_ref[...] = _dot_tn(c_ref[...], d_ref[...])

    return _call(
        body, "ada_bwd", (L, sa // tn),
        [pl.BlockSpec((n, D), lambda l, j: (0, 0)), pl.BlockSpec((None, n, tn), lambda l, j: (l, 0, j))],
        [pl.BlockSpec((None, D, tn), lambda l, j: (l, 0, j))],
        [jax.ShapeDtypeStruct((L, D, sa), F32)],
        [],
        (c_act, d_ada), comm)


def _colsum(a):
    L, n, C = a.shape

    def body(a_ref, o_ref):
        o_ref[...] = jnp.sum(a_ref[...], axis=0, keepdims=True)

    return _call(
        body, "colsum", (L,),
        [pl.BlockSpec((None, n, C), lambda l: (l, 0, 0))],
        [pl.BlockSpec((None, 1, C), lambda l: (l, 0, 0))],
        [jax.ShapeDtypeStruct((L, 1, C), F32)],
        [],
        (a,))[0][0]


def _row_tile(rows, cols, nbuf):
    budget = VMEM_LIMIT // 2 // (2 * nbuf * 4 * cols)
    t = rows
    while t > max(budget, 8) and t % 2 == 0 and (t // 2) % 8 == 0:
        t //= 2
    return t


def _pair_sum(g, recv, core):
    n, _, R, C = g.shape
    tr = _row_tile(R, C, 3)

    def body(core_ref, g_ref, r_ref, o_ref):
        o_ref[...] = (g_ref[...] + r_ref[...]).astype(BF16)

    return pl.pallas_call(
        body,
        name="pair_sum",
        grid_spec=pltpu.PrefetchScalarGridSpec(
            num_scalar_prefetch=1,
            grid=(n, R // tr),
            in_specs=[
                pl.BlockSpec((None, None, tr, C), lambda i, r, core_ref: (i, core_ref[0], r, 0)),
                pl.BlockSpec((None, tr, C), lambda i, r, core_ref: (i, r, 0)),
            ],
            out_specs=pl.BlockSpec((None, tr, C), lambda i, r, core_ref: (i, r, 0)),
        ),
        out_shape=jax.ShapeDtypeStruct((n, R, C), BF16),
        compiler_params=pltpu.CompilerParams(dimension_semantics=("arbitrary", "arbitrary"),
                                             vmem_limit_bytes=VMEM_LIMIT),
    )(core, g, recv)


def _chip_sum(q, core, l, n_layers, prev):
    nq, R, C = q.shape
    tr = _row_tile(R, C, 4)

    def body(core_ref, q_ref, *rest):
        o_ref = rest[-1]
        s = q_ref[0].astype(F32)
        for j in range(1, nq):
            s = s + q_ref[j].astype(F32)
        o_ref[...] = s

    in_specs = [pl.BlockSpec((nq, tr, C), lambda r, core_ref: (0, r, 0))]
    args = [core, q]
    aliases = {}
    if prev is not None:
        in_specs.append(ANY)
        args.append(prev)
        aliases = {2: 0}
    return pl.pallas_call(
        body,
        name="chip_sum",
        grid_spec=pltpu.PrefetchScalarGridSpec(
            num_scalar_prefetch=1,
            grid=(R // tr,),
            in_specs=in_specs,
            out_specs=pl.BlockSpec((None, None, tr, C), lambda r, core_ref: (l, core_ref[0], r, 0)),
        ),
        out_shape=jax.ShapeDtypeStruct((n_layers, 2, R, C), F32),
        input_output_aliases=aliases,
        compiler_params=pltpu.CompilerParams(dimension_semantics=("arbitrary",), vmem_limit_bytes=VMEM_LIMIT),
    )(*args)


def _sum_blocks(a, n):
    M = a.shape[0] // n
    C = a.shape[1]

    def body(a_ref, o_ref):
        s = a_ref[0:M]
        for j in range(1, n):
            s = s + a_ref[j * M:(j + 1) * M]
        o_ref[...] = s

    return pl.pallas_call(
        body,
        name="sum_blocks",
        out_shape=jax.ShapeDtypeStruct((M, C), F32),
        compiler_params=pltpu.CompilerParams(vmem_limit_bytes=VMEM_LIMIT),
    )(a)


def _adamw(w, g, m, v, emit_grad=False):
    R, C = w.shape
    n_out = 4 if emit_grad else 3
    tr = _row_tile(R, C, 4 + n_out) if R % 8 == 0 else R

    def body(w_ref, g_ref, m_ref, v_ref, d_ref, nm_ref, nv_ref, *g_out):
        g = g_ref[...]
        m = ADAM_B1 * m_ref[...] + (1.0 - ADAM_B1) * g
        v = ADAM_B2 * v_ref[...] + (1.0 - ADAM_B2) * (g * g)
        m_hat = m / (1.0 - ADAM_B1 ** ADAM_STEP)
        v_hat = v / (1.0 - ADAM_B2 ** ADAM_STEP)
        d_ref[...] = -ADAM_LR * (m_hat / (jnp.sqrt(v_hat) + ADAM_EPS) + ADAM_WD * w_ref[...])
        nm_ref[...] = m
        nv_ref[...] = v
        if emit_grad:
            g_out[0][...] = g

    spec = pl.BlockSpec((tr, C), lambda i: (i, 0))
    return _call(body, "adamw", (R // tr,), [spec] * 4, [spec] * n_out, [jax.ShapeDtypeStruct((R, C), F32)] * n_out,
                 [], (w, g, m, v))[0]


def kernel(x, c, ada_w, ada_b, norm_ffn1_g, ffn1_w_gu, ffn1_w_down, norm_mix_g, mix_w_in, sgu_ln_g, sgu_ln_b, sgu_w_s, sgu_b, conv_w, out_norm_g, mix_w_out, norm_ffn2_g, ffn2_w_gu, ffn2_w_down, final_norm_g, loss_target, m_ada_w, m_ada_b, m_norm_ffn1_g, m_ffn1_w_gu, m_ffn1_w_down, m_norm_mix_g, m_mix_w_in, m_sgu_ln_g, m_sgu_ln_b, m_sgu_w_s, m_sgu_b, m_conv_w, m_out_norm_g, m_mix_w_out, m_norm_ffn2_g, m_ffn2_w_gu, m_ffn2_w_down, m_final_norm_g, v_ada_w, v_ada_b, v_norm_ffn1_g, v_ffn1_w_gu, v_ffn1_w_down, v_norm_mix_g, v_mix_w_in, v_sgu_ln_g, v_sgu_ln_b, v_sgu_w_s, v_sgu_b, v_conv_w, v_out_norm_g, v_mix_w_out, v_norm_ffn2_g, v_ffn2_w_gu, v_ffn2_w_down, v_final_norm_g):
    weights = dict(ada_w=ada_w, ada_b=ada_b, norm_ffn1_g=norm_ffn1_g, ffn1_w_gu=ffn1_w_gu, ffn1_w_down=ffn1_w_down,
                   norm_mix_g=norm_mix_g, mix_w_in=mix_w_in, sgu_ln_g=sgu_ln_g, sgu_ln_b=sgu_ln_b, sgu_w_s=sgu_w_s,
                   sgu_b=sgu_b, conv_w=conv_w, out_norm_g=out_norm_g, mix_w_out=mix_w_out, norm_ffn2_g=norm_ffn2_g,
                   ffn2_w_gu=ffn2_w_gu, ffn2_w_down=ffn2_w_down, final_norm_g=final_norm_g)
    m_in = dict(ada_w=m_ada_w, ada_b=m_ada_b, norm_ffn1_g=m_norm_ffn1_g, ffn1_w_gu=m_ffn1_w_gu,
                ffn1_w_down=m_ffn1_w_down, norm_mix_g=m_norm_mix_g, mix_w_in=m_mix_w_in, sgu_ln_g=m_sgu_ln_g,
                sgu_ln_b=m_sgu_ln_b, sgu_w_s=m_sgu_w_s, sgu_b=m_sgu_b, conv_w=m_conv_w, out_norm_g=m_out_norm_g,
                mix_w_out=m_mix_w_out, norm_ffn2_g=m_norm_ffn2_g, ffn2_w_gu=m_ffn2_w_gu, ffn2_w_down=m_ffn2_w_down,
                final_norm_g=m_final_norm_g)
    v_in = dict(ada_w=v_ada_w, ada_b=v_ada_b, norm_ffn1_g=v_norm_ffn1_g, ffn1_w_gu=v_ffn1_w_gu,
                ffn1_w_down=v_ffn1_w_down, norm_mix_g=v_norm_mix_g, mix_w_in=v_mix_w_in, sgu_ln_g=v_sgu_ln_g,
                sgu_ln_b=v_sgu_ln_b, sgu_w_s=v_sgu_w_s, sgu_b=v_sgu_b, conv_w=v_conv_w, out_norm_g=v_out_norm_g,
                mix_w_out=v_mix_w_out, norm_ffn2_g=v_norm_ffn2_g, ffn2_w_gu=v_ffn2_w_gu, ffn2_w_down=v_ffn2_w_down,
                final_norm_g=v_final_norm_g)

    B, S, D = x.shape
    T = B * S
    L = ada_w.shape[0]
    F = ffn1_w_down.shape[1] * N_CHIP
    P = mix_w_in.shape[2] * N_CHIP
    DA = D // 2
    DB = D - DA
    HD = DA // N_HEADS
    SA = ada_w.shape[2]
    n_all = B * N_DEV
    mx, my, mc = _position()
    chip = 2 * mx + my
    dev = 2 * chip + mc
    core = jnp.reshape(mc, (1,)).astype(jnp.int32)

    big = ["ffn1_w_gu", "ffn1_w_down", "mix_w_in", "mix_w_out", "ffn2_w_gu", "ffn2_w_down"]
    col_sharded = dict(ffn1_w_gu=True, ffn1_w_down=False, mix_w_in=True, mix_w_out=False,
                       ffn2_w_gu=True, ffn2_w_down=False)
    shards = {k: weights[k].astype(BF16) for k in big}
    gather = lambda l, *names: _gather_comm([(shards[k], l, col_sharded[k]) for k in names])
    full = [dict() for _ in range(L)]

    def arrived(l, names, res):
        full[l].update(zip(names, res))

    n_cw = L * conv_w.shape[1]
    cw_block = jnp.pad(conv_w.reshape(n_cw, conv_w.shape[2]), ((0, 8 - n_cw), (0, 0)))
    c_all, cw_all = _comm_call(_merge(_all_gather_comm(c.reshape(8, B * D // 8)), _all_gather_comm(cw_block)),
                               "gather_c")
    c_all = c_all.reshape(n_all, D)
    cw_all = cw_all.reshape(N_CHIP, 2, 8, conv_w.shape[2])[:, 0, :n_cw]
    conv_full = jnp.transpose(cw_all.reshape(N_CHIP, L, conv_w.shape[1], conv_w.shape[2]), (1, 2, 0, 3))
    conv_full = conv_full.reshape(L, conv_w.shape[1], DB)
    ada_b_mine = lax.dynamic_slice_in_dim(ada_b, chip * SA, SA, axis=1).reshape(L, 1, SA)
    c_act, ada_part = _ada_fwd(c_all, ada_w, ada_b_mine)
    ada_all, first_w = _comm_call(_merge(_all_gather_comm(ada_part.reshape(L * n_all, SA)), gather(0, big[0])),
                                  "gather_first")
    arrived(0, big[:1], [first_w])
    ada_all = ada_all.reshape(N_CHIP, 2, L, n_all, SA)[:, 0]
    ada_all = jnp.transpose(ada_all, (1, 2, 0, 3)).reshape(L, n_all, N_CHIP * SA)
    ada = lax.dynamic_slice_in_dim(ada_all, dev * B, B, axis=1).reshape(L, B, N_MOD, 1, D)
    mods = [[ada[l, :, j] for j in range(N_MOD)] for l in range(L)]

    x0 = x.reshape(T, D)
    gains = lambda name, l: weights[name][l].reshape(1, D)
    hmask = jnp.repeat(jnp.eye(N_HEADS, dtype=F32), HD, axis=0)
    pmat = (jnp.repeat(hmask, HD, axis=1) / HD).astype(BF16)

    def mix_consts(l):
        lng = jnp.tile(sgu_ln_g[l], N_HEADS).reshape(1, DA)
        lnb = jnp.tile(sgu_ln_b[l], N_HEADS).reshape(1, DA)
        wst = sgu_w_s[l].reshape(N_HEADS * CHUNK, CHUNK)
        wstt = jnp.swapaxes(sgu_w_s[l], 1, 2).reshape(N_HEADS * CHUNK, CHUNK)
        bias = jnp.repeat(jnp.transpose(sgu_b[l]), HD, axis=1)
        return lng, lnb, wst, wstt, bias

    def fetch(fn, *args, bring=(), **kw):
        bring = [(l, k) for l, k in bring if l < L]
        comm = _gather_comm([(shards[k], l, col_sharded[k]) for l, k in bring]) if bring else None
        res, got = fn(*args, comm, **kw)
        for (l, k), a in zip(bring, got):
            full[l][k] = a
        return res

    saved = []
    xc = x0
    for l in range(L):
        sh1, sc1, g1, sh2, sc2, g2, sh3, sc3, g3 = mods[l]
        lng, lnb, wst, wstt, bias = mix_consts(l)
        w = full[l]
        if l == 0:
            gu1, a1 = fetch(_ffn_up, xc, gains("norm_ffn1_g", l), sh1, sc1, w["ffn1_w_gu"],
                            bring=[(l, "ffn1_w_down"), (l, "mix_w_in"), (l, "mix_w_out")])
            xa, f1 = fetch(_ffn_down, a1, xc, g1, w["ffn1_w_down"], bring=[(l, "ffn2_w_down")])
        else:
            xa, gu1, a1, f1 = fetch(_ffn_fwd, xc, gains("norm_ffn1_g", l), sh1, sc1, g1, w["ffn1_w_gu"],
                                    w["ffn1_w_down"], bring=[(l, "ffn2_w_gu"), (l, "mix_w_in")])
        xb, proj, h2, yn, sv = fetch(_mix_fwd, xa, gains("norm_mix_g", l), sh2, sc2, g2, w["mix_w_in"], w["mix_w_out"],
                                     lng, lnb, wst, bias, pmat, conv_full[l], gains("out_norm_g", l),
                                     bring=[(l, "ffn2_w_gu")] if l == 0 else [(l, "ffn2_w_down")])
        if l + 1 < L:
            xd, gu2, a2, f2 = fetch(_ffn_fwd, xb, gains("norm_ffn2_g", l), sh3, sc3, g3, w["ffn2_w_gu"],
                                    w["ffn2_w_down"],
                                    bring=[(l + 1, "ffn1_w_gu"), (l + 1, "mix_w_out"), (l + 1, "ffn1_w_down")])
        else:
            dx, gu2, a2, f2, loss_block, d_final = fetch(
                _ffn_fwd, xb, gains("norm_ffn2_g", l), sh3, sc3, g3, w["ffn2_w_gu"], w["ffn2_w_down"],
                head=(loss_target.reshape(T, D), final_norm_g.reshape(1, D)))
            xd = None
        saved.append(dict(x0=xc, xa=xa, xb=xb, gu1=gu1, a1=a1, f1=f1, proj=proj, h2=h2, yn=yn, sv=sv,
                          gu2=gu2, a2=a2, f2=f2))
        xc = xd

    reduced = dict.fromkeys(big)

    def halves(name, g):
        if g.ndim == 4:
            return g
        return g.reshape(N_CHIP, 2, weights[name].shape[1] // 2, g.shape[-1])

    class Reduction:
        def __init__(self, l, name, g):
            self.l, self.name, self.g, self.stage = l, name, halves(name, g), 0
            self.ici_bytes = 3 * (g.size // 8) * 2

        def step(self):
            self.stage += 1
            if self.stage == 1:
                return _sibling_half_comm([self.g])
            if self.stage == 2:
                return _scatter_comm([_pair_sum(self.g, self.got[0], core)])
            if self.stage == 3:
                reduced[self.name] = _chip_sum(self.got[0], core, self.l, L, reduced[self.name])
                return _share_comm([reduced[self.name]], self.l)
            reduced[self.name] = self.got[0]
            return None

    active, extra, gathered = [], [], {}

    def carry(fn, *args, us=None):
        left = None if us is None else us * SCATTER_BYTES_PER_US
        riders = []
        for r in active:
            if r.stage == 1 and left is not None:
                if r.ici_bytes > left * SCATTER_OVERSHOOT:
                    continue
                left -= r.ici_bytes
            riders.append(r)
        comms = [r.step() for r in riders] + [cm for cm, _ in extra]
        takers = [functools.partial(setattr, r, "got") for r in riders] + [cb for _, cb in extra]
        extra.clear()
        if fn is None:
            res, got = None, (_comm_call(_merge(*comms), "reduce_alone") if comms else [])
        else:
            res, got = fn(*args, comm=_merge(*comms))
        at = 0
        for cm, take in zip(comms, takers):
            take(got[at:at + len(cm.out_shape)])
            at += len(cm.out_shape)
        for r in riders:
            if r.stage == 3:
                r.step()
                active.remove(r)
        return res

    def reduce_later(l, name, g):
        active.append(Reduction(l, name, g))

    small = [None] * L
    dwsts = [None] * L
    d_ada = [None] * L
    for l in reversed(range(L)):
        sh1, sc1, g1, sh2, sc2, g2, sh3, sc3, g3 = mods[l]
        lng, lnb, wst, wstt, bias = mix_consts(l)
        s = saved[l]
        w = full[l]
        last = l == 0
        dx, dgu2, h3, df2, dsc3, dsh3, dgain3, dg3 = carry(
            _ffn_bwd, dx, s["xb"], s["gu2"], s["f2"], gains("norm_ffn2_g", l), sh3, sc3, g3, w["ffn2_w_gu"],
            w["ffn2_w_down"], us=170)
        ffn2_grads = [
            lambda: reduce_later(l, "ffn2_w_gu", carry(_wgrad, h3, dgu2, D, 2 * F // N_CHIP, True, "wgrad_gu",
                                                       WGRAD_TOKENS // 2, us=110)[0]),
            lambda: reduce_later(l, "ffn2_w_down", carry(_wgrad, s["a2"], df2[None], F // 2, D, False, "wgrad_down",
                                                         us=50)[0])]
        if not last:
            ffn2_grads[0]()
            ffn2_grads[1]()
        dproj, d_o, dg2, dog, dwst, dbias, dlng, dlnb, dconvw = carry(
            _mix_core_bwd, s["proj"], s["sv"], dx, g2, w["mix_w_out"], lng, lnb, wstt, pmat, conv_full[l],
            gains("out_norm_g", l), us=150)
        mix_grads = [
            lambda: reduce_later(l, "mix_w_out", carry(_wgrad, s["yn"], d_o[None], D, D, False, "wgrad_out", us=30)[0]),
            lambda: reduce_later(l, "mix_w_in", carry(_wgrad, s["h2"], dproj[None], D, P // N_CHIP, True, "wgrad_in",
                                                      us=65)[0])]
        if not last:
            mix_grads[0]()
        dx, dsc2, dsh2, dgain2 = carry(_mixin_bwd, dx, s["xa"], dproj, gains("norm_mix_g", l), sc2, w["mix_w_in"], us=60)
        if not last:
            mix_grads[1]()
        dx, dgu, h1, df, dsc1, dsh1, dgain1, dg1 = carry(
            _ffn_bwd, dx, s["x0"], s["gu1"], s["f1"], gains("norm_ffn1_g", l), sh1, sc1, g1, w["ffn1_w_gu"],
            w["ffn1_w_down"], us=170)
        d_ada[l] = jnp.concatenate([dsh1, dsc1, dg1, dsh2, dsc2, dg2, dsh3, dsc3, dg3], axis=1).reshape(B, N_MOD * D)
        small[l] = [dgain1, dgain2, dgain3, dog, dlng, dlnb, dbias[:, ::HD], dconvw]
        dwsts[l] = dwst
        if last:
            flat = [a.reshape(-1, 128) for ll in range(L) for a in small[ll]]
            flat += [d_final.reshape(-1, 128), loss_block[0:1]]
            pad = (-sum(a.shape[0] for a in flat)) % 8
            packed = jnp.concatenate(flat + [jnp.zeros((pad, 128), F32)], axis=0)
            extra.append((_all_gather_comm(jnp.stack(d_ada).reshape(L * B, N_MOD * D)),
                          lambda got: gathered.update(d_ada=got[0])))
            extra.append((_all_gather_comm(packed), lambda got: gathered.update(small=got[0])))
            for ll in range(L):
                extra.append((_all_gather_comm(dwsts[ll]), lambda got, ll=ll: gathered.update({("dwst", ll): got[0]})))
        reduce_later(l, "ffn1_w_gu", carry(_wgrad, h1, dgu, D, 2 * F // N_CHIP, True, "wgrad_gu", WGRAD_TOKENS // 2,
                                           us=110)[0])
        reduce_later(l, "ffn1_w_down", carry(_wgrad, s["a1"], df[None], F // 2, D, False, "wgrad_down", us=50)[0])
        if last:
            ffn2_grads[0]()
            ffn2_grads[1]()
            mix_grads[1]()
            mix_grads[0]()
    grad_x = dx.reshape(B, S, D)

    def finished(name):
        while any(r.name == name for r in active):
            carry(None)
        return reduced[name].reshape(weights[name].shape)

    grads = {}
    d_ada_all = jnp.transpose(gathered["d_ada"].reshape(N_DEV, L, B, N_MOD * D), (1, 0, 2, 3))
    d_ada_all = d_ada_all.reshape(L, n_all, N_MOD * D)
    grads["ada_b"] = _colsum(d_ada_all).reshape(L, N_MOD * D)
    d_ada_mine = lax.dynamic_slice_in_dim(d_ada_all, chip * SA, SA, axis=2).astype(BF16)
    grads["ada_w"] = _ada_bwd(c_act, d_ada_mine)[0][0]

    total = _sum_blocks(gathered["small"].reshape(-1, 128), N_DEV)
    pieces, at = [], 0
    for a in flat:
        pieces.append(total[at:at + a.shape[0]])
        at += a.shape[0]
    per_layer = len(small[0])
    stack = lambda j, shape: jnp.stack([pieces[l * per_layer + j].reshape(shape) for l in range(L)])
    grads["norm_ffn1_g"] = stack(0, (D,))
    grads["norm_mix_g"] = stack(1, (D,))
    grads["norm_ffn2_g"] = stack(2, (D,))
    grads["out_norm_g"] = stack(3, (D,))
    grads["sgu_ln_g"] = stack(4, (N_HEADS, HD)).sum(axis=1)
    grads["sgu_ln_b"] = stack(5, (N_HEADS, HD)).sum(axis=1)
    grads["sgu_b"] = jnp.swapaxes(stack(6, (CHUNK, N_HEADS)), 1, 2)
    g_conv = stack(7, (8, DB))[:, :conv_w.shape[1]]
    grads["conv_w"] = lax.dynamic_slice_in_dim(g_conv, chip * conv_w.shape[2], conv_w.shape[2], axis=2)
    grads["final_norm_g"] = pieces[-2].reshape(D)
    loss = pieces[-1][0, 0]
    grads["sgu_w_s"] = jnp.stack([_sum_blocks(gathered["dwst", l].reshape(-1, CHUNK), N_DEV) for l in range(L)])
    grads["sgu_w_s"] = grads["sgu_w_s"].reshape(L, N_HEADS, CHUNK, CHUNK)

    names = list(weights)
    delta, new_m, new_v = {}, {}, {}
    for k in big:
        grads[k] = finished(k)
    for k in names:
        wk = weights[k]
        view = (1, wk.shape[0]) if wk.ndim == 1 else (-1, wk.shape[-1])
        d, nm, nv, *g_again = _adamw(wk.reshape(view), grads[k].reshape(view), m_in[k].reshape(view),
                                     v_in[k].reshape(view), emit_grad=k in big)
        delta[k], new_m[k], new_v[k] = d.reshape(wk.shape), nm.reshape(wk.shape), nv.reshape(wk.shape)
        if g_again:
            grads[k] = g_again[0].reshape(wk.shape)

    return (loss, grad_x, *[grads[k] for k in names], *[delta[k] for k in names],
            *[new_m[k] for k in names], *[new_v[k] for k in names])
```

```python
import functools
import math

import jax
import jax.numpy as jnp
from jax import lax
from jax.experimental import pallas as pl
from jax.experimental.pallas import tpu as pltpu

F32 = jnp.float32
BF16 = jnp.bfloat16
MESH = pl.DeviceIdType.MESH

N_HEADS = 8
CHUNK = 128
N_MOD = 9
EPS = 1e-6
N_DEV = 8
N_CHIP = 4

ADAM_LR = 0.001
ADAM_B1 = 0.9
ADAM_B2 = 0.999
ADAM_EPS = 1e-08
ADAM_WD = 0.01
ADAM_STEP = 10

TOKEN_TILE = 512
BWD_TILE = 256
FWD_TILE = 512
FF_SLAB = 768
MIX_TILE = 512
MIX_BWD_TILE = 512
WGRAD_TOKENS = 2048
VMEM_LIMIT = 56 * 1024 * 1024

SCATTER_BYTES_PER_US = 68_000
SCATTER_OVERSHOOT = 1.25

ANY = pl.BlockSpec(memory_space=pl.ANY)


def _tile(pref, n):
    t = min(pref, n)
    assert n % t == 0, (pref, n)
    return t


def _slabs(n, width):
    return [slice(c0, min(c0 + width, n)) for c0 in range(0, n, width)]


def _dot(a, b):
    return jnp.dot(a, b, preferred_element_type=F32)


def _dot_nt(a, b):
    return lax.dot_general(a, b, (((1,), (1,)), ((), ())), preferred_element_type=F32)


def _dot_tn(a, b):
    return lax.dot_general(a, b, (((0,), (0,)), ((), ())), preferred_element_type=F32)


def _sigmoid(x):
    return 1.0 / (1.0 + jnp.exp(-x))


def _sigmoid_fast(x):
    return pl.reciprocal(1.0 + jnp.exp(-x), approx=True)


def _rms(x):
    r = lax.rsqrt(jnp.mean(x * x, axis=-1, keepdims=True) + EPS)
    return x * r, r


def _norm_mod_bwd(x, dh, gain, sc):
    xh, r = _rms(x)
    dsc = jnp.sum(dh * (xh * gain), axis=0, keepdims=True)
    dsh = jnp.sum(dh, axis=0, keepdims=True)
    dn = dh * (1.0 + sc)
    dgain = jnp.sum(dn * xh, axis=0, keepdims=True)
    dy = dn * gain
    dx = r * (dy - xh * jnp.mean(dy * xh, axis=-1, keepdims=True))
    return dx, dsc, dsh, dgain


def _acc(ref, first, val):
    @pl.when(first)
    def _():
        ref[...] = val

    @pl.when(jnp.logical_not(first))
    def _():
        ref[...] += val


class _Comm:
    def __init__(self, args, out_shape, scratch, phases, aliases=None):
        self.args, self.out_shape, self.scratch = list(args), list(out_shape), list(scratch)
        self.phases, self.aliases = phases, dict(aliases or {})


def _merge(*comms):
    comms = [c for c in comms if c is not None]
    if len(comms) <= 1:
        return comms[0] if comms else None
    args = [a for c in comms for a in c.args]
    out_shape = [o for c in comms for o in c.out_shape]
    scratch = [s for c in comms for s in c.scratch]
    aliases, ai, oi = {}, 0, 0
    for c in comms:
        aliases.update({ai + i: oi + o for i, o in c.aliases.items()})
        ai += len(c.args)
        oi += len(c.out_shape)

    def phases(ins, outs, sems):
        parts, ai, oi, si = [], 0, 0, 0
        for c in comms:
            parts.append(c.phases(ins[ai:ai + len(c.args)], outs[oi:oi + len(c.out_shape)], sems[si:si + len(c.scratch)]))
            ai, oi, si = ai + len(c.args), oi + len(c.out_shape), si + len(c.scratch)

        def run(k):
            def go():
                for p in parts:
                    if p[k] is not None:
                        p[k]()
            return go
        return run(0), run(1), run(2)

    return _Comm(args, out_shape, scratch, phases, aliases)


def _call(body, name, grid, in_specs, out_specs, out_shape, scratch, args, comm=None):
    n_in, n_out, n_scr = len(in_specs), len(out_specs), len(scratch)
    sem = ("arbitrary",) * len(grid)
    params = pltpu.CompilerParams(dimension_semantics=sem, vmem_limit_bytes=VMEM_LIMIT)
    if comm is None:
        res = pl.pallas_call(body, name=name, grid=grid, in_specs=in_specs, out_specs=out_specs, out_shape=out_shape,
                             scratch_shapes=scratch, compiler_params=params)(*args)
        return list(res), []
    m_in, m_out = len(comm.args), len(comm.out_shape)

    def full(*refs):
        c_in, c_min = refs[:n_in], refs[n_in:n_in + m_in]
        o = n_in + m_in
        c_out, c_mout = refs[o:o + n_out], refs[o + n_out:o + n_out + m_out]
        o += n_out + m_out
        c_scr, c_sem = refs[o:o + n_scr], refs[o + n_scr:]
        start, mid, finish = comm.phases(c_min, c_mout, c_sem)
        ids = [pl.program_id(a) for a in range(len(grid))]
        first = functools.reduce(jnp.logical_and, [i == 0 for i in ids])
        last = functools.reduce(jnp.logical_and, [i == g - 1 for i, g in zip(ids, grid)])
        pl.when(first)(start)
        if mid is not None:
            pl.when(last)(mid)
        body(*c_in, *c_out, *c_scr)
        pl.when(last)(finish)

    res = pl.pallas_call(
        full, name=name, grid=grid,
        in_specs=list(in_specs) + [ANY] * m_in,
        out_specs=list(out_specs) + [ANY] * m_out,
        out_shape=list(out_shape) + comm.out_shape,
        scratch_shapes=list(scratch) + comm.scratch,
        input_output_aliases={n_in + i: n_out + o for i, o in comm.aliases.items()},
        compiler_params=params,
    )(*args, *comm.args)
    return list(res[:n_out]), list(res[n_out:])


def _comm_call(comm, name):
    m_in, m_out = len(comm.args), len(comm.out_shape)

    def body(*refs):
        start, mid, finish = comm.phases(refs[:m_in], refs[m_in:m_in + m_out], refs[m_in + m_out:])
        start()
        if mid is not None:
            mid()
        finish()

    res = pl.pallas_call(
        body, name=name, in_specs=[ANY] * m_in, out_specs=[ANY] * m_out, out_shape=comm.out_shape,
        scratch_shapes=comm.scratch, input_output_aliases=comm.aliases,
    )(*comm.args)
    return list(res)


def _position():
    return lax.axis_index("x"), lax.axis_index("y"), lax.axis_index("c")


def _gather_comm(items):
    n = len(items)
    half = [s.shape[1] // 2 for s, _, _ in items]

    def full_shape(i):
        s, _, col = items[i]
        _, R, C = s.shape
        return jax.ShapeDtypeStruct((R, N_CHIP * C) if col else (N_CHIP * R, C), s.dtype)

    def phases(ins, outs, sems):
        send_sems, recv_sems, local_sems = sems
        x, y, c = _position()

        def region(i, chip, h):
            s, _, col = items[i]
            _, R, C = s.shape
            if col:
                return outs[i].at[pl.ds(h * half[i], half[i]), pl.ds(chip * C, C)]
            return outs[i].at[pl.ds(chip * R + h * half[i], half[i]), :]

        def mine(i, h):
            return ins[i].at[items[i][1], pl.ds(h * half[i], half[i]), :]

        def copies(kx, ky, kc):
            k_me = 2 * kx + ky
            sibling = (kx, ky, 1 - kc)
            chips = [(1 - kx, ky), (kx, 1 - ky), (1 - kx, 1 - ky)]
            local, first, passed, arrive_ici, arrive_d2d = [], [], [], [], []

            def remote(src, dst, s, to):
                return pltpu.make_async_remote_copy(src_ref=src, dst_ref=dst, send_sem=send_sems.at[s],
                                                    recv_sem=recv_sems.at[s], device_id=to, device_id_type=MESH)

            for i in range(n):
                for h in range(2):
                    local.append(pltpu.make_async_copy(mine(i, h), region(i, k_me, h), local_sems.at[2 * i + h]))
                for j, (px, py) in enumerate(chips):
                    s = 6 * i + j
                    first.append(remote(mine(i, kc), region(i, k_me, kc), s, (px, py, kc)))
                    got = region(i, 2 * px + py, kc)
                    arrive_ici.append(remote(got, got, s, (px, py, kc)))
                    passed.append(remote(got, got, s + 3, sibling))
                    other = region(i, 2 * px + py, 1 - kc)
                    arrive_d2d.append(remote(other, other, s + 3, sibling))
            return local, first, passed, arrive_ici, arrive_d2d

        def on_each_device(fn):
            def go():
                for kx in range(2):
                    for ky in range(2):
                        for kc in range(2):
                            pl.when((x == kx) & (y == ky) & (c == kc))(functools.partial(fn, *copies(kx, ky, kc)))
            return go

        def start(local, first, passed, arrive_ici, arrive_d2d):
            for cp in local + first:
                cp.start()

        def mid(local, first, passed, arrive_ici, arrive_d2d):
            for a, p in zip(arrive_ici, passed):
                a.wait_recv()
                p.start()

        def finish(local, first, passed, arrive_ici, arrive_d2d):
            for a in arrive_d2d:
                a.wait_recv()
            for cp in first + passed:
                cp.wait_send()
            for cp in local:
                cp.wait()

        return on_each_device(start), on_each_device(mid), on_each_device(finish)

    scratch = [pltpu.SemaphoreType.DMA((6 * n,)), pltpu.SemaphoreType.DMA((6 * n,)), pltpu.SemaphoreType.DMA((2 * n,))]
    return _Comm([s for s, _, _ in items], [full_shape(i) for i in range(n)], scratch, phases)


def _sibling_half_comm(gs):
    n = len(gs)

    def phases(ins, outs, sems):
        send_sems, recv_sems = sems
        x, y, c = _position()

        def copies():
            return [pltpu.make_async_remote_copy(
                src_ref=ins[i].at[:, 1 - c], dst_ref=outs[i], send_sem=send_sems.at[i], recv_sem=recv_sems.at[i],
                device_id=(x, y, 1 - c), device_id_type=MESH) for i in range(n)]

        def start():
            for cp in copies():
                cp.start()

        def finish():
            for cp in copies():
                cp.wait()

        return start, None, finish

    out_shape = [jax.ShapeDtypeStruct(g.shape[:1] + g.shape[2:], g.dtype) for g in gs]
    return _Comm(gs, out_shape, [pltpu.SemaphoreType.DMA((n,)), pltpu.SemaphoreType.DMA((n,))], phases)


def _scatter_comm(ps):
    n = len(ps)

    def phases(ins, outs, sems):
        send_sems, recv_sems, local_sems = sems
        x, y, c = _position()
        k_me = 2 * x + y
        chips = [(1 - x, y), (x, 1 - y), (1 - x, 1 - y)]

        def copies():
            local = [pltpu.make_async_copy(ins[i].at[k_me], outs[i].at[k_me], local_sems.at[i]) for i in range(n)]
            remote = [pltpu.make_async_remote_copy(
                src_ref=ins[i].at[2 * px + py], dst_ref=outs[i].at[k_me],
                send_sem=send_sems.at[3 * i + j], recv_sem=recv_sems.at[3 * i + j],
                device_id=(px, py, c), device_id_type=MESH) for i in range(n) for j, (px, py) in enumerate(chips)]
            return local, remote

        def start():
            local, remote = copies()
            for cp in local + remote:
                cp.start()

        def finish():
            local, remote = copies()
            for cp in remote + local:
                cp.wait()

        return start, None, finish

    scratch = [pltpu.SemaphoreType.DMA((3 * n,)), pltpu.SemaphoreType.DMA((3 * n,)), pltpu.SemaphoreType.DMA((n,))]
    return _Comm(ps, [jax.ShapeDtypeStruct(p.shape, p.dtype) for p in ps], scratch, phases)


def _share_comm(rs, l):
    n = len(rs)

    def phases(ins, outs, sems):
        send_sems, recv_sems = sems
        x, y, c = _position()

        def copy(i, h):
            return pltpu.make_async_remote_copy(
                src_ref=outs[i].at[l, h], dst_ref=outs[i].at[l, h], send_sem=send_sems.at[i], recv_sem=recv_sems.at[i],
                device_id=(x, y, 1 - c), device_id_type=MESH)

        def start():
            for i in range(n):
                copy(i, c).start()

        def finish():
            for i in range(n):
                copy(i, 1 - c).wait_recv()
            for i in range(n):
                copy(i, c).wait_send()

        return start, None, finish

    return _Comm(rs, [jax.ShapeDtypeStruct(r.shape, r.dtype) for r in rs],
                 [pltpu.SemaphoreType.DMA((n,)), pltpu.SemaphoreType.DMA((n,))], phases,
                 aliases={i: i for i in range(n)})


def _all_gather_comm(block):
    def phases(ins, outs, sems):
        send_sems, recv_sems, local_sem = sems
        (src,), (out,) = ins, outs
        x, y, c = _position()
        sibling = (x, y, 1 - c)
        chips = [(1 - x, y), (x, 1 - y), (1 - x, 1 - y)]

        def slot(px, py, pc):
            return out.at[4 * px + 2 * py + pc]

        def copy(k, blk, to, own=False):
            return pltpu.make_async_remote_copy(
                src_ref=src if own else slot(*blk), dst_ref=slot(*blk),
                send_sem=send_sems.at[k], recv_sem=recv_sems.at[k], device_id=to, device_id_type=MESH)

        mine = lambda: pltpu.make_async_copy(src, slot(x, y, c), local_sem.at[0])
        first = lambda: [copy(0, (x, y, c), sibling, True)] + [
            copy(1 + j, (x, y, c), (*chip, c), True) for j, chip in enumerate(chips)]
        passed = lambda: [copy(4 + j, (*chip, c), sibling) for j, chip in enumerate(chips)]

        def start():
            mine().start()
            for cp in first():
                cp.start()

        def mid():
            for j, (chip, p) in enumerate(zip(chips, passed())):
                copy(1 + j, (*chip, c), (x, y, c)).wait_recv()
                p.start()

        def finish():
            copy(0, sibling, (x, y, c)).wait_recv()
            for j, chip in enumerate(chips):
                copy(4 + j, (*chip, 1 - c), (x, y, c)).wait_recv()
            for cp in first() + passed():
                cp.wait_send()
            mine().wait()

        return start, mid, finish

    scratch = [pltpu.SemaphoreType.DMA((7,)), pltpu.SemaphoreType.DMA((7,)), pltpu.SemaphoreType.DMA((1,))]
    return _Comm([block], [jax.ShapeDtypeStruct((N_DEV,) + block.shape, block.dtype)], scratch, phases)


def _ffn_up(x, gain, sh, sc, wgu, comm=None):
    T, D = x.shape
    F = wgu.shape[1] // 2
    B = sh.shape[0]
    tm = _tile(TOKEN_TILE, T // B)
    tps = (T // B) // tm
    slabs = _slabs(F, FF_SLAB)

    def body(x_ref, gain_ref, sh_ref, sc_ref, w_ref, gu_ref, a_ref):
        xh, _ = _rms(x_ref[...])
        h = (xh * gain_ref[...] * (1.0 + sc_ref[0]) + sh_ref[0]).astype(BF16)

        def dots(s):
            return _dot(h, w_ref[:, s]), _dot(h, w_ref[:, slice(F + s.start, F + s.stop)])

        nxt = dots(slabs[0])
        for j, s in enumerate(slabs):
            g, u = nxt
            if j + 1 < len(slabs):
                nxt = dots(slabs[j + 1])
            gu_ref[0, :, s] = g.astype(BF16)
            gu_ref[1, :, s] = u.astype(BF16)
            a_ref[:, s] = (g * _sigmoid(g) * u).astype(BF16)

    seq = lambda i: (i // tps, 0, 0)
    return _call(
        body, "ffn_up", (T // tm,),
        [
            pl.BlockSpec((tm, D), lambda i: (i, 0)),
            pl.BlockSpec((1, D), lambda i: (0, 0)),
            pl.BlockSpec((1, 1, D), seq),
            pl.BlockSpec((1, 1, D), seq),
            pl.BlockSpec((D, 2 * F), lambda i: (0, 0), pipeline_mode=pl.Buffered(1)),
        ],
        [
            pl.BlockSpec((2, tm, F), lambda i: (0, i, 0)),
            pl.BlockSpec((tm, F), lambda i: (i, 0)),
        ],
        [
            jax.ShapeDtypeStruct((2, T, F), BF16),
            jax.ShapeDtypeStruct((T, F), BF16),
        ],
        [],
        (x, gain, sh, sc, wgu), comm)


def _ffn_down(a, x, gate, wd, comm=None):
    T, F = a.shape
    D = x.shape[1]
    B = gate.shape[0]
    tm = _tile(2 * TOKEN_TILE, T // B)
    tps = (T // B) // tm

    def body(a_ref, x_ref, gate_ref, wd_ref, xo_ref, f_ref):
        f = _dot(a_ref[...], wd_ref[...])
        f_ref[...] = f.astype(BF16)
        xo_ref[...] = x_ref[...] + 0.5 * gate_ref[0] * f

    return _call(
        body, "ffn_down", (T // tm,),
        [
            pl.BlockSpec((tm, F), lambda i: (i, 0)),
            pl.BlockSpec((tm, D), lambda i: (i, 0)),
            pl.BlockSpec((1, 1, D), lambda i: (i // tps, 0, 0)),
            pl.BlockSpec((F, D), lambda i: (0, 0), pipeline_mode=pl.Buffered(1)),
        ],
        [pl.BlockSpec((tm, D), lambda i: (i, 0)), pl.BlockSpec((tm, D), lambda i: (i, 0))],
        [jax.ShapeDtypeStruct((T, D), F32), jax.ShapeDtypeStruct((T, D), BF16)],
        [],
        (a, x, gate, wd), comm)


def _ffn_fwd(x, gain, sh, sc, gate, wgu, wd, comm=None, head=None):
    T, D = x.shape
    F = wd.shape[0]
    B = sh.shape[0]
    tm = _tile(FWD_TILE, T // B)
    tps = (T // B) // tm
    slabs = _slabs(F, FF_SLAB)

    def body(x_ref, gain_ref, sh_ref, sc_ref, gate_ref, w_ref, wd_ref, *rest):
        if head is None:
            xo_ref, gu_ref, a_ref, f_ref = rest
        else:
            t_ref, fgain_ref, xo_ref, gu_ref, a_ref, f_ref, loss_ref, dfgain_ref = rest
        x = x_ref[...]
        h = (_rms(x)[0] * gain_ref[...] * (1.0 + sc_ref[0]) + sh_ref[0]).astype(BF16)

        def dots(s):
            return _dot(h, w_ref[:, s]), _dot(h, w_ref[:, slice(F + s.start, F + s.stop)])

        nxt = dots(slabs[0])
        for j, s in enumerate(slabs):
            g, u = nxt
            if j + 1 < len(slabs):
                nxt = dots(slabs[j + 1])
            gu_ref[0, :, s] = g.astype(BF16)
            gu_ref[1, :, s] = u.astype(BF16)
            a_ref[:, s] = (g * _sigmoid(g) * u).astype(BF16)
        f = _dot(a_ref[...], wd_ref[...])
        f_ref[...] = f.astype(BF16)
        xo = x + 0.5 * gate_ref[0] * f
        if head is None:
            xo_ref[...] = xo
        else:
            first = pl.program_id(0) == 0
            xh, r = _rms(xo)
            fgain = fgain_ref[...]
            err = xh * fgain - t_ref[...]
            _acc(loss_ref, first, jnp.zeros((8, 128), F32) + 0.5 * jnp.sum(err * err) / D)
            dout = err * (1.0 / D)
            _acc(dfgain_ref, first, jnp.sum(dout * xh, axis=0, keepdims=True))
            dy = dout * fgain
            xo_ref[...] = r * (dy - xh * jnp.mean(dy * xh, axis=-1, keepdims=True))

    seq = lambda i: (i // tps, 0, 0)
    row = lambda i: (i, 0)
    const = lambda i: (0, 0)
    in_specs = [
        pl.BlockSpec((tm, D), row),
        pl.BlockSpec((1, D), const),
        pl.BlockSpec((1, 1, D), seq),
        pl.BlockSpec((1, 1, D), seq),
        pl.BlockSpec((1, 1, D), seq),
        pl.BlockSpec((D, 2 * F), const, pipeline_mode=pl.Buffered(1)),
        pl.BlockSpec((F, D), const, pipeline_mode=pl.Buffered(1)),
    ]
    out_specs = [
        pl.BlockSpec((tm, D), row),
        pl.BlockSpec((2, tm, F), lambda i: (0, i, 0)),
        pl.BlockSpec((tm, F), row),
        pl.BlockSpec((tm, D), row),
    ]
    out_shape = [
        jax.ShapeDtypeStruct((T, D), F32),
        jax.ShapeDtypeStruct((2, T, F), BF16),
        jax.ShapeDtypeStruct((T, F), BF16),
        jax.ShapeDtypeStruct((T, D), BF16),
    ]
    args = (x, gain, sh, sc, gate, wgu, wd)
    if head is not None:
        in_specs += [pl.BlockSpec((tm, D), row), pl.BlockSpec((1, D), const)]
        out_specs += [pl.BlockSpec((8, 128), const), pl.BlockSpec((1, D), const)]
        out_shape += [jax.ShapeDtypeStruct((8, 128), F32), jax.ShapeDtypeStruct((1, D), F32)]
        args += tuple(head)
    return _call(body, "ffn_fwd", (T // tm,), in_specs, out_specs, out_shape, [], args, comm)


def _ffn_bwd(dxo, x, gu, f, gain, sh, sc, gate, wgu, wd, comm=None):
    T, D = x.shape
    F = wd.shape[0]
    B = sc.shape[0]
    tm = _tile(BWD_TILE, T // B)
    tps = (T // B) // tm
    slabs = _slabs(F, FF_SLAB)

    def body(dxo_ref, x_ref, gu_ref, f_ref, gain_ref, sh_ref, sc_ref, gate_ref, w_ref, wd_ref,
             dx_ref, dgu_ref, h_ref, df_ref, dsc_ref, dsh_ref, dgain_ref, dgate_ref):
        i = pl.program_id(0)
        first_of_seq = (i % tps) == 0
        gain = gain_ref[...]
        sc = sc_ref[0]
        dxo = dxo_ref[...]
        x = x_ref[...]
        df = (0.5 * gate_ref[0] * dxo).astype(BF16)
        df_ref[...] = df
        nxt = _dot_nt(df, wd_ref[slabs[0], :])
        for j, s in enumerate(slabs):
            da = nxt
            if j + 1 < len(slabs):
                nxt = _dot_nt(df, wd_ref[slabs[j + 1], :])
            g = gu_ref[0, :, s]
            sg = 1.0 / (1.0 + jnp.exp(-g))
            t = g * sg
            dab = da.astype(BF16)
            dgu_ref[1, :, s] = dab * t
            dgu_ref[0, :, s] = dab * gu_ref[1, :, s] * (sg + t - t * sg)
        dh = _dot_nt(dgu_ref[0], w_ref[:, 0:F]) + _dot_nt(dgu_ref[1], w_ref[:, F:])
        dx, dsc, dsh, dgain = _norm_mod_bwd(x, dh, gain, sc)
        dx_ref[...] = dxo + dx
        h_ref[...] = (_rms(x)[0] * gain * (1.0 + sc) + sh_ref[0]).astype(BF16)
        _acc(dsc_ref.at[0], first_of_seq, dsc)
        _acc(dsh_ref.at[0], first_of_seq, dsh)
        _acc(dgain_ref, i == 0, dgain)
        _acc(dgate_ref.at[0], first_of_seq, 0.5 * jnp.sum(dxo * f_ref[...].astype(F32), axis=0, keepdims=True))

    seq = lambda i: (i // tps, 0, 0)
    row = lambda i: (i, 0)
    return _call(
        body, "ffn_bwd", (T // tm,),
        [
            pl.BlockSpec((tm, D), row),
            pl.BlockSpec((tm, D), row),
            pl.BlockSpec((2, tm, F), lambda i: (0, i, 0)),
            pl.BlockSpec((tm, D), row),
            pl.BlockSpec((1, D), lambda i: (0, 0)),
            pl.BlockSpec((1, 1, D), seq),
            pl.BlockSpec((1, 1, D), seq),
            pl.BlockSpec((1, 1, D), seq),
            pl.BlockSpec((D, 2 * F), lambda i: (0, 0), pipeline_mode=pl.Buffered(1)),
            pl.BlockSpec((F, D), lambda i: (0, 0), pipeline_mode=pl.Buffered(1)),
        ],
        [
            pl.BlockSpec((tm, D), row),
            pl.BlockSpec((2, tm, F), lambda i: (0, i, 0)),
            pl.BlockSpec((tm, D), row),
            pl.BlockSpec((tm, D), row),
            pl.BlockSpec((1, 1, D), seq),
            pl.BlockSpec((1, 1, D), seq),
            pl.BlockSpec((1, D), lambda i: (0, 0)),
            pl.BlockSpec((1, 1, D), seq),
        ],
        [
            jax.ShapeDtypeStruct((T, D), F32),
            jax.ShapeDtypeStruct((2, T, F), BF16),
            jax.ShapeDtypeStruct((T, D), BF16),
            jax.ShapeDtypeStruct((T, D), BF16),
            jax.ShapeDtypeStruct((B, 1, D), F32),
            jax.ShapeDtypeStruct((B, 1, D), F32),
            jax.ShapeDtypeStruct((1, D), F32),
            jax.ShapeDtypeStruct((B, 1, D), F32),
        ],
        [],
        (dxo, x, gu, f, gain, sh, sc, gate, wgu, wd), comm)


def _wgrad(a, b, tmm, tn, col_major, name, tokens=WGRAD_TOKENS, comm=None):
    T, M = a.shape
    nb, _, Nb = b.shape
    N = nb * Nb
    tk = _tile(tokens, T)
    span = 2 if col_major else 1
    wide = span * tn
    npb = Nb // wide
    assert M % tmm == 0 and Nb % wide == 0
    if col_major:
        assert tmm == M
        shape = (N // tn, 2, M // 2, tn)
        out_spec = pl.BlockSpec((span, 2, M // 2, tn), lambda i, j, t: (j, 0, 0, 0))
    else:
        shape = (M // tmm, tmm, N)
        out_spec = pl.BlockSpec((None, tmm, tn), lambda i, j, t: (i, 0, j))

    def body(a_ref, b_ref, o_ref):
        @pl.when(pl.program_id(2) == 0)
        def _():
            o_ref[...] = jnp.zeros_like(o_ref)

        res = _dot_tn(a_ref[...], b_ref[...])
        if col_major:
            for s in range(span):
                for h in range(2):
                    o_ref[s, h] += res[h * (M // 2):(h + 1) * (M // 2), s * tn:(s + 1) * tn]
        else:
            o_ref[...] += res

    return _call(
        body, name, (M // tmm, N // wide, T // tk),
        [
            pl.BlockSpec((tk, tmm), lambda i, j, t: (t, i)),
            pl.BlockSpec((None, tk, wide), lambda i, j, t: (j // npb, t, j % npb)),
        ],
        [out_spec], [jax.ShapeDtypeStruct(shape, F32)], [],
        (a, b), comm)


def _mixin_bwd(dxo, x, dproj, gain, sc, win, comm=None):
    T, D = x.shape
    P = win.shape[1]
    B = sc.shape[0]
    tm = _tile(TOKEN_TILE, T // B)
    tps = (T // B) // tm

    def body(dxo_ref, x_ref, dp_ref, gain_ref, sc_ref, w_ref, dx_ref, dsc_ref, dsh_ref, dgain_ref):
        i = pl.program_id(0)
        first_of_seq = (i % tps) == 0
        halves = _slabs(tm, tm // 2)
        nxt = _dot_nt(dp_ref[halves[0], :], w_ref[...])
        sums = None
        for j, r in enumerate(halves):
            dh = nxt
            if j + 1 < len(halves):
                nxt = _dot_nt(dp_ref[halves[j + 1], :], w_ref[...])
            part = _norm_mod_bwd(x_ref[r, :], dh, gain_ref[...], sc_ref[0])
            dx_ref[r, :] = dxo_ref[r, :] + part[0]
            sums = part[1:] if sums is None else tuple(a + b for a, b in zip(sums, part[1:]))
        _acc(dsc_ref.at[0], first_of_seq, sums[0])
        _acc(dsh_ref.at[0], first_of_seq, sums[1])
        _acc(dgain_ref, i == 0, sums[2])

    seq = lambda i: (i // tps, 0, 0)
    row = lambda i: (i, 0)
    return _call(
        body, "mixin_bwd", (T // tm,),
        [
            pl.BlockSpec((tm, D), row),
            pl.BlockSpec((tm, D), row),
            pl.BlockSpec((tm, P), row),
            pl.BlockSpec((1, D), lambda i: (0, 0)),
            pl.BlockSpec((1, 1, D), seq),
            pl.BlockSpec((D, P), lambda i: (0, 0)),
        ],
        [
            pl.BlockSpec((tm, D), row),
            pl.BlockSpec((1, 1, D), seq),
            pl.BlockSpec((1, 1, D), seq),
            pl.BlockSpec((1, D), lambda i: (0, 0)),
        ],
        [
            jax.ShapeDtypeStruct((T, D), F32),
            jax.ShapeDtypeStruct((B, 1, D), F32),
            jax.ShapeDtypeStruct((B, 1, D), F32),
            jax.ShapeDtypeStruct((1, D), F32),
        ],
        [],
        (dxo, x, dproj, gain, sc, win), comm)


def _head_mean(z, pmat, exact=True):
    hi = z.astype(BF16)
    if not exact:
        return _dot(hi, pmat)
    lo = (z - hi.astype(F32)).astype(BF16)
    return _dot(hi, pmat) + _dot(lo, pmat)


def _gelu_parts(x):
    cdf = 0.5 * (1.0 + lax.erf(x * (1.0 / math.sqrt(2.0))))
    return x * cdf, cdf


def _gelu_grad(x, cdf):
    return cdf + x * jnp.exp(-0.5 * x * x) * (1.0 / math.sqrt(2.0 * math.pi))


LANES = 128


def _head_blocks(da):
    hd = da // N_HEADS
    lb = min(LANES, da)
    col = lax.broadcasted_iota(jnp.int32, (1, lb), 1)
    return lb, lb // hd, da // lb, [(col >= h * hd) & (col < (h + 1) * hd) for h in range(lb // hd)]


def _mix_heads(w_stack, v, da):
    lb, hpb, nb, masks = _head_blocks(da)
    outs = []
    for b in range(nb):
        res = _dot(w_stack[b * hpb * CHUNK:(b + 1) * hpb * CHUNK], v[:, b * lb:(b + 1) * lb])
        out = res[0:CHUNK]
        for h in range(1, hpb):
            out = jnp.where(masks[h], res[h * CHUNK:(h + 1) * CHUNK], out)
        outs.append(out)
    return outs[0] if nb == 1 else jnp.concatenate(outs, axis=1)


def _mix_heads_grad(dm, v, da):
    lb, hpb, nb, masks = _head_blocks(da)
    outs = []
    for b in range(nb):
        dmb = dm[:, b * lb:(b + 1) * lb]
        stack = jnp.concatenate([jnp.where(masks[h], dmb, jnp.zeros_like(dmb)) for h in range(hpb)], axis=0)
        outs.append(_dot_nt(stack, v[:, b * lb:(b + 1) * lb]))
    return outs[0] if nb == 1 else jnp.concatenate(outs, axis=0)


def _causal_stack(w, transposed):
    r = lax.broadcasted_iota(jnp.int32, w.shape, 0) % CHUNK
    c = lax.broadcasted_iota(jnp.int32, w.shape, 1)
    keep = (c >= r) if transposed else (c <= r)
    return jnp.where(keep, w, 0.0)


def _mix_core_forward(proj, zprev, prm, da, db, saved=None):
    n = proj.shape[0]
    ua = proj[:, 0:da]
    va = proj[:, da:2 * da]
    bg = proj[:, 2 * da:2 * da + db]
    cg = proj[:, 2 * da + db:2 * da + 2 * db]
    xb = proj[:, 2 * da + 2 * db:]
    if saved is None:
        ug, ucdf = _gelu_parts(ua)
        vg, vcdf = _gelu_parts(va)
        zc = vg - _head_mean(vg, prm["pmat"])
        rs = lax.rsqrt(_head_mean(zc * zc, prm["pmat"], exact=False) + EPS)
        vhat = zc * rs
        vln = (vhat * prm["lng"] + prm["lnb"]).astype(BF16)
        wst = _causal_stack(prm["wst"], False).astype(BF16)
        mixed = [_mix_heads(wst, vln[j * CHUNK:(j + 1) * CHUNK], da) + prm["bias"] for j in range(n // CHUNK)]
        mixed = mixed[0] if len(mixed) == 1 else jnp.concatenate(mixed, axis=0)
    else:
        ucdf, vcdf, vhat, rs, mixed = [saved[k].astype(F32) for k in range(5)]
        ug = ua * ucdf
        vln = (vhat * prm["lng"] + prm["lnb"]).astype(BF16)
    ya = ug * mixed
    z = cg * xb
    row = lax.broadcasted_iota(jnp.int32, z.shape, 0)
    z1 = jnp.where(row == 0, zprev[7:8], pltpu.roll(z, 1, 0))
    z2 = jnp.where(row == 0, zprev[6:7], jnp.where(row == 1, zprev[7:8], pltpu.roll(z, 2, 0)))
    cw = prm["convw"]
    conv = z2 * cw[0:1] + z1 * cw[1:2] + z * cw[2:3]
    yb = bg * conv
    yah, ra = _rms(ya)
    ybh, rb = _rms(yb)
    return dict(ua=ua, va=va, bg=bg, cg=cg, xb=xb, ug=ug, ucdf=ucdf, vcdf=vcdf, rs=rs, vhat=vhat, vln=vln,
                mixed=mixed, z=z, z1=z1, z2=z2, conv=conv, yah=yah, ra=ra, ybh=ybh, rb=rb)


def _mix_params(lng_ref, lnb_ref, wst_ref, bias_ref, pmat_ref, convw_ref):
    return dict(lng=lng_ref[...], lnb=lnb_ref[...], wst=wst_ref[...], bias=bias_ref[...],
                pmat=pmat_ref[...], convw=convw_ref[...])


def _mix_fwd(x, gain, sh, sc, gate, win, wout, lng, lnb, wst, bias, pmat, convw, og, comm=None):
    T, D = x.shape
    P = win.shape[1]
    B = gate.shape[0]
    da = lng.shape[1]
    db = convw.shape[1]
    tm = _tile(MIX_TILE, T // B)
    tps = (T // B) // tm

    def body(x_ref, gain_ref, sh_ref, sc_ref, gate_ref, win_ref, wout_ref, lng_ref, lnb_ref, wst_ref, bias_ref,
             pmat_ref, convw_ref, og_ref, xo_ref, proj_ref, h_ref, yn_ref, sv_ref, halo):
        i = pl.program_id(0)

        @pl.when((i % tps) == 0)
        def _():
            halo[...] = jnp.zeros_like(halo)

        h = (_rms(x_ref[...])[0] * gain_ref[...] * (1.0 + sc_ref[0]) + sh_ref[0]).astype(BF16)
        h_ref[...] = h
        proj_ref[...] = _dot(h, win_ref[...])
        prm = _mix_params(lng_ref, lnb_ref, wst_ref, bias_ref, pmat_ref, convw_ref)
        r = _mix_core_forward(proj_ref[...], halo[...], prm, da, db)
        halo[...] = r["z"][tm - 8:tm]
        for k, name in enumerate(("ucdf", "vcdf", "vhat", "rs", "mixed")):
            sv_ref[k] = r[name].astype(BF16)
        og = og_ref[...]
        yn_ref[:, 0:da] = (r["yah"] * og[:, 0:da]).astype(BF16)
        yn_ref[:, da:] = (r["ybh"] * og[:, da:]).astype(BF16)
        xo_ref[...] = x_ref[...] + gate_ref[0] * _dot(yn_ref[...], wout_ref[...])

    full = lambda a: pl.BlockSpec(a.shape, lambda i: (0,) * a.ndim)
    seq = lambda i: (i // tps, 0, 0)
    row = lambda i: (i, 0)
    return _call(
        body, "mix_fwd", (T // tm,),
        [
            pl.BlockSpec((tm, D), row),
            pl.BlockSpec((1, D), lambda i: (0, 0)),
            pl.BlockSpec((1, 1, D), seq),
            pl.BlockSpec((1, 1, D), seq),
            pl.BlockSpec((1, 1, D), seq),
            pl.BlockSpec((D, P), lambda i: (0, 0), pipeline_mode=pl.Buffered(1)),
            full(wout), full(lng), full(lnb), full(wst), full(bias), full(pmat), full(convw), full(og),
        ],
        [pl.BlockSpec((tm, D), row), pl.BlockSpec((tm, P), row), pl.BlockSpec((tm, D), row),
         pl.BlockSpec((tm, D), row), pl.BlockSpec((5, tm, da), lambda i: (0, i, 0))],
        [jax.ShapeDtypeStruct((T, D), F32), jax.ShapeDtypeStruct((T, P), F32), jax.ShapeDtypeStruct((T, D), BF16),
         jax.ShapeDtypeStruct((T, D), BF16), jax.ShapeDtypeStruct((5, T, da), BF16)],
        [pltpu.VMEM((8, db), F32)],
        (x, gain, sh, sc, gate, win, wout, lng, lnb, wst, bias, pmat, convw, og), comm)


def _mix_core_bwd(proj, sv, dxo, gate, wout, lng, lnb, wstt, pmat, convw, og, comm=None):
    T, P = proj.shape
    D = dxo.shape[1]
    B = gate.shape[0]
    da = lng.shape[1]
    db = convw.shape[1]
    assert da == db and P == 2 * da + 3 * db
    tm = _tile(MIX_BWD_TILE, T // B)
    tps = (T // B) // tm
    nt = T // tm
    hd = da // N_HEADS

    def body(proj_ref, cgp_ref, xbp_ref, sv_ref, dxo_ref, gate_ref, wout_ref, lng_ref, lnb_ref, wstt_ref,
             pmat_ref, convw_ref, og_ref,
             dproj_ref, do_ref, dgate_ref, dog_ref, dwst_ref, dbias_ref, dlng_ref, dlnb_ref, dconvw_ref, carry):
        i = pl.program_id(0)
        ri = nt - 1 - i
        first = i == 0
        end_of_seq = (ri % tps) == tps - 1
        start_of_seq = (ri % tps) == 0

        @pl.when(end_of_seq)
        def _():
            carry[...] = jnp.zeros_like(carry)

        prm = dict(lng=lng_ref[...], lnb=lnb_ref[...], pmat=pmat_ref[...], convw=convw_ref[...])
        zprev = jnp.where(start_of_seq, 0.0, cgp_ref[...] * xbp_ref[...])
        r = _mix_core_forward(proj_ref[...], zprev, prm, da, db, saved=sv_ref)
        og = og_ref[...]
        pmat = prm["pmat"]

        yn = jnp.concatenate([(r["yah"] * og[:, 0:da]).astype(BF16), (r["ybh"] * og[:, da:]).astype(BF16)], axis=1)
        dxo = dxo_ref[...]
        o = _dot(yn, wout_ref[...])
        _acc(dgate_ref.at[0], end_of_seq, jnp.sum(dxo * o, axis=0, keepdims=True))
        d_o = (gate_ref[0] * dxo).astype(BF16)
        do_ref[...] = d_o
        dyn = _dot_nt(d_o, wout_ref[...])

        def rms_bwd(dyn_g, yh, rr, og_g):
            dog_g = jnp.sum(dyn_g * yh, axis=0, keepdims=True)
            dyh = dyn_g * og_g
            return rr * (dyh - yh * jnp.mean(dyh * yh, axis=-1, keepdims=True)), dog_g

        dya, dog_a = rms_bwd(dyn[:, 0:da], r["yah"], r["ra"], og[:, 0:da])
        dyb, dog_b = rms_bwd(dyn[:, da:], r["ybh"], r["rb"], og[:, da:])
        _acc(dog_ref, first, jnp.concatenate([dog_a, dog_b], axis=1))

        dug = dya * r["mixed"]
        dmixed = dya * r["ug"]
        wstt_b = _causal_stack(wstt_ref[...], True).astype(BF16)
        dbias = jnp.zeros((CHUNK, da), F32)
        dwst = jnp.zeros((N_HEADS * CHUNK, CHUNK), F32)
        dvln = []
        for j in range(tm // CHUNK):
            dm = dmixed[j * CHUNK:(j + 1) * CHUNK]
            dbias = dbias + dm
            dmb = dm.astype(BF16)
            dwst = dwst + _mix_heads_grad(dmb, r["vln"][j * CHUNK:(j + 1) * CHUNK], da)
            dvln.append(_mix_heads(wstt_b, dmb, da))
        dvln = dvln[0] if len(dvln) == 1 else jnp.concatenate(dvln, axis=0)
        _acc(dbias_ref, first, dbias)
        _acc(dwst_ref, first, dwst)
        _acc(dlng_ref, first, jnp.sum(dvln * r["vhat"], axis=0, keepdims=True))
        _acc(dlnb_ref, first, jnp.sum(dvln, axis=0, keepdims=True))
        dvhat = dvln * prm["lng"]
        dvg = r["rs"] * (dvhat - _head_mean(dvhat, pmat, exact=False)
                         - r["vhat"] * _head_mean(dvhat * r["vhat"], pmat, exact=False))
        dproj_ref[:, 0:da] = (dug * _gelu_grad(r["ua"], r["ucdf"])).astype(BF16)
        dproj_ref[:, da:2 * da] = (dvg * _gelu_grad(r["va"], r["vcdf"])).astype(BF16)

        dproj_ref[:, 2 * da:2 * da + db] = (dyb * r["conv"]).astype(BF16)
        dconv = dyb * r["bg"]
        dcw = jnp.concatenate([
            jnp.sum(dconv * r["z2"], axis=0, keepdims=True),
            jnp.sum(dconv * r["z1"], axis=0, keepdims=True),
            jnp.sum(dconv * r["z"], axis=0, keepdims=True),
            jnp.zeros((5, db), F32)], axis=0)
        _acc(dconvw_ref, first, dcw)
        nxt = carry[...]
        row = lax.broadcasted_iota(jnp.int32, dconv.shape, 0)
        dc1 = jnp.where(row == tm - 1, nxt[0:1], pltpu.roll(dconv, tm - 1, 0))
        dc2 = jnp.where(row == tm - 2, nxt[0:1], jnp.where(row == tm - 1, nxt[1:2], pltpu.roll(dconv, tm - 2, 0)))
        carry[...] = dconv[0:8]
        cw = prm["convw"]
        dz = dconv * cw[2:3] + dc1 * cw[1:2] + dc2 * cw[0:1]
        dproj_ref[:, 2 * da + db:2 * da + 2 * db] = (dz * r["xb"]).astype(BF16)
        dproj_ref[:, 2 * da + 2 * db:] = (dz * r["cg"]).astype(BF16)

        @pl.when(i == nt - 1)
        def _():
            dwst_ref[...] = _causal_stack(dwst_ref[...], False)
            dbias_ref[...] = _head_mean(dbias_ref[...], pmat) * float(hd)

    full = lambda a: pl.BlockSpec(a.shape, lambda i: (0,) * a.ndim)
    const = lambda i: (0, 0)
    rev = lambda i: (nt - 1 - i, 0)
    prev8 = lambda col: (lambda i: (jnp.maximum((nt - 1 - i) * (tm // 8) - 1, 0), col))
    return _call(
        body, "mix_core_bwd", (nt,),
        [
            pl.BlockSpec((tm, P), rev),
            pl.BlockSpec((8, db), prev8((2 * da + db) // db)),
            pl.BlockSpec((8, db), prev8((2 * da + 2 * db) // db)),
            pl.BlockSpec((5, tm, da), lambda i: (0, nt - 1 - i, 0)),
            pl.BlockSpec((tm, D), rev),
            pl.BlockSpec((1, 1, D), lambda i: ((nt - 1 - i) // tps, 0, 0)),
            full(wout), full(lng), full(lnb), full(wstt), full(pmat), full(convw), full(og),
        ],
        [
            pl.BlockSpec((tm, P), rev),
            pl.BlockSpec((tm, D), rev),
            pl.BlockSpec((1, 1, D), lambda i: ((nt - 1 - i) // tps, 0, 0)),
            pl.BlockSpec((1, D), const),
            pl.BlockSpec((N_HEADS * CHUNK, CHUNK), const),
            pl.BlockSpec((CHUNK, da), const),
            pl.BlockSpec((1, da), const),
            pl.BlockSpec((1, da), const),
            pl.BlockSpec((8, db), const),
        ],
        [
            jax.ShapeDtypeStruct((T, P), BF16),
            jax.ShapeDtypeStruct((T, D), BF16),
            jax.ShapeDtypeStruct((B, 1, D), F32),
            jax.ShapeDtypeStruct((1, D), F32),
            jax.ShapeDtypeStruct((N_HEADS * CHUNK, CHUNK), F32),
            jax.ShapeDtypeStruct((CHUNK, da), F32),
            jax.ShapeDtypeStruct((1, da), F32),
            jax.ShapeDtypeStruct((1, da), F32),
            jax.ShapeDtypeStruct((8, db), F32),
        ],
        [pltpu.VMEM((8, db), F32)],
        (proj, proj, proj, sv, dxo, gate, wout, lng, lnb, wstt, pmat, convw, og), comm)


def _ada_fwd(c_all, ada_w, ada_b):
    n, D = c_all.shape
    L, _, sa = ada_w.shape
    tn = _tile(768, sa)

    def body(c_ref, w_ref, b_ref, act_ref, o_ref):
        c = c_ref[...]
        act = (c * _sigmoid(c)).astype(BF16)
        act_ref[...] = act
        o_ref[...] = _dot(act, w_ref[...].astype(BF16)) + b_ref[...]

    return _call(
        body, "ada_fwd", (L, sa // tn),
        [
            pl.BlockSpec((n, D), lambda l, j: (0, 0)),
            pl.BlockSpec((None, D, tn), lambda l, j: (l, 0, j)),
            pl.BlockSpec((None, 1, tn), lambda l, j: (l, 0, j)),
        ],
        [
            pl.BlockSpec((n, D), lambda l, j: (0, 0)),
            pl.BlockSpec((None, n, tn), lambda l, j: (l, 0, j)),
        ],
        [jax.ShapeDtypeStruct((n, D), BF16), jax.ShapeDtypeStruct((L, n, sa), F32)],
        [],
        (c_all, ada_w, ada_b))[0]


def _ada_bwd(c_act, d_ada, comm=None):
    n, D = c_act.shape
    L, _, sa = d_ada.shape
    tn = _tile(768, sa)

    def body(c_ref, d_ref, o_ref):
        o_ref[...] = _dot_tn(c_ref[...], d_ref[...])

    return _call(
        body, "ada_bwd", (L, sa // tn),
        [pl.BlockSpec((n, D), lambda l, j: (0, 0)), pl.BlockSpec((None, n, tn), lambda l, j: (l, 0, j))],
        [pl.BlockSpec((None, D, tn), lambda l, j: (l, 0, j))],
        [jax.ShapeDtypeStruct((L, D, sa), F32)],
        [],
        (c_act, d_ada), comm)


def _colsum(a):
    L, n, C = a.shape

    def body(a_ref, o_ref):
        o_ref[...] = jnp.sum(a_ref[...], axis=0, keepdims=True)

    return _call(
        body, "colsum", (L,),
        [pl.BlockSpec((None, n, C), lambda l: (l, 0, 0))],
        [pl.BlockSpec((None, 1, C), lambda l: (l, 0, 0))],
        [jax.ShapeDtypeStruct((L, 1, C), F32)],
        [],
        (a,))[0][0]


def _row_tile(rows, cols, nbuf):
    budget = VMEM_LIMIT // 2 // (2 * nbuf * 4 * cols)
    t = rows
    while t > max(budget, 8) and t % 2 == 0 and (t // 2) % 8 == 0:
        t //= 2
    return t


def _pair_sum(g, recv, core):
    n, _, R, C = g.shape
    tr = _row_tile(R, C, 3)

    def body(core_ref, g_ref, r_ref, o_ref):
        o_ref[...] = (g_ref[...] + r_ref[...]).astype(BF16)

    return pl.pallas_call(
        body,
        name="pair_sum",
        grid_spec=pltpu.PrefetchScalarGridSpec(
            num_scalar_prefetch=1,
            grid=(n, R // tr),
            in_specs=[
                pl.BlockSpec((None, None, tr, C), lambda i, r, core_ref: (i, core_ref[0], r, 0)),
                pl.BlockSpec((None, tr, C), lambda i, r, core_ref: (i, r, 0)),
            ],
            out_specs=pl.BlockSpec((None, tr, C), lambda i, r, core_ref: (i, r, 0)),
        ),
        out_shape=jax.ShapeDtypeStruct((n, R, C), BF16),
        compiler_params=pltpu.CompilerParams(dimension_semantics=("arbitrary", "arbitrary"),
                                             vmem_limit_bytes=VMEM_LIMIT),
    )(core, g, recv)


def _chip_sum(q, core, l, n_layers, prev):
    nq, R, C = q.shape
    tr = _row_tile(R, C, 4)

    def body(core_ref, q_ref, *rest):
        o_ref = rest[-1]
        s = q_ref[0].astype(F32)
        for j in range(1, nq):
            s = s + q_ref[j].astype(F32)
        o_ref[...] = s

    in_specs = [pl.BlockSpec((nq, tr, C), lambda r, core_ref: (0, r, 0))]
    args = [core, q]
    aliases = {}
    if prev is not None:
        in_specs.append(ANY)
        args.append(prev)
        aliases = {2: 0}
    return pl.pallas_call(
        body,
        name="chip_sum",
        grid_spec=pltpu.PrefetchScalarGridSpec(
            num_scalar_prefetch=1,
            grid=(R // tr,),
            in_specs=in_specs,
            out_specs=pl.BlockSpec((None, None, tr, C), lambda r, core_ref: (l, core_ref[0], r, 0)),
        ),
        out_shape=jax.ShapeDtypeStruct((n_layers, 2, R, C), F32),
        input_output_aliases=aliases,
        compiler_params=pltpu.CompilerParams(dimension_semantics=("arbitrary",), vmem_limit_bytes=VMEM_LIMIT),
    )(*args)


def _sum_blocks(a, n):
    M = a.shape[0] // n
    C = a.shape[1]

    def body(a_ref, o_ref):
        s = a_ref[0:M]
        for j in range(1, n):
            s = s + a_ref[j * M:(j + 1) * M]
        o_ref[...] = s

    return pl.pallas_call(
        body,
        name="sum_blocks",
        out_shape=jax.ShapeDtypeStruct((M, C), F32),
        compiler_params=pltpu.CompilerParams(vmem_limit_bytes=VMEM_LIMIT),
    )(a)


def _adamw(w, g, m, v, emit_grad=False, comm=None):
    R, C = w.shape
    n_out = 4 if emit_grad else 3
    tr = _row_tile(R, C, 4 + n_out) if R % 8 == 0 else R

    def body(w_ref, g_ref, m_ref, v_ref, d_ref, nm_ref, nv_ref, *g_out):
        g = g_ref[...]
        m = ADAM_B1 * m_ref[...] + (1.0 - ADAM_B1) * g
        v = ADAM_B2 * v_ref[...] + (1.0 - ADAM_B2) * (g * g)
        m_hat = m / (1.0 - ADAM_B1 ** ADAM_STEP)
        v_hat = v / (1.0 - ADAM_B2 ** ADAM_STEP)
        d_ref[...] = -ADAM_LR * (m_hat / (jnp.sqrt(v_hat) + ADAM_EPS) + ADAM_WD * w_ref[...])
        nm_ref[...] = m
        nv_ref[...] = v
        if emit_grad:
            g_out[0][...] = g

    spec = pl.BlockSpec((tr, C), lambda i: (i, 0))
    return _call(body, "adamw", (R // tr,), [spec] * 4, [spec] * n_out, [jax.ShapeDtypeStruct((R, C), F32)] * n_out,
                 [], (w, g, m, v), comm)


def kernel(x, c, ada_w, ada_b, norm_ffn1_g, ffn1_w_gu, ffn1_w_down, norm_mix_g, mix_w_in, sgu_ln_g, sgu_ln_b, sgu_w_s, sgu_b, conv_w, out_norm_g, mix_w_out, norm_ffn2_g, ffn2_w_gu, ffn2_w_down, final_norm_g, loss_target, m_ada_w, m_ada_b, m_norm_ffn1_g, m_ffn1_w_gu, m_ffn1_w_down, m_norm_mix_g, m_mix_w_in, m_sgu_ln_g, m_sgu_ln_b, m_sgu_w_s, m_sgu_b, m_conv_w, m_out_norm_g, m_mix_w_out, m_norm_ffn2_g, m_ffn2_w_gu, m_ffn2_w_down, m_final_norm_g, v_ada_w, v_ada_b, v_norm_ffn1_g, v_ffn1_w_gu, v_ffn1_w_down, v_norm_mix_g, v_mix_w_in, v_sgu_ln_g, v_sgu_ln_b, v_sgu_w_s, v_sgu_b, v_conv_w, v_out_norm_g, v_mix_w_out, v_norm_ffn2_g, v_ffn2_w_gu, v_ffn2_w_down, v_final_norm_g):
    weights = dict(ada_w=ada_w, ada_b=ada_b, norm_ffn1_g=norm_ffn1_g, ffn1_w_gu=ffn1_w_gu, ffn1_w_down=ffn1_w_down,
                   norm_mix_g=norm_mix_g, mix_w_in=mix_w_in, sgu_ln_g=sgu_ln_g, sgu_ln_b=sgu_ln_b, sgu_w_s=sgu_w_s,
                   sgu_b=sgu_b, conv_w=conv_w, out_norm_g=out_norm_g, mix_w_out=mix_w_out, norm_ffn2_g=norm_ffn2_g,
                   ffn2_w_gu=ffn2_w_gu, ffn2_w_down=ffn2_w_down, final_norm_g=final_norm_g)
    m_in = dict(ada_w=m_ada_w, ada_b=m_ada_b, norm_ffn1_g=m_norm_ffn1_g, ffn1_w_gu=m_ffn1_w_gu,
                ffn1_w_down=m_ffn1_w_down, norm_mix_g=m_norm_mix_g, mix_w_in=m_mix_w_in, sgu_ln_g=m_sgu_ln_g,
                sgu_ln_b=m_sgu_ln_b, sgu_w_s=m_sgu_w_s, sgu_b=m_sgu_b, conv_w=m_conv_w, out_norm_g=m_out_norm_g,
                mix_w_out=m_mix_w_out, norm_ffn2_g=m_norm_ffn2_g, ffn2_w_gu=m_ffn2_w_gu, ffn2_w_down=m_ffn2_w_down,
                final_norm_g=m_final_norm_g)
    v_in = dict(ada_w=v_ada_w, ada_b=v_ada_b, norm_ffn1_g=v_norm_ffn1_g, ffn1_w_gu=v_ffn1_w_gu,
                ffn1_w_down=v_ffn1_w_down, norm_mix_g=v_norm_mix_g, mix_w_in=v_mix_w_in, sgu_ln_g=v_sgu_ln_g,
                sgu_ln_b=v_sgu_ln_b, sgu_w_s=v_sgu_w_s, sgu_b=v_sgu_b, conv_w=v_conv_w, out_norm_g=v_out_norm_g,
                mix_w_out=v_mix_w_out, norm_ffn2_g=v_norm_ffn2_g, ffn2_w_gu=v_ffn2_w_gu, ffn2_w_down=v_ffn2_w_down,
                final_norm_g=v_final_norm_g)

    B, S, D = x.shape
    T = B * S
    L = ada_w.shape[0]
    F = ffn1_w_down.shape[1] * N_CHIP
    P = mix_w_in.shape[2] * N_CHIP
    DA = D // 2
    DB = D - DA
    HD = DA // N_HEADS
    SA = ada_w.shape[2]
    n_all = B * N_DEV
    mx, my, mc = _position()
    chip = 2 * mx + my
    dev = 2 * chip + mc
    core = jnp.reshape(mc, (1,)).astype(jnp.int32)

    big = ["ffn1_w_gu", "ffn1_w_down", "mix_w_in", "mix_w_out", "ffn2_w_gu", "ffn2_w_down"]
    col_sharded = dict(ffn1_w_gu=True, ffn1_w_down=False, mix_w_in=True, mix_w_out=False,
                       ffn2_w_gu=True, ffn2_w_down=False)
    shards = {k: weights[k].astype(BF16) for k in big}
    gather = lambda l, *names: _gather_comm([(shards[k], l, col_sharded[k]) for k in names])
    full = [dict() for _ in range(L)]

    def arrived(l, names, res):
        full[l].update(zip(names, res))

    n_cw = L * conv_w.shape[1]
    cw_block = jnp.pad(conv_w.reshape(n_cw, conv_w.shape[2]), ((0, 8 - n_cw), (0, 0)))
    c_all, cw_all = _comm_call(_merge(_all_gather_comm(c.reshape(8, B * D // 8)), _all_gather_comm(cw_block)),
                               "gather_c")
    c_all = c_all.reshape(n_all, D)
    cw_all = cw_all.reshape(N_CHIP, 2, 8, conv_w.shape[2])[:, 0, :n_cw]
    conv_full = jnp.transpose(cw_all.reshape(N_CHIP, L, conv_w.shape[1], conv_w.shape[2]), (1, 2, 0, 3))
    conv_full = conv_full.reshape(L, conv_w.shape[1], DB)
    ada_b_mine = lax.dynamic_slice_in_dim(ada_b, chip * SA, SA, axis=1).reshape(L, 1, SA)
    c_act, ada_part = _ada_fwd(c_all, ada_w, ada_b_mine)
    ada_all, first_w = _comm_call(_merge(_all_gather_comm(ada_part.reshape(L * n_all, SA)), gather(0, big[0])),
                                  "gather_first")
    arrived(0, big[:1], [first_w])
    ada_all = ada_all.reshape(N_CHIP, 2, L, n_all, SA)[:, 0]
    ada_all = jnp.transpose(ada_all, (1, 2, 0, 3)).reshape(L, n_all, N_CHIP * SA)
    ada = lax.dynamic_slice_in_dim(ada_all, dev * B, B, axis=1).reshape(L, B, N_MOD, 1, D)
    mods = [[ada[l, :, j] for j in range(N_MOD)] for l in range(L)]

    x0 = x.reshape(T, D)
    gains = lambda name, l: weights[name][l].reshape(1, D)
    hmask = jnp.repeat(jnp.eye(N_HEADS, dtype=F32), HD, axis=0)
    pmat = (jnp.repeat(hmask, HD, axis=1) / HD).astype(BF16)

    def mix_consts(l):
        lng = jnp.tile(sgu_ln_g[l], N_HEADS).reshape(1, DA)
        lnb = jnp.tile(sgu_ln_b[l], N_HEADS).reshape(1, DA)
        wst = sgu_w_s[l].reshape(N_HEADS * CHUNK, CHUNK)
        wstt = jnp.swapaxes(sgu_w_s[l], 1, 2).reshape(N_HEADS * CHUNK, CHUNK)
        bias = jnp.repeat(jnp.transpose(sgu_b[l]), HD, axis=1)
        return lng, lnb, wst, wstt, bias

    def fetch(fn, *args, bring=(), **kw):
        bring = [(l, k) for l, k in bring if l < L]
        comm = _gather_comm([(shards[k], l, col_sharded[k]) for l, k in bring]) if bring else None
        res, got = fn(*args, comm, **kw)
        for (l, k), a in zip(bring, got):
            full[l][k] = a
        return res

    saved = []
    xc = x0
    for l in range(L):
        sh1, sc1, g1, sh2, sc2, g2, sh3, sc3, g3 = mods[l]
        lng, lnb, wst, wstt, bias = mix_consts(l)
        w = full[l]
        if l == 0:
            gu1, a1 = fetch(_ffn_up, xc, gains("norm_ffn1_g", l), sh1, sc1, w["ffn1_w_gu"],
                            bring=[(l, "ffn1_w_down"), (l, "mix_w_in"), (l, "mix_w_out")])
            xa, f1 = fetch(_ffn_down, a1, xc, g1, w["ffn1_w_down"], bring=[(l, "ffn2_w_down")])
        else:
            xa, gu1, a1, f1 = fetch(_ffn_fwd, xc, gains("norm_ffn1_g", l), sh1, sc1, g1, w["ffn1_w_gu"],
                                    w["ffn1_w_down"], bring=[(l, "ffn2_w_gu"), (l, "mix_w_in")])
        xb, proj, h2, yn, sv = fetch(_mix_fwd, xa, gains("norm_mix_g", l), sh2, sc2, g2, w["mix_w_in"], w["mix_w_out"],
                                     lng, lnb, wst, bias, pmat, conv_full[l], gains("out_norm_g", l),
                                     bring=[(l, "ffn2_w_gu")] if l == 0 else [(l, "ffn2_w_down")])
        if l + 1 < L:
            xd, gu2, a2, f2 = fetch(_ffn_fwd, xb, gains("norm_ffn2_g", l), sh3, sc3, g3, w["ffn2_w_gu"],
                                    w["ffn2_w_down"],
                                    bring=[(l + 1, "ffn1_w_gu"), (l + 1, "mix_w_out"), (l + 1, "ffn1_w_down")])
        else:
            dx, gu2, a2, f2, loss_block, d_final = fetch(
                _ffn_fwd, xb, gains("norm_ffn2_g", l), sh3, sc3, g3, w["ffn2_w_gu"], w["ffn2_w_down"],
                head=(loss_target.reshape(T, D), final_norm_g.reshape(1, D)))
            xd = None
        saved.append(dict(x0=xc, xa=xa, xb=xb, gu1=gu1, a1=a1, f1=f1, proj=proj, h2=h2, yn=yn, sv=sv,
                          gu2=gu2, a2=a2, f2=f2))
        xc = xd

    reduced = dict.fromkeys(big)

    def halves(name, g):
        if g.ndim == 4:
            return g
        return g.reshape(N_CHIP, 2, weights[name].shape[1] // 2, g.shape[-1])

    class Reduction:
        def __init__(self, l, name, g):
            self.l, self.name, self.g, self.stage = l, name, halves(name, g), 0
            self.ici_bytes = 3 * (g.size // 8) * 2

        def step(self):
            self.stage += 1
            if self.stage == 1:
                return _sibling_half_comm([self.g])
            if self.stage == 2:
                return _scatter_comm([_pair_sum(self.g, self.got[0], core)])
            if self.stage == 3:
                reduced[self.name] = _chip_sum(self.got[0], core, self.l, L, reduced[self.name])
                return _share_comm([reduced[self.name]], self.l)
            reduced[self.name] = self.got[0]
            return None

    active, extra, gathered = [], [], {}

    def carry(fn, *args, us=None):
        left = None if us is None else us * SCATTER_BYTES_PER_US
        riders = []
        for r in active:
            if r.stage == 1 and left is not None:
                if r.ici_bytes > left * SCATTER_OVERSHOOT:
                    continue
                left -= r.ici_bytes
            riders.append(r)
        comms = [r.step() for r in riders] + [cm for cm, _ in extra]
        takers = [functools.partial(setattr, r, "got") for r in riders] + [cb for _, cb in extra]
        extra.clear()
        if fn is None:
            res, got = None, (_comm_call(_merge(*comms), "reduce_alone") if comms else [])
        else:
            res, got = fn(*args, comm=_merge(*comms))
        at = 0
        for cm, take in zip(comms, takers):
            take(got[at:at + len(cm.out_shape)])
            at += len(cm.out_shape)
        for r in riders:
            if r.stage == 3:
                r.step()
                active.remove(r)
        return res

    def reduce_later(l, name, g):
        active.append(Reduction(l, name, g))

    small = [None] * L
    dwsts = [None] * L
    d_ada = [None] * L
    for l in reversed(range(L)):
        sh1, sc1, g1, sh2, sc2, g2, sh3, sc3, g3 = mods[l]
        lng, lnb, wst, wstt, bias = mix_consts(l)
        s = saved[l]
        w = full[l]
        last = l == 0
        dx, dgu2, h3, df2, dsc3, dsh3, dgain3, dg3 = carry(
            _ffn_bwd, dx, s["xb"], s["gu2"], s["f2"], gains("norm_ffn2_g", l), sh3, sc3, g3, w["ffn2_w_gu"],
            w["ffn2_w_down"], us=170)
        ffn2_grads = [
            lambda: reduce_later(l, "ffn2_w_gu", carry(_wgrad, h3, dgu2, D, 2 * F // N_CHIP, True, "wgrad_gu",
                                                       WGRAD_TOKENS // 2, us=110)[0]),
            lambda: reduce_later(l, "ffn2_w_down", carry(_wgrad, s["a2"], df2[None], F // 2, D, False, "wgrad_down",
                                                         us=50)[0])]
        if not last:
            ffn2_grads[0]()
            ffn2_grads[1]()
        dproj, d_o, dg2, dog, dwst, dbias, dlng, dlnb, dconvw = carry(
            _mix_core_bwd, s["proj"], s["sv"], dx, g2, w["mix_w_out"], lng, lnb, wstt, pmat, conv_full[l],
            gains("out_norm_g", l), us=150)
        mix_grads = [
            lambda: reduce_later(l, "mix_w_out", carry(_wgrad, s["yn"], d_o[None], D, D, False, "wgrad_out", us=30)[0]),
            lambda: reduce_later(l, "mix_w_in", carry(_wgrad, s["h2"], dproj[None], D, P // N_CHIP, True, "wgrad_in",
                                                      us=65)[0])]
        if not last:
            mix_grads[0]()
        dx, dsc2, dsh2, dgain2 = carry(_mixin_bwd, dx, s["xa"], dproj, gains("norm_mix_g", l), sc2, w["mix_w_in"], us=60)
        if not last:
            mix_grads[1]()
        dx, dgu, h1, df, dsc1, dsh1, dgain1, dg1 = carry(
            _ffn_bwd, dx, s["x0"], s["gu1"], s["f1"], gains("norm_ffn1_g", l), sh1, sc1, g1, w["ffn1_w_gu"],
            w["ffn1_w_down"], us=170)
        d_ada[l] = jnp.concatenate([dsh1, dsc1, dg1, dsh2, dsc2, dg2, dsh3, dsc3, dg3], axis=1).reshape(B, N_MOD * D)
        small[l] = [dgain1, dgain2, dgain3, dog, dlng, dlnb, dbias[:, ::HD], dconvw]
        dwsts[l] = dwst
        if last:
            flat = [a.reshape(-1, 128) for ll in range(L) for a in small[ll]]
            flat += [d_final.reshape(-1, 128), loss_block[0:1]]
            pad = (-sum(a.shape[0] for a in flat)) % 8
            packed = jnp.concatenate(flat + [jnp.zeros((pad, 128), F32)], axis=0)
            extra.append((_all_gather_comm(jnp.stack(d_ada).reshape(L * B, N_MOD * D)),
                          lambda got: gathered.update(d_ada=got[0])))
            extra.append((_all_gather_comm(packed), lambda got: gathered.update(small=got[0])))
            for ll in range(L):
                extra.append((_all_gather_comm(dwsts[ll]), lambda got, ll=ll: gathered.update({("dwst", ll): got[0]})))
        reduce_later(l, "ffn1_w_gu", carry(_wgrad, h1, dgu, D, 2 * F // N_CHIP, True, "wgrad_gu", WGRAD_TOKENS // 2,
                                           us=110)[0])
        reduce_later(l, "ffn1_w_down", carry(_wgrad, s["a1"], df[None], F // 2, D, False, "wgrad_down", us=50)[0])
        if last:
            ffn2_grads[0]()
            ffn2_grads[1]()
            mix_grads[1]()
            mix_grads[0]()
    grad_x = dx.reshape(B, S, D)

    def finished(name):
        while any(r.name == name for r in active):
            carry(None)
        return reduced[name].reshape(weights[name].shape)

    grads = {}
    d_ada_all = jnp.transpose(gathered["d_ada"].reshape(N_DEV, L, B, N_MOD * D), (1, 0, 2, 3))
    d_ada_all = d_ada_all.reshape(L, n_all, N_MOD * D)
    grads["ada_b"] = _colsum(d_ada_all).reshape(L, N_MOD * D)
    d_ada_mine = lax.dynamic_slice_in_dim(d_ada_all, chip * SA, SA, axis=2).astype(BF16)
    grads["ada_w"] = _ada_bwd(c_act, d_ada_mine)[0][0]

    total = _sum_blocks(gathered["small"].reshape(-1, 128), N_DEV)
    pieces, at = [], 0
    for a in flat:
        pieces.append(total[at:at + a.shape[0]])
        at += a.shape[0]
    per_layer = len(small[0])
    stack = lambda j, shape: jnp.stack([pieces[l * per_layer + j].reshape(shape) for l in range(L)])
    grads["norm_ffn1_g"] = stack(0, (D,))
    grads["norm_mix_g"] = stack(1, (D,))
    grads["norm_ffn2_g"] = stack(2, (D,))
    grads["out_norm_g"] = stack(3, (D,))
    grads["sgu_ln_g"] = stack(4, (N_HEADS, HD)).sum(axis=1)
    grads["sgu_ln_b"] = stack(5, (N_HEADS, HD)).sum(axis=1)
    grads["sgu_b"] = jnp.swapaxes(stack(6, (CHUNK, N_HEADS)), 1, 2)
    g_conv = stack(7, (8, DB))[:, :conv_w.shape[1]]
    grads["conv_w"] = lax.dynamic_slice_in_dim(g_conv, chip * conv_w.shape[2], conv_w.shape[2], axis=2)
    grads["final_norm_g"] = pieces[-2].reshape(D)
    loss = pieces[-1][0, 0]
    grads["sgu_w_s"] = jnp.stack([_sum_blocks(gathered["dwst", l].reshape(-1, CHUNK), N_DEV) for l in range(L)])
    grads["sgu_w_s"] = grads["sgu_w_s"].reshape(L, N_HEADS, CHUNK, CHUNK)

    names = list(weights)
    delta, new_m, new_v = {}, {}, {}
    light = [k for k in names if k not in big and k != "ada_w"]
    for k in light + ["ada_w"] + big:
        wk = weights[k]
        view = (1, wk.shape[0]) if wk.ndim == 1 else (-1, wk.shape[-1])
        if k in big:
            grads[k] = finished(k)
        operands = (wk.reshape(view), grads[k].reshape(view), m_in[k].reshape(view), v_in[k].reshape(view))
        if k in light:
            d, nm, nv = carry(_adamw, *operands)
        else:
            d, nm, nv, *g_again = _adamw(*operands, emit_grad=k in big)[0]
            if g_again:
                grads[k] = g_again[0].reshape(wk.shape)
        delta[k], new_m[k], new_v[k] = d.reshape(wk.shape), nm.reshape(wk.shape), nv.reshape(wk.shape)

    return (loss, grad_x, *[grads[k] for k in names], *[delta[k] for k in names],
            *[new_m[k] for k in names], *[new_v[k] for k in names])
```

```python
import functools
import math

import jax
import jax.numpy as jnp
from jax import lax
from jax.experimental import pallas as pl
from jax.experimental.pallas import tpu as pltpu

F32 = jnp.float32
BF16 = jnp.bfloat16
MESH = pl.DeviceIdType.MESH

N_HEADS = 8
CHUNK = 128
N_MOD = 9
EPS = 1e-6
N_DEV = 8
N_CHIP = 4

ADAM_LR = 0.001
ADAM_B1 = 0.9
ADAM_B2 = 0.999
ADAM_EPS = 1e-08
ADAM_WD = 0.01
ADAM_STEP = 10

TOKEN_TILE = 512
BWD_TILE = 256
FWD_TILE = 512
FF_SLAB = 768
MIX_TILE = 512
MIX_BWD_TILE = 512
WGRAD_TOKENS = 2048
VMEM_LIMIT = 56 * 1024 * 1024

SCATTER_BYTES_PER_US = 68_000
SCATTER_OVERSHOOT = 1.25

ANY = pl.BlockSpec(memory_space=pl.ANY)


def _tile(pref, n):
    t = min(pref, n)
    assert n % t == 0, (pref, n)
    return t


def _slabs(n, width):
    return [slice(c0, min(c0 + width, n)) for c0 in range(0, n, width)]


def _dot(a, b):
    return jnp.dot(a, b, preferred_element_type=F32)


def _dot_nt(a, b):
    return lax.dot_general(a, b, (((1,), (1,)), ((), ())), preferred_element_type=F32)


def _dot_tn(a, b):
    return lax.dot_general(a, b, (((0,), (0,)), ((), ())), preferred_element_type=F32)


def _sigmoid(x):
    return 1.0 / (1.0 + jnp.exp(-x))


def _sigmoid_fast(x):
    return pl.reciprocal(1.0 + jnp.exp(-x), approx=True)


def _rms(x):
    r = lax.rsqrt(jnp.mean(x * x, axis=-1, keepdims=True) + EPS)
    return x * r, r


def _norm_mod_bwd(x, dh, gain, sc):
    xh, r = _rms(x)
    dsc = jnp.sum(dh * (xh * gain), axis=0, keepdims=True)
    dsh = jnp.sum(dh, axis=0, keepdims=True)
    dn = dh * (1.0 + sc)
    dgain = jnp.sum(dn * xh, axis=0, keepdims=True)
    dy = dn * gain
    dx = r * (dy - xh * jnp.mean(dy * xh, axis=-1, keepdims=True))
    return dx, dsc, dsh, dgain


def _acc(ref, first, val):
    @pl.when(first)
    def _():
        ref[...] = val

    @pl.when(jnp.logical_not(first))
    def _():
        ref[...] += val


class _Comm:
    def __init__(self, args, out_shape, scratch, phases, aliases=None):
        self.args, self.out_shape, self.scratch = list(args), list(out_shape), list(scratch)
        self.phases, self.aliases = phases, dict(aliases or {})


def _merge(*comms):
    comms = [c for c in comms if c is not None]
    if len(comms) <= 1:
        return comms[0] if comms else None
    args = [a for c in comms for a in c.args]
    out_shape = [o for c in comms for o in c.out_shape]
    scratch = [s for c in comms for s in c.scratch]
    aliases, ai, oi = {}, 0, 0
    for c in comms:
        aliases.update({ai + i: oi + o for i, o in c.aliases.items()})
        ai += len(c.args)
        oi += len(c.out_shape)

    def phases(ins, outs, sems):
        parts, ai, oi, si = [], 0, 0, 0
        for c in comms:
            parts.append(c.phases(ins[ai:ai + len(c.args)], outs[oi:oi + len(c.out_shape)], sems[si:si + len(c.scratch)]))
            ai, oi, si = ai + len(c.args), oi + len(c.out_shape), si + len(c.scratch)

        def run(k):
            def go():
                for p in parts:
                    if p[k] is not None:
                        p[k]()
            return go
        return run(0), run(1), run(2)

    return _Comm(args, out_shape, scratch, phases, aliases)


def _call(body, name, grid, in_specs, out_specs, out_shape, scratch, args, comm=None):
    n_in, n_out, n_scr = len(in_specs), len(out_specs), len(scratch)
    sem = ("arbitrary",) * len(grid)
    params = pltpu.CompilerParams(dimension_semantics=sem, vmem_limit_bytes=VMEM_LIMIT)
    if comm is None:
        res = pl.pallas_call(body, name=name, grid=grid, in_specs=in_specs, out_specs=out_specs, out_shape=out_shape,
                             scratch_shapes=scratch, compiler_params=params)(*args)
        return list(res), []
    m_in, m_out = len(comm.args), len(comm.out_shape)

    def full(*refs):
        c_in, c_min = refs[:n_in], refs[n_in:n_in + m_in]
        o = n_in + m_in
        c_out, c_mout = refs[o:o + n_out], refs[o + n_out:o + n_out + m_out]
        o += n_out + m_out
        c_scr, c_sem = refs[o:o + n_scr], refs[o + n_scr:]
        start, mid, finish = comm.phases(c_min, c_mout, c_sem)
        ids = [pl.program_id(a) for a in range(len(grid))]
        first = functools.reduce(jnp.logical_and, [i == 0 for i in ids])
        last = functools.reduce(jnp.logical_and, [i == g - 1 for i, g in zip(ids, grid)])
        pl.when(first)(start)
        if mid is not None:
            pl.when(last)(mid)
        body(*c_in, *c_out, *c_scr)
        pl.when(last)(finish)

    res = pl.pallas_call(
        full, name=name, grid=grid,
        in_specs=list(in_specs) + [ANY] * m_in,
        out_specs=list(out_specs) + [ANY] * m_out,
        out_shape=list(out_shape) + comm.out_shape,
        scratch_shapes=list(scratch) + comm.scratch,
        input_output_aliases={n_in + i: n_out + o for i, o in comm.aliases.items()},
        compiler_params=params,
    )(*args, *comm.args)
    return list(res[:n_out]), list(res[n_out:])


def _comm_call(comm, name):
    m_in, m_out = len(comm.args), len(comm.out_shape)

    def body(*refs):
        start, mid, finish = comm.phases(refs[:m_in], refs[m_in:m_in + m_out], refs[m_in + m_out:])
        start()
        if mid is not None:
            mid()
        finish()

    res = pl.pallas_call(
        body, name=name, in_specs=[ANY] * m_in, out_specs=[ANY] * m_out, out_shape=comm.out_shape,
        scratch_shapes=comm.scratch, input_output_aliases=comm.aliases,
    )(*comm.args)
    return list(res)


def _position():
    return lax.axis_index("x"), lax.axis_index("y"), lax.axis_index("c")


def _gather_comm(items):
    n = len(items)
    half = [s.shape[1] // 2 for s, _, _ in items]

    def full_shape(i):
        s, _, col = items[i]
        _, R, C = s.shape
        return jax.ShapeDtypeStruct((R, N_CHIP * C) if col else (N_CHIP * R, C), s.dtype)

    def phases(ins, outs, sems):
        send_sems, recv_sems, local_sems = sems
        x, y, c = _position()

        def region(i, chip, h):
            s, _, col = items[i]
            _, R, C = s.shape
            if col:
                return outs[i].at[pl.ds(h * half[i], half[i]), pl.ds(chip * C, C)]
            return outs[i].at[pl.ds(chip * R + h * half[i], half[i]), :]

        def mine(i, h):
            return ins[i].at[items[i][1], pl.ds(h * half[i], half[i]), :]

        def copies(kx, ky, kc):
            k_me = 2 * kx + ky
            sibling = (kx, ky, 1 - kc)
            chips = [(1 - kx, ky), (kx, 1 - ky), (1 - kx, 1 - ky)]
            local, first, passed, arrive_ici, arrive_d2d = [], [], [], [], []

            def remote(src, dst, s, to):
                return pltpu.make_async_remote_copy(src_ref=src, dst_ref=dst, send_sem=send_sems.at[s],
                                                    recv_sem=recv_sems.at[s], device_id=to, device_id_type=MESH)

            for i in range(n):
                for h in range(2):
                    local.append(pltpu.make_async_copy(mine(i, h), region(i, k_me, h), local_sems.at[2 * i + h]))
                for j, (px, py) in enumerate(chips):
                    s = 6 * i + j
                    first.append(remote(mine(i, kc), region(i, k_me, kc), s, (px, py, kc)))
                    got = region(i, 2 * px + py, kc)
                    arrive_ici.append(remote(got, got, s, (px, py, kc)))
                    passed.append(remote(got, got, s + 3, sibling))
                    other = region(i, 2 * px + py, 1 - kc)
                    arrive_d2d.append(remote(other, other, s + 3, sibling))
            return local, first, passed, arrive_ici, arrive_d2d

        def on_each_device(fn):
            def go():
                for kx in range(2):
                    for ky in range(2):
                        for kc in range(2):
                            pl.when((x == kx) & (y == ky) & (c == kc))(functools.partial(fn, *copies(kx, ky, kc)))
            return go

        def start(local, first, passed, arrive_ici, arrive_d2d):
            for cp in local + first:
                cp.start()

        def mid(local, first, passed, arrive_ici, arrive_d2d):
            for a, p in zip(arrive_ici, passed):
                a.wait_recv()
                p.start()

        def finish(local, first, passed, arrive_ici, arrive_d2d):
            for a in arrive_d2d:
                a.wait_recv()
            for cp in first + passed:
                cp.wait_send()
            for cp in local:
                cp.wait()

        return on_each_device(start), on_each_device(mid), on_each_device(finish)

    scratch = [pltpu.SemaphoreType.DMA((6 * n,)), pltpu.SemaphoreType.DMA((6 * n,)), pltpu.SemaphoreType.DMA((2 * n,))]
    return _Comm([s for s, _, _ in items], [full_shape(i) for i in range(n)], scratch, phases)


def _sibling_half_comm(gs):
    n = len(gs)

    def phases(ins, outs, sems):
        send_sems, recv_sems = sems
        x, y, c = _position()

        def copies():
            return [pltpu.make_async_remote_copy(
                src_ref=ins[i].at[:, 1 - c], dst_ref=outs[i], send_sem=send_sems.at[i], recv_sem=recv_sems.at[i],
                device_id=(x, y, 1 - c), device_id_type=MESH) for i in range(n)]

        def start():
            for cp in copies():
                cp.start()

        def finish():
            for cp in copies():
                cp.wait()

        return start, None, finish

    out_shape = [jax.ShapeDtypeStruct(g.shape[:1] + g.shape[2:], g.dtype) for g in gs]
    return _Comm(gs, out_shape, [pltpu.SemaphoreType.DMA((n,)), pltpu.SemaphoreType.DMA((n,))], phases)


def _scatter_comm(ps):
    n = len(ps)

    def phases(ins, outs, sems):
        send_sems, recv_sems, local_sems = sems
        x, y, c = _position()
        k_me = 2 * x + y
        chips = [(1 - x, y), (x, 1 - y), (1 - x, 1 - y)]

        def copies():
            local = [pltpu.make_async_copy(ins[i].at[k_me], outs[i].at[k_me], local_sems.at[i]) for i in range(n)]
            remote = [pltpu.make_async_remote_copy(
                src_ref=ins[i].at[2 * px + py], dst_ref=outs[i].at[k_me],
                send_sem=send_sems.at[3 * i + j], recv_sem=recv_sems.at[3 * i + j],
                device_id=(px, py, c), device_id_type=MESH) for i in range(n) for j, (px, py) in enumerate(chips)]
            return local, remote

        def start():
            local, remote = copies()
            for cp in local + remote:
                cp.start()

        def finish():
            local, remote = copies()
            for cp in remote + local:
                cp.wait()

        return start, None, finish

    scratch = [pltpu.SemaphoreType.DMA((3 * n,)), pltpu.SemaphoreType.DMA((3 * n,)), pltpu.SemaphoreType.DMA((n,))]
    return _Comm(ps, [jax.ShapeDtypeStruct(p.shape, p.dtype) for p in ps], scratch, phases)


def _share_comm(rs, l):
    n = len(rs)

    def phases(ins, outs, sems):
        send_sems, recv_sems = sems
        x, y, c = _position()

        def copy(i, h):
            return pltpu.make_async_remote_copy(
                src_ref=outs[i].at[l, h], dst_ref=outs[i].at[l, h], send_sem=send_sems.at[i], recv_sem=recv_sems.at[i],
                device_id=(x, y, 1 - c), device_id_type=MESH)

        def start():
            for i in range(n):
                copy(i, c).start()

        def finish():
            for i in range(n):
                copy(i, 1 - c).wait_recv()
            for i in range(n):
                copy(i, c).wait_send()

        return start, None, finish

    return _Comm(rs, [jax.ShapeDtypeStruct(r.shape, r.dtype) for r in rs],
                 [pltpu.SemaphoreType.DMA((n,)), pltpu.SemaphoreType.DMA((n,))], phases,
                 aliases={i: i for i in range(n)})


def _all_gather_comm(block):
    def phases(ins, outs, sems):
        send_sems, recv_sems, local_sem = sems
        (src,), (out,) = ins, outs
        x, y, c = _position()
        sibling = (x, y, 1 - c)
        chips = [(1 - x, y), (x, 1 - y), (1 - x, 1 - y)]

        def slot(px, py, pc):
            return out.at[4 * px + 2 * py + pc]

        def copy(k, blk, to, own=False):
            return pltpu.make_async_remote_copy(
                src_ref=src if own else slot(*blk), dst_ref=slot(*blk),
                send_sem=send_sems.at[k], recv_sem=recv_sems.at[k], device_id=to, device_id_type=MESH)

        mine = lambda: pltpu.make_async_copy(src, slot(x, y, c), local_sem.at[0])
        first = lambda: [copy(0, (x, y, c), sibling, True)] + [
            copy(1 + j, (x, y, c), (*chip, c), True) for j, chip in enumerate(chips)]
        passed = lambda: [copy(4 + j, (*chip, c), sibling) for j, chip in enumerate(chips)]

        def start():
            mine().start()
            for cp in first():
                cp.start()

        def mid():
            for j, (chip, p) in enumerate(zip(chips, passed())):
                copy(1 + j, (*chip, c), (x, y, c)).wait_recv()
                p.start()

        def finish():
            copy(0, sibling, (x, y, c)).wait_recv()
            for j, chip in enumerate(chips):
                copy(4 + j, (*chip, 1 - c), (x, y, c)).wait_recv()
            for cp in first() + passed():
                cp.wait_send()
            mine().wait()

        return start, mid, finish

    scratch = [pltpu.SemaphoreType.DMA((7,)), pltpu.SemaphoreType.DMA((7,)), pltpu.SemaphoreType.DMA((1,))]
    return _Comm([block], [jax.ShapeDtypeStruct((N_DEV,) + block.shape, block.dtype)], scratch, phases)


def _ffn_up(x, gain, sh, sc, wgu, comm=None):
    T, D = x.shape
    F = wgu.shape[1] // 2
    B = sh.shape[0]
    tm = _tile(TOKEN_TILE, T // B)
    tps = (T // B) // tm
    slabs = _slabs(F, FF_SLAB)

    def body(x_ref, gain_ref, sh_ref, sc_ref, w_ref, gu_ref, a_ref):
        xh, _ = _rms(x_ref[...])
        h = (xh * gain_ref[...] * (1.0 + sc_ref[0]) + sh_ref[0]).astype(BF16)

        def dots(s):
            return _dot(h, w_ref[:, s]), _dot(h, w_ref[:, slice(F + s.start, F + s.stop)])

        nxt = dots(slabs[0])
        for j, s in enumerate(slabs):
            g, u = nxt
            if j + 1 < len(slabs):
                nxt = dots(slabs[j + 1])
            gu_ref[0, :, s] = g.astype(BF16)
            gu_ref[1, :, s] = u.astype(BF16)
            a_ref[:, s] = (g * _sigmoid(g) * u).astype(BF16)

    seq = lambda i: (i // tps, 0, 0)
    return _call(
        body, "ffn_up", (T // tm,),
        [
            pl.BlockSpec((tm, D), lambda i: (i, 0)),
            pl.BlockSpec((1, D), lambda i: (0, 0)),
            pl.BlockSpec((1, 1, D), seq),
            pl.BlockSpec((1, 1, D), seq),
            pl.BlockSpec((D, 2 * F), lambda i: (0, 0), pipeline_mode=pl.Buffered(1)),
        ],
        [
            pl.BlockSpec((2, tm, F), lambda i: (0, i, 0)),
            pl.BlockSpec((tm, F), lambda i: (i, 0)),
        ],
        [
            jax.ShapeDtypeStruct((2, T, F), BF16),
            jax.ShapeDtypeStruct((T, F), BF16),
        ],
        [],
        (x, gain, sh, sc, wgu), comm)


def _ffn_down(a, x, gate, wd, comm=None):
    T, F = a.shape
    D = x.shape[1]
    B = gate.shape[0]
    tm = _tile(2 * TOKEN_TILE, T // B)
    tps = (T // B) // tm

    def body(a_ref, x_ref, gate_ref, wd_ref, xo_ref, f_ref):
        f = _dot(a_ref[...], wd_ref[...])
        f_ref[...] = f.astype(BF16)
        xo_ref[...] = x_ref[...] + 0.5 * gate_ref[0] * f

    return _call(
        body, "ffn_down", (T // tm,),
        [
            pl.BlockSpec((tm, F), lambda i: (i, 0)),
            pl.BlockSpec((tm, D), lambda i: (i, 0)),
            pl.BlockSpec((1, 1, D), lambda i: (i // tps, 0, 0)),
            pl.BlockSpec((F, D), lambda i: (0, 0), pipeline_mode=pl.Buffered(1)),
        ],
        [pl.BlockSpec((tm, D), lambda i: (i, 0)), pl.BlockSpec((tm, D), lambda i: (i, 0))],
        [jax.ShapeDtypeStruct((T, D), F32), jax.ShapeDtypeStruct((T, D), BF16)],
        [],
        (a, x, gate, wd), comm)


def _ffn_fwd(x, gain, sh, sc, gate, wgu, wd, comm=None, head=None):
    T, D = x.shape
    F = wd.shape[0]
    B = sh.shape[0]
    tm = _tile(FWD_TILE, T // B)
    tps = (T // B) // tm
    slabs = _slabs(F, FF_SLAB)

    def body(x_ref, gain_ref, sh_ref, sc_ref, gate_ref, w_ref, wd_ref, *rest):
        if head is None:
            xo_ref, gu_ref, a_ref, f_ref = rest
        else:
            t_ref, fgain_ref, xo_ref, gu_ref, a_ref, f_ref, loss_ref, dfgain_ref = rest
        x = x_ref[...]
        h = (_rms(x)[0] * gain_ref[...] * (1.0 + sc_ref[0]) + sh_ref[0]).astype(BF16)

        def dots(s):
            return _dot(h, w_ref[:, s]), _dot(h, w_ref[:, slice(F + s.start, F + s.stop)])

        nxt = dots(slabs[0])
        for j, s in enumerate(slabs):
            g, u = nxt
            if j + 1 < len(slabs):
                nxt = dots(slabs[j + 1])
            gu_ref[0, :, s] = g.astype(BF16)
            gu_ref[1, :, s] = u.astype(BF16)
            a_ref[:, s] = (g * _sigmoid(g) * u).astype(BF16)
        f = _dot(a_ref[...], wd_ref[...])
        f_ref[...] = f.astype(BF16)
        xo = x + 0.5 * gate_ref[0] * f
        if head is None:
            xo_ref[...] = xo
        else:
            first = pl.program_id(0) == 0
            xh, r = _rms(xo)
            fgain = fgain_ref[...]
            err = xh * fgain - t_ref[...]
            _acc(loss_ref, first, jnp.zeros((8, 128), F32) + 0.5 * jnp.sum(err * err) / D)
            dout = err * (1.0 / D)
            _acc(dfgain_ref, first, jnp.sum(dout * xh, axis=0, keepdims=True))
            dy = dout * fgain
            xo_ref[...] = r * (dy - xh * jnp.mean(dy * xh, axis=-1, keepdims=True))

    seq = lambda i: (i // tps, 0, 0)
    row = lambda i: (i, 0)
    const = lambda i: (0, 0)
    in_specs = [
        pl.BlockSpec((tm, D), row),
        pl.BlockSpec((1, D), const),
        pl.BlockSpec((1, 1, D), seq),
        pl.BlockSpec((1, 1, D), seq),
        pl.BlockSpec((1, 1, D), seq),
        pl.BlockSpec((D, 2 * F), const, pipeline_mode=pl.Buffered(1)),
        pl.BlockSpec((F, D), const, pipeline_mode=pl.Buffered(1)),
    ]
    out_specs = [
        pl.BlockSpec((tm, D), row),
        pl.BlockSpec((2, tm, F), lambda i: (0, i, 0)),
        pl.BlockSpec((tm, F), row),
        pl.BlockSpec((tm, D), row),
    ]
    out_shape = [
        jax.ShapeDtypeStruct((T, D), F32),
        jax.ShapeDtypeStruct((2, T, F), BF16),
        jax.ShapeDtypeStruct((T, F), BF16),
        jax.ShapeDtypeStruct((T, D), BF16),
    ]
    args = (x, gain, sh, sc, gate, wgu, wd)
    if head is not None:
        in_specs += [pl.BlockSpec((tm, D), row), pl.BlockSpec((1, D), const)]
        out_specs += [pl.BlockSpec((8, 128), const), pl.BlockSpec((1, D), const)]
        out_shape += [jax.ShapeDtypeStruct((8, 128), F32), jax.ShapeDtypeStruct((1, D), F32)]
        args += tuple(head)
    return _call(body, "ffn_fwd", (T // tm,), in_specs, out_specs, out_shape, [], args, comm)


def _ffn_bwd(dxo, x, gu, f, gain, sh, sc, gate, wgu, wd, comm=None):
    T, D = x.shape
    F = wd.shape[0]
    B = sc.shape[0]
    tm = _tile(BWD_TILE, T // B)
    tps = (T // B) // tm
    slabs = _slabs(F, FF_SLAB)

    def body(dxo_ref, x_ref, gu_ref, f_ref, gain_ref, sh_ref, sc_ref, gate_ref, w_ref, wd_ref,
             dx_ref, dgu_ref, h_ref, df_ref, dsc_ref, dsh_ref, dgain_ref, dgate_ref):
        i = pl.program_id(0)
        first_of_seq = (i % tps) == 0
        gain = gain_ref[...]
        sc = sc_ref[0]
        dxo = dxo_ref[...]
        x = x_ref[...]
        df = (0.5 * gate_ref[0] * dxo).astype(BF16)
        df_ref[...] = df
        nxt = _dot_nt(df, wd_ref[slabs[0], :])
        for j, s in enumerate(slabs):
            da = nxt
            if j + 1 < len(slabs):
                nxt = _dot_nt(df, wd_ref[slabs[j + 1], :])
            g = gu_ref[0, :, s]
            sg = 1.0 / (1.0 + jnp.exp(-g))
            t = g * sg
            dab = da.astype(BF16)
            dgu_ref[1, :, s] = dab * t
            dgu_ref[0, :, s] = dab * gu_ref[1, :, s] * (sg + t - t * sg)
        dh = _dot_nt(dgu_ref[0], w_ref[:, 0:F]) + _dot_nt(dgu_ref[1], w_ref[:, F:])
        dx, dsc, dsh, dgain = _norm_mod_bwd(x, dh, gain, sc)
        dx_ref[...] = dxo + dx
        h_ref[...] = (_rms(x)[0] * gain * (1.0 + sc) + sh_ref[0]).astype(BF16)
        _acc(dsc_ref.at[0], first_of_seq, dsc)
        _acc(dsh_ref.at[0], first_of_seq, dsh)
        _acc(dgain_ref, i == 0, dgain)
        _acc(dgate_ref.at[0], first_of_seq, 0.5 * jnp.sum(dxo * f_ref[...].astype(F32), axis=0, keepdims=True))

    seq = lambda i: (i // tps, 0, 0)
    row = lambda i: (i, 0)
    return _call(
        body, "ffn_bwd", (T // tm,),
        [
            pl.BlockSpec((tm, D), row),
            pl.BlockSpec((tm, D), row),
            pl.BlockSpec((2, tm, F), lambda i: (0, i, 0)),
            pl.BlockSpec((tm, D), row),
            pl.BlockSpec((1, D), lambda i: (0, 0)),
            pl.BlockSpec((1, 1, D), seq),
            pl.BlockSpec((1, 1, D), seq),
            pl.BlockSpec((1, 1, D), seq),
            pl.BlockSpec((D, 2 * F), lambda i: (0, 0), pipeline_mode=pl.Buffered(1)),
            pl.BlockSpec((F, D), lambda i: (0, 0), pipeline_mode=pl.Buffered(1)),
        ],
        [
            pl.BlockSpec((tm, D), row),
            pl.BlockSpec((2, tm, F), lambda i: (0, i, 0)),
            pl.BlockSpec((tm, D), row),
            pl.BlockSpec((tm, D), row),
            pl.BlockSpec((1, 1, D), seq),
            pl.BlockSpec((1, 1, D), seq),
            pl.BlockSpec((1, D), lambda i: (0, 0)),
            pl.BlockSpec((1, 1, D), seq),
        ],
        [
            jax.ShapeDtypeStruct((T, D), F32),
            jax.ShapeDtypeStruct((2, T, F), BF16),
            jax.ShapeDtypeStruct((T, D), BF16),
            jax.ShapeDtypeStruct((T, D), BF16),
            jax.ShapeDtypeStruct((B, 1, D), F32),
            jax.ShapeDtypeStruct((B, 1, D), F32),
            jax.ShapeDtypeStruct((1, D), F32),
            jax.ShapeDtypeStruct((B, 1, D), F32),
        ],
        [],
        (dxo, x, gu, f, gain, sh, sc, gate, wgu, wd), comm)


def _wgrad(a, b, tmm, tn, col_major, name, tokens=WGRAD_TOKENS, comm=None):
    T, M = a.shape
    nb, _, Nb = b.shape
    N = nb * Nb
    tk = _tile(tokens, T)
    span = 2 if col_major else 1
    wide = span * tn
    npb = Nb // wide
    assert M % tmm == 0 and Nb % wide == 0
    if col_major:
        assert tmm == M
        shape = (N // tn, 2, M // 2, tn)
        out_spec = pl.BlockSpec((span, 2, M // 2, tn), lambda i, j, t: (j, 0, 0, 0))
    else:
        shape = (M // tmm, tmm, N)
        out_spec = pl.BlockSpec((None, tmm, tn), lambda i, j, t: (i, 0, j))

    def body(a_ref, b_ref, o_ref):
        @pl.when(pl.program_id(2) == 0)
        def _():
            o_ref[...] = jnp.zeros_like(o_ref)

        res = _dot_tn(a_ref[...], b_ref[...])
        if col_major:
            for s in range(span):
                for h in range(2):
                    o_ref[s, h] += res[h * (M // 2):(h + 1) * (M // 2), s * tn:(s + 1) * tn]
        else:
            o_ref[...] += res

    return _call(
        body, name, (M // tmm, N // wide, T // tk),
        [
            pl.BlockSpec((tk, tmm), lambda i, j, t: (t, i)),
            pl.BlockSpec((None, tk, wide), lambda i, j, t: (j // npb, t, j % npb)),
        ],
        [out_spec], [jax.ShapeDtypeStruct(shape, F32)], [],
        (a, b), comm)


def _mixin_bwd(dxo, x, dproj, gain, sc, win, comm=None):
    T, D = x.shape
    P = win.shape[1]
    B = sc.shape[0]
    tm = _tile(TOKEN_TILE, T // B)
    tps = (T // B) // tm

    def body(dxo_ref, x_ref, dp_ref, gain_ref, sc_ref, w_ref, dx_ref, dsc_ref, dsh_ref, dgain_ref):
        i = pl.program_id(0)
        first_of_seq = (i % tps) == 0
        halves = _slabs(tm, tm // 2)
        nxt = _dot_nt(dp_ref[halves[0], :], w_ref[...])
        sums = None
        for j, r in enumerate(halves):
            dh = nxt
            if j + 1 < len(halves):
                nxt = _dot_nt(dp_ref[halves[j + 1], :], w_ref[...])
            part = _norm_mod_bwd(x_ref[r, :], dh, gain_ref[...], sc_ref[0])
            dx_ref[r, :] = dxo_ref[r, :] + part[0]
            sums = part[1:] if sums is None else tuple(a + b for a, b in zip(sums, part[1:]))
        _acc(dsc_ref.at[0], first_of_seq, sums[0])
        _acc(dsh_ref.at[0], first_of_seq, sums[1])
        _acc(dgain_ref, i == 0, sums[2])

    seq = lambda i: (i // tps, 0, 0)
    row = lambda i: (i, 0)
    return _call(
        body, "mixin_bwd", (T // tm,),
        [
            pl.BlockSpec((tm, D), row),
            pl.BlockSpec((tm, D), row),
            pl.BlockSpec((tm, P), row),
            pl.BlockSpec((1, D), lambda i: (0, 0)),
            pl.BlockSpec((1, 1, D), seq),
            pl.BlockSpec((D, P), lambda i: (0, 0)),
        ],
        [
            pl.BlockSpec((tm, D), row),
            pl.BlockSpec((1, 1, D), seq),
            pl.BlockSpec((1, 1, D), seq),
            pl.BlockSpec((1, D), lambda i: (0, 0)),
        ],
        [
            jax.ShapeDtypeStruct((T, D), F32),
            jax.ShapeDtypeStruct((B, 1, D), F32),
            jax.ShapeDtypeStruct((B, 1, D), F32),
            jax.ShapeDtypeStruct((1, D), F32),
        ],
        [],
        (dxo, x, dproj, gain, sc, win), comm)


def _head_mean(z, pmat, exact=True):
    hi = z.astype(BF16)
    if not exact:
        return _dot(hi, pmat)
    lo = (z - hi.astype(F32)).astype(BF16)
    return _dot(hi, pmat) + _dot(lo, pmat)


def _gelu_parts(x):
    cdf = 0.5 * (1.0 + lax.erf(x * (1.0 / math.sqrt(2.0))))
    return x * cdf, cdf


def _gelu_grad(x, cdf):
    return cdf + x * jnp.exp(-0.5 * x * x) * (1.0 / math.sqrt(2.0 * math.pi))


LANES = 128


def _head_blocks(da):
    hd = da // N_HEADS
    lb = min(LANES, da)
    col = lax.broadcasted_iota(jnp.int32, (1, lb), 1)
    return lb, lb // hd, da // lb, [(col >= h * hd) & (col < (h + 1) * hd) for h in range(lb // hd)]


def _mix_heads(w_stack, v, da):
    lb, hpb, nb, masks = _head_blocks(da)
    outs = []
    for b in range(nb):
        res = _dot(w_stack[b * hpb * CHUNK:(b + 1) * hpb * CHUNK], v[:, b * lb:(b + 1) * lb])
        out = res[0:CHUNK]
        for h in range(1, hpb):
            out = jnp.where(masks[h], res[h * CHUNK:(h + 1) * CHUNK], out)
        outs.append(out)
    return outs[0] if nb == 1 else jnp.concatenate(outs, axis=1)


def _mix_heads_grad(dm, v, da):
    lb, hpb, nb, masks = _head_blocks(da)
    outs = []
    for b in range(nb):
        dmb = dm[:, b * lb:(b + 1) * lb]
        stack = jnp.concatenate([jnp.where(masks[h], dmb, jnp.zeros_like(dmb)) for h in range(hpb)], axis=0)
        outs.append(_dot_nt(stack, v[:, b * lb:(b + 1) * lb]))
    return outs[0] if nb == 1 else jnp.concatenate(outs, axis=0)


def _causal_stack(w, transposed):
    r = lax.broadcasted_iota(jnp.int32, w.shape, 0) % CHUNK
    c = lax.broadcasted_iota(jnp.int32, w.shape, 1)
    keep = (c >= r) if transposed else (c <= r)
    return jnp.where(keep, w, 0.0)


def _mix_core_forward(proj, zprev, prm, da, db, saved=None):
    n = proj.shape[0]
    ua = proj[:, 0:da]
    va = proj[:, da:2 * da]
    bg = proj[:, 2 * da:2 * da + db]
    cg = proj[:, 2 * da + db:2 * da + 2 * db]
    xb = proj[:, 2 * da + 2 * db:]
    if saved is None:
        ug, ucdf = _gelu_parts(ua)
        vg, vcdf = _gelu_parts(va)
        zc = vg - _head_mean(vg, prm["pmat"])
        rs = lax.rsqrt(_head_mean(zc * zc, prm["pmat"], exact=False) + EPS)
        vhat = zc * rs
        vln = (vhat * prm["lng"] + prm["lnb"]).astype(BF16)
        wst = _causal_stack(prm["wst"], False).astype(BF16)
        mixed = [_mix_heads(wst, vln[j * CHUNK:(j + 1) * CHUNK], da) + prm["bias"] for j in range(n // CHUNK)]
        mixed = mixed[0] if len(mixed) == 1 else jnp.concatenate(mixed, axis=0)
    else:
        ucdf, vcdf, vhat, rs, mixed = [saved[k].astype(F32) for k in range(5)]
        ug = ua * ucdf
        vln = (vhat * prm["lng"] + prm["lnb"]).astype(BF16)
    ya = ug * mixed
    z = cg * xb
    row = lax.broadcasted_iota(jnp.int32, z.shape, 0)
    z1 = jnp.where(row == 0, zprev[7:8], pltpu.roll(z, 1, 0))
    z2 = jnp.where(row == 0, zprev[6:7], jnp.where(row == 1, zprev[7:8], pltpu.roll(z, 2, 0)))
    cw = prm["convw"]
    conv = z2 * cw[0:1] + z1 * cw[1:2] + z * cw[2:3]
    yb = bg * conv
    yah, ra = _rms(ya)
    ybh, rb = _rms(yb)
    return dict(ua=ua, va=va, bg=bg, cg=cg, xb=xb, ug=ug, ucdf=ucdf, vcdf=vcdf, rs=rs, vhat=vhat, vln=vln,
                mixed=mixed, z=z, z1=z1, z2=z2, conv=conv, yah=yah, ra=ra, ybh=ybh, rb=rb)


def _mix_params(lng_ref, lnb_ref, wst_ref, bias_ref, pmat_ref, convw_ref):
    return dict(lng=lng_ref[...], lnb=lnb_ref[...], wst=wst_ref[...], bias=bias_ref[...],
                pmat=pmat_ref[...], convw=convw_ref[...])


def _mix_fwd(x, gain, sh, sc, gate, win, wout, lng, lnb, wst, bias, pmat, convw, og, comm=None):
    T, D = x.shape
    P = win.shape[1]
    B = gate.shape[0]
    da = lng.shape[1]
    db = convw.shape[1]
    tm = _tile(MIX_TILE, T // B)
    tps = (T // B) // tm

    def body(x_ref, gain_ref, sh_ref, sc_ref, gate_ref, win_ref, wout_ref, lng_ref, lnb_ref, wst_ref, bias_ref,
             pmat_ref, convw_ref, og_ref, xo_ref, proj_ref, h_ref, yn_ref, sv_ref, halo):
        i = pl.program_id(0)

        @pl.when((i % tps) == 0)
        def _():
            halo[...] = jnp.zeros_like(halo)

        h = (_rms(x_ref[...])[0] * gain_ref[...] * (1.0 + sc_ref[0]) + sh_ref[0]).astype(BF16)
        h_ref[...] = h
        proj_ref[...] = _dot(h, win_ref[...])
        prm = _mix_params(lng_ref, lnb_ref, wst_ref, bias_ref, pmat_ref, convw_ref)
        r = _mix_core_forward(proj_ref[...], halo[...], prm, da, db)
        halo[...] = r["z"][tm - 8:tm]
        for k, name in enumerate(("ucdf", "vcdf", "vhat", "rs", "mixed")):
            sv_ref[k] = r[name].astype(BF16)
        og = og_ref[...]
        yn_ref[:, 0:da] = (r["yah"] * og[:, 0:da]).astype(BF16)
        yn_ref[:, da:] = (r["ybh"] * og[:, da:]).astype(BF16)
        xo_ref[...] = x_ref[...] + gate_ref[0] * _dot(yn_ref[...], wout_ref[...])

    full = lambda a: pl.BlockSpec(a.shape, lambda i: (0,) * a.ndim)
    seq = lambda i: (i // tps, 0, 0)
    row = lambda i: (i, 0)
    return _call(
        body, "mix_fwd", (T // tm,),
        [
            pl.BlockSpec((tm, D), row),
            pl.BlockSpec((1, D), lambda i: (0, 0)),
            pl.BlockSpec((1, 1, D), seq),
            pl.BlockSpec((1, 1, D), seq),
            pl.BlockSpec((1, 1, D), seq),
            pl.BlockSpec((D, P), lambda i: (0, 0), pipeline_mode=pl.Buffered(1)),
            full(wout), full(lng), full(lnb), full(wst), full(bias), full(pmat), full(convw), full(og),
        ],
        [pl.BlockSpec((tm, D), row), pl.BlockSpec((tm, P), row), pl.BlockSpec((tm, D), row),
         pl.BlockSpec((tm, D), row), pl.BlockSpec((5, tm, da), lambda i: (0, i, 0))],
        [jax.ShapeDtypeStruct((T, D), F32), jax.ShapeDtypeStruct((T, P), F32), jax.ShapeDtypeStruct((T, D), BF16),
         jax.ShapeDtypeStruct((T, D), BF16), jax.ShapeDtypeStruct((5, T, da), BF16)],
        [pltpu.VMEM((8, db), F32)],
        (x, gain, sh, sc, gate, win, wout, lng, lnb, wst, bias, pmat, convw, og), comm)


def _mix_core_bwd(proj, sv, dxo, gate, wout, lng, lnb, wstt, pmat, convw, og, comm=None):
    T, P = proj.shape
    D = dxo.shape[1]
    B = gate.shape[0]
    da = lng.shape[1]
    db = convw.shape[1]
    assert da == db and P == 2 * da + 3 * db
    tm = _tile(MIX_BWD_TILE, T // B)
    tps = (T // B) // tm
    nt = T // tm
    hd = da // N_HEADS

    def body(proj_ref, cgp_ref, xbp_ref, sv_ref, dxo_ref, gate_ref, wout_ref, lng_ref, lnb_ref, wstt_ref,
             pmat_ref, convw_ref, og_ref,
             dproj_ref, do_ref, dgate_ref, dog_ref, dwst_ref, dbias_ref, dlng_ref, dlnb_ref, dconvw_ref, carry):
        i = pl.program_id(0)
        ri = nt - 1 - i
        first = i == 0
        end_of_seq = (ri % tps) == tps - 1
        start_of_seq = (ri % tps) == 0

        @pl.when(end_of_seq)
        def _():
            carry[...] = jnp.zeros_like(carry)

        prm = dict(lng=lng_ref[...], lnb=lnb_ref[...], pmat=pmat_ref[...], convw=convw_ref[...])
        zprev = jnp.where(start_of_seq, 0.0, cgp_ref[...] * xbp_ref[...])
        r = _mix_core_forward(proj_ref[...], zprev, prm, da, db, saved=sv_ref)
        og = og_ref[...]
        pmat = prm["pmat"]

        yn = jnp.concatenate([(r["yah"] * og[:, 0:da]).astype(BF16), (r["ybh"] * og[:, da:]).astype(BF16)], axis=1)
        dxo = dxo_ref[...]
        o = _dot(yn, wout_ref[...])
        _acc(dgate_ref.at[0], end_of_seq, jnp.sum(dxo * o, axis=0, keepdims=True))
        d_o = (gate_ref[0] * dxo).astype(BF16)
        do_ref[...] = d_o
        dyn = _dot_nt(d_o, wout_ref[...])

        def rms_bwd(dyn_g, yh, rr, og_g):
            dog_g = jnp.sum(dyn_g * yh, axis=0, keepdims=True)
            dyh = dyn_g * og_g
            return rr * (dyh - yh * jnp.mean(dyh * yh, axis=-1, keepdims=True)), dog_g

        dya, dog_a = rms_bwd(dyn[:, 0:da], r["yah"], r["ra"], og[:, 0:da])
        dyb, dog_b = rms_bwd(dyn[:, da:], r["ybh"], r["rb"], og[:, da:])
        _acc(dog_ref, first, jnp.concatenate([dog_a, dog_b], axis=1))

        dug = dya * r["mixed"]
        dmixed = dya * r["ug"]
        wstt_b = _causal_stack(wstt_ref[...], True).astype(BF16)
        dbias = jnp.zeros((CHUNK, da), F32)
        dwst = jnp.zeros((N_HEADS * CHUNK, CHUNK), F32)
        dvln = []
        for j in range(tm // CHUNK):
            dm = dmixed[j * CHUNK:(j + 1) * CHUNK]
            dbias = dbias + dm
            dmb = dm.astype(BF16)
            dwst = dwst + _mix_heads_grad(dmb, r["vln"][j * CHUNK:(j + 1) * CHUNK], da)
            dvln.append(_mix_heads(wstt_b, dmb, da))
        dvln = dvln[0] if len(dvln) == 1 else jnp.concatenate(dvln, axis=0)
        _acc(dbias_ref, first, dbias)
        _acc(dwst_ref, first, dwst)
        _acc(dlng_ref, first, jnp.sum(dvln * r["vhat"], axis=0, keepdims=True))
        _acc(dlnb_ref, first, jnp.sum(dvln, axis=0, keepdims=True))
        dvhat = dvln * prm["lng"]
        dvg = r["rs"] * (dvhat - _head_mean(dvhat, pmat, exact=False)
                         - r["vhat"] * _head_mean(dvhat * r["vhat"], pmat, exact=False))
        dproj_ref[:, 0:da] = (dug * _gelu_grad(r["ua"], r["ucdf"])).astype(BF16)
        dproj_ref[:, da:2 * da] = (dvg * _gelu_grad(r["va"], r["vcdf"])).astype(BF16)

        dproj_ref[:, 2 * da:2 * da + db] = (dyb * r["conv"]).astype(BF16)
        dconv = dyb * r["bg"]
        dcw = jnp.concatenate([
            jnp.sum(dconv * r["z2"], axis=0, keepdims=True),
            jnp.sum(dconv * r["z1"], axis=0, keepdims=True),
            jnp.sum(dconv * r["z"], axis=0, keepdims=True),
            jnp.zeros((5, db), F32)], axis=0)
        _acc(dconvw_ref, first, dcw)
        nxt = carry[...]
        row = lax.broadcasted_iota(jnp.int32, dconv.shape, 0)
        dc1 = jnp.where(row == tm - 1, nxt[0:1], pltpu.roll(dconv, tm - 1, 0))
        dc2 = jnp.where(row == tm - 2, nxt[0:1], jnp.where(row == tm - 1, nxt[1:2], pltpu.roll(dconv, tm - 2, 0)))
        carry[...] = dconv[0:8]
        cw = prm["convw"]
        dz = dconv * cw[2:3] + dc1 * cw[1:2] + dc2 * cw[0:1]
        dproj_ref[:, 2 * da + db:2 * da + 2 * db] = (dz * r["xb"]).astype(BF16)
        dproj_ref[:, 2 * da + 2 * db:] = (dz * r["cg"]).astype(BF16)

        @pl.when(i == nt - 1)
        def _():
            dwst_ref[...] = _causal_stack(dwst_ref[...], False)
            dbias_ref[...] = _head_mean(dbias_ref[...], pmat) * float(hd)

    full = lambda a: pl.BlockSpec(a.shape, lambda i: (0,) * a.ndim)
    const = lambda i: (0, 0)
    rev = lambda i: (nt - 1 - i, 0)
    prev8 = lambda col: (lambda i: (jnp.maximum((nt - 1 - i) * (tm // 8) - 1, 0), col))
    return _call(
        body, "mix_core_bwd", (nt,),
        [
            pl.BlockSpec((tm, P), rev),
            pl.BlockSpec((8, db), prev8((2 * da + db) // db)),
            pl.BlockSpec((8, db), prev8((2 * da + 2 * db) // db)),
            pl.BlockSpec((5, tm, da), lambda i: (0, nt - 1 - i, 0)),
            pl.BlockSpec((tm, D), rev),
            pl.BlockSpec((1, 1, D), lambda i: ((nt - 1 - i) // tps, 0, 0)),
            full(wout), full(lng), full(lnb), full(wstt), full(pmat), full(convw), full(og),
        ],
        [
            pl.BlockSpec((tm, P), rev),
            pl.BlockSpec((tm, D), rev),
            pl.BlockSpec((1, 1, D), lambda i: ((nt - 1 - i) // tps, 0, 0)),
            pl.BlockSpec((1, D), const),
            pl.BlockSpec((N_HEADS * CHUNK, CHUNK), const),
            pl.BlockSpec((CHUNK, da), const),
            pl.BlockSpec((1, da), const),
            pl.BlockSpec((1, da), const),
            pl.BlockSpec((8, db), const),
        ],
        [
            jax.ShapeDtypeStruct((T, P), BF16),
            jax.ShapeDtypeStruct((T, D), BF16),
            jax.ShapeDtypeStruct((B, 1, D), F32),
            jax.ShapeDtypeStruct((1, D), F32),
            jax.ShapeDtypeStruct((N_HEADS * CHUNK, CHUNK), F32),
            jax.ShapeDtypeStruct((CHUNK, da), F32),
            jax.ShapeDtypeStruct((1, da), F32),
            jax.ShapeDtypeStruct((1, da), F32),
            jax.ShapeDtypeStruct((8, db), F32),
        ],
        [pltpu.VMEM((8, db), F32)],
        (proj, proj, proj, sv, dxo, gate, wout, lng, lnb, wstt, pmat, convw, og), comm)


def _ada_fwd(c_all, ada_w, ada_b):
    n, D = c_all.shape
    L, _, sa = ada_w.shape
    tn = _tile(768, sa)

    def body(c_ref, w_ref, b_ref, act_ref, o_ref):
        c = c_ref[...]
        act = (c * _sigmoid(c)).astype(BF16)
        act_ref[...] = act
        o_ref[...] = _dot(act, w_ref[...].astype(BF16)) + b_ref[...]

    return _call(
        body, "ada_fwd", (L, sa // tn),
        [
            pl.BlockSpec((n, D), lambda l, j: (0, 0)),
            pl.BlockSpec((None, D, tn), lambda l, j: (l, 0, j)),
            pl.BlockSpec((None, 1, tn), lambda l, j: (l, 0, j)),
        ],
        [
            pl.BlockSpec((n, D), lambda l, j: (0, 0)),
            pl.BlockSpec((None, n, tn), lambda l, j: (l, 0, j)),
        ],
        [jax.ShapeDtypeStruct((n, D), BF16), jax.ShapeDtypeStruct((L, n, sa), F32)],
        [],
        (c_all, ada_w, ada_b))[0]


def _ada_bwd(c_act, d_ada, comm=None):
    n, D = c_act.shape
    L, _, sa = d_ada.shape
    tn = _tile(768, sa)

    def body(c_ref, d_ref, o_ref):
        o_ref[...] = _dot_tn(c_ref[...], d_ref[...])

    return _call(
        body, "ada_bwd", (L, sa // tn),
        [pl.BlockSpec((n, D), lambda l, j: (0, 0)), pl.BlockSpec((None, n, tn), lambda l, j: (l, 0, j))],
        [pl.BlockSpec((None, D, tn), lambda l, j: (l, 0, j))],
        [jax.ShapeDtypeStruct((L, D, sa), F32)],
        [],
        (c_act, d_ada), comm)


def _colsum(a):
    L, n, C = a.shape

    def body(a_ref, o_ref):
        o_ref[...] = jnp.sum(a_ref[...], axis=0, keepdims=True)

    return _call(
        body, "colsum", (L,),
        [pl.BlockSpec((None, n, C), lambda l: (l, 0, 0))],
        [pl.BlockSpec((None, 1, C), lambda l: (l, 0, 0))],
        [jax.ShapeDtypeStruct((L, 1, C), F32)],
        [],
        (a,))[0][0]


def _row_tile(rows, cols, nbuf):
    budget = VMEM_LIMIT // 2 // (2 * nbuf * 4 * cols)
    t = rows
    while t > max(budget, 8) and t % 2 == 0 and (t // 2) % 8 == 0:
        t //= 2
    return t


def _pair_sum(g, recv, core):
    n, _, R, C = g.shape
    tr = _row_tile(R, C, 3)

    def body(core_ref, g_ref, r_ref, o_ref):
        o_ref[...] = (g_ref[...] + r_ref[...]).astype(BF16)

    return pl.pallas_call(
        body,
        name="pair_sum",
        grid_spec=pltpu.PrefetchScalarGridSpec(
            num_scalar_prefetch=1,
            grid=(n, R // tr),
            in_specs=[
                pl.BlockSpec((None, None, tr, C), lambda i, r, core_ref: (i, core_ref[0], r, 0)),
                pl.BlockSpec((None, tr, C), lambda i, r, core_ref: (i, r, 0)),
            ],
            out_specs=pl.BlockSpec((None, tr, C), lambda i, r, core_ref: (i, r, 0)),
        ),
        out_shape=jax.ShapeDtypeStruct((n, R, C), BF16),
        compiler_params=pltpu.CompilerParams(dimension_semantics=("arbitrary", "arbitrary"),
                                             vmem_limit_bytes=VMEM_LIMIT),
    )(core, g, recv)


def _chip_sum(q, core, l, n_layers, prev):
    nq, R, C = q.shape
    tr = _row_tile(R, C, 4)

    def body(core_ref, q_ref, *rest):
        o_ref = rest[-1]
        s = q_ref[0].astype(F32)
        for j in range(1, nq):
            s = s + q_ref[j].astype(F32)
        o_ref[...] = s

    in_specs = [pl.BlockSpec((nq, tr, C), lambda r, core_ref: (0, r, 0))]
    args = [core, q]
    aliases = {}
    if prev is not None:
        in_specs.append(ANY)
        args.append(prev)
        aliases = {2: 0}
    return pl.pallas_call(
        body,
        name="chip_sum",
        grid_spec=pltpu.PrefetchScalarGridSpec(
            num_scalar_prefetch=1,
            grid=(R // tr,),
            in_specs=in_specs,
            out_specs=pl.BlockSpec((None, None, tr, C), lambda r, core_ref: (l, core_ref[0], r, 0)),
        ),
        out_shape=jax.ShapeDtypeStruct((n_layers, 2, R, C), F32),
        input_output_aliases=aliases,
        compiler_params=pltpu.CompilerParams(dimension_semantics=("arbitrary",), vmem_limit_bytes=VMEM_LIMIT),
    )(*args)


def _sum_blocks(a, n):
    M = a.shape[0] // n
    C = a.shape[1]

    def body(a_ref, o_ref):
        s = a_ref[0:M]
        for j in range(1, n):
            s = s + a_ref[j * M:(j + 1) * M]
        o_ref[...] = s

    return pl.pallas_call(
        body,
        name="sum_blocks",
        out_shape=jax.ShapeDtypeStruct((M, C), F32),
        compiler_params=pltpu.CompilerParams(vmem_limit_bytes=VMEM_LIMIT),
    )(a)


def _adamw(w, g, m, v, emit_grad=False):
    R, C = w.shape
    n_out = 4 if emit_grad else 3
    tr = _row_tile(R, C, 4 + n_out) if R % 8 == 0 else R

    def body(w_ref, g_ref, m_ref, v_ref, d_ref, nm_ref, nv_ref, *g_out):
        g = g_ref[...]
        m = ADAM_B1 * m_ref[...] + (1.0 - ADAM_B1) * g
        v = ADAM_B2 * v_ref[...] + (1.0 - ADAM_B2) * (g * g)
        m_hat = m / (1.0 - ADAM_B1 ** ADAM_STEP)
        v_hat = v / (1.0 - ADAM_B2 ** ADAM_STEP)
        d_ref[...] = -ADAM_LR * (m_hat / (jnp.sqrt(v_hat) + ADAM_EPS) + ADAM_WD * w_ref[...])
        nm_ref[...] = m
        nv_ref[...] = v
        if emit_grad:
            g_out[0][...] = g

    spec = pl.BlockSpec((tr, C), lambda i: (i, 0))
    return _call(body, "adamw", (R // tr,), [spec] * 4, [spec] * n_out, [jax.ShapeDtypeStruct((R, C), F32)] * n_out,
                 [], (w, g, m, v))[0]


def kernel(x, c, ada_w, ada_b, norm_ffn1_g, ffn1_w_gu, ffn1_w_down, norm_mix_g, mix_w_in, sgu_ln_g, sgu_ln_b, sgu_w_s, sgu_b, conv_w, out_norm_g, mix_w_out, norm_ffn2_g, ffn2_w_gu, ffn2_w_down, final_norm_g, loss_target, m_ada_w, m_ada_b, m_norm_ffn1_g, m_ffn1_w_gu, m_ffn1_w_down, m_norm_mix_g, m_mix_w_in, m_sgu_ln_g, m_sgu_ln_b, m_sgu_w_s, m_sgu_b, m_conv_w, m_out_norm_g, m_mix_w_out, m_norm_ffn2_g, m_ffn2_w_gu, m_ffn2_w_down, m_final_norm_g, v_ada_w, v_ada_b, v_norm_ffn1_g, v_ffn1_w_gu, v_ffn1_w_down, v_norm_mix_g, v_mix_w_in, v_sgu_ln_g, v_sgu_ln_b, v_sgu_w_s, v_sgu_b, v_conv_w, v_out_norm_g, v_mix_w_out, v_norm_ffn2_g, v_ffn2_w_gu, v_ffn2_w_down, v_final_norm_g):
    weights = dict(ada_w=ada_w, ada_b=ada_b, norm_ffn1_g=norm_ffn1_g, ffn1_w_gu=ffn1_w_gu, ffn1_w_down=ffn1_w_down,
                   norm_mix_g=norm_mix_g, mix_w_in=mix_w_in, sgu_ln_g=sgu_ln_g, sgu_ln_b=sgu_ln_b, sgu_w_s=sgu_w_s,
                   sgu_b=sgu_b, conv_w=conv_w, out_norm_g=out_norm_g, mix_w_out=mix_w_out, norm_ffn2_g=norm_ffn2_g,
                   ffn2_w_gu=ffn2_w_gu, ffn2_w_down=ffn2_w_down, final_norm_g=final_norm_g)
    m_in = dict(ada_w=m_ada_w, ada_b=m_ada_b, norm_ffn1_g=m_norm_ffn1_g, ffn1_w_gu=m_ffn1_w_gu,
                ffn1_w_down=m_ffn1_w_down, norm_mix_g=m_norm_mix_g, mix_w_in=m_mix_w_in, sgu_ln_g=m_sgu_ln_g,
                sgu_ln_b=m_sgu_ln_b, sgu_w_s=m_sgu_w_s, sgu_b=m_sgu_b, conv_w=m_conv_w, out_norm_g=m_out_norm_g,
                mix_w_out=m_mix_w_out, norm_ffn2_g=m_norm_ffn2_g, ffn2_w_gu=m_ffn2_w_gu, ffn2_w_down=m_ffn2_w_down,
                final_norm_g=m_final_norm_g)
    v_in = dict(ada_w=v_ada_w, ada_b=v_ada_b, norm_ffn1_g=v_norm_ffn1_g, ffn1_w_gu=v_ffn1_w_gu,
                ffn1_w_down=v_ffn1_w_down, norm_mix_g=v_norm_mix_g, mix_w_in=v_mix_w_in, sgu_ln_g=v_sgu_ln_g,
                sgu_ln_b=v_sgu_ln_b, sgu_w_s=v_sgu_w_s, sgu_b=v_sgu_b, conv_w=v_conv_w, out_norm_g=v_out_norm_g,
                mix_w_out=v_mix_w_out, norm_ffn2_g=v_norm_ffn2_g, ffn2_w_gu=v_ffn2_w_gu, ffn2_w_down=v_ffn2_w_down,
                final_norm_g=v_final_norm_g)

    B, S, D = x.shape
    T = B * S
    L = ada_w.shape[0]
    F = ffn1_w_down.shape[1] * N_CHIP
    P = mix_w_in.shape[2] * N_CHIP
    DA = D // 2
    DB = D - DA
    HD = DA // N_HEADS
    SA = ada_w.shape[2]
    n_all = B * N_DEV
    mx, my, mc = _position()
    chip = 2 * mx + my
    dev = 2 * chip + mc
    core = jnp.reshape(mc, (1,)).astype(jnp.int32)

    big = ["ffn1_w_gu", "ffn1_w_down", "mix_w_in", "mix_w_out", "ffn2_w_gu", "ffn2_w_down"]
    col_sharded = dict(ffn1_w_gu=True, ffn1_w_down=False, mix_w_in=True, mix_w_out=False,
                       ffn2_w_gu=True, ffn2_w_down=False)
    shards = {k: weights[k].astype(BF16) for k in big}
    gather = lambda l, *names: _gather_comm([(shards[k], l, col_sharded[k]) for k in names])
    full = [dict() for _ in range(L)]

    def arrived(l, names, res):
        full[l].update(zip(names, res))

    n_cw = L * conv_w.shape[1]
    cw_block = jnp.pad(conv_w.reshape(n_cw, conv_w.shape[2]), ((0, 8 - n_cw), (0, 0)))
    c_all, cw_all = _comm_call(_merge(_all_gather_comm(c.reshape(8, B * D // 8)), _all_gather_comm(cw_block)),
                               "gather_c")
    c_all = c_all.reshape(n_all, D)
    cw_all = cw_all.reshape(N_CHIP, 2, 8, conv_w.shape[2])[:, 0, :n_cw]
    conv_full = jnp.transpose(cw_all.reshape(N_CHIP, L, conv_w.shape[1], conv_w.shape[2]), (1, 2, 0, 3))
    conv_full = conv_full.reshape(L, conv_w.shape[1], DB)
    ada_b_mine = lax.dynamic_slice_in_dim(ada_b, chip * SA, SA, axis=1).reshape(L, 1, SA)
    c_act, ada_part = _ada_fwd(c_all, ada_w, ada_b_mine)
    ada_all, first_w = _comm_call(_merge(_all_gather_comm(ada_part.reshape(L * n_all, SA)), gather(0, big[0])),
                                  "gather_first")
    arrived(0, big[:1], [first_w])
    ada_all = ada_all.reshape(N_CHIP, 2, L, n_all, SA)[:, 0]
    ada_all = jnp.transpose(ada_all, (1, 2, 0, 3)).reshape(L, n_all, N_CHIP * SA)
    ada = lax.dynamic_slice_in_dim(ada_all, dev * B, B, axis=1).reshape(L, B, N_MOD, 1, D)
    mods = [[ada[l, :, j] for j in range(N_MOD)] for l in range(L)]

    x0 = x.reshape(T, D)
    gains = lambda name, l: weights[name][l].reshape(1, D)
    hmask = jnp.repeat(jnp.eye(N_HEADS, dtype=F32), HD, axis=0)
    pmat = (jnp.repeat(hmask, HD, axis=1) / HD).astype(BF16)

    def mix_consts(l):
        lng = jnp.tile(sgu_ln_g[l], N_HEADS).reshape(1, DA)
        lnb = jnp.tile(sgu_ln_b[l], N_HEADS).reshape(1, DA)
        wst = sgu_w_s[l].reshape(N_HEADS * CHUNK, CHUNK)
        wstt = jnp.swapaxes(sgu_w_s[l], 1, 2).reshape(N_HEADS * CHUNK, CHUNK)
        bias = jnp.repeat(jnp.transpose(sgu_b[l]), HD, axis=1)
        return lng, lnb, wst, wstt, bias

    def fetch(fn, *args, bring=(), **kw):
        bring = [(l, k) for l, k in bring if l < L]
        comm = _gather_comm([(shards[k], l, col_sharded[k]) for l, k in bring]) if bring else None
        res, got = fn(*args, comm, **kw)
        for (l, k), a in zip(bring, got):
            full[l][k] = a
        return res

    saved = []
    xc = x0
    for l in range(L):
        sh1, sc1, g1, sh2, sc2, g2, sh3, sc3, g3 = mods[l]
        lng, lnb, wst, wstt, bias = mix_consts(l)
        w = full[l]
        if l == 0:
            gu1, a1 = fetch(_ffn_up, xc, gains("norm_ffn1_g", l), sh1, sc1, w["ffn1_w_gu"],
                            bring=[(l, "ffn1_w_down"), (l, "mix_w_in"), (l, "mix_w_out")])
            xa, f1 = fetch(_ffn_down, a1, xc, g1, w["ffn1_w_down"], bring=[(l, "ffn2_w_down")])
        else:
            xa, gu1, a1, f1 = fetch(_ffn_fwd, xc, gains("norm_ffn1_g", l), sh1, sc1, g1, w["ffn1_w_gu"],
                                    w["ffn1_w_down"], bring=[(l, "ffn2_w_gu"), (l, "mix_w_in")])
        xb, proj, h2, yn, sv = fetch(_mix_fwd, xa, gains("norm_mix_g", l), sh2, sc2, g2, w["mix_w_in"], w["mix_w_out"],
                                     lng, lnb, wst, bias, pmat, conv_full[l], gains("out_norm_g", l),
                                     bring=[(l, "ffn2_w_gu")] if l == 0 else [(l, "ffn2_w_down")])
        if l + 1 < L:
            xd, gu2, a2, f2 = fetch(_ffn_fwd, xb, gains("norm_ffn2_g", l), sh3, sc3, g3, w["ffn2_w_gu"],
                                    w["ffn2_w_down"],
                                    bring=[(l + 1, "ffn1_w_gu"), (l + 1, "mix_w_out"), (l + 1, "ffn1_w_down")])
        else:
            dx, gu2, a2, f2, loss_block, d_final = fetch(
                _ffn_fwd, xb, gains("norm_ffn2_g", l), sh3, sc3, g3, w["ffn2_w_gu"], w["ffn2_w_down"],
                head=(loss_target.reshape(T, D), final_norm_g.reshape(1, D)))
            xd = None
        saved.append(dict(x0=xc, xa=xa, xb=xb, gu1=gu1, a1=a1, f1=f1, proj=proj, h2=h2, yn=yn, sv=sv,
                          gu2=gu2, a2=a2, f2=f2))
        xc = xd

    reduced = dict.fromkeys(big)

    def halves(name, g):
        if g.ndim == 4:
            return g
        return g.reshape(N_CHIP, 2, weights[name].shape[1] // 2, g.shape[-1])

    class Reduction:
        def __init__(self, l, name, g):
            self.l, self.name, self.g, self.stage = l, name, halves(name, g), 0
            self.ici_bytes = 3 * (g.size // 8) * 2

        def step(self):
            self.stage += 1
            if self.stage == 1:
                return _sibling_half_comm([self.g])
            if self.stage == 2:
                return _scatter_comm([_pair_sum(self.g, self.got[0], core)])
            if self.stage == 3:
                reduced[self.name] = _chip_sum(self.got[0], core, self.l, L, reduced[self.name])
                return _share_comm([reduced[self.name]], self.l)
            reduced[self.name] = self.got[0]
            return None

    active, extra, gathered = [], [], {}

    def carry(fn, *args, us=None):
        left = None if us is None else us * SCATTER_BYTES_PER_US
        riders = []
        for r in active:
            if r.stage == 1 and left is not None:
                if r.ici_bytes > left * SCATTER_OVERSHOOT:
                    continue
                left -= r.ici_bytes
            riders.append(r)
        comms = [r.step() for r in riders] + [cm for cm, _ in extra]
        takers = [functools.partial(setattr, r, "got") for r in riders] + [cb for _, cb in extra]
        extra.clear()
        if fn is None:
            res, got = None, (_comm_call(_merge(*comms), "reduce_alone") if comms else [])
        else:
            res, got = fn(*args, comm=_merge(*comms))
        at = 0
        for cm, take in zip(comms, takers):
            take(got[at:at + len(cm.out_shape)])
            at += len(cm.out_shape)
        for r in riders:
            if r.stage == 3:
                r.step()
                active.remove(r)
        return res

    def reduce_later(l, name, g):
        active.append(Reduction(l, name, g))

    small = [None] * L
    dwsts = [None] * L
    d_ada = [None] * L
    for l in reversed(range(L)):
        sh1, sc1, g1, sh2, sc2, g2, sh3, sc3, g3 = mods[l]
        lng, lnb, wst, wstt, bias = mix_consts(l)
        s = saved[l]
        w = full[l]
        last = l == 0
        dx, dgu2, h3, df2, dsc3, dsh3, dgain3, dg3 = carry(
            _ffn_bwd, dx, s["xb"], s["gu2"], s["f2"], gains("norm_ffn2_g", l), sh3, sc3, g3, w["ffn2_w_gu"],
            w["ffn2_w_down"], us=170)
        ffn2_grads = [
            lambda: reduce_later(l, "ffn2_w_gu", carry(_wgrad, h3, dgu2, D, 2 * F // N_CHIP, True, "wgrad_gu",
                                                       WGRAD_TOKENS // 2, us=110)[0]),
            lambda: reduce_later(l, "ffn2_w_down", carry(_wgrad, s["a2"], df2[None], F // 2, D, False, "wgrad_down",
                                                         us=55)[0])]
        ffn2_grads[0]()
        if not last:
            ffn2_grads[1]()
        dproj, d_o, dg2, dog, dwst, dbias, dlng, dlnb, dconvw = carry(
            _mix_core_bwd, s["proj"], s["sv"], dx, g2, w["mix_w_out"], lng, lnb, wstt, pmat, conv_full[l],
            gains("out_norm_g", l), us=150)
        mix_grads = [
            lambda: reduce_later(l, "mix_w_out", carry(_wgrad, s["yn"], d_o[None], D, D, False, "wgrad_out", us=30)[0]),
            lambda: reduce_later(l, "mix_w_in", carry(_wgrad, s["h2"], dproj[None], D, P // N_CHIP, True, "wgrad_in",
                                                      us=65)[0])]
        if not last:
            mix_grads[0]()
        dx, dsc2, dsh2, dgain2 = carry(_mixin_bwd, dx, s["xa"], dproj, gains("norm_mix_g", l), sc2, w["mix_w_in"], us=60)
        if not last:
            mix_grads[1]()
        dx, dgu, h1, df, dsc1, dsh1, dgain1, dg1 = carry(
            _ffn_bwd, dx, s["x0"], s["gu1"], s["f1"], gains("norm_ffn1_g", l), sh1, sc1, g1, w["ffn1_w_gu"],
            w["ffn1_w_down"], us=170)
        d_ada[l] = jnp.concatenate([dsh1, dsc1, dg1, dsh2, dsc2, dg2, dsh3, dsc3, dg3], axis=1).reshape(B, N_MOD * D)
        small[l] = [dgain1, dgain2, dgain3, dog, dlng, dlnb, dbias[:, ::HD], dconvw]
        dwsts[l] = dwst
        if last:
            flat = [a.reshape(-1, 128) for ll in range(L) for a in small[ll]]
            flat += [d_final.reshape(-1, 128), loss_block[0:1]]
            pad = (-sum(a.shape[0] for a in flat)) % 8
            packed = jnp.concatenate(flat + [jnp.zeros((pad, 128), F32)], axis=0)
            extra.append((_all_gather_comm(jnp.stack(d_ada).reshape(L * B, N_MOD * D)),
                          lambda got: gathered.update(d_ada=got[0])))
            extra.append((_all_gather_comm(packed), lambda got: gathered.update(small=got[0])))
            for ll in range(L):
                extra.append((_all_gather_comm(dwsts[ll]), lambda got, ll=ll: gathered.update({("dwst", ll): got[0]})))
        reduce_later(l, "ffn1_w_gu", carry(_wgrad, h1, dgu, D, 2 * F // N_CHIP, True, "wgrad_gu", WGRAD_TOKENS // 2,
                                           us=110)[0])
        reduce_later(l, "ffn1_w_down", carry(_wgrad, s["a1"], df[None], F // 2, D, False, "wgrad_down", us=50)[0])
        if last:
            ffn2_grads[1]()
            mix_grads[1]()
            mix_grads[0]()
    grad_x = dx.reshape(B, S, D)

    def finished(name):
        while any(r.name == name for r in active):
            carry(None)
        return reduced[name].reshape(weights[name].shape)

    grads = {}
    d_ada_all = jnp.transpose(gathered["d_ada"].reshape(N_DEV, L, B, N_MOD * D), (1, 0, 2, 3))
    d_ada_all = d_ada_all.reshape(L, n_all, N_MOD * D)
    grads["ada_b"] = _colsum(d_ada_all).reshape(L, N_MOD * D)
    d_ada_mine = lax.dynamic_slice_in_dim(d_ada_all, chip * SA, SA, axis=2).astype(BF16)
    grads["ada_w"] = _ada_bwd(c_act, d_ada_mine)[0][0]

    total = _sum_blocks(gathered["small"].reshape(-1, 128), N_DEV)
    pieces, at = [], 0
    for a in flat:
        pieces.append(total[at:at + a.shape[0]])
        at += a.shape[0]
    per_layer = len(small[0])
    stack = lambda j, shape: jnp.stack([pieces[l * per_layer + j].reshape(shape) for l in range(L)])
    grads["norm_ffn1_g"] = stack(0, (D,))
    grads["norm_mix_g"] = stack(1, (D,))
    grads["norm_ffn2_g"] = stack(2, (D,))
    grads["out_norm_g"] = stack(3, (D,))
    grads["sgu_ln_g"] = stack(4, (N_HEADS, HD)).sum(axis=1)
    grads["sgu_ln_b"] = stack(5, (N_HEADS, HD)).sum(axis=1)
    grads["sgu_b"] = jnp.swapaxes(stack(6, (CHUNK, N_HEADS)), 1, 2)
    g_conv = stack(7, (8, DB))[:, :conv_w.shape[1]]
    grads["conv_w"] = lax.dynamic_slice_in_dim(g_conv, chip * conv_w.shape[2], conv_w.shape[2], axis=2)
    grads["final_norm_g"] = pieces[-2].reshape(D)
    loss = pieces[-1][0, 0]
    grads["sgu_w_s"] = jnp.stack([_sum_blocks(gathered["dwst", l].reshape(-1, CHUNK), N_DEV) for l in range(L)])
    grads["sgu_w_s"] = grads["sgu_w_s"].reshape(L, N_HEADS, CHUNK, CHUNK)

    names = list(weights)
    delta, new_m, new_v = {}, {}, {}
    for k in big:
        grads[k] = finished(k)
    for k in names:
        wk = weights[k]
        view = (1, wk.shape[0]) if wk.ndim == 1 else (-1, wk.shape[-1])
        d, nm, nv, *g_again = _adamw(wk.reshape(view), grads[k].reshape(view), m_in[k].reshape(view),
                                     v_in[k].reshape(view), emit_grad=k in big)
        delta[k], new_m[k], new_v[k] = d.reshape(wk.shape), nm.reshape(wk.shape), nv.reshape(wk.shape)
        if g_again:
            grads[k] = g_again[0].reshape(wk.shape)

    return (loss, grad_x, *[grads[k] for k in names], *[delta[k] for k in names],
            *[new_m[k] for k in names], *[new_v[k] for k in names])
```
